```python
import math
import jax, jax.numpy as jnp
from jax import lax
import numpy as np

D_MODEL = 2048
BATCH = 4
SEQ = 4096
DEPTH = 2

HEAD_DIM = 128
ROPE_THETA = 10000.0
NORM_EPS = 1e-6
Q_BLOCK = 128
NEG_INF = -1e30
PLE_DIM = 256
MAX_POS_OFFSET = 1024
N_EVEN = (DEPTH + 1) // 2
N_ODD = DEPTH // 2
A_HEADS = 8
A_KV_HEADS = 2
A_WINDOW = 128
B_HEADS = 8
B_Q_LORA = 512
B_KV_LORA = 256
B_NOPE = 128
B_ROPE = 64
B_V = 128
C_HEADS = 8
D_HEADS = 8
D_KV_HEADS = 2
D_CMP_LEN = 32
D_CMP_STRIDE = 16
D_CMP_HIDDEN = 256
D_SLC_LEN = 64
D_SLC_TOPN = 8
D_WINDOW = 512
FORCE_BONUS = 1e4
N_GROUPS = 4
EXPERTS_PER_GROUP = 8
N_EXPERTS = N_GROUPS * EXPERTS_PER_GROUP
EXPERT_HIDDEN = 512
TOP_K = 2
MOE_BLOCK = 128

EVEN_SPLITS = [A_HEADS * HEAD_DIM, A_KV_HEADS * HEAD_DIM, A_KV_HEADS * HEAD_DIM, B_Q_LORA, B_KV_LORA, B_ROPE]
EVEN_IN = sum(EVEN_SPLITS)
EVEN_OUT = A_HEADS * HEAD_DIM + B_HEADS * B_V
ODD_SPLITS = [C_HEADS * HEAD_DIM] * 3 + [C_HEADS] + [D_HEADS * HEAD_DIM] + [D_KV_HEADS * HEAD_DIM] * 6 + [3 * D_HEADS]
ODD_IN = sum(ODD_SPLITS)
ODD_OUT = C_HEADS * HEAD_DIM + D_HEADS * HEAD_DIM

kernel_name = "hybrid_swa_mla_fox_nsa_hmoe"


def rms_norm(x, g):
    xf = x.astype(jnp.float32)
    y = xf * lax.rsqrt(jnp.mean(xf * xf, axis=-1, keepdims=True) + NORM_EPS)
    return (y * g.astype(jnp.float32)).astype(x.dtype)


def split_cols(y, sizes):
    return jnp.split(y, np.cumsum(sizes)[:-1].tolist(), axis=-1)


def rope_tables(positions, dim):
    inv = 1.0 / (ROPE_THETA ** (jnp.arange(0, dim, 2, dtype=jnp.float32) / dim))
    ang = positions.astype(jnp.float32)[..., None] * inv
    return jnp.cos(ang), jnp.sin(ang)


def apply_rope(x, cos, sin):
    x1, x2 = jnp.split(x.astype(jnp.float32), 2, axis=-1)
    c = cos[:, :, None, :]
    s = sin[:, :, None, :]
    return jnp.concatenate([x1 * c - x2 * s, x2 * c + x1 * s], axis=-1).astype(x.dtype)


def masked_softmax(s, mask):
    s = jnp.where(mask, s, NEG_INF)
    m = jnp.max(s, axis=-1, keepdims=True)
    e = jnp.where(mask, jnp.exp(s - m), 0.0)
    return e / jnp.maximum(jnp.sum(e, axis=-1, keepdims=True), jnp.finfo(jnp.float32).tiny)


def banded_attention(q, k, v, window, sinks=None):
    B, S, H, d = q.shape
    Hk = k.shape[2]
    G = H // Hk
    nb = S // Q_BLOCK
    n_prev = -(-window // Q_BLOCK)
    L = (n_prev + 1) * Q_BLOCK

    def with_history(z):
        zp = jnp.pad(z, ((0, 0), (n_prev * Q_BLOCK, 0), (0, 0), (0, 0)))
        zp = zp.reshape(B, nb + n_prev, Q_BLOCK, Hk, z.shape[-1])
        return jnp.concatenate([zp[:, i:i + nb] for i in range(n_prev + 1)], axis=2)

    kb = with_history(k)
    vb = with_history(v)
    qb = q.reshape(B, nb, Q_BLOCK, Hk, G, d)
    s = jnp.einsum('bnqhgd,bnkhd->bnhgqk', qb, kb).astype(jnp.float32) * (d ** -0.5)
    rel = (jnp.arange(Q_BLOCK)[:, None] + n_prev * Q_BLOCK) - jnp.arange(L)[None, :]
    abs_k = jnp.arange(nb)[:, None] * Q_BLOCK - n_prev * Q_BLOCK + jnp.arange(L)[None, :]
    mask = ((rel >= 0) & (rel < window))[None] & (abs_k >= 0)[:, None, :]
    s = jnp.where(mask[None, :, None, None], s, NEG_INF)
    if sinks is not None:
        sk = jnp.broadcast_to(sinks.astype(jnp.float32).reshape(Hk, G)[None, None, :, :, None, None],
                              s.shape[:-1] + (1,))
        p = jax.nn.softmax(jnp.concatenate([s, sk], axis=-1), axis=-1)[..., :-1]
    else:
        p = jax.nn.softmax(s, axis=-1)
    o = jnp.einsum('bnhgqk,bnkhd->bnqhgd', p.astype(v.dtype), vb)
    return o.reshape(B, S, H, v.shape[-1])


def causal_attention(q, k, v, cum=None):
    B, S, H, dk = q.shape
    nb = S // Q_BLOCK
    scale = dk ** -0.5
    kpos = jnp.arange(S)
    qb = jnp.moveaxis(q.reshape(B, nb, Q_BLOCK, H, dk), 1, 0)

    def attend(i, q_i, cq_i):
        s = jnp.einsum('bqhd,bkhd->bhqk', q_i, k).astype(jnp.float32) * scale
        if cq_i is not None:
            s = s + (jnp.swapaxes(cq_i, 1, 2)[..., None] - jnp.swapaxes(cum, 1, 2)[:, :, None, :])
        qpos = i * Q_BLOCK + jnp.arange(Q_BLOCK)
        s = jnp.where(qpos[:, None] >= kpos[None, :], s, NEG_INF)
        p = jax.nn.softmax(s, axis=-1).astype(v.dtype)
        return jnp.einsum('bhqk,bkhd->bqhd', p, v)

    idx = jnp.arange(nb)
    if cum is None:
        out = lax.map(lambda a: attend(a[0], a[1], None), (idx, qb))
    else:
        cb = jnp.moveaxis(cum.reshape(B, nb, Q_BLOCK, H), 1, 0)
        out = lax.map(lambda a: attend(a[0], a[1], a[2]), (idx, qb, cb))
    return jnp.moveaxis(out, 0, 1).reshape(B, S, H, v.shape[-1])


def mixer_ab(h, cos, sin, cos_r, sin_r, w_in, sinks, q_lat_norm, kv_lat_norm, w_uq, w_ukv, w_o):
    B, S, _ = h.shape
    a_q, a_k, a_v, b_qlat, b_kvlat, b_krope = split_cols(h @ w_in, EVEN_SPLITS)
    a_q = apply_rope(a_q.reshape(B, S, A_HEADS, HEAD_DIM), cos, sin)
    a_k = apply_rope(a_k.reshape(B, S, A_KV_HEADS, HEAD_DIM), cos, sin)
    a_v = a_v.reshape(B, S, A_KV_HEADS, HEAD_DIM)
    o_a = banded_attention(a_q, a_k, a_v, A_WINDOW, sinks)
    qb = (rms_norm(b_qlat, q_lat_norm) @ w_uq).reshape(B, S, B_HEADS, B_NOPE + B_ROPE)
    q_nope, q_rope = jnp.split(qb, [B_NOPE], axis=-1)
    q_rope = apply_rope(q_rope, cos_r, sin_r)
    kv = (rms_norm(b_kvlat, kv_lat_norm) @ w_ukv).reshape(B, S, B_HEADS, B_NOPE + B_V)
    k_nope, v_b = jnp.split(kv, [B_NOPE], axis=-1)
    k_rope = apply_rope(b_krope[:, :, None, :], cos_r, sin_r)
    q_full = jnp.concatenate([q_nope, q_rope], axis=-1)
    k_full = jnp.concatenate([k_nope, jnp.broadcast_to(k_rope, (B, S, B_HEADS, B_ROPE))], axis=-1)
    o_b = causal_attention(q_full, k_full, v_b)
    o = jnp.concatenate([o_a.reshape(B, S, -1), o_b.reshape(B, S, -1)], axis=-1)
    return o @ w_o


def cmp_block_index(S):
    n_cmp = (S - D_CMP_LEN) // D_CMP_STRIDE + 1
    return np.arange(n_cmp)[:, None] * D_CMP_STRIDE + np.arange(D_CMP_LEN)[None, :]


def cmp_slc_overlap(S):
    n_cmp = (S - D_CMP_LEN) // D_CMP_STRIDE + 1
    n_slc = S // D_SLC_LEN
    c0 = np.arange(n_cmp) * D_CMP_STRIDE
    s0 = np.arange(n_slc) * D_SLC_LEN
    ov = (c0[:, None] < (s0 + D_SLC_LEN)[None, :]) & ((c0 + D_CMP_LEN)[:, None] > s0[None, :])
    return ov.astype(np.float32)


def compress_blocks(z, pe, w1, w2):
    B, S, Hk, d = z.shape
    idx = cmp_block_index(S)
    blk = z[:, idx] + pe[:, None, :].astype(z.dtype)
    blk = jnp.swapaxes(blk, 2, 3).reshape(B, idx.shape[0], Hk, D_CMP_LEN * d)
    return jax.nn.gelu(blk @ w1) @ w2


def nsa_compressed(q, k, v, positions, pe_k, w1_k, w2_k, pe_v, w1_v, w2_v):
    B, S, H, d = q.shape
    Hk = k.shape[2]
    G = H // Hk
    end = cmp_block_index(S)[:, -1]
    k_cmp = compress_blocks(k, pe_k, w1_k, w2_k)
    v_cmp = compress_blocks(v, pe_v, w1_v, w2_v)
    cos_c, sin_c = rope_tables(positions[:, end], HEAD_DIM)
    k_cmp = apply_rope(k_cmp, cos_c, sin_c)
    s = jnp.einsum('bthgd,bchd->bhgtc', q.reshape(B, S, Hk, G, d), k_cmp).astype(jnp.float32) * (d ** -0.5)
    valid = end[None, :] <= np.arange(S)[:, None]
    p = masked_softmax(s, valid)
    o = jnp.einsum('bhgtc,bchd->bthgd', p.astype(v_cmp.dtype), v_cmp).reshape(B, S, H, d)
    return o, p


def nsa_select_blocks(p_cmp, S):
    n_slc = S // D_SLC_LEN
    imp = jnp.einsum('bhgtc,cj->bhtj', p_cmp, cmp_slc_overlap(S))
    t = jnp.arange(S)[:, None]
    j = jnp.arange(n_slc)[None, :]
    cur = t // D_SLC_LEN
    forced = (j == 0) | (j == cur) | (j == cur - 1)
    imp = jnp.where(j * D_SLC_LEN > t, NEG_INF, imp + jnp.where(forced, FORCE_BONUS, 0.0))
    return lax.top_k(imp, min(D_SLC_TOPN, n_slc))[1]


def nsa_selected(q, k, v, sel):
    B, S, H, d = q.shape
    Hk = k.shape[2]
    G = H // Hk
    nb = S // Q_BLOCK
    n_slc = S // D_SLC_LEN
    topn = sel.shape[-1]
    kb = jnp.transpose(k.reshape(B, n_slc, D_SLC_LEN, Hk, d), (0, 3, 1, 2, 4))
    vb = jnp.transpose(v.reshape(B, n_slc, D_SLC_LEN, Hk, d), (0, 3, 1, 2, 4))
    qb = jnp.moveaxis(q.reshape(B, nb, Q_BLOCK, Hk, G, d), 1, 0)
    sb = jnp.moveaxis(sel.reshape(B, Hk, nb, Q_BLOCK, topn), 2, 0)
    bi = jnp.arange(B)[:, None, None, None]
    hi = jnp.arange(Hk)[None, :, None, None]

    def one_block(args):
        i, q_i, sel_i = args
        kg = kb[bi, hi, sel_i]
        vg = vb[bi, hi, sel_i]
        s = jnp.einsum('bqhgd,bhqnld->bhgqnl', q_i, kg).astype(jnp.float32) * (d ** -0.5)
        tpos = i * Q_BLOCK + jnp.arange(Q_BLOCK)
        kpos = sel_i[..., None] * D_SLC_LEN + jnp.arange(D_SLC_LEN)
        mask = (kpos <= tpos[None, None, :, None, None])[:, :, None]
        s = jnp.where(mask, s, NEG_INF).reshape(B, Hk, G, Q_BLOCK, topn * D_SLC_LEN)
        p = jax.nn.softmax(s, axis=-1).reshape(B, Hk, G, Q_BLOCK, topn, D_SLC_LEN)
        return jnp.einsum('bhgqnl,bhqnld->bqhgd', p.astype(vg.dtype), vg)

    out = lax.map(one_block, (jnp.arange(nb), qb, sb))
    return jnp.moveaxis(out, 0, 1).reshape(B, S, H, d)


def mixer_cd(h, positions, cos, sin, w_in, forget_bias, pe_k, w1_k, w2_k, pe_v, w1_v, w2_v, w_o):
    B, S, _ = h.shape
    (c_q, c_k, c_v, c_f, d_q, d_kc, d_vc, d_ks, d_vs, d_kw, d_vw, d_g) = split_cols(h @ w_in, ODD_SPLITS)
    log_f = jax.nn.log_sigmoid(c_f.astype(jnp.float32) + forget_bias.astype(jnp.float32))
    cum = jnp.cumsum(log_f, axis=1)
    o_c = causal_attention(c_q.reshape(B, S, C_HEADS, HEAD_DIM), c_k.reshape(B, S, C_HEADS, HEAD_DIM),
                           c_v.reshape(B, S, C_HEADS, HEAD_DIM), cum)
    kvh = lambda z: z.reshape(B, S, D_KV_HEADS, HEAD_DIM)
    q = apply_rope(d_q.reshape(B, S, D_HEADS, HEAD_DIM), cos, sin)
    o_cmp, p_cmp = nsa_compressed(q, kvh(d_kc), kvh(d_vc), positions, pe_k, w1_k, w2_k, pe_v, w1_v, w2_v)
    sel = nsa_select_blocks(p_cmp, S)
    o_slc = nsa_selected(q, apply_rope(kvh(d_ks), cos, sin), kvh(d_vs), sel)
    o_win = banded_attention(q, apply_rope(kvh(d_kw), cos, sin), kvh(d_vw), D_WINDOW)
    g = jax.nn.sigmoid(d_g.reshape(B, S, 3, D_HEADS, 1))
    o_d = g[:, :, 0] * o_cmp + g[:, :, 1] * o_slc + g[:, :, 2] * o_win
    o = jnp.concatenate([o_c.reshape(B, S, -1), o_d.reshape(B, S, -1)], axis=-1)
    return o @ w_o


def hierarchical_moe(h, w_group, b_group, w_expert, b_expert, w_gate, w_up, w_down):
    B, S, D = h.shape
    x = h.reshape(-1, D)
    T = x.shape[0]
    g_prob = jax.nn.softmax((x @ w_group).astype(jnp.float32) + b_group.astype(jnp.float32), axis=-1)
    g_val, g_idx = lax.top_k(g_prob, 1)
    e_logits = ((x @ w_expert).astype(jnp.float32) + b_expert.astype(jnp.float32)).reshape(T, N_GROUPS, EXPERTS_PER_GROUP)
    e_logits = jnp.take_along_axis(e_logits, g_idx[:, :, None], axis=1)[:, 0]
    e_val, e_idx = lax.top_k(jax.nn.softmax(e_logits, axis=-1), TOP_K)
    weights = g_val * e_val / jnp.sum(e_val, axis=-1, keepdims=True)
    experts = g_idx * EXPERTS_PER_GROUP + e_idx

    N = T * TOP_K
    flat_e = experts.reshape(N)
    flat_tok = jnp.repeat(jnp.arange(T, dtype=jnp.int32), TOP_K)
    flat_w = weights.reshape(N)
    order = jnp.argsort(flat_e)
    se = flat_e[order]
    counts = jnp.bincount(flat_e, length=N_EXPERTS)
    starts = jnp.cumsum(counts) - counts
    padded = (counts + MOE_BLOCK - 1) // MOE_BLOCK * MOE_BLOCK
    pends = jnp.cumsum(padded)
    dest = (pends - padded)[se] + jnp.arange(N) - starts[se]
    P = N + N_EXPERTS * MOE_BLOCK
    row_tok = jnp.full((P,), T, jnp.int32).at[dest].set(flat_tok[order])
    row_w = jnp.zeros((P,), jnp.float32).at[dest].set(flat_w[order])
    nblk = P // MOE_BLOCK
    blk_e = jnp.minimum(jnp.searchsorted(pends, jnp.arange(nblk) * MOE_BLOCK, side='right'), N_EXPERTS - 1)
    x_pad = jnp.concatenate([x, jnp.zeros((1, D), x.dtype)], axis=0)
    xb = x_pad[row_tok].reshape(nblk, MOE_BLOCK, D)

    def expert_block(args):
        xb_i, e = args
        return (jax.nn.silu(xb_i @ w_gate[e]) * (xb_i @ w_up[e])) @ w_down[e]

    yb = lax.map(expert_block, (xb, blk_e)).reshape(P, D)
    y = jax.ops.segment_sum(yb * row_w[:, None].astype(yb.dtype), row_tok, num_segments=T + 1)[:T]
    return y.reshape(B, S, D)


def per_layer_input(h, p_i, w_proj, gate_norm, w_gate):
    return h + (p_i @ w_proj) * jax.nn.sigmoid(rms_norm(h, gate_norm) @ w_gate)


def setup_inputs(seed: int = 0) -> dict:
    key = jax.random.key(seed)
    ks = iter(jax.random.split(key, 48))

    def nrm(shape, scale):
        return jax.random.normal(next(ks), shape, jnp.float32) * scale

    def gain(shape):
        return 1.0 + 0.01 * jax.random.normal(next(ks), shape, jnp.float32)

    ne, no, D = N_EVEN, N_ODD, D_MODEL
    cmp_in = D_CMP_LEN * HEAD_DIM
    return {
        "x": nrm((BATCH, SEQ, D), 1.0),
        "p": nrm((DEPTH, BATCH, SEQ, PLE_DIM), 1.0),
        "positions": (jax.random.randint(next(ks), (BATCH, 1), 0, MAX_POS_OFFSET)
                      + jnp.arange(SEQ)[None, :]).astype(jnp.int32),
        "ab_w_in": nrm((ne, D, EVEN_IN), D ** -0.5),
        "ab_sinks": nrm((ne, A_HEADS), 0.5),
        "ab_q_lat_norm": gain((ne, B_Q_LORA)),
        "ab_kv_lat_norm": gain((ne, B_KV_LORA)),
        "ab_w_uq": nrm((ne, B_Q_LORA, B_HEADS * (B_NOPE + B_ROPE)), B_Q_LORA ** -0.5),
        "ab_w_ukv": nrm((ne, B_KV_LORA, B_HEADS * (B_NOPE + B_V)), B_KV_LORA ** -0.5),
        "ab_w_o": nrm((ne, EVEN_OUT, D), EVEN_OUT ** -0.5),
        "cd_w_in": nrm((no, D, ODD_IN), D ** -0.5),
        "cd_forget_bias": 3.0 + nrm((no, C_HEADS), 0.5),
        "cd_cmp_pe_k": nrm((no, D_CMP_LEN, HEAD_DIM), 0.02),
        "cd_cmp_w1_k": nrm((no, cmp_in, D_CMP_HIDDEN), cmp_in ** -0.5),
        "cd_cmp_w2_k": nrm((no, D_CMP_HIDDEN, HEAD_DIM), D_CMP_HIDDEN ** -0.5),
        "cd_cmp_pe_v": nrm((no, D_CMP_LEN, HEAD_DIM), 0.02),
        "cd_cmp_w1_v": nrm((no, cmp_in, D_CMP_HIDDEN), cmp_in ** -0.5),
        "cd_cmp_w2_v": nrm((no, D_CMP_HIDDEN, HEAD_DIM), D_CMP_HIDDEN ** -0.5),
        "cd_w_o": nrm((no, ODD_OUT, D), ODD_OUT ** -0.5),
        "mixer_norm": gain((DEPTH, D)),
        "moe_norm": gain((DEPTH, D)),
        "router_group_w": nrm((DEPTH, D, N_GROUPS), D ** -0.5),
        "router_group_b": nrm((DEPTH, N_GROUPS), 0.01),
        "router_expert_w": nrm((DEPTH, D, N_EXPERTS), D ** -0.5),
        "router_expert_b": nrm((DEPTH, N_EXPERTS), 0.01),
        "expert_w_gate": nrm((DEPTH, N_EXPERTS, D, EXPERT_HIDDEN), D ** -0.5),
        "expert_w_up": nrm((DEPTH, N_EXPERTS, D, EXPERT_HIDDEN), D ** -0.5),
        "expert_w_down": nrm((DEPTH, N_EXPERTS, EXPERT_HIDDEN, D), EXPERT_HIDDEN ** -0.5),
        "ple_proj": nrm((DEPTH, PLE_DIM, D), PLE_DIM ** -0.5),
        "ple_gate_norm": gain((DEPTH, D)),
        "ple_gate_w": nrm((DEPTH, D, D), D ** -0.5),
        "final_norm": gain((D,)),
    }


def reference(x, p, positions, ab_w_in, ab_sinks, ab_q_lat_norm, ab_kv_lat_norm, ab_w_uq, ab_w_ukv, ab_w_o,
              cd_w_in, cd_forget_bias, cd_cmp_pe_k, cd_cmp_w1_k, cd_cmp_w2_k, cd_cmp_pe_v, cd_cmp_w1_v,
              cd_cmp_w2_v, cd_w_o, mixer_norm, moe_norm, router_group_w, router_group_b, router_expert_w,
              router_expert_b, expert_w_gate, expert_w_up, expert_w_down, ple_proj, ple_gate_norm, ple_gate_w,
              final_norm):
    cos, sin = rope_tables(positions, HEAD_DIM)
    cos_r, sin_r = rope_tables(positions, B_ROPE)
    h = x
    for i in range(DEPTH):
        j = i // 2
        hn = rms_norm(h, mixer_norm[i])
        if i % 2 == 0:
            h = h + mixer_ab(hn, cos, sin, cos_r, sin_r, ab_w_in[j], ab_sinks[j], ab_q_lat_norm[j],
                             ab_kv_lat_norm[j], ab_w_uq[j], ab_w_ukv[j], ab_w_o[j])
        else:
            h = h + mixer_cd(hn, positions, cos, sin, cd_w_in[j], cd_forget_bias[j], cd_cmp_pe_k[j],
                             cd_cmp_w1_k[j], cd_cmp_w2_k[j], cd_cmp_pe_v[j], cd_cmp_w1_v[j], cd_cmp_w2_v[j],
                             cd_w_o[j])
        h = h + hierarchical_moe(rms_norm(h, moe_norm[i]), router_group_w[i], router_group_b[i],
                                 router_expert_w[i], router_expert_b[i], expert_w_gate[i], expert_w_up[i],
                                 expert_w_down[i])
        h = per_layer_input(h, p[i], ple_proj[i], ple_gate_norm[i], ple_gate_w[i])
    return rms_norm(h, final_norm)
```

```python
import functools
import math

import numpy as np
import jax
import jax.numpy as jnp
from jax import lax
from jax.experimental import pallas as pl
from jax.experimental.pallas import tpu as pltpu

F32 = jnp.float32
BF16 = jnp.bfloat16

HEAD_DIM = 128
ROPE_THETA = 10000.0
NORM_EPS = 1e-6
NEG_INF = -1e30
TAKEN = -3e38
A_HEADS, A_KV_HEADS, A_WINDOW = 8, 2, 128
B_HEADS, B_Q_LORA, B_KV_LORA, B_NOPE, B_ROPE, B_V = 8, 512, 256, 128, 64, 128
C_HEADS = 8
D_HEADS, D_KV_HEADS = 8, 2
D_CMP_LEN, D_CMP_STRIDE, D_SLC_LEN, D_SLC_TOPN, D_WINDOW = 32, 16, 64, 8, 512
FORCE_BONUS = 1e4
N_GROUPS, EXPERTS_PER_GROUP, TOP_K = 4, 8, 2
N_EXPERTS = N_GROUPS * EXPERTS_PER_GROUP

LANES = 128
VMEM_LIMIT_BYTES = 56 * 1024 * 1024
MOE_ROWS = 256
DMA_WINDOW = 64


def _params(*sem):
    return pltpu.CompilerParams(dimension_semantics=sem, vmem_limit_bytes=VMEM_LIMIT_BYTES)


def _tile(n, pref):
    t = min(n, pref)
    while n % t:
        t -= 1
    return t


def _mm_body(*refs, nx, has_gain, emit_xn, mode):
    it = iter(refs)
    x_refs = [next(it) for _ in range(nx)]
    g_ref = next(it) if has_gain else None
    w_refs = [next(it) for _ in range(nx)]
    r_ref = next(it) if mode in ("resid", "ple") else None
    p_ref = next(it) if mode == "ple" else None
    wp_ref = next(it) if mode == "ple" else None
    o_ref = next(it)
    xo_ref = next(it) if emit_xn else None
    xn_ref = next(it) if has_gain else None

    if has_gain:
        @pl.when(pl.program_id(1) == 0)
        def _():
            x = x_refs[0][...].astype(F32)
            y = x * lax.rsqrt(jnp.mean(x * x, axis=-1, keepdims=True) + NORM_EPS) * g_ref[...]
            xn_ref[...] = y.astype(BF16)
            if emit_xn:
                xo_ref[...] = y
        lhs = [xn_ref[...]]
    else:
        lhs = [x_ref[...].astype(BF16) for x_ref in x_refs]
    acc = None
    for a, w_ref in zip(lhs, w_refs):
        d = jnp.dot(a, w_ref[...], preferred_element_type=F32)
        acc = d if acc is None else acc + d
    if mode == "resid":
        acc = r_ref[...] + acc
    elif mode == "ple":
        pp = jnp.dot(p_ref[...].astype(BF16), wp_ref[...], preferred_element_type=F32)
        acc = r_ref[...] + pp * jax.nn.sigmoid(acc)
    o_ref[...] = acc.astype(o_ref.dtype)


def _mm(xs, ws, *, name, gain=None, out_dtype=F32, tm=1024, tn=512, resid=None, ple=None,
        emit_xn=False):
    M = xs[0][0].shape[0]
    N = ws[0].shape[1]
    tm, tn = _tile(M, tm), _tile(N, tn)
    nx = len(xs)
    has_gain = gain is not None
    mode = "ple" if ple is not None else ("resid" if resid is not None else "none")
    args, in_specs = [], []
    for arr, cb, K in xs:
        args.append(arr)
        in_specs.append(pl.BlockSpec((tm, K), lambda i, j, cb=cb: (i, cb)))
    if has_gain:
        K0 = xs[0][2]
        args.append(gain.reshape(1, K0).astype(F32))
        in_specs.append(pl.BlockSpec((1, K0), lambda i, j: (0, 0)))
    for (arr, cb, K), w in zip(xs, ws):
        args.append(w)
        in_specs.append(pl.BlockSpec((K, tn), lambda i, j: (0, j)))
    if mode == "resid":
        args.append(resid)
        in_specs.append(pl.BlockSpec((tm, tn), lambda i, j: (i, j)))
    if mode == "ple":
        r, p, wp = ple
        args += [r, p, wp]
        in_specs += [pl.BlockSpec((tm, tn), lambda i, j: (i, j)),
                     pl.BlockSpec((tm, p.shape[1]), lambda i, j: (i, 0)),
                     pl.BlockSpec((p.shape[1], tn), lambda i, j: (0, j))]
    out_shape = [jax.ShapeDtypeStruct((M, N), out_dtype)]
    out_specs = [pl.BlockSpec((tm, tn), lambda i, j: (i, j))]
    if emit_xn:
        out_shape.append(jax.ShapeDtypeStruct((M, xs[0][2]), F32))
        out_specs.append(pl.BlockSpec((tm, xs[0][2]), lambda i, j: (i, 0)))
    scratch = [pltpu.VMEM((tm, xs[0][2]), BF16)] if has_gain else []
    res = pl.pallas_call(
        functools.partial(_mm_body, nx=nx, has_gain=has_gain, emit_xn=emit_xn, mode=mode),
        out_shape=out_shape, grid=(M // tm, N // tn), in_specs=in_specs, out_specs=out_specs,
        scratch_shapes=scratch, compiler_params=_params("parallel", "arbitrary"), name=name,
    )(*args)
    return res if emit_xn else res[0]


def _rope_tables(positions):
    def tables(dim):
        inv = 1.0 / (ROPE_THETA ** (jnp.arange(0, dim, 2, dtype=F32) / dim))
        ang = positions.astype(F32)[..., None] * inv
        return jnp.cos(ang), jnp.sin(ang)

    lead = positions.shape
    c, s = tables(HEAD_DIM)
    c128 = jnp.concatenate([c, c], -1)
    s128 = jnp.concatenate([-s, s], -1)
    c, s = tables(B_ROPE)
    z = jnp.zeros_like(c)
    c64 = jnp.concatenate([c, c, z, z], -1)
    s64a = jnp.concatenate([-s, z, z, z], -1)
    s64b = jnp.concatenate([z, s, z, z], -1)
    n = int(np.prod(lead))
    return [t.reshape(n, LANES) for t in (c128, s128, c64, s64a, s64b)]


def _rot128(x, c, s):
    return x * c + pltpu.roll(x, 64, 1) * s


def _rot64(x, c, sa, sb):
    return x * c + pltpu.roll(x, 96, 1) * sa + pltpu.roll(x, 32, 1) * sb


def _prep_body(y_ref, c128, s128, c64, s64a, s64b, *o_refs, program):
    for src, mode, scale, oi, dst in program:
        x = y_ref[:, src * LANES:(src + 1) * LANES].astype(F32)
        if mode == "rot128":
            x = _rot128(x, c128[...], s128[...])
        elif mode == "rot64":
            x = _rot64(x, c64[...], s64a[...], s64b[...])
        if scale != 1.0:
            x = x * scale
        o_refs[oi][:, dst * LANES:(dst + 1) * LANES] = x.astype(o_refs[oi].dtype)


def _prep(y, tables, program, outs, *, name, tm=256):
    M, C = y.shape
    tm = _tile(M, tm)
    row = lambda i: (i, 0)
    return pl.pallas_call(
        functools.partial(_prep_body, program=tuple(program)),
        out_shape=[jax.ShapeDtypeStruct((M, nb * LANES), dt) for nb, dt in outs],
        grid=(M // tm,),
        in_specs=[pl.BlockSpec((tm, C), row)] + [pl.BlockSpec((tm, LANES), row)] * 5,
        out_specs=[pl.BlockSpec((tm, nb * LANES), row) for nb, _ in outs],
        compiler_params=_params("parallel"), name=name,
    )(y, *tables)


def _pair_tables(S, tq, tk, window):
    qi_l, ki_l, fl_l = [], [], []
    for qi in range(S // tq):
        q0, q1 = qi * tq, (qi + 1) * tq - 1
        ks = []
        for ki in range(S // tk):
            k0, k1 = ki * tk, (ki + 1) * tk - 1
            if k0 > q1 or (window is not None and q0 - k1 >= window):
                continue
            full = k1 <= q0 and (window is None or q1 - k0 < window)
            ks.append((ki, not full))
        for n, (ki, partial) in enumerate(ks):
            qi_l.append(qi)
            ki_l.append(ki)
            fl_l.append((1 if n == 0 else 0) | (2 if n == len(ks) - 1 else 0) | (4 if partial else 0))
    return [jnp.asarray(np.array(a, np.int32)) for a in (qi_l, ki_l, fl_l)]


def _flash_body(qi_t, ki_t, fl_t, *refs, nqp, nkp, G, tq, tk, window, has_sink, has_bias, has_sel):
    it = iter(refs)
    q_refs = [next(it) for _ in range(nqp)]
    k_refs = [next(it) for _ in range(nkp)]
    v_ref = next(it)
    sink_ref = next(it) if has_sink else None
    cum_ref = next(it) if has_bias else None
    cumt_ref = next(it) if has_bias else None
    sel_ref = next(it) if has_sel else None
    exp_ref = next(it) if has_sel else None
    o_ref = next(it)
    m_sc, l_sc, acc_sc = next(it), next(it), next(it)
    cq_sc = next(it) if has_bias else None

    step_id = pl.program_id(2)
    qi, ki, fl = qi_t[step_id], ki_t[step_id], fl_t[step_id]

    @pl.when((fl & 1) != 0)
    def _init():
        m_sc[...] = jnp.full(m_sc.shape, NEG_INF, F32)
        l_sc[...] = jnp.zeros(l_sc.shape, F32)
        acc_sc[...] = jnp.zeros(acc_sc.shape, F32)
        if has_bias:
            lane = lax.broadcasted_iota(jnp.int32, (tq, LANES), 1)
            cq_sc[...] = jnp.sum(jnp.where(lane == pl.program_id(1), cum_ref[0], 0.0), axis=-1, keepdims=True)

    def step(partial):
        k = jnp.concatenate([r[0] for r in k_refs], axis=-1) if nkp > 1 else k_refs[0][0]
        v = v_ref[0]
        mask = None
        if partial:
            diff = (qi * tq + lax.broadcasted_iota(jnp.int32, (tq, tk), 0)) - (
                ki * tk + lax.broadcasted_iota(jnp.int32, (tq, tk), 1))
            mask = diff >= 0
            if window is not None:
                mask = mask & (diff < window)
        if has_sel:
            chosen = jnp.dot(sel_ref[0, 0], exp_ref[0], preferred_element_type=F32) > 0.5
            mask = chosen if mask is None else mask & chosen
        bias = (cq_sc[...] - cumt_ref[0, 0]) if has_bias else None
        for g in range(G):
            qs = [r[0][:, g * LANES:(g + 1) * LANES] for r in q_refs]
            q = jnp.concatenate(qs, axis=-1) if nqp > 1 else qs[0]
            s = lax.dot_general(q, k, (((1,), (1,)), ((), ())), preferred_element_type=F32)
            if bias is not None:
                s = s + bias
            if mask is not None:
                s = jnp.where(mask, s, NEG_INF)
            m_prev = m_sc[g]
            m_new = jnp.maximum(m_prev, jnp.max(s, axis=-1, keepdims=True))
            p = jnp.exp(s - m_new)
            if mask is not None:
                p = jnp.where(mask, p, 0.0)
            alpha = jnp.exp(m_prev - m_new)
            l_sc[g] = alpha * l_sc[g] + jnp.sum(p, axis=-1, keepdims=True)
            acc_sc[g] = alpha * acc_sc[g] + jnp.dot(p.astype(BF16), v, preferred_element_type=F32)
            m_sc[g] = m_new

    @pl.when((fl & 4) != 0)
    def _():
        step(True)

    @pl.when((fl & 4) == 0)
    def _():
        step(False)

    @pl.when((fl & 2) != 0)
    def _finish():
        for g in range(G):
            m, l, acc = m_sc[g], l_sc[g], acc_sc[g]
            if has_sink:
                sk = sink_ref[0, g:g + 1, 0:1]
                m_f = jnp.maximum(m, sk)
                w = jnp.exp(m - m_f)
                l = l * w + jnp.exp(sk - m_f)
                acc = acc * w
            o_ref[0, :, g * LANES:(g + 1) * LANES] = (acc / l).astype(o_ref.dtype)


def _flash(q_parts, k_parts, v_part, *, name, B, S, Hk, G, window=None, sinks=None, cum=None,
           cumt=None, sel=None, out_dtype=BF16, tq=256, tk=512):
    tq, tk = _tile(S, tq), _tile(S, tk)
    tabs = _pair_tables(S, tq, tk, window)
    npairs = int(tabs[0].shape[0])
    args, in_specs = [], []
    for arr, cf in q_parts:
        args.append(arr)
        in_specs.append(pl.BlockSpec((1, tq, G * LANES), lambda b, h, s, qt, kt, ft, cf=cf: (b, qt[s], cf(h))))
    for arr, cf in list(k_parts) + [v_part]:
        args.append(arr)
        in_specs.append(pl.BlockSpec((1, tk, LANES), lambda b, h, s, qt, kt, ft, cf=cf: (b, kt[s], cf(h))))
    if sinks is not None:
        args.append(jnp.broadcast_to(sinks.astype(F32).reshape(Hk, G, 1), (Hk, G, LANES)))
        in_specs.append(pl.BlockSpec((1, G, LANES), lambda b, h, s, qt, kt, ft: (h, 0, 0)))
    if cum is not None:
        args += [cum, cumt]
        in_specs += [pl.BlockSpec((1, tq, LANES), lambda b, h, s, qt, kt, ft: (b, qt[s], 0)),
                     pl.BlockSpec((1, 1, 1, tk), lambda b, h, s, qt, kt, ft: (b, h, 0, kt[s]))]
    if sel is not None:
        per = tk // D_SLC_LEN
        e = np.zeros((S // tk, LANES, tk), np.float32)
        for ki in range(S // tk):
            e[ki, ki * per + np.arange(tk) // D_SLC_LEN, np.arange(tk)] = 1.0
        args += [sel, jnp.asarray(e, BF16)]
        in_specs += [pl.BlockSpec((1, 1, tq, LANES), lambda b, h, s, qt, kt, ft: (b, h, qt[s], 0)),
                     pl.BlockSpec((1, LANES, tk), lambda b, h, s, qt, kt, ft: (kt[s], 0, 0))]
    scratch = [pltpu.VMEM((G, tq, 1), F32), pltpu.VMEM((G, tq, 1), F32), pltpu.VMEM((G, tq, LANES), F32)]
    if cum is not None:
        scratch.append(pltpu.VMEM((tq, 1), F32))
    body = functools.partial(_flash_body, nqp=len(q_parts), nkp=len(k_parts), G=G, tq=tq, tk=tk,
                             window=window, has_sink=sinks is not None, has_bias=cum is not None,
                             has_sel=sel is not None)
    return pl.pallas_call(
        body, out_shape=jax.ShapeDtypeStruct((B, S, Hk * G * LANES), out_dtype),
        grid_spec=pltpu.PrefetchScalarGridSpec(
            num_scalar_prefetch=3, grid=(B, Hk, npairs), in_specs=in_specs,
            out_specs=pl.BlockSpec((1, tq, G * LANES), lambda b, h, s, qt, kt, ft: (b, qt[s], h)),
            scratch_shapes=scratch),
        compiler_params=_params("parallel", "parallel", "arbitrary"), name=name,
    )(*tabs, *args)


def _cum_body(y_ref, b_ref, tri_ref, o_ref, carry):
    @pl.when(pl.program_id(1) == 0)
    def _():
        carry[...] = jnp.zeros(carry.shape, F32)
    x = y_ref[0] + b_ref[...]
    logf = jnp.minimum(x, 0.0) - jnp.log1p(jnp.exp(-jnp.abs(x)))
    cum = jnp.dot(tri_ref[...], logf, preferred_element_type=F32, precision=lax.Precision.HIGHEST) + carry[...]
    o_ref[0] = cum
    carry[...] = cum[-1:, :]


def _forget_cum(y3, col_block, bias_row, *, ts=512):
    B, S, _ = y3.shape
    ts = _tile(S, ts)
    tri = jnp.asarray(np.tril(np.ones((ts, ts), np.float32)))
    return pl.pallas_call(
        _cum_body, out_shape=jax.ShapeDtypeStruct((B, S, LANES), F32), grid=(B, S // ts),
        in_specs=[pl.BlockSpec((1, ts, LANES), lambda b, s: (b, s, col_block)),
                  pl.BlockSpec((1, LANES), lambda b, s: (0, 0)),
                  pl.BlockSpec((ts, ts), lambda b, s: (0, 0))],
        out_specs=pl.BlockSpec((1, ts, LANES), lambda b, s: (b, s, 0)),
        scratch_shapes=[pltpu.VMEM((1, LANES), F32)],
        compiler_params=_params("parallel", "arbitrary"), name="forget_cum",
    )(y3, bias_row, tri)


def _compress_body(*refs, rope, nc):
    if rope:
        z_ref, pe_ref, w1_ref, w2_ref, c_ref, s_ref, o_ref = refs
    else:
        z_ref, pe_ref, w1_ref, w2_ref, o_ref = refs
    half = D_CMP_LEN // 2
    width = D_KV_HEADS * HEAD_DIM
    for hk in range(D_KV_HEADS):
        u = jnp.zeros((nc, w1_ref.shape[1]), F32)
        v = jnp.zeros((nc, w1_ref.shape[1]), F32)
        for l in range(half):
            z = z_ref[0, :, l * width + hk * HEAD_DIM:l * width + (hk + 1) * HEAD_DIM]
            zu = (z + pe_ref[l:l + 1, :]).astype(BF16)
            zv = (z + pe_ref[half + l:half + l + 1, :]).astype(BF16)
            u = u + jnp.dot(zu, w1_ref[l * HEAD_DIM:(l + 1) * HEAD_DIM, :], preferred_element_type=F32)
            v = v + jnp.dot(zv, w1_ref[(half + l) * HEAD_DIM:(half + l + 1) * HEAD_DIM, :],
                            preferred_element_type=F32)
        pre = u + pltpu.roll(v, nc - 1, 0)
        hid = jax.nn.gelu(pre, approximate=True)
        out = jnp.dot(hid.astype(BF16), w2_ref[...], preferred_element_type=F32)
        if rope:
            out = _rot128(out, c_ref[0], s_ref[0])
        o_ref[0, hk] = out.astype(o_ref.dtype)


def _compress(z, pe, w1, w2, rope_tabs=None):
    B, S, W = z.shape
    nc = S // D_CMP_STRIDE
    zc = z.reshape(B, nc, D_CMP_STRIDE * W)
    args = [zc, pe.astype(F32), w1.astype(BF16), w2.astype(BF16)]
    in_specs = [pl.BlockSpec((1, nc, D_CMP_STRIDE * W), lambda b: (b, 0, 0)),
                pl.BlockSpec(pe.shape, lambda b: (0, 0)),
                pl.BlockSpec(w1.shape, lambda b: (0, 0)),
                pl.BlockSpec(w2.shape, lambda b: (0, 0))]
    if rope_tabs is not None:
        args += list(rope_tabs)
        in_specs += [pl.BlockSpec((1, nc, LANES), lambda b: (b, 0, 0))] * 2
    return pl.pallas_call(
        functools.partial(_compress_body, rope=rope_tabs is not None, nc=nc),
        out_shape=jax.ShapeDtypeStruct((B, D_KV_HEADS, nc, HEAD_DIM), BF16), grid=(B,),
        in_specs=in_specs, out_specs=pl.BlockSpec((1, D_KV_HEADS, nc, HEAD_DIM), lambda b: (b, 0, 0, 0)),
        compiler_params=_params("parallel"), name="nsa_compress",
    )(*args)


def _cmp_attn_body(q_ref, k_ref, v_ref, ov_ref, o_ref, sel_ref, *, G, tq, nc, n_cmp, n_slc, topn):
    qi = pl.program_id(2)
    t = qi * tq + lax.broadcasted_iota(jnp.int32, (tq, nc), 0)
    c = lax.broadcasted_iota(jnp.int32, (tq, nc), 1)
    valid = (c * D_CMP_STRIDE + (D_CMP_LEN - 1) <= t) & (c < n_cmp)
    k = k_ref[0, 0]
    v = v_ref[0, 0]
    psum = jnp.zeros((tq, nc), F32)
    for g in range(G):
        q = q_ref[0, :, g * LANES:(g + 1) * LANES]
        s = lax.dot_general(q, k, (((1,), (1,)), ((), ())), preferred_element_type=F32)
        s = jnp.where(valid, s, NEG_INF)
        e = jnp.where(valid, jnp.exp(s - jnp.max(s, axis=-1, keepdims=True)), 0.0)
        p = e / jnp.maximum(jnp.sum(e, axis=-1, keepdims=True), jnp.finfo(F32).tiny)
        o_ref[0, :, g * LANES:(g + 1) * LANES] = jnp.dot(
            p.astype(BF16), v, preferred_element_type=F32).astype(o_ref.dtype)
        psum = psum + p
    imp = jnp.dot(psum, ov_ref[...], preferred_element_type=F32, precision=lax.Precision.HIGHEST)
    lane = lax.broadcasted_iota(jnp.int32, (tq, LANES), 1)
    trow = qi * tq + lax.broadcasted_iota(jnp.int32, (tq, LANES), 0)
    cur = jnp.right_shift(trow, int(math.log2(D_SLC_LEN)))
    forced = (lane == 0) | (lane == cur) | (lane == cur - 1)
    imp = jnp.where(lane * D_SLC_LEN > trow, NEG_INF, imp + jnp.where(forced, FORCE_BONUS, 0.0))
    imp = jnp.where(lane >= n_slc, TAKEN, imp)
    chosen = jnp.zeros((tq, LANES), F32)
    for _ in range(topn):
        mx = jnp.max(imp, axis=-1, keepdims=True)
        idx = jnp.min(jnp.where(imp == mx, lane, LANES), axis=-1, keepdims=True)
        hit = lane == idx
        chosen = jnp.where(hit, 1.0, chosen)
        imp = jnp.where(hit, TAKEN, imp)
    sel_ref[0, 0] = chosen.astype(sel_ref.dtype)


def _cmp_attn(q, k_cmp, v_cmp, *, tq=256):
    B, S, _ = q.shape
    Hk, G = D_KV_HEADS, D_HEADS // D_KV_HEADS
    nc = S // D_CMP_STRIDE
    n_cmp = (S - D_CMP_LEN) // D_CMP_STRIDE + 1
    n_slc = S // D_SLC_LEN
    tq = _tile(S, tq)
    c0 = np.arange(nc) * D_CMP_STRIDE
    s0 = np.arange(LANES) * D_SLC_LEN
    ov = ((c0[:, None] < (s0 + D_SLC_LEN)[None, :]) & ((c0 + D_CMP_LEN)[:, None] > s0[None, :])
          & (np.arange(nc) < n_cmp)[:, None] & (np.arange(LANES) < n_slc)[None, :]).astype(np.float32)
    body = functools.partial(_cmp_attn_body, G=G, tq=tq, nc=nc, n_cmp=n_cmp, n_slc=n_slc,
                             topn=min(D_SLC_TOPN, n_slc))
    return pl.pallas_call(
        body,
        out_shape=[jax.ShapeDtypeStruct((B, S, Hk * G * LANES), F32),
                   jax.ShapeDtypeStruct((B, Hk, S, LANES), BF16)],
        grid=(B, Hk, S // tq),
        in_specs=[pl.BlockSpec((1, tq, G * LANES), lambda b, h, i: (b, i, h)),
                  pl.BlockSpec((1, 1, nc, LANES), lambda b, h, i: (b, h, 0, 0)),
                  pl.BlockSpec((1, 1, nc, LANES), lambda b, h, i: (b, h, 0, 0)),
                  pl.BlockSpec((nc, LANES), lambda b, h, i: (0, 0))],
        out_specs=[pl.BlockSpec((1, tq, G * LANES), lambda b, h, i: (b, i, h)),
                   pl.BlockSpec((1, 1, tq, LANES), lambda b, h, i: (b, h, i, 0))],
        compiler_params=_params("parallel", "parallel", "parallel"), name="nsa_cmp_attn",
    )(q, k_cmp, v_cmp, jnp.asarray(ov))


def _gate_body(y_ref, a_ref, b_ref, c_ref, o_ref, *, lane0):
    g = jax.nn.sigmoid(y_ref[...])
    for h in range(D_HEADS):
        cols = slice(h * LANES, (h + 1) * LANES)
        ga = g[:, lane0 + h:lane0 + h + 1]
        gb = g[:, lane0 + D_HEADS + h:lane0 + D_HEADS + h + 1]
        gc = g[:, lane0 + 2 * D_HEADS + h:lane0 + 2 * D_HEADS + h + 1]
        o_ref[:, cols] = (ga * a_ref[:, cols] + gb * b_ref[:, cols] + gc * c_ref[:, cols]).astype(o_ref.dtype)


def _nsa_gate(y, col_block, lane0, o_cmp, o_slc, o_win, *, tm=512):
    M = y.shape[0]
    W = o_cmp.shape[1]
    tm = _tile(M, tm)
    row = lambda i: (i, 0)
    return pl.pallas_call(
        functools.partial(_gate_body, lane0=lane0), out_shape=jax.ShapeDtypeStruct((M, W), BF16),
        grid=(M // tm,),
        in_specs=[pl.BlockSpec((tm, LANES), lambda i: (i, col_block))] + [pl.BlockSpec((tm, W), row)] * 3,
        out_specs=pl.BlockSpec((tm, W), row), compiler_params=_params("parallel"), name="nsa_gate",
    )(y, o_cmp, o_slc, o_win)


def _route_body(lg_ref, b_ref, tri_ref, e_ref, w_ref, pos_ref, cnt_ref, carry, *, tm):
    @pl.when(pl.program_id(0) == 0)
    def _():
        carry[...] = jnp.zeros(carry.shape, F32)
    lane = lax.broadcasted_iota(jnp.int32, (tm, LANES), 1)
    logits = lg_ref[...] + b_ref[...]
    gl = jnp.where(lane < N_GROUPS, logits, -jnp.inf)
    gmax = jnp.max(gl, axis=-1, keepdims=True)
    g_val = 1.0 / jnp.sum(jnp.exp(gl - gmax), axis=-1, keepdims=True)
    g_idx = jnp.min(jnp.where(gl == gmax, lane, LANES), axis=-1, keepdims=True)
    lo = N_GROUPS + EXPERTS_PER_GROUP * g_idx
    el = jnp.where((lane >= lo) & (lane < lo + EXPERTS_PER_GROUP), logits, -jnp.inf)
    e1 = jnp.max(el, axis=-1, keepdims=True)
    i1 = jnp.min(jnp.where(el == e1, lane, LANES), axis=-1, keepdims=True)
    el2 = jnp.where(lane == i1, -jnp.inf, el)
    e2 = jnp.max(el2, axis=-1, keepdims=True)
    i2 = jnp.min(jnp.where(el2 == e2, lane, LANES), axis=-1, keepdims=True)
    r = jnp.exp(e2 - e1)
    w1 = g_val / (1.0 + r)
    w2 = w1 * r
    x1, x2 = i1 - N_GROUPS, i2 - N_GROUPS
    e_ref[...] = jnp.where(lane == 0, x1, jnp.where(lane == 1, x2, 0))
    w_ref[...] = jnp.where(lane == 0, w1, jnp.where(lane == 1, w2, 0.0))
    hot1 = lane == x1
    hot2 = lane == x2
    both = jnp.where(hot1 | hot2, 1.0, 0.0)
    before = jnp.dot(tri_ref[...], both.astype(BF16), preferred_element_type=F32) + carry[...]
    p1 = jnp.sum(jnp.where(hot1, before, 0.0), axis=-1, keepdims=True)
    p2 = jnp.sum(jnp.where(hot2, before, 0.0), axis=-1, keepdims=True)
    pos_ref[...] = jnp.where(lane == 0, p1, jnp.where(lane == 1, p2, 0.0)).astype(jnp.int32)
    carry[...] = carry[...] + jnp.sum(both, axis=0, keepdims=True)
    cnt_ref[...] = carry[...].astype(jnp.int32)


def _route(logits, bias_row, *, tm=512):
    T = logits.shape[0]
    tm = _tile(T, tm)
    tri = jnp.asarray(np.tril(np.ones((tm, tm), np.float32), -1), BF16)
    row = lambda i: (i, 0)
    fixed = lambda i: (0, 0)
    return pl.pallas_call(
        functools.partial(_route_body, tm=tm),
        out_shape=[jax.ShapeDtypeStruct((T, LANES), jnp.int32), jax.ShapeDtypeStruct((T, LANES), F32),
                   jax.ShapeDtypeStruct((T, LANES), jnp.int32), jax.ShapeDtypeStruct((1, LANES), jnp.int32)],
        grid=(T // tm,),
        in_specs=[pl.BlockSpec((tm, LANES), row), pl.BlockSpec((1, LANES), fixed), pl.BlockSpec((tm, tm), fixed)],
        out_specs=[pl.BlockSpec((tm, LANES), row)] * 3 + [pl.BlockSpec((1, LANES), fixed)],
        scratch_shapes=[pltpu.VMEM((1, LANES), F32)],
        compiler_params=_params("arbitrary"), name="moe_route",
    )(logits, bias_row, tri)


def _row_copy_body(*refs, n, has_init):
    if has_init:
        si_ref, di_ref, src_ref, _, out_ref, sem = refs
    else:
        si_ref, di_ref, src_ref, out_ref, sem = refs
    window = min(DMA_WINDOW, n)

    def wait_one():
        pltpu.make_async_copy(src_ref.at[pl.ds(0, 1)], out_ref.at[pl.ds(0, 1)], sem).wait()

    def issue(i, c):
        pltpu.make_async_copy(src_ref.at[pl.ds(si_ref[i], 1)], out_ref.at[pl.ds(di_ref[i], 1)], sem).start()

        @pl.when(i >= window)
        def _():
            wait_one()
        return c

    lax.fori_loop(0, n, issue, 0)

    def drain(i, c):
        wait_one()
        return c

    lax.fori_loop(0, window, drain, 0)


def _row_copy(src, src_idx, dst_idx, n_out, *, name, init=None):
    n = int(src_idx.shape[0])
    D = src.shape[1]
    args = [src_idx, dst_idx, src]
    in_specs = [pl.BlockSpec(memory_space=pl.ANY)]
    aliases = {}
    if init is not None:
        args.append(init)
        in_specs.append(pl.BlockSpec(memory_space=pl.ANY))
        aliases = {3: 0}
    return pl.pallas_call(
        functools.partial(_row_copy_body, n=n, has_init=init is not None),
        out_shape=jax.ShapeDtypeStruct((n_out, D), src.dtype),
        grid_spec=pltpu.PrefetchScalarGridSpec(
            num_scalar_prefetch=2, grid=(1,), in_specs=in_specs,
            out_specs=pl.BlockSpec(memory_space=pl.ANY),
            scratch_shapes=[pltpu.SemaphoreType.DMA(())]),
        input_output_aliases=aliases,
        compiler_params=pltpu.CompilerParams(dimension_semantics=("arbitrary",), has_side_effects=True),
        name=name,
    )(*args)


def _expert_body(be_ref, nu_ref, x_ref, wg_ref, wu_ref, wd_ref, o_ref, wg_sc, wu_sc, wd_sc):
    i = pl.program_id(0)
    fresh = (i == 0) | (be_ref[i] != be_ref[jnp.maximum(i - 1, 0)])

    @pl.when(fresh)
    def _():
        wg_sc[...] = wg_ref[0].astype(BF16)
        wu_sc[...] = wu_ref[0].astype(BF16)
        wd_sc[...] = wd_ref[0].astype(BF16)

    @pl.when(i < nu_ref[0])
    def _():
        x = x_ref[...].astype(BF16)
        gate = jnp.dot(x, wg_sc[...], preferred_element_type=F32)
        up = jnp.dot(x, wu_sc[...], preferred_element_type=F32)
        hid = (gate * jax.nn.sigmoid(gate) * up).astype(BF16)
        o_ref[...] = jnp.dot(hid, wd_sc[...], preferred_element_type=F32)

    @pl.when(i >= nu_ref[0])
    def _():
        o_ref[...] = jnp.zeros(o_ref.shape, F32)


def _experts(xb, blk_e, n_used, w_gate, w_up, w_down):
    P, D = xb.shape
    Hd = w_gate.shape[2]
    nblk = P // MOE_ROWS
    return pl.pallas_call(
        _expert_body, out_shape=jax.ShapeDtypeStruct((P, D), F32),
        grid_spec=pltpu.PrefetchScalarGridSpec(
            num_scalar_prefetch=2, grid=(nblk,),
            in_specs=[pl.BlockSpec((MOE_ROWS, D), lambda i, be, nu: (i, 0)),
                      pl.BlockSpec((1, D, Hd), lambda i, be, nu: (be[i], 0, 0)),
                      pl.BlockSpec((1, D, Hd), lambda i, be, nu: (be[i], 0, 0)),
                      pl.BlockSpec((1, Hd, D), lambda i, be, nu: (be[i], 0, 0))],
            out_specs=pl.BlockSpec((MOE_ROWS, D), lambda i, be, nu: (i, 0)),
            scratch_shapes=[pltpu.VMEM((D, Hd), BF16), pltpu.VMEM((D, Hd), BF16), pltpu.VMEM((Hd, D), BF16)]),
        compiler_params=_params("arbitrary"), name="moe_experts",
    )(blk_e, n_used, xb, w_gate, w_up, w_down)


def _moe_sum_body(h_ref, y_ref, w_ref, o_ref, *, D):
    w = w_ref[...]
    o_ref[...] = h_ref[...] + w[:, 0:1] * y_ref[:, :D] + w[:, 1:2] * y_ref[:, D:]


def _moe_sum(h, y2, wts, *, tm=256):
    T, D = h.shape
    tm = _tile(T, tm)
    row = lambda i: (i, 0)
    return pl.pallas_call(
        functools.partial(_moe_sum_body, D=D), out_shape=jax.ShapeDtypeStruct((T, D), F32), grid=(T // tm,),
        in_specs=[pl.BlockSpec((tm, D), row), pl.BlockSpec((tm, 2 * D), row), pl.BlockSpec((tm, LANES), row)],
        out_specs=pl.BlockSpec((tm, D), row), compiler_params=_params("parallel"), name="moe_sum",
    )(h, y2, wts)


def _moe(h, norm_g, w_group, b_group, w_expert, b_expert, w_gate, w_up, w_down):
    T, D = h.shape
    pad = LANES - N_GROUPS - N_EXPERTS
    w_r = jnp.concatenate([w_group, w_expert, jnp.zeros((D, pad), F32)], axis=1).astype(BF16)
    b_r = jnp.concatenate([b_group, b_expert, jnp.zeros((pad,), F32)]).astype(F32).reshape(1, LANES)
    logits, xn = _mm([(h, 0, D)], [w_r], gain=norm_g, emit_xn=True, tm=512, tn=LANES, name="moe_router")
    eid, wts, pos, cnt = _route(logits, b_r)
    counts = cnt[0, :N_EXPERTS]
    padded = (counts + MOE_ROWS - 1) // MOE_ROWS * MOE_ROWS
    ends = jnp.cumsum(padded)
    offs = ends - padded
    dest = (offs[eid[:, :TOP_K]] + pos[:, :TOP_K]).reshape(T * TOP_K).astype(jnp.int32)
    P = T * TOP_K + N_EXPERTS * MOE_ROWS
    nblk = P // MOE_ROWS
    blk_e = jnp.minimum(jnp.searchsorted(ends, jnp.arange(nblk, dtype=jnp.int32) * MOE_ROWS, side="right"),
                        N_EXPERTS - 1).astype(jnp.int32)
    n_used = (ends[-1:] // MOE_ROWS).astype(jnp.int32)
    tok = jnp.arange(T * TOP_K, dtype=jnp.int32) // TOP_K
    xb = _row_copy(xn, tok, dest, P, init=jnp.zeros((P, D), F32), name="moe_dispatch")
    yb = _experts(xb, blk_e, n_used, w_gate, w_up, w_down)
    y2 = _row_copy(yb, dest, jnp.arange(T * TOP_K, dtype=jnp.int32), T * TOP_K, name="moe_collect")
    return _moe_sum(h, y2.reshape(T, TOP_K * D), wts)


def _pad_cols(w, n):
    return jnp.pad(w, ((0, 0), (0, n - w.shape[1])))


def _layer_even(h, B, S, tabs, norm_g, w_in, sinks, q_lat_norm, kv_lat_norm, w_uq, w_ukv, w_o):
    T, D = h.shape
    n_in = 2560
    y = _mm([(h, 0, D)], [_pad_cols(w_in, n_in).astype(BF16)], gain=norm_g, tn=512, name="ab_in")
    sa = HEAD_DIM ** -0.5
    prog = ([(i, "rot128", sa, 0, i) for i in range(8)] + [(8 + i, "rot128", 1.0, 1, i) for i in range(2)]
            + [(10 + i, "copy", 1.0, 2, i) for i in range(2)] + [(18, "rot64", 1.0, 3, 0)])
    q_a, k_a, v_a, k_r = _prep(y, tabs, prog, [(8, BF16), (2, BF16), (2, BF16), (1, BF16)], name="ab_prep")
    wq = w_uq.reshape(B_Q_LORA, B_HEADS, B_NOPE + B_ROPE)
    wq_n = wq[:, :, :B_NOPE].reshape(B_Q_LORA, B_HEADS * B_NOPE)
    wq_r = jnp.pad(wq[:, :, B_NOPE:], ((0, 0), (0, 0), (0, LANES - B_ROPE))).reshape(B_Q_LORA, B_HEADS * LANES)
    qb = _mm([(y, 1536 // B_Q_LORA, B_Q_LORA)], [jnp.concatenate([wq_n, wq_r], 1).astype(BF16)],
             gain=q_lat_norm, tn=512, name="mla_uq")
    sb = (B_NOPE + B_ROPE) ** -0.5
    prog = [(i, "copy", sb, 0, i) for i in range(8)] + [(8 + i, "rot64", sb, 1, i) for i in range(8)]
    q_n, q_r = _prep(qb, tabs, prog, [(8, BF16), (8, BF16)], name="mla_prep")
    kv = _mm([(y, 2048 // B_KV_LORA, B_KV_LORA)], [w_ukv.astype(BF16)], gain=kv_lat_norm, out_dtype=BF16,
             tn=512, name="mla_ukv")
    r3 = lambda a: a.reshape(B, S, a.shape[-1])
    G = A_HEADS // A_KV_HEADS
    o_a = _flash([(r3(q_a), lambda h: h)], [(r3(k_a), lambda h: h)], (r3(v_a), lambda h: h), name="swa_attn",
                 B=B, S=S, Hk=A_KV_HEADS, G=G, window=A_WINDOW, sinks=sinks, tq=256, tk=256)
    kv3 = r3(kv)
    o_b = _flash([(r3(q_n), lambda h: h), (r3(q_r), lambda h: h)],
                 [(kv3, lambda h: 2 * h), (r3(k_r), lambda h: 0)], (kv3, lambda h: 2 * h + 1),
                 name="mla_attn", B=B, S=S, Hk=B_HEADS, G=1)
    wo = w_o.astype(BF16)
    na = A_HEADS * HEAD_DIM
    return _mm([(o_a.reshape(T, -1), 0, na), (o_b.reshape(T, -1), 0, B_HEADS * B_V)], [wo[:na], wo[na:]],
               resid=h, tn=512, name="ab_out")


def _layer_odd(h, B, S, tabs, cmp_tabs, norm_g, w_in, forget_bias, pe_k, w1_k, w2_k, pe_v, w1_v, w2_v, w_o):
    T, D = h.shape
    hc, kvw = C_HEADS * HEAD_DIM, D_KV_HEADS * HEAD_DIM
    o_cf = 3 * hc
    o_dq = o_cf + C_HEADS
    o_dg = o_dq + D_HEADS * HEAD_DIM + 6 * kvw
    w_r = jnp.concatenate([w_in[:, :o_cf], w_in[:, o_dq:o_dg], w_in[:, o_cf:o_dq], w_in[:, o_dg:]], axis=1)
    n_in = 5760
    y = _mm([(h, 0, D)], [_pad_cols(w_r, n_in).astype(BF16)], gain=norm_g, tn=640, name="cd_in")
    misc = (n_in // LANES) - 1
    sc = HEAD_DIM ** -0.5
    prog = ([(i, "copy", sc, 0, i) for i in range(8)] + [(8 + i, "copy", 1.0, 1, i) for i in range(8)]
            + [(16 + i, "copy", 1.0, 2, i) for i in range(8)] + [(24 + i, "rot128", sc, 3, i) for i in range(8)]
            + [(32 + i, "copy", 1.0, 4, i) for i in range(2)] + [(34 + i, "copy", 1.0, 5, i) for i in range(2)]
            + [(36 + i, "rot128", 1.0, 6, i) for i in range(2)] + [(38 + i, "copy", 1.0, 7, i) for i in range(2)]
            + [(40 + i, "rot128", 1.0, 8, i) for i in range(2)] + [(42 + i, "copy", 1.0, 9, i) for i in range(2)])
    outs = [(8, BF16)] * 4 + [(2, F32), (2, F32)] + [(2, BF16)] * 4
    q_c, k_c, v_c, q_d, z_kc, z_vc, k_s, v_s, k_w, v_w = _prep(y, tabs, prog, outs, name="cd_prep")
    r3 = lambda a: a.reshape(B, S, a.shape[-1])
    fb = jnp.pad(forget_bias.astype(F32), (0, LANES - C_HEADS)).reshape(1, LANES)
    cum = _forget_cum(r3(y), misc, fb)
    cumt = jnp.swapaxes(cum[:, :, :C_HEADS], 1, 2).reshape(B, C_HEADS, 1, S)
    o_c = _flash([(r3(q_c), lambda h: h)], [(r3(k_c), lambda h: h)], (r3(v_c), lambda h: h), name="fox_attn",
                 B=B, S=S, Hk=C_HEADS, G=1, cum=cum, cumt=cumt)
    G = D_HEADS // D_KV_HEADS
    k_cmp = _compress(r3(z_kc), pe_k, w1_k, w2_k, rope_tabs=cmp_tabs)
    v_cmp = _compress(r3(z_vc), pe_v, w1_v, w2_v)
    q3 = r3(q_d)
    o_cmp, sel = _cmp_attn(q3, k_cmp, v_cmp)
    o_slc = _flash([(q3, lambda h: h)], [(r3(k_s), lambda h: h)], (r3(v_s), lambda h: h), name="nsa_slc_attn",
                   B=B, S=S, Hk=D_KV_HEADS, G=G, sel=sel, out_dtype=F32)
    o_win = _flash([(q3, lambda h: h)], [(r3(k_w), lambda h: h)], (r3(v_w), lambda h: h), name="nsa_win_attn",
                   B=B, S=S, Hk=D_KV_HEADS, G=G, window=D_WINDOW, out_dtype=F32)
    o_d = _nsa_gate(y, misc, C_HEADS, o_cmp.reshape(T, -1), o_slc.reshape(T, -1), o_win.reshape(T, -1))
    wo = w_o.astype(BF16)
    return _mm([(o_c.reshape(T, -1), 0, hc), (o_d, 0, D_HEADS * HEAD_DIM)], [wo[:hc], wo[hc:]],
               resid=h, tn=512, name="cd_out")


def _final_norm_body(x_ref, g_ref, o_ref):
    x = x_ref[...]
    o_ref[...] = x * lax.rsqrt(jnp.mean(x * x, axis=-1, keepdims=True) + NORM_EPS) * g_ref[...]


def _final_norm(h, g, *, tm=512):
    T, D = h.shape
    tm = _tile(T, tm)
    return pl.pallas_call(
        _final_norm_body, out_shape=jax.ShapeDtypeStruct((T, D), F32), grid=(T // tm,),
        in_specs=[pl.BlockSpec((tm, D), lambda i: (i, 0)), pl.BlockSpec((1, D), lambda i: (0, 0))],
        out_specs=pl.BlockSpec((tm, D), lambda i: (i, 0)), compiler_params=_params("parallel"),
        name="final_norm",
    )(h, g.reshape(1, D).astype(F32))


def kernel(x, p, positions, ab_w_in, ab_sinks, ab_q_lat_norm, ab_kv_lat_norm, ab_w_uq, ab_w_ukv, ab_w_o,
           cd_w_in, cd_forget_bias, cd_cmp_pe_k, cd_cmp_w1_k, cd_cmp_w2_k, cd_cmp_pe_v, cd_cmp_w1_v,
           cd_cmp_w2_v, cd_w_o, mixer_norm, moe_norm, router_group_w, router_group_b, router_expert_w,
           router_expert_b, expert_w_gate, expert_w_up, expert_w_down, ple_proj, ple_gate_norm, ple_gate_w,
           final_norm):
    B, S, D = x.shape
    T = B * S
    depth = p.shape[0]
    tabs = _rope_tables(positions)
    nc = S // D_CMP_STRIDE
    end = np.minimum(np.arange(nc) * D_CMP_STRIDE + D_CMP_LEN - 1, S - 1)
    cmp_tabs = [t.reshape(B, nc, LANES) for t in _rope_tables(positions[:, end])[:2]]
    h = x.reshape(T, D)
    for i in range(depth):
        j = i // 2
        if i % 2 == 0:
            h = _layer_even(h, B, S, tabs, mixer_norm[i], ab_w_in[j], ab_sinks[j], ab_q_lat_norm[j],
                            ab_kv_lat_norm[j], ab_w_uq[j], ab_w_ukv[j], ab_w_o[j])
        else:
            h = _layer_odd(h, B, S, tabs, cmp_tabs, mixer_norm[i], cd_w_in[j], cd_forget_bias[j],
                           cd_cmp_pe_k[j], cd_cmp_w1_k[j], cd_cmp_w2_k[j], cd_cmp_pe_v[j], cd_cmp_w1_v[j],
                           cd_cmp_w2_v[j], cd_w_o[j])
        h = _moe(h, moe_norm[i], router_group_w[i], router_group_b[i], router_expert_w[i], router_expert_b[i],
                 expert_w_gate[i], expert_w_up[i], expert_w_down[i])
        h = _mm([(h, 0, D)], [ple_gate_w[i].astype(BF16)], gain=ple_gate_norm[i], tn=512,
                ple=(h, p[i].reshape(T, -1), ple_proj[i].astype(BF16)), name="ple")
    return _final_norm(h, final_norm).reshape(B, S, D)
```

```python
import functools
import math

import numpy as np
import jax
import jax.numpy as jnp
from jax import lax
from jax.experimental import pallas as pl
from jax.experimental.pallas import tpu as pltpu

F32 = jnp.float32
BF16 = jnp.bfloat16

HEAD_DIM = 128
ROPE_THETA = 10000.0
NORM_EPS = 1e-6
NEG_INF = -1e30
TAKEN = -3e38
A_HEADS, A_KV_HEADS, A_WINDOW = 8, 2, 128
B_HEADS, B_Q_LORA, B_KV_LORA, B_NOPE, B_ROPE, B_V = 8, 512, 256, 128, 64, 128
C_HEADS = 8
D_HEADS, D_KV_HEADS = 8, 2
D_CMP_LEN, D_CMP_STRIDE, D_SLC_LEN, D_SLC_TOPN, D_WINDOW = 32, 16, 64, 8, 512
FORCE_BONUS = 1e4
N_GROUPS, EXPERTS_PER_GROUP, TOP_K = 4, 8, 2
N_EXPERTS = N_GROUPS * EXPERTS_PER_GROUP

LANES = 128
VMEM_LIMIT_BYTES = 56 * 1024 * 1024
MOE_ROWS = 256
DMA_WINDOW = 128


def _params(*sem):
    return pltpu.CompilerParams(dimension_semantics=sem, vmem_limit_bytes=VMEM_LIMIT_BYTES)


def _tile(n, pref):
    t = min(n, pref)
    while n % t:
        t -= 1
    return t


def _mm_body(*refs, nx, has_gain, emit_xn, mode):
    it = iter(refs)
    x_refs = [next(it) for _ in range(nx)]
    g_ref = next(it) if has_gain else None
    w_refs = [next(it) for _ in range(nx)]
    r_ref = next(it) if mode in ("resid", "ple") else None
    p_ref = next(it) if mode == "ple" else None
    wp_ref = next(it) if mode == "ple" else None
    o_ref = next(it)
    xo_ref = next(it) if emit_xn else None
    xn_ref = next(it) if has_gain else None

    if has_gain:
        @pl.when(pl.program_id(1) == 0)
        def _():
            x = x_refs[0][...].astype(F32)
            y = x * lax.rsqrt(jnp.mean(x * x, axis=-1, keepdims=True) + NORM_EPS) * g_ref[...]
            xn_ref[...] = y.astype(BF16)
            if emit_xn:
                rows, chunks = y.shape[0], y.shape[1] // LANES
                for c in range(chunks):
                    xo_ref[pl.ds(c, rows, stride=chunks), :] = y[:, c * LANES:(c + 1) * LANES]
        lhs = [xn_ref[...]]
    else:
        lhs = [x_ref[...].astype(BF16) for x_ref in x_refs]
    acc = None
    for a, w_ref in zip(lhs, w_refs):
        d = jnp.dot(a, w_ref[...], preferred_element_type=F32)
        acc = d if acc is None else acc + d
    if mode == "resid":
        acc = r_ref[...] + acc
    elif mode == "ple":
        pp = jnp.dot(p_ref[...].astype(BF16), wp_ref[...], preferred_element_type=F32)
        acc = r_ref[...] + pp * jax.nn.sigmoid(acc)
    o_ref[...] = acc.astype(o_ref.dtype)


def _mm(xs, ws, *, name, gain=None, out_dtype=F32, tm=1024, tn=512, resid=None, ple=None,
        emit_xn=False):
    M = xs[0][0].shape[0]
    N = ws[0].shape[1]
    tm, tn = _tile(M, tm), _tile(N, tn)
    nx = len(xs)
    has_gain = gain is not None
    mode = "ple" if ple is not None else ("resid" if resid is not None else "none")
    args, in_specs = [], []
    for arr, cb, K in xs:
        args.append(arr)
        in_specs.append(pl.BlockSpec((tm, K), lambda i, j, cb=cb: (i, cb)))
    if has_gain:
        K0 = xs[0][2]
        args.append(gain.reshape(1, K0).astype(F32))
        in_specs.append(pl.BlockSpec((1, K0), lambda i, j: (0, 0)))
    for (arr, cb, K), w in zip(xs, ws):
        args.append(w)
        in_specs.append(pl.BlockSpec((K, tn), lambda i, j: (0, j)))
    if mode == "resid":
        args.append(resid)
        in_specs.append(pl.BlockSpec((tm, tn), lambda i, j: (i, j)))
    if mode == "ple":
        r, p, wp = ple
        args += [r, p, wp]
        in_specs += [pl.BlockSpec((tm, tn), lambda i, j: (i, j)),
                     pl.BlockSpec((tm, p.shape[1]), lambda i, j: (i, 0)),
                     pl.BlockSpec((p.shape[1], tn), lambda i, j: (0, j))]
    out_shape = [jax.ShapeDtypeStruct((M, N), out_dtype)]
    out_specs = [pl.BlockSpec((tm, tn), lambda i, j: (i, j))]
    if emit_xn:
        chunks = xs[0][2] // LANES
        out_shape.append(jax.ShapeDtypeStruct((M * chunks, LANES), F32))
        out_specs.append(pl.BlockSpec((tm * chunks, LANES), lambda i, j: (i, 0)))
    scratch = [pltpu.VMEM((tm, xs[0][2]), BF16)] if has_gain else []
    res = pl.pallas_call(
        functools.partial(_mm_body, nx=nx, has_gain=has_gain, emit_xn=emit_xn, mode=mode),
        out_shape=out_shape, grid=(M // tm, N // tn), in_specs=in_specs, out_specs=out_specs,
        scratch_shapes=scratch, compiler_params=_params("arbitrary", "arbitrary"), name=name,
    )(*args)
    return res if emit_xn else res[0]


def _rope_tables(positions):
    def tables(dim):
        inv = 1.0 / (ROPE_THETA ** (jnp.arange(0, dim, 2, dtype=F32) / dim))
        ang = positions.astype(F32)[..., None] * inv
        return jnp.cos(ang), jnp.sin(ang)

    lead = positions.shape
    c, s = tables(HEAD_DIM)
    c128 = jnp.concatenate([c, c], -1)
    s128 = jnp.concatenate([-s, s], -1)
    c, s = tables(B_ROPE)
    z = jnp.zeros_like(c)
    c64 = jnp.concatenate([c, c, z, z], -1)
    s64a = jnp.concatenate([-s, z, z, z], -1)
    s64b = jnp.concatenate([z, s, z, z], -1)
    n = int(np.prod(lead))
    return [t.reshape(n, LANES) for t in (c128, s128, c64, s64a, s64b)]


def _rot128(x, c, s):
    return x * c + pltpu.roll(x, 64, 1) * s


def _rot64(x, c, sa, sb):
    return x * c + pltpu.roll(x, 96, 1) * sa + pltpu.roll(x, 32, 1) * sb


def _prep_body(y_ref, c128, s128, c64, s64a, s64b, *o_refs, program):
    for src, mode, scale, oi, dst in program:
        x = y_ref[:, src * LANES:(src + 1) * LANES].astype(F32)
        if mode == "rot128":
            x = _rot128(x, c128[...], s128[...])
        elif mode == "rot64":
            x = _rot64(x, c64[...], s64a[...], s64b[...])
        if scale != 1.0:
            x = x * scale
        o_refs[oi][:, dst * LANES:(dst + 1) * LANES] = x.astype(o_refs[oi].dtype)


def _prep(y, tables, program, outs, *, name, tm=256):
    M, C = y.shape
    tm = _tile(M, tm)
    row = lambda i: (i, 0)
    return pl.pallas_call(
        functools.partial(_prep_body, program=tuple(program)),
        out_shape=[jax.ShapeDtypeStruct((M, nb * LANES), dt) for nb, dt in outs],
        grid=(M // tm,),
        in_specs=[pl.BlockSpec((tm, C), row)] + [pl.BlockSpec((tm, LANES), row)] * 5,
        out_specs=[pl.BlockSpec((tm, nb * LANES), row) for nb, _ in outs],
        compiler_params=_params("arbitrary"), name=name,
    )(y, *tables)


def _pair_tables(S, tq, tk, window):
    qi_l, ki_l, fl_l, mk_l, masks, ids = [], [], [], [], [], {}
    for qi in range(S // tq):
        q0, q1 = qi * tq, (qi + 1) * tq - 1
        ks = []
        for ki in range(S // tk):
            k0, k1 = ki * tk, (ki + 1) * tk - 1
            if k0 > q1 or (window is not None and q0 - k1 >= window):
                continue
            full = k1 <= q0 and (window is None or q1 - k0 < window)
            mid = 0
            if not full:
                delta = q0 - k0
                if delta not in ids:
                    rel = np.arange(tq)[:, None] + delta - np.arange(tk)[None, :]
                    ok = (rel >= 0) if window is None else ((rel >= 0) & (rel < window))
                    masks.append(np.where(ok, 0.0, NEG_INF).astype(np.float32))
                    ids[delta] = len(masks)
                mid = ids[delta]
            ks.append((ki, mid))
        for n, (ki, mid) in enumerate(ks):
            qi_l.append(qi)
            ki_l.append(ki)
            fl_l.append((1 if n == 0 else 0) | (2 if n == len(ks) - 1 else 0))
            mk_l.append(mid)
    if not masks:
        masks.append(np.zeros((tq, tk), np.float32))
    tabs = [jnp.asarray(np.array(a, np.int32)) for a in (qi_l, ki_l, fl_l, mk_l)]
    return tabs, jnp.asarray(np.stack(masks))


def _flash_body(qi_t, ki_t, fl_t, mk_t, *refs, nq, nkv, q_src, k_src, v_src, tq, has_sink, has_bias,
                has_sel):
    nc = len(q_src)
    it = iter(refs)
    q_refs = [next(it) for _ in range(nq)]
    kv_refs = [next(it) for _ in range(nkv)]
    mask_ref = next(it)
    sink_ref = next(it) if has_sink else None
    cum_ref = next(it) if has_bias else None
    cumt_ref = next(it) if has_bias else None
    sel_ref = next(it) if has_sel else None
    exp_ref = next(it) if has_sel else None
    o_ref = next(it)
    m_sc, l_sc, acc_sc = next(it), next(it), next(it)
    cq_sc = next(it) if has_bias else None

    step_id = pl.program_id(2)
    fl, mk = fl_t[step_id], mk_t[step_id]

    def cols(ref, off):
        return ref[0, :, off:off + LANES]

    def cat(refs_, src):
        xs = [cols(refs_[pi], off) for pi, off in src]
        return xs[0] if len(xs) == 1 else jnp.concatenate(xs, axis=-1)

    @pl.when((fl & 1) != 0)
    def _init():
        m_sc[...] = jnp.full(m_sc.shape, NEG_INF, F32)
        l_sc[...] = jnp.zeros(l_sc.shape, F32)
        acc_sc[...] = jnp.zeros(acc_sc.shape, F32)
        if has_bias:
            lane = lax.broadcasted_iota(jnp.int32, (tq, LANES), 1)
            for g in range(nc):
                head = pl.program_id(1) * nc + g
                cq_sc[g] = jnp.sum(jnp.where(lane == head, cum_ref[0], 0.0), axis=-1, keepdims=True)

    def step(masked):
        add = mask_ref[mk - 1] if masked else None
        if has_sel:
            hidden = (jnp.dot(sel_ref[0, 0], exp_ref[0], preferred_element_type=F32) - 1.0) * (-NEG_INF)
            add = hidden if add is None else add + hidden
        for g in range(nc):
            q = cat(q_refs, q_src[g])
            k = cat(kv_refs, k_src[g])
            v = cols(kv_refs[v_src[g][0]], v_src[g][1])
            s = lax.dot_general(q, k, (((1,), (1,)), ((), ())), preferred_element_type=F32)
            if has_bias:
                s = s + (cq_sc[g] - cumt_ref[0, g])
            if add is not None:
                s = s + add
            m_prev = m_sc[g]
            m_new = jnp.maximum(m_prev, jnp.max(s, axis=-1, keepdims=True))
            p = jnp.exp(s - m_new)
            alpha = jnp.exp(m_prev - m_new)
            l_sc[g] = alpha * l_sc[g] + jnp.sum(p, axis=-1, keepdims=True)
            acc_sc[g] = alpha * acc_sc[g] + jnp.dot(p.astype(BF16), v, preferred_element_type=F32)
            m_sc[g] = m_new

    @pl.when(mk != 0)
    def _():
        step(True)

    @pl.when(mk == 0)
    def _():
        step(False)

    @pl.when((fl & 2) != 0)
    def _finish():
        for g in range(nc):
            m, l, acc = m_sc[g], l_sc[g], acc_sc[g]
            if has_sink:
                sk = sink_ref[0, g:g + 1, 0:1]
                m_f = jnp.maximum(m, sk)
                w = jnp.exp(m - m_f)
                l = l * w + jnp.exp(sk - m_f)
                acc = acc * w
            o_ref[0, :, g * LANES:(g + 1) * LANES] = (acc / l).astype(o_ref.dtype)


def _flash(q_parts, kv_parts, q_src, k_src, v_src, *, name, B, S, n_steps, tq, tk, window=None, sinks=None,
           cum=None, cumt=None, sel=None, out_dtype=BF16):
    nc = len(q_src)
    tq, tk = _tile(S, tq), _tile(S, tk)
    tabs, masks = _pair_tables(S, tq, tk, window)
    npairs = int(tabs[0].shape[0])
    args, in_specs = [], []
    for arr, width, cf in q_parts:
        args.append(arr)
        in_specs.append(pl.BlockSpec((1, tq, width), lambda b, h, s, qt, kt, ft, mt, cf=cf: (b, qt[s], cf(h))))
    for arr, width, cf in kv_parts:
        args.append(arr)
        in_specs.append(pl.BlockSpec((1, tk, width), lambda b, h, s, qt, kt, ft, mt, cf=cf: (b, kt[s], cf(h))))
    args.append(masks)
    in_specs.append(pl.BlockSpec(masks.shape, lambda b, h, s, qt, kt, ft, mt: (0, 0, 0)))
    if sinks is not None:
        args.append(jnp.broadcast_to(sinks.astype(F32).reshape(n_steps, nc, 1), (n_steps, nc, LANES)))
        in_specs.append(pl.BlockSpec((1, nc, LANES), lambda b, h, s, qt, kt, ft, mt: (h, 0, 0)))
    if cum is not None:
        args += [cum, cumt]
        in_specs += [pl.BlockSpec((1, tq, LANES), lambda b, h, s, qt, kt, ft, mt: (b, qt[s], 0)),
                     pl.BlockSpec((1, nc, 1, tk), lambda b, h, s, qt, kt, ft, mt: (b, h, 0, kt[s]))]
    if sel is not None:
        per = tk // D_SLC_LEN
        e = np.zeros((S // tk, LANES, tk), np.float32)
        for ki in range(S // tk):
            e[ki, ki * per + np.arange(tk) // D_SLC_LEN, np.arange(tk)] = 1.0
        args += [sel, jnp.asarray(e, BF16)]
        in_specs += [pl.BlockSpec((1, 1, tq, LANES), lambda b, h, s, qt, kt, ft, mt: (b, h, qt[s], 0)),
                     pl.BlockSpec((1, LANES, tk), lambda b, h, s, qt, kt, ft, mt: (kt[s], 0, 0))]
    scratch = [pltpu.VMEM((nc, tq, 1), F32), pltpu.VMEM((nc, tq, 1), F32), pltpu.VMEM((nc, tq, LANES), F32)]
    if cum is not None:
        scratch.append(pltpu.VMEM((nc, tq, 1), F32))
    body = functools.partial(_flash_body, nq=len(q_parts), nkv=len(kv_parts), q_src=q_src, k_src=k_src,
                             v_src=v_src, tq=tq, has_sink=sinks is not None, has_bias=cum is not None,
                             has_sel=sel is not None)
    return pl.pallas_call(
        body, out_shape=jax.ShapeDtypeStruct((B, S, n_steps * nc * LANES), out_dtype),
        grid_spec=pltpu.PrefetchScalarGridSpec(
            num_scalar_prefetch=4, grid=(B, n_steps, npairs), in_specs=in_specs,
            out_specs=pl.BlockSpec((1, tq, nc * LANES), lambda b, h, s, qt, kt, ft, mt: (b, qt[s], h)),
            scratch_shapes=scratch),
        compiler_params=_params("arbitrary", "arbitrary", "arbitrary"), name=name,
    )(*tabs, *args)


def _gqa_flash(q, k, v, *, name, B, S, Hk, G, **kw):
    head = lambda h: h
    return _flash([(q, G * LANES, head)], [(k, LANES, head), (v, LANES, head)],
                  [[(0, g * LANES)] for g in range(G)], [[(0, 0)]] * G, [(1, 0)] * G,
                  name=name, B=B, S=S, n_steps=Hk, **kw)


def _cum_body(y_ref, b_ref, tri_ref, o_ref, carry):
    @pl.when(pl.program_id(1) == 0)
    def _():
        carry[...] = jnp.zeros(carry.shape, F32)
    x = y_ref[0] + b_ref[...]
    logf = jnp.minimum(x, 0.0) - jnp.log1p(jnp.exp(-jnp.abs(x)))
    cum = jnp.dot(tri_ref[...], logf, preferred_element_type=F32, precision=lax.Precision.HIGHEST) + carry[...]
    o_ref[0] = cum
    carry[...] = cum[-1:, :]


def _forget_cum(y3, col_block, bias_row, *, ts=512):
    B, S, _ = y3.shape
    ts = _tile(S, ts)
    tri = jnp.asarray(np.tril(np.ones((ts, ts), np.float32)))
    return pl.pallas_call(
        _cum_body, out_shape=jax.ShapeDtypeStruct((B, S, LANES), F32), grid=(B, S // ts),
        in_specs=[pl.BlockSpec((1, ts, LANES), lambda b, s: (b, s, col_block)),
                  pl.BlockSpec((1, LANES), lambda b, s: (0, 0)),
                  pl.BlockSpec((ts, ts), lambda b, s: (0, 0))],
        out_specs=pl.BlockSpec((1, ts, LANES), lambda b, s: (b, s, 0)),
        scratch_shapes=[pltpu.VMEM((1, LANES), F32)],
        compiler_params=_params("arbitrary", "arbitrary"), name="forget_cum",
    )(y3, bias_row, tri)


def _compress_body(*refs, rope, nc):
    if rope:
        z_ref, pe_ref, w1_ref, w2_ref, c_ref, s_ref, o_ref = refs
    else:
        z_ref, pe_ref, w1_ref, w2_ref, o_ref = refs
    half = D_CMP_LEN // 2
    width = D_KV_HEADS * HEAD_DIM
    for hk in range(D_KV_HEADS):
        u = jnp.zeros((nc, w1_ref.shape[1]), F32)
        v = jnp.zeros((nc, w1_ref.shape[1]), F32)
        for l in range(half):
            z = z_ref[0, :, l * width + hk * HEAD_DIM:l * width + (hk + 1) * HEAD_DIM]
            zu = (z + pe_ref[l:l + 1, :]).astype(BF16)
            zv = (z + pe_ref[half + l:half + l + 1, :]).astype(BF16)
            u = u + jnp.dot(zu, w1_ref[l * HEAD_DIM:(l + 1) * HEAD_DIM, :], preferred_element_type=F32)
            v = v + jnp.dot(zv, w1_ref[(half + l) * HEAD_DIM:(half + l + 1) * HEAD_DIM, :],
                            preferred_element_type=F32)
        pre = u + pltpu.roll(v, nc - 1, 0)
        hid = jax.nn.gelu(pre, approximate=True)
        out = jnp.dot(hid.astype(BF16), w2_ref[...], preferred_element_type=F32)
        if rope:
            out = _rot128(out, c_ref[0], s_ref[0])
        o_ref[0, hk] = out.astype(o_ref.dtype)


def _compress(z, pe, w1, w2, rope_tabs=None):
    B, S, W = z.shape
    nc = S // D_CMP_STRIDE
    zc = z.reshape(B, nc, D_CMP_STRIDE * W)
    args = [zc, pe.astype(F32), w1.astype(BF16), w2.astype(BF16)]
    in_specs = [pl.BlockSpec((1, nc, D_CMP_STRIDE * W), lambda b: (b, 0, 0)),
                pl.BlockSpec(pe.shape, lambda b: (0, 0)),
                pl.BlockSpec(w1.shape, lambda b: (0, 0)),
                pl.BlockSpec(w2.shape, lambda b: (0, 0))]
    if rope_tabs is not None:
        args += list(rope_tabs)
        in_specs += [pl.BlockSpec((1, nc, LANES), lambda b: (b, 0, 0))] * 2
    return pl.pallas_call(
        functools.partial(_compress_body, rope=rope_tabs is not None, nc=nc),
        out_shape=jax.ShapeDtypeStruct((B, D_KV_HEADS, nc, HEAD_DIM), BF16), grid=(B,),
        in_specs=in_specs, out_specs=pl.BlockSpec((1, D_KV_HEADS, nc, HEAD_DIM), lambda b: (b, 0, 0, 0)),
        compiler_params=_params("arbitrary"), name="nsa_compress",
    )(*args)


def _cmp_attn_body(q_ref, k_ref, v_ref, ov_ref, o_ref, sel_ref, *, G, tq, nc, n_cmp, n_slc, topn):
    qi = pl.program_id(2)
    t = qi * tq + lax.broadcasted_iota(jnp.int32, (tq, nc), 0)
    c = lax.broadcasted_iota(jnp.int32, (tq, nc), 1)
    valid = (c * D_CMP_STRIDE + (D_CMP_LEN - 1) <= t) & (c < n_cmp)
    k = k_ref[0, 0]
    v = v_ref[0, 0]
    psum = jnp.zeros((tq, nc), F32)
    for g in range(G):
        q = q_ref[0, :, g * LANES:(g + 1) * LANES]
        s = lax.dot_general(q, k, (((1,), (1,)), ((), ())), preferred_element_type=F32)
        s = jnp.where(valid, s, NEG_INF)
        e = jnp.where(valid, jnp.exp(s - jnp.max(s, axis=-1, keepdims=True)), 0.0)
        p = e / jnp.maximum(jnp.sum(e, axis=-1, keepdims=True), jnp.finfo(F32).tiny)
        o_ref[0, :, g * LANES:(g + 1) * LANES] = jnp.dot(
            p.astype(BF16), v, preferred_element_type=F32).astype(o_ref.dtype)
        psum = psum + p
    imp = jnp.dot(psum, ov_ref[...], preferred_element_type=F32, precision=lax.Precision.HIGHEST)
    lane = lax.broadcasted_iota(jnp.int32, (tq, LANES), 1)
    trow = qi * tq + lax.broadcasted_iota(jnp.int32, (tq, LANES), 0)
    cur = jnp.right_shift(trow, int(math.log2(D_SLC_LEN)))
    forced = (lane == 0) | (lane == cur) | (lane == cur - 1)
    imp = jnp.where(lane * D_SLC_LEN > trow, NEG_INF, imp + jnp.where(forced, FORCE_BONUS, 0.0))
    imp = jnp.where(lane >= n_slc, TAKEN, imp)
    chosen = jnp.zeros((tq, LANES), F32)
    for _ in range(topn):
        mx = jnp.max(imp, axis=-1, keepdims=True)
        idx = jnp.min(jnp.where(imp == mx, lane, LANES), axis=-1, keepdims=True)
        hit = lane == idx
        chosen = jnp.where(hit, 1.0, chosen)
        imp = jnp.where(hit, TAKEN, imp)
    sel_ref[0, 0] = chosen.astype(sel_ref.dtype)


def _cmp_attn(q, k_cmp, v_cmp, *, tq=256):
    B, S, _ = q.shape
    Hk, G = D_KV_HEADS, D_HEADS // D_KV_HEADS
    nc = S // D_CMP_STRIDE
    n_cmp = (S - D_CMP_LEN) // D_CMP_STRIDE + 1
    n_slc = S // D_SLC_LEN
    tq = _tile(S, tq)
    c0 = np.arange(nc) * D_CMP_STRIDE
    s0 = np.arange(LANES) * D_SLC_LEN
    ov = ((c0[:, None] < (s0 + D_SLC_LEN)[None, :]) & ((c0 + D_CMP_LEN)[:, None] > s0[None, :])
          & (np.arange(nc) < n_cmp)[:, None] & (np.arange(LANES) < n_slc)[None, :]).astype(np.float32)
    body = functools.partial(_cmp_attn_body, G=G, tq=tq, nc=nc, n_cmp=n_cmp, n_slc=n_slc,
                             topn=min(D_SLC_TOPN, n_slc))
    return pl.pallas_call(
        body,
        out_shape=[jax.ShapeDtypeStruct((B, S, Hk * G * LANES), F32),
                   jax.ShapeDtypeStruct((B, Hk, S, LANES), BF16)],
        grid=(B, Hk, S // tq),
        in_specs=[pl.BlockSpec((1, tq, G * LANES), lambda b, h, i: (b, i, h)),
                  pl.BlockSpec((1, 1, nc, LANES), lambda b, h, i: (b, h, 0, 0)),
                  pl.BlockSpec((1, 1, nc, LANES), lambda b, h, i: (b, h, 0, 0)),
                  pl.BlockSpec((nc, LANES), lambda b, h, i: (0, 0))],
        out_specs=[pl.BlockSpec((1, tq, G * LANES), lambda b, h, i: (b, i, h)),
                   pl.BlockSpec((1, 1, tq, LANES), lambda b, h, i: (b, h, i, 0))],
        compiler_params=_params("arbitrary", "arbitrary", "arbitrary"), name="nsa_cmp_attn",
    )(q, k_cmp, v_cmp, jnp.asarray(ov))


def _gate_body(y_ref, a_ref, b_ref, c_ref, o_ref, *, lane0):
    g = jax.nn.sigmoid(y_ref[...])
    for h in range(D_HEADS):
        cols = slice(h * LANES, (h + 1) * LANES)
        ga = g[:, lane0 + h:lane0 + h + 1]
        gb = g[:, lane0 + D_HEADS + h:lane0 + D_HEADS + h + 1]
        gc = g[:, lane0 + 2 * D_HEADS + h:lane0 + 2 * D_HEADS + h + 1]
        o_ref[:, cols] = (ga * a_ref[:, cols] + gb * b_ref[:, cols] + gc * c_ref[:, cols]).astype(o_ref.dtype)


def _nsa_gate(y, col_block, lane0, o_cmp, o_slc, o_win, *, tm=512):
    M = y.shape[0]
    W = o_cmp.shape[1]
    tm = _tile(M, tm)
    row = lambda i: (i, 0)
    return pl.pallas_call(
        functools.partial(_gate_body, lane0=lane0), out_shape=jax.ShapeDtypeStruct((M, W), BF16),
        grid=(M // tm,),
        in_specs=[pl.BlockSpec((tm, LANES), lambda i: (i, col_block))] + [pl.BlockSpec((tm, W), row)] * 3,
        out_specs=pl.BlockSpec((tm, W), row), compiler_params=_params("arbitrary"), name="nsa_gate",
    )(y, o_cmp, o_slc, o_win)


def _route_body(lg_ref, b_ref, tri_ref, e_ref, w_ref, pos_ref, cnt_ref, carry, *, tm):
    @pl.when(pl.program_id(0) == 0)
    def _():
        carry[...] = jnp.zeros(carry.shape, F32)
    lane = lax.broadcasted_iota(jnp.int32, (tm, LANES), 1)
    logits = lg_ref[...] + b_ref[...]
    gl = jnp.where(lane < N_GROUPS, logits, -jnp.inf)
    gmax = jnp.max(gl, axis=-1, keepdims=True)
    g_val = 1.0 / jnp.sum(jnp.exp(gl - gmax), axis=-1, keepdims=True)
    g_idx = jnp.min(jnp.where(gl == gmax, lane, LANES), axis=-1, keepdims=True)
    lo = N_GROUPS + EXPERTS_PER_GROUP * g_idx
    el = jnp.where((lane >= lo) & (lane < lo + EXPERTS_PER_GROUP), logits, -jnp.inf)
    e1 = jnp.max(el, axis=-1, keepdims=True)
    i1 = jnp.min(jnp.where(el == e1, lane, LANES), axis=-1, keepdims=True)
    el2 = jnp.where(lane == i1, -jnp.inf, el)
    e2 = jnp.max(el2, axis=-1, keepdims=True)
    i2 = jnp.min(jnp.where(el2 == e2, lane, LANES), axis=-1, keepdims=True)
    r = jnp.exp(e2 - e1)
    w1 = g_val / (1.0 + r)
    w2 = w1 * r
    x1, x2 = i1 - N_GROUPS, i2 - N_GROUPS
    e_ref[...] = jnp.where(lane == 0, x1, jnp.where(lane == 1, x2, 0))
    w_ref[...] = jnp.where(lane == 0, w1, jnp.where(lane == 1, w2, 0.0))
    hot1 = lane == x1
    hot2 = lane == x2
    both = jnp.where(hot1 | hot2, 1.0, 0.0)
    before = jnp.dot(tri_ref[...], both.astype(BF16), preferred_element_type=F32) + carry[...]
    p1 = jnp.sum(jnp.where(hot1, before, 0.0), axis=-1, keepdims=True)
    p2 = jnp.sum(jnp.where(hot2, before, 0.0), axis=-1, keepdims=True)
    pos_ref[...] = jnp.where(lane == 0, p1, jnp.where(lane == 1, p2, 0.0)).astype(jnp.int32)
    carry[...] = carry[...] + jnp.sum(both, axis=0, keepdims=True)
    cnt_ref[...] = carry[...].astype(jnp.int32)


def _route(logits, bias_row, *, tm=512):
    T = logits.shape[0]
    tm = _tile(T, tm)
    tri = jnp.asarray(np.tril(np.ones((tm, tm), np.float32), -1), BF16)
    row = lambda i: (i, 0)
    fixed = lambda i: (0, 0)
    return pl.pallas_call(
        functools.partial(_route_body, tm=tm),
        out_shape=[jax.ShapeDtypeStruct((T, LANES), jnp.int32), jax.ShapeDtypeStruct((T, LANES), F32),
                   jax.ShapeDtypeStruct((T, LANES), jnp.int32), jax.ShapeDtypeStruct((1, LANES), jnp.int32)],
        grid=(T // tm,),
        in_specs=[pl.BlockSpec((tm, LANES), row), pl.BlockSpec((1, LANES), fixed), pl.BlockSpec((tm, tm), fixed)],
        out_specs=[pl.BlockSpec((tm, LANES), row)] * 3 + [pl.BlockSpec((1, LANES), fixed)],
        scratch_shapes=[pltpu.VMEM((1, LANES), F32)],
        compiler_params=_params("arbitrary"), name="moe_route",
    )(logits, bias_row, tri)


def _row_copy_body(*refs, n, chunks, has_init):
    if has_init:
        si_ref, di_ref, src_ref, _, out_ref, sem = refs
    else:
        si_ref, di_ref, src_ref, out_ref, sem = refs
    window = min(DMA_WINDOW, n)

    def copy(s_row, d_row):
        return pltpu.make_async_copy(src_ref.at[pl.ds(pl.multiple_of(s_row * chunks, chunks), chunks)],
                                     out_ref.at[pl.ds(pl.multiple_of(d_row * chunks, chunks), chunks)], sem)

    def issue(i, c):
        copy(si_ref[i], di_ref[i]).start()

        @pl.when(i >= window)
        def _():
            copy(0, 0).wait()
        return c

    lax.fori_loop(0, n, issue, 0)

    def drain(i, c):
        copy(0, 0).wait()
        return c

    lax.fori_loop(0, window, drain, 0)


def _row_copy(src, src_idx, dst_idx, n_out, chunks, *, name, init=None):
    n = int(src_idx.shape[0])
    args = [src_idx, dst_idx, src]
    in_specs = [pl.BlockSpec(memory_space=pl.ANY)]
    aliases = {}
    if init is not None:
        args.append(init)
        in_specs.append(pl.BlockSpec(memory_space=pl.ANY))
        aliases = {3: 0}
    return pl.pallas_call(
        functools.partial(_row_copy_body, n=n, chunks=chunks, has_init=init is not None),
        out_shape=jax.ShapeDtypeStruct((n_out * chunks, LANES), src.dtype),
        grid_spec=pltpu.PrefetchScalarGridSpec(
            num_scalar_prefetch=2, grid=(1,), in_specs=in_specs,
            out_specs=pl.BlockSpec(memory_space=pl.ANY),
            scratch_shapes=[pltpu.SemaphoreType.DMA(())]),
        input_output_aliases=aliases,
        compiler_params=pltpu.CompilerParams(dimension_semantics=("arbitrary",), has_side_effects=True),
        name=name,
    )(*args)


def _expert_body(be_ref, nu_ref, x_ref, wg_ref, wu_ref, wd_ref, o_ref, x_sc, wg_sc, wu_sc, wd_sc, *, chunks):
    i = pl.program_id(0)
    fresh = (i == 0) | (be_ref[i] != be_ref[jnp.maximum(i - 1, 0)])

    @pl.when(fresh)
    def _():
        wg_sc[...] = wg_ref[0].astype(BF16)
        wu_sc[...] = wu_ref[0].astype(BF16)
        wd_sc[...] = wd_ref[0].astype(BF16)

    @pl.when(i < nu_ref[0])
    def _():
        for c in range(chunks):
            x_sc[:, c * LANES:(c + 1) * LANES] = x_ref[pl.ds(c, MOE_ROWS, stride=chunks), :].astype(BF16)
        x = x_sc[...]
        gate = jnp.dot(x, wg_sc[...], preferred_element_type=F32)
        up = jnp.dot(x, wu_sc[...], preferred_element_type=F32)
        hid = (gate * jax.nn.sigmoid(gate) * up).astype(BF16)
        y = jnp.dot(hid, wd_sc[...], preferred_element_type=F32)
        for c in range(chunks):
            o_ref[pl.ds(c, MOE_ROWS, stride=chunks), :] = y[:, c * LANES:(c + 1) * LANES]

    @pl.when(i >= nu_ref[0])
    def _():
        o_ref[...] = jnp.zeros(o_ref.shape, F32)


def _experts(xb, blk_e, n_used, w_gate, w_up, w_down):
    D, Hd = w_gate.shape[1], w_gate.shape[2]
    chunks = D // LANES
    nblk = xb.shape[0] // (MOE_ROWS * chunks)
    return pl.pallas_call(
        functools.partial(_expert_body, chunks=chunks), out_shape=jax.ShapeDtypeStruct(xb.shape, F32),
        grid_spec=pltpu.PrefetchScalarGridSpec(
            num_scalar_prefetch=2, grid=(nblk,),
            in_specs=[pl.BlockSpec((MOE_ROWS * chunks, LANES), lambda i, be, nu: (i, 0)),
                      pl.BlockSpec((1, D, Hd), lambda i, be, nu: (be[i], 0, 0)),
                      pl.BlockSpec((1, D, Hd), lambda i, be, nu: (be[i], 0, 0)),
                      pl.BlockSpec((1, Hd, D), lambda i, be, nu: (be[i], 0, 0))],
            out_specs=pl.BlockSpec((MOE_ROWS * chunks, LANES), lambda i, be, nu: (i, 0)),
            scratch_shapes=[pltpu.VMEM((MOE_ROWS, D), BF16), pltpu.VMEM((D, Hd), BF16),
                            pltpu.VMEM((D, Hd), BF16), pltpu.VMEM((Hd, D), BF16)]),
        compiler_params=_params("arbitrary"), name="moe_experts",
    )(blk_e, n_used, xb, w_gate, w_up, w_down)


def _moe_sum_body(h_ref, ya_ref, yb_ref, w_ref, o_ref, *, chunks):
    w = w_ref[...]
    rows = h_ref.shape[0]
    for c in range(chunks):
        cols = slice(c * LANES, (c + 1) * LANES)
        o_ref[:, cols] = (h_ref[:, cols] + w[:, 0:1] * ya_ref[pl.ds(c, rows, stride=chunks), :]
                          + w[:, 1:2] * yb_ref[pl.ds(c, rows, stride=chunks), :])


def _moe_sum(h, y2, wts, *, tm=256):
    T, D = h.shape
    chunks = D // LANES
    tm = _tile(T, tm)
    row = lambda i: (i, 0)
    return pl.pallas_call(
        functools.partial(_moe_sum_body, chunks=chunks), out_shape=jax.ShapeDtypeStruct((T, D), F32),
        grid=(T // tm,),
        in_specs=[pl.BlockSpec((tm, D), row), pl.BlockSpec((tm * chunks, LANES), row),
                  pl.BlockSpec((tm * chunks, LANES), lambda i: (i + T // tm, 0)), pl.BlockSpec((tm, LANES), row)],
        out_specs=pl.BlockSpec((tm, D), row), compiler_params=_params("arbitrary"), name="moe_sum",
    )(h, y2, y2, wts)


def _moe(h, norm_g, w_group, b_group, w_expert, b_expert, w_gate, w_up, w_down):
    T, D = h.shape
    chunks = D // LANES
    pad = LANES - N_GROUPS - N_EXPERTS
    w_r = jnp.concatenate([w_group, w_expert, jnp.zeros((D, pad), F32)], axis=1).astype(BF16)
    b_r = jnp.concatenate([b_group, b_expert, jnp.zeros((pad,), F32)]).astype(F32).reshape(1, LANES)
    logits, xn = _mm([(h, 0, D)], [w_r], gain=norm_g, emit_xn=True, tm=512, tn=LANES, name="moe_router")
    eid, wts, pos, cnt = _route(logits, b_r)
    counts = cnt[0, :N_EXPERTS]
    padded = (counts + MOE_ROWS - 1) // MOE_ROWS * MOE_ROWS
    ends = jnp.cumsum(padded)
    offs = ends - padded
    dest = (offs[eid[:, :TOP_K]] + pos[:, :TOP_K]).reshape(T * TOP_K).astype(jnp.int32)
    P = T * TOP_K + N_EXPERTS * MOE_ROWS
    nblk = P // MOE_ROWS
    starts = jnp.arange(nblk, dtype=jnp.int32) * MOE_ROWS
    blk_e = jnp.minimum(jnp.sum(ends[None, :] <= starts[:, None], axis=1), N_EXPERTS - 1).astype(jnp.int32)
    n_used = (ends[-1:] // MOE_ROWS).astype(jnp.int32)
    flat = jnp.arange(T * TOP_K, dtype=jnp.int32)
    xb = _row_copy(xn, flat // TOP_K, dest, P, chunks, init=jnp.zeros((P * chunks, LANES), F32),
                   name="moe_dispatch")
    yb = _experts(xb, blk_e, n_used, w_gate, w_up, w_down)
    y2 = _row_copy(yb, dest, (flat % TOP_K) * T + flat // TOP_K, T * TOP_K, chunks, name="moe_collect")
    return _moe_sum(h, y2, wts)


def _pad_cols(w, n):
    return jnp.pad(w, ((0, 0), (0, n - w.shape[1])))


def _layer_even(h, B, S, tabs, norm_g, w_in, sinks, q_lat_norm, kv_lat_norm, w_uq, w_ukv, w_o):
    T, D = h.shape
    n_in = 2560
    y = _mm([(h, 0, D)], [_pad_cols(w_in, n_in).astype(BF16)], gain=norm_g, tn=512, name="ab_in")
    sa = HEAD_DIM ** -0.5
    prog = ([(i, "rot128", sa, 0, i) for i in range(8)] + [(8 + i, "rot128", 1.0, 1, i) for i in range(2)]
            + [(10 + i, "copy", 1.0, 2, i) for i in range(2)] + [(18, "rot64", 1.0, 3, 0)])
    q_a, k_a, v_a, k_r = _prep(y, tabs, prog, [(8, BF16), (2, BF16), (2, BF16), (1, BF16)], name="ab_prep")
    wq = w_uq.reshape(B_Q_LORA, B_HEADS, B_NOPE + B_ROPE)
    wq_n = wq[:, :, :B_NOPE].reshape(B_Q_LORA, B_HEADS * B_NOPE)
    wq_r = jnp.pad(wq[:, :, B_NOPE:], ((0, 0), (0, 0), (0, LANES - B_ROPE))).reshape(B_Q_LORA, B_HEADS * LANES)
    qb = _mm([(y, 1536 // B_Q_LORA, B_Q_LORA)], [jnp.concatenate([wq_n, wq_r], 1).astype(BF16)],
             gain=q_lat_norm, tn=512, name="mla_uq")
    sb = (B_NOPE + B_ROPE) ** -0.5
    prog = [(i, "copy", sb, 0, i) for i in range(8)] + [(8 + i, "rot64", sb, 1, i) for i in range(8)]
    q_n, q_r = _prep(qb, tabs, prog, [(8, BF16), (8, BF16)], name="mla_prep")
    kv = _mm([(y, 2048 // B_KV_LORA, B_KV_LORA)], [w_ukv.astype(BF16)], gain=kv_lat_norm, out_dtype=BF16,
             tn=512, name="mla_ukv")
    r3 = lambda a: a.reshape(B, S, a.shape[-1])
    o_a = _gqa_flash(r3(q_a), r3(k_a), r3(v_a), name="swa_attn", B=B, S=S, Hk=A_KV_HEADS,
                     G=A_HEADS // A_KV_HEADS, window=A_WINDOW, sinks=sinks, tq=256, tk=256)
    hp = 2
    head = lambda h: h
    o_b = _flash([(r3(q_n), hp * LANES, head), (r3(q_r), hp * LANES, head)],
                 [(r3(kv), hp * 2 * LANES, head), (r3(k_r), LANES, lambda h: 0)],
                 [[(0, g * LANES), (1, g * LANES)] for g in range(hp)],
                 [[(0, 2 * g * LANES), (1, 0)] for g in range(hp)], [(0, (2 * g + 1) * LANES) for g in range(hp)],
                 name="mla_attn", B=B, S=S, n_steps=B_HEADS // hp, tq=512, tk=512)
    wo = w_o.astype(BF16)
    na = A_HEADS * HEAD_DIM
    return _mm([(o_a.reshape(T, -1), 0, na), (o_b.reshape(T, -1), 0, B_HEADS * B_V)], [wo[:na], wo[na:]],
               resid=h, tn=512, name="ab_out")


def _layer_odd(h, B, S, tabs, cmp_tabs, norm_g, w_in, forget_bias, pe_k, w1_k, w2_k, pe_v, w1_v, w2_v, w_o):
    T, D = h.shape
    hc, kvw = C_HEADS * HEAD_DIM, D_KV_HEADS * HEAD_DIM
    o_cf = 3 * hc
    o_dq = o_cf + C_HEADS
    o_dg = o_dq + D_HEADS * HEAD_DIM + 6 * kvw
    w_r = jnp.concatenate([w_in[:, :o_cf], w_in[:, o_dq:o_dg], w_in[:, o_cf:o_dq], w_in[:, o_dg:]], axis=1)
    n_in = 5760
    y = _mm([(h, 0, D)], [_pad_cols(w_r, n_in).astype(BF16)], gain=norm_g, tn=640, name="cd_in")
    misc = (n_in // LANES) - 1
    sc = HEAD_DIM ** -0.5
    prog = ([(i, "copy", sc, 0, i) for i in range(8)] + [(8 + i, "copy", 1.0, 1, i) for i in range(8)]
            + [(16 + i, "copy", 1.0, 2, i) for i in range(8)] + [(24 + i, "rot128", sc, 3, i) for i in range(8)]
            + [(32 + i, "copy", 1.0, 4, i) for i in range(2)] + [(34 + i, "copy", 1.0, 5, i) for i in range(2)]
            + [(36 + i, "rot128", 1.0, 6, i) for i in range(2)] + [(38 + i, "copy", 1.0, 7, i) for i in range(2)]
            + [(40 + i, "rot128", 1.0, 8, i) for i in range(2)] + [(42 + i, "copy", 1.0, 9, i) for i in range(2)])
    outs = [(8, BF16)] * 4 + [(2, F32), (2, F32)] + [(2, BF16)] * 4
    q_c, k_c, v_c, q_d, z_kc, z_vc, k_s, v_s, k_w, v_w = _prep(y, tabs, prog, outs, name="cd_prep")
    r3 = lambda a: a.reshape(B, S, a.shape[-1])
    fb = jnp.pad(forget_bias.astype(F32), (0, LANES - C_HEADS)).reshape(1, LANES)
    cum = _forget_cum(r3(y), misc, fb)
    cumt = jnp.swapaxes(cum[:, :, :C_HEADS], 1, 2).reshape(B, C_HEADS, 1, S)
    hp = 2
    head = lambda h: h
    o_c = _flash([(r3(q_c), hp * LANES, head)], [(r3(k_c), hp * LANES, head), (r3(v_c), hp * LANES, head)],
                 [[(0, g * LANES)] for g in range(hp)], [[(0, g * LANES)] for g in range(hp)],
                 [(1, g * LANES) for g in range(hp)], name="fox_attn", B=B, S=S, n_steps=C_HEADS // hp,
                 tq=512, tk=512, cum=cum, cumt=cumt)
    G = D_HEADS // D_KV_HEADS
    k_cmp = _compress(r3(z_kc), pe_k, w1_k, w2_k, rope_tabs=cmp_tabs)
    v_cmp = _compress(r3(z_vc), pe_v, w1_v, w2_v)
    q3 = r3(q_d)
    o_cmp, sel = _cmp_attn(q3, k_cmp, v_cmp)
    o_slc = _gqa_flash(q3, r3(k_s), r3(v_s), name="nsa_slc_attn", B=B, S=S, Hk=D_KV_HEADS, G=G, sel=sel,
                       out_dtype=F32, tq=512, tk=512)
    o_win = _gqa_flash(q3, r3(k_w), r3(v_w), name="nsa_win_attn", B=B, S=S, Hk=D_KV_HEADS, G=G,
                       window=D_WINDOW, out_dtype=F32, tq=256, tk=256)
    o_d = _nsa_gate(y, misc, C_HEADS, o_cmp.reshape(T, -1), o_slc.reshape(T, -1), o_win.reshape(T, -1))
    wo = w_o.astype(BF16)
    return _mm([(o_c.reshape(T, -1), 0, hc), (o_d, 0, D_HEADS * HEAD_DIM)], [wo[:hc], wo[hc:]],
               resid=h, tn=512, name="cd_out")


def _final_norm_body(x_ref, g_ref, o_ref):
    x = x_ref[...]
    o_ref[...] = x * lax.rsqrt(jnp.mean(x * x, axis=-1, keepdims=True) + NORM_EPS) * g_ref[...]


def _final_norm(h, g, *, tm=512):
    T, D = h.shape
    tm = _tile(T, tm)
    return pl.pallas_call(
        _final_norm_body, out_shape=jax.ShapeDtypeStruct((T, D), F32), grid=(T // tm,),
        in_specs=[pl.BlockSpec((tm, D), lambda i: (i, 0)), pl.BlockSpec((1, D), lambda i: (0, 0))],
        out_specs=pl.BlockSpec((tm, D), lambda i: (i, 0)), compiler_params=_params("arbitrary"),
        name="final_norm",
    )(h, g.reshape(1, D).astype(F32))


def kernel(x, p, positions, ab_w_in, ab_sinks, ab_q_lat_norm, ab_kv_lat_norm, ab_w_uq, ab_w_ukv, ab_w_o,
           cd_w_in, cd_forget_bias, cd_cmp_pe_k, cd_cmp_w1_k, cd_cmp_w2_k, cd_cmp_pe_v, cd_cmp_w1_v,
           cd_cmp_w2_v, cd_w_o, mixer_norm, moe_norm, router_group_w, router_group_b, router_expert_w,
           router_expert_b, expert_w_gate, expert_w_up, expert_w_down, ple_proj, ple_gate_norm, ple_gate_w,
           final_norm):
    B, S, D = x.shape
    T = B * S
    depth = p.shape[0]
    tabs = _rope_tables(positions)
    nc = S // D_CMP_STRIDE
    end = np.minimum(np.arange(nc) * D_CMP_STRIDE + D_CMP_LEN - 1, S - 1)
    cmp_tabs = [t.reshape(B, nc, LANES) for t in _rope_tables(positions[:, end])[:2]]
    h = x.reshape(T, D)
    for i in range(depth):
        j = i // 2
        if i % 2 == 0:
            h = _layer_even(h, B, S, tabs, mixer_norm[i], ab_w_in[j], ab_sinks[j], ab_q_lat_norm[j],
                            ab_kv_lat_norm[j], ab_w_uq[j], ab_w_ukv[j], ab_w_o[j])
        else:
            h = _layer_odd(h, B, S, tabs, cmp_tabs, mixer_norm[i], cd_w_in[j], cd_forget_bias[j],
                           cd_cmp_pe_k[j], cd_cmp_w1_k[j], cd_cmp_w2_k[j], cd_cmp_pe_v[j], cd_cmp_w1_v[j],
                           cd_cmp_w2_v[j], cd_w_o[j])
        h = _moe(h, moe_norm[i], router_group_w[i], router_group_b[i], router_expert_w[i], router_expert_b[i],
                 expert_w_gate[i], expert_w_up[i], expert_w_down[i])
        h = _mm([(h, 0, D)], [ple_gate_w[i].astype(BF16)], gain=ple_gate_norm[i], tn=512,
                ple=(h, p[i].reshape(T, -1), ple_proj[i].astype(BF16)), name="ple")
    return _final_norm(h, final_norm).reshape(B, S, D)
```

```python
import functools
import math

import numpy as np
import jax
import jax.numpy as jnp
from jax import lax
from jax.experimental import pallas as pl
from jax.experimental.pallas import tpu as pltpu

F32 = jnp.float32
BF16 = jnp.bfloat16

HEAD_DIM = 128
ROPE_THETA = 10000.0
NORM_EPS = 1e-6
NEG_INF = -1e30
TAKEN = -3e38
A_HEADS, A_KV_HEADS, A_WINDOW = 8, 2, 128
B_HEADS, B_Q_LORA, B_KV_LORA, B_NOPE, B_ROPE, B_V = 8, 512, 256, 128, 64, 128
C_HEADS = 8
D_HEADS, D_KV_HEADS = 8, 2
D_CMP_LEN, D_CMP_STRIDE, D_SLC_LEN, D_SLC_TOPN, D_WINDOW = 32, 16, 64, 8, 512
FORCE_BONUS = 1e4
N_GROUPS, EXPERTS_PER_GROUP, TOP_K = 4, 8, 2
N_EXPERTS = N_GROUPS * EXPERTS_PER_GROUP

LANES = 128
VMEM_LIMIT_BYTES = 56 * 1024 * 1024
MOE_ROWS = 256


def _params(*sem):
    return pltpu.CompilerParams(dimension_semantics=sem, vmem_limit_bytes=VMEM_LIMIT_BYTES)


def _tile(n, pref):
    t = min(n, pref)
    while n % t:
        t -= 1
    return t


def _mm_body(*refs, nx, has_gain, emit_xn, mode):
    it = iter(refs)
    x_refs = [next(it) for _ in range(nx)]
    g_ref = next(it) if has_gain else None
    w_refs = [next(it) for _ in range(nx)]
    r_ref = next(it) if mode in ("resid", "ple") else None
    p_ref = next(it) if mode == "ple" else None
    wp_ref = next(it) if mode == "ple" else None
    o_ref = next(it)
    xo_ref = next(it) if emit_xn else None
    xn_ref = next(it) if has_gain else None

    if has_gain:
        @pl.when(pl.program_id(1) == 0)
        def _():
            x = x_refs[0][...].astype(F32)
            y = x * lax.rsqrt(jnp.mean(x * x, axis=-1, keepdims=True) + NORM_EPS) * g_ref[...]
            xn_ref[...] = y.astype(BF16)
            if emit_xn:
                rows, chunks = y.shape[0], y.shape[1] // LANES
                for c in range(chunks):
                    xo_ref[pl.ds(c, rows, stride=chunks), :] = y[:, c * LANES:(c + 1) * LANES]
        lhs = [xn_ref[...]]
    else:
        lhs = [x_ref[...].astype(BF16) for x_ref in x_refs]
    acc = None
    for a, w_ref in zip(lhs, w_refs):
        d = jnp.dot(a, w_ref[...], preferred_element_type=F32)
        acc = d if acc is None else acc + d
    if mode == "resid":
        acc = r_ref[...] + acc
    elif mode == "ple":
        pp = jnp.dot(p_ref[...].astype(BF16), wp_ref[...], preferred_element_type=F32)
        acc = r_ref[...] + pp * jax.nn.sigmoid(acc)
    o_ref[...] = acc.astype(o_ref.dtype)


def _mm(xs, ws, *, name, gain=None, out_dtype=F32, tm=1024, tn=512, resid=None, ple=None,
        emit_xn=False):
    M = xs[0][0].shape[0]
    N = ws[0].shape[1]
    tm, tn = _tile(M, tm), _tile(N, tn)
    nx = len(xs)
    has_gain = gain is not None
    mode = "ple" if ple is not None else ("resid" if resid is not None else "none")
    args, in_specs = [], []
    for arr, cb, K in xs:
        args.append(arr)
        in_specs.append(pl.BlockSpec((tm, K), lambda i, j, cb=cb: (i, cb)))
    if has_gain:
        K0 = xs[0][2]
        args.append(gain.reshape(1, K0).astype(F32))
        in_specs.append(pl.BlockSpec((1, K0), lambda i, j: (0, 0)))
    for (arr, cb, K), w in zip(xs, ws):
        args.append(w)
        in_specs.append(pl.BlockSpec((K, tn), lambda i, j: (0, j)))
    if mode == "resid":
        args.append(resid)
        in_specs.append(pl.BlockSpec((tm, tn), lambda i, j: (i, j)))
    if mode == "ple":
        r, p, wp = ple
        args += [r, p, wp]
        in_specs += [pl.BlockSpec((tm, tn), lambda i, j: (i, j)),
                     pl.BlockSpec((tm, p.shape[1]), lambda i, j: (i, 0)),
                     pl.BlockSpec((p.shape[1], tn), lambda i, j: (0, j))]
    out_shape = [jax.ShapeDtypeStruct((M, N), out_dtype)]
    out_specs = [pl.BlockSpec((tm, tn), lambda i, j: (i, j))]
    if emit_xn:
        chunks = xs[0][2] // LANES
        out_shape.append(jax.ShapeDtypeStruct((M * chunks, LANES), F32))
        out_specs.append(pl.BlockSpec((tm * chunks, LANES), lambda i, j: (i, 0)))
    scratch = [pltpu.VMEM((tm, xs[0][2]), BF16)] if has_gain else []
    res = pl.pallas_call(
        functools.partial(_mm_body, nx=nx, has_gain=has_gain, emit_xn=emit_xn, mode=mode),
        out_shape=out_shape, grid=(M // tm, N // tn), in_specs=in_specs, out_specs=out_specs,
        scratch_shapes=scratch, compiler_params=_params("arbitrary", "arbitrary"), name=name,
    )(*args)
    return res if emit_xn else res[0]


def _rope_tables(positions):
    def tables(dim):
        inv = 1.0 / (ROPE_THETA ** (jnp.arange(0, dim, 2, dtype=F32) / dim))
        ang = positions.astype(F32)[..., None] * inv
        return jnp.cos(ang), jnp.sin(ang)

    lead = positions.shape
    c, s = tables(HEAD_DIM)
    c128 = jnp.concatenate([c, c], -1)
    s128 = jnp.concatenate([-s, s], -1)
    c, s = tables(B_ROPE)
    z = jnp.zeros_like(c)
    c64 = jnp.concatenate([c, c, z, z], -1)
    s64a = jnp.concatenate([-s, z, z, z], -1)
    s64b = jnp.concatenate([z, s, z, z], -1)
    n = int(np.prod(lead))
    return [t.reshape(n, LANES) for t in (c128, s128, c64, s64a, s64b)]


def _rot128(x, c, s):
    return x * c + pltpu.roll(x, 64, 1) * s


def _rot64(x, c, sa, sb):
    return x * c + pltpu.roll(x, 96, 1) * sa + pltpu.roll(x, 32, 1) * sb


def _prep_body(y_ref, c128, s128, c64, s64a, s64b, *o_refs, program):
    for src, mode, scale, oi, dst in program:
        x = y_ref[:, src * LANES:(src + 1) * LANES].astype(F32)
        if mode == "rot128":
            x = _rot128(x, c128[...], s128[...])
        elif mode == "rot64":
            x = _rot64(x, c64[...], s64a[...], s64b[...])
        if scale != 1.0:
            x = x * scale
        o_refs[oi][:, dst * LANES:(dst + 1) * LANES] = x.astype(o_refs[oi].dtype)


def _prep(y, tables, program, outs, *, name, tm=256):
    M, C = y.shape
    tm = _tile(M, tm)
    row = lambda i: (i, 0)
    return pl.pallas_call(
        functools.partial(_prep_body, program=tuple(program)),
        out_shape=[jax.ShapeDtypeStruct((M, nb * LANES), dt) for nb, dt in outs],
        grid=(M // tm,),
        in_specs=[pl.BlockSpec((tm, C), row)] + [pl.BlockSpec((tm, LANES), row)] * 5,
        out_specs=[pl.BlockSpec((tm, nb * LANES), row) for nb, _ in outs],
        compiler_params=_params("arbitrary"), name=name,
    )(y, *tables)


def _pair_tables(S, tq, tk, window):
    qi_l, ki_l, fl_l, mk_l, masks, ids = [], [], [], [], [], {}
    for qi in range(S // tq):
        q0, q1 = qi * tq, (qi + 1) * tq - 1
        ks = []
        for ki in range(S // tk):
            k0, k1 = ki * tk, (ki + 1) * tk - 1
            if k0 > q1 or (window is not None and q0 - k1 >= window):
                continue
            full = k1 <= q0 and (window is None or q1 - k0 < window)
            mid = 0
            if not full:
                delta = q0 - k0
                if delta not in ids:
                    rel = np.arange(tq)[:, None] + delta - np.arange(tk)[None, :]
                    ok = (rel >= 0) if window is None else ((rel >= 0) & (rel < window))
                    masks.append(np.where(ok, 0.0, NEG_INF).astype(np.float32))
                    ids[delta] = len(masks)
                mid = ids[delta]
            ks.append((ki, mid))
        for n, (ki, mid) in enumerate(ks):
            qi_l.append(qi)
            ki_l.append(ki)
            fl_l.append((1 if n == 0 else 0) | (2 if n == len(ks) - 1 else 0))
            mk_l.append(mid)
    if not masks:
        masks.append(np.zeros((tq, tk), np.float32))
    tabs = [jnp.asarray(np.array(a, np.int32)) for a in (qi_l, ki_l, fl_l, mk_l)]
    return tabs, jnp.asarray(np.stack(masks))


def _flash_body(qi_t, ki_t, fl_t, mk_t, *refs, nq, nkv, q_src, k_src, v_src, tq, has_sink, has_bias,
                has_sel):
    nc = len(q_src)
    it = iter(refs)
    q_refs = [next(it) for _ in range(nq)]
    kv_refs = [next(it) for _ in range(nkv)]
    mask_ref = next(it)
    sink_ref = next(it) if has_sink else None
    cum_ref = next(it) if has_bias else None
    cumt_ref = next(it) if has_bias else None
    sel_ref = next(it) if has_sel else None
    exp_ref = next(it) if has_sel else None
    o_ref = next(it)
    m_sc, l_sc, acc_sc = next(it), next(it), next(it)
    cq_sc = next(it) if has_bias else None

    step_id = pl.program_id(2)
    fl, mk = fl_t[step_id], mk_t[step_id]

    def cols(ref, off):
        return ref[0, :, off:off + LANES]

    def cat(refs_, src):
        xs = [cols(refs_[pi], off) for pi, off in src]
        return xs[0] if len(xs) == 1 else jnp.concatenate(xs, axis=-1)

    @pl.when((fl & 1) != 0)
    def _init():
        m_sc[...] = jnp.full(m_sc.shape, NEG_INF, F32)
        l_sc[...] = jnp.zeros(l_sc.shape, F32)
        acc_sc[...] = jnp.zeros(acc_sc.shape, F32)
        if has_bias:
            lane = lax.broadcasted_iota(jnp.int32, (tq, LANES), 1)
            for g in range(nc):
                head = pl.program_id(1) * nc + g
                cq_sc[g] = jnp.sum(jnp.where(lane == head, cum_ref[0], 0.0), axis=-1, keepdims=True)

    def step(masked):
        add = mask_ref[mk - 1] if masked else None
        if has_sel:
            hidden = (jnp.dot(sel_ref[0, 0], exp_ref[0], preferred_element_type=F32) - 1.0) * (-NEG_INF)
            add = hidden if add is None else add + hidden
        for g in range(nc):
            q = cat(q_refs, q_src[g])
            k = cat(kv_refs, k_src[g])
            v = cols(kv_refs[v_src[g][0]], v_src[g][1])
            s = lax.dot_general(q, k, (((1,), (1,)), ((), ())), preferred_element_type=F32)
            if has_bias:
                s = s + (cq_sc[g] - cumt_ref[0, g])
            if add is not None:
                s = s + add
            m_prev = m_sc[g]
            m_new = jnp.maximum(m_prev, jnp.max(s, axis=-1, keepdims=True))
            reps = s.shape[1] // LANES
            p = jnp.exp(s - (jnp.concatenate([m_new] * reps, axis=1) if reps > 1 else m_new))
            alpha = jnp.exp(m_prev - m_new)
            l_sc[g] = alpha * l_sc[g] + jnp.sum(p, axis=-1, keepdims=True)
            acc_sc[g] = alpha * acc_sc[g] + jnp.dot(p.astype(BF16), v, preferred_element_type=F32)
            m_sc[g] = m_new

    @pl.when(mk != 0)
    def _():
        step(True)

    @pl.when(mk == 0)
    def _():
        step(False)

    @pl.when((fl & 2) != 0)
    def _finish():
        for g in range(nc):
            m, l, acc = m_sc[g], l_sc[g], acc_sc[g]
            if has_sink:
                sk = sink_ref[0, g:g + 1, 0:1]
                m_f = jnp.maximum(m, sk)
                w = jnp.exp(m - m_f)
                l = l * w + jnp.exp(sk - m_f)
                acc = acc * w
            o_ref[0, :, g * LANES:(g + 1) * LANES] = (acc / l).astype(o_ref.dtype)


def _flash(q_parts, kv_parts, q_src, k_src, v_src, *, name, B, S, n_steps, tq, tk, window=None, sinks=None,
           cum=None, cumt=None, sel=None, out_dtype=BF16):
    nc = len(q_src)
    tq, tk = _tile(S, tq), _tile(S, tk)
    tabs, masks = _pair_tables(S, tq, tk, window)
    npairs = int(tabs[0].shape[0])
    args, in_specs = [], []
    for arr, width, cf in q_parts:
        args.append(arr)
        in_specs.append(pl.BlockSpec((1, tq, width), lambda b, h, s, qt, kt, ft, mt, cf=cf: (b, qt[s], cf(h))))
    for arr, width, cf in kv_parts:
        args.append(arr)
        in_specs.append(pl.BlockSpec((1, tk, width), lambda b, h, s, qt, kt, ft, mt, cf=cf: (b, kt[s], cf(h))))
    args.append(masks)
    in_specs.append(pl.BlockSpec(masks.shape, lambda b, h, s, qt, kt, ft, mt: (0, 0, 0)))
    if sinks is not None:
        args.append(jnp.broadcast_to(sinks.astype(F32).reshape(n_steps, nc, 1), (n_steps, nc, LANES)))
        in_specs.append(pl.BlockSpec((1, nc, LANES), lambda b, h, s, qt, kt, ft, mt: (h, 0, 0)))
    if cum is not None:
        args += [cum, cumt]
        in_specs += [pl.BlockSpec((1, tq, LANES), lambda b, h, s, qt, kt, ft, mt: (b, qt[s], 0)),
                     pl.BlockSpec((1, nc, 1, tk), lambda b, h, s, qt, kt, ft, mt: (b, h, 0, kt[s]))]
    if sel is not None:
        per = tk // D_SLC_LEN
        e = np.zeros((S // tk, LANES, tk), np.float32)
        for ki in range(S // tk):
            e[ki, ki * per + np.arange(tk) // D_SLC_LEN, np.arange(tk)] = 1.0
        args += [sel, jnp.asarray(e, BF16)]
        in_specs += [pl.BlockSpec((1, 1, tq, LANES), lambda b, h, s, qt, kt, ft, mt: (b, h, qt[s], 0)),
                     pl.BlockSpec((1, LANES, tk), lambda b, h, s, qt, kt, ft, mt: (kt[s], 0, 0))]
    scratch = [pltpu.VMEM((nc, tq, LANES), F32)] * 3
    if cum is not None:
        scratch.append(pltpu.VMEM((nc, tq, 1), F32))
    body = functools.partial(_flash_body, nq=len(q_parts), nkv=len(kv_parts), q_src=q_src, k_src=k_src,
                             v_src=v_src, tq=tq, has_sink=sinks is not None, has_bias=cum is not None,
                             has_sel=sel is not None)
    return pl.pallas_call(
        body, out_shape=jax.ShapeDtypeStruct((B, S, n_steps * nc * LANES), out_dtype),
        grid_spec=pltpu.PrefetchScalarGridSpec(
            num_scalar_prefetch=4, grid=(B, n_steps, npairs), in_specs=in_specs,
            out_specs=pl.BlockSpec((1, tq, nc * LANES), lambda b, h, s, qt, kt, ft, mt: (b, qt[s], h)),
            scratch_shapes=scratch),
        compiler_params=_params("arbitrary", "arbitrary", "arbitrary"), name=name,
    )(*tabs, *args)


def _gqa_flash(q, k, v, *, name, B, S, Hk, G, **kw):
    head = lambda h: h
    return _flash([(q, G * LANES, head)], [(k, LANES, head), (v, LANES, head)],
                  [[(0, g * LANES)] for g in range(G)], [[(0, 0)]] * G, [(1, 0)] * G,
                  name=name, B=B, S=S, n_steps=Hk, **kw)


def _cum_body(y_ref, b_ref, tri_ref, o_ref, carry):
    @pl.when(pl.program_id(1) == 0)
    def _():
        carry[...] = jnp.zeros(carry.shape, F32)
    x = y_ref[0] + b_ref[...]
    logf = jnp.minimum(x, 0.0) - jnp.log1p(jnp.exp(-jnp.abs(x)))
    cum = jnp.dot(tri_ref[...], logf, preferred_element_type=F32, precision=lax.Precision.HIGHEST) + carry[...]
    o_ref[0] = cum
    carry[...] = cum[-1:, :]


def _forget_cum(y3, col_block, bias_row, *, ts=512):
    B, S, _ = y3.shape
    ts = _tile(S, ts)
    tri = jnp.asarray(np.tril(np.ones((ts, ts), np.float32)))
    return pl.pallas_call(
        _cum_body, out_shape=jax.ShapeDtypeStruct((B, S, LANES), F32), grid=(B, S // ts),
        in_specs=[pl.BlockSpec((1, ts, LANES), lambda b, s: (b, s, col_block)),
                  pl.BlockSpec((1, LANES), lambda b, s: (0, 0)),
                  pl.BlockSpec((ts, ts), lambda b, s: (0, 0))],
        out_specs=pl.BlockSpec((1, ts, LANES), lambda b, s: (b, s, 0)),
        scratch_shapes=[pltpu.VMEM((1, LANES), F32)],
        compiler_params=_params("arbitrary", "arbitrary"), name="forget_cum",
    )(y3, bias_row, tri)


def _compress_body(*refs, rope, nc):
    if rope:
        z_ref, pe_ref, w1_ref, w2_ref, c_ref, s_ref, o_ref = refs
    else:
        z_ref, pe_ref, w1_ref, w2_ref, o_ref = refs
    half = D_CMP_LEN // 2
    width = D_KV_HEADS * HEAD_DIM
    for hk in range(D_KV_HEADS):
        u = jnp.zeros((nc, w1_ref.shape[1]), F32)
        v = jnp.zeros((nc, w1_ref.shape[1]), F32)
        for l in range(half):
            z = z_ref[0, :, l * width + hk * HEAD_DIM:l * width + (hk + 1) * HEAD_DIM]
            zu = (z + pe_ref[l:l + 1, :]).astype(BF16)
            zv = (z + pe_ref[half + l:half + l + 1, :]).astype(BF16)
            u = u + jnp.dot(zu, w1_ref[l * HEAD_DIM:(l + 1) * HEAD_DIM, :], preferred_element_type=F32)
            v = v + jnp.dot(zv, w1_ref[(half + l) * HEAD_DIM:(half + l + 1) * HEAD_DIM, :],
                            preferred_element_type=F32)
        pre = u + pltpu.roll(v, nc - 1, 0)
        hid = jax.nn.gelu(pre, approximate=True)
        out = jnp.dot(hid.astype(BF16), w2_ref[...], preferred_element_type=F32)
        if rope:
            out = _rot128(out, c_ref[0], s_ref[0])
        o_ref[0, hk] = out.astype(o_ref.dtype)


def _compress(z, pe, w1, w2, rope_tabs=None):
    B, S, W = z.shape
    nc = S // D_CMP_STRIDE
    zc = z.reshape(B, nc, D_CMP_STRIDE * W)
    args = [zc, pe.astype(F32), w1.astype(BF16), w2.astype(BF16)]
    in_specs = [pl.BlockSpec((1, nc, D_CMP_STRIDE * W), lambda b: (b, 0, 0)),
                pl.BlockSpec(pe.shape, lambda b: (0, 0)),
                pl.BlockSpec(w1.shape, lambda b: (0, 0)),
                pl.BlockSpec(w2.shape, lambda b: (0, 0))]
    if rope_tabs is not None:
        args += list(rope_tabs)
        in_specs += [pl.BlockSpec((1, nc, LANES), lambda b: (b, 0, 0))] * 2
    return pl.pallas_call(
        functools.partial(_compress_body, rope=rope_tabs is not None, nc=nc),
        out_shape=jax.ShapeDtypeStruct((B, D_KV_HEADS, nc, HEAD_DIM), BF16), grid=(B,),
        in_specs=in_specs, out_specs=pl.BlockSpec((1, D_KV_HEADS, nc, HEAD_DIM), lambda b: (b, 0, 0, 0)),
        compiler_params=_params("arbitrary"), name="nsa_compress",
    )(*args)


def _cmp_attn_body(q_ref, k_ref, v_ref, ov_ref, o_ref, sel_ref, *, G, tq, nc, n_cmp, n_slc, topn):
    qi = pl.program_id(2)
    t = qi * tq + lax.broadcasted_iota(jnp.int32, (tq, nc), 0)
    c = lax.broadcasted_iota(jnp.int32, (tq, nc), 1)
    valid = (c * D_CMP_STRIDE + (D_CMP_LEN - 1) <= t) & (c < n_cmp)
    k = k_ref[0, 0]
    v = v_ref[0, 0]
    psum = jnp.zeros((tq, nc), F32)
    for g in range(G):
        q = q_ref[0, :, g * LANES:(g + 1) * LANES]
        s = lax.dot_general(q, k, (((1,), (1,)), ((), ())), preferred_element_type=F32)
        s = jnp.where(valid, s, NEG_INF)
        e = jnp.where(valid, jnp.exp(s - jnp.max(s, axis=-1, keepdims=True)), 0.0)
        p = e / jnp.maximum(jnp.sum(e, axis=-1, keepdims=True), jnp.finfo(F32).tiny)
        o_ref[0, :, g * LANES:(g + 1) * LANES] = jnp.dot(
            p.astype(BF16), v, preferred_element_type=F32).astype(o_ref.dtype)
        psum = psum + p
    imp = jnp.dot(psum, ov_ref[...], preferred_element_type=F32, precision=lax.Precision.HIGHEST)
    lane = lax.broadcasted_iota(jnp.int32, (tq, LANES), 1)
    trow = qi * tq + lax.broadcasted_iota(jnp.int32, (tq, LANES), 0)
    cur = jnp.right_shift(trow, int(math.log2(D_SLC_LEN)))
    forced = (lane == 0) | (lane == cur) | (lane == cur - 1)
    imp = jnp.where(lane * D_SLC_LEN > trow, NEG_INF, imp + jnp.where(forced, FORCE_BONUS, 0.0))
    imp = jnp.where(lane >= n_slc, TAKEN, imp)
    chosen = jnp.zeros((tq, LANES), F32)
    for _ in range(topn):
        mx = jnp.max(imp, axis=-1, keepdims=True)
        idx = jnp.min(jnp.where(imp == mx, lane, LANES), axis=-1, keepdims=True)
        hit = lane == idx
        chosen = jnp.where(hit, 1.0, chosen)
        imp = jnp.where(hit, TAKEN, imp)
    sel_ref[0, 0] = chosen.astype(sel_ref.dtype)


def _cmp_attn(q, k_cmp, v_cmp, *, tq=256):
    B, S, _ = q.shape
    Hk, G = D_KV_HEADS, D_HEADS // D_KV_HEADS
    nc = S // D_CMP_STRIDE
    n_cmp = (S - D_CMP_LEN) // D_CMP_STRIDE + 1
    n_slc = S // D_SLC_LEN
    tq = _tile(S, tq)
    c0 = np.arange(nc) * D_CMP_STRIDE
    s0 = np.arange(LANES) * D_SLC_LEN
    ov = ((c0[:, None] < (s0 + D_SLC_LEN)[None, :]) & ((c0 + D_CMP_LEN)[:, None] > s0[None, :])
          & (np.arange(nc) < n_cmp)[:, None] & (np.arange(LANES) < n_slc)[None, :]).astype(np.float32)
    body = functools.partial(_cmp_attn_body, G=G, tq=tq, nc=nc, n_cmp=n_cmp, n_slc=n_slc,
                             topn=min(D_SLC_TOPN, n_slc))
    return pl.pallas_call(
        body,
        out_shape=[jax.ShapeDtypeStruct((B, S, Hk * G * LANES), F32),
                   jax.ShapeDtypeStruct((B, Hk, S, LANES), BF16)],
        grid=(B, Hk, S // tq),
        in_specs=[pl.BlockSpec((1, tq, G * LANES), lambda b, h, i: (b, i, h)),
                  pl.BlockSpec((1, 1, nc, LANES), lambda b, h, i: (b, h, 0, 0)),
                  pl.BlockSpec((1, 1, nc, LANES), lambda b, h, i: (b, h, 0, 0)),
                  pl.BlockSpec((nc, LANES), lambda b, h, i: (0, 0))],
        out_specs=[pl.BlockSpec((1, tq, G * LANES), lambda b, h, i: (b, i, h)),
                   pl.BlockSpec((1, 1, tq, LANES), lambda b, h, i: (b, h, i, 0))],
        compiler_params=_params("arbitrary", "arbitrary", "arbitrary"), name="nsa_cmp_attn",
    )(q, k_cmp, v_cmp, jnp.asarray(ov))


def _gate_body(y_ref, a_ref, b_ref, c_ref, o_ref, *, lane0):
    g = jax.nn.sigmoid(y_ref[...])
    for h in range(D_HEADS):
        cols = slice(h * LANES, (h + 1) * LANES)
        ga = g[:, lane0 + h:lane0 + h + 1]
        gb = g[:, lane0 + D_HEADS + h:lane0 + D_HEADS + h + 1]
        gc = g[:, lane0 + 2 * D_HEADS + h:lane0 + 2 * D_HEADS + h + 1]
        o_ref[:, cols] = (ga * a_ref[:, cols] + gb * b_ref[:, cols] + gc * c_ref[:, cols]).astype(o_ref.dtype)


def _nsa_gate(y, col_block, lane0, o_cmp, o_slc, o_win, *, tm=512):
    M = y.shape[0]
    W = o_cmp.shape[1]
    tm = _tile(M, tm)
    row = lambda i: (i, 0)
    return pl.pallas_call(
        functools.partial(_gate_body, lane0=lane0), out_shape=jax.ShapeDtypeStruct((M, W), BF16),
        grid=(M // tm,),
        in_specs=[pl.BlockSpec((tm, LANES), lambda i: (i, col_block))] + [pl.BlockSpec((tm, W), row)] * 3,
        out_specs=pl.BlockSpec((tm, W), row), compiler_params=_params("arbitrary"), name="nsa_gate",
    )(y, o_cmp, o_slc, o_win)


def _route_body(lg_ref, b_ref, tri_ref, e_ref, w_ref, pos_ref, cnt_ref, carry, *, tm):
    @pl.when(pl.program_id(0) == 0)
    def _():
        carry[...] = jnp.zeros(carry.shape, F32)
    lane = lax.broadcasted_iota(jnp.int32, (tm, LANES), 1)
    logits = lg_ref[...] + b_ref[...]
    gl = jnp.where(lane < N_GROUPS, logits, -jnp.inf)
    gmax = jnp.max(gl, axis=-1, keepdims=True)
    g_val = 1.0 / jnp.sum(jnp.exp(gl - gmax), axis=-1, keepdims=True)
    g_idx = jnp.min(jnp.where(gl == gmax, lane, LANES), axis=-1, keepdims=True)
    lo = N_GROUPS + EXPERTS_PER_GROUP * g_idx
    el = jnp.where((lane >= lo) & (lane < lo + EXPERTS_PER_GROUP), logits, -jnp.inf)
    e1 = jnp.max(el, axis=-1, keepdims=True)
    i1 = jnp.min(jnp.where(el == e1, lane, LANES), axis=-1, keepdims=True)
    el2 = jnp.where(lane == i1, -jnp.inf, el)
    e2 = jnp.max(el2, axis=-1, keepdims=True)
    i2 = jnp.min(jnp.where(el2 == e2, lane, LANES), axis=-1, keepdims=True)
    r = jnp.exp(e2 - e1)
    w1 = g_val / (1.0 + r)
    w2 = w1 * r
    x1, x2 = i1 - N_GROUPS, i2 - N_GROUPS
    e_ref[...] = jnp.where(lane == 0, x1, jnp.where(lane == 1, x2, 0))
    w_ref[...] = jnp.where(lane == 0, w1, jnp.where(lane == 1, w2, 0.0))
    hot1 = lane == x1
    hot2 = lane == x2
    both = jnp.where(hot1 | hot2, 1.0, 0.0)
    before = jnp.dot(tri_ref[...], both.astype(BF16), preferred_element_type=F32) + carry[...]
    p1 = jnp.sum(jnp.where(hot1, before, 0.0), axis=-1, keepdims=True)
    p2 = jnp.sum(jnp.where(hot2, before, 0.0), axis=-1, keepdims=True)
    pos_ref[...] = jnp.where(lane == 0, p1, jnp.where(lane == 1, p2, 0.0)).astype(jnp.int32)
    carry[...] = carry[...] + jnp.sum(both, axis=0, keepdims=True)
    cnt_ref[...] = carry[...].astype(jnp.int32)


def _route(logits, bias_row, *, tm=512):
    T = logits.shape[0]
    tm = _tile(T, tm)
    tri = jnp.asarray(np.tril(np.ones((tm, tm), np.float32), -1), BF16)
    row = lambda i: (i, 0)
    fixed = lambda i: (0, 0)
    return pl.pallas_call(
        functools.partial(_route_body, tm=tm),
        out_shape=[jax.ShapeDtypeStruct((T, LANES), jnp.int32), jax.ShapeDtypeStruct((T, LANES), F32),
                   jax.ShapeDtypeStruct((T, LANES), jnp.int32), jax.ShapeDtypeStruct((1, LANES), jnp.int32)],
        grid=(T // tm,),
        in_specs=[pl.BlockSpec((tm, LANES), row), pl.BlockSpec((1, LANES), fixed), pl.BlockSpec((tm, tm), fixed)],
        out_specs=[pl.BlockSpec((tm, LANES), row)] * 3 + [pl.BlockSpec((1, LANES), fixed)],
        scratch_shapes=[pltpu.VMEM((1, LANES), F32)],
        compiler_params=_params("arbitrary"), name="moe_route",
    )(logits, bias_row, tri)


def _dispatch_body(d_ref, x_ref, _, out_ref, sem, *, tm, chunks):
    base = pl.program_id(0) * tm

    def issue(r, c):
        src = x_ref.at[pl.ds(pl.multiple_of(r * chunks, chunks), chunks)]
        for k in range(TOP_K):
            row = d_ref[(base + r) * TOP_K + k]
            pltpu.make_async_copy(src, out_ref.at[pl.ds(pl.multiple_of(row * chunks, chunks), chunks)], sem).start()
        return c

    lax.fori_loop(0, tm, issue, 0)
    for _ in range(TOP_K):
        pltpu.make_async_copy(x_ref, out_ref.at[pl.ds(0, tm * chunks)], sem).wait()


def _dispatch(xn, dest, n_rows, chunks, *, tm=512):
    T = xn.shape[0] // chunks
    tm = _tile(T, tm)
    return pl.pallas_call(
        functools.partial(_dispatch_body, tm=tm, chunks=chunks),
        out_shape=jax.ShapeDtypeStruct((n_rows * chunks, LANES), xn.dtype),
        grid_spec=pltpu.PrefetchScalarGridSpec(
            num_scalar_prefetch=1, grid=(T // tm,),
            in_specs=[pl.BlockSpec((tm * chunks, LANES), lambda i, d: (i, 0)), pl.BlockSpec(memory_space=pl.ANY)],
            out_specs=pl.BlockSpec(memory_space=pl.ANY),
            scratch_shapes=[pltpu.SemaphoreType.DMA(())]),
        input_output_aliases={2: 0},
        compiler_params=pltpu.CompilerParams(dimension_semantics=("arbitrary",), has_side_effects=True,
                                             vmem_limit_bytes=VMEM_LIMIT_BYTES),
        name="moe_dispatch",
    )(dest, xn, jnp.zeros((n_rows * chunks, LANES), xn.dtype))


def _expert_body(be_ref, nu_ref, x_ref, wg_ref, wu_ref, wd_ref, o_ref, x_sc, wg_sc, wu_sc, wd_sc, *, chunks):
    i = pl.program_id(0)
    fresh = (i == 0) | (be_ref[i] != be_ref[jnp.maximum(i - 1, 0)])

    @pl.when(fresh)
    def _():
        wg_sc[...] = wg_ref[0].astype(BF16)
        wu_sc[...] = wu_ref[0].astype(BF16)
        wd_sc[...] = wd_ref[0].astype(BF16)

    @pl.when(i < nu_ref[0])
    def _():
        for c in range(chunks):
            x_sc[:, c * LANES:(c + 1) * LANES] = x_ref[pl.ds(c, MOE_ROWS, stride=chunks), :].astype(BF16)
        x = x_sc[...]
        gate = jnp.dot(x, wg_sc[...], preferred_element_type=F32)
        up = jnp.dot(x, wu_sc[...], preferred_element_type=F32)
        hid = (gate * jax.nn.sigmoid(gate) * up).astype(BF16)
        y = jnp.dot(hid, wd_sc[...], preferred_element_type=F32)
        for c in range(chunks):
            o_ref[pl.ds(c, MOE_ROWS, stride=chunks), :] = y[:, c * LANES:(c + 1) * LANES]

    @pl.when(i >= nu_ref[0])
    def _():
        o_ref[...] = jnp.zeros(o_ref.shape, F32)


def _experts(xb, blk_e, n_used, w_gate, w_up, w_down):
    D, Hd = w_gate.shape[1], w_gate.shape[2]
    chunks = D // LANES
    nblk = xb.shape[0] // (MOE_ROWS * chunks)
    return pl.pallas_call(
        functools.partial(_expert_body, chunks=chunks), out_shape=jax.ShapeDtypeStruct(xb.shape, F32),
        grid_spec=pltpu.PrefetchScalarGridSpec(
            num_scalar_prefetch=2, grid=(nblk,),
            in_specs=[pl.BlockSpec((MOE_ROWS * chunks, LANES), lambda i, be, nu: (i, 0)),
                      pl.BlockSpec((1, D, Hd), lambda i, be, nu: (be[i], 0, 0)),
                      pl.BlockSpec((1, D, Hd), lambda i, be, nu: (be[i], 0, 0)),
                      pl.BlockSpec((1, Hd, D), lambda i, be, nu: (be[i], 0, 0))],
            out_specs=pl.BlockSpec((MOE_ROWS * chunks, LANES), lambda i, be, nu: (i, 0)),
            scratch_shapes=[pltpu.VMEM((MOE_ROWS, D), BF16), pltpu.VMEM((D, Hd), BF16),
                            pltpu.VMEM((D, Hd), BF16), pltpu.VMEM((Hd, D), BF16)]),
        compiler_params=_params("arbitrary"), name="moe_experts",
    )(blk_e, n_used, xb, w_gate, w_up, w_down)


def _collect_body(d_ref, h_ref, w_ref, yb_ref, o_ref, buf_a, buf_b, sem, *, tm, chunks, nsteps):
    i = pl.program_id(0)
    slot = lax.rem(i, 2)

    def fetch(step, slot_):
        base = step * tm

        def issue(r, c):
            dst = pl.ds(pl.multiple_of(r * chunks, chunks), chunks)
            for k, buf in enumerate((buf_a, buf_b)):
                row = d_ref[(base + r) * TOP_K + k]
                pltpu.make_async_copy(yb_ref.at[pl.ds(pl.multiple_of(row * chunks, chunks), chunks)],
                                      buf.at[slot_, dst], sem.at[slot_]).start()
            return c

        lax.fori_loop(0, tm, issue, 0)

    @pl.when(i == 0)
    def _():
        fetch(0, 0)

    @pl.when(i + 1 < nsteps)
    def _():
        fetch(i + 1, 1 - slot)

    for buf in (buf_a, buf_b):
        pltpu.make_async_copy(yb_ref.at[pl.ds(0, tm * chunks)], buf.at[slot], sem.at[slot]).wait()
    w = w_ref[...]
    for c in range(chunks):
        cols = slice(c * LANES, (c + 1) * LANES)
        o_ref[:, cols] = (h_ref[:, cols] + w[:, 0:1] * buf_a[slot, pl.ds(c, tm, stride=chunks), :]
                          + w[:, 1:2] * buf_b[slot, pl.ds(c, tm, stride=chunks), :])


def _collect(h, yb, dest, wts, *, tm=256):
    T, D = h.shape
    chunks = D // LANES
    tm = _tile(T, tm)
    nsteps = T // tm
    return pl.pallas_call(
        functools.partial(_collect_body, tm=tm, chunks=chunks, nsteps=nsteps),
        out_shape=jax.ShapeDtypeStruct((T, D), F32),
        grid_spec=pltpu.PrefetchScalarGridSpec(
            num_scalar_prefetch=1, grid=(nsteps,),
            in_specs=[pl.BlockSpec((tm, D), lambda i, d: (i, 0)), pl.BlockSpec((tm, LANES), lambda i, d: (i, 0)),
                      pl.BlockSpec(memory_space=pl.ANY)],
            out_specs=pl.BlockSpec((tm, D), lambda i, d: (i, 0)),
            scratch_shapes=[pltpu.VMEM((2, tm * chunks, LANES), F32), pltpu.VMEM((2, tm * chunks, LANES), F32),
                            pltpu.SemaphoreType.DMA((2,))]),
        compiler_params=_params("arbitrary"), name="moe_collect",
    )(dest, h, wts, yb)


def _moe(h, norm_g, w_group, b_group, w_expert, b_expert, w_gate, w_up, w_down):
    T, D = h.shape
    chunks = D // LANES
    pad = LANES - N_GROUPS - N_EXPERTS
    w_r = jnp.concatenate([w_group, w_expert, jnp.zeros((D, pad), F32)], axis=1).astype(BF16)
    b_r = jnp.concatenate([b_group, b_expert, jnp.zeros((pad,), F32)]).astype(F32).reshape(1, LANES)
    logits, xn = _mm([(h, 0, D)], [w_r], gain=norm_g, emit_xn=True, tm=512, tn=LANES, name="moe_router")
    eid, wts, pos, cnt = _route(logits, b_r)
    counts = cnt[0, :N_EXPERTS]
    padded = (counts + MOE_ROWS - 1) // MOE_ROWS * MOE_ROWS
    ends = jnp.cumsum(padded)
    offs = ends - padded
    dest = (offs[eid[:, :TOP_K]] + pos[:, :TOP_K]).reshape(T * TOP_K).astype(jnp.int32)
    P = T * TOP_K + N_EXPERTS * MOE_ROWS
    nblk = P // MOE_ROWS
    starts = jnp.arange(nblk, dtype=jnp.int32) * MOE_ROWS
    blk_e = jnp.minimum(jnp.sum(ends[None, :] <= starts[:, None], axis=1), N_EXPERTS - 1).astype(jnp.int32)
    n_used = (ends[-1:] // MOE_ROWS).astype(jnp.int32)
    xb = _dispatch(xn, dest, P, chunks)
    yb = _experts(xb, blk_e, n_used, w_gate, w_up, w_down)
    return _collect(h, yb, dest, wts)


def _pad_cols(w, n):
    return jnp.pad(w, ((0, 0), (0, n - w.shape[1])))


def _layer_even(h, B, S, tabs, norm_g, w_in, sinks, q_lat_norm, kv_lat_norm, w_uq, w_ukv, w_o):
    T, D = h.shape
    n_in = 2560
    y = _mm([(h, 0, D)], [_pad_cols(w_in, n_in).astype(BF16)], gain=norm_g, tn=512, name="ab_in")
    sa = HEAD_DIM ** -0.5
    prog = ([(i, "rot128", sa, 0, i) for i in range(8)] + [(8 + i, "rot128", 1.0, 1, i) for i in range(2)]
            + [(10 + i, "copy", 1.0, 2, i) for i in range(2)] + [(18, "rot64", 1.0, 3, 0)])
    q_a, k_a, v_a, k_r = _prep(y, tabs, prog, [(8, BF16), (2, BF16), (2, BF16), (1, BF16)], name="ab_prep")
    wq = w_uq.reshape(B_Q_LORA, B_HEADS, B_NOPE + B_ROPE)
    wq_n = wq[:, :, :B_NOPE].reshape(B_Q_LORA, B_HEADS * B_NOPE)
    wq_r = jnp.pad(wq[:, :, B_NOPE:], ((0, 0), (0, 0), (0, LANES - B_ROPE))).reshape(B_Q_LORA, B_HEADS * LANES)
    qb = _mm([(y, 1536 // B_Q_LORA, B_Q_LORA)], [jnp.concatenate([wq_n, wq_r], 1).astype(BF16)],
             gain=q_lat_norm, tn=512, name="mla_uq")
    sb = (B_NOPE + B_ROPE) ** -0.5
    prog = [(i, "copy", sb, 0, i) for i in range(8)] + [(8 + i, "rot64", sb, 1, i) for i in range(8)]
    q_n, q_r = _prep(qb, tabs, prog, [(8, BF16), (8, BF16)], name="mla_prep")
    kv = _mm([(y, 2048 // B_KV_LORA, B_KV_LORA)], [w_ukv.astype(BF16)], gain=kv_lat_norm, out_dtype=BF16,
             tn=512, name="mla_ukv")
    r3 = lambda a: a.reshape(B, S, a.shape[-1])
    o_a = _gqa_flash(r3(q_a), r3(k_a), r3(v_a), name="swa_attn", B=B, S=S, Hk=A_KV_HEADS,
                     G=A_HEADS // A_KV_HEADS, window=A_WINDOW, sinks=sinks, tq=256, tk=256)
    hp = 2
    head = lambda h: h
    o_b = _flash([(r3(q_n), hp * LANES, head), (r3(q_r), hp * LANES, head)],
                 [(r3(kv), hp * 2 * LANES, head), (r3(k_r), LANES, lambda h: 0)],
                 [[(0, g * LANES), (1, g * LANES)] for g in range(hp)],
                 [[(0, 2 * g * LANES), (1, 0)] for g in range(hp)], [(0, (2 * g + 1) * LANES) for g in range(hp)],
                 name="mla_attn", B=B, S=S, n_steps=B_HEADS // hp, tq=512, tk=512)
    wo = w_o.astype(BF16)
    na = A_HEADS * HEAD_DIM
    return _mm([(o_a.reshape(T, -1), 0, na), (o_b.reshape(T, -1), 0, B_HEADS * B_V)], [wo[:na], wo[na:]],
               resid=h, tn=512, name="ab_out")


def _layer_odd(h, B, S, tabs, cmp_tabs, norm_g, w_in, forget_bias, pe_k, w1_k, w2_k, pe_v, w1_v, w2_v, w_o):
    T, D = h.shape
    hc, kvw = C_HEADS * HEAD_DIM, D_KV_HEADS * HEAD_DIM
    o_cf = 3 * hc
    o_dq = o_cf + C_HEADS
    o_dg = o_dq + D_HEADS * HEAD_DIM + 6 * kvw
    w_r = jnp.concatenate([w_in[:, :o_cf], w_in[:, o_dq:o_dg], w_in[:, o_cf:o_dq], w_in[:, o_dg:]], axis=1)
    n_in = 5760
    y = _mm([(h, 0, D)], [_pad_cols(w_r, n_in).astype(BF16)], gain=norm_g, tn=640, name="cd_in")
    misc = (n_in // LANES) - 1
    sc = HEAD_DIM ** -0.5
    prog = ([(i, "copy", sc, 0, i) for i in range(8)] + [(8 + i, "copy", 1.0, 1, i) for i in range(8)]
            + [(16 + i, "copy", 1.0, 2, i) for i in range(8)] + [(24 + i, "rot128", sc, 3, i) for i in range(8)]
            + [(32 + i, "copy", 1.0, 4, i) for i in range(2)] + [(34 + i, "copy", 1.0, 5, i) for i in range(2)]
            + [(36 + i, "rot128", 1.0, 6, i) for i in range(2)] + [(38 + i, "copy", 1.0, 7, i) for i in range(2)]
            + [(40 + i, "rot128", 1.0, 8, i) for i in range(2)] + [(42 + i, "copy", 1.0, 9, i) for i in range(2)])
    outs = [(8, BF16)] * 4 + [(2, F32), (2, F32)] + [(2, BF16)] * 4
    q_c, k_c, v_c, q_d, z_kc, z_vc, k_s, v_s, k_w, v_w = _prep(y, tabs, prog, outs, name="cd_prep")
    r3 = lambda a: a.reshape(B, S, a.shape[-1])
    fb = jnp.pad(forget_bias.astype(F32), (0, LANES - C_HEADS)).reshape(1, LANES)
    cum = _forget_cum(r3(y), misc, fb)
    cumt = jnp.swapaxes(cum[:, :, :C_HEADS], 1, 2).reshape(B, C_HEADS, 1, S)
    hp = 2
    head = lambda h: h
    o_c = _flash([(r3(q_c), hp * LANES, head)], [(r3(k_c), hp * LANES, head), (r3(v_c), hp * LANES, head)],
                 [[(0, g * LANES)] for g in range(hp)], [[(0, g * LANES)] for g in range(hp)],
                 [(1, g * LANES) for g in range(hp)], name="fox_attn", B=B, S=S, n_steps=C_HEADS // hp,
                 tq=512, tk=512, cum=cum, cumt=cumt)
    G = D_HEADS // D_KV_HEADS
    k_cmp = _compress(r3(z_kc), pe_k, w1_k, w2_k, rope_tabs=cmp_tabs)
    v_cmp = _compress(r3(z_vc), pe_v, w1_v, w2_v)
    q3 = r3(q_d)
    o_cmp, sel = _cmp_attn(q3, k_cmp, v_cmp)
    o_slc = _gqa_flash(q3, r3(k_s), r3(v_s), name="nsa_slc_attn", B=B, S=S, Hk=D_KV_HEADS, G=G, sel=sel,
                       out_dtype=F32, tq=512, tk=512)
    o_win = _gqa_flash(q3, r3(k_w), r3(v_w), name="nsa_win_attn", B=B, S=S, Hk=D_KV_HEADS, G=G,
                       window=D_WINDOW, out_dtype=F32, tq=256, tk=256)
    o_d = _nsa_gate(y, misc, C_HEADS, o_cmp.reshape(T, -1), o_slc.reshape(T, -1), o_win.reshape(T, -1))
    wo = w_o.astype(BF16)
    return _mm([(o_c.reshape(T, -1), 0, hc), (o_d, 0, D_HEADS * HEAD_DIM)], [wo[:hc], wo[hc:]],
               resid=h, tn=512, name="cd_out")


def _final_norm_body(x_ref, g_ref, o_ref):
    x = x_ref[...]
    o_ref[...] = x * lax.rsqrt(jnp.mean(x * x, axis=-1, keepdims=True) + NORM_EPS) * g_ref[...]


def _final_norm(h, g, *, tm=512):
    T, D = h.shape
    tm = _tile(T, tm)
    return pl.pallas_call(
        _final_norm_body, out_shape=jax.ShapeDtypeStruct((T, D), F32), grid=(T // tm,),
        in_specs=[pl.BlockSpec((tm, D), lambda i: (i, 0)), pl.BlockSpec((1, D), lambda i: (0, 0))],
        out_specs=pl.BlockSpec((tm, D), lambda i: (i, 0)), compiler_params=_params("arbitrary"),
        name="final_norm",
    )(h, g.reshape(1, D).astype(F32))


def kernel(x, p, positions, ab_w_in, ab_sinks, ab_q_lat_norm, ab_kv_lat_norm, ab_w_uq, ab_w_ukv, ab_w_o,
           cd_w_in, cd_forget_bias, cd_cmp_pe_k, cd_cmp_w1_k, cd_cmp_w2_k, cd_cmp_pe_v, cd_cmp_w1_v,
           cd_cmp_w2_v, cd_w_o, mixer_norm, moe_norm, router_group_w, router_group_b, router_expert_w,
           router_expert_b, expert_w_gate, expert_w_up, expert_w_down, ple_proj, ple_gate_norm, ple_gate_w,
           final_norm):
    B, S, D = x.shape
    T = B * S
    depth = p.shape[0]
    tabs = _rope_tables(positions)
    nc = S // D_CMP_STRIDE
    end = np.minimum(np.arange(nc) * D_CMP_STRIDE + D_CMP_LEN - 1, S - 1)
    cmp_tabs = [t.reshape(B, nc, LANES) for t in _rope_tables(positions[:, end])[:2]]
    h = x.reshape(T, D)
    for i in range(depth):
        j = i // 2
        if i % 2 == 0:
            h = _layer_even(h, B, S, tabs, mixer_norm[i], ab_w_in[j], ab_sinks[j], ab_q_lat_norm[j],
                            ab_kv_lat_norm[j], ab_w_uq[j], ab_w_ukv[j], ab_w_o[j])
        else:
            h = _layer_odd(h, B, S, tabs, cmp_tabs, mixer_norm[i], cd_w_in[j], cd_forget_bias[j],
                           cd_cmp_pe_k[j], cd_cmp_w1_k[j], cd_cmp_w2_k[j], cd_cmp_pe_v[j], cd_cmp_w1_v[j],
                           cd_cmp_w2_v[j], cd_w_o[j])
        h = _moe(h, moe_norm[i], router_group_w[i], router_group_b[i], router_expert_w[i], router_expert_b[i],
                 expert_w_gate[i], expert_w_up[i], expert_w_down[i])
        h = _mm([(h, 0, D)], [ple_gate_w[i].astype(BF16)], gain=ple_gate_norm[i], tn=512,
                ple=(h, p[i].reshape(T, -1), ple_proj[i].astype(BF16)), name="ple")
    return _final_norm(h, final_norm).reshape(B, S, D)
```

```python
import functools
import math

import numpy as np
import jax
import jax.numpy as jnp
from jax import lax
from jax.experimental import pallas as pl
from jax.experimental.pallas import tpu as pltpu

F32 = jnp.float32
BF16 = jnp.bfloat16

HEAD_DIM = 128
ROPE_THETA = 10000.0
NORM_EPS = 1e-6
NEG_INF = -1e30
TAKEN = -3e38
A_HEADS, A_KV_HEADS, A_WINDOW = 8, 2, 128
B_HEADS, B_Q_LORA, B_KV_LORA, B_NOPE, B_ROPE, B_V = 8, 512, 256, 128, 64, 128
C_HEADS = 8
D_HEADS, D_KV_HEADS = 8, 2
D_CMP_LEN, D_CMP_STRIDE, D_SLC_LEN, D_SLC_TOPN, D_WINDOW = 32, 16, 64, 8, 512
FORCE_BONUS = 1e4
N_GROUPS, EXPERTS_PER_GROUP, TOP_K = 4, 8, 2
N_EXPERTS = N_GROUPS * EXPERTS_PER_GROUP

LANES = 128
VMEM_LIMIT_BYTES = 56 * 1024 * 1024
MOE_ROWS = 256
FLASH_ROW_CHUNK = 128
LOG2E = math.log2(math.e)


def _params(*sem):
    return pltpu.CompilerParams(dimension_semantics=sem, vmem_limit_bytes=VMEM_LIMIT_BYTES)


def _tile(n, pref):
    t = min(n, pref)
    while n % t:
        t -= 1
    return t


def _mm_body(*refs, nx, has_gain, emit_xn, mode):
    it = iter(refs)
    x_refs = [next(it) for _ in range(nx)]
    g_ref = next(it) if has_gain else None
    w_refs = [next(it) for _ in range(nx)]
    r_ref = next(it) if mode in ("resid", "ple") else None
    p_ref = next(it) if mode == "ple" else None
    wp_ref = next(it) if mode == "ple" else None
    o_ref = next(it)
    xo_ref = next(it) if emit_xn else None
    xn_ref = next(it) if has_gain else None

    if has_gain:
        @pl.when(pl.program_id(1) == 0)
        def _():
            x = x_refs[0][...].astype(F32)
            y = x * lax.rsqrt(jnp.mean(x * x, axis=-1, keepdims=True) + NORM_EPS) * g_ref[...]
            xn_ref[...] = y.astype(BF16)
            if emit_xn:
                xo_ref[...] = y
        lhs = [xn_ref[...]]
    else:
        lhs = [x_ref[...].astype(BF16) for x_ref in x_refs]
    acc = None
    for a, w_ref in zip(lhs, w_refs):
        d = jnp.dot(a, w_ref[...], preferred_element_type=F32)
        acc = d if acc is None else acc + d
    if mode == "resid":
        acc = r_ref[...] + acc
    elif mode == "ple":
        pp = jnp.dot(p_ref[...].astype(BF16), wp_ref[...], preferred_element_type=F32)
        acc = r_ref[...] + pp * jax.nn.sigmoid(acc)
    o_ref[...] = acc.astype(o_ref.dtype)


def _mm(xs, ws, *, name, gain=None, out_dtype=F32, tm=1024, tn=512, resid=None, ple=None,
        emit_xn=False):
    M = xs[0][0].shape[0]
    N = ws[0].shape[1]
    tm, tn = _tile(M, tm), _tile(N, tn)
    nx = len(xs)
    has_gain = gain is not None
    mode = "ple" if ple is not None else ("resid" if resid is not None else "none")
    args, in_specs = [], []
    for arr, cb, K in xs:
        args.append(arr)
        in_specs.append(pl.BlockSpec((tm, K), lambda i, j, cb=cb: (i, cb)))
    if has_gain:
        K0 = xs[0][2]
        args.append(gain.reshape(1, K0).astype(F32))
        in_specs.append(pl.BlockSpec((1, K0), lambda i, j: (0, 0)))
    for (arr, cb, K), w in zip(xs, ws):
        args.append(w)
        in_specs.append(pl.BlockSpec((K, tn), lambda i, j: (0, j)))
    if mode == "resid":
        args.append(resid)
        in_specs.append(pl.BlockSpec((tm, tn), lambda i, j: (i, j)))
    if mode == "ple":
        r, p, wp = ple
        args += [r, p, wp]
        in_specs += [pl.BlockSpec((tm, tn), lambda i, j: (i, j)),
                     pl.BlockSpec((tm, p.shape[1]), lambda i, j: (i, 0)),
                     pl.BlockSpec((p.shape[1], tn), lambda i, j: (0, j))]
    out_shape = [jax.ShapeDtypeStruct((M, N), out_dtype)]
    out_specs = [pl.BlockSpec((tm, tn), lambda i, j: (i, j))]
    if emit_xn:
        out_shape.append(jax.ShapeDtypeStruct((M, xs[0][2]), F32))
        out_specs.append(pl.BlockSpec((tm, xs[0][2]), lambda i, j: (i, 0)))
    scratch = [pltpu.VMEM((tm, xs[0][2]), BF16)] if has_gain else []
    res = pl.pallas_call(
        functools.partial(_mm_body, nx=nx, has_gain=has_gain, emit_xn=emit_xn, mode=mode),
        out_shape=out_shape, grid=(M // tm, N // tn), in_specs=in_specs, out_specs=out_specs,
        scratch_shapes=scratch, compiler_params=_params("arbitrary", "arbitrary"), name=name,
    )(*args)
    return res if emit_xn else res[0]


def _rope_tables(positions):
    def tables(dim):
        inv = 1.0 / (ROPE_THETA ** (jnp.arange(0, dim, 2, dtype=F32) / dim))
        ang = positions.astype(F32)[..., None] * inv
        return jnp.cos(ang), jnp.sin(ang)

    lead = positions.shape
    c, s = tables(HEAD_DIM)
    c128 = jnp.concatenate([c, c], -1)
    s128 = jnp.concatenate([-s, s], -1)
    c, s = tables(B_ROPE)
    z = jnp.zeros_like(c)
    c64 = jnp.concatenate([c, c, z, z], -1)
    s64a = jnp.concatenate([-s, z, z, z], -1)
    s64b = jnp.concatenate([z, s, z, z], -1)
    n = int(np.prod(lead))
    return [t.reshape(n, LANES) for t in (c128, s128, c64, s64a, s64b)]


def _rot128(x, c, s):
    return x * c + pltpu.roll(x, 64, 1) * s


def _rot64(x, c, sa, sb):
    return x * c + pltpu.roll(x, 96, 1) * sa + pltpu.roll(x, 32, 1) * sb


def _prep_body(y_ref, c128, s128, c64, s64a, s64b, *o_refs, program):
    for src, mode, scale, oi, dst in program:
        x = y_ref[:, src * LANES:(src + 1) * LANES].astype(F32)
        if mode == "rot128":
            x = _rot128(x, c128[...], s128[...])
        elif mode == "rot64":
            x = _rot64(x, c64[...], s64a[...], s64b[...])
        if scale != 1.0:
            x = x * scale
        o_refs[oi][:, dst * LANES:(dst + 1) * LANES] = x.astype(o_refs[oi].dtype)


def _prep(y, tables, program, outs, *, name, tm=256):
    M, C = y.shape
    tm = _tile(M, tm)
    row = lambda i: (i, 0)
    return pl.pallas_call(
        functools.partial(_prep_body, program=tuple(program)),
        out_shape=[jax.ShapeDtypeStruct((M, nb * LANES), dt) for nb, dt in outs],
        grid=(M // tm,),
        in_specs=[pl.BlockSpec((tm, C), row)] + [pl.BlockSpec((tm, LANES), row)] * 5,
        out_specs=[pl.BlockSpec((tm, nb * LANES), row) for nb, _ in outs],
        compiler_params=_params("arbitrary"), name=name,
    )(y, *tables)


def _pair_tables(S, tq, tk, window):
    qi_l, ki_l, fl_l, mk_l, masks, ids = [], [], [], [], [], {}
    for qi in range(S // tq):
        q0, q1 = qi * tq, (qi + 1) * tq - 1
        ks = []
        for ki in range(S // tk):
            k0, k1 = ki * tk, (ki + 1) * tk - 1
            if k0 > q1 or (window is not None and q0 - k1 >= window):
                continue
            full = k1 <= q0 and (window is None or q1 - k0 < window)
            mid = 0
            if not full:
                delta = q0 - k0
                if delta not in ids:
                    rel = np.arange(tq)[:, None] + delta - np.arange(tk)[None, :]
                    ok = (rel >= 0) if window is None else ((rel >= 0) & (rel < window))
                    masks.append(np.where(ok, 0.0, NEG_INF).astype(np.float32))
                    ids[delta] = len(masks)
                mid = ids[delta]
            ks.append((ki, mid))
        for n, (ki, mid) in enumerate(ks):
            qi_l.append(qi)
            ki_l.append(ki)
            fl_l.append((1 if n == 0 else 0) | (2 if n == len(ks) - 1 else 0))
            mk_l.append(mid)
    if not masks:
        masks.append(np.zeros((tq, tk), np.float32))
    tabs = [jnp.asarray(np.array(a, np.int32)) for a in (qi_l, ki_l, fl_l, mk_l)]
    return tabs, jnp.asarray(np.stack(masks))


def _flash_body(qi_t, ki_t, fl_t, mk_t, *refs, nq, nkv, q_src, k_src, v_src, tq, rc, has_sink,
                has_bias, has_sel):
    nc = len(q_src)
    it = iter(refs)
    q_refs = [next(it) for _ in range(nq)]
    kv_refs = [next(it) for _ in range(nkv)]
    mask_ref = next(it)
    sink_ref = next(it) if has_sink else None
    cum_ref = next(it) if has_bias else None
    cumt_ref = next(it) if has_bias else None
    sel_ref = next(it) if has_sel else None
    exp_ref = next(it) if has_sel else None
    o_ref = next(it)
    m_sc, acc_sc = next(it), next(it)
    cq_sc = next(it) if has_bias else None

    step_id = pl.program_id(2)
    fl, mk = fl_t[step_id], mk_t[step_id]

    def cat(refs_, src, rows=slice(None)):
        xs = [refs_[pi][0, rows, off:off + LANES] for pi, off in src]
        return xs[0] if len(xs) == 1 else jnp.concatenate(xs, axis=-1)

    @pl.when((fl & 1) != 0)
    def _init():
        m_sc[...] = jnp.full(m_sc.shape, NEG_INF, F32)
        acc_sc[...] = jnp.zeros(acc_sc.shape, F32)
        if has_bias:
            lane = lax.broadcasted_iota(jnp.int32, (tq, LANES), 1)
            for g in range(nc):
                head = pl.program_id(1) * nc + g
                cq_sc[g] = jnp.sum(jnp.where(lane == head, cum_ref[0], 0.0), axis=-1, keepdims=True)

    tk = kv_refs[0].shape[1]
    reps = tk // LANES
    ones = jnp.ones((tk, LANES), BF16)

    def step(masked):
        ks = [cat(kv_refs, k_src[g]) for g in range(nc)]
        vs = [jnp.concatenate([kv_refs[pi][0, :, off:off + LANES], ones], axis=1) for pi, off in v_src]
        for r in range(tq // rc):
            rows = slice(r * rc, (r + 1) * rc)
            add = mask_ref[mk - 1, rows, :] if masked else None
            if has_sel:
                hidden = (jnp.dot(sel_ref[0, 0, rows, :], exp_ref[0], preferred_element_type=F32) - 1.0) * (-NEG_INF)
                add = hidden if add is None else add + hidden
            for g in range(nc):
                q = cat(q_refs, q_src[g], rows)
                s = lax.dot_general(q, ks[g], (((1,), (1,)), ((), ())), preferred_element_type=F32)
                if has_bias:
                    s = s + (cq_sc[g, rows, :] - cumt_ref[0, g])
                if add is not None:
                    s = s + add
                m_prev = m_sc[g, rows, :]
                m_new = jnp.maximum(m_prev, jnp.max(s, axis=-1, keepdims=True))
                p = jnp.exp2(s - (jnp.concatenate([m_new] * reps, axis=1) if reps > 1 else m_new))
                alpha = jnp.exp2(m_prev - m_new)
                acc_sc[g, rows, :] = (jnp.concatenate([alpha, alpha], axis=1) * acc_sc[g, rows, :]
                                      + jnp.dot(p.astype(BF16), vs[g], preferred_element_type=F32))
                m_sc[g, rows, :] = m_new

    @pl.when(mk != 0)
    def _():
        step(True)

    @pl.when(mk == 0)
    def _():
        step(False)

    @pl.when((fl & 2) != 0)
    def _finish():
        for g in range(nc):
            m, acc, l = m_sc[g], acc_sc[g, :, :LANES], acc_sc[g, :, LANES:]
            if has_sink:
                sk = sink_ref[0, g:g + 1, 0:1] * LOG2E
                m_f = jnp.maximum(m, sk)
                w = jnp.exp2(m - m_f)
                l = l * w + jnp.exp2(sk - m_f)
                acc = acc * w
            o_ref[0, :, g * LANES:(g + 1) * LANES] = (acc / l).astype(o_ref.dtype)


def _flash(q_parts, kv_parts, q_src, k_src, v_src, *, name, B, S, n_steps, tq, tk, window=None, sinks=None,
           cum=None, cumt=None, sel=None, out_dtype=BF16, rc=FLASH_ROW_CHUNK):
    nc = len(q_src)
    tq, tk = _tile(S, tq), _tile(S, tk)
    tabs, masks = _pair_tables(S, tq, tk, window)
    npairs = int(tabs[0].shape[0])
    args, in_specs = [], []
    for arr, width, cf in q_parts:
        args.append(arr)
        in_specs.append(pl.BlockSpec((1, tq, width), lambda b, h, s, qt, kt, ft, mt, cf=cf: (b, qt[s], cf(h))))
    for arr, width, cf in kv_parts:
        args.append(arr)
        in_specs.append(pl.BlockSpec((1, tk, width), lambda b, h, s, qt, kt, ft, mt, cf=cf: (b, kt[s], cf(h))))
    args.append(masks)
    in_specs.append(pl.BlockSpec(masks.shape, lambda b, h, s, qt, kt, ft, mt: (0, 0, 0)))
    if sinks is not None:
        args.append(jnp.broadcast_to(sinks.astype(F32).reshape(n_steps, nc, 1), (n_steps, nc, LANES)))
        in_specs.append(pl.BlockSpec((1, nc, LANES), lambda b, h, s, qt, kt, ft, mt: (h, 0, 0)))
    if cum is not None:
        args += [cum, cumt]
        in_specs += [pl.BlockSpec((1, tq, LANES), lambda b, h, s, qt, kt, ft, mt: (b, qt[s], 0)),
                     pl.BlockSpec((1, nc, 1, tk), lambda b, h, s, qt, kt, ft, mt: (b, h, 0, kt[s]))]
    if sel is not None:
        per = tk // D_SLC_LEN
        e = np.zeros((S // tk, LANES, tk), np.float32)
        for ki in range(S // tk):
            e[ki, ki * per + np.arange(tk) // D_SLC_LEN, np.arange(tk)] = 1.0
        args += [sel, jnp.asarray(e, BF16)]
        in_specs += [pl.BlockSpec((1, 1, tq, LANES), lambda b, h, s, qt, kt, ft, mt: (b, h, qt[s], 0)),
                     pl.BlockSpec((1, LANES, tk), lambda b, h, s, qt, kt, ft, mt: (kt[s], 0, 0))]
    scratch = [pltpu.VMEM((nc, tq, LANES), F32), pltpu.VMEM((nc, tq, 2 * LANES), F32)]
    if cum is not None:
        scratch.append(pltpu.VMEM((nc, tq, 1), F32))
    body = functools.partial(_flash_body, nq=len(q_parts), nkv=len(kv_parts), q_src=q_src, k_src=k_src,
                             v_src=v_src, tq=tq, rc=_tile(tq, rc), has_sink=sinks is not None,
                             has_bias=cum is not None, has_sel=sel is not None)
    return pl.pallas_call(
        body, out_shape=jax.ShapeDtypeStruct((B, S, n_steps * nc * LANES), out_dtype),
        grid_spec=pltpu.PrefetchScalarGridSpec(
            num_scalar_prefetch=4, grid=(B, n_steps, npairs), in_specs=in_specs,
            out_specs=pl.BlockSpec((1, tq, nc * LANES), lambda b, h, s, qt, kt, ft, mt: (b, qt[s], h)),
            scratch_shapes=scratch),
        compiler_params=_params("arbitrary", "arbitrary", "arbitrary"), name=name,
    )(*tabs, *args)


def _gqa_flash(q, k, v, *, name, B, S, Hk, G, **kw):
    head = lambda h: h
    return _flash([(q, G * LANES, head)], [(k, LANES, head), (v, LANES, head)],
                  [[(0, g * LANES)] for g in range(G)], [[(0, 0)]] * G, [(1, 0)] * G,
                  name=name, B=B, S=S, n_steps=Hk, **kw)


def _cum_body(y_ref, b_ref, tri_ref, o_ref, carry):
    @pl.when(pl.program_id(1) == 0)
    def _():
        carry[...] = jnp.zeros(carry.shape, F32)
    x = y_ref[0] + b_ref[...]
    logf = jnp.minimum(x, 0.0) - jnp.log1p(jnp.exp(-jnp.abs(x)))
    cum = jnp.dot(tri_ref[...], logf, preferred_element_type=F32, precision=lax.Precision.HIGHEST) + carry[...]
    o_ref[0] = cum * LOG2E
    carry[...] = cum[-1:, :]


def _forget_cum(y3, col_block, bias_row, *, ts=512):
    B, S, _ = y3.shape
    ts = _tile(S, ts)
    tri = jnp.asarray(np.tril(np.ones((ts, ts), np.float32)))
    return pl.pallas_call(
        _cum_body, out_shape=jax.ShapeDtypeStruct((B, S, LANES), F32), grid=(B, S // ts),
        in_specs=[pl.BlockSpec((1, ts, LANES), lambda b, s: (b, s, col_block)),
                  pl.BlockSpec((1, LANES), lambda b, s: (0, 0)),
                  pl.BlockSpec((ts, ts), lambda b, s: (0, 0))],
        out_specs=pl.BlockSpec((1, ts, LANES), lambda b, s: (b, s, 0)),
        scratch_shapes=[pltpu.VMEM((1, LANES), F32)],
        compiler_params=_params("arbitrary", "arbitrary"), name="forget_cum",
    )(y3, bias_row, tri)


def _compress_body(*refs, rope, nc):
    if rope:
        z_ref, pe_ref, w1_ref, w2_ref, c_ref, s_ref, o_ref = refs
    else:
        z_ref, pe_ref, w1_ref, w2_ref, o_ref = refs
    half = D_CMP_LEN // 2
    width = D_KV_HEADS * HEAD_DIM
    for hk in range(D_KV_HEADS):
        u = jnp.zeros((nc, w1_ref.shape[1]), F32)
        v = jnp.zeros((nc, w1_ref.shape[1]), F32)
        for l in range(half):
            z = z_ref[0, :, l * width + hk * HEAD_DIM:l * width + (hk + 1) * HEAD_DIM]
            zu = (z + pe_ref[l:l + 1, :]).astype(BF16)
            zv = (z + pe_ref[half + l:half + l + 1, :]).astype(BF16)
            u = u + jnp.dot(zu, w1_ref[l * HEAD_DIM:(l + 1) * HEAD_DIM, :], preferred_element_type=F32)
            v = v + jnp.dot(zv, w1_ref[(half + l) * HEAD_DIM:(half + l + 1) * HEAD_DIM, :],
                            preferred_element_type=F32)
        pre = u + pltpu.roll(v, nc - 1, 0)
        hid = jax.nn.gelu(pre, approximate=True)
        out = jnp.dot(hid.astype(BF16), w2_ref[...], preferred_element_type=F32)
        if rope:
            out = _rot128(out, c_ref[0], s_ref[0])
        o_ref[0, hk] = out.astype(o_ref.dtype)


def _compress(z, pe, w1, w2, rope_tabs=None):
    B, S, W = z.shape
    nc = S // D_CMP_STRIDE
    zc = z.reshape(B, nc, D_CMP_STRIDE * W)
    args = [zc, pe.astype(F32), w1.astype(BF16), w2.astype(BF16)]
    in_specs = [pl.BlockSpec((1, nc, D_CMP_STRIDE * W), lambda b: (b, 0, 0)),
                pl.BlockSpec(pe.shape, lambda b: (0, 0)),
                pl.BlockSpec(w1.shape, lambda b: (0, 0)),
                pl.BlockSpec(w2.shape, lambda b: (0, 0))]
    if rope_tabs is not None:
        args += list(rope_tabs)
        in_specs += [pl.BlockSpec((1, nc, LANES), lambda b: (b, 0, 0))] * 2
    return pl.pallas_call(
        functools.partial(_compress_body, rope=rope_tabs is not None, nc=nc),
        out_shape=jax.ShapeDtypeStruct((B, D_KV_HEADS, nc, HEAD_DIM), BF16), grid=(B,),
        in_specs=in_specs, out_specs=pl.BlockSpec((1, D_KV_HEADS, nc, HEAD_DIM), lambda b: (b, 0, 0, 0)),
        compiler_params=_params("arbitrary"), name="nsa_compress",
    )(*args)


def _cmp_attn_body(q_ref, k_ref, v_ref, ov_ref, o_ref, sel_ref, *, G, tq, nc, n_cmp, n_slc, topn):
    qi = pl.program_id(2)
    t = qi * tq + lax.broadcasted_iota(jnp.int32, (tq, nc), 0)
    c = lax.broadcasted_iota(jnp.int32, (tq, nc), 1)
    valid = (c * D_CMP_STRIDE + (D_CMP_LEN - 1) <= t) & (c < n_cmp)
    k = k_ref[0, 0]
    v = v_ref[0, 0]
    psum = jnp.zeros((tq, nc), F32)
    for g in range(G):
        q = q_ref[0, :, g * LANES:(g + 1) * LANES]
        s = lax.dot_general(q, k, (((1,), (1,)), ((), ())), preferred_element_type=F32)
        s = jnp.where(valid, s, NEG_INF)
        e = jnp.where(valid, jnp.exp2(s - jnp.max(s, axis=-1, keepdims=True)), 0.0)
        p = e / jnp.maximum(jnp.sum(e, axis=-1, keepdims=True), jnp.finfo(F32).tiny)
        o_ref[0, :, g * LANES:(g + 1) * LANES] = jnp.dot(
            p.astype(BF16), v, preferred_element_type=F32).astype(o_ref.dtype)
        psum = psum + p
    imp = jnp.dot(psum, ov_ref[...], preferred_element_type=F32, precision=lax.Precision.HIGHEST)
    lane = lax.broadcasted_iota(jnp.int32, (tq, LANES), 1)
    trow = qi * tq + lax.broadcasted_iota(jnp.int32, (tq, LANES), 0)
    cur = jnp.right_shift(trow, int(math.log2(D_SLC_LEN)))
    forced = (lane == 0) | (lane == cur) | (lane == cur - 1)
    imp = jnp.where(lane * D_SLC_LEN > trow, NEG_INF, imp + jnp.where(forced, FORCE_BONUS, 0.0))
    imp = jnp.where(lane >= n_slc, TAKEN, imp)
    chosen = jnp.zeros((tq, LANES), F32)
    for _ in range(topn):
        mx = jnp.max(imp, axis=-1, keepdims=True)
        idx = jnp.min(jnp.where(imp == mx, lane, LANES), axis=-1, keepdims=True)
        hit = lane == idx
        chosen = jnp.where(hit, 1.0, chosen)
        imp = jnp.where(hit, TAKEN, imp)
    sel_ref[0, 0] = chosen.astype(sel_ref.dtype)


def _cmp_attn(q, k_cmp, v_cmp, *, tq=256):
    B, S, _ = q.shape
    Hk, G = D_KV_HEADS, D_HEADS // D_KV_HEADS
    nc = S // D_CMP_STRIDE
    n_cmp = (S - D_CMP_LEN) // D_CMP_STRIDE + 1
    n_slc = S // D_SLC_LEN
    tq = _tile(S, tq)
    c0 = np.arange(nc) * D_CMP_STRIDE
    s0 = np.arange(LANES) * D_SLC_LEN
    ov = ((c0[:, None] < (s0 + D_SLC_LEN)[None, :]) & ((c0 + D_CMP_LEN)[:, None] > s0[None, :])
          & (np.arange(nc) < n_cmp)[:, None] & (np.arange(LANES) < n_slc)[None, :]).astype(np.float32)
    body = functools.partial(_cmp_attn_body, G=G, tq=tq, nc=nc, n_cmp=n_cmp, n_slc=n_slc,
                             topn=min(D_SLC_TOPN, n_slc))
    return pl.pallas_call(
        body,
        out_shape=[jax.ShapeDtypeStruct((B, S, Hk * G * LANES), F32),
                   jax.ShapeDtypeStruct((B, Hk, S, LANES), BF16)],
        grid=(B, Hk, S // tq),
        in_specs=[pl.BlockSpec((1, tq, G * LANES), lambda b, h, i: (b, i, h)),
                  pl.BlockSpec((1, 1, nc, LANES), lambda b, h, i: (b, h, 0, 0)),
                  pl.BlockSpec((1, 1, nc, LANES), lambda b, h, i: (b, h, 0, 0)),
                  pl.BlockSpec((nc, LANES), lambda b, h, i: (0, 0))],
        out_specs=[pl.BlockSpec((1, tq, G * LANES), lambda b, h, i: (b, i, h)),
                   pl.BlockSpec((1, 1, tq, LANES), lambda b, h, i: (b, h, i, 0))],
        compiler_params=_params("arbitrary", "arbitrary", "arbitrary"), name="nsa_cmp_attn",
    )(q, k_cmp, v_cmp, jnp.asarray(ov))


def _gate_body(y_ref, a_ref, b_ref, c_ref, o_ref, *, lane0):
    g = jax.nn.sigmoid(y_ref[...])
    for h in range(D_HEADS):
        cols = slice(h * LANES, (h + 1) * LANES)
        ga = g[:, lane0 + h:lane0 + h + 1]
        gb = g[:, lane0 + D_HEADS + h:lane0 + D_HEADS + h + 1]
        gc = g[:, lane0 + 2 * D_HEADS + h:lane0 + 2 * D_HEADS + h + 1]
        o_ref[:, cols] = (ga * a_ref[:, cols] + gb * b_ref[:, cols] + gc * c_ref[:, cols]).astype(o_ref.dtype)


def _nsa_gate(y, col_block, lane0, o_cmp, o_slc, o_win, *, tm=512):
    M = y.shape[0]
    W = o_cmp.shape[1]
    tm = _tile(M, tm)
    row = lambda i: (i, 0)
    return pl.pallas_call(
        functools.partial(_gate_body, lane0=lane0), out_shape=jax.ShapeDtypeStruct((M, W), BF16),
        grid=(M // tm,),
        in_specs=[pl.BlockSpec((tm, LANES), lambda i: (i, col_block))] + [pl.BlockSpec((tm, W), row)] * 3,
        out_specs=pl.BlockSpec((tm, W), row), compiler_params=_params("arbitrary"), name="nsa_gate",
    )(y, o_cmp, o_slc, o_win)


def _route_body(lg_ref, b_ref, tri_ref, e_ref, w_ref, pos_ref, cnt_ref, carry, *, tm):
    @pl.when(pl.program_id(0) == 0)
    def _():
        carry[...] = jnp.zeros(carry.shape, F32)
    lane = lax.broadcasted_iota(jnp.int32, (tm, LANES), 1)
    logits = lg_ref[...] + b_ref[...]
    gl = jnp.where(lane < N_GROUPS, logits, -jnp.inf)
    gmax = jnp.max(gl, axis=-1, keepdims=True)
    g_val = 1.0 / jnp.sum(jnp.exp(gl - gmax), axis=-1, keepdims=True)
    g_idx = jnp.min(jnp.where(gl == gmax, lane, LANES), axis=-1, keepdims=True)
    lo = N_GROUPS + EXPERTS_PER_GROUP * g_idx
    el = jnp.where((lane >= lo) & (lane < lo + EXPERTS_PER_GROUP), logits, -jnp.inf)
    e1 = jnp.max(el, axis=-1, keepdims=True)
    i1 = jnp.min(jnp.where(el == e1, lane, LANES), axis=-1, keepdims=True)
    el2 = jnp.where(lane == i1, -jnp.inf, el)
    e2 = jnp.max(el2, axis=-1, keepdims=True)
    i2 = jnp.min(jnp.where(el2 == e2, lane, LANES), axis=-1, keepdims=True)
    r = jnp.exp(e2 - e1)
    w1 = g_val / (1.0 + r)
    w2 = w1 * r
    x1, x2 = i1 - N_GROUPS, i2 - N_GROUPS
    e_ref[...] = jnp.where(lane == 0, x1, jnp.where(lane == 1, x2, 0))
    w_ref[...] = jnp.where(lane == 0, w1, jnp.where(lane == 1, w2, 0.0))
    hot1 = lane == x1
    hot2 = lane == x2
    both = jnp.where(hot1 | hot2, 1.0, 0.0)
    before = jnp.dot(tri_ref[...], both.astype(BF16), preferred_element_type=F32) + carry[...]
    p1 = jnp.sum(jnp.where(hot1, before, 0.0), axis=-1, keepdims=True)
    p2 = jnp.sum(jnp.where(hot2, before, 0.0), axis=-1, keepdims=True)
    pos_ref[...] = jnp.where(lane == 0, p1, jnp.where(lane == 1, p2, 0.0)).astype(jnp.int32)
    carry[...] = carry[...] + jnp.sum(both, axis=0, keepdims=True)
    cnt_ref[...] = carry[...].astype(jnp.int32)


def _route(logits, bias_row, *, tm=512):
    T = logits.shape[0]
    tm = _tile(T, tm)
    tri = jnp.asarray(np.tril(np.ones((tm, tm), np.float32), -1), BF16)
    row = lambda i: (i, 0)
    fixed = lambda i: (0, 0)
    return pl.pallas_call(
        functools.partial(_route_body, tm=tm),
        out_shape=[jax.ShapeDtypeStruct((T, LANES), jnp.int32), jax.ShapeDtypeStruct((T, LANES), F32),
                   jax.ShapeDtypeStruct((T, LANES), jnp.int32), jax.ShapeDtypeStruct((1, LANES), jnp.int32)],
        grid=(T // tm,),
        in_specs=[pl.BlockSpec((tm, LANES), row), pl.BlockSpec((1, LANES), fixed), pl.BlockSpec((tm, tm), fixed)],
        out_specs=[pl.BlockSpec((tm, LANES), row)] * 3 + [pl.BlockSpec((1, LANES), fixed)],
        scratch_shapes=[pltpu.VMEM((1, LANES), F32)],
        compiler_params=_params("arbitrary"), name="moe_route",
    )(logits, bias_row, tri)


def _dispatch_body(d_ref, x_ref, _, out_ref, sem, *, tm):
    base = pl.program_id(0) * tm

    def issue(r, c):
        for k in range(TOP_K):
            row = d_ref[(base + r) * TOP_K + k]
            pltpu.make_async_copy(x_ref.at[pl.ds(r, 1)], out_ref.at[pl.ds(row, 1)], sem).start()
        return c

    lax.fori_loop(0, tm, issue, 0)
    for _ in range(TOP_K):
        pltpu.make_async_copy(x_ref, out_ref.at[pl.ds(0, tm)], sem).wait()


def _dispatch(xn, dest, n_rows, *, tm=512):
    T, D = xn.shape
    tm = _tile(T, tm)
    return pl.pallas_call(
        functools.partial(_dispatch_body, tm=tm),
        out_shape=jax.ShapeDtypeStruct((n_rows, D), xn.dtype),
        grid_spec=pltpu.PrefetchScalarGridSpec(
            num_scalar_prefetch=1, grid=(T // tm,),
            in_specs=[pl.BlockSpec((tm, D), lambda i, d: (i, 0)), pl.BlockSpec(memory_space=pl.ANY)],
            out_specs=pl.BlockSpec(memory_space=pl.ANY),
            scratch_shapes=[pltpu.SemaphoreType.DMA(())]),
        input_output_aliases={2: 0},
        compiler_params=pltpu.CompilerParams(dimension_semantics=("arbitrary",), has_side_effects=True,
                                             vmem_limit_bytes=VMEM_LIMIT_BYTES),
        name="moe_dispatch",
    )(dest, xn, jnp.zeros((n_rows, D), xn.dtype))


def _expert_body(be_ref, nu_ref, x_ref, wg_ref, wu_ref, wd_ref, o_ref, wg_sc, wu_sc, wd_sc):
    i = pl.program_id(0)
    fresh = (i == 0) | (be_ref[i] != be_ref[jnp.maximum(i - 1, 0)])

    @pl.when(fresh)
    def _():
        wg_sc[...] = wg_ref[0, 0].astype(BF16)
        wu_sc[...] = wu_ref[0, 0].astype(BF16)
        wd_sc[...] = wd_ref[0, 0].astype(BF16)

    @pl.when(i < nu_ref[0])
    def _():
        x = x_ref[...].astype(BF16)
        gate = jnp.dot(x, wg_sc[...], preferred_element_type=F32)
        up = jnp.dot(x, wu_sc[...], preferred_element_type=F32)
        hid = (gate * jax.nn.sigmoid(gate) * up).astype(BF16)
        o_ref[...] = jnp.dot(hid, wd_sc[...], preferred_element_type=F32)

    @pl.when(i >= nu_ref[0])
    def _():
        o_ref[...] = jnp.zeros(o_ref.shape, F32)


def _experts(xb, blk_e, n_used, layer, w_gate, w_up, w_down):
    D, Hd = w_gate.shape[2], w_gate.shape[3]
    nblk = xb.shape[0] // MOE_ROWS
    return pl.pallas_call(
        _expert_body, out_shape=jax.ShapeDtypeStruct(xb.shape, F32),
        grid_spec=pltpu.PrefetchScalarGridSpec(
            num_scalar_prefetch=2, grid=(nblk,),
            in_specs=[pl.BlockSpec((MOE_ROWS, D), lambda i, be, nu: (jnp.minimum(i, nu[0] - 1), 0)),
                      pl.BlockSpec((1, 1, D, Hd), lambda i, be, nu: (layer, be[i], 0, 0)),
                      pl.BlockSpec((1, 1, D, Hd), lambda i, be, nu: (layer, be[i], 0, 0)),
                      pl.BlockSpec((1, 1, Hd, D), lambda i, be, nu: (layer, be[i], 0, 0))],
            out_specs=pl.BlockSpec((MOE_ROWS, D), lambda i, be, nu: (i, 0)),
            scratch_shapes=[pltpu.VMEM((D, Hd), BF16), pltpu.VMEM((D, Hd), BF16), pltpu.VMEM((Hd, D), BF16)]),
        compiler_params=_params("arbitrary"), name="moe_experts",
    )(blk_e, n_used, xb, w_gate, w_up, w_down)


def _collect_body(d_ref, h_ref, w_ref, yb_ref, o_ref, buf_a, buf_b, sem, *, tm, nsteps):
    i = pl.program_id(0)
    slot = lax.rem(i, 2)

    def fetch(step, slot_):
        base = step * tm

        def issue(r, c):
            for k, buf in enumerate((buf_a, buf_b)):
                row = d_ref[(base + r) * TOP_K + k]
                pltpu.make_async_copy(yb_ref.at[pl.ds(row, 1)], buf.at[slot_, pl.ds(r, 1)], sem.at[slot_]).start()
            return c

        lax.fori_loop(0, tm, issue, 0)

    @pl.when(i == 0)
    def _():
        fetch(0, 0)

    @pl.when(i + 1 < nsteps)
    def _():
        fetch(i + 1, 1 - slot)

    for buf in (buf_a, buf_b):
        pltpu.make_async_copy(yb_ref.at[pl.ds(0, tm)], buf.at[slot], sem.at[slot]).wait()
    w = w_ref[...]
    o_ref[...] = h_ref[...] + w[:, 0:1] * buf_a[slot] + w[:, 1:2] * buf_b[slot]


def _collect(h, yb, dest, wts, *, tm=256):
    T, D = h.shape
    tm = _tile(T, tm)
    nsteps = T // tm
    return pl.pallas_call(
        functools.partial(_collect_body, tm=tm, nsteps=nsteps),
        out_shape=jax.ShapeDtypeStruct((T, D), F32),
        grid_spec=pltpu.PrefetchScalarGridSpec(
            num_scalar_prefetch=1, grid=(nsteps,),
            in_specs=[pl.BlockSpec((tm, D), lambda i, d: (i, 0)), pl.BlockSpec((tm, LANES), lambda i, d: (i, 0)),
                      pl.BlockSpec(memory_space=pl.ANY)],
            out_specs=pl.BlockSpec((tm, D), lambda i, d: (i, 0)),
            scratch_shapes=[pltpu.VMEM((2, tm, D), F32), pltpu.VMEM((2, tm, D), F32),
                            pltpu.SemaphoreType.DMA((2,))]),
        compiler_params=_params("arbitrary"), name="moe_collect",
    )(dest, h, wts, yb)


def _moe(h, norm_g, w_group, b_group, w_expert, b_expert, layer, w_gate, w_up, w_down):
    T, D = h.shape
    pad = LANES - N_GROUPS - N_EXPERTS
    w_r = jnp.concatenate([w_group, w_expert, jnp.zeros((D, pad), F32)], axis=1).astype(BF16)
    b_r = jnp.concatenate([b_group, b_expert, jnp.zeros((pad,), F32)]).astype(F32).reshape(1, LANES)
    logits, xn = _mm([(h, 0, D)], [w_r], gain=norm_g, emit_xn=True, tm=512, tn=LANES, name="moe_router")
    eid, wts, pos, cnt = _route(logits, b_r)
    counts = cnt[0, :N_EXPERTS]
    padded = (counts + MOE_ROWS - 1) // MOE_ROWS * MOE_ROWS
    ends = jnp.cumsum(padded)
    offs = ends - padded
    dest = (offs[eid[:, :TOP_K]] + pos[:, :TOP_K]).reshape(T * TOP_K).astype(jnp.int32)
    P = T * TOP_K + N_EXPERTS * MOE_ROWS
    nblk = P // MOE_ROWS
    starts = jnp.arange(nblk, dtype=jnp.int32) * MOE_ROWS
    blk_e = jnp.minimum(jnp.sum(ends[None, :] <= starts[:, None], axis=1), N_EXPERTS - 1).astype(jnp.int32)
    n_used = (ends[-1:] // MOE_ROWS).astype(jnp.int32)
    xb = _dispatch(xn, dest, P)
    yb = _experts(xb, blk_e, n_used, layer, w_gate, w_up, w_down)
    return _collect(h, yb, dest, wts)


def _pad_cols(w, n):
    return jnp.pad(w, ((0, 0), (0, n - w.shape[1])))


def _layer_even(h, B, S, tabs, norm_g, w_in, sinks, q_lat_norm, kv_lat_norm, w_uq, w_ukv, w_o):
    T, D = h.shape
    n_in = 2560
    y = _mm([(h, 0, D)], [_pad_cols(w_in, n_in).astype(BF16)], gain=norm_g, tn=512, name="ab_in")
    sa = HEAD_DIM ** -0.5 * LOG2E
    prog = ([(i, "rot128", sa, 0, i) for i in range(8)] + [(8 + i, "rot128", 1.0, 1, i) for i in range(2)]
            + [(10 + i, "copy", 1.0, 2, i) for i in range(2)] + [(18, "rot64", 1.0, 3, 0)])
    q_a, k_a, v_a, k_r = _prep(y, tabs, prog, [(8, BF16), (2, BF16), (2, BF16), (1, BF16)], name="ab_prep")
    wq = w_uq.reshape(B_Q_LORA, B_HEADS, B_NOPE + B_ROPE)
    wq_n = wq[:, :, :B_NOPE].reshape(B_Q_LORA, B_HEADS * B_NOPE)
    wq_r = jnp.pad(wq[:, :, B_NOPE:], ((0, 0), (0, 0), (0, LANES - B_ROPE))).reshape(B_Q_LORA, B_HEADS * LANES)
    qb = _mm([(y, 1536 // B_Q_LORA, B_Q_LORA)], [jnp.concatenate([wq_n, wq_r], 1).astype(BF16)],
             gain=q_lat_norm, tn=512, name="mla_uq")
    sb = (B_NOPE + B_ROPE) ** -0.5 * LOG2E
    prog = [(i, "copy", sb, 0, i) for i in range(8)] + [(8 + i, "rot64", sb, 1, i) for i in range(8)]
    q_n, q_r = _prep(qb, tabs, prog, [(8, BF16), (8, BF16)], name="mla_prep")
    kv = _mm([(y, 2048 // B_KV_LORA, B_KV_LORA)], [w_ukv.astype(BF16)], gain=kv_lat_norm, out_dtype=BF16,
             tn=512, name="mla_ukv")
    r3 = lambda a: a.reshape(B, S, a.shape[-1])
    o_a = _gqa_flash(r3(q_a), r3(k_a), r3(v_a), name="swa_attn", B=B, S=S, Hk=A_KV_HEADS,
                     G=A_HEADS // A_KV_HEADS, window=A_WINDOW, sinks=sinks, tq=256, tk=256)
    hp = 2
    head = lambda h: h
    o_b = _flash([(r3(q_n), hp * LANES, head), (r3(q_r), hp * LANES, head)],
                 [(r3(kv), hp * 2 * LANES, head), (r3(k_r), LANES, lambda h: 0)],
                 [[(0, g * LANES), (1, g * LANES)] for g in range(hp)],
                 [[(0, 2 * g * LANES), (1, 0)] for g in range(hp)], [(0, (2 * g + 1) * LANES) for g in range(hp)],
                 name="mla_attn", B=B, S=S, n_steps=B_HEADS // hp, tq=512, tk=512, rc=512)
    wo = w_o.astype(BF16)
    na = A_HEADS * HEAD_DIM
    return _mm([(o_a.reshape(T, -1), 0, na), (o_b.reshape(T, -1), 0, B_HEADS * B_V)], [wo[:na], wo[na:]],
               resid=h, tn=512, name="ab_out")


def _layer_odd(h, B, S, tabs, cmp_tabs, norm_g, w_in, forget_bias, pe_k, w1_k, w2_k, pe_v, w1_v, w2_v, w_o):
    T, D = h.shape
    hc, kvw = C_HEADS * HEAD_DIM, D_KV_HEADS * HEAD_DIM
    o_cf = 3 * hc
    o_dq = o_cf + C_HEADS
    o_dg = o_dq + D_HEADS * HEAD_DIM + 6 * kvw
    w_r = jnp.concatenate([w_in[:, :o_cf], w_in[:, o_dq:o_dg], w_in[:, o_cf:o_dq], w_in[:, o_dg:]], axis=1)
    n_in = 5760
    y = _mm([(h, 0, D)], [_pad_cols(w_r, n_in).astype(BF16)], gain=norm_g, tn=640, name="cd_in")
    misc = (n_in // LANES) - 1
    sc = HEAD_DIM ** -0.5 * LOG2E
    prog = ([(i, "copy", sc, 0, i) for i in range(8)] + [(8 + i, "copy", 1.0, 1, i) for i in range(8)]
            + [(16 + i, "copy", 1.0, 2, i) for i in range(8)] + [(24 + i, "rot128", sc, 3, i) for i in range(8)]
            + [(32 + i, "copy", 1.0, 4, i) for i in range(2)] + [(34 + i, "copy", 1.0, 5, i) for i in range(2)]
            + [(36 + i, "rot128", 1.0, 6, i) for i in range(2)] + [(38 + i, "copy", 1.0, 7, i) for i in range(2)]
            + [(40 + i, "rot128", 1.0, 8, i) for i in range(2)] + [(42 + i, "copy", 1.0, 9, i) for i in range(2)])
    outs = [(8, BF16)] * 4 + [(2, F32), (2, F32)] + [(2, BF16)] * 4
    q_c, k_c, v_c, q_d, z_kc, z_vc, k_s, v_s, k_w, v_w = _prep(y, tabs, prog, outs, name="cd_prep")
    r3 = lambda a: a.reshape(B, S, a.shape[-1])
    fb = jnp.pad(forget_bias.astype(F32), (0, LANES - C_HEADS)).reshape(1, LANES)
    cum = _forget_cum(r3(y), misc, fb)
    cumt = jnp.swapaxes(cum[:, :, :C_HEADS], 1, 2).reshape(B, C_HEADS, 1, S)
    hp = 2
    head = lambda h: h
    o_c = _flash([(r3(q_c), hp * LANES, head)], [(r3(k_c), hp * LANES, head), (r3(v_c), hp * LANES, head)],
                 [[(0, g * LANES)] for g in range(hp)], [[(0, g * LANES)] for g in range(hp)],
                 [(1, g * LANES) for g in range(hp)], name="fox_attn", B=B, S=S, n_steps=C_HEADS // hp,
                 tq=512, tk=512, cum=cum, cumt=cumt)
    G = D_HEADS // D_KV_HEADS
    k_cmp = _compress(r3(z_kc), pe_k, w1_k, w2_k, rope_tabs=cmp_tabs)
    v_cmp = _compress(r3(z_vc), pe_v, w1_v, w2_v)
    q3 = r3(q_d)
    o_cmp, sel = _cmp_attn(q3, k_cmp, v_cmp)
    o_slc = _gqa_flash(q3, r3(k_s), r3(v_s), name="nsa_slc_attn", B=B, S=S, Hk=D_KV_HEADS, G=G, sel=sel,
                       out_dtype=F32, tq=512, tk=512, rc=256)
    o_win = _gqa_flash(q3, r3(k_w), r3(v_w), name="nsa_win_attn", B=B, S=S, Hk=D_KV_HEADS, G=G,
                       window=D_WINDOW, out_dtype=F32, tq=256, tk=256)
    o_d = _nsa_gate(y, misc, C_HEADS, o_cmp.reshape(T, -1), o_slc.reshape(T, -1), o_win.reshape(T, -1))
    wo = w_o.astype(BF16)
    return _mm([(o_c.reshape(T, -1), 0, hc), (o_d, 0, D_HEADS * HEAD_DIM)], [wo[:hc], wo[hc:]],
               resid=h, tn=512, name="cd_out")


def _final_norm_body(x_ref, g_ref, o_ref):
    x = x_ref[...]
    o_ref[...] = x * lax.rsqrt(jnp.mean(x * x, axis=-1, keepdims=True) + NORM_EPS) * g_ref[...]


def _final_norm(h, g, *, tm=512):
    T, D = h.shape
    tm = _tile(T, tm)
    return pl.pallas_call(
        _final_norm_body, out_shape=jax.ShapeDtypeStruct((T, D), F32), grid=(T // tm,),
        in_specs=[pl.BlockSpec((tm, D), lambda i: (i, 0)), pl.BlockSpec((1, D), lambda i: (0, 0))],
        out_specs=pl.BlockSpec((tm, D), lambda i: (i, 0)), compiler_params=_params("arbitrary"),
        name="final_norm",
    )(h, g.reshape(1, D).astype(F32))


def kernel(x, p, positions, ab_w_in, ab_sinks, ab_q_lat_norm, ab_kv_lat_norm, ab_w_uq, ab_w_ukv, ab_w_o,
           cd_w_in, cd_forget_bias, cd_cmp_pe_k, cd_cmp_w1_k, cd_cmp_w2_k, cd_cmp_pe_v, cd_cmp_w1_v,
           cd_cmp_w2_v, cd_w_o, mixer_norm, moe_norm, router_group_w, router_group_b, router_expert_w,
           router_expert_b, expert_w_gate, expert_w_up, expert_w_down, ple_proj, ple_gate_norm, ple_gate_w,
           final_norm):
    B, S, D = x.shape
    T = B * S
    depth = p.shape[0]
    tabs = _rope_tables(positions)
    nc = S // D_CMP_STRIDE
    end = np.minimum(np.arange(nc) * D_CMP_STRIDE + D_CMP_LEN - 1, S - 1)
    cmp_tabs = [t.reshape(B, nc, LANES) for t in _rope_tables(positions[:, end])[:2]]
    h = x.reshape(T, D)
    for i in range(depth):
        j = i // 2
        if i % 2 == 0:
            h = _layer_even(h, B, S, tabs, mixer_norm[i], ab_w_in[j], ab_sinks[j], ab_q_lat_norm[j],
                            ab_kv_lat_norm[j], ab_w_uq[j], ab_w_ukv[j], ab_w_o[j])
        else:
            h = _layer_odd(h, B, S, tabs, cmp_tabs, mixer_norm[i], cd_w_in[j], cd_forget_bias[j],
                           cd_cmp_pe_k[j], cd_cmp_w1_k[j], cd_cmp_w2_k[j], cd_cmp_pe_v[j], cd_cmp_w1_v[j],
                           cd_cmp_w2_v[j], cd_w_o[j])
        h = _moe(h, moe_norm[i], router_group_w[i], router_group_b[i], router_expert_w[i], router_expert_b[i],
                 i, expert_w_gate, expert_w_up, expert_w_down)
        h = _mm([(h, 0, D)], [ple_gate_w[i].astype(BF16)], gain=ple_gate_norm[i], tn=512,
                ple=(h, p[i].reshape(T, -1), ple_proj[i].astype(BF16)), name="ple")
    return _final_norm(h, final_norm).reshape(B, S, D)
```

```python
import functools
import math

import numpy as np
import jax
import jax.numpy as jnp
from jax import lax
from jax.experimental import pallas as pl
from jax.experimental.pallas import tpu as pltpu

F32 = jnp.float32
BF16 = jnp.bfloat16

HEAD_DIM = 128
ROPE_THETA = 10000.0
NORM_EPS = 1e-6
NEG_INF = -1e30
TAKEN = -3e38
A_HEADS, A_KV_HEADS, A_WINDOW = 8, 2, 128
B_HEADS, B_Q_LORA, B_KV_LORA, B_NOPE, B_ROPE, B_V = 8, 512, 256, 128, 64, 128
C_HEADS = 8
D_HEADS, D_KV_HEADS = 8, 2
D_CMP_LEN, D_CMP_STRIDE, D_SLC_LEN, D_SLC_TOPN, D_WINDOW = 32, 16, 64, 8, 512
FORCE_BONUS = 1e4
N_GROUPS, EXPERTS_PER_GROUP, TOP_K = 4, 8, 2
N_EXPERTS = N_GROUPS * EXPERTS_PER_GROUP

LANES = 128
VMEM_LIMIT_BYTES = 56 * 1024 * 1024
MOE_ROWS = 256
FLASH_ROW_CHUNK = 128
LOG2E = math.log2(math.e)


def _params(*sem):
    return pltpu.CompilerParams(dimension_semantics=sem, vmem_limit_bytes=VMEM_LIMIT_BYTES)


def _tile(n, pref):
    t = min(n, pref)
    while n % t:
        t -= 1
    return t


def _mm_body(*refs, nx, has_gain, emit_xn, mode, resid_is_x, has_out_gain):
    it = iter(refs)
    x_refs = [next(it) for _ in range(nx)]
    g_ref = next(it) if has_gain else None
    w_refs = [next(it) for _ in range(nx)]
    r_ref = next(it) if mode in ("resid", "ple") and not resid_is_x else None
    p_ref = next(it) if mode == "ple" else None
    wp_ref = next(it) if mode == "ple" else None
    og_ref = next(it) if has_out_gain else None
    o_ref = next(it)
    xo_ref = next(it) if emit_xn else None
    xn_ref = next(it) if has_gain else None

    if has_gain:
        @pl.when(pl.program_id(1) == 0)
        def _():
            x = x_refs[0][...].astype(F32)
            y = x * lax.rsqrt(jnp.mean(x * x, axis=-1, keepdims=True) + NORM_EPS) * g_ref[...]
            xn_ref[...] = y.astype(BF16)
            if emit_xn:
                xo_ref[...] = y
        lhs = [xn_ref[...]]
    else:
        lhs = [x_ref[...].astype(BF16) for x_ref in x_refs]
    acc = None
    for a, w_ref in zip(lhs, w_refs):
        d = jnp.dot(a, w_ref[...], preferred_element_type=F32)
        acc = d if acc is None else acc + d
    if resid_is_x:
        r_ref = x_refs[0]
    if mode == "resid":
        acc = r_ref[...] + acc
    elif mode == "ple":
        pp = jnp.dot(p_ref[...].astype(BF16), wp_ref[...], preferred_element_type=F32)
        acc = r_ref[...] + pp * jax.nn.sigmoid(acc)
    if has_out_gain:
        acc = acc * lax.rsqrt(jnp.mean(acc * acc, axis=-1, keepdims=True) + NORM_EPS) * og_ref[...]
    o_ref[...] = acc.astype(o_ref.dtype)


def _mm(xs, ws, *, name, gain=None, out_dtype=F32, tm=1024, tn=512, resid=None, ple=None,
        emit_xn=False, out_gain=None):
    M = xs[0][0].shape[0]
    N = ws[0].shape[1]
    tm, tn = _tile(M, tm), _tile(N, tn)
    nx = len(xs)
    has_gain = gain is not None
    mode = "ple" if ple is not None else ("resid" if resid is not None else "none")
    r_arr = ple[0] if mode == "ple" else resid
    resid_is_x = r_arr is xs[0][0] and tn == N == xs[0][2] and xs[0][1] == 0
    assert out_gain is None or tn == N
    args, in_specs = [], []
    for arr, cb, K in xs:
        args.append(arr)
        in_specs.append(pl.BlockSpec((tm, K), lambda i, j, cb=cb: (i, cb)))
    if has_gain:
        K0 = xs[0][2]
        args.append(gain.reshape(1, K0).astype(F32))
        in_specs.append(pl.BlockSpec((1, K0), lambda i, j: (0, 0)))
    for (arr, cb, K), w in zip(xs, ws):
        args.append(w)
        in_specs.append(pl.BlockSpec((K, tn), lambda i, j: (0, j)))
    if mode in ("resid", "ple") and not resid_is_x:
        args.append(r_arr)
        in_specs.append(pl.BlockSpec((tm, tn), lambda i, j: (i, j)))
    if mode == "ple":
        _, p, wp = ple
        args += [p, wp]
        in_specs += [pl.BlockSpec((tm, p.shape[1]), lambda i, j: (i, 0)),
                     pl.BlockSpec((p.shape[1], tn), lambda i, j: (0, j))]
    if out_gain is not None:
        args.append(out_gain.reshape(1, N).astype(F32))
        in_specs.append(pl.BlockSpec((1, N), lambda i, j: (0, 0)))
    out_shape = [jax.ShapeDtypeStruct((M, N), out_dtype)]
    out_specs = [pl.BlockSpec((tm, tn), lambda i, j: (i, j))]
    if emit_xn:
        out_shape.append(jax.ShapeDtypeStruct((M, xs[0][2]), F32))
        out_specs.append(pl.BlockSpec((tm, xs[0][2]), lambda i, j: (i, 0)))
    scratch = [pltpu.VMEM((tm, xs[0][2]), BF16)] if has_gain else []
    res = pl.pallas_call(
        functools.partial(_mm_body, nx=nx, has_gain=has_gain, emit_xn=emit_xn, mode=mode, resid_is_x=resid_is_x,
                          has_out_gain=out_gain is not None),
        out_shape=out_shape, grid=(M // tm, N // tn), in_specs=in_specs, out_specs=out_specs,
        scratch_shapes=scratch, compiler_params=_params("arbitrary", "arbitrary"), name=name,
    )(*args)
    return res if emit_xn else res[0]


def _rope_tables(positions):
    def tables(dim):
        inv = 1.0 / (ROPE_THETA ** (jnp.arange(0, dim, 2, dtype=F32) / dim))
        ang = positions.astype(F32)[..., None] * inv
        return jnp.cos(ang), jnp.sin(ang)

    lead = positions.shape
    c, s = tables(HEAD_DIM)
    c128 = jnp.concatenate([c, c], -1)
    s128 = jnp.concatenate([-s, s], -1)
    c, s = tables(B_ROPE)
    z = jnp.zeros_like(c)
    c64 = jnp.concatenate([c, c, z, z], -1)
    s64a = jnp.concatenate([-s, z, z, z], -1)
    s64b = jnp.concatenate([z, s, z, z], -1)
    n = int(np.prod(lead))
    return [t.reshape(n, LANES) for t in (c128, s128, c64, s64a, s64b)]


def _rot128(x, c, s):
    return x * c + pltpu.roll(x, 64, 1) * s


def _rot64(x, c, sa, sb):
    return x * c + pltpu.roll(x, 96, 1) * sa + pltpu.roll(x, 32, 1) * sb


def _prep_body(y_ref, c128, s128, c64, s64a, s64b, *o_refs, program):
    for src, mode, scale, oi, dst in program:
        x = y_ref[:, src * LANES:(src + 1) * LANES].astype(F32)
        if mode == "rot128":
            x = _rot128(x, c128[...], s128[...])
        elif mode == "rot64":
            x = _rot64(x, c64[...], s64a[...], s64b[...])
        if scale != 1.0:
            x = x * scale
        o_refs[oi][:, dst * LANES:(dst + 1) * LANES] = x.astype(o_refs[oi].dtype)


def _prep(y, tables, program, outs, *, name, tm=256):
    M, C = y.shape
    tm = _tile(M, tm)
    row = lambda i: (i, 0)
    return pl.pallas_call(
        functools.partial(_prep_body, program=tuple(program)),
        out_shape=[jax.ShapeDtypeStruct((M, nb * LANES), dt) for nb, dt in outs],
        grid=(M // tm,),
        in_specs=[pl.BlockSpec((tm, C), row)] + [pl.BlockSpec((tm, LANES), row)] * 5,
        out_specs=[pl.BlockSpec((tm, nb * LANES), row) for nb, _ in outs],
        compiler_params=_params("arbitrary"), name=name,
    )(y, *tables)


def _pair_tables(S, tq, tk, window):
    qi_l, ki_l, fl_l, mk_l, masks, ids = [], [], [], [], [], {}
    for qi in range(S // tq):
        q0, q1 = qi * tq, (qi + 1) * tq - 1
        ks = []
        for ki in range(S // tk):
            k0, k1 = ki * tk, (ki + 1) * tk - 1
            if k0 > q1 or (window is not None and q0 - k1 >= window):
                continue
            full = k1 <= q0 and (window is None or q1 - k0 < window)
            mid = 0
            if not full:
                delta = q0 - k0
                if delta not in ids:
                    rel = np.arange(tq)[:, None] + delta - np.arange(tk)[None, :]
                    ok = (rel >= 0) if window is None else ((rel >= 0) & (rel < window))
                    masks.append(np.where(ok, 0.0, NEG_INF).astype(np.float32))
                    ids[delta] = len(masks)
                mid = ids[delta]
            ks.append((ki, mid))
        for n, (ki, mid) in enumerate(ks):
            qi_l.append(qi)
            ki_l.append(ki)
            fl_l.append((1 if n == 0 else 0) | (2 if n == len(ks) - 1 else 0))
            mk_l.append(mid)
    if not masks:
        masks.append(np.zeros((tq, tk), np.float32))
    tabs = [jnp.asarray(np.array(a, np.int32)) for a in (qi_l, ki_l, fl_l, mk_l)]
    return tabs, jnp.asarray(np.stack(masks))


def _flash_body(qi_t, ki_t, fl_t, mk_t, *refs, nq, nkv, q_src, k_src, v_src, tq, rc, has_sink,
                has_bias, has_sel):
    nc = len(q_src)
    it = iter(refs)
    q_refs = [next(it) for _ in range(nq)]
    kv_refs = [next(it) for _ in range(nkv)]
    mask_ref = next(it)
    sink_ref = next(it) if has_sink else None
    cum_ref = next(it) if has_bias else None
    cumt_ref = next(it) if has_bias else None
    sel_ref = next(it) if has_sel else None
    exp_ref = next(it) if has_sel else None
    o_ref = next(it)
    m_sc, acc_sc = next(it), next(it)
    cq_sc = next(it) if has_bias else None

    step_id = pl.program_id(2)
    fl, mk = fl_t[step_id], mk_t[step_id]

    def cat(refs_, src, rows=slice(None)):
        xs = [refs_[pi][0, rows, off:off + LANES] for pi, off in src]
        return xs[0] if len(xs) == 1 else jnp.concatenate(xs, axis=-1)

    @pl.when((fl & 1) != 0)
    def _init():
        m_sc[...] = jnp.full(m_sc.shape, NEG_INF, F32)
        acc_sc[...] = jnp.zeros(acc_sc.shape, F32)
        if has_bias:
            lane = lax.broadcasted_iota(jnp.int32, (tq, LANES), 1)
            for g in range(nc):
                head = pl.program_id(1) * nc + g
                cq_sc[g] = jnp.sum(jnp.where(lane == head, cum_ref[0], 0.0), axis=-1, keepdims=True)

    tk = kv_refs[0].shape[1]
    reps = tk // LANES
    ones = jnp.ones((tk, LANES), BF16)

    def step(masked):
        ks = [cat(kv_refs, k_src[g]) for g in range(nc)]
        vs = [jnp.concatenate([kv_refs[pi][0, :, off:off + LANES], ones], axis=1) for pi, off in v_src]
        for r in range(tq // rc):
            rows = slice(r * rc, (r + 1) * rc)
            add = mask_ref[mk - 1, rows, :] if masked else None
            if has_sel:
                hidden = (jnp.dot(sel_ref[0, 0, rows, :], exp_ref[0], preferred_element_type=F32) - 1.0) * (-NEG_INF)
                add = hidden if add is None else add + hidden
            for g in range(nc):
                q = cat(q_refs, q_src[g], rows)
                s = lax.dot_general(q, ks[g], (((1,), (1,)), ((), ())), preferred_element_type=F32)
                if has_bias:
                    s = s + (cq_sc[g, rows, :] - cumt_ref[0, g])
                if add is not None:
                    s = s + add
                m_prev = m_sc[g, rows, :]
                m_new = jnp.maximum(m_prev, jnp.max(s, axis=-1, keepdims=True))
                p = jnp.exp2(s - (jnp.concatenate([m_new] * reps, axis=1) if reps > 1 else m_new))
                alpha = jnp.exp2(m_prev - m_new)
                acc_sc[g, rows, :] = (jnp.concatenate([alpha, alpha], axis=1) * acc_sc[g, rows, :]
                                      + jnp.dot(p.astype(BF16), vs[g], preferred_element_type=F32))
                m_sc[g, rows, :] = m_new

    @pl.when(mk != 0)
    def _():
        step(True)

    @pl.when(mk == 0)
    def _():
        step(False)

    @pl.when((fl & 2) != 0)
    def _finish():
        for g in range(nc):
            m, acc, l = m_sc[g], acc_sc[g, :, :LANES], acc_sc[g, :, LANES:]
            if has_sink:
                sk = sink_ref[0, g:g + 1, 0:1] * LOG2E
                m_f = jnp.maximum(m, sk)
                w = jnp.exp2(m - m_f)
                l = l * w + jnp.exp2(sk - m_f)
                acc = acc * w
            o_ref[0, :, g * LANES:(g + 1) * LANES] = (acc / l).astype(o_ref.dtype)


def _flash(q_parts, kv_parts, q_src, k_src, v_src, *, name, B, S, n_steps, tq, tk, window=None, sinks=None,
           cum=None, cumt=None, sel=None, out_dtype=BF16, rc=FLASH_ROW_CHUNK):
    nc = len(q_src)
    tq, tk = _tile(S, tq), _tile(S, tk)
    tabs, masks = _pair_tables(S, tq, tk, window)
    npairs = int(tabs[0].shape[0])
    args, in_specs = [], []
    for arr, width, cf in q_parts:
        args.append(arr)
        in_specs.append(pl.BlockSpec((1, tq, width), lambda b, h, s, qt, kt, ft, mt, cf=cf: (b, qt[s], cf(h))))
    for arr, width, cf in kv_parts:
        args.append(arr)
        in_specs.append(pl.BlockSpec((1, tk, width), lambda b, h, s, qt, kt, ft, mt, cf=cf: (b, kt[s], cf(h))))
    args.append(masks)
    in_specs.append(pl.BlockSpec(masks.shape, lambda b, h, s, qt, kt, ft, mt: (0, 0, 0)))
    if sinks is not None:
        args.append(jnp.broadcast_to(sinks.astype(F32).reshape(n_steps, nc, 1), (n_steps, nc, LANES)))
        in_specs.append(pl.BlockSpec((1, nc, LANES), lambda b, h, s, qt, kt, ft, mt: (h, 0, 0)))
    if cum is not None:
        args += [cum, cumt]
        in_specs += [pl.BlockSpec((1, tq, LANES), lambda b, h, s, qt, kt, ft, mt: (b, qt[s], 0)),
                     pl.BlockSpec((1, nc, 1, tk), lambda b, h, s, qt, kt, ft, mt: (b, h, 0, kt[s]))]
    if sel is not None:
        per = tk // D_SLC_LEN
        e = np.zeros((S // tk, LANES, tk), np.float32)
        for ki in range(S // tk):
            e[ki, ki * per + np.arange(tk) // D_SLC_LEN, np.arange(tk)] = 1.0
        args += [sel, jnp.asarray(e, BF16)]
        in_specs += [pl.BlockSpec((1, 1, tq, LANES), lambda b, h, s, qt, kt, ft, mt: (b, h, qt[s], 0)),
                     pl.BlockSpec((1, LANES, tk), lambda b, h, s, qt, kt, ft, mt: (kt[s], 0, 0))]
    scratch = [pltpu.VMEM((nc, tq, LANES), F32), pltpu.VMEM((nc, tq, 2 * LANES), F32)]
    if cum is not None:
        scratch.append(pltpu.VMEM((nc, tq, 1), F32))
    body = functools.partial(_flash_body, nq=len(q_parts), nkv=len(kv_parts), q_src=q_src, k_src=k_src,
                             v_src=v_src, tq=tq, rc=_tile(tq, rc), has_sink=sinks is not None,
                             has_bias=cum is not None, has_sel=sel is not None)
    return pl.pallas_call(
        body, out_shape=jax.ShapeDtypeStruct((B, S, n_steps * nc * LANES), out_dtype),
        grid_spec=pltpu.PrefetchScalarGridSpec(
            num_scalar_prefetch=4, grid=(B, n_steps, npairs), in_specs=in_specs,
            out_specs=pl.BlockSpec((1, tq, nc * LANES), lambda b, h, s, qt, kt, ft, mt: (b, qt[s], h)),
            scratch_shapes=scratch),
        compiler_params=_params("arbitrary", "arbitrary", "arbitrary"), name=name,
    )(*tabs, *args)


def _gqa_flash(q, k, v, *, name, B, S, Hk, G, **kw):
    head = lambda h: h
    return _flash([(q, G * LANES, head)], [(k, LANES, head), (v, LANES, head)],
                  [[(0, g * LANES)] for g in range(G)], [[(0, 0)]] * G, [(1, 0)] * G,
                  name=name, B=B, S=S, n_steps=Hk, **kw)


def _cum_body(y_ref, b_ref, tri_ref, o_ref, carry):
    @pl.when(pl.program_id(1) == 0)
    def _():
        carry[...] = jnp.zeros(carry.shape, F32)
    x = y_ref[0] + b_ref[...]
    logf = jnp.minimum(x, 0.0) - jnp.log1p(jnp.exp(-jnp.abs(x)))
    cum = jnp.dot(tri_ref[...], logf, preferred_element_type=F32, precision=lax.Precision.HIGHEST) + carry[...]
    o_ref[0] = cum * LOG2E
    carry[...] = cum[-1:, :]


def _forget_cum(y3, col_block, bias_row, *, ts=512):
    B, S, _ = y3.shape
    ts = _tile(S, ts)
    tri = jnp.asarray(np.tril(np.ones((ts, ts), np.float32)))
    return pl.pallas_call(
        _cum_body, out_shape=jax.ShapeDtypeStruct((B, S, LANES), F32), grid=(B, S // ts),
        in_specs=[pl.BlockSpec((1, ts, LANES), lambda b, s: (b, s, col_block)),
                  pl.BlockSpec((1, LANES), lambda b, s: (0, 0)),
                  pl.BlockSpec((ts, ts), lambda b, s: (0, 0))],
        out_specs=pl.BlockSpec((1, ts, LANES), lambda b, s: (b, s, 0)),
        scratch_shapes=[pltpu.VMEM((1, LANES), F32)],
        compiler_params=_params("arbitrary", "arbitrary"), name="forget_cum",
    )(y3, bias_row, tri)


def _compress_body(*refs, rope, nc):
    if rope:
        z_ref, pe_ref, w1_ref, w2_ref, c_ref, s_ref, o_ref = refs
    else:
        z_ref, pe_ref, w1_ref, w2_ref, o_ref = refs
    half = D_CMP_LEN // 2
    width = D_KV_HEADS * HEAD_DIM
    for hk in range(D_KV_HEADS):
        u = jnp.zeros((nc, w1_ref.shape[1]), F32)
        v = jnp.zeros((nc, w1_ref.shape[1]), F32)
        for l in range(half):
            z = z_ref[0, :, l * width + hk * HEAD_DIM:l * width + (hk + 1) * HEAD_DIM]
            zu = (z + pe_ref[l:l + 1, :]).astype(BF16)
            zv = (z + pe_ref[half + l:half + l + 1, :]).astype(BF16)
            u = u + jnp.dot(zu, w1_ref[l * HEAD_DIM:(l + 1) * HEAD_DIM, :], preferred_element_type=F32)
            v = v + jnp.dot(zv, w1_ref[(half + l) * HEAD_DIM:(half + l + 1) * HEAD_DIM, :],
                            preferred_element_type=F32)
        pre = u + pltpu.roll(v, nc - 1, 0)
        hid = jax.nn.gelu(pre, approximate=True)
        out = jnp.dot(hid.astype(BF16), w2_ref[...], preferred_element_type=F32)
        if rope:
            out = _rot128(out, c_ref[0], s_ref[0])
        o_ref[0, hk] = out.astype(o_ref.dtype)


def _compress(z, pe, w1, w2, rope_tabs=None):
    B, S, W = z.shape
    nc = S // D_CMP_STRIDE
    zc = z.reshape(B, nc, D_CMP_STRIDE * W)
    args = [zc, pe.astype(F32), w1.astype(BF16), w2.astype(BF16)]
    in_specs = [pl.BlockSpec((1, nc, D_CMP_STRIDE * W), lambda b: (b, 0, 0)),
                pl.BlockSpec(pe.shape, lambda b: (0, 0)),
                pl.BlockSpec(w1.shape, lambda b: (0, 0)),
                pl.BlockSpec(w2.shape, lambda b: (0, 0))]
    if rope_tabs is not None:
        args += list(rope_tabs)
        in_specs += [pl.BlockSpec((1, nc, LANES), lambda b: (b, 0, 0))] * 2
    return pl.pallas_call(
        functools.partial(_compress_body, rope=rope_tabs is not None, nc=nc),
        out_shape=jax.ShapeDtypeStruct((B, D_KV_HEADS, nc, HEAD_DIM), BF16), grid=(B,),
        in_specs=in_specs, out_specs=pl.BlockSpec((1, D_KV_HEADS, nc, HEAD_DIM), lambda b: (b, 0, 0, 0)),
        compiler_params=_params("arbitrary"), name="nsa_compress",
    )(*args)


def _cmp_attn_body(q_ref, k_ref, v_ref, ov_ref, o_ref, sel_ref, *, G, tq, nc, n_cmp, n_slc, topn):
    qi = pl.program_id(2)
    t = qi * tq + lax.broadcasted_iota(jnp.int32, (tq, nc), 0)
    c = lax.broadcasted_iota(jnp.int32, (tq, nc), 1)
    valid = (c * D_CMP_STRIDE + (D_CMP_LEN - 1) <= t) & (c < n_cmp)
    k = k_ref[0, 0]
    v = v_ref[0, 0]
    psum = jnp.zeros((tq, nc), F32)
    for g in range(G):
        q = q_ref[0, :, g * LANES:(g + 1) * LANES]
        s = lax.dot_general(q, k, (((1,), (1,)), ((), ())), preferred_element_type=F32)
        s = jnp.where(valid, s, NEG_INF)
        e = jnp.where(valid, jnp.exp2(s - jnp.max(s, axis=-1, keepdims=True)), 0.0)
        p = e / jnp.maximum(jnp.sum(e, axis=-1, keepdims=True), jnp.finfo(F32).tiny)
        o_ref[0, :, g * LANES:(g + 1) * LANES] = jnp.dot(
            p.astype(BF16), v, preferred_element_type=F32).astype(o_ref.dtype)
        psum = psum + p
    imp = jnp.dot(psum, ov_ref[...], preferred_element_type=F32, precision=lax.Precision.HIGHEST)
    lane = lax.broadcasted_iota(jnp.int32, (tq, LANES), 1)
    trow = qi * tq + lax.broadcasted_iota(jnp.int32, (tq, LANES), 0)
    cur = jnp.right_shift(trow, int(math.log2(D_SLC_LEN)))
    forced = (lane == 0) | (lane == cur) | (lane == cur - 1)
    imp = jnp.where(lane * D_SLC_LEN > trow, NEG_INF, imp + jnp.where(forced, FORCE_BONUS, 0.0))
    imp = jnp.where(lane >= n_slc, TAKEN, imp)
    chosen = jnp.zeros((tq, LANES), F32)
    lane_f = lane.astype(F32)
    for _ in range(topn):
        mx = jnp.max(imp, axis=-1, keepdims=True)
        idx = jnp.min(jnp.where(imp == mx, lane_f, float(LANES)), axis=-1, keepdims=True)
        hit = lane_f == idx
        chosen = jnp.where(hit, 1.0, chosen)
        imp = jnp.where(hit, TAKEN, imp)
    sel_ref[0, 0] = chosen.astype(sel_ref.dtype)


def _cmp_attn(q, k_cmp, v_cmp, *, tq=256):
    B, S, _ = q.shape
    Hk, G = D_KV_HEADS, D_HEADS // D_KV_HEADS
    nc = S // D_CMP_STRIDE
    n_cmp = (S - D_CMP_LEN) // D_CMP_STRIDE + 1
    n_slc = S // D_SLC_LEN
    tq = _tile(S, tq)
    c0 = np.arange(nc) * D_CMP_STRIDE
    s0 = np.arange(LANES) * D_SLC_LEN
    ov = ((c0[:, None] < (s0 + D_SLC_LEN)[None, :]) & ((c0 + D_CMP_LEN)[:, None] > s0[None, :])
          & (np.arange(nc) < n_cmp)[:, None] & (np.arange(LANES) < n_slc)[None, :]).astype(np.float32)
    body = functools.partial(_cmp_attn_body, G=G, tq=tq, nc=nc, n_cmp=n_cmp, n_slc=n_slc,
                             topn=min(D_SLC_TOPN, n_slc))
    return pl.pallas_call(
        body,
        out_shape=[jax.ShapeDtypeStruct((B, S, Hk * G * LANES), F32),
                   jax.ShapeDtypeStruct((B, Hk, S, LANES), BF16)],
        grid=(B, Hk, S // tq),
        in_specs=[pl.BlockSpec((1, tq, G * LANES), lambda b, h, i: (b, i, h)),
                  pl.BlockSpec((1, 1, nc, LANES), lambda b, h, i: (b, h, 0, 0)),
                  pl.BlockSpec((1, 1, nc, LANES), lambda b, h, i: (b, h, 0, 0)),
                  pl.BlockSpec((nc, LANES), lambda b, h, i: (0, 0))],
        out_specs=[pl.BlockSpec((1, tq, G * LANES), lambda b, h, i: (b, i, h)),
                   pl.BlockSpec((1, 1, tq, LANES), lambda b, h, i: (b, h, i, 0))],
        compiler_params=_params("arbitrary", "arbitrary", "arbitrary"), name="nsa_cmp_attn",
    )(q, k_cmp, v_cmp, jnp.asarray(ov))


def _gate_body(y_ref, a_ref, b_ref, c_ref, o_ref, *, lane0):
    g = jax.nn.sigmoid(y_ref[...])
    for h in range(D_HEADS):
        cols = slice(h * LANES, (h + 1) * LANES)
        ga = g[:, lane0 + h:lane0 + h + 1]
        gb = g[:, lane0 + D_HEADS + h:lane0 + D_HEADS + h + 1]
        gc = g[:, lane0 + 2 * D_HEADS + h:lane0 + 2 * D_HEADS + h + 1]
        o_ref[:, cols] = (ga * a_ref[:, cols] + gb * b_ref[:, cols] + gc * c_ref[:, cols]).astype(o_ref.dtype)


def _nsa_gate(y, col_block, lane0, o_cmp, o_slc, o_win, *, tm=512):
    M = y.shape[0]
    W = o_cmp.shape[1]
    tm = _tile(M, tm)
    row = lambda i: (i, 0)
    return pl.pallas_call(
        functools.partial(_gate_body, lane0=lane0), out_shape=jax.ShapeDtypeStruct((M, W), BF16),
        grid=(M // tm,),
        in_specs=[pl.BlockSpec((tm, LANES), lambda i: (i, col_block))] + [pl.BlockSpec((tm, W), row)] * 3,
        out_specs=pl.BlockSpec((tm, W), row), compiler_params=_params("arbitrary"), name="nsa_gate",
    )(y, o_cmp, o_slc, o_win)


def _route_body(lg_ref, b_ref, tri_ref, e_ref, w_ref, pos_ref, cnt_ref, carry, *, tm):
    @pl.when(pl.program_id(0) == 0)
    def _():
        carry[...] = jnp.zeros(carry.shape, F32)
    lane = lax.broadcasted_iota(jnp.int32, (tm, LANES), 1)
    logits = lg_ref[...] + b_ref[...]
    gl = jnp.where(lane < N_GROUPS, logits, -jnp.inf)
    gmax = jnp.max(gl, axis=-1, keepdims=True)
    g_val = 1.0 / jnp.sum(jnp.exp(gl - gmax), axis=-1, keepdims=True)
    g_idx = jnp.min(jnp.where(gl == gmax, lane, LANES), axis=-1, keepdims=True)
    lo = N_GROUPS + EXPERTS_PER_GROUP * g_idx
    el = jnp.where((lane >= lo) & (lane < lo + EXPERTS_PER_GROUP), logits, -jnp.inf)
    e1 = jnp.max(el, axis=-1, keepdims=True)
    i1 = jnp.min(jnp.where(el == e1, lane, LANES), axis=-1, keepdims=True)
    el2 = jnp.where(lane == i1, -jnp.inf, el)
    e2 = jnp.max(el2, axis=-1, keepdims=True)
    i2 = jnp.min(jnp.where(el2 == e2, lane, LANES), axis=-1, keepdims=True)
    r = jnp.exp(e2 - e1)
    w1 = g_val / (1.0 + r)
    w2 = w1 * r
    x1, x2 = i1 - N_GROUPS, i2 - N_GROUPS
    e_ref[...] = jnp.where(lane == 0, x1, jnp.where(lane == 1, x2, 0))
    w_ref[...] = jnp.where(lane == 0, w1, jnp.where(lane == 1, w2, 0.0))
    hot1 = lane == x1
    hot2 = lane == x2
    both = jnp.where(hot1 | hot2, 1.0, 0.0)
    before = jnp.dot(tri_ref[...], both.astype(BF16), preferred_element_type=F32) + carry[...]
    p1 = jnp.sum(jnp.where(hot1, before, 0.0), axis=-1, keepdims=True)
    p2 = jnp.sum(jnp.where(hot2, before, 0.0), axis=-1, keepdims=True)
    pos_ref[...] = jnp.where(lane == 0, p1, jnp.where(lane == 1, p2, 0.0)).astype(jnp.int32)
    carry[...] = carry[...] + jnp.sum(both, axis=0, keepdims=True)
    cnt_ref[...] = carry[...].astype(jnp.int32)


def _route(logits, bias_row, *, tm=512):
    T = logits.shape[0]
    tm = _tile(T, tm)
    tri = jnp.asarray(np.tril(np.ones((tm, tm), np.float32), -1), BF16)
    row = lambda i: (i, 0)
    fixed = lambda i: (0, 0)
    return pl.pallas_call(
        functools.partial(_route_body, tm=tm),
        out_shape=[jax.ShapeDtypeStruct((T, LANES), jnp.int32), jax.ShapeDtypeStruct((T, LANES), F32),
                   jax.ShapeDtypeStruct((T, LANES), jnp.int32), jax.ShapeDtypeStruct((1, LANES), jnp.int32)],
        grid=(T // tm,),
        in_specs=[pl.BlockSpec((tm, LANES), row), pl.BlockSpec((1, LANES), fixed), pl.BlockSpec((tm, tm), fixed)],
        out_specs=[pl.BlockSpec((tm, LANES), row)] * 3 + [pl.BlockSpec((1, LANES), fixed)],
        scratch_shapes=[pltpu.VMEM((1, LANES), F32)],
        compiler_params=_params("arbitrary"), name="moe_route",
    )(logits, bias_row, tri)


def _dispatch_body(d_ref, ends_ref, pad_ref, nu_ref, x_ref, out_ref, zero_sc, sem, zsem, *, tm, nblk):
    base = pl.program_id(0) * tm

    @pl.when(pl.program_id(0) == 0)
    def _():
        zero_sc[...] = jnp.zeros(zero_sc.shape, zero_sc.dtype)

        def zero_block(first_row):
            return pltpu.make_async_copy(zero_sc, out_ref.at[pl.ds(pl.multiple_of(first_row, MOE_ROWS), MOE_ROWS)],
                                         zsem)

        for wait in (False, True):
            for e in range(N_EXPERTS):
                for live, first_row in ((pad_ref[e] > 0, ends_ref[e] - MOE_ROWS),
                                        (nu_ref[0] + e < nblk, (nu_ref[0] + e) * MOE_ROWS)):
                    @pl.when(live)
                    def _():
                        zero_block(first_row).wait() if wait else zero_block(first_row).start()

    def issue(r, c):
        for k in range(TOP_K):
            row = d_ref[(base + r) * TOP_K + k]
            pltpu.make_async_copy(x_ref.at[pl.ds(r, 1)], out_ref.at[pl.ds(row, 1)], sem).start()
        return c

    lax.fori_loop(0, tm, issue, 0)
    for _ in range(TOP_K):
        pltpu.make_async_copy(x_ref, out_ref.at[pl.ds(0, tm)], sem).wait()


def _dispatch(xn, dest, ends, padded, n_used, n_rows, *, tm=512):
    T, D = xn.shape
    tm = _tile(T, tm)
    return pl.pallas_call(
        functools.partial(_dispatch_body, tm=tm, nblk=n_rows // MOE_ROWS),
        out_shape=jax.ShapeDtypeStruct((n_rows, D), xn.dtype),
        grid_spec=pltpu.PrefetchScalarGridSpec(
            num_scalar_prefetch=4, grid=(T // tm,),
            in_specs=[pl.BlockSpec((tm, D), lambda i, *_: (i, 0))],
            out_specs=pl.BlockSpec(memory_space=pl.ANY),
            scratch_shapes=[pltpu.VMEM((MOE_ROWS, D), xn.dtype), pltpu.SemaphoreType.DMA(()),
                            pltpu.SemaphoreType.DMA(())]),
        compiler_params=pltpu.CompilerParams(dimension_semantics=("arbitrary",), has_side_effects=True,
                                             vmem_limit_bytes=VMEM_LIMIT_BYTES, disable_bounds_checks=True),
        name="moe_dispatch",
    )(dest, ends, padded, n_used, xn)


def _expert_body(be_ref, nu_ref, x_ref, wg_ref, wu_ref, wd_ref, o_ref, wg_sc, wu_sc, wd_sc):
    i = pl.program_id(0)
    fresh = (i == 0) | (be_ref[i] != be_ref[jnp.maximum(i - 1, 0)])

    @pl.when(fresh)
    def _():
        wg_sc[...] = wg_ref[0, 0].astype(BF16)
        wu_sc[...] = wu_ref[0, 0].astype(BF16)
        wd_sc[...] = wd_ref[0, 0].astype(BF16)

    @pl.when(i < nu_ref[0])
    def _():
        x = x_ref[...].astype(BF16)
        gate = jnp.dot(x, wg_sc[...], preferred_element_type=F32)
        up = jnp.dot(x, wu_sc[...], preferred_element_type=F32)
        hid = (gate * jax.nn.sigmoid(gate) * up).astype(BF16)
        o_ref[...] = jnp.dot(hid, wd_sc[...], preferred_element_type=F32)

    @pl.when(i >= nu_ref[0])
    def _():
        o_ref[...] = jnp.zeros(o_ref.shape, F32)


def _experts(xb, blk_e, n_used, layer, w_gate, w_up, w_down):
    D, Hd = w_gate.shape[2], w_gate.shape[3]
    nblk = xb.shape[0] // MOE_ROWS
    return pl.pallas_call(
        _expert_body, out_shape=jax.ShapeDtypeStruct(xb.shape, F32),
        grid_spec=pltpu.PrefetchScalarGridSpec(
            num_scalar_prefetch=2, grid=(nblk,),
            in_specs=[pl.BlockSpec((MOE_ROWS, D), lambda i, be, nu: (jnp.minimum(i, nu[0] - 1), 0)),
                      pl.BlockSpec((1, 1, D, Hd), lambda i, be, nu: (layer, be[i], 0, 0)),
                      pl.BlockSpec((1, 1, D, Hd), lambda i, be, nu: (layer, be[i], 0, 0)),
                      pl.BlockSpec((1, 1, Hd, D), lambda i, be, nu: (layer, be[i], 0, 0))],
            out_specs=pl.BlockSpec((MOE_ROWS, D), lambda i, be, nu: (i, 0)),
            scratch_shapes=[pltpu.VMEM((D, Hd), BF16), pltpu.VMEM((D, Hd), BF16), pltpu.VMEM((Hd, D), BF16)]),
        compiler_params=_params("arbitrary"), name="moe_experts",
    )(blk_e, n_used, xb, w_gate, w_up, w_down)


def _collect_body(d_ref, h_ref, w_ref, yb_ref, o_ref, buf_a, buf_b, sem, *, tm, nsteps):
    i = pl.program_id(0)
    slot = lax.rem(i, 2)

    def fetch(step, slot_):
        base = step * tm

        def issue(r, c):
            for k, buf in enumerate((buf_a, buf_b)):
                row = d_ref[(base + r) * TOP_K + k]
                pltpu.make_async_copy(yb_ref.at[pl.ds(row, 1)], buf.at[slot_, pl.ds(r, 1)], sem.at[slot_]).start()
            return c

        lax.fori_loop(0, tm, issue, 0)

    @pl.when(i == 0)
    def _():
        fetch(0, 0)

    @pl.when(i + 1 < nsteps)
    def _():
        fetch(i + 1, 1 - slot)

    for buf in (buf_a, buf_b):
        pltpu.make_async_copy(yb_ref.at[pl.ds(0, tm)], buf.at[slot], sem.at[slot]).wait()
    w = w_ref[...]
    o_ref[...] = h_ref[...] + w[:, 0:1] * buf_a[slot] + w[:, 1:2] * buf_b[slot]


def _collect(h, yb, dest, wts, *, tm=256):
    T, D = h.shape
    tm = _tile(T, tm)
    nsteps = T // tm
    return pl.pallas_call(
        functools.partial(_collect_body, tm=tm, nsteps=nsteps),
        out_shape=jax.ShapeDtypeStruct((T, D), F32),
        grid_spec=pltpu.PrefetchScalarGridSpec(
            num_scalar_prefetch=1, grid=(nsteps,),
            in_specs=[pl.BlockSpec((tm, D), lambda i, d: (i, 0)), pl.BlockSpec((tm, LANES), lambda i, d: (i, 0)),
                      pl.BlockSpec(memory_space=pl.ANY)],
            out_specs=pl.BlockSpec((tm, D), lambda i, d: (i, 0)),
            scratch_shapes=[pltpu.VMEM((2, tm, D), F32), pltpu.VMEM((2, tm, D), F32),
                            pltpu.SemaphoreType.DMA((2,))]),
        compiler_params=pltpu.CompilerParams(dimension_semantics=("arbitrary",), disable_bounds_checks=True,
                                             vmem_limit_bytes=VMEM_LIMIT_BYTES),
        name="moe_collect",
    )(dest, h, wts, yb)


def _moe(h, norm_g, w_group, b_group, w_expert, b_expert, layer, w_gate, w_up, w_down):
    T, D = h.shape
    pad = LANES - N_GROUPS - N_EXPERTS
    w_r = jnp.concatenate([w_group, w_expert, jnp.zeros((D, pad), F32)], axis=1).astype(BF16)
    b_r = jnp.concatenate([b_group, b_expert, jnp.zeros((pad,), F32)]).astype(F32).reshape(1, LANES)
    logits, xn = _mm([(h, 0, D)], [w_r], gain=norm_g, emit_xn=True, tm=512, tn=LANES, name="moe_router")
    eid, wts, pos, cnt = _route(logits, b_r)
    counts = cnt[0, :N_EXPERTS]
    padded = (counts + MOE_ROWS - 1) // MOE_ROWS * MOE_ROWS
    ends = jnp.cumsum(padded)
    offs = ends - padded
    dest = (offs[eid[:, :TOP_K]] + pos[:, :TOP_K]).reshape(T * TOP_K).astype(jnp.int32)
    P = T * TOP_K + N_EXPERTS * MOE_ROWS
    nblk = P // MOE_ROWS
    starts = jnp.arange(nblk, dtype=jnp.int32) * MOE_ROWS
    blk_e = jnp.minimum(jnp.sum(ends[None, :] <= starts[:, None], axis=1), N_EXPERTS - 1).astype(jnp.int32)
    n_used = (ends[-1:] // MOE_ROWS).astype(jnp.int32)
    xb = _dispatch(xn, dest, ends.astype(jnp.int32), padded.astype(jnp.int32), n_used, P)
    yb = _experts(xb, blk_e, n_used, layer, w_gate, w_up, w_down)
    return _collect(h, yb, dest, wts)


def _pad_cols(w, n):
    return jnp.pad(w, ((0, 0), (0, n - w.shape[1])))


def _layer_even(h, B, S, tabs, norm_g, w_in, sinks, q_lat_norm, kv_lat_norm, w_uq, w_ukv, w_o):
    T, D = h.shape
    n_in = 2560
    y = _mm([(h, 0, D)], [_pad_cols(w_in, n_in).astype(BF16)], gain=norm_g, tn=512, name="ab_in")
    sa = HEAD_DIM ** -0.5 * LOG2E
    prog = ([(i, "rot128", sa, 0, i) for i in range(8)] + [(8 + i, "rot128", 1.0, 1, i) for i in range(2)]
            + [(10 + i, "copy", 1.0, 2, i) for i in range(2)] + [(18, "rot64", 1.0, 3, 0)])
    q_a, k_a, v_a, k_r = _prep(y, tabs, prog, [(8, BF16), (2, BF16), (2, BF16), (1, BF16)], name="ab_prep")
    wq = w_uq.reshape(B_Q_LORA, B_HEADS, B_NOPE + B_ROPE)
    wq_n = wq[:, :, :B_NOPE].reshape(B_Q_LORA, B_HEADS * B_NOPE)
    wq_r = jnp.pad(wq[:, :, B_NOPE:], ((0, 0), (0, 0), (0, LANES - B_ROPE))).reshape(B_Q_LORA, B_HEADS * LANES)
    qb = _mm([(y, 1536 // B_Q_LORA, B_Q_LORA)], [jnp.concatenate([wq_n, wq_r], 1).astype(BF16)],
             gain=q_lat_norm, tn=512, name="mla_uq")
    sb = (B_NOPE + B_ROPE) ** -0.5 * LOG2E
    prog = [(i, "copy", sb, 0, i) for i in range(8)] + [(8 + i, "rot64", sb, 1, i) for i in range(8)]
    q_n, q_r = _prep(qb, tabs, prog, [(8, BF16), (8, BF16)], name="mla_prep")
    kv = _mm([(y, 2048 // B_KV_LORA, B_KV_LORA)], [w_ukv.astype(BF16)], gain=kv_lat_norm, out_dtype=BF16,
             tn=512, name="mla_ukv")
    r3 = lambda a: a.reshape(B, S, a.shape[-1])
    o_a = _gqa_flash(r3(q_a), r3(k_a), r3(v_a), name="swa_attn", B=B, S=S, Hk=A_KV_HEADS,
                     G=A_HEADS // A_KV_HEADS, window=A_WINDOW, sinks=sinks, tq=256, tk=256)
    hp = 2
    head = lambda h: h
    o_b = _flash([(r3(q_n), hp * LANES, head), (r3(q_r), hp * LANES, head)],
                 [(r3(kv), hp * 2 * LANES, head), (r3(k_r), LANES, lambda h: 0)],
                 [[(0, g * LANES), (1, g * LANES)] for g in range(hp)],
                 [[(0, 2 * g * LANES), (1, 0)] for g in range(hp)], [(0, (2 * g + 1) * LANES) for g in range(hp)],
                 name="mla_attn", B=B, S=S, n_steps=B_HEADS // hp, tq=1024, tk=1024, rc=256)
    wo = w_o.astype(BF16)
    na = A_HEADS * HEAD_DIM
    return _mm([(o_a.reshape(T, -1), 0, na), (o_b.reshape(T, -1), 0, B_HEADS * B_V)], [wo[:na], wo[na:]],
               resid=h, tn=512, name="ab_out")


def _layer_odd(h, B, S, tabs, cmp_tabs, norm_g, w_in, forget_bias, pe_k, w1_k, w2_k, pe_v, w1_v, w2_v, w_o):
    T, D = h.shape
    hc, kvw = C_HEADS * HEAD_DIM, D_KV_HEADS * HEAD_DIM
    o_cf = 3 * hc
    o_dq = o_cf + C_HEADS
    o_dg = o_dq + D_HEADS * HEAD_DIM + 6 * kvw
    w_r = jnp.concatenate([w_in[:, :o_cf], w_in[:, o_dq:o_dg], w_in[:, o_cf:o_dq], w_in[:, o_dg:]], axis=1)
    n_in = 5760
    y = _mm([(h, 0, D)], [_pad_cols(w_r, n_in).astype(BF16)], gain=norm_g, tn=640, name="cd_in")
    misc = (n_in // LANES) - 1
    sc = HEAD_DIM ** -0.5 * LOG2E
    prog = ([(i, "copy", sc, 0, i) for i in range(8)] + [(8 + i, "copy", 1.0, 1, i) for i in range(8)]
            + [(16 + i, "copy", 1.0, 2, i) for i in range(8)] + [(24 + i, "rot128", sc, 3, i) for i in range(8)]
            + [(32 + i, "copy", 1.0, 4, i) for i in range(2)] + [(34 + i, "copy", 1.0, 5, i) for i in range(2)]
            + [(36 + i, "rot128", 1.0, 6, i) for i in range(2)] + [(38 + i, "copy", 1.0, 7, i) for i in range(2)]
            + [(40 + i, "rot128", 1.0, 8, i) for i in range(2)] + [(42 + i, "copy", 1.0, 9, i) for i in range(2)])
    outs = [(8, BF16)] * 4 + [(2, F32), (2, F32)] + [(2, BF16)] * 4
    q_c, k_c, v_c, q_d, z_kc, z_vc, k_s, v_s, k_w, v_w = _prep(y, tabs, prog, outs, name="cd_prep")
    r3 = lambda a: a.reshape(B, S, a.shape[-1])
    fb = jnp.pad(forget_bias.astype(F32), (0, LANES - C_HEADS)).reshape(1, LANES)
    cum = _forget_cum(r3(y), misc, fb)
    cumt = jnp.swapaxes(cum[:, :, :C_HEADS], 1, 2).reshape(B, C_HEADS, 1, S)
    hp = 2
    head = lambda h: h
    o_c = _flash([(r3(q_c), hp * LANES, head)], [(r3(k_c), hp * LANES, head), (r3(v_c), hp * LANES, head)],
                 [[(0, g * LANES)] for g in range(hp)], [[(0, g * LANES)] for g in range(hp)],
                 [(1, g * LANES) for g in range(hp)], name="fox_attn", B=B, S=S, n_steps=C_HEADS // hp,
                 tq=1024, tk=1024, cum=cum, cumt=cumt)
    G = D_HEADS // D_KV_HEADS
    k_cmp = _compress(r3(z_kc), pe_k, w1_k, w2_k, rope_tabs=cmp_tabs)
    v_cmp = _compress(r3(z_vc), pe_v, w1_v, w2_v)
    q3 = r3(q_d)
    o_cmp, sel = _cmp_attn(q3, k_cmp, v_cmp)
    o_slc = _gqa_flash(q3, r3(k_s), r3(v_s), name="nsa_slc_attn", B=B, S=S, Hk=D_KV_HEADS, G=G, sel=sel,
                       out_dtype=F32, tq=1024, tk=1024, rc=256)
    o_win = _gqa_flash(q3, r3(k_w), r3(v_w), name="nsa_win_attn", B=B, S=S, Hk=D_KV_HEADS, G=G,
                       window=D_WINDOW, out_dtype=F32, tq=512, tk=512)
    o_d = _nsa_gate(y, misc, C_HEADS, o_cmp.reshape(T, -1), o_slc.reshape(T, -1), o_win.reshape(T, -1))
    wo = w_o.astype(BF16)
    return _mm([(o_c.reshape(T, -1), 0, hc), (o_d, 0, D_HEADS * HEAD_DIM)], [wo[:hc], wo[hc:]],
               resid=h, tn=512, name="cd_out")


def kernel(x, p, positions, ab_w_in, ab_sinks, ab_q_lat_norm, ab_kv_lat_norm, ab_w_uq, ab_w_ukv, ab_w_o,
           cd_w_in, cd_forget_bias, cd_cmp_pe_k, cd_cmp_w1_k, cd_cmp_w2_k, cd_cmp_pe_v, cd_cmp_w1_v,
           cd_cmp_w2_v, cd_w_o, mixer_norm, moe_norm, router_group_w, router_group_b, router_expert_w,
           router_expert_b, expert_w_gate, expert_w_up, expert_w_down, ple_proj, ple_gate_norm, ple_gate_w,
           final_norm):
    B, S, D = x.shape
    T = B * S
    depth = p.shape[0]
    tabs = _rope_tables(positions)
    nc = S // D_CMP_STRIDE
    end = np.minimum(np.arange(nc) * D_CMP_STRIDE + D_CMP_LEN - 1, S - 1)
    cmp_tabs = [t.reshape(B, nc, LANES) for t in _rope_tables(positions[:, end])[:2]]
    h = x.reshape(T, D)
    for i in range(depth):
        j = i // 2
        if i % 2 == 0:
            h = _layer_even(h, B, S, tabs, mixer_norm[i], ab_w_in[j], ab_sinks[j], ab_q_lat_norm[j],
                            ab_kv_lat_norm[j], ab_w_uq[j], ab_w_ukv[j], ab_w_o[j])
        else:
            h = _layer_odd(h, B, S, tabs, cmp_tabs, mixer_norm[i], cd_w_in[j], cd_forget_bias[j],
                           cd_cmp_pe_k[j], cd_cmp_w1_k[j], cd_cmp_w2_k[j], cd_cmp_pe_v[j], cd_cmp_w1_v[j],
                           cd_cmp_w2_v[j], cd_w_o[j])
        h = _moe(h, moe_norm[i], router_group_w[i], router_group_b[i], router_expert_w[i], router_expert_b[i],
                 i, expert_w_gate, expert_w_up, expert_w_down)
        last = i == depth - 1
        h = _mm([(h, 0, D)], [ple_gate_w[i].astype(BF16)], gain=ple_gate_norm[i],
                tm=512 if last else 1024, tn=D if last else 512,
                ple=(h, p[i].reshape(T, -1), ple_proj[i].astype(BF16)),
                out_gain=final_norm if last else None, name="ple")
    return h.reshape(B, S, D)
```

```python
import functools
import math

import numpy as np
import jax
import jax.numpy as jnp
from jax import lax
from jax.experimental import pallas as pl
from jax.experimental.pallas import tpu as pltpu

F32 = jnp.float32
BF16 = jnp.bfloat16

HEAD_DIM = 128
ROPE_THETA = 10000.0
NORM_EPS = 1e-6
NEG_INF = -1e30
TAKEN = -3e38
A_HEADS, A_KV_HEADS, A_WINDOW = 8, 2, 128
B_HEADS, B_Q_LORA, B_KV_LORA, B_NOPE, B_ROPE, B_V = 8, 512, 256, 128, 64, 128
C_HEADS = 8
D_HEADS, D_KV_HEADS = 8, 2
D_CMP_LEN, D_CMP_STRIDE, D_SLC_LEN, D_SLC_TOPN, D_WINDOW = 32, 16, 64, 8, 512
FORCE_BONUS = 1e4
N_GROUPS, EXPERTS_PER_GROUP, TOP_K = 4, 8, 2
N_EXPERTS = N_GROUPS * EXPERTS_PER_GROUP

LANES = 128
VMEM_LIMIT_BYTES = 56 * 1024 * 1024
MOE_ROWS = 256
FLASH_ROW_CHUNK = 128
LOG2E = math.log2(math.e)


def _params(*sem):
    return pltpu.CompilerParams(dimension_semantics=sem, vmem_limit_bytes=VMEM_LIMIT_BYTES)


def _tile(n, pref):
    t = min(n, pref)
    while n % t:
        t -= 1
    return t


def _pack_bf16_pairs(y):
    n = y.shape[1] // 2
    bits = lambda a: lax.bitcast_convert_type(a.astype(BF16).astype(F32), jnp.uint32)
    return (bits(y[:, :n]) >> 16) | (bits(y[:, n:]) & jnp.uint32(0xFFFF0000))


def _unpack_bf16_pairs(w):
    return (lax.bitcast_convert_type(w << 16, F32), lax.bitcast_convert_type(w & jnp.uint32(0xFFFF0000), F32))


def _mm_body(*refs, nx, has_gain, emit_xn, mode, resid_is_x, has_out_gain):
    it = iter(refs)
    x_refs = [next(it) for _ in range(nx)]
    g_ref = next(it) if has_gain else None
    w_refs = [next(it) for _ in range(nx)]
    r_ref = next(it) if mode in ("resid", "ple") and not resid_is_x else None
    p_ref = next(it) if mode == "ple" else None
    wp_ref = next(it) if mode == "ple" else None
    og_ref = next(it) if has_out_gain else None
    o_ref = next(it)
    xo_ref = next(it) if emit_xn else None
    xn_ref = next(it) if has_gain else None

    if has_gain:
        @pl.when(pl.program_id(1) == 0)
        def _():
            x = x_refs[0][...].astype(F32)
            y = x * lax.rsqrt(jnp.mean(x * x, axis=-1, keepdims=True) + NORM_EPS) * g_ref[...]
            xn_ref[...] = y.astype(BF16)
            if emit_xn:
                xo_ref[...] = _pack_bf16_pairs(y)
        lhs = [xn_ref[...]]
    else:
        lhs = [x_ref[...].astype(BF16) for x_ref in x_refs]
    acc = None
    for a, w_ref in zip(lhs, w_refs):
        d = jnp.dot(a, w_ref[...], preferred_element_type=F32)
        acc = d if acc is None else acc + d
    if resid_is_x:
        r_ref = x_refs[0]
    if mode == "resid":
        acc = r_ref[...] + acc
    elif mode == "ple":
        pp = jnp.dot(p_ref[...].astype(BF16), wp_ref[...], preferred_element_type=F32)
        acc = r_ref[...] + pp * jax.nn.sigmoid(acc)
    if has_out_gain:
        acc = acc * lax.rsqrt(jnp.mean(acc * acc, axis=-1, keepdims=True) + NORM_EPS) * og_ref[...]
    o_ref[...] = acc.astype(o_ref.dtype)


def _mm(xs, ws, *, name, gain=None, out_dtype=F32, tm=1024, tn=512, resid=None, ple=None,
        emit_xn=False, out_gain=None):
    M = xs[0][0].shape[0]
    N = ws[0].shape[1]
    tm, tn = _tile(M, tm), _tile(N, tn)
    nx = len(xs)
    has_gain = gain is not None
    mode = "ple" if ple is not None else ("resid" if resid is not None else "none")
    r_arr = ple[0] if mode == "ple" else resid
    resid_is_x = r_arr is xs[0][0] and tn == N == xs[0][2] and xs[0][1] == 0
    assert out_gain is None or tn == N
    args, in_specs = [], []
    for arr, cb, K in xs:
        args.append(arr)
        in_specs.append(pl.BlockSpec((tm, K), lambda i, j, cb=cb: (i, cb)))
    if has_gain:
        K0 = xs[0][2]
        args.append(gain.reshape(1, K0).astype(F32))
        in_specs.append(pl.BlockSpec((1, K0), lambda i, j: (0, 0)))
    for (arr, cb, K), w in zip(xs, ws):
        args.append(w)
        in_specs.append(pl.BlockSpec((K, tn), lambda i, j: (0, j)))
    if mode in ("resid", "ple") and not resid_is_x:
        args.append(r_arr)
        in_specs.append(pl.BlockSpec((tm, tn), lambda i, j: (i, j)))
    if mode == "ple":
        _, p, wp = ple
        args += [p, wp]
        in_specs += [pl.BlockSpec((tm, p.shape[1]), lambda i, j: (i, 0)),
                     pl.BlockSpec((p.shape[1], tn), lambda i, j: (0, j))]
    if out_gain is not None:
        args.append(out_gain.reshape(1, N).astype(F32))
        in_specs.append(pl.BlockSpec((1, N), lambda i, j: (0, 0)))
    out_shape = [jax.ShapeDtypeStruct((M, N), out_dtype)]
    out_specs = [pl.BlockSpec((tm, tn), lambda i, j: (i, j))]
    if emit_xn:
        out_shape.append(jax.ShapeDtypeStruct((M, xs[0][2] // 2), jnp.uint32))
        out_specs.append(pl.BlockSpec((tm, xs[0][2] // 2), lambda i, j: (i, 0)))
    scratch = [pltpu.VMEM((tm, xs[0][2]), BF16)] if has_gain else []
    res = pl.pallas_call(
        functools.partial(_mm_body, nx=nx, has_gain=has_gain, emit_xn=emit_xn, mode=mode, resid_is_x=resid_is_x,
                          has_out_gain=out_gain is not None),
        out_shape=out_shape, grid=(M // tm, N // tn), in_specs=in_specs, out_specs=out_specs,
        scratch_shapes=scratch, compiler_params=_params("arbitrary", "arbitrary"), name=name,
    )(*args)
    return res if emit_xn else res[0]


def _rope_tables(positions):
    def tables(dim):
        inv = 1.0 / (ROPE_THETA ** (jnp.arange(0, dim, 2, dtype=F32) / dim))
        ang = positions.astype(F32)[..., None] * inv
        return jnp.cos(ang), jnp.sin(ang)

    lead = positions.shape
    c, s = tables(HEAD_DIM)
    c128 = jnp.concatenate([c, c], -1)
    s128 = jnp.concatenate([-s, s], -1)
    c, s = tables(B_ROPE)
    z = jnp.zeros_like(c)
    c64 = jnp.concatenate([c, c, z, z], -1)
    s64a = jnp.concatenate([-s, z, z, z], -1)
    s64b = jnp.concatenate([z, s, z, z], -1)
    n = int(np.prod(lead))
    return [t.reshape(n, LANES) for t in (c128, s128, c64, s64a, s64b)]


def _rot128(x, c, s):
    return x * c + pltpu.roll(x, 64, 1) * s


def _rot64(x, c, sa, sb):
    return x * c + pltpu.roll(x, 96, 1) * sa + pltpu.roll(x, 32, 1) * sb


def _prep_body(y_ref, c128, s128, c64, s64a, s64b, *o_refs, program):
    for src, mode, scale, oi, dst in program:
        x = y_ref[:, src * LANES:(src + 1) * LANES].astype(F32)
        if mode == "rot128":
            x = _rot128(x, c128[...], s128[...])
        elif mode == "rot64":
            x = _rot64(x, c64[...], s64a[...], s64b[...])
        if scale != 1.0:
            x = x * scale
        o_refs[oi][:, dst * LANES:(dst + 1) * LANES] = x.astype(o_refs[oi].dtype)


def _prep(y, tables, program, outs, *, name, tm=256):
    M, C = y.shape
    tm = _tile(M, tm)
    row = lambda i: (i, 0)
    return pl.pallas_call(
        functools.partial(_prep_body, program=tuple(program)),
        out_shape=[jax.ShapeDtypeStruct((M, nb * LANES), dt) for nb, dt in outs],
        grid=(M // tm,),
        in_specs=[pl.BlockSpec((tm, C), row)] + [pl.BlockSpec((tm, LANES), row)] * 5,
        out_specs=[pl.BlockSpec((tm, nb * LANES), row) for nb, _ in outs],
        compiler_params=_params("arbitrary"), name=name,
    )(y, *tables)


def _pair_tables(S, tq, tk, window):
    qi_l, ki_l, fl_l, mk_l, masks, ids = [], [], [], [], [], {}
    for qi in range(S // tq):
        q0, q1 = qi * tq, (qi + 1) * tq - 1
        ks = []
        for ki in range(S // tk):
            k0, k1 = ki * tk, (ki + 1) * tk - 1
            if k0 > q1 or (window is not None and q0 - k1 >= window):
                continue
            full = k1 <= q0 and (window is None or q1 - k0 < window)
            mid = 0
            if not full:
                delta = q0 - k0
                if delta not in ids:
                    rel = np.arange(tq)[:, None] + delta - np.arange(tk)[None, :]
                    ok = (rel >= 0) if window is None else ((rel >= 0) & (rel < window))
                    masks.append(np.where(ok, 0.0, NEG_INF).astype(np.float32))
                    ids[delta] = len(masks)
                mid = ids[delta]
            ks.append((ki, mid))
        for n, (ki, mid) in enumerate(ks):
            qi_l.append(qi)
            ki_l.append(ki)
            fl_l.append((1 if n == 0 else 0) | (2 if n == len(ks) - 1 else 0))
            mk_l.append(mid)
    if not masks:
        masks.append(np.zeros((tq, tk), np.float32))
    tabs = [jnp.asarray(np.array(a, np.int32)) for a in (qi_l, ki_l, fl_l, mk_l)]
    return tabs, jnp.asarray(np.stack(masks))


def _flash_body(qi_t, ki_t, fl_t, mk_t, *refs, nq, nkv, q_src, k_src, v_src, tq, rc, has_sink,
                has_bias, has_sel):
    nc = len(q_src)
    it = iter(refs)
    q_refs = [next(it) for _ in range(nq)]
    kv_refs = [next(it) for _ in range(nkv)]
    mask_ref = next(it)
    sink_ref = next(it) if has_sink else None
    cum_ref = next(it) if has_bias else None
    cumt_ref = next(it) if has_bias else None
    sel_ref = next(it) if has_sel else None
    exp_ref = next(it) if has_sel else None
    o_ref = next(it)
    m_sc, acc_sc = next(it), next(it)
    cq_sc = next(it) if has_bias else None

    step_id = pl.program_id(2)
    fl, mk = fl_t[step_id], mk_t[step_id]

    def cat(refs_, src, rows=slice(None)):
        xs = [refs_[pi][0, rows, off:off + LANES] for pi, off in src]
        return xs[0] if len(xs) == 1 else jnp.concatenate(xs, axis=-1)

    @pl.when((fl & 1) != 0)
    def _init():
        m_sc[...] = jnp.full(m_sc.shape, NEG_INF, F32)
        acc_sc[...] = jnp.zeros(acc_sc.shape, F32)
        if has_bias:
            lane = lax.broadcasted_iota(jnp.int32, (tq, LANES), 1)
            for g in range(nc):
                head = pl.program_id(1) * nc + g
                cq_sc[g] = jnp.sum(jnp.where(lane == head, cum_ref[0], 0.0), axis=-1, keepdims=True)

    tk = kv_refs[0].shape[1]
    reps = tk // LANES
    ones = jnp.ones((tk, LANES), BF16)

    def step(masked):
        ks = [cat(kv_refs, k_src[g]) for g in range(nc)]
        vs = [jnp.concatenate([kv_refs[pi][0, :, off:off + LANES], ones], axis=1) for pi, off in v_src]
        for r in range(tq // rc):
            rows = slice(r * rc, (r + 1) * rc)
            add = mask_ref[mk - 1, rows, :] if masked else None
            if has_sel:
                hidden = (jnp.dot(sel_ref[0, 0, rows, :], exp_ref[0], preferred_element_type=F32) - 1.0) * (-NEG_INF)
                add = hidden if add is None else add + hidden
            for g in range(nc):
                q = cat(q_refs, q_src[g], rows)
                s = lax.dot_general(q, ks[g], (((1,), (1,)), ((), ())), preferred_element_type=F32)
                if has_bias:
                    s = s + (cq_sc[g, rows, :] - cumt_ref[0, g])
                if add is not None:
                    s = s + add
                m_prev = m_sc[g, rows, :]
                m_new = jnp.maximum(m_prev, jnp.max(s, axis=-1, keepdims=True))
                p = jnp.exp2(s - (jnp.concatenate([m_new] * reps, axis=1) if reps > 1 else m_new))
                alpha = jnp.exp2(m_prev - m_new)
                acc_sc[g, rows, :] = (jnp.concatenate([alpha, alpha], axis=1) * acc_sc[g, rows, :]
                                      + jnp.dot(p.astype(BF16), vs[g], preferred_element_type=F32))
                m_sc[g, rows, :] = m_new

    @pl.when(mk != 0)
    def _():
        step(True)

    @pl.when(mk == 0)
    def _():
        step(False)

    @pl.when((fl & 2) != 0)
    def _finish():
        for g in range(nc):
            m, acc, l = m_sc[g], acc_sc[g, :, :LANES], acc_sc[g, :, LANES:]
            if has_sink:
                sk = sink_ref[0, g:g + 1, 0:1] * LOG2E
                m_f = jnp.maximum(m, sk)
                w = jnp.exp2(m - m_f)
                l = l * w + jnp.exp2(sk - m_f)
                acc = acc * w
            o_ref[0, :, g * LANES:(g + 1) * LANES] = (acc / l).astype(o_ref.dtype)


def _flash(q_parts, kv_parts, q_src, k_src, v_src, *, name, B, S, n_steps, tq, tk, window=None, sinks=None,
           cum=None, cumt=None, sel=None, out_dtype=BF16, rc=FLASH_ROW_CHUNK):
    nc = len(q_src)
    tq, tk = _tile(S, tq), _tile(S, tk)
    tabs, masks = _pair_tables(S, tq, tk, window)
    npairs = int(tabs[0].shape[0])
    args, in_specs = [], []
    for arr, width, cf in q_parts:
        args.append(arr)
        in_specs.append(pl.BlockSpec((1, tq, width), lambda b, h, s, qt, kt, ft, mt, cf=cf: (b, qt[s], cf(h))))
    for arr, width, cf in kv_parts:
        args.append(arr)
        in_specs.append(pl.BlockSpec((1, tk, width), lambda b, h, s, qt, kt, ft, mt, cf=cf: (b, kt[s], cf(h))))
    args.append(masks)
    in_specs.append(pl.BlockSpec(masks.shape, lambda b, h, s, qt, kt, ft, mt: (0, 0, 0)))
    if sinks is not None:
        args.append(jnp.broadcast_to(sinks.astype(F32).reshape(n_steps, nc, 1), (n_steps, nc, LANES)))
        in_specs.append(pl.BlockSpec((1, nc, LANES), lambda b, h, s, qt, kt, ft, mt: (h, 0, 0)))
    if cum is not None:
        args += [cum, cumt]
        in_specs += [pl.BlockSpec((1, tq, LANES), lambda b, h, s, qt, kt, ft, mt: (b, qt[s], 0)),
                     pl.BlockSpec((1, nc, 1, tk), lambda b, h, s, qt, kt, ft, mt: (b, h, 0, kt[s]))]
    if sel is not None:
        per = tk // D_SLC_LEN
        e = np.zeros((S // tk, LANES, tk), np.float32)
        for ki in range(S // tk):
            e[ki, ki * per + np.arange(tk) // D_SLC_LEN, np.arange(tk)] = 1.0
        args += [sel, jnp.asarray(e, BF16)]
        in_specs += [pl.BlockSpec((1, 1, tq, LANES), lambda b, h, s, qt, kt, ft, mt: (b, h, qt[s], 0)),
                     pl.BlockSpec((1, LANES, tk), lambda b, h, s, qt, kt, ft, mt: (kt[s], 0, 0))]
    scratch = [pltpu.VMEM((nc, tq, LANES), F32), pltpu.VMEM((nc, tq, 2 * LANES), F32)]
    if cum is not None:
        scratch.append(pltpu.VMEM((nc, tq, 1), F32))
    body = functools.partial(_flash_body, nq=len(q_parts), nkv=len(kv_parts), q_src=q_src, k_src=k_src,
                             v_src=v_src, tq=tq, rc=_tile(tq, rc), has_sink=sinks is not None,
                             has_bias=cum is not None, has_sel=sel is not None)
    return pl.pallas_call(
        body, out_shape=jax.ShapeDtypeStruct((B, S, n_steps * nc * LANES), out_dtype),
        grid_spec=pltpu.PrefetchScalarGridSpec(
            num_scalar_prefetch=4, grid=(B, n_steps, npairs), in_specs=in_specs,
            out_specs=pl.BlockSpec((1, tq, nc * LANES), lambda b, h, s, qt, kt, ft, mt: (b, qt[s], h)),
            scratch_shapes=scratch),
        compiler_params=_params("arbitrary", "arbitrary", "arbitrary"), name=name,
    )(*tabs, *args)


def _gqa_flash(q, k, v, *, name, B, S, Hk, G, **kw):
    head = lambda h: h
    return _flash([(q, G * LANES, head)], [(k, LANES, head), (v, LANES, head)],
                  [[(0, g * LANES)] for g in range(G)], [[(0, 0)]] * G, [(1, 0)] * G,
                  name=name, B=B, S=S, n_steps=Hk, **kw)


def _cum_body(y_ref, b_ref, tri_ref, o_ref, carry):
    @pl.when(pl.program_id(1) == 0)
    def _():
        carry[...] = jnp.zeros(carry.shape, F32)
    x = y_ref[0] + b_ref[...]
    logf = jnp.minimum(x, 0.0) - jnp.log1p(jnp.exp(-jnp.abs(x)))
    cum = jnp.dot(tri_ref[...], logf, preferred_element_type=F32, precision=lax.Precision.HIGHEST) + carry[...]
    o_ref[0] = cum * LOG2E
    carry[...] = cum[-1:, :]


def _forget_cum(y3, col_block, bias_row, *, ts=512):
    B, S, _ = y3.shape
    ts = _tile(S, ts)
    tri = jnp.asarray(np.tril(np.ones((ts, ts), np.float32)))
    return pl.pallas_call(
        _cum_body, out_shape=jax.ShapeDtypeStruct((B, S, LANES), F32), grid=(B, S // ts),
        in_specs=[pl.BlockSpec((1, ts, LANES), lambda b, s: (b, s, col_block)),
                  pl.BlockSpec((1, LANES), lambda b, s: (0, 0)),
                  pl.BlockSpec((ts, ts), lambda b, s: (0, 0))],
        out_specs=pl.BlockSpec((1, ts, LANES), lambda b, s: (b, s, 0)),
        scratch_shapes=[pltpu.VMEM((1, LANES), F32)],
        compiler_params=_params("arbitrary", "arbitrary"), name="forget_cum",
    )(y3, bias_row, tri)


def _compress_body(*refs, rope, nc):
    if rope:
        z_ref, pe_ref, w1_ref, w2_ref, c_ref, s_ref, o_ref = refs
    else:
        z_ref, pe_ref, w1_ref, w2_ref, o_ref = refs
    half = D_CMP_LEN // 2
    width = D_KV_HEADS * HEAD_DIM
    for hk in range(D_KV_HEADS):
        u = jnp.zeros((nc, w1_ref.shape[1]), F32)
        v = jnp.zeros((nc, w1_ref.shape[1]), F32)
        for l in range(half):
            z = z_ref[0, :, l * width + hk * HEAD_DIM:l * width + (hk + 1) * HEAD_DIM]
            zu = (z + pe_ref[l:l + 1, :]).astype(BF16)
            zv = (z + pe_ref[half + l:half + l + 1, :]).astype(BF16)
            u = u + jnp.dot(zu, w1_ref[l * HEAD_DIM:(l + 1) * HEAD_DIM, :], preferred_element_type=F32)
            v = v + jnp.dot(zv, w1_ref[(half + l) * HEAD_DIM:(half + l + 1) * HEAD_DIM, :],
                            preferred_element_type=F32)
        pre = u + pltpu.roll(v, nc - 1, 0)
        hid = jax.nn.gelu(pre, approximate=True)
        out = jnp.dot(hid.astype(BF16), w2_ref[...], preferred_element_type=F32)
        if rope:
            out = _rot128(out, c_ref[0], s_ref[0])
        o_ref[0, hk] = out.astype(o_ref.dtype)


def _compress(z, pe, w1, w2, rope_tabs=None):
    B, S, W = z.shape
    nc = S // D_CMP_STRIDE
    zc = z.reshape(B, nc, D_CMP_STRIDE * W)
    args = [zc, pe.astype(F32), w1.astype(BF16), w2.astype(BF16)]
    in_specs = [pl.BlockSpec((1, nc, D_CMP_STRIDE * W), lambda b: (b, 0, 0)),
                pl.BlockSpec(pe.shape, lambda b: (0, 0)),
                pl.BlockSpec(w1.shape, lambda b: (0, 0)),
                pl.BlockSpec(w2.shape, lambda b: (0, 0))]
    if rope_tabs is not None:
        args += list(rope_tabs)
        in_specs += [pl.BlockSpec((1, nc, LANES), lambda b: (b, 0, 0))] * 2
    return pl.pallas_call(
        functools.partial(_compress_body, rope=rope_tabs is not None, nc=nc),
        out_shape=jax.ShapeDtypeStruct((B, D_KV_HEADS, nc, HEAD_DIM), BF16), grid=(B,),
        in_specs=in_specs, out_specs=pl.BlockSpec((1, D_KV_HEADS, nc, HEAD_DIM), lambda b: (b, 0, 0, 0)),
        compiler_params=_params("arbitrary"), name="nsa_compress",
    )(*args)


def _cmp_attn_body(q_ref, k_ref, v_ref, ov_ref, o_ref, sel_ref, *, G, tq, nc, n_cmp, n_slc, topn):
    qi = pl.program_id(2)
    t = qi * tq + lax.broadcasted_iota(jnp.int32, (tq, nc), 0)
    c = lax.broadcasted_iota(jnp.int32, (tq, nc), 1)
    valid = (c * D_CMP_STRIDE + (D_CMP_LEN - 1) <= t) & (c < n_cmp)
    k = k_ref[0, 0]
    v = v_ref[0, 0]
    psum = jnp.zeros((tq, nc), F32)
    for g in range(G):
        q = q_ref[0, :, g * LANES:(g + 1) * LANES]
        s = lax.dot_general(q, k, (((1,), (1,)), ((), ())), preferred_element_type=F32)
        s = jnp.where(valid, s, NEG_INF)
        e = jnp.where(valid, jnp.exp2(s - jnp.max(s, axis=-1, keepdims=True)), 0.0)
        p = e / jnp.maximum(jnp.sum(e, axis=-1, keepdims=True), jnp.finfo(F32).tiny)
        o_ref[0, :, g * LANES:(g + 1) * LANES] = jnp.dot(
            p.astype(BF16), v, preferred_element_type=F32).astype(o_ref.dtype)
        psum = psum + p
    imp = jnp.dot(psum, ov_ref[...], preferred_element_type=F32, precision=lax.Precision.HIGHEST)
    lane = lax.broadcasted_iota(jnp.int32, (tq, LANES), 1)
    trow = qi * tq + lax.broadcasted_iota(jnp.int32, (tq, LANES), 0)
    cur = jnp.right_shift(trow, int(math.log2(D_SLC_LEN)))
    forced = (lane == 0) | (lane == cur) | (lane == cur - 1)
    imp = jnp.where(lane * D_SLC_LEN > trow, NEG_INF, imp + jnp.where(forced, FORCE_BONUS, 0.0))
    imp = jnp.where(lane >= n_slc, TAKEN, imp)
    chosen = jnp.zeros((tq, LANES), F32)
    lane_f = lane.astype(F32)
    for _ in range(topn):
        mx = jnp.max(imp, axis=-1, keepdims=True)
        idx = jnp.min(jnp.where(imp == mx, lane_f, float(LANES)), axis=-1, keepdims=True)
        hit = lane_f == idx
        chosen = jnp.where(hit, 1.0, chosen)
        imp = jnp.where(hit, TAKEN, imp)
    sel_ref[0, 0] = chosen.astype(sel_ref.dtype)


def _cmp_attn(q, k_cmp, v_cmp, *, tq=256):
    B, S, _ = q.shape
    Hk, G = D_KV_HEADS, D_HEADS // D_KV_HEADS
    nc = S // D_CMP_STRIDE
    n_cmp = (S - D_CMP_LEN) // D_CMP_STRIDE + 1
    n_slc = S // D_SLC_LEN
    tq = _tile(S, tq)
    c0 = np.arange(nc) * D_CMP_STRIDE
    s0 = np.arange(LANES) * D_SLC_LEN
    ov = ((c0[:, None] < (s0 + D_SLC_LEN)[None, :]) & ((c0 + D_CMP_LEN)[:, None] > s0[None, :])
          & (np.arange(nc) < n_cmp)[:, None] & (np.arange(LANES) < n_slc)[None, :]).astype(np.float32)
    body = functools.partial(_cmp_attn_body, G=G, tq=tq, nc=nc, n_cmp=n_cmp, n_slc=n_slc,
                             topn=min(D_SLC_TOPN, n_slc))
    return pl.pallas_call(
        body,
        out_shape=[jax.ShapeDtypeStruct((B, S, Hk * G * LANES), F32),
                   jax.ShapeDtypeStruct((B, Hk, S, LANES), BF16)],
        grid=(B, Hk, S // tq),
        in_specs=[pl.BlockSpec((1, tq, G * LANES), lambda b, h, i: (b, i, h)),
                  pl.BlockSpec((1, 1, nc, LANES), lambda b, h, i: (b, h, 0, 0)),
                  pl.BlockSpec((1, 1, nc, LANES), lambda b, h, i: (b, h, 0, 0)),
                  pl.BlockSpec((nc, LANES), lambda b, h, i: (0, 0))],
        out_specs=[pl.BlockSpec((1, tq, G * LANES), lambda b, h, i: (b, i, h)),
                   pl.BlockSpec((1, 1, tq, LANES), lambda b, h, i: (b, h, i, 0))],
        compiler_params=_params("arbitrary", "arbitrary", "arbitrary"), name="nsa_cmp_attn",
    )(q, k_cmp, v_cmp, jnp.asarray(ov))


def _gate_body(y_ref, a_ref, b_ref, c_ref, o_ref, *, lane0):
    g = jax.nn.sigmoid(y_ref[...])
    for h in range(D_HEADS):
        cols = slice(h * LANES, (h + 1) * LANES)
        ga = g[:, lane0 + h:lane0 + h + 1]
        gb = g[:, lane0 + D_HEADS + h:lane0 + D_HEADS + h + 1]
        gc = g[:, lane0 + 2 * D_HEADS + h:lane0 + 2 * D_HEADS + h + 1]
        o_ref[:, cols] = (ga * a_ref[:, cols] + gb * b_ref[:, cols] + gc * c_ref[:, cols]).astype(o_ref.dtype)


def _nsa_gate(y, col_block, lane0, o_cmp, o_slc, o_win, *, tm=512):
    M = y.shape[0]
    W = o_cmp.shape[1]
    tm = _tile(M, tm)
    row = lambda i: (i, 0)
    return pl.pallas_call(
        functools.partial(_gate_body, lane0=lane0), out_shape=jax.ShapeDtypeStruct((M, W), BF16),
        grid=(M // tm,),
        in_specs=[pl.BlockSpec((tm, LANES), lambda i: (i, col_block))] + [pl.BlockSpec((tm, W), row)] * 3,
        out_specs=pl.BlockSpec((tm, W), row), compiler_params=_params("arbitrary"), name="nsa_gate",
    )(y, o_cmp, o_slc, o_win)


def _route_body(lg_ref, b_ref, tri_ref, e_ref, w_ref, pos_ref, cnt_ref, carry, *, tm):
    @pl.when(pl.program_id(0) == 0)
    def _():
        carry[...] = jnp.zeros(carry.shape, F32)
    lane = lax.broadcasted_iota(jnp.int32, (tm, LANES), 1)
    logits = lg_ref[...] + b_ref[...]
    gl = jnp.where(lane < N_GROUPS, logits, -jnp.inf)
    gmax = jnp.max(gl, axis=-1, keepdims=True)
    g_val = 1.0 / jnp.sum(jnp.exp(gl - gmax), axis=-1, keepdims=True)
    g_idx = jnp.min(jnp.where(gl == gmax, lane, LANES), axis=-1, keepdims=True)
    lo = N_GROUPS + EXPERTS_PER_GROUP * g_idx
    el = jnp.where((lane >= lo) & (lane < lo + EXPERTS_PER_GROUP), logits, -jnp.inf)
    e1 = jnp.max(el, axis=-1, keepdims=True)
    i1 = jnp.min(jnp.where(el == e1, lane, LANES), axis=-1, keepdims=True)
    el2 = jnp.where(lane == i1, -jnp.inf, el)
    e2 = jnp.max(el2, axis=-1, keepdims=True)
    i2 = jnp.min(jnp.where(el2 == e2, lane, LANES), axis=-1, keepdims=True)
    r = jnp.exp(e2 - e1)
    w1 = g_val / (1.0 + r)
    w2 = w1 * r
    x1, x2 = i1 - N_GROUPS, i2 - N_GROUPS
    e_ref[...] = jnp.where(lane == 0, x1, jnp.where(lane == 1, x2, 0))
    w_ref[...] = jnp.where(lane == 0, w1, jnp.where(lane == 1, w2, 0.0))
    hot1 = lane == x1
    hot2 = lane == x2
    both = jnp.where(hot1 | hot2, 1.0, 0.0)
    before = jnp.dot(tri_ref[...], both.astype(BF16), preferred_element_type=F32) + carry[...]
    p1 = jnp.sum(jnp.where(hot1, before, 0.0), axis=-1, keepdims=True)
    p2 = jnp.sum(jnp.where(hot2, before, 0.0), axis=-1, keepdims=True)
    pos_ref[...] = jnp.where(lane == 0, p1, jnp.where(lane == 1, p2, 0.0)).astype(jnp.int32)
    carry[...] = carry[...] + jnp.sum(both, axis=0, keepdims=True)
    cnt_ref[...] = carry[...].astype(jnp.int32)


def _route(logits, bias_row, *, tm=512):
    T = logits.shape[0]
    tm = _tile(T, tm)
    tri = jnp.asarray(np.tril(np.ones((tm, tm), np.float32), -1), BF16)
    row = lambda i: (i, 0)
    fixed = lambda i: (0, 0)
    return pl.pallas_call(
        functools.partial(_route_body, tm=tm),
        out_shape=[jax.ShapeDtypeStruct((T, LANES), jnp.int32), jax.ShapeDtypeStruct((T, LANES), F32),
                   jax.ShapeDtypeStruct((T, LANES), jnp.int32), jax.ShapeDtypeStruct((1, LANES), jnp.int32)],
        grid=(T // tm,),
        in_specs=[pl.BlockSpec((tm, LANES), row), pl.BlockSpec((1, LANES), fixed), pl.BlockSpec((tm, tm), fixed)],
        out_specs=[pl.BlockSpec((tm, LANES), row)] * 3 + [pl.BlockSpec((1, LANES), fixed)],
        scratch_shapes=[pltpu.VMEM((1, LANES), F32)],
        compiler_params=_params("arbitrary"), name="moe_route",
    )(logits, bias_row, tri)


def _dispatch_body(d_ref, ends_ref, pad_ref, nu_ref, x_ref, out_ref, zero_sc, sem, zsem, *, tm, nblk):
    base = pl.program_id(0) * tm

    @pl.when(pl.program_id(0) == 0)
    def _():
        zero_sc[...] = jnp.zeros(zero_sc.shape, zero_sc.dtype)

        def zero_block(first_row):
            return pltpu.make_async_copy(zero_sc, out_ref.at[pl.ds(pl.multiple_of(first_row, MOE_ROWS), MOE_ROWS)],
                                         zsem)

        for wait in (False, True):
            for e in range(N_EXPERTS):
                for live, first_row in ((pad_ref[e] > 0, ends_ref[e] - MOE_ROWS),
                                        (nu_ref[0] + e < nblk, (nu_ref[0] + e) * MOE_ROWS)):
                    @pl.when(live)
                    def _():
                        zero_block(first_row).wait() if wait else zero_block(first_row).start()

    def issue(r, c):
        for k in range(TOP_K):
            row = d_ref[(base + r) * TOP_K + k]
            pltpu.make_async_copy(x_ref.at[pl.ds(r, 1)], out_ref.at[pl.ds(row, 1)], sem).start()
        return c

    lax.fori_loop(0, tm, issue, 0)
    for _ in range(TOP_K):
        pltpu.make_async_copy(x_ref, out_ref.at[pl.ds(0, tm)], sem).wait()


def _dispatch(xn, dest, ends, padded, n_used, n_rows, *, tm=512):
    T, D = xn.shape
    tm = _tile(T, tm)
    return pl.pallas_call(
        functools.partial(_dispatch_body, tm=tm, nblk=n_rows // MOE_ROWS),
        out_shape=jax.ShapeDtypeStruct((n_rows, D), xn.dtype),
        grid_spec=pltpu.PrefetchScalarGridSpec(
            num_scalar_prefetch=4, grid=(T // tm,),
            in_specs=[pl.BlockSpec((tm, D), lambda i, *_: (i, 0))],
            out_specs=pl.BlockSpec(memory_space=pl.ANY),
            scratch_shapes=[pltpu.VMEM((MOE_ROWS, D), xn.dtype), pltpu.SemaphoreType.DMA(()),
                            pltpu.SemaphoreType.DMA(())]),
        compiler_params=pltpu.CompilerParams(dimension_semantics=("arbitrary",), has_side_effects=True,
                                             vmem_limit_bytes=VMEM_LIMIT_BYTES, disable_bounds_checks=True),
        name="moe_dispatch",
    )(dest, ends, padded, n_used, xn)


def _expert_body(be_ref, nu_ref, x_ref, wg_ref, wu_ref, wd_ref, o_ref, wg_sc, wu_sc, wd_sc):
    i = pl.program_id(0)
    fresh = (i == 0) | (be_ref[i] != be_ref[jnp.maximum(i - 1, 0)])

    @pl.when(fresh)
    def _():
        wg_sc[...] = wg_ref[0, 0].astype(BF16)
        wu_sc[...] = wu_ref[0, 0].astype(BF16)
        wd_sc[...] = wd_ref[0, 0].astype(BF16)

    @pl.when(i < nu_ref[0])
    def _():
        x = jnp.concatenate(_unpack_bf16_pairs(x_ref[...]), axis=1).astype(BF16)
        gate = jnp.dot(x, wg_sc[...], preferred_element_type=F32)
        up = jnp.dot(x, wu_sc[...], preferred_element_type=F32)
        hid = (gate * jax.nn.sigmoid(gate) * up).astype(BF16)
        o_ref[...] = _pack_bf16_pairs(jnp.dot(hid, wd_sc[...], preferred_element_type=F32))

    @pl.when(i >= nu_ref[0])
    def _():
        o_ref[...] = jnp.zeros(o_ref.shape, o_ref.dtype)


def _experts(xb, blk_e, n_used, layer, w_gate, w_up, w_down):
    D, Hd = w_gate.shape[2], w_gate.shape[3]
    nblk = xb.shape[0] // MOE_ROWS
    return pl.pallas_call(
        _expert_body, out_shape=jax.ShapeDtypeStruct(xb.shape, xb.dtype),
        grid_spec=pltpu.PrefetchScalarGridSpec(
            num_scalar_prefetch=2, grid=(nblk,),
            in_specs=[pl.BlockSpec((MOE_ROWS, D // 2), lambda i, be, nu: (jnp.minimum(i, nu[0] - 1), 0)),
                      pl.BlockSpec((1, 1, D, Hd), lambda i, be, nu: (layer, be[i], 0, 0)),
                      pl.BlockSpec((1, 1, D, Hd), lambda i, be, nu: (layer, be[i], 0, 0)),
                      pl.BlockSpec((1, 1, Hd, D), lambda i, be, nu: (layer, be[i], 0, 0))],
            out_specs=pl.BlockSpec((MOE_ROWS, D // 2), lambda i, be, nu: (i, 0)),
            scratch_shapes=[pltpu.VMEM((D, Hd), BF16), pltpu.VMEM((D, Hd), BF16), pltpu.VMEM((Hd, D), BF16)]),
        compiler_params=_params("arbitrary"), name="moe_experts",
    )(blk_e, n_used, xb, w_gate, w_up, w_down)


def _collect_body(d_ref, h_ref, w_ref, yb_ref, o_ref, buf_a, buf_b, sem, *, tm, nsteps):
    i = pl.program_id(0)
    slot = lax.rem(i, 2)

    def fetch(step, slot_):
        base = step * tm

        def issue(r, c):
            for k, buf in enumerate((buf_a, buf_b)):
                row = d_ref[(base + r) * TOP_K + k]
                pltpu.make_async_copy(yb_ref.at[pl.ds(row, 1)], buf.at[slot_, pl.ds(r, 1)], sem.at[slot_]).start()
            return c

        lax.fori_loop(0, tm, issue, 0)

    @pl.when(i == 0)
    def _():
        fetch(0, 0)

    @pl.when(i + 1 < nsteps)
    def _():
        fetch(i + 1, 1 - slot)

    for buf in (buf_a, buf_b):
        pltpu.make_async_copy(yb_ref.at[pl.ds(0, tm)], buf.at[slot], sem.at[slot]).wait()
    w = w_ref[...]
    half = h_ref.shape[1] // 2
    for part, ya, yb in zip((slice(0, half), slice(half, None)), _unpack_bf16_pairs(buf_a[slot]),
                            _unpack_bf16_pairs(buf_b[slot])):
        o_ref[:, part] = h_ref[:, part] + w[:, 0:1] * ya + w[:, 1:2] * yb


def _collect(h, yb, dest, wts, *, tm=256):
    T, D = h.shape
    tm = _tile(T, tm)
    nsteps = T // tm
    return pl.pallas_call(
        functools.partial(_collect_body, tm=tm, nsteps=nsteps),
        out_shape=jax.ShapeDtypeStruct((T, D), F32),
        grid_spec=pltpu.PrefetchScalarGridSpec(
            num_scalar_prefetch=1, grid=(nsteps,),
            in_specs=[pl.BlockSpec((tm, D), lambda i, d: (i, 0)), pl.BlockSpec((tm, LANES), lambda i, d: (i, 0)),
                      pl.BlockSpec(memory_space=pl.ANY)],
            out_specs=pl.BlockSpec((tm, D), lambda i, d: (i, 0)),
            scratch_shapes=[pltpu.VMEM((2, tm, D // 2), yb.dtype), pltpu.VMEM((2, tm, D // 2), yb.dtype),
                            pltpu.SemaphoreType.DMA((2,))]),
        compiler_params=pltpu.CompilerParams(dimension_semantics=("arbitrary",), disable_bounds_checks=True,
                                             vmem_limit_bytes=VMEM_LIMIT_BYTES),
        name="moe_collect",
    )(dest, h, wts, yb)


def _moe(h, norm_g, w_group, b_group, w_expert, b_expert, layer, w_gate, w_up, w_down):
    T, D = h.shape
    pad = LANES - N_GROUPS - N_EXPERTS
    w_r = jnp.concatenate([w_group, w_expert, jnp.zeros((D, pad), F32)], axis=1).astype(BF16)
    b_r = jnp.concatenate([b_group, b_expert, jnp.zeros((pad,), F32)]).astype(F32).reshape(1, LANES)
    logits, xn = _mm([(h, 0, D)], [w_r], gain=norm_g, emit_xn=True, tm=512, tn=LANES, name="moe_router")
    eid, wts, pos, cnt = _route(logits, b_r)
    counts = cnt[0, :N_EXPERTS]
    padded = (counts + MOE_ROWS - 1) // MOE_ROWS * MOE_ROWS
    ends = jnp.cumsum(padded)
    offs = ends - padded
    dest = (offs[eid[:, :TOP_K]] + pos[:, :TOP_K]).reshape(T * TOP_K).astype(jnp.int32)
    P = T * TOP_K + N_EXPERTS * MOE_ROWS
    nblk = P // MOE_ROWS
    starts = jnp.arange(nblk, dtype=jnp.int32) * MOE_ROWS
    blk_e = jnp.minimum(jnp.sum(ends[None, :] <= starts[:, None], axis=1), N_EXPERTS - 1).astype(jnp.int32)
    n_used = (ends[-1:] // MOE_ROWS).astype(jnp.int32)
    xb = _dispatch(xn, dest, ends.astype(jnp.int32), padded.astype(jnp.int32), n_used, P)
    yb = _experts(xb, blk_e, n_used, layer, w_gate, w_up, w_down)
    return _collect(h, yb, dest, wts)


def _pad_cols(w, n):
    return jnp.pad(w, ((0, 0), (0, n - w.shape[1])))


def _layer_even(h, B, S, tabs, norm_g, w_in, sinks, q_lat_norm, kv_lat_norm, w_uq, w_ukv, w_o):
    T, D = h.shape
    n_in = 2560
    y = _mm([(h, 0, D)], [_pad_cols(w_in, n_in).astype(BF16)], gain=norm_g, tn=512, name="ab_in")
    sa = HEAD_DIM ** -0.5 * LOG2E
    prog = ([(i, "rot128", sa, 0, i) for i in range(8)] + [(8 + i, "rot128", 1.0, 1, i) for i in range(2)]
            + [(10 + i, "copy", 1.0, 2, i) for i in range(2)] + [(18, "rot64", 1.0, 3, 0)])
    q_a, k_a, v_a, k_r = _prep(y, tabs, prog, [(8, BF16), (2, BF16), (2, BF16), (1, BF16)], name="ab_prep")
    wq = w_uq.reshape(B_Q_LORA, B_HEADS, B_NOPE + B_ROPE)
    wq_n = wq[:, :, :B_NOPE].reshape(B_Q_LORA, B_HEADS * B_NOPE)
    wq_r = jnp.pad(wq[:, :, B_NOPE:], ((0, 0), (0, 0), (0, LANES - B_ROPE))).reshape(B_Q_LORA, B_HEADS * LANES)
    qb = _mm([(y, 1536 // B_Q_LORA, B_Q_LORA)], [jnp.concatenate([wq_n, wq_r], 1).astype(BF16)],
             gain=q_lat_norm, tn=512, name="mla_uq")
    sb = (B_NOPE + B_ROPE) ** -0.5 * LOG2E
    prog = [(i, "copy", sb, 0, i) for i in range(8)] + [(8 + i, "rot64", sb, 1, i) for i in range(8)]
    q_n, q_r = _prep(qb, tabs, prog, [(8, BF16), (8, BF16)], name="mla_prep")
    kv = _mm([(y, 2048 // B_KV_LORA, B_KV_LORA)], [w_ukv.astype(BF16)], gain=kv_lat_norm, out_dtype=BF16,
             tn=512, name="mla_ukv")
    r3 = lambda a: a.reshape(B, S, a.shape[-1])
    o_a = _gqa_flash(r3(q_a), r3(k_a), r3(v_a), name="swa_attn", B=B, S=S, Hk=A_KV_HEADS,
                     G=A_HEADS // A_KV_HEADS, window=A_WINDOW, sinks=sinks, tq=256, tk=256)
    hp = 2
    head = lambda h: h
    o_b = _flash([(r3(q_n), hp * LANES, head), (r3(q_r), hp * LANES, head)],
                 [(r3(kv), hp * 2 * LANES, head), (r3(k_r), LANES, lambda h: 0)],
                 [[(0, g * LANES), (1, g * LANES)] for g in range(hp)],
                 [[(0, 2 * g * LANES), (1, 0)] for g in range(hp)], [(0, (2 * g + 1) * LANES) for g in range(hp)],
                 name="mla_attn", B=B, S=S, n_steps=B_HEADS // hp, tq=1024, tk=1024, rc=256)
    wo = w_o.astype(BF16)
    na = A_HEADS * HEAD_DIM
    return _mm([(o_a.reshape(T, -1), 0, na), (o_b.reshape(T, -1), 0, B_HEADS * B_V)], [wo[:na], wo[na:]],
               resid=h, tm=512, tn=D, name="ab_out")


def _layer_odd(h, B, S, tabs, cmp_tabs, norm_g, w_in, forget_bias, pe_k, w1_k, w2_k, pe_v, w1_v, w2_v, w_o):
    T, D = h.shape
    hc, kvw = C_HEADS * HEAD_DIM, D_KV_HEADS * HEAD_DIM
    o_cf = 3 * hc
    o_dq = o_cf + C_HEADS
    o_dg = o_dq + D_HEADS * HEAD_DIM + 6 * kvw
    w_r = jnp.concatenate([w_in[:, :o_cf], w_in[:, o_dq:o_dg], w_in[:, o_cf:o_dq], w_in[:, o_dg:]], axis=1)
    n_in = 5760
    y = _mm([(h, 0, D)], [_pad_cols(w_r, n_in).astype(BF16)], gain=norm_g, tn=640, name="cd_in")
    misc = (n_in // LANES) - 1
    sc = HEAD_DIM ** -0.5 * LOG2E
    prog = ([(i, "copy", sc, 0, i) for i in range(8)] + [(8 + i, "copy", 1.0, 1, i) for i in range(8)]
            + [(16 + i, "copy", 1.0, 2, i) for i in range(8)] + [(24 + i, "rot128", sc, 3, i) for i in range(8)]
            + [(32 + i, "copy", 1.0, 4, i) for i in range(2)] + [(34 + i, "copy", 1.0, 5, i) for i in range(2)]
            + [(36 + i, "rot128", 1.0, 6, i) for i in range(2)] + [(38 + i, "copy", 1.0, 7, i) for i in range(2)]
            + [(40 + i, "rot128", 1.0, 8, i) for i in range(2)] + [(42 + i, "copy", 1.0, 9, i) for i in range(2)])
    outs = [(8, BF16)] * 4 + [(2, F32), (2, F32)] + [(2, BF16)] * 4
    q_c, k_c, v_c, q_d, z_kc, z_vc, k_s, v_s, k_w, v_w = _prep(y, tabs, prog, outs, name="cd_prep")
    r3 = lambda a: a.reshape(B, S, a.shape[-1])
    fb = jnp.pad(forget_bias.astype(F32), (0, LANES - C_HEADS)).reshape(1, LANES)
    cum = _forget_cum(r3(y), misc, fb)
    cumt = jnp.swapaxes(cum[:, :, :C_HEADS], 1, 2).reshape(B, C_HEADS, 1, S)
    hp = 2
    head = lambda h: h
    o_c = _flash([(r3(q_c), hp * LANES, head)], [(r3(k_c), hp * LANES, head), (r3(v_c), hp * LANES, head)],
                 [[(0, g * LANES)] for g in range(hp)], [[(0, g * LANES)] for g in range(hp)],
                 [(1, g * LANES) for g in range(hp)], name="fox_attn", B=B, S=S, n_steps=C_HEADS // hp,
                 tq=1024, tk=1024, cum=cum, cumt=cumt)
    G = D_HEADS // D_KV_HEADS
    k_cmp = _compress(r3(z_kc), pe_k, w1_k, w2_k, rope_tabs=cmp_tabs)
    v_cmp = _compress(r3(z_vc), pe_v, w1_v, w2_v)
    q3 = r3(q_d)
    o_cmp, sel = _cmp_attn(q3, k_cmp, v_cmp)
    o_slc = _gqa_flash(q3, r3(k_s), r3(v_s), name="nsa_slc_attn", B=B, S=S, Hk=D_KV_HEADS, G=G, sel=sel,
                       out_dtype=F32, tq=1024, tk=1024, rc=256)
    o_win = _gqa_flash(q3, r3(k_w), r3(v_w), name="nsa_win_attn", B=B, S=S, Hk=D_KV_HEADS, G=G,
                       window=D_WINDOW, out_dtype=F32, tq=512, tk=512)
    o_d = _nsa_gate(y, misc, C_HEADS, o_cmp.reshape(T, -1), o_slc.reshape(T, -1), o_win.reshape(T, -1))
    wo = w_o.astype(BF16)
    return _mm([(o_c.reshape(T, -1), 0, hc), (o_d, 0, D_HEADS * HEAD_DIM)], [wo[:hc], wo[hc:]],
               resid=h, tm=512, tn=D, name="cd_out")


def kernel(x, p, positions, ab_w_in, ab_sinks, ab_q_lat_norm, ab_kv_lat_norm, ab_w_uq, ab_w_ukv, ab_w_o,
           cd_w_in, cd_forget_bias, cd_cmp_pe_k, cd_cmp_w1_k, cd_cmp_w2_k, cd_cmp_pe_v, cd_cmp_w1_v,
           cd_cmp_w2_v, cd_w_o, mixer_norm, moe_norm, router_group_w, router_group_b, router_expert_w,
           router_expert_b, expert_w_gate, expert_w_up, expert_w_down, ple_proj, ple_gate_norm, ple_gate_w,
           final_norm):
    B, S, D = x.shape
    T = B * S
    depth = p.shape[0]
    tabs = _rope_tables(positions)
    nc = S // D_CMP_STRIDE
    end = np.minimum(np.arange(nc) * D_CMP_STRIDE + D_CMP_LEN - 1, S - 1)
    cmp_tabs = [t.reshape(B, nc, LANES) for t in _rope_tables(positions[:, end])[:2]]
    h = x.reshape(T, D)
    for i in range(depth):
        j = i // 2
        if i % 2 == 0:
            h = _layer_even(h, B, S, tabs, mixer_norm[i], ab_w_in[j], ab_sinks[j], ab_q_lat_norm[j],
                            ab_kv_lat_norm[j], ab_w_uq[j], ab_w_ukv[j], ab_w_o[j])
        else:
            h = _layer_odd(h, B, S, tabs, cmp_tabs, mixer_norm[i], cd_w_in[j], cd_forget_bias[j],
                           cd_cmp_pe_k[j], cd_cmp_w1_k[j], cd_cmp_w2_k[j], cd_cmp_pe_v[j], cd_cmp_w1_v[j],
                           cd_cmp_w2_v[j], cd_w_o[j])
        h = _moe(h, moe_norm[i], router_group_w[i], router_group_b[i], router_expert_w[i], router_expert_b[i],
                 i, expert_w_gate, expert_w_up, expert_w_down)
        h = _mm([(h, 0, D)], [ple_gate_w[i].astype(BF16)], gain=ple_gate_norm[i], tm=512, tn=D,
                ple=(h, p[i].reshape(T, -1), ple_proj[i].astype(BF16)),
                out_gain=final_norm if i == depth - 1 else None, name="ple")
    return h.reshape(B, S, D)
```

```python
import functools
import math

import numpy as np
import jax
import jax.numpy as jnp
from jax import lax
from jax.experimental import pallas as pl
from jax.experimental.pallas import tpu as pltpu

F32 = jnp.float32
BF16 = jnp.bfloat16

HEAD_DIM = 128
ROPE_THETA = 10000.0
NORM_EPS = 1e-6
NEG_INF = -1e30
TAKEN = -3e38
A_HEADS, A_KV_HEADS, A_WINDOW = 8, 2, 128
B_HEADS, B_Q_LORA, B_KV_LORA, B_NOPE, B_ROPE, B_V = 8, 512, 256, 128, 64, 128
C_HEADS = 8
D_HEADS, D_KV_HEADS = 8, 2
D_CMP_LEN, D_CMP_STRIDE, D_SLC_LEN, D_SLC_TOPN, D_WINDOW = 32, 16, 64, 8, 512
FORCE_BONUS = 1e4
N_GROUPS, EXPERTS_PER_GROUP, TOP_K = 4, 8, 2
N_EXPERTS = N_GROUPS * EXPERTS_PER_GROUP

LANES = 128
VMEM_LIMIT_BYTES = 56 * 1024 * 1024
MOE_ROWS = 256
DMA_UNROLL = 8
FLASH_ROW_CHUNK = 128
LOG2E = math.log2(math.e)


def _params(*sem):
    return pltpu.CompilerParams(dimension_semantics=sem, vmem_limit_bytes=VMEM_LIMIT_BYTES)


def _tile(n, pref):
    t = min(n, pref)
    while n % t:
        t -= 1
    return t


def _pack_bf16_pairs(y):
    n = y.shape[1] // 2
    bits = lambda a: lax.bitcast_convert_type(a.astype(BF16).astype(F32), jnp.uint32)
    return (bits(y[:, :n]) >> 16) | (bits(y[:, n:]) & jnp.uint32(0xFFFF0000))


def _unpack_bf16_pairs(w):
    return (lax.bitcast_convert_type(w << 16, F32), lax.bitcast_convert_type(w & jnp.uint32(0xFFFF0000), F32))


def _rot128(x, c, s):
    return x * c + pltpu.roll(x, 64, 1) * s


def _rot64(x, c, sa, sb):
    return x * c + pltpu.roll(x, 96, 1) * sa + pltpu.roll(x, 32, 1) * sb


def _mm_body(*refs, nx, has_gain, emit_xn, mode, resid_is_x, has_out_gain, col_prog, f32_from):
    it = iter(refs)
    x_refs = [next(it) for _ in range(nx)]
    g_ref = next(it) if has_gain else None
    w_refs = [next(it) for _ in range(nx)]
    r_ref = next(it) if mode in ("resid", "ple") and not resid_is_x else None
    p_ref = next(it) if mode == "ple" else None
    wp_ref = next(it) if mode == "ple" else None
    og_ref = next(it) if has_out_gain else None
    tab_refs = [next(it) for _ in range(5)] if col_prog is not None else None
    o_ref = next(it)
    xo_ref = next(it) if emit_xn else None
    of_ref = next(it) if f32_from is not None else None
    xn_ref = next(it) if has_gain else None

    if has_gain:
        @pl.when(pl.program_id(1) == 0)
        def _():
            x = x_refs[0][...].astype(F32)
            y = x * lax.rsqrt(jnp.mean(x * x, axis=-1, keepdims=True) + NORM_EPS) * g_ref[...]
            xn_ref[...] = y.astype(BF16)
            if emit_xn:
                xo_ref[...] = _pack_bf16_pairs(y)
        lhs = [xn_ref[...]]
    else:
        lhs = [x_ref[...].astype(BF16) for x_ref in x_refs]
    acc = None
    for a, w_ref in zip(lhs, w_refs):
        d = jnp.dot(a, w_ref[...], preferred_element_type=F32)
        acc = d if acc is None else acc + d
    if resid_is_x:
        r_ref = x_refs[0]
    if mode == "resid":
        acc = r_ref[...] + acc
    elif mode == "ple":
        pp = jnp.dot(p_ref[...].astype(BF16), wp_ref[...], preferred_element_type=F32)
        acc = r_ref[...] + pp * jax.nn.sigmoid(acc)
    if has_out_gain:
        acc = acc * lax.rsqrt(jnp.mean(acc * acc, axis=-1, keepdims=True) + NORM_EPS) * og_ref[...]
    if col_prog is None:
        o_ref[...] = acc.astype(o_ref.dtype)
        return
    j = pl.program_id(1)
    nb = o_ref.shape[1] // LANES
    tiles = {}
    for jj in range(len(col_prog) // nb):
        tiles.setdefault(tuple(col_prog[jj * nb:(jj + 1) * nb]), []).append(jj)
    for prog, jjs in tiles.items():
        @pl.when(functools.reduce(jnp.logical_or, [j == jj for jj in jjs]))
        def _():
            c128, s128, c64, s64a, s64b = tab_refs
            for b, (kind, scale) in enumerate(prog):
                x = acc[:, b * LANES:(b + 1) * LANES]
                if kind == "rot128":
                    x = _rot128(x, c128[...], s128[...])
                elif kind == "rot64":
                    x = _rot64(x, c64[...], s64a[...], s64b[...])
                if scale != 1.0:
                    x = x * scale
                o_ref[:, b * LANES:(b + 1) * LANES] = x.astype(o_ref.dtype)
    if f32_from is not None:
        @pl.when(j >= f32_from)
        def _():
            of_ref[...] = acc


def _mm(xs, ws, *, name, gain=None, out_dtype=F32, tm=1024, tn=512, resid=None, ple=None,
        emit_xn=False, out_gain=None, col_prog=None, tables=None, f32_from=None):
    M = xs[0][0].shape[0]
    N = ws[0].shape[1]
    tm, tn = _tile(M, tm), _tile(N, tn)
    nx = len(xs)
    has_gain = gain is not None
    mode = "ple" if ple is not None else ("resid" if resid is not None else "none")
    r_arr = ple[0] if mode == "ple" else resid
    resid_is_x = r_arr is xs[0][0] and tn == N == xs[0][2] and xs[0][1] == 0
    assert out_gain is None or tn == N
    args, in_specs = [], []
    for arr, cb, K in xs:
        args.append(arr)
        in_specs.append(pl.BlockSpec((tm, K), lambda i, j, cb=cb: (i, cb)))
    if has_gain:
        K0 = xs[0][2]
        args.append(gain.reshape(1, K0).astype(F32))
        in_specs.append(pl.BlockSpec((1, K0), lambda i, j: (0, 0)))
    for (arr, cb, K), w in zip(xs, ws):
        args.append(w)
        in_specs.append(pl.BlockSpec((K, tn), lambda i, j: (0, j)))
    if mode in ("resid", "ple") and not resid_is_x:
        args.append(r_arr)
        in_specs.append(pl.BlockSpec((tm, tn), lambda i, j: (i, j)))
    if mode == "ple":
        _, p, wp = ple
        args += [p, wp]
        in_specs += [pl.BlockSpec((tm, p.shape[1]), lambda i, j: (i, 0)),
                     pl.BlockSpec((p.shape[1], tn), lambda i, j: (0, j))]
    if out_gain is not None:
        args.append(out_gain.reshape(1, N).astype(F32))
        in_specs.append(pl.BlockSpec((1, N), lambda i, j: (0, 0)))
    if col_prog is not None:
        assert len(col_prog) * LANES == N
        args += list(tables)
        in_specs += [pl.BlockSpec((tm, LANES), lambda i, j: (i, 0))] * 5
    out_shape = [jax.ShapeDtypeStruct((M, N), out_dtype)]
    out_specs = [pl.BlockSpec((tm, tn), lambda i, j: (i, j))]
    if emit_xn:
        out_shape.append(jax.ShapeDtypeStruct((M, xs[0][2] // 2), jnp.uint32))
        out_specs.append(pl.BlockSpec((tm, xs[0][2] // 2), lambda i, j: (i, 0)))
    if f32_from is not None:
        out_shape.append(jax.ShapeDtypeStruct((M, N - f32_from * tn), F32))
        out_specs.append(pl.BlockSpec((tm, tn), lambda i, j: (i, jnp.maximum(j - f32_from, 0))))
    scratch = [pltpu.VMEM((tm, xs[0][2]), BF16)] if has_gain else []
    res = pl.pallas_call(
        functools.partial(_mm_body, nx=nx, has_gain=has_gain, emit_xn=emit_xn, mode=mode, resid_is_x=resid_is_x,
                          has_out_gain=out_gain is not None,
                          col_prog=None if col_prog is None else tuple(col_prog), f32_from=f32_from),
        out_shape=out_shape, grid=(M // tm, N // tn), in_specs=in_specs, out_specs=out_specs,
        scratch_shapes=scratch, compiler_params=_params("arbitrary", "arbitrary"), name=name,
    )(*args)
    return res if len(res) > 1 else res[0]


def _rope_tables(positions):
    def tables(dim):
        inv = 1.0 / (ROPE_THETA ** (jnp.arange(0, dim, 2, dtype=F32) / dim))
        ang = positions.astype(F32)[..., None] * inv
        return jnp.cos(ang), jnp.sin(ang)

    lead = positions.shape
    c, s = tables(HEAD_DIM)
    c128 = jnp.concatenate([c, c], -1)
    s128 = jnp.concatenate([-s, s], -1)
    c, s = tables(B_ROPE)
    z = jnp.zeros_like(c)
    c64 = jnp.concatenate([c, c, z, z], -1)
    s64a = jnp.concatenate([-s, z, z, z], -1)
    s64b = jnp.concatenate([z, s, z, z], -1)
    n = int(np.prod(lead))
    return [t.reshape(n, LANES) for t in (c128, s128, c64, s64a, s64b)]


def _pair_tables(S, tq, tk, window):
    qi_l, ki_l, fl_l, mk_l, masks, ids = [], [], [], [], [], {}
    for qi in range(S // tq):
        q0, q1 = qi * tq, (qi + 1) * tq - 1
        ks = []
        for ki in range(S // tk):
            k0, k1 = ki * tk, (ki + 1) * tk - 1
            if k0 > q1 or (window is not None and q0 - k1 >= window):
                continue
            full = k1 <= q0 and (window is None or q1 - k0 < window)
            mid = 0
            if not full:
                delta = q0 - k0
                if delta not in ids:
                    rel = np.arange(tq)[:, None] + delta - np.arange(tk)[None, :]
                    ok = (rel >= 0) if window is None else ((rel >= 0) & (rel < window))
                    masks.append(np.where(ok, 0.0, NEG_INF).astype(np.float32))
                    ids[delta] = len(masks)
                mid = ids[delta]
            ks.append((ki, mid))
        for n, (ki, mid) in enumerate(ks):
            qi_l.append(qi)
            ki_l.append(ki)
            fl_l.append((1 if n == 0 else 0) | (2 if n == len(ks) - 1 else 0))
            mk_l.append(mid)
    if not masks:
        masks.append(np.zeros((tq, tk), np.float32))
    tabs = [jnp.asarray(np.array(a, np.int32)) for a in (qi_l, ki_l, fl_l, mk_l)]
    return tabs, jnp.asarray(np.stack(masks))


def _flash_body(qi_t, ki_t, fl_t, mk_t, *refs, nq, nkv, q_src, k_src, v_src, tq, rc, has_sink,
                has_bias, has_sel):
    nc = len(q_src)
    it = iter(refs)
    q_refs = [next(it) for _ in range(nq)]
    kv_refs = [next(it) for _ in range(nkv)]
    mask_ref = next(it)
    sink_ref = next(it) if has_sink else None
    cum_ref = next(it) if has_bias else None
    cumt_ref = next(it) if has_bias else None
    sel_ref = next(it) if has_sel else None
    exp_ref = next(it) if has_sel else None
    o_ref = next(it)
    m_sc, acc_sc = next(it), next(it)
    cq_sc = next(it) if has_bias else None

    step_id = pl.program_id(2)
    fl, mk = fl_t[step_id], mk_t[step_id]

    def cat(refs_, src, rows=slice(None)):
        xs = [refs_[pi][0, rows, off:off + LANES] for pi, off in src]
        return xs[0] if len(xs) == 1 else jnp.concatenate(xs, axis=-1)

    @pl.when((fl & 1) != 0)
    def _init():
        m_sc[...] = jnp.full(m_sc.shape, NEG_INF, F32)
        acc_sc[...] = jnp.zeros(acc_sc.shape, F32)
        if has_bias:
            lane = lax.broadcasted_iota(jnp.int32, (tq, LANES), 1)
            for g in range(nc):
                head = pl.program_id(1) * nc + g
                cq_sc[g] = jnp.sum(jnp.where(lane == head, cum_ref[0], 0.0), axis=-1, keepdims=True)

    tk = kv_refs[0].shape[1]
    reps = tk // LANES
    ones = jnp.ones((tk, LANES), BF16)

    def step(masked):
        ks = [cat(kv_refs, k_src[g]) for g in range(nc)]
        vs = [jnp.concatenate([kv_refs[pi][0, :, off:off + LANES], ones], axis=1) for pi, off in v_src]
        for r in range(tq // rc):
            rows = slice(r * rc, (r + 1) * rc)
            add = mask_ref[mk - 1, rows, :] if masked else None
            if has_sel:
                hidden = (jnp.dot(sel_ref[0, 0, rows, :], exp_ref[0], preferred_element_type=F32) - 1.0) * (-NEG_INF)
                add = hidden if add is None else add + hidden
            for g in range(nc):
                q = cat(q_refs, q_src[g], rows)
                s = lax.dot_general(q, ks[g], (((1,), (1,)), ((), ())), preferred_element_type=F32)
                if has_bias:
                    s = s + (cq_sc[g, rows, :] - cumt_ref[0, g])
                if add is not None:
                    s = s + add
                m_prev = m_sc[g, rows, :]
                m_new = jnp.maximum(m_prev, jnp.max(s, axis=-1, keepdims=True))
                p = jnp.exp2(s - (jnp.concatenate([m_new] * reps, axis=1) if reps > 1 else m_new))
                alpha = jnp.exp2(m_prev - m_new)
                acc_sc[g, rows, :] = (jnp.concatenate([alpha, alpha], axis=1) * acc_sc[g, rows, :]
                                      + jnp.dot(p.astype(BF16), vs[g], preferred_element_type=F32))
                m_sc[g, rows, :] = m_new

    @pl.when(mk != 0)
    def _():
        step(True)

    @pl.when(mk == 0)
    def _():
        step(False)

    @pl.when((fl & 2) != 0)
    def _finish():
        for g in range(nc):
            m, acc, l = m_sc[g], acc_sc[g, :, :LANES], acc_sc[g, :, LANES:]
            if has_sink:
                sk = sink_ref[0, g:g + 1, 0:1] * LOG2E
                m_f = jnp.maximum(m, sk)
                w = jnp.exp2(m - m_f)
                l = l * w + jnp.exp2(sk - m_f)
                acc = acc * w
            o_ref[0, :, g * LANES:(g + 1) * LANES] = (acc / l).astype(o_ref.dtype)


def _flash(q_parts, kv_parts, q_src, k_src, v_src, *, name, B, S, n_steps, tq, tk, window=None, sinks=None,
           cum=None, cumt=None, sel=None, out_dtype=BF16, rc=FLASH_ROW_CHUNK):
    nc = len(q_src)
    tq, tk = _tile(S, tq), _tile(S, tk)
    tabs, masks = _pair_tables(S, tq, tk, window)
    npairs = int(tabs[0].shape[0])
    args, in_specs = [], []
    for arr, width, cf in q_parts:
        args.append(arr)
        in_specs.append(pl.BlockSpec((1, tq, width), lambda b, h, s, qt, kt, ft, mt, cf=cf: (b, qt[s], cf(h))))
    for arr, width, cf in kv_parts:
        args.append(arr)
        in_specs.append(pl.BlockSpec((1, tk, width), lambda b, h, s, qt, kt, ft, mt, cf=cf: (b, kt[s], cf(h))))
    args.append(masks)
    in_specs.append(pl.BlockSpec(masks.shape, lambda b, h, s, qt, kt, ft, mt: (0, 0, 0)))
    if sinks is not None:
        args.append(jnp.broadcast_to(sinks.astype(F32).reshape(n_steps, nc, 1), (n_steps, nc, LANES)))
        in_specs.append(pl.BlockSpec((1, nc, LANES), lambda b, h, s, qt, kt, ft, mt: (h, 0, 0)))
    if cum is not None:
        args += [cum, cumt]
        in_specs += [pl.BlockSpec((1, tq, LANES), lambda b, h, s, qt, kt, ft, mt: (b, qt[s], 0)),
                     pl.BlockSpec((1, nc, 1, tk), lambda b, h, s, qt, kt, ft, mt: (b, h, 0, kt[s]))]
    if sel is not None:
        per = tk // D_SLC_LEN
        e = np.zeros((S // tk, LANES, tk), np.float32)
        for ki in range(S // tk):
            e[ki, ki * per + np.arange(tk) // D_SLC_LEN, np.arange(tk)] = 1.0
        args += [sel, jnp.asarray(e, BF16)]
        in_specs += [pl.BlockSpec((1, 1, tq, LANES), lambda b, h, s, qt, kt, ft, mt: (b, h, qt[s], 0)),
                     pl.BlockSpec((1, LANES, tk), lambda b, h, s, qt, kt, ft, mt: (kt[s], 0, 0))]
    scratch = [pltpu.VMEM((nc, tq, LANES), F32), pltpu.VMEM((nc, tq, 2 * LANES), F32)]
    if cum is not None:
        scratch.append(pltpu.VMEM((nc, tq, 1), F32))
    body = functools.partial(_flash_body, nq=len(q_parts), nkv=len(kv_parts), q_src=q_src, k_src=k_src,
                             v_src=v_src, tq=tq, rc=_tile(tq, rc), has_sink=sinks is not None,
                             has_bias=cum is not None, has_sel=sel is not None)
    return pl.pallas_call(
        body, out_shape=jax.ShapeDtypeStruct((B, S, n_steps * nc * LANES), out_dtype),
        grid_spec=pltpu.PrefetchScalarGridSpec(
            num_scalar_prefetch=4, grid=(B, n_steps, npairs), in_specs=in_specs,
            out_specs=pl.BlockSpec((1, tq, nc * LANES), lambda b, h, s, qt, kt, ft, mt: (b, qt[s], h)),
            scratch_shapes=scratch),
        compiler_params=_params("arbitrary", "arbitrary", "arbitrary"), name=name,
    )(*tabs, *args)


def _gqa_flash(q, k, v, *, name, B, S, Hk, G, **kw):
    at = lambda first: (lambda h: first + h)
    return _flash([(q[0], G * LANES, at(q[1]))], [(k[0], LANES, at(k[1])), (v[0], LANES, at(v[1]))],
                  [[(0, g * LANES)] for g in range(G)], [[(0, 0)]] * G, [(1, 0)] * G,
                  name=name, B=B, S=S, n_steps=Hk, **kw)


def _cum_body(y_ref, b_ref, tri_ref, o_ref, carry):
    @pl.when(pl.program_id(1) == 0)
    def _():
        carry[...] = jnp.zeros(carry.shape, F32)
    x = y_ref[0] + b_ref[...]
    logf = jnp.minimum(x, 0.0) - jnp.log1p(jnp.exp(-jnp.abs(x)))
    cum = jnp.dot(tri_ref[...], logf, preferred_element_type=F32, precision=lax.Precision.HIGHEST) + carry[...]
    o_ref[0] = cum * LOG2E
    carry[...] = cum[-1:, :]


def _forget_cum(y3, col_block, bias_row, *, ts=512):
    B, S, _ = y3.shape
    ts = _tile(S, ts)
    tri = jnp.asarray(np.tril(np.ones((ts, ts), np.float32)))
    return pl.pallas_call(
        _cum_body, out_shape=jax.ShapeDtypeStruct((B, S, LANES), F32), grid=(B, S // ts),
        in_specs=[pl.BlockSpec((1, ts, LANES), lambda b, s: (b, s, col_block)),
                  pl.BlockSpec((1, LANES), lambda b, s: (0, 0)),
                  pl.BlockSpec((ts, ts), lambda b, s: (0, 0))],
        out_specs=pl.BlockSpec((1, ts, LANES), lambda b, s: (b, s, 0)),
        scratch_shapes=[pltpu.VMEM((1, LANES), F32)],
        compiler_params=_params("arbitrary", "arbitrary"), name="forget_cum",
    )(y3, bias_row, tri)


def _compress_body(*refs, rope, nc, width, col0):
    if rope:
        z_ref, pe_ref, w1_ref, w2_ref, c_ref, s_ref, o_ref = refs
    else:
        z_ref, pe_ref, w1_ref, w2_ref, o_ref = refs
    half = D_CMP_LEN // 2
    for hk in range(D_KV_HEADS):
        u = jnp.zeros((nc, w1_ref.shape[1]), F32)
        v = jnp.zeros((nc, w1_ref.shape[1]), F32)
        for l in range(half):
            first = l * width + col0 + hk * HEAD_DIM
            z = z_ref[0, :, first:first + HEAD_DIM]
            zu = (z + pe_ref[l:l + 1, :]).astype(BF16)
            zv = (z + pe_ref[half + l:half + l + 1, :]).astype(BF16)
            u = u + jnp.dot(zu, w1_ref[l * HEAD_DIM:(l + 1) * HEAD_DIM, :], preferred_element_type=F32)
            v = v + jnp.dot(zv, w1_ref[(half + l) * HEAD_DIM:(half + l + 1) * HEAD_DIM, :],
                            preferred_element_type=F32)
        pre = u + pltpu.roll(v, nc - 1, 0)
        hid = jax.nn.gelu(pre, approximate=True)
        out = jnp.dot(hid.astype(BF16), w2_ref[...], preferred_element_type=F32)
        if rope:
            out = _rot128(out, c_ref[0], s_ref[0])
        o_ref[0, hk] = out.astype(o_ref.dtype)


def _compress(z, col0, pe, w1, w2, rope_tabs=None):
    B, S, W = z.shape
    nc = S // D_CMP_STRIDE
    zc = z.reshape(B, nc, D_CMP_STRIDE * W)
    args = [zc, pe.astype(F32), w1.astype(BF16), w2.astype(BF16)]
    in_specs = [pl.BlockSpec((1, nc, D_CMP_STRIDE * W), lambda b: (b, 0, 0)),
                pl.BlockSpec(pe.shape, lambda b: (0, 0)),
                pl.BlockSpec(w1.shape, lambda b: (0, 0)),
                pl.BlockSpec(w2.shape, lambda b: (0, 0))]
    if rope_tabs is not None:
        args += list(rope_tabs)
        in_specs += [pl.BlockSpec((1, nc, LANES), lambda b: (b, 0, 0))] * 2
    return pl.pallas_call(
        functools.partial(_compress_body, rope=rope_tabs is not None, nc=nc, width=W, col0=col0),
        out_shape=jax.ShapeDtypeStruct((B, D_KV_HEADS, nc, HEAD_DIM), BF16), grid=(B,),
        in_specs=in_specs, out_specs=pl.BlockSpec((1, D_KV_HEADS, nc, HEAD_DIM), lambda b: (b, 0, 0, 0)),
        compiler_params=_params("arbitrary"), name="nsa_compress",
    )(*args)


def _cmp_attn_body(q_ref, k_ref, v_ref, ov_ref, o_ref, sel_ref, *, G, tq, nc, n_cmp, n_slc, topn):
    qi = pl.program_id(2)
    t = qi * tq + lax.broadcasted_iota(jnp.int32, (tq, nc), 0)
    c = lax.broadcasted_iota(jnp.int32, (tq, nc), 1)
    valid = (c * D_CMP_STRIDE + (D_CMP_LEN - 1) <= t) & (c < n_cmp)
    k = k_ref[0, 0]
    v = v_ref[0, 0]
    psum = jnp.zeros((tq, nc), F32)
    for g in range(G):
        q = q_ref[0, :, g * LANES:(g + 1) * LANES]
        s = lax.dot_general(q, k, (((1,), (1,)), ((), ())), preferred_element_type=F32)
        s = jnp.where(valid, s, NEG_INF)
        e = jnp.where(valid, jnp.exp2(s - jnp.max(s, axis=-1, keepdims=True)), 0.0)
        p = e / jnp.maximum(jnp.sum(e, axis=-1, keepdims=True), jnp.finfo(F32).tiny)
        o_ref[0, :, g * LANES:(g + 1) * LANES] = jnp.dot(
            p.astype(BF16), v, preferred_element_type=F32).astype(o_ref.dtype)
        psum = psum + p
    imp = jnp.dot(psum, ov_ref[...], preferred_element_type=F32, precision=lax.Precision.HIGHEST)
    lane = lax.broadcasted_iota(jnp.int32, (tq, LANES), 1)
    trow = qi * tq + lax.broadcasted_iota(jnp.int32, (tq, LANES), 0)
    cur = jnp.right_shift(trow, int(math.log2(D_SLC_LEN)))
    forced = (lane == 0) | (lane == cur) | (lane == cur - 1)
    imp = jnp.where(lane * D_SLC_LEN > trow, NEG_INF, imp + jnp.where(forced, FORCE_BONUS, 0.0))
    imp = jnp.where(lane >= n_slc, TAKEN, imp)
    chosen = jnp.zeros((tq, LANES), F32)
    lane_f = lane.astype(F32)
    for _ in range(topn):
        mx = jnp.max(imp, axis=-1, keepdims=True)
        idx = jnp.min(jnp.where(imp == mx, lane_f, float(LANES)), axis=-1, keepdims=True)
        hit = lane_f == idx
        chosen = jnp.where(hit, 1.0, chosen)
        imp = jnp.where(hit, TAKEN, imp)
    sel_ref[0, 0] = chosen.astype(sel_ref.dtype)


def _cmp_attn(q, q_first, k_cmp, v_cmp, *, tq=256):
    B, S, _ = q.shape
    Hk, G = D_KV_HEADS, D_HEADS // D_KV_HEADS
    nc = S // D_CMP_STRIDE
    n_cmp = (S - D_CMP_LEN) // D_CMP_STRIDE + 1
    n_slc = S // D_SLC_LEN
    tq = _tile(S, tq)
    c0 = np.arange(nc) * D_CMP_STRIDE
    s0 = np.arange(LANES) * D_SLC_LEN
    ov = ((c0[:, None] < (s0 + D_SLC_LEN)[None, :]) & ((c0 + D_CMP_LEN)[:, None] > s0[None, :])
          & (np.arange(nc) < n_cmp)[:, None] & (np.arange(LANES) < n_slc)[None, :]).astype(np.float32)
    body = functools.partial(_cmp_attn_body, G=G, tq=tq, nc=nc, n_cmp=n_cmp, n_slc=n_slc,
                             topn=min(D_SLC_TOPN, n_slc))
    return pl.pallas_call(
        body,
        out_shape=[jax.ShapeDtypeStruct((B, S, Hk * G * LANES), F32),
                   jax.ShapeDtypeStruct((B, Hk, S, LANES), BF16)],
        grid=(B, Hk, S // tq),
        in_specs=[pl.BlockSpec((1, tq, G * LANES), lambda b, h, i: (b, i, q_first + h)),
                  pl.BlockSpec((1, 1, nc, LANES), lambda b, h, i: (b, h, 0, 0)),
                  pl.BlockSpec((1, 1, nc, LANES), lambda b, h, i: (b, h, 0, 0)),
                  pl.BlockSpec((nc, LANES), lambda b, h, i: (0, 0))],
        out_specs=[pl.BlockSpec((1, tq, G * LANES), lambda b, h, i: (b, i, h)),
                   pl.BlockSpec((1, 1, tq, LANES), lambda b, h, i: (b, h, i, 0))],
        compiler_params=_params("arbitrary", "arbitrary", "arbitrary"), name="nsa_cmp_attn",
    )(q, k_cmp, v_cmp, jnp.asarray(ov))


def _gate_body(y_ref, a_ref, b_ref, c_ref, o_ref, *, lane0):
    g = jax.nn.sigmoid(y_ref[...])
    for h in range(D_HEADS):
        cols = slice(h * LANES, (h + 1) * LANES)
        ga = g[:, lane0 + h:lane0 + h + 1]
        gb = g[:, lane0 + D_HEADS + h:lane0 + D_HEADS + h + 1]
        gc = g[:, lane0 + 2 * D_HEADS + h:lane0 + 2 * D_HEADS + h + 1]
        o_ref[:, cols] = (ga * a_ref[:, cols] + gb * b_ref[:, cols] + gc * c_ref[:, cols]).astype(o_ref.dtype)


def _nsa_gate(y, col_block, lane0, o_cmp, o_slc, o_win, *, tm=512):
    M = y.shape[0]
    W = o_cmp.shape[1]
    tm = _tile(M, tm)
    row = lambda i: (i, 0)
    return pl.pallas_call(
        functools.partial(_gate_body, lane0=lane0), out_shape=jax.ShapeDtypeStruct((M, W), BF16),
        grid=(M // tm,),
        in_specs=[pl.BlockSpec((tm, LANES), lambda i: (i, col_block))] + [pl.BlockSpec((tm, W), row)] * 3,
        out_specs=pl.BlockSpec((tm, W), row), compiler_params=_params("arbitrary"), name="nsa_gate",
    )(y, o_cmp, o_slc, o_win)


def _route_body(lg_ref, b_ref, tri_ref, e_ref, w_ref, pos_ref, cnt_ref, carry, *, tm):
    @pl.when(pl.program_id(0) == 0)
    def _():
        carry[...] = jnp.zeros(carry.shape, F32)
    lane = lax.broadcasted_iota(jnp.int32, (tm, LANES), 1)
    logits = lg_ref[...] + b_ref[...]
    gl = jnp.where(lane < N_GROUPS, logits, -jnp.inf)
    gmax = jnp.max(gl, axis=-1, keepdims=True)
    g_val = 1.0 / jnp.sum(jnp.exp(gl - gmax), axis=-1, keepdims=True)
    g_idx = jnp.min(jnp.where(gl == gmax, lane, LANES), axis=-1, keepdims=True)
    lo = N_GROUPS + EXPERTS_PER_GROUP * g_idx
    el = jnp.where((lane >= lo) & (lane < lo + EXPERTS_PER_GROUP), logits, -jnp.inf)
    e1 = jnp.max(el, axis=-1, keepdims=True)
    i1 = jnp.min(jnp.where(el == e1, lane, LANES), axis=-1, keepdims=True)
    el2 = jnp.where(lane == i1, -jnp.inf, el)
    e2 = jnp.max(el2, axis=-1, keepdims=True)
    i2 = jnp.min(jnp.where(el2 == e2, lane, LANES), axis=-1, keepdims=True)
    r = jnp.exp(e2 - e1)
    w1 = g_val / (1.0 + r)
    w2 = w1 * r
    x1, x2 = i1 - N_GROUPS, i2 - N_GROUPS
    e_ref[...] = jnp.where(lane == 0, x1, jnp.where(lane == 1, x2, 0))
    w_ref[...] = jnp.where(lane == 0, w1, jnp.where(lane == 1, w2, 0.0))
    hot1 = lane == x1
    hot2 = lane == x2
    both = jnp.where(hot1 | hot2, 1.0, 0.0)
    before = jnp.dot(tri_ref[...], both.astype(BF16), preferred_element_type=F32) + carry[...]
    p1 = jnp.sum(jnp.where(hot1, before, 0.0), axis=-1, keepdims=True)
    p2 = jnp.sum(jnp.where(hot2, before, 0.0), axis=-1, keepdims=True)
    pos_ref[...] = jnp.where(lane == 0, p1, jnp.where(lane == 1, p2, 0.0)).astype(jnp.int32)
    carry[...] = carry[...] + jnp.sum(both, axis=0, keepdims=True)
    cnt_ref[...] = carry[...].astype(jnp.int32)


def _route(logits, bias_row, *, tm=512):
    T = logits.shape[0]
    tm = _tile(T, tm)
    tri = jnp.asarray(np.tril(np.ones((tm, tm), np.float32), -1), BF16)
    row = lambda i: (i, 0)
    fixed = lambda i: (0, 0)
    return pl.pallas_call(
        functools.partial(_route_body, tm=tm),
        out_shape=[jax.ShapeDtypeStruct((T, LANES), jnp.int32), jax.ShapeDtypeStruct((T, LANES), F32),
                   jax.ShapeDtypeStruct((T, LANES), jnp.int32), jax.ShapeDtypeStruct((1, LANES), jnp.int32)],
        grid=(T // tm,),
        in_specs=[pl.BlockSpec((tm, LANES), row), pl.BlockSpec((1, LANES), fixed), pl.BlockSpec((tm, tm), fixed)],
        out_specs=[pl.BlockSpec((tm, LANES), row)] * 3 + [pl.BlockSpec((1, LANES), fixed)],
        scratch_shapes=[pltpu.VMEM((1, LANES), F32)],
        compiler_params=_params("arbitrary"), name="moe_route",
    )(logits, bias_row, tri)


def _dispatch_body(d_ref, ends_ref, pad_ref, nu_ref, x_ref, out_ref, zero_sc, sem, zsem, *, tm, nblk):
    base = pl.program_id(0) * tm

    @pl.when(pl.program_id(0) == 0)
    def _():
        zero_sc[...] = jnp.zeros(zero_sc.shape, zero_sc.dtype)

        def zero_block(first_row):
            return pltpu.make_async_copy(zero_sc, out_ref.at[pl.ds(pl.multiple_of(first_row, MOE_ROWS), MOE_ROWS)],
                                         zsem)

        for wait in (False, True):
            for e in range(N_EXPERTS):
                for live, first_row in ((pad_ref[e] > 0, ends_ref[e] - MOE_ROWS),
                                        (nu_ref[0] + e < nblk, (nu_ref[0] + e) * MOE_ROWS)):
                    @pl.when(live)
                    def _():
                        zero_block(first_row).wait() if wait else zero_block(first_row).start()

    def issue(r, c):
        for k in range(TOP_K):
            row = d_ref[(base + r) * TOP_K + k]
            pltpu.make_async_copy(x_ref.at[pl.ds(r, 1)], out_ref.at[pl.ds(row, 1)], sem).start(priority=k)
        return c

    lax.fori_loop(0, tm, issue, 0, unroll=DMA_UNROLL)
    for _ in range(TOP_K):
        pltpu.make_async_copy(x_ref, out_ref.at[pl.ds(0, tm)], sem).wait()


def _dispatch(xn, dest, ends, padded, n_used, n_rows, *, tm=512):
    T, D = xn.shape
    tm = _tile(T, tm)
    return pl.pallas_call(
        functools.partial(_dispatch_body, tm=tm, nblk=n_rows // MOE_ROWS),
        out_shape=jax.ShapeDtypeStruct((n_rows, D), xn.dtype),
        grid_spec=pltpu.PrefetchScalarGridSpec(
            num_scalar_prefetch=4, grid=(T // tm,),
            in_specs=[pl.BlockSpec((tm, D), lambda i, *_: (i, 0))],
            out_specs=pl.BlockSpec(memory_space=pl.ANY),
            scratch_shapes=[pltpu.VMEM((MOE_ROWS, D), xn.dtype), pltpu.SemaphoreType.DMA(()),
                            pltpu.SemaphoreType.DMA(())]),
        compiler_params=pltpu.CompilerParams(dimension_semantics=("arbitrary",), has_side_effects=True,
                                             vmem_limit_bytes=VMEM_LIMIT_BYTES, disable_bounds_checks=True),
        name="moe_dispatch",
    )(dest, ends, padded, n_used, xn)


def _expert_body(be_ref, nu_ref, x_ref, wg_ref, wu_ref, wd_ref, o_ref, wg_sc, wu_sc, wd_sc):
    i = pl.program_id(0)
    fresh = (i == 0) | (be_ref[i] != be_ref[jnp.maximum(i - 1, 0)])

    @pl.when(fresh)
    def _():
        wg_sc[...] = wg_ref[0, 0].astype(BF16)
        wu_sc[...] = wu_ref[0, 0].astype(BF16)
        wd_sc[...] = wd_ref[0, 0].astype(BF16)

    @pl.when(i < nu_ref[0])
    def _():
        x = jnp.concatenate(_unpack_bf16_pairs(x_ref[...]), axis=1).astype(BF16)
        gate = jnp.dot(x, wg_sc[...], preferred_element_type=F32)
        up = jnp.dot(x, wu_sc[...], preferred_element_type=F32)
        hid = (gate * jax.nn.sigmoid(gate) * up).astype(BF16)
        o_ref[...] = _pack_bf16_pairs(jnp.dot(hid, wd_sc[...], preferred_element_type=F32))

    @pl.when(i >= nu_ref[0])
    def _():
        o_ref[...] = jnp.zeros(o_ref.shape, o_ref.dtype)


def _experts(xb, blk_e, n_used, layer, w_gate, w_up, w_down):
    D, Hd = w_gate.shape[2], w_gate.shape[3]
    nblk = xb.shape[0] // MOE_ROWS
    return pl.pallas_call(
        _expert_body, out_shape=jax.ShapeDtypeStruct(xb.shape, xb.dtype),
        grid_spec=pltpu.PrefetchScalarGridSpec(
            num_scalar_prefetch=2, grid=(nblk,),
            in_specs=[pl.BlockSpec((MOE_ROWS, D // 2), lambda i, be, nu: (jnp.minimum(i, nu[0] - 1), 0)),
                      pl.BlockSpec((1, 1, D, Hd), lambda i, be, nu: (layer, be[i], 0, 0)),
                      pl.BlockSpec((1, 1, D, Hd), lambda i, be, nu: (layer, be[i], 0, 0)),
                      pl.BlockSpec((1, 1, Hd, D), lambda i, be, nu: (layer, be[i], 0, 0))],
            out_specs=pl.BlockSpec((MOE_ROWS, D // 2), lambda i, be, nu: (i, 0)),
            scratch_shapes=[pltpu.VMEM((D, Hd), BF16), pltpu.VMEM((D, Hd), BF16), pltpu.VMEM((Hd, D), BF16)]),
        compiler_params=_params("arbitrary"), name="moe_experts",
    )(blk_e, n_used, xb, w_gate, w_up, w_down)


def _collect_body(d_ref, h_ref, w_ref, yb_ref, o_ref, buf_a, buf_b, sem, *, tm, nsteps):
    i = pl.program_id(0)
    slot = lax.rem(i, 2)

    def fetch(step, slot_):
        base = step * tm

        def issue(r, c):
            for k, buf in enumerate((buf_a, buf_b)):
                row = d_ref[(base + r) * TOP_K + k]
                pltpu.make_async_copy(yb_ref.at[pl.ds(row, 1)], buf.at[slot_, pl.ds(r, 1)],
                                      sem.at[slot_]).start(priority=k)
            return c

        lax.fori_loop(0, tm, issue, 0, unroll=DMA_UNROLL)

    @pl.when(i == 0)
    def _():
        fetch(0, 0)

    @pl.when(i + 1 < nsteps)
    def _():
        fetch(i + 1, 1 - slot)

    for buf in (buf_a, buf_b):
        pltpu.make_async_copy(yb_ref.at[pl.ds(0, tm)], buf.at[slot], sem.at[slot]).wait()
    w = w_ref[...]
    half = h_ref.shape[1] // 2
    for part, ya, yb in zip((slice(0, half), slice(half, None)), _unpack_bf16_pairs(buf_a[slot]),
                            _unpack_bf16_pairs(buf_b[slot])):
        o_ref[:, part] = h_ref[:, part] + w[:, 0:1] * ya + w[:, 1:2] * yb


def _collect(h, yb, dest, wts, *, tm=256):
    T, D = h.shape
    tm = _tile(T, tm)
    nsteps = T // tm
    return pl.pallas_call(
        functools.partial(_collect_body, tm=tm, nsteps=nsteps),
        out_shape=jax.ShapeDtypeStruct((T, D), F32),
        grid_spec=pltpu.PrefetchScalarGridSpec(
            num_scalar_prefetch=1, grid=(nsteps,),
            in_specs=[pl.BlockSpec((tm, D), lambda i, d: (i, 0)), pl.BlockSpec((tm, LANES), lambda i, d: (i, 0)),
                      pl.BlockSpec(memory_space=pl.ANY)],
            out_specs=pl.BlockSpec((tm, D), lambda i, d: (i, 0)),
            scratch_shapes=[pltpu.VMEM((2, tm, D // 2), yb.dtype), pltpu.VMEM((2, tm, D // 2), yb.dtype),
                            pltpu.SemaphoreType.DMA((2,))]),
        compiler_params=pltpu.CompilerParams(dimension_semantics=("arbitrary",), disable_bounds_checks=True,
                                             vmem_limit_bytes=VMEM_LIMIT_BYTES),
        name="moe_collect",
    )(dest, h, wts, yb)


def _moe(h, norm_g, w_group, b_group, w_expert, b_expert, layer, w_gate, w_up, w_down):
    T, D = h.shape
    pad = LANES - N_GROUPS - N_EXPERTS
    w_r = jnp.concatenate([w_group, w_expert, jnp.zeros((D, pad), F32)], axis=1).astype(BF16)
    b_r = jnp.concatenate([b_group, b_expert, jnp.zeros((pad,), F32)]).astype(F32).reshape(1, LANES)
    logits, xn = _mm([(h, 0, D)], [w_r], gain=norm_g, emit_xn=True, tm=512, tn=LANES, name="moe_router")
    eid, wts, pos, cnt = _route(logits, b_r)
    counts = cnt[0, :N_EXPERTS]
    padded = (counts + MOE_ROWS - 1) // MOE_ROWS * MOE_ROWS
    ends = jnp.cumsum(padded)
    offs = ends - padded
    dest = (offs[eid[:, :TOP_K]] + pos[:, :TOP_K]).reshape(T * TOP_K).astype(jnp.int32)
    P = T * TOP_K + N_EXPERTS * MOE_ROWS
    nblk = P // MOE_ROWS
    starts = jnp.arange(nblk, dtype=jnp.int32) * MOE_ROWS
    blk_e = jnp.minimum(jnp.sum(ends[None, :] <= starts[:, None], axis=1), N_EXPERTS - 1).astype(jnp.int32)
    n_used = (ends[-1:] // MOE_ROWS).astype(jnp.int32)
    xb = _dispatch(xn, dest, ends.astype(jnp.int32), padded.astype(jnp.int32), n_used, P)
    yb = _experts(xb, blk_e, n_used, layer, w_gate, w_up, w_down)
    return _collect(h, yb, dest, wts)


def _pad_cols(w, n):
    return jnp.pad(w, ((0, 0), (0, n - w.shape[1])))


def _layer_even(h, B, S, tabs, norm_g, w_in, sinks, q_lat_norm, kv_lat_norm, w_uq, w_ukv, w_o):
    T, D = h.shape
    n_in, tn = 2560, 512
    sa = HEAD_DIM ** -0.5 * LOG2E
    prog = ([("rot128", sa)] * 8 + [("rot128", 1.0)] * 2 + [("copy", 1.0)] * 8 + [("rot64", 1.0), ("copy", 1.0)])
    y, yf = _mm([(h, 0, D)], [_pad_cols(w_in, n_in).astype(BF16)], gain=norm_g, tn=tn, out_dtype=BF16,
                col_prog=prog, tables=tabs, f32_from=12 * LANES // tn, name="ab_in")
    wq = w_uq.reshape(B_Q_LORA, B_HEADS, B_NOPE + B_ROPE)
    wq_n = wq[:, :, :B_NOPE].reshape(B_Q_LORA, B_HEADS * B_NOPE)
    wq_r = jnp.pad(wq[:, :, B_NOPE:], ((0, 0), (0, 0), (0, LANES - B_ROPE))).reshape(B_Q_LORA, B_HEADS * LANES)
    sb = (B_NOPE + B_ROPE) ** -0.5 * LOG2E
    qq = _mm([(yf, 0, B_Q_LORA)], [jnp.concatenate([wq_n, wq_r], 1).astype(BF16)], gain=q_lat_norm, tn=512,
             out_dtype=BF16, col_prog=[("copy", sb)] * 8 + [("rot64", sb)] * 8, tables=tabs, name="mla_uq")
    kv = _mm([(yf, B_Q_LORA // B_KV_LORA, B_KV_LORA)], [w_ukv.astype(BF16)], gain=kv_lat_norm, out_dtype=BF16,
             tn=512, name="mla_ukv")
    r3 = lambda a: a.reshape(B, S, a.shape[-1])
    y3 = r3(y)
    o_a = _gqa_flash((y3, 0), (y3, 8), (y3, 10), name="swa_attn", B=B, S=S, Hk=A_KV_HEADS,
                     G=A_HEADS // A_KV_HEADS, window=A_WINDOW, sinks=sinks, tq=256, tk=256)
    hp = 2
    head = lambda h: h
    o_b = _flash([(r3(qq), hp * LANES, head), (r3(qq), hp * LANES, lambda h: B_HEADS // hp + h)],
                 [(r3(kv), hp * 2 * LANES, head), (y3, LANES, lambda h: 18)],
                 [[(0, g * LANES), (1, g * LANES)] for g in range(hp)],
                 [[(0, 2 * g * LANES), (1, 0)] for g in range(hp)], [(0, (2 * g + 1) * LANES) for g in range(hp)],
                 name="mla_attn", B=B, S=S, n_steps=B_HEADS // hp, tq=1024, tk=1024, rc=256)
    wo = w_o.astype(BF16)
    na = A_HEADS * HEAD_DIM
    return _mm([(o_a.reshape(T, -1), 0, na), (o_b.reshape(T, -1), 0, B_HEADS * B_V)], [wo[:na], wo[na:]],
               resid=h, tm=512, tn=D, name="ab_out")


def _layer_odd(h, B, S, tabs, cmp_tabs, norm_g, w_in, forget_bias, pe_k, w1_k, w2_k, pe_v, w1_v, w2_v, w_o):
    T, D = h.shape
    hc, kvw = C_HEADS * HEAD_DIM, D_KV_HEADS * HEAD_DIM
    o_cf = 3 * hc
    o_dq = o_cf + C_HEADS
    o_kc = o_dq + D_HEADS * HEAD_DIM
    o_ks = o_kc + 2 * kvw
    o_dg = o_ks + 4 * kvw
    w_r = jnp.concatenate([w_in[:, :o_cf], w_in[:, o_dq:o_kc], w_in[:, o_ks:o_dg], w_in[:, o_kc:o_ks],
                           w_in[:, o_cf:o_dq], w_in[:, o_dg:]], axis=1)
    n_in, tn = 5760, 640
    sc = HEAD_DIM ** -0.5 * LOG2E
    prog = ([("copy", sc)] * 8 + [("copy", 1.0)] * 16 + [("rot128", sc)] * 8
            + [("rot128", 1.0)] * 2 + [("copy", 1.0)] * 2 + [("rot128", 1.0)] * 2 + [("copy", 1.0)] * 7)
    y, yf = _mm([(h, 0, D)], [_pad_cols(w_r, n_in).astype(BF16)], gain=norm_g, tn=tn, out_dtype=BF16,
                col_prog=prog, tables=tabs, f32_from=n_in // tn - 1, name="cd_in")
    misc = tn // LANES - 1
    r3 = lambda a: a.reshape(B, S, a.shape[-1])
    y3, yf3 = r3(y), r3(yf)
    fb = jnp.pad(forget_bias.astype(F32), (0, LANES - C_HEADS)).reshape(1, LANES)
    cum = _forget_cum(yf3, misc, fb)
    cumt = jnp.swapaxes(cum[:, :, :C_HEADS], 1, 2).reshape(B, C_HEADS, 1, S)
    hp = 2
    at = lambda first: (lambda h: first + h)
    o_c = _flash([(y3, hp * LANES, at(0))], [(y3, hp * LANES, at(C_HEADS // hp)), (y3, hp * LANES, at(C_HEADS))],
                 [[(0, g * LANES)] for g in range(hp)], [[(0, g * LANES)] for g in range(hp)],
                 [(1, g * LANES) for g in range(hp)], name="fox_attn", B=B, S=S, n_steps=C_HEADS // hp,
                 tq=1024, tk=1024, cum=cum, cumt=cumt)
    G = D_HEADS // D_KV_HEADS
    k_cmp = _compress(yf3, 0, pe_k, w1_k, w2_k, rope_tabs=cmp_tabs)
    v_cmp = _compress(yf3, kvw, pe_v, w1_v, w2_v)
    q_d = (y3, 24 // G)
    o_cmp, sel = _cmp_attn(y3, q_d[1], k_cmp, v_cmp)
    o_slc = _gqa_flash(q_d, (y3, 32), (y3, 34), name="nsa_slc_attn", B=B, S=S, Hk=D_KV_HEADS, G=G, sel=sel,
                       out_dtype=F32, tq=1024, tk=1024, rc=256)
    o_win = _gqa_flash(q_d, (y3, 36), (y3, 38), name="nsa_win_attn", B=B, S=S, Hk=D_KV_HEADS, G=G,
                       window=D_WINDOW, out_dtype=F32, tq=512, tk=512)
    o_d = _nsa_gate(yf, misc, C_HEADS, o_cmp.reshape(T, -1), o_slc.reshape(T, -1), o_win.reshape(T, -1))
    wo = w_o.astype(BF16)
    return _mm([(o_c.reshape(T, -1), 0, hc), (o_d, 0, D_HEADS * HEAD_DIM)], [wo[:hc], wo[hc:]],
               resid=h, tm=512, tn=D, name="cd_out")


def kernel(x, p, positions, ab_w_in, ab_sinks, ab_q_lat_norm, ab_kv_lat_norm, ab_w_uq, ab_w_ukv, ab_w_o,
           cd_w_in, cd_forget_bias, cd_cmp_pe_k, cd_cmp_w1_k, cd_cmp_w2_k, cd_cmp_pe_v, cd_cmp_w1_v,
           cd_cmp_w2_v, cd_w_o, mixer_norm, moe_norm, router_group_w, router_group_b, router_expert_w,
           router_expert_b, expert_w_gate, expert_w_up, expert_w_down, ple_proj, ple_gate_norm, ple_gate_w,
           final_norm):
    B, S, D = x.shape
    T = B * S
    depth = p.shape[0]
    tabs = _rope_tables(positions)
    nc = S // D_CMP_STRIDE
    end = np.minimum(np.arange(nc) * D_CMP_STRIDE + D_CMP_LEN - 1, S - 1)
    cmp_tabs = [t.reshape(B, nc, LANES) for t in _rope_tables(positions[:, end])[:2]]
    h = x.reshape(T, D)
    for i in range(depth):
        j = i // 2
        if i % 2 == 0:
            h = _layer_even(h, B, S, tabs, mixer_norm[i], ab_w_in[j], ab_sinks[j], ab_q_lat_norm[j],
                            ab_kv_lat_norm[j], ab_w_uq[j], ab_w_ukv[j], ab_w_o[j])
        else:
            h = _layer_odd(h, B, S, tabs, cmp_tabs, mixer_norm[i], cd_w_in[j], cd_forget_bias[j],
                           cd_cmp_pe_k[j], cd_cmp_w1_k[j], cd_cmp_w2_k[j], cd_cmp_pe_v[j], cd_cmp_w1_v[j],
                           cd_cmp_w2_v[j], cd_w_o[j])
        h = _moe(h, moe_norm[i], router_group_w[i], router_group_b[i], router_expert_w[i], router_expert_b[i],
                 i, expert_w_gate, expert_w_up, expert_w_down)
        h = _mm([(h, 0, D)], [ple_gate_w[i].astype(BF16)], gain=ple_gate_norm[i], tm=512, tn=D,
                ple=(h, p[i].reshape(T, -1), ple_proj[i].astype(BF16)),
                out_gain=final_norm if i == depth - 1 else None, name="ple")
    return h.reshape(B, S, D)
```

```python
import functools
import math

import numpy as np
import jax
import jax.numpy as jnp
from jax import lax
from jax.experimental import pallas as pl
from jax.experimental.pallas import tpu as pltpu

F32 = jnp.float32
BF16 = jnp.bfloat16

HEAD_DIM = 128
ROPE_THETA = 10000.0
NORM_EPS = 1e-6
NEG_INF = -1e30
TAKEN = -3e38
A_HEADS, A_KV_HEADS, A_WINDOW = 8, 2, 128
B_HEADS, B_Q_LORA, B_KV_LORA, B_NOPE, B_ROPE, B_V = 8, 512, 256, 128, 64, 128
C_HEADS = 8
D_HEADS, D_KV_HEADS = 8, 2
D_CMP_LEN, D_CMP_STRIDE, D_SLC_LEN, D_SLC_TOPN, D_WINDOW = 32, 16, 64, 8, 512
FORCE_BONUS = 1e4
N_GROUPS, EXPERTS_PER_GROUP, TOP_K = 4, 8, 2
N_EXPERTS = N_GROUPS * EXPERTS_PER_GROUP

LANES = 128
VMEM_LIMIT_BYTES = 56 * 1024 * 1024
MOE_ROWS = 256
DMA_UNROLL = 8
FLASH_ROW_CHUNK = 128
LOG2E = math.log2(math.e)


def _params(*sem):
    return pltpu.CompilerParams(dimension_semantics=sem, vmem_limit_bytes=VMEM_LIMIT_BYTES)


def _tile(n, pref):
    t = min(n, pref)
    while n % t:
        t -= 1
    return t


def _pack_bf16_pairs(y):
    n = y.shape[1] // 2
    bits = lambda a: lax.bitcast_convert_type(a.astype(BF16).astype(F32), jnp.uint32)
    return (bits(y[:, :n]) >> 16) | (bits(y[:, n:]) & jnp.uint32(0xFFFF0000))


def _unpack_bf16_pairs(w):
    return (lax.bitcast_convert_type(w << 16, F32), lax.bitcast_convert_type(w & jnp.uint32(0xFFFF0000), F32))


def _rot128(x, c, s):
    return x * c + pltpu.roll(x, 64, 1) * s


def _rot64(x, c, sa, sb):
    return x * c + pltpu.roll(x, 96, 1) * sa + pltpu.roll(x, 32, 1) * sb


def _mm_body(*refs, nx, has_gain, emit_xn, mode, resid_is_x, has_out_gain, col_prog, f32_from):
    it = iter(refs)
    x_refs = [next(it) for _ in range(nx)]
    g_ref = next(it) if has_gain else None
    w_refs = [next(it) for _ in range(nx)]
    r_ref = next(it) if mode in ("resid", "ple") and not resid_is_x else None
    p_ref = next(it) if mode == "ple" else None
    wp_ref = next(it) if mode == "ple" else None
    og_ref = next(it) if has_out_gain else None
    tab_refs = [next(it) for _ in range(5)] if col_prog is not None else None
    o_ref = next(it)
    xo_ref = next(it) if emit_xn else None
    of_ref = next(it) if f32_from is not None else None
    xn_ref = next(it) if has_gain else None

    if has_gain:
        @pl.when(pl.program_id(1) == 0)
        def _():
            x = x_refs[0][...].astype(F32)
            y = x * lax.rsqrt(jnp.mean(x * x, axis=-1, keepdims=True) + NORM_EPS) * g_ref[...]
            xn_ref[...] = y.astype(BF16)
            if emit_xn:
                xo_ref[...] = _pack_bf16_pairs(y)
        lhs = [xn_ref[...]]
    else:
        lhs = [x_ref[...].astype(BF16) for x_ref in x_refs]
    acc = None
    for a, w_ref in zip(lhs, w_refs):
        d = jnp.dot(a, w_ref[...], preferred_element_type=F32)
        acc = d if acc is None else acc + d
    if resid_is_x:
        r_ref = x_refs[0]
    if mode == "resid":
        acc = r_ref[...] + acc
    elif mode == "ple":
        pp = jnp.dot(p_ref[...].astype(BF16), wp_ref[...], preferred_element_type=F32)
        acc = r_ref[...] + pp * jax.nn.sigmoid(acc)
    if has_out_gain:
        acc = acc * lax.rsqrt(jnp.mean(acc * acc, axis=-1, keepdims=True) + NORM_EPS) * og_ref[...]
    if col_prog is None:
        o_ref[...] = acc.astype(o_ref.dtype)
        return
    j = pl.program_id(1)
    nb = o_ref.shape[1] // LANES
    tiles = {}
    for jj in range(len(col_prog) // nb):
        tiles.setdefault(tuple(col_prog[jj * nb:(jj + 1) * nb]), []).append(jj)
    for prog, jjs in tiles.items():
        @pl.when(functools.reduce(jnp.logical_or, [j == jj for jj in jjs]))
        def _():
            c128, s128, c64, s64a, s64b = tab_refs
            for b, (kind, scale) in enumerate(prog):
                x = acc[:, b * LANES:(b + 1) * LANES]
                if kind == "rot128":
                    x = _rot128(x, c128[...], s128[...])
                elif kind == "rot64":
                    x = _rot64(x, c64[...], s64a[...], s64b[...])
                if scale != 1.0:
                    x = x * scale
                o_ref[:, b * LANES:(b + 1) * LANES] = x.astype(o_ref.dtype)
    if f32_from is not None:
        @pl.when(j >= f32_from)
        def _():
            of_ref[...] = acc


def _mm(xs, ws, *, name, gain=None, out_dtype=F32, tm=1024, tn=512, resid=None, ple=None,
        emit_xn=False, out_gain=None, col_prog=None, tables=None, f32_from=None):
    M = xs[0][0].shape[0]
    N = ws[0].shape[1]
    tm, tn = _tile(M, tm), _tile(N, tn)
    nx = len(xs)
    has_gain = gain is not None
    mode = "ple" if ple is not None else ("resid" if resid is not None else "none")
    r_arr = ple[0] if mode == "ple" else resid
    resid_is_x = r_arr is xs[0][0] and tn == N == xs[0][2] and xs[0][1] == 0
    assert out_gain is None or tn == N
    args, in_specs = [], []
    for arr, cb, K in xs:
        args.append(arr)
        in_specs.append(pl.BlockSpec((tm, K), lambda i, j, cb=cb: (i, cb)))
    if has_gain:
        K0 = xs[0][2]
        args.append(gain.reshape(1, K0).astype(F32))
        in_specs.append(pl.BlockSpec((1, K0), lambda i, j: (0, 0)))
    for (arr, cb, K), w in zip(xs, ws):
        args.append(w)
        in_specs.append(pl.BlockSpec((K, tn), lambda i, j: (0, j)))
    if mode in ("resid", "ple") and not resid_is_x:
        args.append(r_arr)
        in_specs.append(pl.BlockSpec((tm, tn), lambda i, j: (i, j)))
    if mode == "ple":
        _, p, wp = ple
        args += [p, wp]
        in_specs += [pl.BlockSpec((tm, p.shape[1]), lambda i, j: (i, 0)),
                     pl.BlockSpec((p.shape[1], tn), lambda i, j: (0, j))]
    if out_gain is not None:
        args.append(out_gain.reshape(1, N).astype(F32))
        in_specs.append(pl.BlockSpec((1, N), lambda i, j: (0, 0)))
    if col_prog is not None:
        assert len(col_prog) * LANES == N
        args += list(tables)
        in_specs += [pl.BlockSpec((tm, LANES), lambda i, j: (i, 0))] * 5
    out_shape = [jax.ShapeDtypeStruct((M, N), out_dtype)]
    out_specs = [pl.BlockSpec((tm, tn), lambda i, j: (i, j))]
    if emit_xn:
        out_shape.append(jax.ShapeDtypeStruct((M, xs[0][2] // 2), jnp.uint32))
        out_specs.append(pl.BlockSpec((tm, xs[0][2] // 2), lambda i, j: (i, 0)))
    if f32_from is not None:
        out_shape.append(jax.ShapeDtypeStruct((M, N - f32_from * tn), F32))
        out_specs.append(pl.BlockSpec((tm, tn), lambda i, j: (i, jnp.maximum(j - f32_from, 0))))
    scratch = [pltpu.VMEM((tm, xs[0][2]), BF16)] if has_gain else []
    res = pl.pallas_call(
        functools.partial(_mm_body, nx=nx, has_gain=has_gain, emit_xn=emit_xn, mode=mode, resid_is_x=resid_is_x,
                          has_out_gain=out_gain is not None,
                          col_prog=None if col_prog is None else tuple(col_prog), f32_from=f32_from),
        out_shape=out_shape, grid=(M // tm, N // tn), in_specs=in_specs, out_specs=out_specs,
        scratch_shapes=scratch, compiler_params=_params("arbitrary", "arbitrary"), name=name,
    )(*args)
    return res if len(res) > 1 else res[0]


def _rope_tables(positions):
    def tables(dim):
        inv = 1.0 / (ROPE_THETA ** (jnp.arange(0, dim, 2, dtype=F32) / dim))
        ang = positions.astype(F32)[..., None] * inv
        return jnp.cos(ang), jnp.sin(ang)

    lead = positions.shape
    c, s = tables(HEAD_DIM)
    c128 = jnp.concatenate([c, c], -1)
    s128 = jnp.concatenate([-s, s], -1)
    c, s = tables(B_ROPE)
    z = jnp.zeros_like(c)
    c64 = jnp.concatenate([c, c, z, z], -1)
    s64a = jnp.concatenate([-s, z, z, z], -1)
    s64b = jnp.concatenate([z, s, z, z], -1)
    n = int(np.prod(lead))
    return [t.reshape(n, LANES) for t in (c128, s128, c64, s64a, s64b)]


def _pair_tables(S, tq, tk, window):
    qi_l, ki_l, fl_l, mk_l, masks, ids = [], [], [], [], [], {}
    for qi in range(S // tq):
        q0, q1 = qi * tq, (qi + 1) * tq - 1
        ks = []
        for ki in range(S // tk):
            k0, k1 = ki * tk, (ki + 1) * tk - 1
            if k0 > q1 or (window is not None and q0 - k1 >= window):
                continue
            full = k1 <= q0 and (window is None or q1 - k0 < window)
            mid = 0
            if not full:
                delta = q0 - k0
                if delta not in ids:
                    rel = np.arange(tq)[:, None] + delta - np.arange(tk)[None, :]
                    ok = (rel >= 0) if window is None else ((rel >= 0) & (rel < window))
                    masks.append(np.where(ok, 0.0, NEG_INF).astype(np.float32))
                    ids[delta] = len(masks)
                mid = ids[delta]
            ks.append((ki, mid))
        for n, (ki, mid) in enumerate(ks):
            qi_l.append(qi)
            ki_l.append(ki)
            fl_l.append((1 if n == 0 else 0) | (2 if n == len(ks) - 1 else 0))
            mk_l.append(mid)
    if not masks:
        masks.append(np.zeros((tq, tk), np.float32))
    tabs = [jnp.asarray(np.array(a, np.int32)) for a in (qi_l, ki_l, fl_l, mk_l)]
    return tabs, jnp.asarray(np.stack(masks))


def _flash_body(qi_t, ki_t, fl_t, mk_t, *refs, nq, nkv, q_src, k_src, v_src, tq, rc, has_sink,
                has_bias, has_sel):
    nc = len(q_src)
    it = iter(refs)
    q_refs = [next(it) for _ in range(nq)]
    kv_refs = [next(it) for _ in range(nkv)]
    mask_ref = next(it)
    sink_ref = next(it) if has_sink else None
    cum_ref = next(it) if has_bias else None
    cumt_ref = next(it) if has_bias else None
    sel_ref = next(it) if has_sel else None
    exp_ref = next(it) if has_sel else None
    o_ref = next(it)
    m_sc, acc_sc = next(it), next(it)
    cq_sc = next(it) if has_bias else None

    step_id = pl.program_id(2)
    fl, mk = fl_t[step_id], mk_t[step_id]

    def cat(refs_, src, rows=slice(None)):
        xs = [refs_[pi][0, rows, off:off + LANES] for pi, off in src]
        return xs[0] if len(xs) == 1 else jnp.concatenate(xs, axis=-1)

    @pl.when((fl & 1) != 0)
    def _init():
        m_sc[...] = jnp.full(m_sc.shape, NEG_INF, F32)
        acc_sc[...] = jnp.zeros(acc_sc.shape, F32)
        if has_bias:
            lane = lax.broadcasted_iota(jnp.int32, (tq, LANES), 1)
            for g in range(nc):
                head = pl.program_id(1) * nc + g
                cq_sc[g] = jnp.sum(jnp.where(lane == head, cum_ref[0], 0.0), axis=-1, keepdims=True)

    tk = kv_refs[0].shape[1]
    reps = tk // LANES
    ones = jnp.ones((tk, LANES), BF16)

    def step(masked):
        ks = [cat(kv_refs, k_src[g]) for g in range(nc)]
        vs = [jnp.concatenate([kv_refs[pi][0, :, off:off + LANES], ones], axis=1) for pi, off in v_src]
        for r in range(tq // rc):
            rows = slice(r * rc, (r + 1) * rc)
            add = mask_ref[mk - 1, rows, :] if masked else None
            if has_sel:
                hidden = (jnp.dot(sel_ref[0, 0, rows, :], exp_ref[0], preferred_element_type=F32) - 1.0) * (-NEG_INF)
                add = hidden if add is None else add + hidden
            for g in range(nc):
                q = cat(q_refs, q_src[g], rows)
                s = lax.dot_general(q, ks[g], (((1,), (1,)), ((), ())), preferred_element_type=F32)
                if has_bias:
                    s = s + (cq_sc[g, rows, :] - cumt_ref[0, g])
                if add is not None:
                    s = s + add
                m_prev = m_sc[g, rows, :]
                m_new = jnp.maximum(m_prev, jnp.max(s, axis=-1, keepdims=True))
                p = jnp.exp2(s - (jnp.concatenate([m_new] * reps, axis=1) if reps > 1 else m_new))
                alpha = jnp.exp2(m_prev - m_new)
                acc_sc[g, rows, :] = (jnp.concatenate([alpha, alpha], axis=1) * acc_sc[g, rows, :]
                                      + jnp.dot(p.astype(BF16), vs[g], preferred_element_type=F32))
                m_sc[g, rows, :] = m_new

    @pl.when(mk != 0)
    def _():
        step(True)

    @pl.when(mk == 0)
    def _():
        step(False)

    @pl.when((fl & 2) != 0)
    def _finish():
        for g in range(nc):
            m, acc, l = m_sc[g], acc_sc[g, :, :LANES], acc_sc[g, :, LANES:]
            if has_sink:
                sk = sink_ref[0, g:g + 1, 0:1] * LOG2E
                m_f = jnp.maximum(m, sk)
                w = jnp.exp2(m - m_f)
                l = l * w + jnp.exp2(sk - m_f)
                acc = acc * w
            o_ref[0, :, g * LANES:(g + 1) * LANES] = (acc / l).astype(o_ref.dtype)


def _flash(q_parts, kv_parts, q_src, k_src, v_src, *, name, B, S, n_steps, tq, tk, window=None, sinks=None,
           cum=None, cumt=None, sel=None, out_dtype=BF16, rc=FLASH_ROW_CHUNK):
    nc = len(q_src)
    tq, tk = _tile(S, tq), _tile(S, tk)
    tabs, masks = _pair_tables(S, tq, tk, window)
    npairs = int(tabs[0].shape[0])
    args, in_specs = [], []
    for arr, width, cf in q_parts:
        args.append(arr)
        in_specs.append(pl.BlockSpec((1, tq, width), lambda b, h, s, qt, kt, ft, mt, cf=cf: (b, qt[s], cf(h))))
    for arr, width, cf in kv_parts:
        args.append(arr)
        in_specs.append(pl.BlockSpec((1, tk, width), lambda b, h, s, qt, kt, ft, mt, cf=cf: (b, kt[s], cf(h))))
    args.append(masks)
    in_specs.append(pl.BlockSpec(masks.shape, lambda b, h, s, qt, kt, ft, mt: (0, 0, 0)))
    if sinks is not None:
        args.append(jnp.broadcast_to(sinks.astype(F32).reshape(n_steps, nc, 1), (n_steps, nc, LANES)))
        in_specs.append(pl.BlockSpec((1, nc, LANES), lambda b, h, s, qt, kt, ft, mt: (h, 0, 0)))
    if cum is not None:
        args += [cum, cumt]
        in_specs += [pl.BlockSpec((1, tq, LANES), lambda b, h, s, qt, kt, ft, mt: (b, qt[s], 0)),
                     pl.BlockSpec((1, nc, 1, tk), lambda b, h, s, qt, kt, ft, mt: (b, h, 0, kt[s]))]
    if sel is not None:
        per = tk // D_SLC_LEN
        e = np.zeros((S // tk, LANES, tk), np.float32)
        for ki in range(S // tk):
            e[ki, ki * per + np.arange(tk) // D_SLC_LEN, np.arange(tk)] = 1.0
        args += [sel, jnp.asarray(e, BF16)]
        in_specs += [pl.BlockSpec((1, 1, tq, LANES), lambda b, h, s, qt, kt, ft, mt: (b, h, qt[s], 0)),
                     pl.BlockSpec((1, LANES, tk), lambda b, h, s, qt, kt, ft, mt: (kt[s], 0, 0))]
    scratch = [pltpu.VMEM((nc, tq, LANES), F32), pltpu.VMEM((nc, tq, 2 * LANES), F32)]
    if cum is not None:
        scratch.append(pltpu.VMEM((nc, tq, 1), F32))
    body = functools.partial(_flash_body, nq=len(q_parts), nkv=len(kv_parts), q_src=q_src, k_src=k_src,
                             v_src=v_src, tq=tq, rc=_tile(tq, rc), has_sink=sinks is not None,
                             has_bias=cum is not None, has_sel=sel is not None)
    return pl.pallas_call(
        body, out_shape=jax.ShapeDtypeStruct((B, S, n_steps * nc * LANES), out_dtype),
        grid_spec=pltpu.PrefetchScalarGridSpec(
            num_scalar_prefetch=4, grid=(B, n_steps, npairs), in_specs=in_specs,
            out_specs=pl.BlockSpec((1, tq, nc * LANES), lambda b, h, s, qt, kt, ft, mt: (b, qt[s], h)),
            scratch_shapes=scratch),
        compiler_params=_params("arbitrary", "arbitrary", "arbitrary"), name=name,
    )(*tabs, *args)


def _gqa_flash(q, k, v, *, name, B, S, Hk, G, **kw):
    at = lambda first: (lambda h: first + h)
    return _flash([(q[0], G * LANES, at(q[1]))], [(k[0], LANES, at(k[1])), (v[0], LANES, at(v[1]))],
                  [[(0, g * LANES)] for g in range(G)], [[(0, 0)]] * G, [(1, 0)] * G,
                  name=name, B=B, S=S, n_steps=Hk, **kw)


def _cum_body(y_ref, b_ref, tri_ref, o_ref, carry):
    @pl.when(pl.program_id(1) == 0)
    def _():
        carry[...] = jnp.zeros(carry.shape, F32)
    x = y_ref[0] + b_ref[...]
    logf = jnp.minimum(x, 0.0) - jnp.log1p(jnp.exp(-jnp.abs(x)))
    cum = jnp.dot(tri_ref[...], logf, preferred_element_type=F32, precision=lax.Precision.HIGHEST) + carry[...]
    o_ref[0] = cum * LOG2E
    carry[...] = cum[-1:, :]


def _forget_cum(y3, col_block, bias_row, *, ts=512):
    B, S, _ = y3.shape
    ts = _tile(S, ts)
    tri = jnp.asarray(np.tril(np.ones((ts, ts), np.float32)))
    return pl.pallas_call(
        _cum_body, out_shape=jax.ShapeDtypeStruct((B, S, LANES), F32), grid=(B, S // ts),
        in_specs=[pl.BlockSpec((1, ts, LANES), lambda b, s: (b, s, col_block)),
                  pl.BlockSpec((1, LANES), lambda b, s: (0, 0)),
                  pl.BlockSpec((ts, ts), lambda b, s: (0, 0))],
        out_specs=pl.BlockSpec((1, ts, LANES), lambda b, s: (b, s, 0)),
        scratch_shapes=[pltpu.VMEM((1, LANES), F32)],
        compiler_params=_params("arbitrary", "arbitrary"), name="forget_cum",
    )(y3, bias_row, tri)


def _compress_body(*refs, rope, nc, width, col0):
    if rope:
        z_ref, pe_ref, w1_ref, w2_ref, c_ref, s_ref, o_ref = refs
    else:
        z_ref, pe_ref, w1_ref, w2_ref, o_ref = refs
    half = D_CMP_LEN // 2
    for hk in range(D_KV_HEADS):
        u = jnp.zeros((nc, w1_ref.shape[1]), F32)
        v = jnp.zeros((nc, w1_ref.shape[1]), F32)
        for l in range(half):
            first = l * width + col0 + hk * HEAD_DIM
            z = z_ref[0, :, first:first + HEAD_DIM]
            zu = (z + pe_ref[l:l + 1, :]).astype(BF16)
            zv = (z + pe_ref[half + l:half + l + 1, :]).astype(BF16)
            u = u + jnp.dot(zu, w1_ref[l * HEAD_DIM:(l + 1) * HEAD_DIM, :], preferred_element_type=F32)
            v = v + jnp.dot(zv, w1_ref[(half + l) * HEAD_DIM:(half + l + 1) * HEAD_DIM, :],
                            preferred_element_type=F32)
        pre = u + pltpu.roll(v, nc - 1, 0)
        hid = jax.nn.gelu(pre, approximate=True)
        out = jnp.dot(hid.astype(BF16), w2_ref[...], preferred_element_type=F32)
        if rope:
            out = _rot128(out, c_ref[0], s_ref[0])
        o_ref[0, hk] = out.astype(o_ref.dtype)


def _compress(z, col0, pe, w1, w2, rope_tabs=None):
    B, S, W = z.shape
    nc = S // D_CMP_STRIDE
    zc = z.reshape(B, nc, D_CMP_STRIDE * W)
    args = [zc, pe.astype(F32), w1.astype(BF16), w2.astype(BF16)]
    in_specs = [pl.BlockSpec((1, nc, D_CMP_STRIDE * W), lambda b: (b, 0, 0)),
                pl.BlockSpec(pe.shape, lambda b: (0, 0)),
                pl.BlockSpec(w1.shape, lambda b: (0, 0)),
                pl.BlockSpec(w2.shape, lambda b: (0, 0))]
    if rope_tabs is not None:
        args += list(rope_tabs)
        in_specs += [pl.BlockSpec((1, nc, LANES), lambda b: (b, 0, 0))] * 2
    return pl.pallas_call(
        functools.partial(_compress_body, rope=rope_tabs is not None, nc=nc, width=W, col0=col0),
        out_shape=jax.ShapeDtypeStruct((B, D_KV_HEADS, nc, HEAD_DIM), BF16), grid=(B,),
        in_specs=in_specs, out_specs=pl.BlockSpec((1, D_KV_HEADS, nc, HEAD_DIM), lambda b: (b, 0, 0, 0)),
        compiler_params=_params("arbitrary"), name="nsa_compress",
    )(*args)


def _cmp_attn_body(q_ref, k_ref, v_ref, ov_ref, o_ref, sel_ref, *, G, tq, nc, n_cmp, n_slc, topn):
    qi = pl.program_id(2)
    t = qi * tq + lax.broadcasted_iota(jnp.int32, (tq, nc), 0)
    c = lax.broadcasted_iota(jnp.int32, (tq, nc), 1)
    valid = (c * D_CMP_STRIDE + (D_CMP_LEN - 1) <= t) & (c < n_cmp)
    k = k_ref[0, 0]
    v = v_ref[0, 0]
    psum = jnp.zeros((tq, nc), F32)
    for g in range(G):
        q = q_ref[0, :, g * LANES:(g + 1) * LANES]
        s = lax.dot_general(q, k, (((1,), (1,)), ((), ())), preferred_element_type=F32)
        s = jnp.where(valid, s, NEG_INF)
        e = jnp.where(valid, jnp.exp2(s - jnp.max(s, axis=-1, keepdims=True)), 0.0)
        p = e / jnp.maximum(jnp.sum(e, axis=-1, keepdims=True), jnp.finfo(F32).tiny)
        o_ref[0, :, g * LANES:(g + 1) * LANES] = jnp.dot(
            p.astype(BF16), v, preferred_element_type=F32).astype(o_ref.dtype)
        psum = psum + p
    imp = lax.dot_general(ov_ref[...], psum, (((1,), (1,)), ((), ())), preferred_element_type=F32,
                          precision=lax.Precision.HIGHEST)
    blk = lax.broadcasted_iota(jnp.int32, (LANES, tq), 0)
    tcol = qi * tq + lax.broadcasted_iota(jnp.int32, (LANES, tq), 1)
    cur = jnp.right_shift(tcol, int(math.log2(D_SLC_LEN)))
    forced = (blk == 0) | (blk == cur) | (blk == cur - 1)
    imp = jnp.where(blk * D_SLC_LEN > tcol, NEG_INF, imp + jnp.where(forced, FORCE_BONUS, 0.0))
    imp = jnp.where(blk >= n_slc, TAKEN, imp)
    chosen = jnp.zeros((LANES, tq), F32)
    blk_f = blk.astype(F32)
    for _ in range(topn):
        mx = jnp.max(imp, axis=0, keepdims=True)
        idx = jnp.min(jnp.where(imp == mx, blk_f, float(LANES)), axis=0, keepdims=True)
        hit = blk_f == idx
        chosen = jnp.where(hit, 1.0, chosen)
        imp = jnp.where(hit, TAKEN, imp)
    sel_ref[0, 0] = chosen.T.astype(sel_ref.dtype)


def _cmp_attn(q, q_first, k_cmp, v_cmp, *, tq=256):
    B, S, _ = q.shape
    Hk, G = D_KV_HEADS, D_HEADS // D_KV_HEADS
    nc = S // D_CMP_STRIDE
    n_cmp = (S - D_CMP_LEN) // D_CMP_STRIDE + 1
    n_slc = S // D_SLC_LEN
    tq = _tile(S, tq)
    c0 = np.arange(nc) * D_CMP_STRIDE
    s0 = np.arange(LANES) * D_SLC_LEN
    ov = ((c0[:, None] < (s0 + D_SLC_LEN)[None, :]) & ((c0 + D_CMP_LEN)[:, None] > s0[None, :])
          & (np.arange(nc) < n_cmp)[:, None] & (np.arange(LANES) < n_slc)[None, :]).astype(np.float32)
    body = functools.partial(_cmp_attn_body, G=G, tq=tq, nc=nc, n_cmp=n_cmp, n_slc=n_slc,
                             topn=min(D_SLC_TOPN, n_slc))
    return pl.pallas_call(
        body,
        out_shape=[jax.ShapeDtypeStruct((B, S, Hk * G * LANES), F32),
                   jax.ShapeDtypeStruct((B, Hk, S, LANES), BF16)],
        grid=(B, Hk, S // tq),
        in_specs=[pl.BlockSpec((1, tq, G * LANES), lambda b, h, i: (b, i, q_first + h)),
                  pl.BlockSpec((1, 1, nc, LANES), lambda b, h, i: (b, h, 0, 0)),
                  pl.BlockSpec((1, 1, nc, LANES), lambda b, h, i: (b, h, 0, 0)),
                  pl.BlockSpec((LANES, nc), lambda b, h, i: (0, 0))],
        out_specs=[pl.BlockSpec((1, tq, G * LANES), lambda b, h, i: (b, i, h)),
                   pl.BlockSpec((1, 1, tq, LANES), lambda b, h, i: (b, h, i, 0))],
        compiler_params=_params("arbitrary", "arbitrary", "arbitrary"), name="nsa_cmp_attn",
    )(q, k_cmp, v_cmp, jnp.asarray(ov.T))


def _gate_body(y_ref, a_ref, b_ref, c_ref, o_ref, *, lane0):
    g = jax.nn.sigmoid(y_ref[...])
    for h in range(D_HEADS):
        cols = slice(h * LANES, (h + 1) * LANES)
        ga = g[:, lane0 + h:lane0 + h + 1]
        gb = g[:, lane0 + D_HEADS + h:lane0 + D_HEADS + h + 1]
        gc = g[:, lane0 + 2 * D_HEADS + h:lane0 + 2 * D_HEADS + h + 1]
        o_ref[:, cols] = (ga * a_ref[:, cols] + gb * b_ref[:, cols] + gc * c_ref[:, cols]).astype(o_ref.dtype)


def _nsa_gate(y, col_block, lane0, o_cmp, o_slc, o_win, *, tm=512):
    M = y.shape[0]
    W = o_cmp.shape[1]
    tm = _tile(M, tm)
    row = lambda i: (i, 0)
    return pl.pallas_call(
        functools.partial(_gate_body, lane0=lane0), out_shape=jax.ShapeDtypeStruct((M, W), BF16),
        grid=(M // tm,),
        in_specs=[pl.BlockSpec((tm, LANES), lambda i: (i, col_block))] + [pl.BlockSpec((tm, W), row)] * 3,
        out_specs=pl.BlockSpec((tm, W), row), compiler_params=_params("arbitrary"), name="nsa_gate",
    )(y, o_cmp, o_slc, o_win)


def _route_body(lg_ref, b_ref, tri_ref, e_ref, w_ref, pos_ref, cnt_ref, carry, *, tm):
    @pl.when(pl.program_id(0) == 0)
    def _():
        carry[...] = jnp.zeros(carry.shape, F32)
    lane = lax.broadcasted_iota(jnp.int32, (tm, LANES), 1)
    logits = lg_ref[...] + b_ref[...]
    gl = jnp.where(lane < N_GROUPS, logits, -jnp.inf)
    gmax = jnp.max(gl, axis=-1, keepdims=True)
    g_val = 1.0 / jnp.sum(jnp.exp(gl - gmax), axis=-1, keepdims=True)
    g_idx = jnp.min(jnp.where(gl == gmax, lane, LANES), axis=-1, keepdims=True)
    lo = N_GROUPS + EXPERTS_PER_GROUP * g_idx
    el = jnp.where((lane >= lo) & (lane < lo + EXPERTS_PER_GROUP), logits, -jnp.inf)
    e1 = jnp.max(el, axis=-1, keepdims=True)
    i1 = jnp.min(jnp.where(el == e1, lane, LANES), axis=-1, keepdims=True)
    el2 = jnp.where(lane == i1, -jnp.inf, el)
    e2 = jnp.max(el2, axis=-1, keepdims=True)
    i2 = jnp.min(jnp.where(el2 == e2, lane, LANES), axis=-1, keepdims=True)
    r = jnp.exp(e2 - e1)
    w1 = g_val / (1.0 + r)
    w2 = w1 * r
    x1, x2 = i1 - N_GROUPS, i2 - N_GROUPS
    e_ref[...] = jnp.where(lane == 0, x1, jnp.where(lane == 1, x2, 0))
    w_ref[...] = jnp.where(lane == 0, w1, jnp.where(lane == 1, w2, 0.0))
    hot1 = lane == x1
    hot2 = lane == x2
    both = jnp.where(hot1 | hot2, 1.0, 0.0)
    before = jnp.dot(tri_ref[...], both.astype(BF16), preferred_element_type=F32) + carry[...]
    p1 = jnp.sum(jnp.where(hot1, before, 0.0), axis=-1, keepdims=True)
    p2 = jnp.sum(jnp.where(hot2, before, 0.0), axis=-1, keepdims=True)
    pos_ref[...] = jnp.where(lane == 0, p1, jnp.where(lane == 1, p2, 0.0)).astype(jnp.int32)
    carry[...] = carry[...] + jnp.sum(both, axis=0, keepdims=True)
    cnt_ref[...] = carry[...].astype(jnp.int32)


def _route(logits, bias_row, *, tm=512):
    T = logits.shape[0]
    tm = _tile(T, tm)
    tri = jnp.asarray(np.tril(np.ones((tm, tm), np.float32), -1), BF16)
    row = lambda i: (i, 0)
    fixed = lambda i: (0, 0)
    return pl.pallas_call(
        functools.partial(_route_body, tm=tm),
        out_shape=[jax.ShapeDtypeStruct((T, LANES), jnp.int32), jax.ShapeDtypeStruct((T, LANES), F32),
                   jax.ShapeDtypeStruct((T, LANES), jnp.int32), jax.ShapeDtypeStruct((1, LANES), jnp.int32)],
        grid=(T // tm,),
        in_specs=[pl.BlockSpec((tm, LANES), row), pl.BlockSpec((1, LANES), fixed), pl.BlockSpec((tm, tm), fixed)],
        out_specs=[pl.BlockSpec((tm, LANES), row)] * 3 + [pl.BlockSpec((1, LANES), fixed)],
        scratch_shapes=[pltpu.VMEM((1, LANES), F32)],
        compiler_params=_params("arbitrary"), name="moe_route",
    )(logits, bias_row, tri)


def _dispatch_body(d_ref, ends_ref, pad_ref, nu_ref, x_ref, out_ref, zero_sc, sem, zsem, *, tm, nblk):
    base = pl.program_id(0) * tm

    @pl.when(pl.program_id(0) == 0)
    def _():
        zero_sc[...] = jnp.zeros(zero_sc.shape, zero_sc.dtype)

        def zero_block(first_row):
            return pltpu.make_async_copy(zero_sc, out_ref.at[pl.ds(pl.multiple_of(first_row, MOE_ROWS), MOE_ROWS)],
                                         zsem)

        for wait in (False, True):
            for e in range(N_EXPERTS):
                for live, first_row in ((pad_ref[e] > 0, ends_ref[e] - MOE_ROWS),
                                        (nu_ref[0] + e < nblk, (nu_ref[0] + e) * MOE_ROWS)):
                    @pl.when(live)
                    def _():
                        zero_block(first_row).wait() if wait else zero_block(first_row).start()

    def issue(r, c):
        for k in range(TOP_K):
            row = d_ref[(base + r) * TOP_K + k]
            pltpu.make_async_copy(x_ref.at[pl.ds(r, 1)], out_ref.at[pl.ds(row, 1)], sem).start(priority=k)
        return c

    lax.fori_loop(0, tm, issue, 0, unroll=DMA_UNROLL)
    for _ in range(TOP_K):
        pltpu.make_async_copy(x_ref, out_ref.at[pl.ds(0, tm)], sem).wait()


def _dispatch(xn, dest, ends, padded, n_used, n_rows, *, tm=512):
    T, D = xn.shape
    tm = _tile(T, tm)
    return pl.pallas_call(
        functools.partial(_dispatch_body, tm=tm, nblk=n_rows // MOE_ROWS),
        out_shape=jax.ShapeDtypeStruct((n_rows, D), xn.dtype),
        grid_spec=pltpu.PrefetchScalarGridSpec(
            num_scalar_prefetch=4, grid=(T // tm,),
            in_specs=[pl.BlockSpec((tm, D), lambda i, *_: (i, 0))],
            out_specs=pl.BlockSpec(memory_space=pl.ANY),
            scratch_shapes=[pltpu.VMEM((MOE_ROWS, D), xn.dtype), pltpu.SemaphoreType.DMA(()),
                            pltpu.SemaphoreType.DMA(())]),
        compiler_params=pltpu.CompilerParams(dimension_semantics=("arbitrary",), has_side_effects=True,
                                             vmem_limit_bytes=VMEM_LIMIT_BYTES, disable_bounds_checks=True),
        name="moe_dispatch",
    )(dest, ends, padded, n_used, xn)


def _expert_body(nb_ref, first_ref, nu_ref, xb_ref, wg_ref, wu_ref, wd_ref, yb_ref, wg_sc, wu_sc, wd_sc, xbuf,
                 obuf, sem_in, sem_out, *, nblk):
    e = pl.program_id(0)
    nb, first = nb_ref[e], first_ref[e]

    def rows(b):
        return pl.ds(pl.multiple_of((first + b) * MOE_ROWS, MOE_ROWS), MOE_ROWS)

    def fetch(b, slot):
        return pltpu.make_async_copy(xb_ref.at[rows(b)], xbuf.at[slot], sem_in.at[slot])

    def flush(b, slot):
        return pltpu.make_async_copy(obuf.at[slot], yb_ref.at[rows(b)], sem_out.at[slot])

    @pl.when(nb > 0)
    def _():
        fetch(0, 0).start()
        wg_sc[...] = wg_ref[0, 0].astype(BF16)
        wu_sc[...] = wu_ref[0, 0].astype(BF16)
        wd_sc[...] = wd_ref[0, 0].astype(BF16)

        def block(b, c):
            slot = lax.rem(b, 2)

            @pl.when(b + 1 < nb)
            def _():
                fetch(b + 1, 1 - slot).start()

            fetch(b, slot).wait()

            @pl.when(b >= 2)
            def _():
                flush(b - 2, slot).wait()

            x = jnp.concatenate(_unpack_bf16_pairs(xbuf[slot]), axis=1).astype(BF16)
            gate = jnp.dot(x, wg_sc[...], preferred_element_type=F32)
            up = jnp.dot(x, wu_sc[...], preferred_element_type=F32)
            hid = (gate * jax.nn.sigmoid(gate) * up).astype(BF16)
            obuf[slot] = _pack_bf16_pairs(jnp.dot(hid, wd_sc[...], preferred_element_type=F32))
            flush(b, slot).start()
            return c

        lax.fori_loop(0, nb, block, 0)

        @pl.when(nb >= 2)
        def _():
            flush(nb - 2, lax.rem(nb, 2)).wait()

        flush(nb - 1, lax.rem(nb - 1, 2)).wait()

    @pl.when(e == pl.num_programs(0) - 1)
    def _():
        obuf[0] = jnp.zeros(obuf.shape[1:], obuf.dtype)

        def tail(t):
            return pltpu.make_async_copy(obuf.at[0], yb_ref.at[pl.ds(pl.multiple_of((nu_ref[0] + t) * MOE_ROWS,
                                                                                   MOE_ROWS), MOE_ROWS)], sem_out.at[0])

        for wait in (False, True):
            for t in range(N_EXPERTS):
                @pl.when(nu_ref[0] + t < nblk)
                def _():
                    tail(t).wait() if wait else tail(t).start()


def _experts(xb, n_blocks, first_block, n_used, layer, w_gate, w_up, w_down):
    D, Hd = w_gate.shape[2], w_gate.shape[3]
    nblk = xb.shape[0] // MOE_ROWS
    wspec = lambda shape: pl.BlockSpec((1, 1) + shape, lambda e, *_: (layer, e, 0, 0))
    return pl.pallas_call(
        functools.partial(_expert_body, nblk=nblk), out_shape=jax.ShapeDtypeStruct(xb.shape, xb.dtype),
        grid_spec=pltpu.PrefetchScalarGridSpec(
            num_scalar_prefetch=3, grid=(w_gate.shape[1],),
            in_specs=[pl.BlockSpec(memory_space=pl.ANY), wspec((D, Hd)), wspec((D, Hd)), wspec((Hd, D))],
            out_specs=pl.BlockSpec(memory_space=pl.ANY),
            scratch_shapes=[pltpu.VMEM((D, Hd), BF16), pltpu.VMEM((D, Hd), BF16), pltpu.VMEM((Hd, D), BF16),
                            pltpu.VMEM((2, MOE_ROWS, D // 2), xb.dtype), pltpu.VMEM((2, MOE_ROWS, D // 2), xb.dtype),
                            pltpu.SemaphoreType.DMA((2,)), pltpu.SemaphoreType.DMA((2,))]),
        compiler_params=pltpu.CompilerParams(dimension_semantics=("arbitrary",), has_side_effects=True,
                                             vmem_limit_bytes=VMEM_LIMIT_BYTES),
        name="moe_experts",
    )(n_blocks, first_block, n_used, xb, w_gate, w_up, w_down)


def _collect_body(d_ref, h_ref, w_ref, yb_ref, o_ref, buf_a, buf_b, sem, *, tm, nsteps):
    i = pl.program_id(0)
    slot = lax.rem(i, 2)

    def fetch(step, slot_):
        base = step * tm

        def issue(r, c):
            for k, buf in enumerate((buf_a, buf_b)):
                row = d_ref[(base + r) * TOP_K + k]
                pltpu.make_async_copy(yb_ref.at[pl.ds(row, 1)], buf.at[slot_, pl.ds(r, 1)],
                                      sem.at[slot_]).start(priority=k)
            return c

        lax.fori_loop(0, tm, issue, 0, unroll=DMA_UNROLL)

    @pl.when(i == 0)
    def _():
        fetch(0, 0)

    @pl.when(i + 1 < nsteps)
    def _():
        fetch(i + 1, 1 - slot)

    for buf in (buf_a, buf_b):
        pltpu.make_async_copy(yb_ref.at[pl.ds(0, tm)], buf.at[slot], sem.at[slot]).wait()
    w = w_ref[...]
    half = h_ref.shape[1] // 2
    for part, ya, yb in zip((slice(0, half), slice(half, None)), _unpack_bf16_pairs(buf_a[slot]),
                            _unpack_bf16_pairs(buf_b[slot])):
        o_ref[:, part] = h_ref[:, part] + w[:, 0:1] * ya + w[:, 1:2] * yb


def _collect(h, yb, dest, wts, *, tm=256):
    T, D = h.shape
    tm = _tile(T, tm)
    nsteps = T // tm
    return pl.pallas_call(
        functools.partial(_collect_body, tm=tm, nsteps=nsteps),
        out_shape=jax.ShapeDtypeStruct((T, D), F32),
        grid_spec=pltpu.PrefetchScalarGridSpec(
            num_scalar_prefetch=1, grid=(nsteps,),
            in_specs=[pl.BlockSpec((tm, D), lambda i, d: (i, 0)), pl.BlockSpec((tm, LANES), lambda i, d: (i, 0)),
                      pl.BlockSpec(memory_space=pl.ANY)],
            out_specs=pl.BlockSpec((tm, D), lambda i, d: (i, 0)),
            scratch_shapes=[pltpu.VMEM((2, tm, D // 2), yb.dtype), pltpu.VMEM((2, tm, D // 2), yb.dtype),
                            pltpu.SemaphoreType.DMA((2,))]),
        compiler_params=pltpu.CompilerParams(dimension_semantics=("arbitrary",), disable_bounds_checks=True,
                                             vmem_limit_bytes=VMEM_LIMIT_BYTES),
        name="moe_collect",
    )(dest, h, wts, yb)


def _moe(h, norm_g, w_group, b_group, w_expert, b_expert, layer, w_gate, w_up, w_down):
    T, D = h.shape
    pad = LANES - N_GROUPS - N_EXPERTS
    w_r = jnp.concatenate([w_group, w_expert, jnp.zeros((D, pad), F32)], axis=1).astype(BF16)
    b_r = jnp.concatenate([b_group, b_expert, jnp.zeros((pad,), F32)]).astype(F32).reshape(1, LANES)
    logits, xn = _mm([(h, 0, D)], [w_r], gain=norm_g, emit_xn=True, tm=512, tn=LANES, name="moe_router")
    eid, wts, pos, cnt = _route(logits, b_r)
    counts = cnt[0, :N_EXPERTS]
    padded = (counts + MOE_ROWS - 1) // MOE_ROWS * MOE_ROWS
    ends = jnp.cumsum(padded)
    offs = ends - padded
    dest = (offs[eid[:, :TOP_K]] + pos[:, :TOP_K]).reshape(T * TOP_K).astype(jnp.int32)
    P = T * TOP_K + N_EXPERTS * MOE_ROWS
    n_used = (ends[-1:] // MOE_ROWS).astype(jnp.int32)
    xb = _dispatch(xn, dest, ends.astype(jnp.int32), padded.astype(jnp.int32), n_used, P)
    yb = _experts(xb, (padded // MOE_ROWS).astype(jnp.int32), (offs // MOE_ROWS).astype(jnp.int32), n_used, layer,
                  w_gate, w_up, w_down)
    return _collect(h, yb, dest, wts)


def _pad_cols(w, n):
    return jnp.pad(w, ((0, 0), (0, n - w.shape[1])))


def _layer_even(h, B, S, tabs, norm_g, w_in, sinks, q_lat_norm, kv_lat_norm, w_uq, w_ukv, w_o):
    T, D = h.shape
    n_in, tn = 2560, 512
    sa = HEAD_DIM ** -0.5 * LOG2E
    prog = ([("rot128", sa)] * 8 + [("rot128", 1.0)] * 2 + [("copy", 1.0)] * 8 + [("rot64", 1.0), ("copy", 1.0)])
    y, yf = _mm([(h, 0, D)], [_pad_cols(w_in, n_in).astype(BF16)], gain=norm_g, tn=tn, out_dtype=BF16,
                col_prog=prog, tables=tabs, f32_from=12 * LANES // tn, name="ab_in")
    wq = w_uq.reshape(B_Q_LORA, B_HEADS, B_NOPE + B_ROPE)
    wq_n = wq[:, :, :B_NOPE].reshape(B_Q_LORA, B_HEADS * B_NOPE)
    wq_r = jnp.pad(wq[:, :, B_NOPE:], ((0, 0), (0, 0), (0, LANES - B_ROPE))).reshape(B_Q_LORA, B_HEADS * LANES)
    sb = (B_NOPE + B_ROPE) ** -0.5 * LOG2E
    qq = _mm([(yf, 0, B_Q_LORA)], [jnp.concatenate([wq_n, wq_r], 1).astype(BF16)], gain=q_lat_norm, tn=512,
             out_dtype=BF16, col_prog=[("copy", sb)] * 8 + [("rot64", sb)] * 8, tables=tabs, name="mla_uq")
    kv = _mm([(yf, B_Q_LORA // B_KV_LORA, B_KV_LORA)], [w_ukv.astype(BF16)], gain=kv_lat_norm, out_dtype=BF16,
             tn=512, name="mla_ukv")
    r3 = lambda a: a.reshape(B, S, a.shape[-1])
    y3 = r3(y)
    o_a = _gqa_flash((y3, 0), (y3, 8), (y3, 10), name="swa_attn", B=B, S=S, Hk=A_KV_HEADS,
                     G=A_HEADS // A_KV_HEADS, window=A_WINDOW, sinks=sinks, tq=256, tk=256)
    hp = 2
    head = lambda h: h
    o_b = _flash([(r3(qq), hp * LANES, head), (r3(qq), hp * LANES, lambda h: B_HEADS // hp + h)],
                 [(r3(kv), hp * 2 * LANES, head), (y3, LANES, lambda h: 18)],
                 [[(0, g * LANES), (1, g * LANES)] for g in range(hp)],
                 [[(0, 2 * g * LANES), (1, 0)] for g in range(hp)], [(0, (2 * g + 1) * LANES) for g in range(hp)],
                 name="mla_attn", B=B, S=S, n_steps=B_HEADS // hp, tq=1024, tk=1024, rc=256)
    wo = w_o.astype(BF16)
    na = A_HEADS * HEAD_DIM
    return _mm([(o_a.reshape(T, -1), 0, na), (o_b.reshape(T, -1), 0, B_HEADS * B_V)], [wo[:na], wo[na:]],
               resid=h, tm=512, tn=D, name="ab_out")


def _layer_odd(h, B, S, tabs, cmp_tabs, norm_g, w_in, forget_bias, pe_k, w1_k, w2_k, pe_v, w1_v, w2_v, w_o):
    T, D = h.shape
    hc, kvw = C_HEADS * HEAD_DIM, D_KV_HEADS * HEAD_DIM
    o_cf = 3 * hc
    o_dq = o_cf + C_HEADS
    o_kc = o_dq + D_HEADS * HEAD_DIM
    o_ks = o_kc + 2 * kvw
    o_dg = o_ks + 4 * kvw
    w_r = jnp.concatenate([w_in[:, :o_cf], w_in[:, o_dq:o_kc], w_in[:, o_ks:o_dg], w_in[:, o_kc:o_ks],
                           w_in[:, o_cf:o_dq], w_in[:, o_dg:]], axis=1)
    n_in, tn = 5760, 640
    sc = HEAD_DIM ** -0.5 * LOG2E
    prog = ([("copy", sc)] * 8 + [("copy", 1.0)] * 16 + [("rot128", sc)] * 8
            + [("rot128", 1.0)] * 2 + [("copy", 1.0)] * 2 + [("rot128", 1.0)] * 2 + [("copy", 1.0)] * 7)
    y, yf = _mm([(h, 0, D)], [_pad_cols(w_r, n_in).astype(BF16)], gain=norm_g, tn=tn, out_dtype=BF16,
                col_prog=prog, tables=tabs, f32_from=n_in // tn - 1, name="cd_in")
    misc = tn // LANES - 1
    r3 = lambda a: a.reshape(B, S, a.shape[-1])
    y3, yf3 = r3(y), r3(yf)
    fb = jnp.pad(forget_bias.astype(F32), (0, LANES - C_HEADS)).reshape(1, LANES)
    cum = _forget_cum(yf3, misc, fb)
    cumt = jnp.swapaxes(cum[:, :, :C_HEADS], 1, 2).reshape(B, C_HEADS, 1, S)
    hp = 2
    at = lambda first: (lambda h: first + h)
    o_c = _flash([(y3, hp * LANES, at(0))], [(y3, hp * LANES, at(C_HEADS // hp)), (y3, hp * LANES, at(C_HEADS))],
                 [[(0, g * LANES)] for g in range(hp)], [[(0, g * LANES)] for g in range(hp)],
                 [(1, g * LANES) for g in range(hp)], name="fox_attn", B=B, S=S, n_steps=C_HEADS // hp,
                 tq=1024, tk=1024, cum=cum, cumt=cumt)
    G = D_HEADS // D_KV_HEADS
    k_cmp = _compress(yf3, 0, pe_k, w1_k, w2_k, rope_tabs=cmp_tabs)
    v_cmp = _compress(yf3, kvw, pe_v, w1_v, w2_v)
    q_d = (y3, 24 // G)
    o_cmp, sel = _cmp_attn(y3, q_d[1], k_cmp, v_cmp)
    o_slc = _gqa_flash(q_d, (y3, 32), (y3, 34), name="nsa_slc_attn", B=B, S=S, Hk=D_KV_HEADS, G=G, sel=sel,
                       out_dtype=F32, tq=1024, tk=1024, rc=256)
    o_win = _gqa_flash(q_d, (y3, 36), (y3, 38), name="nsa_win_attn", B=B, S=S, Hk=D_KV_HEADS, G=G,
                       window=D_WINDOW, out_dtype=F32, tq=512, tk=512)
    o_d = _nsa_gate(yf, misc, C_HEADS, o_cmp.reshape(T, -1), o_slc.reshape(T, -1), o_win.reshape(T, -1))
    wo = w_o.astype(BF16)
    return _mm([(o_c.reshape(T, -1), 0, hc), (o_d, 0, D_HEADS * HEAD_DIM)], [wo[:hc], wo[hc:]],
               resid=h, tm=512, tn=D, name="cd_out")


def kernel(x, p, positions, ab_w_in, ab_sinks, ab_q_lat_norm, ab_kv_lat_norm, ab_w_uq, ab_w_ukv, ab_w_o,
           cd_w_in, cd_forget_bias, cd_cmp_pe_k, cd_cmp_w1_k, cd_cmp_w2_k, cd_cmp_pe_v, cd_cmp_w1_v,
           cd_cmp_w2_v, cd_w_o, mixer_norm, moe_norm, router_group_w, router_group_b, router_expert_w,
           router_expert_b, expert_w_gate, expert_w_up, expert_w_down, ple_proj, ple_gate_norm, ple_gate_w,
           final_norm):
    B, S, D = x.shape
    T = B * S
    depth = p.shape[0]
    tabs = _rope_tables(positions)
    nc = S // D_CMP_STRIDE
    end = np.minimum(np.arange(nc) * D_CMP_STRIDE + D_CMP_LEN - 1, S - 1)
    cmp_tabs = [t.reshape(B, nc, LANES) for t in _rope_tables(positions[:, end])[:2]]
    h = x.reshape(T, D)
    for i in range(depth):
        j = i // 2
        if i % 2 == 0:
            h = _layer_even(h, B, S, tabs, mixer_norm[i], ab_w_in[j], ab_sinks[j], ab_q_lat_norm[j],
                            ab_kv_lat_norm[j], ab_w_uq[j], ab_w_ukv[j], ab_w_o[j])
        else:
            h = _layer_odd(h, B, S, tabs, cmp_tabs, mixer_norm[i], cd_w_in[j], cd_forget_bias[j],
                           cd_cmp_pe_k[j], cd_cmp_w1_k[j], cd_cmp_w2_k[j], cd_cmp_pe_v[j], cd_cmp_w1_v[j],
                           cd_cmp_w2_v[j], cd_w_o[j])
        h = _moe(h, moe_norm[i], router_group_w[i], router_group_b[i], router_expert_w[i], router_expert_b[i],
                 i, expert_w_gate, expert_w_up, expert_w_down)
        h = _mm([(h, 0, D)], [ple_gate_w[i].astype(BF16)], gain=ple_gate_norm[i], tm=512, tn=D,
                ple=(h, p[i].reshape(T, -1), ple_proj[i].astype(BF16)),
                out_gain=final_norm if i == depth - 1 else None, name="ple")
    return h.reshape(B, S, D)
```

```python
import functools
import math

import numpy as np
import jax
import jax.numpy as jnp
from jax import lax
from jax.experimental import pallas as pl
from jax.experimental.pallas import tpu as pltpu

F32 = jnp.float32
BF16 = jnp.bfloat16

HEAD_DIM = 128
ROPE_THETA = 10000.0
NORM_EPS = 1e-6
NEG_INF = -1e30
TAKEN = -3e38
A_HEADS, A_KV_HEADS, A_WINDOW = 8, 2, 128
B_HEADS, B_Q_LORA, B_KV_LORA, B_NOPE, B_ROPE, B_V = 8, 512, 256, 128, 64, 128
C_HEADS = 8
D_HEADS, D_KV_HEADS = 8, 2
D_CMP_LEN, D_CMP_STRIDE, D_SLC_LEN, D_SLC_TOPN, D_WINDOW = 32, 16, 64, 8, 512
FORCE_BONUS = 1e4
N_GROUPS, EXPERTS_PER_GROUP, TOP_K = 4, 8, 2
N_EXPERTS = N_GROUPS * EXPERTS_PER_GROUP

LANES = 128
VMEM_LIMIT_BYTES = 56 * 1024 * 1024
MOE_ROWS = 256
DMA_UNROLL = 8
FLASH_ROW_CHUNK = 128
LOG2E = math.log2(math.e)


def _params(*sem):
    return pltpu.CompilerParams(dimension_semantics=sem, vmem_limit_bytes=VMEM_LIMIT_BYTES)


def _tile(n, pref):
    t = min(n, pref)
    while n % t:
        t -= 1
    return t


def _pack_bf16_pairs(y):
    n = y.shape[1] // 2
    bits = lambda a: lax.bitcast_convert_type(a.astype(BF16).astype(F32), jnp.uint32)
    return (bits(y[:, :n]) >> 16) | (bits(y[:, n:]) & jnp.uint32(0xFFFF0000))


def _unpack_bf16_pairs(w):
    return (lax.bitcast_convert_type(w << 16, F32), lax.bitcast_convert_type(w & jnp.uint32(0xFFFF0000), F32))


def _rot128(x, c, s):
    return x * c + pltpu.roll(x, 64, 1) * s


def _rot64(x, c, sa, sb):
    return x * c + pltpu.roll(x, 96, 1) * sa + pltpu.roll(x, 32, 1) * sb


def _mm_body(*refs, nx, has_gain, emit_xn, mode, resid_is_x, has_out_gain, col_prog, f32_from):
    it = iter(refs)
    x_refs = [next(it) for _ in range(nx)]
    g_ref = next(it) if has_gain else None
    w_refs = [next(it) for _ in range(nx)]
    r_ref = next(it) if mode in ("resid", "ple") and not resid_is_x else None
    p_ref = next(it) if mode == "ple" else None
    wp_ref = next(it) if mode == "ple" else None
    og_ref = next(it) if has_out_gain else None
    tab_refs = [next(it) for _ in range(5)] if col_prog is not None else None
    o_ref = next(it)
    xo_ref = next(it) if emit_xn else None
    of_ref = next(it) if f32_from is not None else None
    xn_ref = next(it) if has_gain else None

    if has_gain:
        @pl.when(pl.program_id(1) == 0)
        def _():
            x = x_refs[0][...].astype(F32)
            y = x * lax.rsqrt(jnp.mean(x * x, axis=-1, keepdims=True) + NORM_EPS) * g_ref[...]
            xn_ref[...] = y.astype(BF16)
            if emit_xn:
                xo_ref[...] = _pack_bf16_pairs(y)
        lhs = [xn_ref[...]]
    else:
        lhs = [x_ref[...].astype(BF16) for x_ref in x_refs]
    acc = None
    for a, w_ref in zip(lhs, w_refs):
        d = jnp.dot(a, w_ref[...], preferred_element_type=F32)
        acc = d if acc is None else acc + d
    if resid_is_x:
        r_ref = x_refs[0]
    if mode == "resid":
        acc = r_ref[...] + acc
    elif mode == "ple":
        pp = jnp.dot(p_ref[...].astype(BF16), wp_ref[...], preferred_element_type=F32)
        acc = r_ref[...] + pp * jax.nn.sigmoid(acc)
    if has_out_gain:
        acc = acc * lax.rsqrt(jnp.mean(acc * acc, axis=-1, keepdims=True) + NORM_EPS) * og_ref[...]
    if col_prog is None:
        o_ref[...] = acc.astype(o_ref.dtype)
        return
    j = pl.program_id(1)
    nb = o_ref.shape[1] // LANES
    tiles = {}
    for jj in range(len(col_prog) // nb):
        tiles.setdefault(tuple(col_prog[jj * nb:(jj + 1) * nb]), []).append(jj)
    for prog, jjs in tiles.items():
        @pl.when(functools.reduce(jnp.logical_or, [j == jj for jj in jjs]))
        def _():
            c128, s128, c64, s64a, s64b = tab_refs
            for b, (kind, scale) in enumerate(prog):
                x = acc[:, b * LANES:(b + 1) * LANES]
                if kind == "rot128":
                    x = _rot128(x, c128[...], s128[...])
                elif kind == "rot64":
                    x = _rot64(x, c64[...], s64a[...], s64b[...])
                if scale != 1.0:
                    x = x * scale
                o_ref[:, b * LANES:(b + 1) * LANES] = x.astype(o_ref.dtype)
    if f32_from is not None:
        @pl.when(j >= f32_from)
        def _():
            of_ref[...] = acc


def _mm(xs, ws, *, name, gain=None, out_dtype=F32, tm=1024, tn=512, resid=None, ple=None,
        emit_xn=False, out_gain=None, col_prog=None, tables=None, f32_from=None):
    M = xs[0][0].shape[0]
    N = ws[0].shape[1]
    tm, tn = _tile(M, tm), _tile(N, tn)
    nx = len(xs)
    has_gain = gain is not None
    mode = "ple" if ple is not None else ("resid" if resid is not None else "none")
    r_arr = ple[0] if mode == "ple" else resid
    resid_is_x = r_arr is xs[0][0] and tn == N == xs[0][2] and xs[0][1] == 0
    assert out_gain is None or tn == N
    args, in_specs = [], []
    for arr, cb, K in xs:
        args.append(arr)
        in_specs.append(pl.BlockSpec((tm, K), lambda i, j, cb=cb: (i, cb)))
    if has_gain:
        K0 = xs[0][2]
        args.append(gain.reshape(1, K0).astype(F32))
        in_specs.append(pl.BlockSpec((1, K0), lambda i, j: (0, 0)))
    for (arr, cb, K), w in zip(xs, ws):
        args.append(w)
        in_specs.append(pl.BlockSpec((K, tn), lambda i, j: (0, j)))
    if mode in ("resid", "ple") and not resid_is_x:
        args.append(r_arr)
        in_specs.append(pl.BlockSpec((tm, tn), lambda i, j: (i, j)))
    if mode == "ple":
        _, p, wp = ple
        args += [p, wp]
        in_specs += [pl.BlockSpec((tm, p.shape[1]), lambda i, j: (i, 0)),
                     pl.BlockSpec((p.shape[1], tn), lambda i, j: (0, j))]
    if out_gain is not None:
        args.append(out_gain.reshape(1, N).astype(F32))
        in_specs.append(pl.BlockSpec((1, N), lambda i, j: (0, 0)))
    if col_prog is not None:
        assert len(col_prog) * LANES == N
        args += list(tables)
        in_specs += [pl.BlockSpec((tm, LANES), lambda i, j: (i, 0))] * 5
    out_shape = [jax.ShapeDtypeStruct((M, N), out_dtype)]
    out_specs = [pl.BlockSpec((tm, tn), lambda i, j: (i, j))]
    if emit_xn:
        out_shape.append(jax.ShapeDtypeStruct((M, xs[0][2] // 2), jnp.uint32))
        out_specs.append(pl.BlockSpec((tm, xs[0][2] // 2), lambda i, j: (i, 0)))
    if f32_from is not None:
        out_shape.append(jax.ShapeDtypeStruct((M, N - f32_from * tn), F32))
        out_specs.append(pl.BlockSpec((tm, tn), lambda i, j: (i, jnp.maximum(j - f32_from, 0))))
    scratch = [pltpu.VMEM((tm, xs[0][2]), BF16)] if has_gain else []
    res = pl.pallas_call(
        functools.partial(_mm_body, nx=nx, has_gain=has_gain, emit_xn=emit_xn, mode=mode, resid_is_x=resid_is_x,
                          has_out_gain=out_gain is not None,
                          col_prog=None if col_prog is None else tuple(col_prog), f32_from=f32_from),
        out_shape=out_shape, grid=(M // tm, N // tn), in_specs=in_specs, out_specs=out_specs,
        scratch_shapes=scratch, compiler_params=_params("arbitrary", "arbitrary"), name=name,
    )(*args)
    return res if len(res) > 1 else res[0]


def _rope_tables(positions):
    def tables(dim):
        inv = 1.0 / (ROPE_THETA ** (jnp.arange(0, dim, 2, dtype=F32) / dim))
        ang = positions.astype(F32)[..., None] * inv
        return jnp.cos(ang), jnp.sin(ang)

    lead = positions.shape
    c, s = tables(HEAD_DIM)
    c128 = jnp.concatenate([c, c], -1)
    s128 = jnp.concatenate([-s, s], -1)
    c, s = tables(B_ROPE)
    z = jnp.zeros_like(c)
    c64 = jnp.concatenate([c, c, z, z], -1)
    s64a = jnp.concatenate([-s, z, z, z], -1)
    s64b = jnp.concatenate([z, s, z, z], -1)
    n = int(np.prod(lead))
    return [t.reshape(n, LANES) for t in (c128, s128, c64, s64a, s64b)]


def _pair_tables(S, tq, tk, window):
    qi_l, ki_l, fl_l, mk_l, masks, ids = [], [], [], [], [], {}
    for qi in range(S // tq):
        q0, q1 = qi * tq, (qi + 1) * tq - 1
        ks = []
        for ki in range(S // tk):
            k0, k1 = ki * tk, (ki + 1) * tk - 1
            if k0 > q1 or (window is not None and q0 - k1 >= window):
                continue
            full = k1 <= q0 and (window is None or q1 - k0 < window)
            mid = 0
            if not full:
                delta = q0 - k0
                if delta not in ids:
                    rel = np.arange(tq)[:, None] + delta - np.arange(tk)[None, :]
                    ok = (rel >= 0) if window is None else ((rel >= 0) & (rel < window))
                    masks.append(np.where(ok, 0.0, NEG_INF).astype(np.float32))
                    ids[delta] = len(masks)
                mid = ids[delta]
            ks.append((ki, mid))
        for n, (ki, mid) in enumerate(ks):
            qi_l.append(qi)
            ki_l.append(ki)
            fl_l.append((1 if n == 0 else 0) | (2 if n == len(ks) - 1 else 0))
            mk_l.append(mid)
    if not masks:
        masks.append(np.zeros((tq, tk), np.float32))
    tabs = [jnp.asarray(np.array(a, np.int32)) for a in (qi_l, ki_l, fl_l, mk_l)]
    return tabs, jnp.asarray(np.stack(masks))


def _flash_body(qi_t, ki_t, fl_t, mk_t, *refs, nq, nkv, q_src, k_src, v_src, tq, rc, causal_diag,
                has_sink, has_bias, has_sel):
    nc = len(q_src)
    it = iter(refs)
    q_refs = [next(it) for _ in range(nq)]
    kv_refs = [next(it) for _ in range(nkv)]
    mask_ref = next(it)
    sink_ref = next(it) if has_sink else None
    cum_ref = next(it) if has_bias else None
    cumt_ref = next(it) if has_bias else None
    sel_ref = next(it) if has_sel else None
    exp_ref = next(it) if has_sel else None
    o_ref = next(it)
    m_sc, acc_sc = next(it), next(it)
    cq_sc = next(it) if has_bias else None

    step_id = pl.program_id(2)
    fl, mk = fl_t[step_id], mk_t[step_id]

    def cat(refs_, src, rows=slice(None)):
        xs = [refs_[pi][0, rows, off:off + LANES] for pi, off in src]
        return xs[0] if len(xs) == 1 else jnp.concatenate(xs, axis=-1)

    @pl.when((fl & 1) != 0)
    def _init():
        m_sc[...] = jnp.full(m_sc.shape, NEG_INF, F32)
        acc_sc[...] = jnp.zeros(acc_sc.shape, F32)
        if has_bias:
            lane = lax.broadcasted_iota(jnp.int32, (tq, LANES), 1)
            for g in range(nc):
                head = pl.program_id(1) * nc + g
                cq_sc[g] = jnp.sum(jnp.where(lane == head, cum_ref[0], 0.0), axis=-1, keepdims=True)

    tk = kv_refs[0].shape[1]
    ones = jnp.ones((tk, LANES), BF16)

    def step(masked):
        ks = [cat(kv_refs, k_src[g]) for g in range(nc)]
        vs = [jnp.concatenate([kv_refs[pi][0, :, off:off + LANES], ones], axis=1) for pi, off in v_src]
        for r in range(tq // rc):
            rows = slice(r * rc, (r + 1) * rc)
            nk = min((r + 1) * rc, tk) if (masked and causal_diag) else tk
            reps = nk // LANES
            add = mask_ref[mk - 1, rows, :nk] if masked else None
            if has_sel:
                hidden = (jnp.dot(sel_ref[0, 0, rows, :], exp_ref[0, :, :nk], preferred_element_type=F32)
                          - 1.0) * (-NEG_INF)
                add = hidden if add is None else add + hidden
            for g in range(nc):
                q = cat(q_refs, q_src[g], rows)
                s = lax.dot_general(q, ks[g][:nk], (((1,), (1,)), ((), ())), preferred_element_type=F32)
                if has_bias:
                    s = s + (cq_sc[g, rows, :] - cumt_ref[0, g, :, :nk])
                if add is not None:
                    s = s + add
                m_prev = m_sc[g, rows, :]
                m_new = jnp.maximum(m_prev, jnp.max(s, axis=-1, keepdims=True))
                p = jnp.exp2(s - (jnp.concatenate([m_new] * reps, axis=1) if reps > 1 else m_new))
                alpha = jnp.exp2(m_prev - m_new)
                acc_sc[g, rows, :] = (jnp.concatenate([alpha, alpha], axis=1) * acc_sc[g, rows, :]
                                      + jnp.dot(p.astype(BF16), vs[g][:nk], preferred_element_type=F32))
                m_sc[g, rows, :] = m_new

    @pl.when(mk != 0)
    def _():
        step(True)

    @pl.when(mk == 0)
    def _():
        step(False)

    @pl.when((fl & 2) != 0)
    def _finish():
        for g in range(nc):
            m, acc, l = m_sc[g], acc_sc[g, :, :LANES], acc_sc[g, :, LANES:]
            if has_sink:
                sk = sink_ref[0, g:g + 1, 0:1] * LOG2E
                m_f = jnp.maximum(m, sk)
                w = jnp.exp2(m - m_f)
                l = l * w + jnp.exp2(sk - m_f)
                acc = acc * w
            o_ref[0, :, g * LANES:(g + 1) * LANES] = (acc / l).astype(o_ref.dtype)


def _flash(q_parts, kv_parts, q_src, k_src, v_src, *, name, B, S, n_steps, tq, tk, window=None, sinks=None,
           cum=None, cumt=None, sel=None, out_dtype=BF16, rc=FLASH_ROW_CHUNK):
    nc = len(q_src)
    tq, tk = _tile(S, tq), _tile(S, tk)
    tabs, masks = _pair_tables(S, tq, tk, window)
    npairs = int(tabs[0].shape[0])
    args, in_specs = [], []
    for arr, width, cf in q_parts:
        args.append(arr)
        in_specs.append(pl.BlockSpec((1, tq, width), lambda b, h, s, qt, kt, ft, mt, cf=cf: (b, qt[s], cf(h))))
    for arr, width, cf in kv_parts:
        args.append(arr)
        in_specs.append(pl.BlockSpec((1, tk, width), lambda b, h, s, qt, kt, ft, mt, cf=cf: (b, kt[s], cf(h))))
    args.append(masks)
    in_specs.append(pl.BlockSpec(masks.shape, lambda b, h, s, qt, kt, ft, mt: (0, 0, 0)))
    if sinks is not None:
        args.append(jnp.broadcast_to(sinks.astype(F32).reshape(n_steps, nc, 1), (n_steps, nc, LANES)))
        in_specs.append(pl.BlockSpec((1, nc, LANES), lambda b, h, s, qt, kt, ft, mt: (h, 0, 0)))
    if cum is not None:
        args += [cum, cumt]
        in_specs += [pl.BlockSpec((1, tq, LANES), lambda b, h, s, qt, kt, ft, mt: (b, qt[s], 0)),
                     pl.BlockSpec((1, nc, 1, tk), lambda b, h, s, qt, kt, ft, mt: (b, h, 0, kt[s]))]
    if sel is not None:
        per = tk // D_SLC_LEN
        e = np.zeros((S // tk, LANES, tk), np.float32)
        for ki in range(S // tk):
            e[ki, ki * per + np.arange(tk) // D_SLC_LEN, np.arange(tk)] = 1.0
        args += [sel, jnp.asarray(e, BF16)]
        in_specs += [pl.BlockSpec((1, 1, tq, LANES), lambda b, h, s, qt, kt, ft, mt: (b, h, qt[s], 0)),
                     pl.BlockSpec((1, LANES, tk), lambda b, h, s, qt, kt, ft, mt: (kt[s], 0, 0))]
    scratch = [pltpu.VMEM((nc, tq, LANES), F32), pltpu.VMEM((nc, tq, 2 * LANES), F32)]
    if cum is not None:
        scratch.append(pltpu.VMEM((nc, tq, 1), F32))
    body = functools.partial(_flash_body, nq=len(q_parts), nkv=len(kv_parts), q_src=q_src, k_src=k_src,
                             v_src=v_src, tq=tq, rc=_tile(tq, rc), causal_diag=window is None and tq == tk,
                             has_sink=sinks is not None,
                             has_bias=cum is not None, has_sel=sel is not None)
    return pl.pallas_call(
        body, out_shape=jax.ShapeDtypeStruct((B, S, n_steps * nc * LANES), out_dtype),
        grid_spec=pltpu.PrefetchScalarGridSpec(
            num_scalar_prefetch=4, grid=(B, n_steps, npairs), in_specs=in_specs,
            out_specs=pl.BlockSpec((1, tq, nc * LANES), lambda b, h, s, qt, kt, ft, mt: (b, qt[s], h)),
            scratch_shapes=scratch),
        compiler_params=_params("arbitrary", "arbitrary", "arbitrary"), name=name,
    )(*tabs, *args)


def _gqa_flash(q, k, v, *, name, B, S, Hk, G, **kw):
    at = lambda first: (lambda h: first + h)
    return _flash([(q[0], G * LANES, at(q[1]))], [(k[0], LANES, at(k[1])), (v[0], LANES, at(v[1]))],
                  [[(0, g * LANES)] for g in range(G)], [[(0, 0)]] * G, [(1, 0)] * G,
                  name=name, B=B, S=S, n_steps=Hk, **kw)


def _cum_body(y_ref, b_ref, tri_ref, o_ref, carry):
    @pl.when(pl.program_id(1) == 0)
    def _():
        carry[...] = jnp.zeros(carry.shape, F32)
    x = y_ref[0] + b_ref[...]
    logf = jnp.minimum(x, 0.0) - jnp.log1p(jnp.exp(-jnp.abs(x)))
    cum = jnp.dot(tri_ref[...], logf, preferred_element_type=F32, precision=lax.Precision.HIGHEST) + carry[...]
    o_ref[0] = cum * LOG2E
    carry[...] = cum[-1:, :]


def _forget_cum(y3, col_block, bias_row, *, ts=512):
    B, S, _ = y3.shape
    ts = _tile(S, ts)
    tri = jnp.asarray(np.tril(np.ones((ts, ts), np.float32)))
    return pl.pallas_call(
        _cum_body, out_shape=jax.ShapeDtypeStruct((B, S, LANES), F32), grid=(B, S // ts),
        in_specs=[pl.BlockSpec((1, ts, LANES), lambda b, s: (b, s, col_block)),
                  pl.BlockSpec((1, LANES), lambda b, s: (0, 0)),
                  pl.BlockSpec((ts, ts), lambda b, s: (0, 0))],
        out_specs=pl.BlockSpec((1, ts, LANES), lambda b, s: (b, s, 0)),
        scratch_shapes=[pltpu.VMEM((1, LANES), F32)],
        compiler_params=_params("arbitrary", "arbitrary"), name="forget_cum",
    )(y3, bias_row, tri)


def _compress_body(*refs, rope, nc, width, col0):
    if rope:
        z_ref, pe_ref, w1_ref, w2_ref, c_ref, s_ref, o_ref = refs
    else:
        z_ref, pe_ref, w1_ref, w2_ref, o_ref = refs
    half = D_CMP_LEN // 2
    for hk in range(D_KV_HEADS):
        u = jnp.zeros((nc, w1_ref.shape[1]), F32)
        v = jnp.zeros((nc, w1_ref.shape[1]), F32)
        for l in range(half):
            first = l * width + col0 + hk * HEAD_DIM
            z = z_ref[0, :, first:first + HEAD_DIM]
            zu = (z + pe_ref[l:l + 1, :]).astype(BF16)
            zv = (z + pe_ref[half + l:half + l + 1, :]).astype(BF16)
            u = u + jnp.dot(zu, w1_ref[l * HEAD_DIM:(l + 1) * HEAD_DIM, :], preferred_element_type=F32)
            v = v + jnp.dot(zv, w1_ref[(half + l) * HEAD_DIM:(half + l + 1) * HEAD_DIM, :],
                            preferred_element_type=F32)
        pre = u + pltpu.roll(v, nc - 1, 0)
        hid = jax.nn.gelu(pre, approximate=True)
        out = jnp.dot(hid.astype(BF16), w2_ref[...], preferred_element_type=F32)
        if rope:
            out = _rot128(out, c_ref[0], s_ref[0])
        o_ref[0, hk] = out.astype(o_ref.dtype)


def _compress(z, col0, pe, w1, w2, rope_tabs=None):
    B, S, W = z.shape
    nc = S // D_CMP_STRIDE
    zc = z.reshape(B, nc, D_CMP_STRIDE * W)
    args = [zc, pe.astype(F32), w1.astype(BF16), w2.astype(BF16)]
    in_specs = [pl.BlockSpec((1, nc, D_CMP_STRIDE * W), lambda b: (b, 0, 0)),
                pl.BlockSpec(pe.shape, lambda b: (0, 0)),
                pl.BlockSpec(w1.shape, lambda b: (0, 0)),
                pl.BlockSpec(w2.shape, lambda b: (0, 0))]
    if rope_tabs is not None:
        args += list(rope_tabs)
        in_specs += [pl.BlockSpec((1, nc, LANES), lambda b: (b, 0, 0))] * 2
    return pl.pallas_call(
        functools.partial(_compress_body, rope=rope_tabs is not None, nc=nc, width=W, col0=col0),
        out_shape=jax.ShapeDtypeStruct((B, D_KV_HEADS, nc, HEAD_DIM), BF16), grid=(B,),
        in_specs=in_specs, out_specs=pl.BlockSpec((1, D_KV_HEADS, nc, HEAD_DIM), lambda b: (b, 0, 0, 0)),
        compiler_params=_params("arbitrary"), name="nsa_compress",
    )(*args)


def _cmp_attn_body(q_ref, k_ref, v_ref, ov_ref, o_ref, sel_ref, *, G, tq, nc, n_cmp, n_slc, topn):
    qi = pl.program_id(2)
    t = qi * tq + lax.broadcasted_iota(jnp.int32, (tq, nc), 0)
    c = lax.broadcasted_iota(jnp.int32, (tq, nc), 1)
    valid = (c * D_CMP_STRIDE + (D_CMP_LEN - 1) <= t) & (c < n_cmp)
    k = k_ref[0, 0]
    v = v_ref[0, 0]
    psum = jnp.zeros((tq, nc), F32)
    for g in range(G):
        q = q_ref[0, :, g * LANES:(g + 1) * LANES]
        s = lax.dot_general(q, k, (((1,), (1,)), ((), ())), preferred_element_type=F32)
        s = jnp.where(valid, s, NEG_INF)
        e = jnp.where(valid, jnp.exp2(s - jnp.max(s, axis=-1, keepdims=True)), 0.0)
        p = e / jnp.maximum(jnp.sum(e, axis=-1, keepdims=True), jnp.finfo(F32).tiny)
        o_ref[0, :, g * LANES:(g + 1) * LANES] = jnp.dot(
            p.astype(BF16), v, preferred_element_type=F32).astype(o_ref.dtype)
        psum = psum + p
    imp = lax.dot_general(ov_ref[...], psum, (((1,), (1,)), ((), ())), preferred_element_type=F32,
                          precision=lax.Precision.HIGHEST)
    blk = lax.broadcasted_iota(jnp.int32, (LANES, tq), 0)
    tcol = qi * tq + lax.broadcasted_iota(jnp.int32, (LANES, tq), 1)
    cur = jnp.right_shift(tcol, int(math.log2(D_SLC_LEN)))
    forced = (blk == 0) | (blk == cur) | (blk == cur - 1)
    imp = jnp.where(blk * D_SLC_LEN > tcol, NEG_INF, imp + jnp.where(forced, FORCE_BONUS, 0.0))
    imp = jnp.where(blk >= n_slc, TAKEN, imp)
    chosen = jnp.zeros((LANES, tq), F32)
    blk_f = blk.astype(F32)
    for _ in range(topn):
        mx = jnp.max(imp, axis=0, keepdims=True)
        idx = jnp.min(jnp.where(imp == mx, blk_f, float(LANES)), axis=0, keepdims=True)
        hit = blk_f == idx
        chosen = jnp.where(hit, 1.0, chosen)
        imp = jnp.where(hit, TAKEN, imp)
    sel_ref[0, 0] = chosen.T.astype(sel_ref.dtype)


def _cmp_attn(q, q_first, k_cmp, v_cmp, *, tq=256):
    B, S, _ = q.shape
    Hk, G = D_KV_HEADS, D_HEADS // D_KV_HEADS
    nc = S // D_CMP_STRIDE
    n_cmp = (S - D_CMP_LEN) // D_CMP_STRIDE + 1
    n_slc = S // D_SLC_LEN
    tq = _tile(S, tq)
    c0 = np.arange(nc) * D_CMP_STRIDE
    s0 = np.arange(LANES) * D_SLC_LEN
    ov = ((c0[:, None] < (s0 + D_SLC_LEN)[None, :]) & ((c0 + D_CMP_LEN)[:, None] > s0[None, :])
          & (np.arange(nc) < n_cmp)[:, None] & (np.arange(LANES) < n_slc)[None, :]).astype(np.float32)
    body = functools.partial(_cmp_attn_body, G=G, tq=tq, nc=nc, n_cmp=n_cmp, n_slc=n_slc,
                             topn=min(D_SLC_TOPN, n_slc))
    return pl.pallas_call(
        body,
        out_shape=[jax.ShapeDtypeStruct((B, S, Hk * G * LANES), F32),
                   jax.ShapeDtypeStruct((B, Hk, S, LANES), BF16)],
        grid=(B, Hk, S // tq),
        in_specs=[pl.BlockSpec((1, tq, G * LANES), lambda b, h, i: (b, i, q_first + h)),
                  pl.BlockSpec((1, 1, nc, LANES), lambda b, h, i: (b, h, 0, 0)),
                  pl.BlockSpec((1, 1, nc, LANES), lambda b, h, i: (b, h, 0, 0)),
                  pl.BlockSpec((LANES, nc), lambda b, h, i: (0, 0))],
        out_specs=[pl.BlockSpec((1, tq, G * LANES), lambda b, h, i: (b, i, h)),
                   pl.BlockSpec((1, 1, tq, LANES), lambda b, h, i: (b, h, i, 0))],
        compiler_params=_params("arbitrary", "arbitrary", "arbitrary"), name="nsa_cmp_attn",
    )(q, k_cmp, v_cmp, jnp.asarray(ov.T))


def _gate_body(y_ref, a_ref, b_ref, c_ref, o_ref, *, lane0):
    g = jax.nn.sigmoid(y_ref[...])
    for h in range(D_HEADS):
        cols = slice(h * LANES, (h + 1) * LANES)
        ga = g[:, lane0 + h:lane0 + h + 1]
        gb = g[:, lane0 + D_HEADS + h:lane0 + D_HEADS + h + 1]
        gc = g[:, lane0 + 2 * D_HEADS + h:lane0 + 2 * D_HEADS + h + 1]
        o_ref[:, cols] = (ga * a_ref[:, cols] + gb * b_ref[:, cols] + gc * c_ref[:, cols]).astype(o_ref.dtype)


def _nsa_gate(y, col_block, lane0, o_cmp, o_slc, o_win, *, tm=512):
    M = y.shape[0]
    W = o_cmp.shape[1]
    tm = _tile(M, tm)
    row = lambda i: (i, 0)
    return pl.pallas_call(
        functools.partial(_gate_body, lane0=lane0), out_shape=jax.ShapeDtypeStruct((M, W), BF16),
        grid=(M // tm,),
        in_specs=[pl.BlockSpec((tm, LANES), lambda i: (i, col_block))] + [pl.BlockSpec((tm, W), row)] * 3,
        out_specs=pl.BlockSpec((tm, W), row), compiler_params=_params("arbitrary"), name="nsa_gate",
    )(y, o_cmp, o_slc, o_win)


def _route_body(lg_ref, b_ref, tri_ref, e_ref, w_ref, pos_ref, cnt_ref, carry, *, tm):
    @pl.when(pl.program_id(0) == 0)
    def _():
        carry[...] = jnp.zeros(carry.shape, F32)
    lane = lax.broadcasted_iota(jnp.int32, (tm, LANES), 1)
    logits = lg_ref[...] + b_ref[...]
    gl = jnp.where(lane < N_GROUPS, logits, -jnp.inf)
    gmax = jnp.max(gl, axis=-1, keepdims=True)
    g_val = 1.0 / jnp.sum(jnp.exp(gl - gmax), axis=-1, keepdims=True)
    g_idx = jnp.min(jnp.where(gl == gmax, lane, LANES), axis=-1, keepdims=True)
    lo = N_GROUPS + EXPERTS_PER_GROUP * g_idx
    el = jnp.where((lane >= lo) & (lane < lo + EXPERTS_PER_GROUP), logits, -jnp.inf)
    e1 = jnp.max(el, axis=-1, keepdims=True)
    i1 = jnp.min(jnp.where(el == e1, lane, LANES), axis=-1, keepdims=True)
    el2 = jnp.where(lane == i1, -jnp.inf, el)
    e2 = jnp.max(el2, axis=-1, keepdims=True)
    i2 = jnp.min(jnp.where(el2 == e2, lane, LANES), axis=-1, keepdims=True)
    r = jnp.exp(e2 - e1)
    w1 = g_val / (1.0 + r)
    w2 = w1 * r
    x1, x2 = i1 - N_GROUPS, i2 - N_GROUPS
    e_ref[...] = jnp.where(lane == 0, x1, jnp.where(lane == 1, x2, 0))
    w_ref[...] = jnp.where(lane == 0, w1, jnp.where(lane == 1, w2, 0.0))
    hot1 = lane == x1
    hot2 = lane == x2
    both = jnp.where(hot1 | hot2, 1.0, 0.0)
    before = jnp.dot(tri_ref[...], both.astype(BF16), preferred_element_type=F32) + carry[...]
    p1 = jnp.sum(jnp.where(hot1, before, 0.0), axis=-1, keepdims=True)
    p2 = jnp.sum(jnp.where(hot2, before, 0.0), axis=-1, keepdims=True)
    pos_ref[...] = jnp.where(lane == 0, p1, jnp.where(lane == 1, p2, 0.0)).astype(jnp.int32)
    carry[...] = carry[...] + jnp.sum(both, axis=0, keepdims=True)
    cnt_ref[...] = carry[...].astype(jnp.int32)


def _route(logits, bias_row, *, tm=512):
    T = logits.shape[0]
    tm = _tile(T, tm)
    tri = jnp.asarray(np.tril(np.ones((tm, tm), np.float32), -1), BF16)
    row = lambda i: (i, 0)
    fixed = lambda i: (0, 0)
    return pl.pallas_call(
        functools.partial(_route_body, tm=tm),
        out_shape=[jax.ShapeDtypeStruct((T, LANES), jnp.int32), jax.ShapeDtypeStruct((T, LANES), F32),
                   jax.ShapeDtypeStruct((T, LANES), jnp.int32), jax.ShapeDtypeStruct((1, LANES), jnp.int32)],
        grid=(T // tm,),
        in_specs=[pl.BlockSpec((tm, LANES), row), pl.BlockSpec((1, LANES), fixed), pl.BlockSpec((tm, tm), fixed)],
        out_specs=[pl.BlockSpec((tm, LANES), row)] * 3 + [pl.BlockSpec((1, LANES), fixed)],
        scratch_shapes=[pltpu.VMEM((1, LANES), F32)],
        compiler_params=_params("arbitrary"), name="moe_route",
    )(logits, bias_row, tri)


def _dispatch_body(d_ref, ends_ref, pad_ref, nu_ref, x_ref, out_ref, zero_sc, sem, zsem, *, tm, nblk):
    base = pl.program_id(0) * tm

    @pl.when(pl.program_id(0) == 0)
    def _():
        zero_sc[...] = jnp.zeros(zero_sc.shape, zero_sc.dtype)

        def zero_block(first_row):
            return pltpu.make_async_copy(zero_sc, out_ref.at[pl.ds(pl.multiple_of(first_row, MOE_ROWS), MOE_ROWS)],
                                         zsem)

        for wait in (False, True):
            for e in range(N_EXPERTS):
                for live, first_row in ((pad_ref[e] > 0, ends_ref[e] - MOE_ROWS),
                                        (nu_ref[0] + e < nblk, (nu_ref[0] + e) * MOE_ROWS)):
                    @pl.when(live)
                    def _():
                        zero_block(first_row).wait() if wait else zero_block(first_row).start()

    def issue(r, c):
        for k in range(TOP_K):
            row = d_ref[(base + r) * TOP_K + k]
            pltpu.make_async_copy(x_ref.at[pl.ds(r, 1)], out_ref.at[pl.ds(row, 1)], sem).start(priority=k)
        return c

    lax.fori_loop(0, tm, issue, 0, unroll=DMA_UNROLL)
    for _ in range(TOP_K):
        pltpu.make_async_copy(x_ref, out_ref.at[pl.ds(0, tm)], sem).wait()


def _dispatch(xn, dest, ends, padded, n_used, n_rows, *, tm=512):
    T, D = xn.shape
    tm = _tile(T, tm)
    return pl.pallas_call(
        functools.partial(_dispatch_body, tm=tm, nblk=n_rows // MOE_ROWS),
        out_shape=jax.ShapeDtypeStruct((n_rows, D), xn.dtype),
        grid_spec=pltpu.PrefetchScalarGridSpec(
            num_scalar_prefetch=4, grid=(T // tm,),
            in_specs=[pl.BlockSpec((tm, D), lambda i, *_: (i, 0))],
            out_specs=pl.BlockSpec(memory_space=pl.ANY),
            scratch_shapes=[pltpu.VMEM((MOE_ROWS, D), xn.dtype), pltpu.SemaphoreType.DMA(()),
                            pltpu.SemaphoreType.DMA(())]),
        compiler_params=pltpu.CompilerParams(dimension_semantics=("arbitrary",), has_side_effects=True,
                                             vmem_limit_bytes=VMEM_LIMIT_BYTES, disable_bounds_checks=True),
        name="moe_dispatch",
    )(dest, ends, padded, n_used, xn)


def _expert_body(nb_ref, first_ref, nu_ref, xb_ref, wg_ref, wu_ref, wd_ref, yb_ref, wg_sc, wu_sc, wd_sc, xbuf,
                 obuf, sem_in, sem_out, state, *, nblk):
    e = pl.program_id(0)
    last = pl.num_programs(0) - 1
    nb, first = nb_ref[e], first_ref[e]

    @pl.when(e == 0)
    def _():
        for i in range(3):
            state[i] = 0

    def rows(first_block, b):
        return pl.ds(pl.multiple_of((first_block + b) * MOE_ROWS, MOE_ROWS), MOE_ROWS)

    def fetch(first_block, b, slot):
        return pltpu.make_async_copy(xb_ref.at[rows(first_block, b)], xbuf.at[slot], sem_in.at[slot])

    def flush(b, slot):
        return pltpu.make_async_copy(obuf.at[slot], yb_ref.at[rows(first, b)], sem_out.at[slot])

    def drain(slot):
        @pl.when(state[slot] != 0)
        def _():
            pltpu.make_async_copy(obuf.at[slot], yb_ref.at[pl.ds(0, MOE_ROWS)], sem_out.at[slot]).wait()
            state[slot] = 0

    @pl.when(nb > 0)
    def _():
        @pl.when(state[2] == 0)
        def _():
            fetch(first, 0, 0).start()

        state[2] = 0
        wg_sc[...] = wg_ref[0, 0].astype(BF16)
        wu_sc[...] = wu_ref[0, 0].astype(BF16)
        wd_sc[...] = wd_ref[0, 0].astype(BF16)

        def block(b, c):
            slot = lax.rem(b, 2)

            @pl.when(b + 1 < nb)
            def _():
                fetch(first, b + 1, 1 - slot).start()

            fetch(first, b, slot).wait()
            drain(slot)
            x = jnp.concatenate(_unpack_bf16_pairs(xbuf[slot]), axis=1).astype(BF16)
            gate = jnp.dot(x, wg_sc[...], preferred_element_type=F32)
            up = jnp.dot(x, wu_sc[...], preferred_element_type=F32)
            hid = (gate * jax.nn.sigmoid(gate) * up).astype(BF16)
            obuf[slot] = _pack_bf16_pairs(jnp.dot(hid, wd_sc[...], preferred_element_type=F32))
            flush(b, slot).start()
            state[slot] = 1
            return c

        lax.fori_loop(0, nb, block, 0)
        nxt = jnp.minimum(e + 1, last)

        @pl.when((e < last) & (nb_ref[nxt] > 0))
        def _():
            fetch(first_ref[nxt], 0, 0).start()
            state[2] = 1

    @pl.when(e == last)
    def _():
        drain(0)
        drain(1)
        obuf[0] = jnp.zeros(obuf.shape[1:], obuf.dtype)

        def tail(t):
            return pltpu.make_async_copy(obuf.at[0], yb_ref.at[pl.ds(pl.multiple_of((nu_ref[0] + t) * MOE_ROWS,
                                                                                   MOE_ROWS), MOE_ROWS)], sem_out.at[0])

        for wait in (False, True):
            for t in range(N_EXPERTS):
                @pl.when(nu_ref[0] + t < nblk)
                def _():
                    tail(t).wait() if wait else tail(t).start()


def _experts(xb, n_blocks, first_block, n_used, layer, w_gate, w_up, w_down):
    D, Hd = w_gate.shape[2], w_gate.shape[3]
    nblk = xb.shape[0] // MOE_ROWS
    wspec = lambda shape: pl.BlockSpec((1, 1) + shape, lambda e, *_: (layer, e, 0, 0))
    return pl.pallas_call(
        functools.partial(_expert_body, nblk=nblk), out_shape=jax.ShapeDtypeStruct(xb.shape, xb.dtype),
        grid_spec=pltpu.PrefetchScalarGridSpec(
            num_scalar_prefetch=3, grid=(w_gate.shape[1],),
            in_specs=[pl.BlockSpec(memory_space=pl.ANY), wspec((D, Hd)), wspec((D, Hd)), wspec((Hd, D))],
            out_specs=pl.BlockSpec(memory_space=pl.ANY),
            scratch_shapes=[pltpu.VMEM((D, Hd), BF16), pltpu.VMEM((D, Hd), BF16), pltpu.VMEM((Hd, D), BF16),
                            pltpu.VMEM((2, MOE_ROWS, D // 2), xb.dtype), pltpu.VMEM((2, MOE_ROWS, D // 2), xb.dtype),
                            pltpu.SemaphoreType.DMA((2,)), pltpu.SemaphoreType.DMA((2,)),
                            pltpu.SMEM((3,), jnp.int32)]),
        compiler_params=pltpu.CompilerParams(dimension_semantics=("arbitrary",), has_side_effects=True,
                                             vmem_limit_bytes=VMEM_LIMIT_BYTES),
        name="moe_experts",
    )(n_blocks, first_block, n_used, xb, w_gate, w_up, w_down)


def _collect_body(d_ref, h_ref, w_ref, yb_ref, o_ref, buf_a, buf_b, sem, *, tm, nsteps):
    i = pl.program_id(0)
    slot = lax.rem(i, 2)

    def fetch(step, slot_):
        base = step * tm

        def issue(r, c):
            for k, buf in enumerate((buf_a, buf_b)):
                row = d_ref[(base + r) * TOP_K + k]
                pltpu.make_async_copy(yb_ref.at[pl.ds(row, 1)], buf.at[slot_, pl.ds(r, 1)],
                                      sem.at[slot_]).start(priority=k)
            return c

        lax.fori_loop(0, tm, issue, 0, unroll=DMA_UNROLL)

    @pl.when(i == 0)
    def _():
        fetch(0, 0)

    @pl.when(i + 1 < nsteps)
    def _():
        fetch(i + 1, 1 - slot)

    for buf in (buf_a, buf_b):
        pltpu.make_async_copy(yb_ref.at[pl.ds(0, tm)], buf.at[slot], sem.at[slot]).wait()
    w = w_ref[...]
    half = h_ref.shape[1] // 2
    for part, ya, yb in zip((slice(0, half), slice(half, None)), _unpack_bf16_pairs(buf_a[slot]),
                            _unpack_bf16_pairs(buf_b[slot])):
        o_ref[:, part] = h_ref[:, part] + w[:, 0:1] * ya + w[:, 1:2] * yb


def _collect(h, yb, dest, wts, *, tm=256):
    T, D = h.shape
    tm = _tile(T, tm)
    nsteps = T // tm
    return pl.pallas_call(
        functools.partial(_collect_body, tm=tm, nsteps=nsteps),
        out_shape=jax.ShapeDtypeStruct((T, D), F32),
        grid_spec=pltpu.PrefetchScalarGridSpec(
            num_scalar_prefetch=1, grid=(nsteps,),
            in_specs=[pl.BlockSpec((tm, D), lambda i, d: (i, 0)), pl.BlockSpec((tm, LANES), lambda i, d: (i, 0)),
                      pl.BlockSpec(memory_space=pl.ANY)],
            out_specs=pl.BlockSpec((tm, D), lambda i, d: (i, 0)),
            scratch_shapes=[pltpu.VMEM((2, tm, D // 2), yb.dtype), pltpu.VMEM((2, tm, D // 2), yb.dtype),
                            pltpu.SemaphoreType.DMA((2,))]),
        compiler_params=pltpu.CompilerParams(dimension_semantics=("arbitrary",), disable_bounds_checks=True,
                                             vmem_limit_bytes=VMEM_LIMIT_BYTES),
        name="moe_collect",
    )(dest, h, wts, yb)


def _moe(h, norm_g, w_group, b_group, w_expert, b_expert, layer, w_gate, w_up, w_down):
    T, D = h.shape
    pad = LANES - N_GROUPS - N_EXPERTS
    w_r = jnp.concatenate([w_group, w_expert, jnp.zeros((D, pad), F32)], axis=1).astype(BF16)
    b_r = jnp.concatenate([b_group, b_expert, jnp.zeros((pad,), F32)]).astype(F32).reshape(1, LANES)
    logits, xn = _mm([(h, 0, D)], [w_r], gain=norm_g, emit_xn=True, tm=512, tn=LANES, name="moe_router")
    eid, wts, pos, cnt = _route(logits, b_r)
    counts = cnt[0, :N_EXPERTS]
    padded = (counts + MOE_ROWS - 1) // MOE_ROWS * MOE_ROWS
    ends = jnp.cumsum(padded)
    offs = ends - padded
    dest = (offs[eid[:, :TOP_K]] + pos[:, :TOP_K]).reshape(T * TOP_K).astype(jnp.int32)
    P = T * TOP_K + N_EXPERTS * MOE_ROWS
    n_used = (ends[-1:] // MOE_ROWS).astype(jnp.int32)
    xb = _dispatch(xn, dest, ends.astype(jnp.int32), padded.astype(jnp.int32), n_used, P)
    yb = _experts(xb, (padded // MOE_ROWS).astype(jnp.int32), (offs // MOE_ROWS).astype(jnp.int32), n_used, layer,
                  w_gate, w_up, w_down)
    return _collect(h, yb, dest, wts)


def _pad_cols(w, n):
    return jnp.pad(w, ((0, 0), (0, n - w.shape[1])))


def _layer_even(h, B, S, tabs, norm_g, w_in, sinks, q_lat_norm, kv_lat_norm, w_uq, w_ukv, w_o):
    T, D = h.shape
    n_in, tn = 2560, 512
    sa = HEAD_DIM ** -0.5 * LOG2E
    prog = ([("rot128", sa)] * 8 + [("rot128", 1.0)] * 2 + [("copy", 1.0)] * 8 + [("rot64", 1.0), ("copy", 1.0)])
    y, yf = _mm([(h, 0, D)], [_pad_cols(w_in.astype(BF16), n_in)], gain=norm_g, tn=tn, out_dtype=BF16,
                col_prog=prog, tables=tabs, f32_from=12 * LANES // tn, name="ab_in")
    wq = w_uq.reshape(B_Q_LORA, B_HEADS, B_NOPE + B_ROPE)
    wq_n = wq[:, :, :B_NOPE].reshape(B_Q_LORA, B_HEADS * B_NOPE)
    wq_r = jnp.pad(wq[:, :, B_NOPE:], ((0, 0), (0, 0), (0, LANES - B_ROPE))).reshape(B_Q_LORA, B_HEADS * LANES)
    sb = (B_NOPE + B_ROPE) ** -0.5 * LOG2E
    qq = _mm([(yf, 0, B_Q_LORA)], [jnp.concatenate([wq_n, wq_r], 1).astype(BF16)], gain=q_lat_norm, tn=512,
             out_dtype=BF16, col_prog=[("copy", sb)] * 8 + [("rot64", sb)] * 8, tables=tabs, name="mla_uq")
    kv = _mm([(yf, B_Q_LORA // B_KV_LORA, B_KV_LORA)], [w_ukv.astype(BF16)], gain=kv_lat_norm, out_dtype=BF16,
             tn=512, name="mla_ukv")
    r3 = lambda a: a.reshape(B, S, a.shape[-1])
    y3 = r3(y)
    o_a = _gqa_flash((y3, 0), (y3, 8), (y3, 10), name="swa_attn", B=B, S=S, Hk=A_KV_HEADS,
                     G=A_HEADS // A_KV_HEADS, window=A_WINDOW, sinks=sinks, tq=256, tk=256)
    hp = 2
    head = lambda h: h
    o_b = _flash([(r3(qq), hp * LANES, head), (r3(qq), hp * LANES, lambda h: B_HEADS // hp + h)],
                 [(r3(kv), hp * 2 * LANES, head), (y3, LANES, lambda h: 18)],
                 [[(0, g * LANES), (1, g * LANES)] for g in range(hp)],
                 [[(0, 2 * g * LANES), (1, 0)] for g in range(hp)], [(0, (2 * g + 1) * LANES) for g in range(hp)],
                 name="mla_attn", B=B, S=S, n_steps=B_HEADS // hp, tq=1024, tk=1024, rc=256)
    wo = w_o.astype(BF16)
    na = A_HEADS * HEAD_DIM
    return _mm([(o_a.reshape(T, -1), 0, na), (o_b.reshape(T, -1), 0, B_HEADS * B_V)], [wo[:na], wo[na:]],
               resid=h, tm=512, tn=D, name="ab_out")


def _layer_odd(h, B, S, tabs, cmp_tabs, norm_g, w_in, forget_bias, pe_k, w1_k, w2_k, pe_v, w1_v, w2_v, w_o):
    T, D = h.shape
    hc, kvw = C_HEADS * HEAD_DIM, D_KV_HEADS * HEAD_DIM
    o_cf = 3 * hc
    o_dq = o_cf + C_HEADS
    o_kc = o_dq + D_HEADS * HEAD_DIM
    o_ks = o_kc + 2 * kvw
    o_dg = o_ks + 4 * kvw
    n_in, tn = 5760, 640
    w16 = w_in.astype(BF16)
    w_r = jnp.concatenate([w16[:, :o_cf], w16[:, o_dq:o_kc], w16[:, o_ks:o_dg], w16[:, o_kc:o_ks], w16[:, o_cf:o_dq],
                           w16[:, o_dg:], jnp.zeros((D, n_in - w_in.shape[1]), BF16)], axis=1)
    sc = HEAD_DIM ** -0.5 * LOG2E
    prog = ([("copy", sc)] * 8 + [("copy", 1.0)] * 16 + [("rot128", sc)] * 8
            + [("rot128", 1.0)] * 2 + [("copy", 1.0)] * 2 + [("rot128", 1.0)] * 2 + [("copy", 1.0)] * 7)
    y, yf = _mm([(h, 0, D)], [w_r], gain=norm_g, tn=tn, out_dtype=BF16,
                col_prog=prog, tables=tabs, f32_from=n_in // tn - 1, name="cd_in")
    misc = tn // LANES - 1
    r3 = lambda a: a.reshape(B, S, a.shape[-1])
    y3, yf3 = r3(y), r3(yf)
    fb = jnp.pad(forget_bias.astype(F32), (0, LANES - C_HEADS)).reshape(1, LANES)
    cum = _forget_cum(yf3, misc, fb)
    cumt = jnp.swapaxes(cum[:, :, :C_HEADS], 1, 2).reshape(B, C_HEADS, 1, S)
    hp = 2
    at = lambda first: (lambda h: first + h)
    o_c = _flash([(y3, hp * LANES, at(0))], [(y3, hp * LANES, at(C_HEADS // hp)), (y3, hp * LANES, at(C_HEADS))],
                 [[(0, g * LANES)] for g in range(hp)], [[(0, g * LANES)] for g in range(hp)],
                 [(1, g * LANES) for g in range(hp)], name="fox_attn", B=B, S=S, n_steps=C_HEADS // hp,
                 tq=1024, tk=1024, cum=cum, cumt=cumt)
    G = D_HEADS // D_KV_HEADS
    k_cmp = _compress(yf3, 0, pe_k, w1_k, w2_k, rope_tabs=cmp_tabs)
    v_cmp = _compress(yf3, kvw, pe_v, w1_v, w2_v)
    q_d = (y3, 24 // G)
    o_cmp, sel = _cmp_attn(y3, q_d[1], k_cmp, v_cmp)
    o_slc = _gqa_flash(q_d, (y3, 32), (y3, 34), name="nsa_slc_attn", B=B, S=S, Hk=D_KV_HEADS, G=G, sel=sel,
                       out_dtype=F32, tq=1024, tk=1024, rc=256)
    o_win = _gqa_flash(q_d, (y3, 36), (y3, 38), name="nsa_win_attn", B=B, S=S, Hk=D_KV_HEADS, G=G,
                       window=D_WINDOW, out_dtype=F32, tq=512, tk=512)
    o_d = _nsa_gate(yf, misc, C_HEADS, o_cmp.reshape(T, -1), o_slc.reshape(T, -1), o_win.reshape(T, -1))
    wo = w_o.astype(BF16)
    return _mm([(o_c.reshape(T, -1), 0, hc), (o_d, 0, D_HEADS * HEAD_DIM)], [wo[:hc], wo[hc:]],
               resid=h, tm=512, tn=D, name="cd_out")


def kernel(x, p, positions, ab_w_in, ab_sinks, ab_q_lat_norm, ab_kv_lat_norm, ab_w_uq, ab_w_ukv, ab_w_o,
           cd_w_in, cd_forget_bias, cd_cmp_pe_k, cd_cmp_w1_k, cd_cmp_w2_k, cd_cmp_pe_v, cd_cmp_w1_v,
           cd_cmp_w2_v, cd_w_o, mixer_norm, moe_norm, router_group_w, router_group_b, router_expert_w,
           router_expert_b, expert_w_gate, expert_w_up, expert_w_down, ple_proj, ple_gate_norm, ple_gate_w,
           final_norm):
    B, S, D = x.shape
    T = B * S
    depth = p.shape[0]
    tabs = _rope_tables(positions)
    nc = S // D_CMP_STRIDE
    end = np.minimum(np.arange(nc) * D_CMP_STRIDE + D_CMP_LEN - 1, S - 1)
    cmp_tabs = [t.reshape(B, nc, LANES) for t in _rope_tables(positions[:, end])[:2]]
    h = x.reshape(T, D)
    for i in range(depth):
        j = i // 2
        if i % 2 == 0:
            h = _layer_even(h, B, S, tabs, mixer_norm[i], ab_w_in[j], ab_sinks[j], ab_q_lat_norm[j],
                            ab_kv_lat_norm[j], ab_w_uq[j], ab_w_ukv[j], ab_w_o[j])
        else:
            h = _layer_odd(h, B, S, tabs, cmp_tabs, mixer_norm[i], cd_w_in[j], cd_forget_bias[j],
                           cd_cmp_pe_k[j], cd_cmp_w1_k[j], cd_cmp_w2_k[j], cd_cmp_pe_v[j], cd_cmp_w1_v[j],
                           cd_cmp_w2_v[j], cd_w_o[j])
        h = _moe(h, moe_norm[i], router_group_w[i], router_group_b[i], router_expert_w[i], router_expert_b[i],
                 i, expert_w_gate, expert_w_up, expert_w_down)
        h = _mm([(h, 0, D)], [ple_gate_w[i].astype(BF16)], gain=ple_gate_norm[i], tm=512, tn=D,
                ple=(h, p[i].reshape(T, -1), ple_proj[i].astype(BF16)),
                out_gain=final_norm if i == depth - 1 else None, name="ple")
    return h.reshape(B, S, D)
```

```python
import functools
import math

import numpy as np
import jax
import jax.numpy as jnp
from jax import lax
from jax.experimental import pallas as pl
from jax.experimental.pallas import tpu as pltpu

F32 = jnp.float32
BF16 = jnp.bfloat16

HEAD_DIM = 128
ROPE_THETA = 10000.0
NORM_EPS = 1e-6
NEG_INF = -1e30
TAKEN = -3e38
A_HEADS, A_KV_HEADS, A_WINDOW = 8, 2, 128
B_HEADS, B_Q_LORA, B_KV_LORA, B_NOPE, B_ROPE, B_V = 8, 512, 256, 128, 64, 128
C_HEADS = 8
D_HEADS, D_KV_HEADS = 8, 2
D_CMP_LEN, D_CMP_STRIDE, D_SLC_LEN, D_SLC_TOPN, D_WINDOW = 32, 16, 64, 8, 512
FORCE_BONUS = 1e4
N_GROUPS, EXPERTS_PER_GROUP, TOP_K = 4, 8, 2
N_EXPERTS = N_GROUPS * EXPERTS_PER_GROUP

LANES = 128
VMEM_LIMIT_BYTES = 56 * 1024 * 1024
MOE_ROWS = 256
DMA_UNROLL = 8
FLASH_ROW_CHUNK = 128
LOG2E = math.log2(math.e)


def _params(*sem):
    return pltpu.CompilerParams(dimension_semantics=sem, vmem_limit_bytes=VMEM_LIMIT_BYTES)


def _tile(n, pref):
    t = min(n, pref)
    while n % t:
        t -= 1
    return t


def _pack_bf16_pairs(y):
    n = y.shape[1] // 2
    bits = lambda a: lax.bitcast_convert_type(a.astype(BF16).astype(F32), jnp.uint32)
    return (bits(y[:, :n]) >> 16) | (bits(y[:, n:]) & jnp.uint32(0xFFFF0000))


def _unpack_bf16_pairs(w):
    return (lax.bitcast_convert_type(w << 16, F32), lax.bitcast_convert_type(w & jnp.uint32(0xFFFF0000), F32))


def _rot128(x, c, s):
    return x * c + pltpu.roll(x, 64, 1) * s


def _rot64(x, c, sa, sb):
    return x * c + pltpu.roll(x, 96, 1) * sa + pltpu.roll(x, 32, 1) * sb


def _tiles_store(ref, lead, words):
    rows, c = words.shape[0], words.shape[1] // LANES
    for j in range(c):
        ref[lead + (pl.ds(j, rows, stride=c), slice(None))] = words[:, j * LANES:(j + 1) * LANES]


def _tiles_load(ref, lead, rows, c):
    return jnp.concatenate([ref[lead + (pl.ds(j, rows, stride=c), slice(None))] for j in range(c)], axis=1)


def _mm_body(*refs, nx, has_gain, emit_xn, mode, resid_is_x, has_out_gain, col_prog, f32_from):
    it = iter(refs)
    x_refs = [next(it) for _ in range(nx)]
    g_ref = next(it) if has_gain else None
    w_refs = [next(it) for _ in range(nx)]
    r_ref = next(it) if mode in ("resid", "ple") and not resid_is_x else None
    p_ref = next(it) if mode == "ple" else None
    wp_ref = next(it) if mode == "ple" else None
    og_ref = next(it) if has_out_gain else None
    tab_refs = [next(it) for _ in range(5)] if col_prog is not None else None
    o_ref = next(it)
    xo_ref = next(it) if emit_xn else None
    of_ref = next(it) if f32_from is not None else None
    xn_ref = next(it) if has_gain else None

    if has_gain:
        @pl.when(pl.program_id(1) == 0)
        def _():
            x = x_refs[0][...].astype(F32)
            y = x * lax.rsqrt(jnp.mean(x * x, axis=-1, keepdims=True) + NORM_EPS) * g_ref[...]
            xn_ref[...] = y.astype(BF16)
            if emit_xn:
                _tiles_store(xo_ref, (), _pack_bf16_pairs(y))
        lhs = [xn_ref[...]]
    else:
        lhs = [x_ref[...].astype(BF16) for x_ref in x_refs]
    acc = None
    for a, w_ref in zip(lhs, w_refs):
        d = jnp.dot(a, w_ref[...], preferred_element_type=F32)
        acc = d if acc is None else acc + d
    if resid_is_x:
        r_ref = x_refs[0]
    if mode == "resid":
        acc = r_ref[...] + acc
    elif mode == "ple":
        pp = jnp.dot(p_ref[...].astype(BF16), wp_ref[...], preferred_element_type=F32)
        acc = r_ref[...] + pp * jax.nn.sigmoid(acc)
    if has_out_gain:
        acc = acc * lax.rsqrt(jnp.mean(acc * acc, axis=-1, keepdims=True) + NORM_EPS) * og_ref[...]
    if col_prog is None:
        o_ref[...] = acc.astype(o_ref.dtype)
        return
    j = pl.program_id(1)
    nb = o_ref.shape[1] // LANES
    tiles = {}
    for jj in range(len(col_prog) // nb):
        tiles.setdefault(tuple(col_prog[jj * nb:(jj + 1) * nb]), []).append(jj)
    for prog, jjs in tiles.items():
        @pl.when(functools.reduce(jnp.logical_or, [j == jj for jj in jjs]))
        def _():
            c128, s128, c64, s64a, s64b = tab_refs
            for b, (kind, scale) in enumerate(prog):
                x = acc[:, b * LANES:(b + 1) * LANES]
                if kind == "rot128":
                    x = _rot128(x, c128[...], s128[...])
                elif kind == "rot64":
                    x = _rot64(x, c64[...], s64a[...], s64b[...])
                if scale != 1.0:
                    x = x * scale
                o_ref[:, b * LANES:(b + 1) * LANES] = x.astype(o_ref.dtype)
    if f32_from is not None:
        @pl.when(j >= f32_from)
        def _():
            of_ref[...] = acc


def _mm(xs, ws, *, name, gain=None, out_dtype=F32, tm=1024, tn=512, resid=None, ple=None,
        emit_xn=False, out_gain=None, col_prog=None, tables=None, f32_from=None):
    M = xs[0][0].shape[0]
    N = ws[0].shape[1]
    tm, tn = _tile(M, tm), _tile(N, tn)
    nx = len(xs)
    has_gain = gain is not None
    mode = "ple" if ple is not None else ("resid" if resid is not None else "none")
    r_arr = ple[0] if mode == "ple" else resid
    resid_is_x = r_arr is xs[0][0] and tn == N == xs[0][2] and xs[0][1] == 0
    assert out_gain is None or tn == N
    args, in_specs = [], []
    for arr, cb, K in xs:
        args.append(arr)
        in_specs.append(pl.BlockSpec((tm, K), lambda i, j, cb=cb: (i, cb)))
    if has_gain:
        K0 = xs[0][2]
        args.append(gain.reshape(1, K0).astype(F32))
        in_specs.append(pl.BlockSpec((1, K0), lambda i, j: (0, 0)))
    for (arr, cb, K), w in zip(xs, ws):
        args.append(w)
        in_specs.append(pl.BlockSpec((K, tn), lambda i, j: (0, j)))
    if mode in ("resid", "ple") and not resid_is_x:
        args.append(r_arr)
        in_specs.append(pl.BlockSpec((tm, tn), lambda i, j: (i, j)))
    if mode == "ple":
        _, p, wp = ple
        args += [p, wp]
        in_specs += [pl.BlockSpec((tm, p.shape[1]), lambda i, j: (i, 0)),
                     pl.BlockSpec((p.shape[1], tn), lambda i, j: (0, j))]
    if out_gain is not None:
        args.append(out_gain.reshape(1, N).astype(F32))
        in_specs.append(pl.BlockSpec((1, N), lambda i, j: (0, 0)))
    if col_prog is not None:
        assert len(col_prog) * LANES == N
        args += list(tables)
        in_specs += [pl.BlockSpec((tm, LANES), lambda i, j: (i, 0))] * 5
    out_shape = [jax.ShapeDtypeStruct((M, N), out_dtype)]
    out_specs = [pl.BlockSpec((tm, tn), lambda i, j: (i, j))]
    if emit_xn:
        c = xs[0][2] // 2 // LANES
        out_shape.append(jax.ShapeDtypeStruct((M * c, LANES), jnp.uint32))
        out_specs.append(pl.BlockSpec((tm * c, LANES), lambda i, j: (i, 0)))
    if f32_from is not None:
        out_shape.append(jax.ShapeDtypeStruct((M, N - f32_from * tn), F32))
        out_specs.append(pl.BlockSpec((tm, tn), lambda i, j: (i, jnp.maximum(j - f32_from, 0))))
    scratch = [pltpu.VMEM((tm, xs[0][2]), BF16)] if has_gain else []
    res = pl.pallas_call(
        functools.partial(_mm_body, nx=nx, has_gain=has_gain, emit_xn=emit_xn, mode=mode, resid_is_x=resid_is_x,
                          has_out_gain=out_gain is not None,
                          col_prog=None if col_prog is None else tuple(col_prog), f32_from=f32_from),
        out_shape=out_shape, grid=(M // tm, N // tn), in_specs=in_specs, out_specs=out_specs,
        scratch_shapes=scratch, compiler_params=_params("arbitrary", "arbitrary"), name=name,
    )(*args)
    return res if len(res) > 1 else res[0]


def _rope_tables(positions):
    def tables(dim):
        inv = 1.0 / (ROPE_THETA ** (jnp.arange(0, dim, 2, dtype=F32) / dim))
        ang = positions.astype(F32)[..., None] * inv
        return jnp.cos(ang), jnp.sin(ang)

    lead = positions.shape
    c, s = tables(HEAD_DIM)
    c128 = jnp.concatenate([c, c], -1)
    s128 = jnp.concatenate([-s, s], -1)
    c, s = tables(B_ROPE)
    z = jnp.zeros_like(c)
    c64 = jnp.concatenate([c, c, z, z], -1)
    s64a = jnp.concatenate([-s, z, z, z], -1)
    s64b = jnp.concatenate([z, s, z, z], -1)
    n = int(np.prod(lead))
    return [t.reshape(n, LANES) for t in (c128, s128, c64, s64a, s64b)]


def _pair_tables(S, tq, tk, window):
    qi_l, ki_l, fl_l, mk_l, masks, ids = [], [], [], [], [], {}
    for qi in range(S // tq):
        q0, q1 = qi * tq, (qi + 1) * tq - 1
        ks = []
        for ki in range(S // tk):
            k0, k1 = ki * tk, (ki + 1) * tk - 1
            if k0 > q1 or (window is not None and q0 - k1 >= window):
                continue
            full = k1 <= q0 and (window is None or q1 - k0 < window)
            mid = 0
            if not full:
                delta = q0 - k0
                if delta not in ids:
                    rel = np.arange(tq)[:, None] + delta - np.arange(tk)[None, :]
                    ok = (rel >= 0) if window is None else ((rel >= 0) & (rel < window))
                    masks.append(np.where(ok, 0.0, NEG_INF).astype(np.float32))
                    ids[delta] = len(masks)
                mid = ids[delta]
            ks.append((ki, mid))
        for n, (ki, mid) in enumerate(ks):
            qi_l.append(qi)
            ki_l.append(ki)
            fl_l.append((1 if n == 0 else 0) | (2 if n == len(ks) - 1 else 0))
            mk_l.append(mid)
    if not masks:
        masks.append(np.zeros((tq, tk), np.float32))
    tabs = [jnp.asarray(np.array(a, np.int32)) for a in (qi_l, ki_l, fl_l, mk_l)]
    return tabs, jnp.asarray(np.stack(masks))


def _flash_body(qi_t, ki_t, fl_t, mk_t, *refs, nq, nkv, q_src, k_src, v_src, tq, rc, causal_diag,
                has_sink, has_bias, has_sel):
    nc = len(q_src)
    it = iter(refs)
    q_refs = [next(it) for _ in range(nq)]
    kv_refs = [next(it) for _ in range(nkv)]
    mask_ref = next(it)
    sink_ref = next(it) if has_sink else None
    cum_ref = next(it) if has_bias else None
    cumt_ref = next(it) if has_bias else None
    sel_ref = next(it) if has_sel else None
    exp_ref = next(it) if has_sel else None
    o_ref = next(it)
    m_sc, acc_sc = next(it), next(it)
    cq_sc = next(it) if has_bias else None

    step_id = pl.program_id(2)
    fl, mk = fl_t[step_id], mk_t[step_id]

    def cat(refs_, src, rows=slice(None)):
        xs = [refs_[pi][0, rows, off:off + LANES] for pi, off in src]
        return xs[0] if len(xs) == 1 else jnp.concatenate(xs, axis=-1)

    @pl.when((fl & 1) != 0)
    def _init():
        m_sc[...] = jnp.full(m_sc.shape, NEG_INF, F32)
        acc_sc[...] = jnp.zeros(acc_sc.shape, F32)
        if has_bias:
            lane = lax.broadcasted_iota(jnp.int32, (tq, LANES), 1)
            for g in range(nc):
                head = pl.program_id(1) * nc + g
                cq_sc[g] = jnp.sum(jnp.where(lane == head, cum_ref[0], 0.0), axis=-1, keepdims=True)

    tk = kv_refs[0].shape[1]
    ones = jnp.ones((tk, LANES), BF16)

    def step(masked):
        ks = [cat(kv_refs, k_src[g]) for g in range(nc)]
        vs = [jnp.concatenate([kv_refs[pi][0, :, off:off + LANES], ones], axis=1) for pi, off in v_src]
        for r in range(tq // rc):
            rows = slice(r * rc, (r + 1) * rc)
            nk = min((r + 1) * rc, tk) if (masked and causal_diag) else tk
            reps = nk // LANES
            add = mask_ref[mk - 1, rows, :nk] if masked else None
            if has_sel:
                hidden = (jnp.dot(sel_ref[0, 0, rows, :], exp_ref[0, :, :nk], preferred_element_type=F32)
                          - 1.0) * (-NEG_INF)
                add = hidden if add is None else add + hidden
            for g in range(nc):
                q = cat(q_refs, q_src[g], rows)
                s = lax.dot_general(q, ks[g][:nk], (((1,), (1,)), ((), ())), preferred_element_type=F32)
                if has_bias:
                    s = s + (cq_sc[g, rows, :] - cumt_ref[0, g, :, :nk])
                if add is not None:
                    s = s + add
                m_prev = m_sc[g, rows, :]
                m_new = jnp.maximum(m_prev, jnp.max(s, axis=-1, keepdims=True))
                p = jnp.exp2(s - (jnp.concatenate([m_new] * reps, axis=1) if reps > 1 else m_new))
                alpha = jnp.exp2(m_prev - m_new)
                acc_sc[g, rows, :] = (jnp.concatenate([alpha, alpha], axis=1) * acc_sc[g, rows, :]
                                      + jnp.dot(p.astype(BF16), vs[g][:nk], preferred_element_type=F32))
                m_sc[g, rows, :] = m_new

    @pl.when(mk != 0)
    def _():
        step(True)

    @pl.when(mk == 0)
    def _():
        step(False)

    @pl.when((fl & 2) != 0)
    def _finish():
        for g in range(nc):
            m, acc, l = m_sc[g], acc_sc[g, :, :LANES], acc_sc[g, :, LANES:]
            if has_sink:
                sk = sink_ref[0, g:g + 1, 0:1] * LOG2E
                m_f = jnp.maximum(m, sk)
                w = jnp.exp2(m - m_f)
                l = l * w + jnp.exp2(sk - m_f)
                acc = acc * w
            o_ref[0, :, g * LANES:(g + 1) * LANES] = (acc / l).astype(o_ref.dtype)


def _flash(q_parts, kv_parts, q_src, k_src, v_src, *, name, B, S, n_steps, tq, tk, window=None, sinks=None,
           cum=None, cumt=None, sel=None, out_dtype=BF16, rc=FLASH_ROW_CHUNK):
    nc = len(q_src)
    tq, tk = _tile(S, tq), _tile(S, tk)
    tabs, masks = _pair_tables(S, tq, tk, window)
    npairs = int(tabs[0].shape[0])
    args, in_specs = [], []
    for arr, width, cf in q_parts:
        args.append(arr)
        in_specs.append(pl.BlockSpec((1, tq, width), lambda b, h, s, qt, kt, ft, mt, cf=cf: (b, qt[s], cf(h))))
    for arr, width, cf in kv_parts:
        args.append(arr)
        in_specs.append(pl.BlockSpec((1, tk, width), lambda b, h, s, qt, kt, ft, mt, cf=cf: (b, kt[s], cf(h))))
    args.append(masks)
    in_specs.append(pl.BlockSpec(masks.shape, lambda b, h, s, qt, kt, ft, mt: (0, 0, 0)))
    if sinks is not None:
        args.append(jnp.broadcast_to(sinks.astype(F32).reshape(n_steps, nc, 1), (n_steps, nc, LANES)))
        in_specs.append(pl.BlockSpec((1, nc, LANES), lambda b, h, s, qt, kt, ft, mt: (h, 0, 0)))
    if cum is not None:
        args += [cum, cumt]
        in_specs += [pl.BlockSpec((1, tq, LANES), lambda b, h, s, qt, kt, ft, mt: (b, qt[s], 0)),
                     pl.BlockSpec((1, nc, 1, tk), lambda b, h, s, qt, kt, ft, mt: (b, h, 0, kt[s]))]
    if sel is not None:
        per = tk // D_SLC_LEN
        e = np.zeros((S // tk, LANES, tk), np.float32)
        for ki in range(S // tk):
            e[ki, ki * per + np.arange(tk) // D_SLC_LEN, np.arange(tk)] = 1.0
        args += [sel, jnp.asarray(e, BF16)]
        in_specs += [pl.BlockSpec((1, 1, tq, LANES), lambda b, h, s, qt, kt, ft, mt: (b, h, qt[s], 0)),
                     pl.BlockSpec((1, LANES, tk), lambda b, h, s, qt, kt, ft, mt: (kt[s], 0, 0))]
    scratch = [pltpu.VMEM((nc, tq, LANES), F32), pltpu.VMEM((nc, tq, 2 * LANES), F32)]
    if cum is not None:
        scratch.append(pltpu.VMEM((nc, tq, 1), F32))
    body = functools.partial(_flash_body, nq=len(q_parts), nkv=len(kv_parts), q_src=q_src, k_src=k_src,
                             v_src=v_src, tq=tq, rc=_tile(tq, rc), causal_diag=window is None and tq == tk,
                             has_sink=sinks is not None,
                             has_bias=cum is not None, has_sel=sel is not None)
    return pl.pallas_call(
        body, out_shape=jax.ShapeDtypeStruct((B, S, n_steps * nc * LANES), out_dtype),
        grid_spec=pltpu.PrefetchScalarGridSpec(
            num_scalar_prefetch=4, grid=(B, n_steps, npairs), in_specs=in_specs,
            out_specs=pl.BlockSpec((1, tq, nc * LANES), lambda b, h, s, qt, kt, ft, mt: (b, qt[s], h)),
            scratch_shapes=scratch),
        compiler_params=_params("arbitrary", "arbitrary", "arbitrary"), name=name,
    )(*tabs, *args)


def _gqa_flash(q, k, v, *, name, B, S, Hk, G, **kw):
    at = lambda first: (lambda h: first + h)
    return _flash([(q[0], G * LANES, at(q[1]))], [(k[0], LANES, at(k[1])), (v[0], LANES, at(v[1]))],
                  [[(0, g * LANES)] for g in range(G)], [[(0, 0)]] * G, [(1, 0)] * G,
                  name=name, B=B, S=S, n_steps=Hk, **kw)


def _cum_body(y_ref, b_ref, tri_ref, o_ref, carry):
    @pl.when(pl.program_id(1) == 0)
    def _():
        carry[...] = jnp.zeros(carry.shape, F32)
    x = y_ref[0] + b_ref[...]
    logf = jnp.minimum(x, 0.0) - jnp.log1p(jnp.exp(-jnp.abs(x)))
    cum = jnp.dot(tri_ref[...], logf, preferred_element_type=F32, precision=lax.Precision.HIGHEST) + carry[...]
    o_ref[0] = cum * LOG2E
    carry[...] = cum[-1:, :]


def _forget_cum(y3, col_block, bias_row, *, ts=512):
    B, S, _ = y3.shape
    ts = _tile(S, ts)
    tri = jnp.asarray(np.tril(np.ones((ts, ts), np.float32)))
    return pl.pallas_call(
        _cum_body, out_shape=jax.ShapeDtypeStruct((B, S, LANES), F32), grid=(B, S // ts),
        in_specs=[pl.BlockSpec((1, ts, LANES), lambda b, s: (b, s, col_block)),
                  pl.BlockSpec((1, LANES), lambda b, s: (0, 0)),
                  pl.BlockSpec((ts, ts), lambda b, s: (0, 0))],
        out_specs=pl.BlockSpec((1, ts, LANES), lambda b, s: (b, s, 0)),
        scratch_shapes=[pltpu.VMEM((1, LANES), F32)],
        compiler_params=_params("arbitrary", "arbitrary"), name="forget_cum",
    )(y3, bias_row, tri)


def _compress_body(*refs, rope, nc, width, col0):
    if rope:
        z_ref, pe_ref, w1_ref, w2_ref, c_ref, s_ref, o_ref = refs
    else:
        z_ref, pe_ref, w1_ref, w2_ref, o_ref = refs
    half = D_CMP_LEN // 2
    for hk in range(D_KV_HEADS):
        u = jnp.zeros((nc, w1_ref.shape[1]), F32)
        v = jnp.zeros((nc, w1_ref.shape[1]), F32)
        for l in range(half):
            first = l * width + col0 + hk * HEAD_DIM
            z = z_ref[0, :, first:first + HEAD_DIM]
            zu = (z + pe_ref[l:l + 1, :]).astype(BF16)
            zv = (z + pe_ref[half + l:half + l + 1, :]).astype(BF16)
            u = u + jnp.dot(zu, w1_ref[l * HEAD_DIM:(l + 1) * HEAD_DIM, :], preferred_element_type=F32)
            v = v + jnp.dot(zv, w1_ref[(half + l) * HEAD_DIM:(half + l + 1) * HEAD_DIM, :],
                            preferred_element_type=F32)
        pre = u + pltpu.roll(v, nc - 1, 0)
        hid = jax.nn.gelu(pre, approximate=True)
        out = jnp.dot(hid.astype(BF16), w2_ref[...], preferred_element_type=F32)
        if rope:
            out = _rot128(out, c_ref[0], s_ref[0])
        o_ref[0, hk] = out.astype(o_ref.dtype)


def _compress(z, col0, pe, w1, w2, rope_tabs=None):
    B, S, W = z.shape
    nc = S // D_CMP_STRIDE
    zc = z.reshape(B, nc, D_CMP_STRIDE * W)
    args = [zc, pe.astype(F32), w1.astype(BF16), w2.astype(BF16)]
    in_specs = [pl.BlockSpec((1, nc, D_CMP_STRIDE * W), lambda b: (b, 0, 0)),
                pl.BlockSpec(pe.shape, lambda b: (0, 0)),
                pl.BlockSpec(w1.shape, lambda b: (0, 0)),
                pl.BlockSpec(w2.shape, lambda b: (0, 0))]
    if rope_tabs is not None:
        args += list(rope_tabs)
        in_specs += [pl.BlockSpec((1, nc, LANES), lambda b: (b, 0, 0))] * 2
    return pl.pallas_call(
        functools.partial(_compress_body, rope=rope_tabs is not None, nc=nc, width=W, col0=col0),
        out_shape=jax.ShapeDtypeStruct((B, D_KV_HEADS, nc, HEAD_DIM), BF16), grid=(B,),
        in_specs=in_specs, out_specs=pl.BlockSpec((1, D_KV_HEADS, nc, HEAD_DIM), lambda b: (b, 0, 0, 0)),
        compiler_params=_params("arbitrary"), name="nsa_compress",
    )(*args)


def _cmp_attn_body(q_ref, k_ref, v_ref, ov_ref, o_ref, sel_ref, *, G, tq, nc, n_cmp, n_slc, topn):
    qi = pl.program_id(2)
    t = qi * tq + lax.broadcasted_iota(jnp.int32, (tq, nc), 0)
    c = lax.broadcasted_iota(jnp.int32, (tq, nc), 1)
    valid = (c * D_CMP_STRIDE + (D_CMP_LEN - 1) <= t) & (c < n_cmp)
    k = k_ref[0, 0]
    v = v_ref[0, 0]
    psum = jnp.zeros((tq, nc), F32)
    for g in range(G):
        q = q_ref[0, :, g * LANES:(g + 1) * LANES]
        s = lax.dot_general(q, k, (((1,), (1,)), ((), ())), preferred_element_type=F32)
        s = jnp.where(valid, s, NEG_INF)
        e = jnp.where(valid, jnp.exp2(s - jnp.max(s, axis=-1, keepdims=True)), 0.0)
        p = e / jnp.maximum(jnp.sum(e, axis=-1, keepdims=True), jnp.finfo(F32).tiny)
        o_ref[0, :, g * LANES:(g + 1) * LANES] = jnp.dot(
            p.astype(BF16), v, preferred_element_type=F32).astype(o_ref.dtype)
        psum = psum + p
    imp = lax.dot_general(ov_ref[...], psum, (((1,), (1,)), ((), ())), preferred_element_type=F32,
                          precision=lax.Precision.HIGHEST)
    blk = lax.broadcasted_iota(jnp.int32, (LANES, tq), 0)
    tcol = qi * tq + lax.broadcasted_iota(jnp.int32, (LANES, tq), 1)
    cur = jnp.right_shift(tcol, int(math.log2(D_SLC_LEN)))
    forced = (blk == 0) | (blk == cur) | (blk == cur - 1)
    imp = jnp.where(blk * D_SLC_LEN > tcol, NEG_INF, imp + jnp.where(forced, FORCE_BONUS, 0.0))
    imp = jnp.where(blk >= n_slc, TAKEN, imp)
    chosen = jnp.zeros((LANES, tq), F32)
    blk_f = blk.astype(F32)
    for _ in range(topn):
        mx = jnp.max(imp, axis=0, keepdims=True)
        idx = jnp.min(jnp.where(imp == mx, blk_f, float(LANES)), axis=0, keepdims=True)
        hit = blk_f == idx
        chosen = jnp.where(hit, 1.0, chosen)
        imp = jnp.where(hit, TAKEN, imp)
    sel_ref[0, 0] = chosen.T.astype(sel_ref.dtype)


def _cmp_attn(q, q_first, k_cmp, v_cmp, *, tq=256):
    B, S, _ = q.shape
    Hk, G = D_KV_HEADS, D_HEADS // D_KV_HEADS
    nc = S // D_CMP_STRIDE
    n_cmp = (S - D_CMP_LEN) // D_CMP_STRIDE + 1
    n_slc = S // D_SLC_LEN
    tq = _tile(S, tq)
    c0 = np.arange(nc) * D_CMP_STRIDE
    s0 = np.arange(LANES) * D_SLC_LEN
    ov = ((c0[:, None] < (s0 + D_SLC_LEN)[None, :]) & ((c0 + D_CMP_LEN)[:, None] > s0[None, :])
          & (np.arange(nc) < n_cmp)[:, None] & (np.arange(LANES) < n_slc)[None, :]).astype(np.float32)
    body = functools.partial(_cmp_attn_body, G=G, tq=tq, nc=nc, n_cmp=n_cmp, n_slc=n_slc,
                             topn=min(D_SLC_TOPN, n_slc))
    return pl.pallas_call(
        body,
        out_shape=[jax.ShapeDtypeStruct((B, S, Hk * G * LANES), F32),
                   jax.ShapeDtypeStruct((B, Hk, S, LANES), BF16)],
        grid=(B, Hk, S // tq),
        in_specs=[pl.BlockSpec((1, tq, G * LANES), lambda b, h, i: (b, i, q_first + h)),
                  pl.BlockSpec((1, 1, nc, LANES), lambda b, h, i: (b, h, 0, 0)),
                  pl.BlockSpec((1, 1, nc, LANES), lambda b, h, i: (b, h, 0, 0)),
                  pl.BlockSpec((LANES, nc), lambda b, h, i: (0, 0))],
        out_specs=[pl.BlockSpec((1, tq, G * LANES), lambda b, h, i: (b, i, h)),
                   pl.BlockSpec((1, 1, tq, LANES), lambda b, h, i: (b, h, i, 0))],
        compiler_params=_params("arbitrary", "arbitrary", "arbitrary"), name="nsa_cmp_attn",
    )(q, k_cmp, v_cmp, jnp.asarray(ov.T))


def _gate_body(y_ref, a_ref, b_ref, c_ref, o_ref, *, lane0):
    g = jax.nn.sigmoid(y_ref[...])
    for h in range(D_HEADS):
        cols = slice(h * LANES, (h + 1) * LANES)
        ga = g[:, lane0 + h:lane0 + h + 1]
        gb = g[:, lane0 + D_HEADS + h:lane0 + D_HEADS + h + 1]
        gc = g[:, lane0 + 2 * D_HEADS + h:lane0 + 2 * D_HEADS + h + 1]
        o_ref[:, cols] = (ga * a_ref[:, cols] + gb * b_ref[:, cols] + gc * c_ref[:, cols]).astype(o_ref.dtype)


def _nsa_gate(y, col_block, lane0, o_cmp, o_slc, o_win, *, tm=512):
    M = y.shape[0]
    W = o_cmp.shape[1]
    tm = _tile(M, tm)
    row = lambda i: (i, 0)
    return pl.pallas_call(
        functools.partial(_gate_body, lane0=lane0), out_shape=jax.ShapeDtypeStruct((M, W), BF16),
        grid=(M // tm,),
        in_specs=[pl.BlockSpec((tm, LANES), lambda i: (i, col_block))] + [pl.BlockSpec((tm, W), row)] * 3,
        out_specs=pl.BlockSpec((tm, W), row), compiler_params=_params("arbitrary"), name="nsa_gate",
    )(y, o_cmp, o_slc, o_win)


def _route_body(lg_ref, b_ref, tri_ref, e_ref, w_ref, pos_ref, cnt_ref, carry, *, tm):
    @pl.when(pl.program_id(0) == 0)
    def _():
        carry[...] = jnp.zeros(carry.shape, F32)
    lane = lax.broadcasted_iota(jnp.int32, (tm, LANES), 1)
    logits = lg_ref[...] + b_ref[...]
    gl = jnp.where(lane < N_GROUPS, logits, -jnp.inf)
    gmax = jnp.max(gl, axis=-1, keepdims=True)
    g_val = 1.0 / jnp.sum(jnp.exp(gl - gmax), axis=-1, keepdims=True)
    g_idx = jnp.min(jnp.where(gl == gmax, lane, LANES), axis=-1, keepdims=True)
    lo = N_GROUPS + EXPERTS_PER_GROUP * g_idx
    el = jnp.where((lane >= lo) & (lane < lo + EXPERTS_PER_GROUP), logits, -jnp.inf)
    e1 = jnp.max(el, axis=-1, keepdims=True)
    i1 = jnp.min(jnp.where(el == e1, lane, LANES), axis=-1, keepdims=True)
    el2 = jnp.where(lane == i1, -jnp.inf, el)
    e2 = jnp.max(el2, axis=-1, keepdims=True)
    i2 = jnp.min(jnp.where(el2 == e2, lane, LANES), axis=-1, keepdims=True)
    r = jnp.exp(e2 - e1)
    w1 = g_val / (1.0 + r)
    w2 = w1 * r
    x1, x2 = i1 - N_GROUPS, i2 - N_GROUPS
    e_ref[...] = jnp.where(lane == 0, x1, jnp.where(lane == 1, x2, 0))
    w_ref[...] = jnp.where(lane == 0, w1, jnp.where(lane == 1, w2, 0.0))
    hot1 = lane == x1
    hot2 = lane == x2
    both = jnp.where(hot1 | hot2, 1.0, 0.0)
    before = jnp.dot(tri_ref[...], both.astype(BF16), preferred_element_type=F32) + carry[...]
    p1 = jnp.sum(jnp.where(hot1, before, 0.0), axis=-1, keepdims=True)
    p2 = jnp.sum(jnp.where(hot2, before, 0.0), axis=-1, keepdims=True)
    pos_ref[...] = jnp.where(lane == 0, p1, jnp.where(lane == 1, p2, 0.0)).astype(jnp.int32)
    carry[...] = carry[...] + jnp.sum(both, axis=0, keepdims=True)
    cnt_ref[...] = carry[...].astype(jnp.int32)


def _route(logits, bias_row, *, tm=512):
    T = logits.shape[0]
    tm = _tile(T, tm)
    tri = jnp.asarray(np.tril(np.ones((tm, tm), np.float32), -1), BF16)
    row = lambda i: (i, 0)
    fixed = lambda i: (0, 0)
    return pl.pallas_call(
        functools.partial(_route_body, tm=tm),
        out_shape=[jax.ShapeDtypeStruct((T, LANES), jnp.int32), jax.ShapeDtypeStruct((T, LANES), F32),
                   jax.ShapeDtypeStruct((T, LANES), jnp.int32), jax.ShapeDtypeStruct((1, LANES), jnp.int32)],
        grid=(T // tm,),
        in_specs=[pl.BlockSpec((tm, LANES), row), pl.BlockSpec((1, LANES), fixed), pl.BlockSpec((tm, tm), fixed)],
        out_specs=[pl.BlockSpec((tm, LANES), row)] * 3 + [pl.BlockSpec((1, LANES), fixed)],
        scratch_shapes=[pltpu.VMEM((1, LANES), F32)],
        compiler_params=_params("arbitrary"), name="moe_route",
    )(logits, bias_row, tri)


def _dispatch_body(d_ref, ends_ref, pad_ref, nu_ref, x_ref, out_ref, zero_sc, sem, zsem, *, tm, nblk, c):
    base = pl.program_id(0) * tm

    @pl.when(pl.program_id(0) == 0)
    def _():
        zero_sc[...] = jnp.zeros(zero_sc.shape, zero_sc.dtype)

        def zero_block(first_row):
            return pltpu.make_async_copy(
                zero_sc, out_ref.at[pl.ds(pl.multiple_of(first_row * c, MOE_ROWS * c), MOE_ROWS * c)], zsem)

        for wait in (False, True):
            for e in range(N_EXPERTS):
                for live, first_row in ((pad_ref[e] > 0, ends_ref[e] - MOE_ROWS),
                                        (nu_ref[0] + e < nblk, (nu_ref[0] + e) * MOE_ROWS)):
                    @pl.when(live)
                    def _():
                        zero_block(first_row).wait() if wait else zero_block(first_row).start()

    def issue(r, carry):
        for k in range(TOP_K):
            row = d_ref[(base + r) * TOP_K + k]
            pltpu.make_async_copy(x_ref.at[pl.ds(pl.multiple_of(r * c, c), c)],
                                  out_ref.at[pl.ds(pl.multiple_of(row * c, c), c)], sem).start(priority=k)
        return carry

    lax.fori_loop(0, tm, issue, 0, unroll=DMA_UNROLL)
    for _ in range(TOP_K):
        pltpu.make_async_copy(x_ref, out_ref.at[pl.ds(0, tm * c)], sem).wait()


def _dispatch(xn, dest, ends, padded, n_used, n_rows, c, *, tm=512):
    T = xn.shape[0] // c
    tm = _tile(T, tm)
    return pl.pallas_call(
        functools.partial(_dispatch_body, tm=tm, nblk=n_rows // MOE_ROWS, c=c),
        out_shape=jax.ShapeDtypeStruct((n_rows * c, LANES), xn.dtype),
        grid_spec=pltpu.PrefetchScalarGridSpec(
            num_scalar_prefetch=4, grid=(T // tm,),
            in_specs=[pl.BlockSpec((tm * c, LANES), lambda i, *_: (i, 0))],
            out_specs=pl.BlockSpec(memory_space=pl.ANY),
            scratch_shapes=[pltpu.VMEM((MOE_ROWS * c, LANES), xn.dtype), pltpu.SemaphoreType.DMA(()),
                            pltpu.SemaphoreType.DMA(())]),
        compiler_params=pltpu.CompilerParams(dimension_semantics=("arbitrary",), has_side_effects=True,
                                             vmem_limit_bytes=VMEM_LIMIT_BYTES, disable_bounds_checks=True),
        name="moe_dispatch",
    )(dest, ends, padded, n_used, xn)


def _expert_body(nb_ref, first_ref, nu_ref, xb_ref, wg_ref, wu_ref, wd_ref, yb_ref, wg_sc, wu_sc, wd_sc, xbuf,
                 obuf, sem_in, sem_out, state, *, nblk, c):
    e = pl.program_id(0)
    last = pl.num_programs(0) - 1
    nb, first = nb_ref[e], first_ref[e]

    @pl.when(e == 0)
    def _():
        for i in range(3):
            state[i] = 0

    def rows(first_block, b):
        return pl.ds(pl.multiple_of((first_block + b) * (MOE_ROWS * c), MOE_ROWS * c), MOE_ROWS * c)

    def fetch(first_block, b, slot):
        return pltpu.make_async_copy(xb_ref.at[rows(first_block, b)], xbuf.at[slot], sem_in.at[slot])

    def flush(b, slot):
        return pltpu.make_async_copy(obuf.at[slot], yb_ref.at[rows(first, b)], sem_out.at[slot])

    def drain(slot):
        @pl.when(state[slot] != 0)
        def _():
            pltpu.make_async_copy(obuf.at[slot], yb_ref.at[pl.ds(0, MOE_ROWS * c)], sem_out.at[slot]).wait()
            state[slot] = 0

    @pl.when(nb > 0)
    def _():
        @pl.when(state[2] == 0)
        def _():
            fetch(first, 0, 0).start()

        state[2] = 0
        wg_sc[...] = wg_ref[0, 0].astype(BF16)
        wu_sc[...] = wu_ref[0, 0].astype(BF16)
        wd_sc[...] = wd_ref[0, 0].astype(BF16)

        def block(b, carry):
            slot = lax.rem(b, 2)

            @pl.when(b + 1 < nb)
            def _():
                fetch(first, b + 1, 1 - slot).start()

            fetch(first, b, slot).wait()
            drain(slot)
            x = jnp.concatenate(_unpack_bf16_pairs(_tiles_load(xbuf, (slot,), MOE_ROWS, c)), axis=1).astype(BF16)
            gate = jnp.dot(x, wg_sc[...], preferred_element_type=F32)
            up = jnp.dot(x, wu_sc[...], preferred_element_type=F32)
            hid = (gate * jax.nn.sigmoid(gate) * up).astype(BF16)
            _tiles_store(obuf, (slot,), _pack_bf16_pairs(jnp.dot(hid, wd_sc[...], preferred_element_type=F32)))
            flush(b, slot).start()
            state[slot] = 1
            return carry

        lax.fori_loop(0, nb, block, 0)
        nxt = jnp.minimum(e + 1, last)

        @pl.when((e < last) & (nb_ref[nxt] > 0))
        def _():
            fetch(first_ref[nxt], 0, 0).start()
            state[2] = 1

    @pl.when(e == last)
    def _():
        drain(0)
        drain(1)
        obuf[0] = jnp.zeros(obuf.shape[1:], obuf.dtype)

        def tail(t):
            return pltpu.make_async_copy(obuf.at[0], yb_ref.at[rows(nu_ref[0], t)], sem_out.at[0])

        for wait in (False, True):
            for t in range(N_EXPERTS):
                @pl.when(nu_ref[0] + t < nblk)
                def _():
                    tail(t).wait() if wait else tail(t).start()


def _experts(xb, n_blocks, first_block, n_used, layer, w_gate, w_up, w_down):
    D, Hd = w_gate.shape[2], w_gate.shape[3]
    c = D // 2 // LANES
    nblk = xb.shape[0] // (MOE_ROWS * c)
    wspec = lambda shape: pl.BlockSpec((1, 1) + shape, lambda e, *_: (layer, e, 0, 0))
    return pl.pallas_call(
        functools.partial(_expert_body, nblk=nblk, c=c), out_shape=jax.ShapeDtypeStruct(xb.shape, xb.dtype),
        grid_spec=pltpu.PrefetchScalarGridSpec(
            num_scalar_prefetch=3, grid=(w_gate.shape[1],),
            in_specs=[pl.BlockSpec(memory_space=pl.ANY), wspec((D, Hd)), wspec((D, Hd)), wspec((Hd, D))],
            out_specs=pl.BlockSpec(memory_space=pl.ANY),
            scratch_shapes=[pltpu.VMEM((D, Hd), BF16), pltpu.VMEM((D, Hd), BF16), pltpu.VMEM((Hd, D), BF16),
                            pltpu.VMEM((2, MOE_ROWS * c, LANES), xb.dtype),
                            pltpu.VMEM((2, MOE_ROWS * c, LANES), xb.dtype),
                            pltpu.SemaphoreType.DMA((2,)), pltpu.SemaphoreType.DMA((2,)),
                            pltpu.SMEM((3,), jnp.int32)]),
        compiler_params=pltpu.CompilerParams(dimension_semantics=("arbitrary",), has_side_effects=True,
                                             vmem_limit_bytes=VMEM_LIMIT_BYTES),
        name="moe_experts",
    )(n_blocks, first_block, n_used, xb, w_gate, w_up, w_down)


def _collect_body(d_ref, h_ref, w_ref, yb_ref, o_ref, buf_a, buf_b, sem, *, tm, nsteps, c):
    i = pl.program_id(0)
    slot = lax.rem(i, 2)

    def fetch(step, slot_):
        base = step * tm

        def issue(r, carry):
            for k, buf in enumerate((buf_a, buf_b)):
                row = d_ref[(base + r) * TOP_K + k]
                pltpu.make_async_copy(yb_ref.at[pl.ds(pl.multiple_of(row * c, c), c)],
                                      buf.at[slot_, pl.ds(pl.multiple_of(r * c, c), c)],
                                      sem.at[slot_]).start(priority=k)
            return carry

        lax.fori_loop(0, tm, issue, 0, unroll=DMA_UNROLL)

    @pl.when(i == 0)
    def _():
        fetch(0, 0)

    @pl.when(i + 1 < nsteps)
    def _():
        fetch(i + 1, 1 - slot)

    for buf in (buf_a, buf_b):
        pltpu.make_async_copy(yb_ref.at[pl.ds(0, tm * c)], buf.at[slot], sem.at[slot]).wait()
    w = w_ref[...]
    half = h_ref.shape[1] // 2
    for part, ya, yb in zip((slice(0, half), slice(half, None)),
                            _unpack_bf16_pairs(_tiles_load(buf_a, (slot,), tm, c)),
                            _unpack_bf16_pairs(_tiles_load(buf_b, (slot,), tm, c))):
        o_ref[:, part] = h_ref[:, part] + w[:, 0:1] * ya + w[:, 1:2] * yb


def _collect(h, yb, dest, wts, *, tm=256):
    T, D = h.shape
    c = D // 2 // LANES
    tm = _tile(T, tm)
    nsteps = T // tm
    return pl.pallas_call(
        functools.partial(_collect_body, tm=tm, nsteps=nsteps, c=c),
        out_shape=jax.ShapeDtypeStruct((T, D), F32),
        grid_spec=pltpu.PrefetchScalarGridSpec(
            num_scalar_prefetch=1, grid=(nsteps,),
            in_specs=[pl.BlockSpec((tm, D), lambda i, d: (i, 0)), pl.BlockSpec((tm, LANES), lambda i, d: (i, 0)),
                      pl.BlockSpec(memory_space=pl.ANY)],
            out_specs=pl.BlockSpec((tm, D), lambda i, d: (i, 0)),
            scratch_shapes=[pltpu.VMEM((2, tm * c, LANES), yb.dtype), pltpu.VMEM((2, tm * c, LANES), yb.dtype),
                            pltpu.SemaphoreType.DMA((2,))]),
        compiler_params=pltpu.CompilerParams(dimension_semantics=("arbitrary",), disable_bounds_checks=True,
                                             vmem_limit_bytes=VMEM_LIMIT_BYTES),
        name="moe_collect",
    )(dest, h, wts, yb)


def _moe(h, norm_g, w_group, b_group, w_expert, b_expert, layer, w_gate, w_up, w_down):
    T, D = h.shape
    pad = LANES - N_GROUPS - N_EXPERTS
    w_r = jnp.concatenate([w_group, w_expert, jnp.zeros((D, pad), F32)], axis=1).astype(BF16)
    b_r = jnp.concatenate([b_group, b_expert, jnp.zeros((pad,), F32)]).astype(F32).reshape(1, LANES)
    logits, xn = _mm([(h, 0, D)], [w_r], gain=norm_g, emit_xn=True, tm=512, tn=LANES, name="moe_router")
    eid, wts, pos, cnt = _route(logits, b_r)
    counts = cnt[0, :N_EXPERTS]
    padded = (counts + MOE_ROWS - 1) // MOE_ROWS * MOE_ROWS
    ends = jnp.cumsum(padded)
    offs = ends - padded
    dest = (offs[eid[:, :TOP_K]] + pos[:, :TOP_K]).reshape(T * TOP_K).astype(jnp.int32)
    P = T * TOP_K + N_EXPERTS * MOE_ROWS
    n_used = (ends[-1:] // MOE_ROWS).astype(jnp.int32)
    xb = _dispatch(xn, dest, ends.astype(jnp.int32), padded.astype(jnp.int32), n_used, P, D // 2 // LANES)
    yb = _experts(xb, (padded // MOE_ROWS).astype(jnp.int32), (offs // MOE_ROWS).astype(jnp.int32), n_used, layer,
                  w_gate, w_up, w_down)
    return _collect(h, yb, dest, wts)


def _pad_cols(w, n):
    return jnp.pad(w, ((0, 0), (0, n - w.shape[1])))


def _layer_even(h, B, S, tabs, norm_g, w_in, sinks, q_lat_norm, kv_lat_norm, w_uq, w_ukv, w_o):
    T, D = h.shape
    n_in, tn = 2560, 512
    sa = HEAD_DIM ** -0.5 * LOG2E
    prog = ([("rot128", sa)] * 8 + [("rot128", 1.0)] * 2 + [("copy", 1.0)] * 8 + [("rot64", 1.0), ("copy", 1.0)])
    y, yf = _mm([(h, 0, D)], [_pad_cols(w_in.astype(BF16), n_in)], gain=norm_g, tn=tn, out_dtype=BF16,
                col_prog=prog, tables=tabs, f32_from=12 * LANES // tn, name="ab_in")
    wq = w_uq.reshape(B_Q_LORA, B_HEADS, B_NOPE + B_ROPE)
    wq_n = wq[:, :, :B_NOPE].reshape(B_Q_LORA, B_HEADS * B_NOPE)
    wq_r = jnp.pad(wq[:, :, B_NOPE:], ((0, 0), (0, 0), (0, LANES - B_ROPE))).reshape(B_Q_LORA, B_HEADS * LANES)
    sb = (B_NOPE + B_ROPE) ** -0.5 * LOG2E
    qq = _mm([(yf, 0, B_Q_LORA)], [jnp.concatenate([wq_n, wq_r], 1).astype(BF16)], gain=q_lat_norm, tn=512,
             out_dtype=BF16, col_prog=[("copy", sb)] * 8 + [("rot64", sb)] * 8, tables=tabs, name="mla_uq")
    kv = _mm([(yf, B_Q_LORA // B_KV_LORA, B_KV_LORA)], [w_ukv.astype(BF16)], gain=kv_lat_norm, out_dtype=BF16,
             tn=512, name="mla_ukv")
    r3 = lambda a: a.reshape(B, S, a.shape[-1])
    y3 = r3(y)
    o_a = _gqa_flash((y3, 0), (y3, 8), (y3, 10), name="swa_attn", B=B, S=S, Hk=A_KV_HEADS,
                     G=A_HEADS // A_KV_HEADS, window=A_WINDOW, sinks=sinks, tq=256, tk=256)
    hp = 2
    head = lambda h: h
    o_b = _flash([(r3(qq), hp * LANES, head), (r3(qq), hp * LANES, lambda h: B_HEADS // hp + h)],
                 [(r3(kv), hp * 2 * LANES, head), (y3, LANES, lambda h: 18)],
                 [[(0, g * LANES), (1, g * LANES)] for g in range(hp)],
                 [[(0, 2 * g * LANES), (1, 0)] for g in range(hp)], [(0, (2 * g + 1) * LANES) for g in range(hp)],
                 name="mla_attn", B=B, S=S, n_steps=B_HEADS // hp, tq=1024, tk=1024, rc=256)
    wo = w_o.astype(BF16)
    na = A_HEADS * HEAD_DIM
    return _mm([(o_a.reshape(T, -1), 0, na), (o_b.reshape(T, -1), 0, B_HEADS * B_V)], [wo[:na], wo[na:]],
               resid=h, tm=512, tn=D, name="ab_out")


def _layer_odd(h, B, S, tabs, cmp_tabs, norm_g, w_in, forget_bias, pe_k, w1_k, w2_k, pe_v, w1_v, w2_v, w_o):
    T, D = h.shape
    hc, kvw = C_HEADS * HEAD_DIM, D_KV_HEADS * HEAD_DIM
    o_cf = 3 * hc
    o_dq = o_cf + C_HEADS
    o_kc = o_dq + D_HEADS * HEAD_DIM
    o_ks = o_kc + 2 * kvw
    o_dg = o_ks + 4 * kvw
    n_in, tn = 5760, 640
    w16 = w_in.astype(BF16)
    w_r = jnp.concatenate([w16[:, :o_cf], w16[:, o_dq:o_kc], w16[:, o_ks:o_dg], w16[:, o_kc:o_ks], w16[:, o_cf:o_dq],
                           w16[:, o_dg:], jnp.zeros((D, n_in - w_in.shape[1]), BF16)], axis=1)
    sc = HEAD_DIM ** -0.5 * LOG2E
    prog = ([("copy", sc)] * 8 + [("copy", 1.0)] * 16 + [("rot128", sc)] * 8
            + [("rot128", 1.0)] * 2 + [("copy", 1.0)] * 2 + [("rot128", 1.0)] * 2 + [("copy", 1.0)] * 7)
    y, yf = _mm([(h, 0, D)], [w_r], gain=norm_g, tn=tn, out_dtype=BF16,
                col_prog=prog, tables=tabs, f32_from=n_in // tn - 1, name="cd_in")
    misc = tn // LANES - 1
    r3 = lambda a: a.reshape(B, S, a.shape[-1])
    y3, yf3 = r3(y), r3(yf)
    fb = jnp.pad(forget_bias.astype(F32), (0, LANES - C_HEADS)).reshape(1, LANES)
    cum = _forget_cum(yf3, misc, fb)
    cumt = jnp.swapaxes(cum[:, :, :C_HEADS], 1, 2).reshape(B, C_HEADS, 1, S)
    hp = 2
    at = lambda first: (lambda h: first + h)
    o_c = _flash([(y3, hp * LANES, at(0))], [(y3, hp * LANES, at(C_HEADS // hp)), (y3, hp * LANES, at(C_HEADS))],
                 [[(0, g * LANES)] for g in range(hp)], [[(0, g * LANES)] for g in range(hp)],
                 [(1, g * LANES) for g in range(hp)], name="fox_attn", B=B, S=S, n_steps=C_HEADS // hp,
                 tq=1024, tk=1024, cum=cum, cumt=cumt)
    G = D_HEADS // D_KV_HEADS
    k_cmp = _compress(yf3, 0, pe_k, w1_k, w2_k, rope_tabs=cmp_tabs)
    v_cmp = _compress(yf3, kvw, pe_v, w1_v, w2_v)
    q_d = (y3, 24 // G)
    o_cmp, sel = _cmp_attn(y3, q_d[1], k_cmp, v_cmp)
    o_slc = _gqa_flash(q_d, (y3, 32), (y3, 34), name="nsa_slc_attn", B=B, S=S, Hk=D_KV_HEADS, G=G, sel=sel,
                       out_dtype=F32, tq=1024, tk=1024, rc=256)
    o_win = _gqa_flash(q_d, (y3, 36), (y3, 38), name="nsa_win_attn", B=B, S=S, Hk=D_KV_HEADS, G=G,
                       window=D_WINDOW, out_dtype=F32, tq=512, tk=512)
    o_d = _nsa_gate(yf, misc, C_HEADS, o_cmp.reshape(T, -1), o_slc.reshape(T, -1), o_win.reshape(T, -1))
    wo = w_o.astype(BF16)
    return _mm([(o_c.reshape(T, -1), 0, hc), (o_d, 0, D_HEADS * HEAD_DIM)], [wo[:hc], wo[hc:]],
               resid=h, tm=512, tn=D, name="cd_out")


def kernel(x, p, positions, ab_w_in, ab_sinks, ab_q_lat_norm, ab_kv_lat_norm, ab_w_uq, ab_w_ukv, ab_w_o,
           cd_w_in, cd_forget_bias, cd_cmp_pe_k, cd_cmp_w1_k, cd_cmp_w2_k, cd_cmp_pe_v, cd_cmp_w1_v,
           cd_cmp_w2_v, cd_w_o, mixer_norm, moe_norm, router_group_w, router_group_b, router_expert_w,
           router_expert_b, expert_w_gate, expert_w_up, expert_w_down, ple_proj, ple_gate_norm, ple_gate_w,
           final_norm):
    B, S, D = x.shape
    T = B * S
    depth = p.shape[0]
    tabs = _rope_tables(positions)
    nc = S // D_CMP_STRIDE
    end = np.minimum(np.arange(nc) * D_CMP_STRIDE + D_CMP_LEN - 1, S - 1)
    cmp_tabs = [t.reshape(B, nc, LANES) for t in _rope_tables(positions[:, end])[:2]]
    h = x.reshape(T, D)
    for i in range(depth):
        j = i // 2
        if i % 2 == 0:
            h = _layer_even(h, B, S, tabs, mixer_norm[i], ab_w_in[j], ab_sinks[j], ab_q_lat_norm[j],
                            ab_kv_lat_norm[j], ab_w_uq[j], ab_w_ukv[j], ab_w_o[j])
        else:
            h = _layer_odd(h, B, S, tabs, cmp_tabs, mixer_norm[i], cd_w_in[j], cd_forget_bias[j],
                           cd_cmp_pe_k[j], cd_cmp_w1_k[j], cd_cmp_w2_k[j], cd_cmp_pe_v[j], cd_cmp_w1_v[j],
                           cd_cmp_w2_v[j], cd_w_o[j])
        h = _moe(h, moe_norm[i], router_group_w[i], router_group_b[i], router_expert_w[i], router_expert_b[i],
                 i, expert_w_gate, expert_w_up, expert_w_down)
        h = _mm([(h, 0, D)], [ple_gate_w[i].astype(BF16)], gain=ple_gate_norm[i], tm=512, tn=D,
                ple=(h, p[i].reshape(T, -1), ple_proj[i].astype(BF16)),
                out_gain=final_norm if i == depth - 1 else None, name="ple")
    return h.reshape(B, S, D)
```

```python
import functools
import math

import numpy as np
import jax
import jax.numpy as jnp
from jax import lax
from jax.experimental import pallas as pl
from jax.experimental.pallas import tpu as pltpu

F32 = jnp.float32
BF16 = jnp.bfloat16

HEAD_DIM = 128
ROPE_THETA = 10000.0
NORM_EPS = 1e-6
NEG_INF = -1e30
TAKEN = -3e38
A_HEADS, A_KV_HEADS, A_WINDOW = 8, 2, 128
B_HEADS, B_Q_LORA, B_KV_LORA, B_NOPE, B_ROPE, B_V = 8, 512, 256, 128, 64, 128
C_HEADS = 8
D_HEADS, D_KV_HEADS = 8, 2
D_CMP_LEN, D_CMP_STRIDE, D_SLC_LEN, D_SLC_TOPN, D_WINDOW = 32, 16, 64, 8, 512
FORCE_BONUS = 1e4
N_GROUPS, EXPERTS_PER_GROUP, TOP_K = 4, 8, 2
N_EXPERTS = N_GROUPS * EXPERTS_PER_GROUP

LANES = 128
VMEM_LIMIT_BYTES = 56 * 1024 * 1024
MOE_ROWS = 256
DMA_UNROLL = 8
FLASH_ROW_CHUNK = 128
LOG2E = math.log2(math.e)


def _params(*sem):
    return pltpu.CompilerParams(dimension_semantics=sem, vmem_limit_bytes=VMEM_LIMIT_BYTES)


def _tile(n, pref):
    t = min(n, pref)
    while n % t:
        t -= 1
    return t


def _pack_bf16_pairs(y):
    n = y.shape[1] // 2
    bits = lambda a: lax.bitcast_convert_type(a.astype(BF16).astype(F32), jnp.uint32)
    return (bits(y[:, :n]) >> 16) | (bits(y[:, n:]) & jnp.uint32(0xFFFF0000))


def _unpack_bf16_pairs(w):
    return (lax.bitcast_convert_type(w << 16, F32), lax.bitcast_convert_type(w & jnp.uint32(0xFFFF0000), F32))


def _rot128(x, c, s):
    return x * c + pltpu.roll(x, 64, 1) * s


def _rot_perms():
    i = np.arange(LANES)
    p128 = (i[:, None] == (i[None, :] + HEAD_DIM // 2) % LANES)
    half = B_ROPE // 2
    p64 = ((i[None, :] < half) & (i[:, None] == i[None, :] + half)) | (
        (i[None, :] >= half) & (i[None, :] < B_ROPE) & (i[:, None] == i[None, :] - half))
    return [jnp.asarray(p.astype(np.float32), BF16) for p in (p128, p64)]


def _tiles_store(ref, lead, words):
    rows, c = words.shape[0], words.shape[1] // LANES
    for j in range(c):
        ref[lead + (pl.ds(j, rows, stride=c), slice(None))] = words[:, j * LANES:(j + 1) * LANES]


def _tiles_load(ref, lead, rows, c):
    return jnp.concatenate([ref[lead + (pl.ds(j, rows, stride=c), slice(None))] for j in range(c)], axis=1)


def _mm_body(*refs, nx, has_gain, emit_xn, mode, resid_is_x, has_out_gain, col_prog, f32_from):
    it = iter(refs)
    x_refs = [next(it) for _ in range(nx)]
    g_ref = next(it) if has_gain else None
    w_refs = [next(it) for _ in range(nx)]
    r_ref = next(it) if mode in ("resid", "ple") and not resid_is_x else None
    p_ref = next(it) if mode == "ple" else None
    wp_ref = next(it) if mode == "ple" else None
    og_ref = next(it) if has_out_gain else None
    tab_refs = [next(it) for _ in range(7)] if col_prog is not None else None
    o_ref = next(it)
    xo_ref = next(it) if emit_xn else None
    of_ref = next(it) if f32_from is not None else None
    xn_ref = next(it) if has_gain else None

    if has_gain:
        @pl.when(pl.program_id(1) == 0)
        def _():
            x = x_refs[0][...].astype(F32)
            y = x * lax.rsqrt(jnp.mean(x * x, axis=-1, keepdims=True) + NORM_EPS) * g_ref[...]
            xn_ref[...] = y.astype(BF16)
            if emit_xn:
                _tiles_store(xo_ref, (), _pack_bf16_pairs(y))
        lhs = [xn_ref[...]]
    else:
        lhs = [x_ref[...].astype(BF16) for x_ref in x_refs]
    acc = None
    for a, w_ref in zip(lhs, w_refs):
        d = jnp.dot(a, w_ref[...], preferred_element_type=F32)
        acc = d if acc is None else acc + d
    if resid_is_x:
        r_ref = x_refs[0]
    if mode == "resid":
        acc = r_ref[...] + acc
    elif mode == "ple":
        pp = jnp.dot(p_ref[...].astype(BF16), wp_ref[...], preferred_element_type=F32)
        acc = r_ref[...] + pp * jax.nn.sigmoid(acc)
    if has_out_gain:
        acc = acc * lax.rsqrt(jnp.mean(acc * acc, axis=-1, keepdims=True) + NORM_EPS) * og_ref[...]
    if col_prog is None:
        o_ref[...] = acc.astype(o_ref.dtype)
        return
    j = pl.program_id(1)
    nb = o_ref.shape[1] // LANES
    tiles = {}
    for jj in range(len(col_prog) // nb):
        tiles.setdefault(tuple(col_prog[jj * nb:(jj + 1) * nb]), []).append(jj)
    for prog, jjs in tiles.items():
        @pl.when(functools.reduce(jnp.logical_or, [j == jj for jj in jjs]))
        def _():
            c128, s128, c64, s64a, s64b, p128, p64 = tab_refs
            for b, (kind, scale) in enumerate(prog):
                x = acc[:, b * LANES:(b + 1) * LANES]
                if kind == "rot128":
                    x = x * c128[...] + jnp.dot(x.astype(BF16), p128[...], preferred_element_type=F32) * s128[...]
                elif kind == "rot64":
                    x = x * c64[...] + jnp.dot(x.astype(BF16), p64[...], preferred_element_type=F32) * (
                        s64a[...] + s64b[...])
                if scale != 1.0:
                    x = x * scale
                o_ref[:, b * LANES:(b + 1) * LANES] = x.astype(o_ref.dtype)
    if f32_from is not None:
        @pl.when(j >= f32_from)
        def _():
            of_ref[...] = acc


def _mm(xs, ws, *, name, gain=None, out_dtype=F32, tm=1024, tn=512, resid=None, ple=None,
        emit_xn=False, out_gain=None, col_prog=None, tables=None, f32_from=None):
    M = xs[0][0].shape[0]
    N = ws[0].shape[1]
    tm, tn = _tile(M, tm), _tile(N, tn)
    nx = len(xs)
    has_gain = gain is not None
    mode = "ple" if ple is not None else ("resid" if resid is not None else "none")
    r_arr = ple[0] if mode == "ple" else resid
    resid_is_x = r_arr is xs[0][0] and tn == N == xs[0][2] and xs[0][1] == 0
    assert out_gain is None or tn == N
    args, in_specs = [], []
    for arr, cb, K in xs:
        args.append(arr)
        in_specs.append(pl.BlockSpec((tm, K), lambda i, j, cb=cb: (i, cb)))
    if has_gain:
        K0 = xs[0][2]
        args.append(gain.reshape(1, K0).astype(F32))
        in_specs.append(pl.BlockSpec((1, K0), lambda i, j: (0, 0)))
    for (arr, cb, K), w in zip(xs, ws):
        args.append(w)
        in_specs.append(pl.BlockSpec((K, tn), lambda i, j: (0, j)))
    if mode in ("resid", "ple") and not resid_is_x:
        args.append(r_arr)
        in_specs.append(pl.BlockSpec((tm, tn), lambda i, j: (i, j)))
    if mode == "ple":
        _, p, wp = ple
        args += [p, wp]
        in_specs += [pl.BlockSpec((tm, p.shape[1]), lambda i, j: (i, 0)),
                     pl.BlockSpec((p.shape[1], tn), lambda i, j: (0, j))]
    if out_gain is not None:
        args.append(out_gain.reshape(1, N).astype(F32))
        in_specs.append(pl.BlockSpec((1, N), lambda i, j: (0, 0)))
    if col_prog is not None:
        assert len(col_prog) * LANES == N
        args += list(tables) + _rot_perms()
        in_specs += [pl.BlockSpec((tm, LANES), lambda i, j: (i, 0))] * 5
        in_specs += [pl.BlockSpec((LANES, LANES), lambda i, j: (0, 0))] * 2
    out_shape = [jax.ShapeDtypeStruct((M, N), out_dtype)]
    out_specs = [pl.BlockSpec((tm, tn), lambda i, j: (i, j))]
    if emit_xn:
        c = xs[0][2] // 2 // LANES
        out_shape.append(jax.ShapeDtypeStruct((M * c, LANES), jnp.uint32))
        out_specs.append(pl.BlockSpec((tm * c, LANES), lambda i, j: (i, 0)))
    if f32_from is not None:
        out_shape.append(jax.ShapeDtypeStruct((M, N - f32_from * tn), F32))
        out_specs.append(pl.BlockSpec((tm, tn), lambda i, j: (i, jnp.maximum(j - f32_from, 0))))
    scratch = [pltpu.VMEM((tm, xs[0][2]), BF16)] if has_gain else []
    res = pl.pallas_call(
        functools.partial(_mm_body, nx=nx, has_gain=has_gain, emit_xn=emit_xn, mode=mode, resid_is_x=resid_is_x,
                          has_out_gain=out_gain is not None,
                          col_prog=None if col_prog is None else tuple(col_prog), f32_from=f32_from),
        out_shape=out_shape, grid=(M // tm, N // tn), in_specs=in_specs, out_specs=out_specs,
        scratch_shapes=scratch, compiler_params=_params("arbitrary", "arbitrary"), name=name,
    )(*args)
    return res if len(res) > 1 else res[0]


def _rope_tables(positions):
    def tables(dim):
        inv = 1.0 / (ROPE_THETA ** (jnp.arange(0, dim, 2, dtype=F32) / dim))
        ang = positions.astype(F32)[..., None] * inv
        return jnp.cos(ang), jnp.sin(ang)

    lead = positions.shape
    c, s = tables(HEAD_DIM)
    c128 = jnp.concatenate([c, c], -1)
    s128 = jnp.concatenate([-s, s], -1)
    c, s = tables(B_ROPE)
    z = jnp.zeros_like(c)
    c64 = jnp.concatenate([c, c, z, z], -1)
    s64a = jnp.concatenate([-s, z, z, z], -1)
    s64b = jnp.concatenate([z, s, z, z], -1)
    n = int(np.prod(lead))
    return [t.reshape(n, LANES) for t in (c128, s128, c64, s64a, s64b)]


def _pair_tables(S, tq, tk, window):
    qi_l, ki_l, fl_l, mk_l, masks, ids = [], [], [], [], [], {}
    for qi in range(S // tq):
        q0, q1 = qi * tq, (qi + 1) * tq - 1
        ks = []
        for ki in range(S // tk):
            k0, k1 = ki * tk, (ki + 1) * tk - 1
            if k0 > q1 or (window is not None and q0 - k1 >= window):
                continue
            full = k1 <= q0 and (window is None or q1 - k0 < window)
            mid = 0
            if not full:
                delta = q0 - k0
                if delta not in ids:
                    rel = np.arange(tq)[:, None] + delta - np.arange(tk)[None, :]
                    ok = (rel >= 0) if window is None else ((rel >= 0) & (rel < window))
                    masks.append(np.where(ok, 0.0, NEG_INF).astype(np.float32))
                    ids[delta] = len(masks)
                mid = ids[delta]
            ks.append((ki, mid))
        for n, (ki, mid) in enumerate(ks):
            qi_l.append(qi)
            ki_l.append(ki)
            fl_l.append((1 if n == 0 else 0) | (2 if n == len(ks) - 1 else 0))
            mk_l.append(mid)
    if not masks:
        masks.append(np.zeros((tq, tk), np.float32))
    tabs = [jnp.asarray(np.array(a, np.int32)) for a in (qi_l, ki_l, fl_l, mk_l)]
    return tabs, jnp.asarray(np.stack(masks))


def _flash_body(qi_t, ki_t, fl_t, mk_t, *refs, nq, nkv, q_src, k_src, v_src, tq, rc, causal_diag,
                has_sink, has_bias, has_sel):
    nc = len(q_src)
    it = iter(refs)
    q_refs = [next(it) for _ in range(nq)]
    kv_refs = [next(it) for _ in range(nkv)]
    mask_ref = next(it)
    sink_ref = next(it) if has_sink else None
    cum_ref = next(it) if has_bias else None
    cumt_ref = next(it) if has_bias else None
    sel_ref = next(it) if has_sel else None
    exp_ref = next(it) if has_sel else None
    o_ref = next(it)
    m_sc, acc_sc = next(it), next(it)
    cq_sc = next(it) if has_bias else None

    step_id = pl.program_id(2)
    fl, mk = fl_t[step_id], mk_t[step_id]

    def cat(refs_, src, rows=slice(None)):
        xs = [refs_[pi][0, rows, off:off + LANES] for pi, off in src]
        return xs[0] if len(xs) == 1 else jnp.concatenate(xs, axis=-1)

    @pl.when((fl & 1) != 0)
    def _init():
        m_sc[...] = jnp.full(m_sc.shape, NEG_INF, F32)
        acc_sc[...] = jnp.zeros(acc_sc.shape, F32)
        if has_bias:
            lane = lax.broadcasted_iota(jnp.int32, (tq, LANES), 1)
            for g in range(nc):
                head = pl.program_id(1) * nc + g
                cq_sc[g] = jnp.sum(jnp.where(lane == head, cum_ref[0], 0.0), axis=-1, keepdims=True)

    tk = kv_refs[0].shape[1]
    ones = jnp.ones((tk, LANES), BF16)

    def step(masked):
        ks = [cat(kv_refs, k_src[g]) for g in range(nc)]
        vs = [jnp.concatenate([kv_refs[pi][0, :, off:off + LANES], ones], axis=1) for pi, off in v_src]
        for r in range(tq // rc):
            rows = slice(r * rc, (r + 1) * rc)
            nk = min((r + 1) * rc, tk) if (masked and causal_diag) else tk
            reps = nk // LANES
            add = mask_ref[mk - 1, rows, :nk] if masked else None
            if has_sel:
                hidden = (jnp.dot(sel_ref[0, 0, rows, :], exp_ref[0, :, :nk], preferred_element_type=F32)
                          - 1.0) * (-NEG_INF)
                add = hidden if add is None else add + hidden
            for g in range(nc):
                q = cat(q_refs, q_src[g], rows)
                s = lax.dot_general(q, ks[g][:nk], (((1,), (1,)), ((), ())), preferred_element_type=F32)
                if has_bias:
                    s = s + (cq_sc[g, rows, :] - cumt_ref[0, g, :, :nk])
                if add is not None:
                    s = s + add
                m_prev = m_sc[g, rows, :]
                m_new = jnp.maximum(m_prev, jnp.max(s, axis=-1, keepdims=True))
                p = jnp.exp2(s - (jnp.concatenate([m_new] * reps, axis=1) if reps > 1 else m_new))
                alpha = jnp.exp2(m_prev - m_new)
                acc_sc[g, rows, :] = (jnp.concatenate([alpha, alpha], axis=1) * acc_sc[g, rows, :]
                                      + jnp.dot(p.astype(BF16), vs[g][:nk], preferred_element_type=F32))
                m_sc[g, rows, :] = m_new

    @pl.when(mk != 0)
    def _():
        step(True)

    @pl.when(mk == 0)
    def _():
        step(False)

    @pl.when((fl & 2) != 0)
    def _finish():
        for g in range(nc):
            m, acc, l = m_sc[g], acc_sc[g, :, :LANES], acc_sc[g, :, LANES:]
            if has_sink:
                sk = sink_ref[0, g:g + 1, 0:1] * LOG2E
                m_f = jnp.maximum(m, sk)
                w = jnp.exp2(m - m_f)
                l = l * w + jnp.exp2(sk - m_f)
                acc = acc * w
            o_ref[0, :, g * LANES:(g + 1) * LANES] = (acc / l).astype(o_ref.dtype)


def _flash(q_parts, kv_parts, q_src, k_src, v_src, *, name, B, S, n_steps, tq, tk, window=None, sinks=None,
           cum=None, cumt=None, sel=None, out_dtype=BF16, rc=FLASH_ROW_CHUNK):
    nc = len(q_src)
    tq, tk = _tile(S, tq), _tile(S, tk)
    tabs, masks = _pair_tables(S, tq, tk, window)
    npairs = int(tabs[0].shape[0])
    args, in_specs = [], []
    for arr, width, cf in q_parts:
        args.append(arr)
        in_specs.append(pl.BlockSpec((1, tq, width), lambda b, h, s, qt, kt, ft, mt, cf=cf: (b, qt[s], cf(h))))
    for arr, width, cf in kv_parts:
        args.append(arr)
        in_specs.append(pl.BlockSpec((1, tk, width), lambda b, h, s, qt, kt, ft, mt, cf=cf: (b, kt[s], cf(h))))
    args.append(masks)
    in_specs.append(pl.BlockSpec(masks.shape, lambda b, h, s, qt, kt, ft, mt: (0, 0, 0)))
    if sinks is not None:
        args.append(jnp.broadcast_to(sinks.astype(F32).reshape(n_steps, nc, 1), (n_steps, nc, LANES)))
        in_specs.append(pl.BlockSpec((1, nc, LANES), lambda b, h, s, qt, kt, ft, mt: (h, 0, 0)))
    if cum is not None:
        args += [cum, cumt]
        in_specs += [pl.BlockSpec((1, tq, LANES), lambda b, h, s, qt, kt, ft, mt: (b, qt[s], 0)),
                     pl.BlockSpec((1, nc, 1, tk), lambda b, h, s, qt, kt, ft, mt: (b, h, 0, kt[s]))]
    if sel is not None:
        per = tk // D_SLC_LEN
        e = np.zeros((S // tk, LANES, tk), np.float32)
        for ki in range(S // tk):
            e[ki, ki * per + np.arange(tk) // D_SLC_LEN, np.arange(tk)] = 1.0
        args += [sel, jnp.asarray(e, BF16)]
        in_specs += [pl.BlockSpec((1, 1, tq, LANES), lambda b, h, s, qt, kt, ft, mt: (b, h, qt[s], 0)),
                     pl.BlockSpec((1, LANES, tk), lambda b, h, s, qt, kt, ft, mt: (kt[s], 0, 0))]
    scratch = [pltpu.VMEM((nc, tq, LANES), F32), pltpu.VMEM((nc, tq, 2 * LANES), F32)]
    if cum is not None:
        scratch.append(pltpu.VMEM((nc, tq, 1), F32))
    body = functools.partial(_flash_body, nq=len(q_parts), nkv=len(kv_parts), q_src=q_src, k_src=k_src,
                             v_src=v_src, tq=tq, rc=_tile(tq, rc), causal_diag=window is None and tq == tk,
                             has_sink=sinks is not None,
                             has_bias=cum is not None, has_sel=sel is not None)
    return pl.pallas_call(
        body, out_shape=jax.ShapeDtypeStruct((B, S, n_steps * nc * LANES), out_dtype),
        grid_spec=pltpu.PrefetchScalarGridSpec(
            num_scalar_prefetch=4, grid=(B, n_steps, npairs), in_specs=in_specs,
            out_specs=pl.BlockSpec((1, tq, nc * LANES), lambda b, h, s, qt, kt, ft, mt: (b, qt[s], h)),
            scratch_shapes=scratch),
        compiler_params=_params("arbitrary", "arbitrary", "arbitrary"), name=name,
    )(*tabs, *args)


def _gqa_flash(q, k, v, *, name, B, S, Hk, G, **kw):
    at = lambda first: (lambda h: first + h)
    return _flash([(q[0], G * LANES, at(q[1]))], [(k[0], LANES, at(k[1])), (v[0], LANES, at(v[1]))],
                  [[(0, g * LANES)] for g in range(G)], [[(0, 0)]] * G, [(1, 0)] * G,
                  name=name, B=B, S=S, n_steps=Hk, **kw)


def _cum_body(y_ref, b_ref, tri_ref, o_ref, carry):
    @pl.when(pl.program_id(1) == 0)
    def _():
        carry[...] = jnp.zeros(carry.shape, F32)
    x = y_ref[0] + b_ref[...]
    logf = jnp.minimum(x, 0.0) - jnp.log1p(jnp.exp(-jnp.abs(x)))
    cum = jnp.dot(tri_ref[...], logf, preferred_element_type=F32, precision=lax.Precision.HIGHEST) + carry[...]
    o_ref[0] = cum * LOG2E
    carry[...] = cum[-1:, :]


def _forget_cum(y3, col_block, bias_row, *, ts=512):
    B, S, _ = y3.shape
    ts = _tile(S, ts)
    tri = jnp.asarray(np.tril(np.ones((ts, ts), np.float32)))
    return pl.pallas_call(
        _cum_body, out_shape=jax.ShapeDtypeStruct((B, S, LANES), F32), grid=(B, S // ts),
        in_specs=[pl.BlockSpec((1, ts, LANES), lambda b, s: (b, s, col_block)),
                  pl.BlockSpec((1, LANES), lambda b, s: (0, 0)),
                  pl.BlockSpec((ts, ts), lambda b, s: (0, 0))],
        out_specs=pl.BlockSpec((1, ts, LANES), lambda b, s: (b, s, 0)),
        scratch_shapes=[pltpu.VMEM((1, LANES), F32)],
        compiler_params=_params("arbitrary", "arbitrary"), name="forget_cum",
    )(y3, bias_row, tri)


def _compress_body(*refs, rope, nc, width, col0):
    if rope:
        z_ref, pe_ref, w1_ref, w2_ref, c_ref, s_ref, o_ref = refs
    else:
        z_ref, pe_ref, w1_ref, w2_ref, o_ref = refs
    half = D_CMP_LEN // 2
    for hk in range(D_KV_HEADS):
        u = jnp.zeros((nc, w1_ref.shape[1]), F32)
        v = jnp.zeros((nc, w1_ref.shape[1]), F32)
        for l in range(half):
            first = l * width + col0 + hk * HEAD_DIM
            z = z_ref[0, :, first:first + HEAD_DIM]
            zu = (z + pe_ref[l:l + 1, :]).astype(BF16)
            zv = (z + pe_ref[half + l:half + l + 1, :]).astype(BF16)
            u = u + jnp.dot(zu, w1_ref[l * HEAD_DIM:(l + 1) * HEAD_DIM, :], preferred_element_type=F32)
            v = v + jnp.dot(zv, w1_ref[(half + l) * HEAD_DIM:(half + l + 1) * HEAD_DIM, :],
                            preferred_element_type=F32)
        pre = u + pltpu.roll(v, nc - 1, 0)
        hid = jax.nn.gelu(pre, approximate=True)
        out = jnp.dot(hid.astype(BF16), w2_ref[...], preferred_element_type=F32)
        if rope:
            out = _rot128(out, c_ref[0], s_ref[0])
        o_ref[0, hk] = out.astype(o_ref.dtype)


def _compress(z, col0, pe, w1, w2, rope_tabs=None):
    B, S, W = z.shape
    nc = S // D_CMP_STRIDE
    zc = z.reshape(B, nc, D_CMP_STRIDE * W)
    args = [zc, pe.astype(F32), w1.astype(BF16), w2.astype(BF16)]
    in_specs = [pl.BlockSpec((1, nc, D_CMP_STRIDE * W), lambda b: (b, 0, 0)),
                pl.BlockSpec(pe.shape, lambda b: (0, 0)),
                pl.BlockSpec(w1.shape, lambda b: (0, 0)),
                pl.BlockSpec(w2.shape, lambda b: (0, 0))]
    if rope_tabs is not None:
        args += list(rope_tabs)
        in_specs += [pl.BlockSpec((1, nc, LANES), lambda b: (b, 0, 0))] * 2
    return pl.pallas_call(
        functools.partial(_compress_body, rope=rope_tabs is not None, nc=nc, width=W, col0=col0),
        out_shape=jax.ShapeDtypeStruct((B, D_KV_HEADS, nc, HEAD_DIM), BF16), grid=(B,),
        in_specs=in_specs, out_specs=pl.BlockSpec((1, D_KV_HEADS, nc, HEAD_DIM), lambda b: (b, 0, 0, 0)),
        compiler_params=_params("arbitrary"), name="nsa_compress",
    )(*args)


def _cmp_attn_body(q_ref, k_ref, v_ref, ov_ref, o_ref, sel_ref, *, G, tq, nc, n_cmp, n_slc, topn):
    qi = pl.program_id(2)
    t = qi * tq + lax.broadcasted_iota(jnp.int32, (tq, nc), 0)
    c = lax.broadcasted_iota(jnp.int32, (tq, nc), 1)
    valid = (c * D_CMP_STRIDE + (D_CMP_LEN - 1) <= t) & (c < n_cmp)
    k = k_ref[0, 0]
    v = v_ref[0, 0]
    psum = jnp.zeros((tq, nc), F32)
    for g in range(G):
        q = q_ref[0, :, g * LANES:(g + 1) * LANES]
        s = lax.dot_general(q, k, (((1,), (1,)), ((), ())), preferred_element_type=F32)
        s = jnp.where(valid, s, NEG_INF)
        e = jnp.where(valid, jnp.exp2(s - jnp.max(s, axis=-1, keepdims=True)), 0.0)
        p = e / jnp.maximum(jnp.sum(e, axis=-1, keepdims=True), jnp.finfo(F32).tiny)
        o_ref[0, :, g * LANES:(g + 1) * LANES] = jnp.dot(
            p.astype(BF16), v, preferred_element_type=F32).astype(o_ref.dtype)
        psum = psum + p
    imp = lax.dot_general(ov_ref[...], psum, (((1,), (1,)), ((), ())), preferred_element_type=F32,
                          precision=lax.Precision.HIGHEST)
    blk = lax.broadcasted_iota(jnp.int32, (LANES, tq), 0)
    tcol = qi * tq + lax.broadcasted_iota(jnp.int32, (LANES, tq), 1)
    cur = jnp.right_shift(tcol, int(math.log2(D_SLC_LEN)))
    forced = (blk == 0) | (blk == cur) | (blk == cur - 1)
    imp = jnp.where(blk * D_SLC_LEN > tcol, NEG_INF, imp + jnp.where(forced, FORCE_BONUS, 0.0))
    imp = jnp.where(blk >= n_slc, TAKEN, imp)
    chosen = jnp.zeros((LANES, tq), F32)
    blk_f = blk.astype(F32)
    for _ in range(topn):
        mx = jnp.max(imp, axis=0, keepdims=True)
        idx = jnp.min(jnp.where(imp == mx, blk_f, float(LANES)), axis=0, keepdims=True)
        hit = blk_f == idx
        chosen = jnp.where(hit, 1.0, chosen)
        imp = jnp.where(hit, TAKEN, imp)
    sel_ref[0, 0] = chosen.T.astype(sel_ref.dtype)


def _cmp_attn(q, q_first, k_cmp, v_cmp, *, tq=256):
    B, S, _ = q.shape
    Hk, G = D_KV_HEADS, D_HEADS // D_KV_HEADS
    nc = S // D_CMP_STRIDE
    n_cmp = (S - D_CMP_LEN) // D_CMP_STRIDE + 1
    n_slc = S // D_SLC_LEN
    tq = _tile(S, tq)
    c0 = np.arange(nc) * D_CMP_STRIDE
    s0 = np.arange(LANES) * D_SLC_LEN
    ov = ((c0[:, None] < (s0 + D_SLC_LEN)[None, :]) & ((c0 + D_CMP_LEN)[:, None] > s0[None, :])
          & (np.arange(nc) < n_cmp)[:, None] & (np.arange(LANES) < n_slc)[None, :]).astype(np.float32)
    body = functools.partial(_cmp_attn_body, G=G, tq=tq, nc=nc, n_cmp=n_cmp, n_slc=n_slc,
                             topn=min(D_SLC_TOPN, n_slc))
    return pl.pallas_call(
        body,
        out_shape=[jax.ShapeDtypeStruct((B, S, Hk * G * LANES), BF16),
                   jax.ShapeDtypeStruct((B, Hk, S, LANES), BF16)],
        grid=(B, Hk, S // tq),
        in_specs=[pl.BlockSpec((1, tq, G * LANES), lambda b, h, i: (b, i, q_first + h)),
                  pl.BlockSpec((1, 1, nc, LANES), lambda b, h, i: (b, h, 0, 0)),
                  pl.BlockSpec((1, 1, nc, LANES), lambda b, h, i: (b, h, 0, 0)),
                  pl.BlockSpec((LANES, nc), lambda b, h, i: (0, 0))],
        out_specs=[pl.BlockSpec((1, tq, G * LANES), lambda b, h, i: (b, i, h)),
                   pl.BlockSpec((1, 1, tq, LANES), lambda b, h, i: (b, h, i, 0))],
        compiler_params=_params("arbitrary", "arbitrary", "arbitrary"), name="nsa_cmp_attn",
    )(q, k_cmp, v_cmp, jnp.asarray(ov.T))


def _gate_body(y_ref, a_ref, b_ref, c_ref, o_ref, *, lane0):
    g = jax.nn.sigmoid(y_ref[...])
    for h in range(D_HEADS):
        cols = slice(h * LANES, (h + 1) * LANES)
        ga = g[:, lane0 + h:lane0 + h + 1]
        gb = g[:, lane0 + D_HEADS + h:lane0 + D_HEADS + h + 1]
        gc = g[:, lane0 + 2 * D_HEADS + h:lane0 + 2 * D_HEADS + h + 1]
        o_ref[:, cols] = (ga * a_ref[:, cols] + gb * b_ref[:, cols] + gc * c_ref[:, cols]).astype(o_ref.dtype)


def _nsa_gate(y, col_block, lane0, o_cmp, o_slc, o_win, *, tm=512):
    M = y.shape[0]
    W = o_cmp.shape[1]
    tm = _tile(M, tm)
    row = lambda i: (i, 0)
    return pl.pallas_call(
        functools.partial(_gate_body, lane0=lane0), out_shape=jax.ShapeDtypeStruct((M, W), BF16),
        grid=(M // tm,),
        in_specs=[pl.BlockSpec((tm, LANES), lambda i: (i, col_block))] + [pl.BlockSpec((tm, W), row)] * 3,
        out_specs=pl.BlockSpec((tm, W), row), compiler_params=_params("arbitrary"), name="nsa_gate",
    )(y, o_cmp, o_slc, o_win)


def _route_body(lg_ref, b_ref, tri_ref, e_ref, w_ref, pos_ref, cnt_ref, carry, *, tm):
    @pl.when(pl.program_id(0) == 0)
    def _():
        carry[...] = jnp.zeros(carry.shape, F32)
    lane = lax.broadcasted_iota(jnp.int32, (tm, LANES), 1)
    logits = lg_ref[...] + b_ref[...]
    gl = jnp.where(lane < N_GROUPS, logits, -jnp.inf)
    gmax = jnp.max(gl, axis=-1, keepdims=True)
    g_val = 1.0 / jnp.sum(jnp.exp(gl - gmax), axis=-1, keepdims=True)
    g_idx = jnp.min(jnp.where(gl == gmax, lane, LANES), axis=-1, keepdims=True)
    lo = N_GROUPS + EXPERTS_PER_GROUP * g_idx
    el = jnp.where((lane >= lo) & (lane < lo + EXPERTS_PER_GROUP), logits, -jnp.inf)
    e1 = jnp.max(el, axis=-1, keepdims=True)
    i1 = jnp.min(jnp.where(el == e1, lane, LANES), axis=-1, keepdims=True)
    el2 = jnp.where(lane == i1, -jnp.inf, el)
    e2 = jnp.max(el2, axis=-1, keepdims=True)
    i2 = jnp.min(jnp.where(el2 == e2, lane, LANES), axis=-1, keepdims=True)
    r = jnp.exp(e2 - e1)
    w1 = g_val / (1.0 + r)
    w2 = w1 * r
    x1, x2 = i1 - N_GROUPS, i2 - N_GROUPS
    e_ref[...] = jnp.where(lane == 0, x1, jnp.where(lane == 1, x2, 0))
    w_ref[...] = jnp.where(lane == 0, w1, jnp.where(lane == 1, w2, 0.0))
    hot1 = lane == x1
    hot2 = lane == x2
    both = jnp.where(hot1 | hot2, 1.0, 0.0)
    before = jnp.dot(tri_ref[...], both.astype(BF16), preferred_element_type=F32) + carry[...]
    p1 = jnp.sum(jnp.where(hot1, before, 0.0), axis=-1, keepdims=True)
    p2 = jnp.sum(jnp.where(hot2, before, 0.0), axis=-1, keepdims=True)
    pos_ref[...] = jnp.where(lane == 0, p1, jnp.where(lane == 1, p2, 0.0)).astype(jnp.int32)
    carry[...] = carry[...] + jnp.sum(both, axis=0, keepdims=True)
    cnt_ref[...] = carry[...].astype(jnp.int32)


def _route(logits, bias_row, *, tm=512):
    T = logits.shape[0]
    tm = _tile(T, tm)
    tri = jnp.asarray(np.tril(np.ones((tm, tm), np.float32), -1), BF16)
    row = lambda i: (i, 0)
    fixed = lambda i: (0, 0)
    return pl.pallas_call(
        functools.partial(_route_body, tm=tm),
        out_shape=[jax.ShapeDtypeStruct((T, LANES), jnp.int32), jax.ShapeDtypeStruct((T, LANES), F32),
                   jax.ShapeDtypeStruct((T, LANES), jnp.int32), jax.ShapeDtypeStruct((1, LANES), jnp.int32)],
        grid=(T // tm,),
        in_specs=[pl.BlockSpec((tm, LANES), row), pl.BlockSpec((1, LANES), fixed), pl.BlockSpec((tm, tm), fixed)],
        out_specs=[pl.BlockSpec((tm, LANES), row)] * 3 + [pl.BlockSpec((1, LANES), fixed)],
        scratch_shapes=[pltpu.VMEM((1, LANES), F32)],
        compiler_params=_params("arbitrary"), name="moe_route",
    )(logits, bias_row, tri)


def _dispatch_body(d_ref, ends_ref, pad_ref, nu_ref, x_ref, out_ref, zero_sc, sem, zsem, *, tm, nblk, c):
    base = pl.program_id(0) * tm

    @pl.when(pl.program_id(0) == 0)
    def _():
        zero_sc[...] = jnp.zeros(zero_sc.shape, zero_sc.dtype)

        def zero_block(first_row):
            return pltpu.make_async_copy(
                zero_sc, out_ref.at[pl.ds(pl.multiple_of(first_row * c, MOE_ROWS * c), MOE_ROWS * c)], zsem)

        for wait in (False, True):
            for e in range(N_EXPERTS):
                for live, first_row in ((pad_ref[e] > 0, ends_ref[e] - MOE_ROWS),
                                        (nu_ref[0] + e < nblk, (nu_ref[0] + e) * MOE_ROWS)):
                    @pl.when(live)
                    def _():
                        zero_block(first_row).wait() if wait else zero_block(first_row).start()

    def issue(r, carry):
        for k in range(TOP_K):
            row = d_ref[(base + r) * TOP_K + k]
            pltpu.make_async_copy(x_ref.at[pl.ds(pl.multiple_of(r * c, c), c)],
                                  out_ref.at[pl.ds(pl.multiple_of(row * c, c), c)], sem).start(priority=k)
        return carry

    lax.fori_loop(0, tm, issue, 0, unroll=DMA_UNROLL)
    for _ in range(TOP_K):
        pltpu.make_async_copy(x_ref, out_ref.at[pl.ds(0, tm * c)], sem).wait()


def _dispatch(xn, dest, ends, padded, n_used, n_rows, c, *, tm=512):
    T = xn.shape[0] // c
    tm = _tile(T, tm)
    return pl.pallas_call(
        functools.partial(_dispatch_body, tm=tm, nblk=n_rows // MOE_ROWS, c=c),
        out_shape=jax.ShapeDtypeStruct((n_rows * c, LANES), xn.dtype),
        grid_spec=pltpu.PrefetchScalarGridSpec(
            num_scalar_prefetch=4, grid=(T // tm,),
            in_specs=[pl.BlockSpec((tm * c, LANES), lambda i, *_: (i, 0))],
            out_specs=pl.BlockSpec(memory_space=pl.ANY),
            scratch_shapes=[pltpu.VMEM((MOE_ROWS * c, LANES), xn.dtype), pltpu.SemaphoreType.DMA(()),
                            pltpu.SemaphoreType.DMA(())]),
        compiler_params=pltpu.CompilerParams(dimension_semantics=("arbitrary",), has_side_effects=True,
                                             vmem_limit_bytes=VMEM_LIMIT_BYTES, disable_bounds_checks=True),
        name="moe_dispatch",
    )(dest, ends, padded, n_used, xn)


def _expert_body(nb_ref, first_ref, nu_ref, xb_ref, wg_ref, wu_ref, wd_ref, yb_ref, wg_sc, wu_sc, wd_sc, xbuf,
                 obuf, sem_in, sem_out, state, *, nblk, c):
    e = pl.program_id(0)
    last = pl.num_programs(0) - 1
    nb, first = nb_ref[e], first_ref[e]

    @pl.when(e == 0)
    def _():
        for i in range(3):
            state[i] = 0

    def rows(first_block, b):
        return pl.ds(pl.multiple_of((first_block + b) * (MOE_ROWS * c), MOE_ROWS * c), MOE_ROWS * c)

    def fetch(first_block, b, slot):
        return pltpu.make_async_copy(xb_ref.at[rows(first_block, b)], xbuf.at[slot], sem_in.at[slot])

    def flush(b, slot):
        return pltpu.make_async_copy(obuf.at[slot], yb_ref.at[rows(first, b)], sem_out.at[slot])

    def drain(slot):
        @pl.when(state[slot] != 0)
        def _():
            pltpu.make_async_copy(obuf.at[slot], yb_ref.at[pl.ds(0, MOE_ROWS * c)], sem_out.at[slot]).wait()
            state[slot] = 0

    @pl.when(nb > 0)
    def _():
        @pl.when(state[2] == 0)
        def _():
            fetch(first, 0, 0).start()

        state[2] = 0
        wg_sc[...] = wg_ref[0, 0].astype(BF16)
        wu_sc[...] = wu_ref[0, 0].astype(BF16)
        wd_sc[...] = wd_ref[0, 0].astype(BF16)

        def block(b, carry):
            slot = lax.rem(b, 2)

            @pl.when(b + 1 < nb)
            def _():
                fetch(first, b + 1, 1 - slot).start()

            fetch(first, b, slot).wait()
            drain(slot)
            x = jnp.concatenate(_unpack_bf16_pairs(_tiles_load(xbuf, (slot,), MOE_ROWS, c)), axis=1).astype(BF16)
            gate = jnp.dot(x, wg_sc[...], preferred_element_type=F32)
            up = jnp.dot(x, wu_sc[...], preferred_element_type=F32)
            hid = (gate * jax.nn.sigmoid(gate) * up).astype(BF16)
            _tiles_store(obuf, (slot,), _pack_bf16_pairs(jnp.dot(hid, wd_sc[...], preferred_element_type=F32)))
            flush(b, slot).start()
            state[slot] = 1
            return carry

        lax.fori_loop(0, nb, block, 0)
        nxt = jnp.minimum(e + 1, last)

        @pl.when((e < last) & (nb_ref[nxt] > 0))
        def _():
            fetch(first_ref[nxt], 0, 0).start()
            state[2] = 1

    @pl.when(e == last)
    def _():
        drain(0)
        drain(1)
        obuf[0] = jnp.zeros(obuf.shape[1:], obuf.dtype)

        def tail(t):
            return pltpu.make_async_copy(obuf.at[0], yb_ref.at[rows(nu_ref[0], t)], sem_out.at[0])

        for wait in (False, True):
            for t in range(N_EXPERTS):
                @pl.when(nu_ref[0] + t < nblk)
                def _():
                    tail(t).wait() if wait else tail(t).start()


def _experts(xb, n_blocks, first_block, n_used, layer, w_gate, w_up, w_down):
    D, Hd = w_gate.shape[2], w_gate.shape[3]
    c = D // 2 // LANES
    nblk = xb.shape[0] // (MOE_ROWS * c)
    wspec = lambda shape: pl.BlockSpec((1, 1) + shape, lambda e, *_: (layer, e, 0, 0))
    return pl.pallas_call(
        functools.partial(_expert_body, nblk=nblk, c=c), out_shape=jax.ShapeDtypeStruct(xb.shape, xb.dtype),
        grid_spec=pltpu.PrefetchScalarGridSpec(
            num_scalar_prefetch=3, grid=(w_gate.shape[1],),
            in_specs=[pl.BlockSpec(memory_space=pl.ANY), wspec((D, Hd)), wspec((D, Hd)), wspec((Hd, D))],
            out_specs=pl.BlockSpec(memory_space=pl.ANY),
            scratch_shapes=[pltpu.VMEM((D, Hd), BF16), pltpu.VMEM((D, Hd), BF16), pltpu.VMEM((Hd, D), BF16),
                            pltpu.VMEM((2, MOE_ROWS * c, LANES), xb.dtype),
                            pltpu.VMEM((2, MOE_ROWS * c, LANES), xb.dtype),
                            pltpu.SemaphoreType.DMA((2,)), pltpu.SemaphoreType.DMA((2,)),
                            pltpu.SMEM((3,), jnp.int32)]),
        compiler_params=pltpu.CompilerParams(dimension_semantics=("arbitrary",), has_side_effects=True,
                                             vmem_limit_bytes=VMEM_LIMIT_BYTES),
        name="moe_experts",
    )(n_blocks, first_block, n_used, xb, w_gate, w_up, w_down)


def _collect_body(d_ref, h_ref, w_ref, yb_ref, o_ref, buf_a, buf_b, sem, *, tm, nsteps, c):
    i = pl.program_id(0)
    slot = lax.rem(i, 2)

    def fetch(step, slot_):
        base = step * tm

        def issue(r, carry):
            for k, buf in enumerate((buf_a, buf_b)):
                row = d_ref[(base + r) * TOP_K + k]
                pltpu.make_async_copy(yb_ref.at[pl.ds(pl.multiple_of(row * c, c), c)],
                                      buf.at[slot_, pl.ds(pl.multiple_of(r * c, c), c)],
                                      sem.at[slot_]).start(priority=k)
            return carry

        lax.fori_loop(0, tm, issue, 0, unroll=DMA_UNROLL)

    @pl.when(i == 0)
    def _():
        fetch(0, 0)

    @pl.when(i + 1 < nsteps)
    def _():
        fetch(i + 1, 1 - slot)

    for buf in (buf_a, buf_b):
        pltpu.make_async_copy(yb_ref.at[pl.ds(0, tm * c)], buf.at[slot], sem.at[slot]).wait()
    w = w_ref[...]
    half = h_ref.shape[1] // 2
    for part, ya, yb in zip((slice(0, half), slice(half, None)),
                            _unpack_bf16_pairs(_tiles_load(buf_a, (slot,), tm, c)),
                            _unpack_bf16_pairs(_tiles_load(buf_b, (slot,), tm, c))):
        o_ref[:, part] = h_ref[:, part] + w[:, 0:1] * ya + w[:, 1:2] * yb


def _collect(h, yb, dest, wts, *, tm=256):
    T, D = h.shape
    c = D // 2 // LANES
    tm = _tile(T, tm)
    nsteps = T // tm
    return pl.pallas_call(
        functools.partial(_collect_body, tm=tm, nsteps=nsteps, c=c),
        out_shape=jax.ShapeDtypeStruct((T, D), F32),
        grid_spec=pltpu.PrefetchScalarGridSpec(
            num_scalar_prefetch=1, grid=(nsteps,),
            in_specs=[pl.BlockSpec((tm, D), lambda i, d: (i, 0)), pl.BlockSpec((tm, LANES), lambda i, d: (i, 0)),
                      pl.BlockSpec(memory_space=pl.ANY)],
            out_specs=pl.BlockSpec((tm, D), lambda i, d: (i, 0)),
            scratch_shapes=[pltpu.VMEM((2, tm * c, LANES), yb.dtype), pltpu.VMEM((2, tm * c, LANES), yb.dtype),
                            pltpu.SemaphoreType.DMA((2,))]),
        compiler_params=pltpu.CompilerParams(dimension_semantics=("arbitrary",), disable_bounds_checks=True,
                                             vmem_limit_bytes=VMEM_LIMIT_BYTES),
        name="moe_collect",
    )(dest, h, wts, yb)


def _moe(h, norm_g, w_group, b_group, w_expert, b_expert, layer, w_gate, w_up, w_down):
    T, D = h.shape
    pad = LANES - N_GROUPS - N_EXPERTS
    w_r = jnp.concatenate([w_group, w_expert, jnp.zeros((D, pad), F32)], axis=1).astype(BF16)
    b_r = jnp.concatenate([b_group, b_expert, jnp.zeros((pad,), F32)]).astype(F32).reshape(1, LANES)
    logits, xn = _mm([(h, 0, D)], [w_r], gain=norm_g, emit_xn=True, tm=512, tn=LANES, name="moe_router")
    eid, wts, pos, cnt = _route(logits, b_r)
    counts = cnt[0, :N_EXPERTS]
    padded = (counts + MOE_ROWS - 1) // MOE_ROWS * MOE_ROWS
    ends = jnp.cumsum(padded)
    offs = ends - padded
    dest = (offs[eid[:, :TOP_K]] + pos[:, :TOP_K]).reshape(T * TOP_K).astype(jnp.int32)
    P = T * TOP_K + N_EXPERTS * MOE_ROWS
    n_used = (ends[-1:] // MOE_ROWS).astype(jnp.int32)
    xb = _dispatch(xn, dest, ends.astype(jnp.int32), padded.astype(jnp.int32), n_used, P, D // 2 // LANES)
    yb = _experts(xb, (padded // MOE_ROWS).astype(jnp.int32), (offs // MOE_ROWS).astype(jnp.int32), n_used, layer,
                  w_gate, w_up, w_down)
    return _collect(h, yb, dest, wts)


def _pad_cols(w, n):
    return jnp.pad(w, ((0, 0), (0, n - w.shape[1])))


def _layer_even(h, B, S, tabs, norm_g, w_in, sinks, q_lat_norm, kv_lat_norm, w_uq, w_ukv, w_o):
    T, D = h.shape
    n_in, tn = 2560, 512
    sa = HEAD_DIM ** -0.5 * LOG2E
    prog = ([("rot128", sa)] * 8 + [("rot128", 1.0)] * 2 + [("copy", 1.0)] * 8 + [("rot64", 1.0), ("copy", 1.0)])
    y, yf = _mm([(h, 0, D)], [_pad_cols(w_in.astype(BF16), n_in)], gain=norm_g, tn=tn, out_dtype=BF16,
                col_prog=prog, tables=tabs, f32_from=12 * LANES // tn, name="ab_in")
    wq = w_uq.reshape(B_Q_LORA, B_HEADS, B_NOPE + B_ROPE)
    wq_n = wq[:, :, :B_NOPE].reshape(B_Q_LORA, B_HEADS * B_NOPE)
    wq_r = jnp.pad(wq[:, :, B_NOPE:], ((0, 0), (0, 0), (0, LANES - B_ROPE))).reshape(B_Q_LORA, B_HEADS * LANES)
    sb = (B_NOPE + B_ROPE) ** -0.5 * LOG2E
    qq = _mm([(yf, 0, B_Q_LORA)], [jnp.concatenate([wq_n, wq_r], 1).astype(BF16)], gain=q_lat_norm, tn=512,
             out_dtype=BF16, col_prog=[("copy", sb)] * 8 + [("rot64", sb)] * 8, tables=tabs, name="mla_uq")
    kv = _mm([(yf, B_Q_LORA // B_KV_LORA, B_KV_LORA)], [w_ukv.astype(BF16)], gain=kv_lat_norm, out_dtype=BF16,
             tn=512, name="mla_ukv")
    r3 = lambda a: a.reshape(B, S, a.shape[-1])
    y3 = r3(y)
    o_a = _gqa_flash((y3, 0), (y3, 8), (y3, 10), name="swa_attn", B=B, S=S, Hk=A_KV_HEADS,
                     G=A_HEADS // A_KV_HEADS, window=A_WINDOW, sinks=sinks, tq=256, tk=256)
    hp = 2
    head = lambda h: h
    o_b = _flash([(r3(qq), hp * LANES, head), (r3(qq), hp * LANES, lambda h: B_HEADS // hp + h)],
                 [(r3(kv), hp * 2 * LANES, head), (y3, LANES, lambda h: 18)],
                 [[(0, g * LANES), (1, g * LANES)] for g in range(hp)],
                 [[(0, 2 * g * LANES), (1, 0)] for g in range(hp)], [(0, (2 * g + 1) * LANES) for g in range(hp)],
                 name="mla_attn", B=B, S=S, n_steps=B_HEADS // hp, tq=1024, tk=1024, rc=256)
    wo = w_o.astype(BF16)
    na = A_HEADS * HEAD_DIM
    return _mm([(o_a.reshape(T, -1), 0, na), (o_b.reshape(T, -1), 0, B_HEADS * B_V)], [wo[:na], wo[na:]],
               resid=h, tm=512, tn=D, name="ab_out")


def _layer_odd(h, B, S, tabs, cmp_tabs, norm_g, w_in, forget_bias, pe_k, w1_k, w2_k, pe_v, w1_v, w2_v, w_o):
    T, D = h.shape
    hc, kvw = C_HEADS * HEAD_DIM, D_KV_HEADS * HEAD_DIM
    o_cf = 3 * hc
    o_dq = o_cf + C_HEADS
    o_kc = o_dq + D_HEADS * HEAD_DIM
    o_ks = o_kc + 2 * kvw
    o_dg = o_ks + 4 * kvw
    n_in, tn = 5760, 640
    w16 = w_in.astype(BF16)
    w_r = jnp.concatenate([w16[:, :o_cf], w16[:, o_dq:o_kc], w16[:, o_ks:o_dg], w16[:, o_kc:o_ks], w16[:, o_cf:o_dq],
                           w16[:, o_dg:], jnp.zeros((D, n_in - w_in.shape[1]), BF16)], axis=1)
    sc = HEAD_DIM ** -0.5 * LOG2E
    prog = ([("copy", sc)] * 8 + [("copy", 1.0)] * 16 + [("rot128", sc)] * 8
            + [("rot128", 1.0)] * 2 + [("copy", 1.0)] * 2 + [("rot128", 1.0)] * 2 + [("copy", 1.0)] * 7)
    y, yf = _mm([(h, 0, D)], [w_r], gain=norm_g, tn=tn, out_dtype=BF16,
                col_prog=prog, tables=tabs, f32_from=n_in // tn - 1, name="cd_in")
    misc = tn // LANES - 1
    r3 = lambda a: a.reshape(B, S, a.shape[-1])
    y3, yf3 = r3(y), r3(yf)
    fb = jnp.pad(forget_bias.astype(F32), (0, LANES - C_HEADS)).reshape(1, LANES)
    cum = _forget_cum(yf3, misc, fb)
    cumt = jnp.swapaxes(cum[:, :, :C_HEADS], 1, 2).reshape(B, C_HEADS, 1, S)
    hp = 2
    at = lambda first: (lambda h: first + h)
    o_c = _flash([(y3, hp * LANES, at(0))], [(y3, hp * LANES, at(C_HEADS // hp)), (y3, hp * LANES, at(C_HEADS))],
                 [[(0, g * LANES)] for g in range(hp)], [[(0, g * LANES)] for g in range(hp)],
                 [(1, g * LANES) for g in range(hp)], name="fox_attn", B=B, S=S, n_steps=C_HEADS // hp,
                 tq=1024, tk=1024, cum=cum, cumt=cumt)
    G = D_HEADS // D_KV_HEADS
    k_cmp = _compress(yf3, 0, pe_k, w1_k, w2_k, rope_tabs=cmp_tabs)
    v_cmp = _compress(yf3, kvw, pe_v, w1_v, w2_v)
    q_d = (y3, 24 // G)
    o_cmp, sel = _cmp_attn(y3, q_d[1], k_cmp, v_cmp)
    o_slc = _gqa_flash(q_d, (y3, 32), (y3, 34), name="nsa_slc_attn", B=B, S=S, Hk=D_KV_HEADS, G=G, sel=sel,
                       tq=1024, tk=1024, rc=256)
    o_win = _gqa_flash(q_d, (y3, 36), (y3, 38), name="nsa_win_attn", B=B, S=S, Hk=D_KV_HEADS, G=G,
                       window=D_WINDOW, tq=512, tk=512)
    o_d = _nsa_gate(yf, misc, C_HEADS, o_cmp.reshape(T, -1), o_slc.reshape(T, -1), o_win.reshape(T, -1))
    wo = w_o.astype(BF16)
    return _mm([(o_c.reshape(T, -1), 0, hc), (o_d, 0, D_HEADS * HEAD_DIM)], [wo[:hc], wo[hc:]],
               resid=h, tm=512, tn=D, name="cd_out")


def kernel(x, p, positions, ab_w_in, ab_sinks, ab_q_lat_norm, ab_kv_lat_norm, ab_w_uq, ab_w_ukv, ab_w_o,
           cd_w_in, cd_forget_bias, cd_cmp_pe_k, cd_cmp_w1_k, cd_cmp_w2_k, cd_cmp_pe_v, cd_cmp_w1_v,
           cd_cmp_w2_v, cd_w_o, mixer_norm, moe_norm, router_group_w, router_group_b, router_expert_w,
           router_expert_b, expert_w_gate, expert_w_up, expert_w_down, ple_proj, ple_gate_norm, ple_gate_w,
           final_norm):
    B, S, D = x.shape
    T = B * S
    depth = p.shape[0]
    tabs = _rope_tables(positions)
    nc = S // D_CMP_STRIDE
    end = np.minimum(np.arange(nc) * D_CMP_STRIDE + D_CMP_LEN - 1, S - 1)
    cmp_tabs = [t.reshape(B, nc, LANES) for t in _rope_tables(positions[:, end])[:2]]
    h = x.reshape(T, D)
    for i in range(depth):
        j = i // 2
        if i % 2 == 0:
            h = _layer_even(h, B, S, tabs, mixer_norm[i], ab_w_in[j], ab_sinks[j], ab_q_lat_norm[j],
                            ab_kv_lat_norm[j], ab_w_uq[j], ab_w_ukv[j], ab_w_o[j])
        else:
            h = _layer_odd(h, B, S, tabs, cmp_tabs, mixer_norm[i], cd_w_in[j], cd_forget_bias[j],
                           cd_cmp_pe_k[j], cd_cmp_w1_k[j], cd_cmp_w2_k[j], cd_cmp_pe_v[j], cd_cmp_w1_v[j],
                           cd_cmp_w2_v[j], cd_w_o[j])
        h = _moe(h, moe_norm[i], router_group_w[i], router_group_b[i], router_expert_w[i], router_expert_b[i],
                 i, expert_w_gate, expert_w_up, expert_w_down)
        h = _mm([(h, 0, D)], [ple_gate_w[i].astype(BF16)], gain=ple_gate_norm[i], tm=512, tn=D,
                ple=(h, p[i].reshape(T, -1), ple_proj[i].astype(BF16)),
                out_gain=final_norm if i == depth - 1 else None, name="ple")
    return h.reshape(B, S, D)
```

```python
import functools
import math

import numpy as np
import jax
import jax.numpy as jnp
from jax import lax
from jax.experimental import pallas as pl
from jax.experimental.pallas import tpu as pltpu

F32 = jnp.float32
BF16 = jnp.bfloat16

HEAD_DIM = 128
ROPE_THETA = 10000.0
NORM_EPS = 1e-6
NEG_INF = -1e30
TAKEN = -3e38
A_HEADS, A_KV_HEADS, A_WINDOW = 8, 2, 128
B_HEADS, B_Q_LORA, B_KV_LORA, B_NOPE, B_ROPE, B_V = 8, 512, 256, 128, 64, 128
C_HEADS = 8
D_HEADS, D_KV_HEADS = 8, 2
D_CMP_LEN, D_CMP_STRIDE, D_SLC_LEN, D_SLC_TOPN, D_WINDOW = 32, 16, 64, 8, 512
FORCE_BONUS = 1e4
N_GROUPS, EXPERTS_PER_GROUP, TOP_K = 4, 8, 2
N_EXPERTS = N_GROUPS * EXPERTS_PER_GROUP

LANES = 128
VMEM_LIMIT_BYTES = 56 * 1024 * 1024
MOE_ROWS = 256
DMA_UNROLL = 16
FLASH_ROW_CHUNK = 128
LOG2E = math.log2(math.e)


def _params(*sem):
    return pltpu.CompilerParams(dimension_semantics=sem, vmem_limit_bytes=VMEM_LIMIT_BYTES)


def _tile(n, pref):
    t = min(n, pref)
    while n % t:
        t -= 1
    return t


def _pack_bf16_pairs(y):
    n = y.shape[1] // 2
    bits = lambda a: lax.bitcast_convert_type(a.astype(BF16).astype(F32), jnp.uint32)
    return (bits(y[:, :n]) >> 16) | (bits(y[:, n:]) & jnp.uint32(0xFFFF0000))


def _unpack_bf16_pairs(w):
    return (lax.bitcast_convert_type(w << 16, F32), lax.bitcast_convert_type(w & jnp.uint32(0xFFFF0000), F32))


def _rot128(x, c, s):
    return x * c + pltpu.roll(x, 64, 1) * s


def _rot_perms():
    i = np.arange(LANES)
    p128 = (i[:, None] == (i[None, :] + HEAD_DIM // 2) % LANES)
    half = B_ROPE // 2
    p64 = ((i[None, :] < half) & (i[:, None] == i[None, :] + half)) | (
        (i[None, :] >= half) & (i[None, :] < B_ROPE) & (i[:, None] == i[None, :] - half))
    return [jnp.asarray(p.astype(np.float32), BF16) for p in (p128, p64)]


def _tiles_store(ref, lead, words):
    rows, c = words.shape[0], words.shape[1] // LANES
    for j in range(c):
        ref[lead + (pl.ds(j, rows, stride=c), slice(None))] = words[:, j * LANES:(j + 1) * LANES]


def _tiles_load(ref, lead, rows, c):
    return jnp.concatenate([ref[lead + (pl.ds(j, rows, stride=c), slice(None))] for j in range(c)], axis=1)


def _mm_body(*refs, nx, has_gain, emit_xn, mode, resid_is_x, has_out_gain, col_prog, f32_from):
    it = iter(refs)
    x_refs = [next(it) for _ in range(nx)]
    g_ref = next(it) if has_gain else None
    w_refs = [next(it) for _ in range(nx)]
    r_ref = next(it) if mode in ("resid", "ple") and not resid_is_x else None
    p_ref = next(it) if mode == "ple" else None
    wp_ref = next(it) if mode == "ple" else None
    og_ref = next(it) if has_out_gain else None
    tab_refs = [next(it) for _ in range(7)] if col_prog is not None else None
    o_ref = next(it)
    xo_ref = next(it) if emit_xn else None
    of_ref = next(it) if f32_from is not None else None
    xn_ref = next(it) if has_gain else None

    if has_gain:
        @pl.when(pl.program_id(1) == 0)
        def _():
            x = x_refs[0][...].astype(F32)
            y = x * lax.rsqrt(jnp.mean(x * x, axis=-1, keepdims=True) + NORM_EPS) * g_ref[...]
            xn_ref[...] = y.astype(BF16)
            if emit_xn:
                _tiles_store(xo_ref, (), _pack_bf16_pairs(y))
        lhs = [xn_ref[...]]
    else:
        lhs = [x_ref[...].astype(BF16) for x_ref in x_refs]
    acc = None
    for a, w_ref in zip(lhs, w_refs):
        d = jnp.dot(a, w_ref[...], preferred_element_type=F32)
        acc = d if acc is None else acc + d
    if resid_is_x:
        r_ref = x_refs[0]
    if mode == "resid":
        acc = r_ref[...] + acc
    elif mode == "ple":
        pp = jnp.dot(p_ref[...].astype(BF16), wp_ref[...], preferred_element_type=F32)
        acc = r_ref[...] + pp * jax.nn.sigmoid(acc)
    if has_out_gain:
        acc = acc * lax.rsqrt(jnp.mean(acc * acc, axis=-1, keepdims=True) + NORM_EPS) * og_ref[...]
    if col_prog is None:
        o_ref[...] = acc.astype(o_ref.dtype)
        return
    j = pl.program_id(1)
    nb = o_ref.shape[1] // LANES
    tiles = {}
    for jj in range(len(col_prog) // nb):
        tiles.setdefault(tuple(col_prog[jj * nb:(jj + 1) * nb]), []).append(jj)
    for prog, jjs in tiles.items():
        @pl.when(functools.reduce(jnp.logical_or, [j == jj for jj in jjs]))
        def _():
            c128, s128, c64, s64a, s64b, p128, p64 = tab_refs
            for b, (kind, scale) in enumerate(prog):
                x = acc[:, b * LANES:(b + 1) * LANES]
                if kind == "rot128":
                    x = x * c128[...] + jnp.dot(x.astype(BF16), p128[...], preferred_element_type=F32) * s128[...]
                elif kind == "rot64":
                    x = x * c64[...] + jnp.dot(x.astype(BF16), p64[...], preferred_element_type=F32) * (
                        s64a[...] + s64b[...])
                if scale != 1.0:
                    x = x * scale
                o_ref[:, b * LANES:(b + 1) * LANES] = x.astype(o_ref.dtype)
    if f32_from is not None:
        @pl.when(j >= f32_from)
        def _():
            of_ref[...] = acc


def _mm(xs, ws, *, name, gain=None, out_dtype=F32, tm=1024, tn=512, resid=None, ple=None,
        emit_xn=False, out_gain=None, col_prog=None, tables=None, f32_from=None):
    M = xs[0][0].shape[0]
    N = ws[0].shape[1]
    tm, tn = _tile(M, tm), _tile(N, tn)
    nx = len(xs)
    has_gain = gain is not None
    mode = "ple" if ple is not None else ("resid" if resid is not None else "none")
    r_arr = ple[0] if mode == "ple" else resid
    resid_is_x = r_arr is xs[0][0] and tn == N == xs[0][2] and xs[0][1] == 0
    assert out_gain is None or tn == N
    args, in_specs = [], []
    for arr, cb, K in xs:
        args.append(arr)
        in_specs.append(pl.BlockSpec((tm, K), lambda i, j, cb=cb: (i, cb)))
    if has_gain:
        K0 = xs[0][2]
        args.append(gain.reshape(1, K0).astype(F32))
        in_specs.append(pl.BlockSpec((1, K0), lambda i, j: (0, 0)))
    for (arr, cb, K), w in zip(xs, ws):
        args.append(w)
        in_specs.append(pl.BlockSpec((K, tn), lambda i, j: (0, j)))
    if mode in ("resid", "ple") and not resid_is_x:
        args.append(r_arr)
        in_specs.append(pl.BlockSpec((tm, tn), lambda i, j: (i, j)))
    if mode == "ple":
        _, p, wp = ple
        args += [p, wp]
        in_specs += [pl.BlockSpec((tm, p.shape[1]), lambda i, j: (i, 0)),
                     pl.BlockSpec((p.shape[1], tn), lambda i, j: (0, j))]
    if out_gain is not None:
        args.append(out_gain.reshape(1, N).astype(F32))
        in_specs.append(pl.BlockSpec((1, N), lambda i, j: (0, 0)))
    if col_prog is not None:
        assert len(col_prog) * LANES == N
        args += list(tables) + _rot_perms()
        in_specs += [pl.BlockSpec((tm, LANES), lambda i, j: (i, 0))] * 5
        in_specs += [pl.BlockSpec((LANES, LANES), lambda i, j: (0, 0))] * 2
    out_shape = [jax.ShapeDtypeStruct((M, N), out_dtype)]
    out_specs = [pl.BlockSpec((tm, tn), lambda i, j: (i, j))]
    if emit_xn:
        c = xs[0][2] // 2 // LANES
        out_shape.append(jax.ShapeDtypeStruct((M * c, LANES), jnp.uint32))
        out_specs.append(pl.BlockSpec((tm * c, LANES), lambda i, j: (i, 0)))
    if f32_from is not None:
        out_shape.append(jax.ShapeDtypeStruct((M, N - f32_from * tn), F32))
        out_specs.append(pl.BlockSpec((tm, tn), lambda i, j: (i, jnp.maximum(j - f32_from, 0))))
    scratch = [pltpu.VMEM((tm, xs[0][2]), BF16)] if has_gain else []
    res = pl.pallas_call(
        functools.partial(_mm_body, nx=nx, has_gain=has_gain, emit_xn=emit_xn, mode=mode, resid_is_x=resid_is_x,
                          has_out_gain=out_gain is not None,
                          col_prog=None if col_prog is None else tuple(col_prog), f32_from=f32_from),
        out_shape=out_shape, grid=(M // tm, N // tn), in_specs=in_specs, out_specs=out_specs,
        scratch_shapes=scratch, compiler_params=_params("arbitrary", "arbitrary"), name=name,
    )(*args)
    return res if len(res) > 1 else res[0]


def _rope_tables(positions):
    def tables(dim):
        inv = 1.0 / (ROPE_THETA ** (jnp.arange(0, dim, 2, dtype=F32) / dim))
        ang = positions.astype(F32)[..., None] * inv
        return jnp.cos(ang), jnp.sin(ang)

    lead = positions.shape
    c, s = tables(HEAD_DIM)
    c128 = jnp.concatenate([c, c], -1)
    s128 = jnp.concatenate([-s, s], -1)
    c, s = tables(B_ROPE)
    z = jnp.zeros_like(c)
    c64 = jnp.concatenate([c, c, z, z], -1)
    s64a = jnp.concatenate([-s, z, z, z], -1)
    s64b = jnp.concatenate([z, s, z, z], -1)
    n = int(np.prod(lead))
    return [t.reshape(n, LANES) for t in (c128, s128, c64, s64a, s64b)]


def _pair_tables(S, tq, tk, window):
    qi_l, ki_l, fl_l, mk_l, masks, ids = [], [], [], [], [], {}
    for qi in range(S // tq):
        q0, q1 = qi * tq, (qi + 1) * tq - 1
        ks = []
        for ki in range(S // tk):
            k0, k1 = ki * tk, (ki + 1) * tk - 1
            if k0 > q1 or (window is not None and q0 - k1 >= window):
                continue
            full = k1 <= q0 and (window is None or q1 - k0 < window)
            mid = 0
            if not full:
                delta = q0 - k0
                if delta not in ids:
                    rel = np.arange(tq)[:, None] + delta - np.arange(tk)[None, :]
                    ok = (rel >= 0) if window is None else ((rel >= 0) & (rel < window))
                    masks.append(np.where(ok, 0.0, NEG_INF).astype(np.float32))
                    ids[delta] = len(masks)
                mid = ids[delta]
            ks.append((ki, mid))
        for n, (ki, mid) in enumerate(ks):
            qi_l.append(qi)
            ki_l.append(ki)
            fl_l.append((1 if n == 0 else 0) | (2 if n == len(ks) - 1 else 0))
            mk_l.append(mid)
    if not masks:
        masks.append(np.zeros((tq, tk), np.float32))
    tabs = [jnp.asarray(np.array(a, np.int32)) for a in (qi_l, ki_l, fl_l, mk_l)]
    return tabs, jnp.asarray(np.stack(masks))


def _flash_body(qi_t, ki_t, fl_t, mk_t, *refs, nq, nkv, q_src, k_src, v_src, tq, rc, causal_diag,
                has_sink, has_bias, has_sel):
    nc = len(q_src)
    it = iter(refs)
    q_refs = [next(it) for _ in range(nq)]
    kv_refs = [next(it) for _ in range(nkv)]
    mask_ref = next(it)
    sink_ref = next(it) if has_sink else None
    cum_ref = next(it) if has_bias else None
    cumt_ref = next(it) if has_bias else None
    sel_ref = next(it) if has_sel else None
    exp_ref = next(it) if has_sel else None
    o_ref = next(it)
    m_sc, acc_sc = next(it), next(it)
    cq_sc = next(it) if has_bias else None

    step_id = pl.program_id(2)
    fl, mk = fl_t[step_id], mk_t[step_id]

    def cat(refs_, src, rows=slice(None)):
        xs = [refs_[pi][0, rows, off:off + LANES] for pi, off in src]
        return xs[0] if len(xs) == 1 else jnp.concatenate(xs, axis=-1)

    @pl.when((fl & 1) != 0)
    def _init():
        m_sc[...] = jnp.full(m_sc.shape, NEG_INF, F32)
        acc_sc[...] = jnp.zeros(acc_sc.shape, F32)
        if has_bias:
            lane = lax.broadcasted_iota(jnp.int32, (tq, LANES), 1)
            for g in range(nc):
                head = pl.program_id(1) * nc + g
                cq_sc[g] = jnp.sum(jnp.where(lane == head, cum_ref[0], 0.0), axis=-1, keepdims=True)

    tk = kv_refs[0].shape[1]
    ones = jnp.ones((tk, LANES), BF16)

    def step(masked):
        ks = [cat(kv_refs, k_src[g]) for g in range(nc)]
        vs = [jnp.concatenate([kv_refs[pi][0, :, off:off + LANES], ones], axis=1) for pi, off in v_src]
        for r in range(tq // rc):
            rows = slice(r * rc, (r + 1) * rc)
            nk = min((r + 1) * rc, tk) if (masked and causal_diag) else tk
            reps = nk // LANES
            add = mask_ref[mk - 1, rows, :nk] if masked else None
            if has_sel:
                hidden = (jnp.dot(sel_ref[0, 0, rows, :], exp_ref[0, :, :nk], preferred_element_type=F32)
                          - 1.0) * (-NEG_INF)
                add = hidden if add is None else add + hidden
            for g in range(nc):
                q = cat(q_refs, q_src[g], rows)
                s = lax.dot_general(q, ks[g][:nk], (((1,), (1,)), ((), ())), preferred_element_type=F32)
                if has_bias:
                    s = s + (cq_sc[g, rows, :] - cumt_ref[0, g, :, :nk])
                if add is not None:
                    s = s + add
                m_prev = m_sc[g, rows, :]
                m_new = jnp.maximum(m_prev, jnp.max(s, axis=-1, keepdims=True))
                p = jnp.exp2(s - (jnp.concatenate([m_new] * reps, axis=1) if reps > 1 else m_new))
                alpha = jnp.exp2(m_prev - m_new)
                acc_sc[g, rows, :] = (jnp.concatenate([alpha, alpha], axis=1) * acc_sc[g, rows, :]
                                      + jnp.dot(p.astype(BF16), vs[g][:nk], preferred_element_type=F32))
                m_sc[g, rows, :] = m_new

    @pl.when(mk != 0)
    def _():
        step(True)

    @pl.when(mk == 0)
    def _():
        step(False)

    @pl.when((fl & 2) != 0)
    def _finish():
        for g in range(nc):
            m, acc, l = m_sc[g], acc_sc[g, :, :LANES], acc_sc[g, :, LANES:]
            if has_sink:
                sk = sink_ref[0, g:g + 1, 0:1] * LOG2E
                m_f = jnp.maximum(m, sk)
                w = jnp.exp2(m - m_f)
                l = l * w + jnp.exp2(sk - m_f)
                acc = acc * w
            o_ref[0, :, g * LANES:(g + 1) * LANES] = (acc / l).astype(o_ref.dtype)


def _flash(q_parts, kv_parts, q_src, k_src, v_src, *, name, B, S, n_steps, tq, tk, window=None, sinks=None,
           cum=None, cumt=None, sel=None, out_dtype=BF16, rc=FLASH_ROW_CHUNK):
    nc = len(q_src)
    tq, tk = _tile(S, tq), _tile(S, tk)
    tabs, masks = _pair_tables(S, tq, tk, window)
    npairs = int(tabs[0].shape[0])
    args, in_specs = [], []
    for arr, width, cf in q_parts:
        args.append(arr)
        in_specs.append(pl.BlockSpec((1, tq, width), lambda b, h, s, qt, kt, ft, mt, cf=cf: (b, qt[s], cf(h))))
    for arr, width, cf in kv_parts:
        args.append(arr)
        in_specs.append(pl.BlockSpec((1, tk, width), lambda b, h, s, qt, kt, ft, mt, cf=cf: (b, kt[s], cf(h))))
    args.append(masks)
    in_specs.append(pl.BlockSpec(masks.shape, lambda b, h, s, qt, kt, ft, mt: (0, 0, 0)))
    if sinks is not None:
        args.append(jnp.broadcast_to(sinks.astype(F32).reshape(n_steps, nc, 1), (n_steps, nc, LANES)))
        in_specs.append(pl.BlockSpec((1, nc, LANES), lambda b, h, s, qt, kt, ft, mt: (h, 0, 0)))
    if cum is not None:
        args += [cum, cumt]
        in_specs += [pl.BlockSpec((1, tq, LANES), lambda b, h, s, qt, kt, ft, mt: (b, qt[s], 0)),
                     pl.BlockSpec((1, nc, 1, tk), lambda b, h, s, qt, kt, ft, mt: (b, h, 0, kt[s]))]
    if sel is not None:
        per = tk // D_SLC_LEN
        e = np.zeros((S // tk, LANES, tk), np.float32)
        for ki in range(S // tk):
            e[ki, ki * per + np.arange(tk) // D_SLC_LEN, np.arange(tk)] = 1.0
        args += [sel, jnp.asarray(e, BF16)]
        in_specs += [pl.BlockSpec((1, 1, tq, LANES), lambda b, h, s, qt, kt, ft, mt: (b, h, qt[s], 0)),
                     pl.BlockSpec((1, LANES, tk), lambda b, h, s, qt, kt, ft, mt: (kt[s], 0, 0))]
    scratch = [pltpu.VMEM((nc, tq, LANES), F32), pltpu.VMEM((nc, tq, 2 * LANES), F32)]
    if cum is not None:
        scratch.append(pltpu.VMEM((nc, tq, 1), F32))
    body = functools.partial(_flash_body, nq=len(q_parts), nkv=len(kv_parts), q_src=q_src, k_src=k_src,
                             v_src=v_src, tq=tq, rc=_tile(tq, rc), causal_diag=window is None and tq == tk,
                             has_sink=sinks is not None,
                             has_bias=cum is not None, has_sel=sel is not None)
    return pl.pallas_call(
        body, out_shape=jax.ShapeDtypeStruct((B, S, n_steps * nc * LANES), out_dtype),
        grid_spec=pltpu.PrefetchScalarGridSpec(
            num_scalar_prefetch=4, grid=(B, n_steps, npairs), in_specs=in_specs,
            out_specs=pl.BlockSpec((1, tq, nc * LANES), lambda b, h, s, qt, kt, ft, mt: (b, qt[s], h)),
            scratch_shapes=scratch),
        compiler_params=_params("arbitrary", "arbitrary", "arbitrary"), name=name,
    )(*tabs, *args)


def _gqa_flash(q, k, v, *, name, B, S, Hk, G, **kw):
    at = lambda first: (lambda h: first + h)
    return _flash([(q[0], G * LANES, at(q[1]))], [(k[0], LANES, at(k[1])), (v[0], LANES, at(v[1]))],
                  [[(0, g * LANES)] for g in range(G)], [[(0, 0)]] * G, [(1, 0)] * G,
                  name=name, B=B, S=S, n_steps=Hk, **kw)


def _cum_body(y_ref, b_ref, tri_ref, o_ref, carry):
    @pl.when(pl.program_id(1) == 0)
    def _():
        carry[...] = jnp.zeros(carry.shape, F32)
    x = y_ref[0] + b_ref[...]
    logf = jnp.minimum(x, 0.0) - jnp.log1p(jnp.exp(-jnp.abs(x)))
    cum = jnp.dot(tri_ref[...], logf, preferred_element_type=F32, precision=lax.Precision.HIGHEST) + carry[...]
    o_ref[0] = cum * LOG2E
    carry[...] = cum[-1:, :]


def _forget_cum(y3, col_block, bias_row, *, ts=512):
    B, S, _ = y3.shape
    ts = _tile(S, ts)
    tri = jnp.asarray(np.tril(np.ones((ts, ts), np.float32)))
    return pl.pallas_call(
        _cum_body, out_shape=jax.ShapeDtypeStruct((B, S, LANES), F32), grid=(B, S // ts),
        in_specs=[pl.BlockSpec((1, ts, LANES), lambda b, s: (b, s, col_block)),
                  pl.BlockSpec((1, LANES), lambda b, s: (0, 0)),
                  pl.BlockSpec((ts, ts), lambda b, s: (0, 0))],
        out_specs=pl.BlockSpec((1, ts, LANES), lambda b, s: (b, s, 0)),
        scratch_shapes=[pltpu.VMEM((1, LANES), F32)],
        compiler_params=_params("arbitrary", "arbitrary"), name="forget_cum",
    )(y3, bias_row, tri)


def _compress_body(*refs, rope, nc, width, col0):
    if rope:
        z_ref, pe_ref, w1_ref, w2_ref, c_ref, s_ref, o_ref = refs
    else:
        z_ref, pe_ref, w1_ref, w2_ref, o_ref = refs
    half = D_CMP_LEN // 2
    for hk in range(D_KV_HEADS):
        u = jnp.zeros((nc, w1_ref.shape[1]), F32)
        v = jnp.zeros((nc, w1_ref.shape[1]), F32)
        for l in range(half):
            first = l * width + col0 + hk * HEAD_DIM
            z = z_ref[0, :, first:first + HEAD_DIM]
            zu = (z + pe_ref[l:l + 1, :]).astype(BF16)
            zv = (z + pe_ref[half + l:half + l + 1, :]).astype(BF16)
            u = u + jnp.dot(zu, w1_ref[l * HEAD_DIM:(l + 1) * HEAD_DIM, :], preferred_element_type=F32)
            v = v + jnp.dot(zv, w1_ref[(half + l) * HEAD_DIM:(half + l + 1) * HEAD_DIM, :],
                            preferred_element_type=F32)
        pre = u + pltpu.roll(v, nc - 1, 0)
        hid = jax.nn.gelu(pre, approximate=True)
        out = jnp.dot(hid.astype(BF16), w2_ref[...], preferred_element_type=F32)
        if rope:
            out = _rot128(out, c_ref[0], s_ref[0])
        o_ref[0, hk] = out.astype(o_ref.dtype)


def _compress(z, col0, pe, w1, w2, rope_tabs=None):
    B, S, W = z.shape
    nc = S // D_CMP_STRIDE
    zc = z.reshape(B, nc, D_CMP_STRIDE * W)
    args = [zc, pe.astype(F32), w1.astype(BF16), w2.astype(BF16)]
    in_specs = [pl.BlockSpec((1, nc, D_CMP_STRIDE * W), lambda b: (b, 0, 0)),
                pl.BlockSpec(pe.shape, lambda b: (0, 0)),
                pl.BlockSpec(w1.shape, lambda b: (0, 0)),
                pl.BlockSpec(w2.shape, lambda b: (0, 0))]
    if rope_tabs is not None:
        args += list(rope_tabs)
        in_specs += [pl.BlockSpec((1, nc, LANES), lambda b: (b, 0, 0))] * 2
    return pl.pallas_call(
        functools.partial(_compress_body, rope=rope_tabs is not None, nc=nc, width=W, col0=col0),
        out_shape=jax.ShapeDtypeStruct((B, D_KV_HEADS, nc, HEAD_DIM), BF16), grid=(B,),
        in_specs=in_specs, out_specs=pl.BlockSpec((1, D_KV_HEADS, nc, HEAD_DIM), lambda b: (b, 0, 0, 0)),
        compiler_params=_params("arbitrary"), name="nsa_compress",
    )(*args)


def _cmp_attn_body(q_ref, k_ref, v_ref, ov_ref, o_ref, sel_ref, *, G, tq, nc, n_cmp, n_slc, topn):
    qi = pl.program_id(2)
    t = qi * tq + lax.broadcasted_iota(jnp.int32, (tq, nc), 0)
    c = lax.broadcasted_iota(jnp.int32, (tq, nc), 1)
    valid = (c * D_CMP_STRIDE + (D_CMP_LEN - 1) <= t) & (c < n_cmp)
    k = k_ref[0, 0]
    v = v_ref[0, 0]
    psum = jnp.zeros((tq, nc), F32)
    for g in range(G):
        q = q_ref[0, :, g * LANES:(g + 1) * LANES]
        s = lax.dot_general(q, k, (((1,), (1,)), ((), ())), preferred_element_type=F32)
        s = jnp.where(valid, s, NEG_INF)
        e = jnp.where(valid, jnp.exp2(s - jnp.max(s, axis=-1, keepdims=True)), 0.0)
        p = e / jnp.maximum(jnp.sum(e, axis=-1, keepdims=True), jnp.finfo(F32).tiny)
        o_ref[0, :, g * LANES:(g + 1) * LANES] = jnp.dot(
            p.astype(BF16), v, preferred_element_type=F32).astype(o_ref.dtype)
        psum = psum + p
    imp = lax.dot_general(ov_ref[...], psum, (((1,), (1,)), ((), ())), preferred_element_type=F32,
                          precision=lax.Precision.HIGHEST)
    blk = lax.broadcasted_iota(jnp.int32, (LANES, tq), 0)
    tcol = qi * tq + lax.broadcasted_iota(jnp.int32, (LANES, tq), 1)
    cur = jnp.right_shift(tcol, int(math.log2(D_SLC_LEN)))
    forced = (blk == 0) | (blk == cur) | (blk == cur - 1)
    imp = jnp.where(blk * D_SLC_LEN > tcol, NEG_INF, imp + jnp.where(forced, FORCE_BONUS, 0.0))
    imp = jnp.where(blk >= n_slc, TAKEN, imp)
    chosen = jnp.zeros((LANES, tq), F32)
    blk_f = blk.astype(F32)
    for _ in range(topn):
        mx = jnp.max(imp, axis=0, keepdims=True)
        idx = jnp.min(jnp.where(imp == mx, blk_f, float(LANES)), axis=0, keepdims=True)
        hit = blk_f == idx
        chosen = jnp.where(hit, 1.0, chosen)
        imp = jnp.where(hit, TAKEN, imp)
    sel_ref[0, 0] = chosen.T.astype(sel_ref.dtype)


def _cmp_attn(q, q_first, k_cmp, v_cmp, *, tq=512):
    B, S, _ = q.shape
    Hk, G = D_KV_HEADS, D_HEADS // D_KV_HEADS
    nc = S // D_CMP_STRIDE
    n_cmp = (S - D_CMP_LEN) // D_CMP_STRIDE + 1
    n_slc = S // D_SLC_LEN
    tq = _tile(S, tq)
    c0 = np.arange(nc) * D_CMP_STRIDE
    s0 = np.arange(LANES) * D_SLC_LEN
    ov = ((c0[:, None] < (s0 + D_SLC_LEN)[None, :]) & ((c0 + D_CMP_LEN)[:, None] > s0[None, :])
          & (np.arange(nc) < n_cmp)[:, None] & (np.arange(LANES) < n_slc)[None, :]).astype(np.float32)
    body = functools.partial(_cmp_attn_body, G=G, tq=tq, nc=nc, n_cmp=n_cmp, n_slc=n_slc,
                             topn=min(D_SLC_TOPN, n_slc))
    return pl.pallas_call(
        body,
        out_shape=[jax.ShapeDtypeStruct((B, S, Hk * G * LANES), BF16),
                   jax.ShapeDtypeStruct((B, Hk, S, LANES), BF16)],
        grid=(B, Hk, S // tq),
        in_specs=[pl.BlockSpec((1, tq, G * LANES), lambda b, h, i: (b, i, q_first + h)),
                  pl.BlockSpec((1, 1, nc, LANES), lambda b, h, i: (b, h, 0, 0)),
                  pl.BlockSpec((1, 1, nc, LANES), lambda b, h, i: (b, h, 0, 0)),
                  pl.BlockSpec((LANES, nc), lambda b, h, i: (0, 0))],
        out_specs=[pl.BlockSpec((1, tq, G * LANES), lambda b, h, i: (b, i, h)),
                   pl.BlockSpec((1, 1, tq, LANES), lambda b, h, i: (b, h, i, 0))],
        compiler_params=_params("arbitrary", "arbitrary", "arbitrary"), name="nsa_cmp_attn",
    )(q, k_cmp, v_cmp, jnp.asarray(ov.T))


def _gate_body(y_ref, a_ref, b_ref, c_ref, o_ref, *, lane0):
    g = jax.nn.sigmoid(y_ref[...])
    for h in range(D_HEADS):
        cols = slice(h * LANES, (h + 1) * LANES)
        ga = g[:, lane0 + h:lane0 + h + 1]
        gb = g[:, lane0 + D_HEADS + h:lane0 + D_HEADS + h + 1]
        gc = g[:, lane0 + 2 * D_HEADS + h:lane0 + 2 * D_HEADS + h + 1]
        o_ref[:, cols] = (ga * a_ref[:, cols] + gb * b_ref[:, cols] + gc * c_ref[:, cols]).astype(o_ref.dtype)


def _nsa_gate(y, col_block, lane0, o_cmp, o_slc, o_win, *, tm=512):
    M = y.shape[0]
    W = o_cmp.shape[1]
    tm = _tile(M, tm)
    row = lambda i: (i, 0)
    return pl.pallas_call(
        functools.partial(_gate_body, lane0=lane0), out_shape=jax.ShapeDtypeStruct((M, W), BF16),
        grid=(M // tm,),
        in_specs=[pl.BlockSpec((tm, LANES), lambda i: (i, col_block))] + [pl.BlockSpec((tm, W), row)] * 3,
        out_specs=pl.BlockSpec((tm, W), row), compiler_params=_params("arbitrary"), name="nsa_gate",
    )(y, o_cmp, o_slc, o_win)


def _route_body(lg_ref, b_ref, tri_ref, e_ref, w_ref, pos_ref, cnt_ref, carry, *, tm):
    @pl.when(pl.program_id(0) == 0)
    def _():
        carry[...] = jnp.zeros(carry.shape, F32)
    lane = lax.broadcasted_iota(jnp.int32, (tm, LANES), 1)
    logits = lg_ref[...] + b_ref[...]
    gl = jnp.where(lane < N_GROUPS, logits, -jnp.inf)
    gmax = jnp.max(gl, axis=-1, keepdims=True)
    g_val = 1.0 / jnp.sum(jnp.exp(gl - gmax), axis=-1, keepdims=True)
    g_idx = jnp.min(jnp.where(gl == gmax, lane, LANES), axis=-1, keepdims=True)
    lo = N_GROUPS + EXPERTS_PER_GROUP * g_idx
    el = jnp.where((lane >= lo) & (lane < lo + EXPERTS_PER_GROUP), logits, -jnp.inf)
    e1 = jnp.max(el, axis=-1, keepdims=True)
    i1 = jnp.min(jnp.where(el == e1, lane, LANES), axis=-1, keepdims=True)
    el2 = jnp.where(lane == i1, -jnp.inf, el)
    e2 = jnp.max(el2, axis=-1, keepdims=True)
    i2 = jnp.min(jnp.where(el2 == e2, lane, LANES), axis=-1, keepdims=True)
    r = jnp.exp(e2 - e1)
    w1 = g_val / (1.0 + r)
    w2 = w1 * r
    x1, x2 = i1 - N_GROUPS, i2 - N_GROUPS
    e_ref[...] = jnp.where(lane == 0, x1, jnp.where(lane == 1, x2, 0))
    w_ref[...] = jnp.where(lane == 0, w1, jnp.where(lane == 1, w2, 0.0))
    hot1 = lane == x1
    hot2 = lane == x2
    both = jnp.where(hot1 | hot2, 1.0, 0.0)
    before = jnp.dot(tri_ref[...], both.astype(BF16), preferred_element_type=F32) + carry[...]
    p1 = jnp.sum(jnp.where(hot1, before, 0.0), axis=-1, keepdims=True)
    p2 = jnp.sum(jnp.where(hot2, before, 0.0), axis=-1, keepdims=True)
    pos_ref[...] = jnp.where(lane == 0, p1, jnp.where(lane == 1, p2, 0.0)).astype(jnp.int32)
    carry[...] = carry[...] + jnp.sum(both, axis=0, keepdims=True)
    cnt_ref[...] = carry[...].astype(jnp.int32)


def _route(logits, bias_row, *, tm=512):
    T = logits.shape[0]
    tm = _tile(T, tm)
    tri = jnp.asarray(np.tril(np.ones((tm, tm), np.float32), -1), BF16)
    row = lambda i: (i, 0)
    fixed = lambda i: (0, 0)
    return pl.pallas_call(
        functools.partial(_route_body, tm=tm),
        out_shape=[jax.ShapeDtypeStruct((T, LANES), jnp.int32), jax.ShapeDtypeStruct((T, LANES), F32),
                   jax.ShapeDtypeStruct((T, LANES), jnp.int32), jax.ShapeDtypeStruct((1, LANES), jnp.int32)],
        grid=(T // tm,),
        in_specs=[pl.BlockSpec((tm, LANES), row), pl.BlockSpec((1, LANES), fixed), pl.BlockSpec((tm, tm), fixed)],
        out_specs=[pl.BlockSpec((tm, LANES), row)] * 3 + [pl.BlockSpec((1, LANES), fixed)],
        scratch_shapes=[pltpu.VMEM((1, LANES), F32)],
        compiler_params=_params("arbitrary"), name="moe_route",
    )(logits, bias_row, tri)


def _dispatch_body(d_ref, ends_ref, pad_ref, nu_ref, x_ref, out_ref, zero_sc, sem, zsem, *, tm, nblk, c):
    base = pl.program_id(0) * tm

    @pl.when(pl.program_id(0) == 0)
    def _():
        zero_sc[...] = jnp.zeros(zero_sc.shape, zero_sc.dtype)

        def zero_block(first_row):
            return pltpu.make_async_copy(
                zero_sc, out_ref.at[pl.ds(pl.multiple_of(first_row * c, MOE_ROWS * c), MOE_ROWS * c)], zsem)

        for wait in (False, True):
            for e in range(N_EXPERTS):
                for live, first_row in ((pad_ref[e] > 0, ends_ref[e] - MOE_ROWS),
                                        (nu_ref[0] + e < nblk, (nu_ref[0] + e) * MOE_ROWS)):
                    @pl.when(live)
                    def _():
                        zero_block(first_row).wait() if wait else zero_block(first_row).start()

    def issue(r, carry):
        for k in range(TOP_K):
            row = d_ref[(base + r) * TOP_K + k]
            pltpu.make_async_copy(x_ref.at[pl.ds(pl.multiple_of(r * c, c), c)],
                                  out_ref.at[pl.ds(pl.multiple_of(row * c, c), c)], sem).start(priority=k)
        return carry

    lax.fori_loop(0, tm, issue, 0, unroll=DMA_UNROLL)
    for _ in range(TOP_K):
        pltpu.make_async_copy(x_ref, out_ref.at[pl.ds(0, tm * c)], sem).wait()


def _dispatch(xn, dest, ends, padded, n_used, n_rows, c, *, tm=512):
    T = xn.shape[0] // c
    tm = _tile(T, tm)
    return pl.pallas_call(
        functools.partial(_dispatch_body, tm=tm, nblk=n_rows // MOE_ROWS, c=c),
        out_shape=jax.ShapeDtypeStruct((n_rows * c, LANES), xn.dtype),
        grid_spec=pltpu.PrefetchScalarGridSpec(
            num_scalar_prefetch=4, grid=(T // tm,),
            in_specs=[pl.BlockSpec((tm * c, LANES), lambda i, *_: (i, 0))],
            out_specs=pl.BlockSpec(memory_space=pl.ANY),
            scratch_shapes=[pltpu.VMEM((MOE_ROWS * c, LANES), xn.dtype), pltpu.SemaphoreType.DMA(()),
                            pltpu.SemaphoreType.DMA(())]),
        compiler_params=pltpu.CompilerParams(dimension_semantics=("arbitrary",), has_side_effects=True,
                                             vmem_limit_bytes=VMEM_LIMIT_BYTES, disable_bounds_checks=True),
        name="moe_dispatch",
    )(dest, ends, padded, n_used, xn)


def _expert_body(nb_ref, first_ref, nu_ref, xb_ref, wg_ref, wu_ref, wd_ref, yb_ref, wg_sc, wu_sc, wd_sc, xbuf,
                 obuf, sem_in, sem_out, state, *, nblk, c):
    e = pl.program_id(0)
    last = pl.num_programs(0) - 1
    nb, first = nb_ref[e], first_ref[e]

    @pl.when(e == 0)
    def _():
        for i in range(3):
            state[i] = 0

    def rows(first_block, b):
        return pl.ds(pl.multiple_of((first_block + b) * (MOE_ROWS * c), MOE_ROWS * c), MOE_ROWS * c)

    def fetch(first_block, b, slot):
        return pltpu.make_async_copy(xb_ref.at[rows(first_block, b)], xbuf.at[slot], sem_in.at[slot])

    def flush(b, slot):
        return pltpu.make_async_copy(obuf.at[slot], yb_ref.at[rows(first, b)], sem_out.at[slot])

    def drain(slot):
        @pl.when(state[slot] != 0)
        def _():
            pltpu.make_async_copy(obuf.at[slot], yb_ref.at[pl.ds(0, MOE_ROWS * c)], sem_out.at[slot]).wait()
            state[slot] = 0

    @pl.when(nb > 0)
    def _():
        @pl.when(state[2] == 0)
        def _():
            fetch(first, 0, 0).start()

        state[2] = 0
        wg_sc[...] = wg_ref[0, 0].astype(BF16)
        wu_sc[...] = wu_ref[0, 0].astype(BF16)
        wd_sc[...] = wd_ref[0, 0].astype(BF16)

        def block(b, carry):
            slot = lax.rem(b, 2)

            @pl.when(b + 1 < nb)
            def _():
                fetch(first, b + 1, 1 - slot).start()

            fetch(first, b, slot).wait()
            drain(slot)
            x = jnp.concatenate(_unpack_bf16_pairs(_tiles_load(xbuf, (slot,), MOE_ROWS, c)), axis=1).astype(BF16)
            gate = jnp.dot(x, wg_sc[...], preferred_element_type=F32)
            up = jnp.dot(x, wu_sc[...], preferred_element_type=F32)
            hid = (gate * jax.nn.sigmoid(gate) * up).astype(BF16)
            _tiles_store(obuf, (slot,), _pack_bf16_pairs(jnp.dot(hid, wd_sc[...], preferred_element_type=F32)))
            flush(b, slot).start()
            state[slot] = 1
            return carry

        lax.fori_loop(0, nb, block, 0)
        nxt = jnp.minimum(e + 1, last)

        @pl.when((e < last) & (nb_ref[nxt] > 0))
        def _():
            fetch(first_ref[nxt], 0, 0).start()
            state[2] = 1

    @pl.when(e == last)
    def _():
        drain(0)
        drain(1)
        obuf[0] = jnp.zeros(obuf.shape[1:], obuf.dtype)

        def tail(t):
            return pltpu.make_async_copy(obuf.at[0], yb_ref.at[rows(nu_ref[0], t)], sem_out.at[0])

        for wait in (False, True):
            for t in range(N_EXPERTS):
                @pl.when(nu_ref[0] + t < nblk)
                def _():
                    tail(t).wait() if wait else tail(t).start()


def _experts(xb, n_blocks, first_block, n_used, layer, w_gate, w_up, w_down):
    D, Hd = w_gate.shape[2], w_gate.shape[3]
    c = D // 2 // LANES
    nblk = xb.shape[0] // (MOE_ROWS * c)
    wspec = lambda shape: pl.BlockSpec((1, 1) + shape, lambda e, *_: (layer, e, 0, 0))
    return pl.pallas_call(
        functools.partial(_expert_body, nblk=nblk, c=c), out_shape=jax.ShapeDtypeStruct(xb.shape, xb.dtype),
        grid_spec=pltpu.PrefetchScalarGridSpec(
            num_scalar_prefetch=3, grid=(w_gate.shape[1],),
            in_specs=[pl.BlockSpec(memory_space=pl.ANY), wspec((D, Hd)), wspec((D, Hd)), wspec((Hd, D))],
            out_specs=pl.BlockSpec(memory_space=pl.ANY),
            scratch_shapes=[pltpu.VMEM((D, Hd), BF16), pltpu.VMEM((D, Hd), BF16), pltpu.VMEM((Hd, D), BF16),
                            pltpu.VMEM((2, MOE_ROWS * c, LANES), xb.dtype),
                            pltpu.VMEM((2, MOE_ROWS * c, LANES), xb.dtype),
                            pltpu.SemaphoreType.DMA((2,)), pltpu.SemaphoreType.DMA((2,)),
                            pltpu.SMEM((3,), jnp.int32)]),
        compiler_params=pltpu.CompilerParams(dimension_semantics=("arbitrary",), has_side_effects=True,
                                             vmem_limit_bytes=VMEM_LIMIT_BYTES),
        name="moe_experts",
    )(n_blocks, first_block, n_used, xb, w_gate, w_up, w_down)


def _collect_body(d_ref, h_ref, w_ref, yb_ref, o_ref, buf_a, buf_b, sem, *, tm, nsteps, c):
    i = pl.program_id(0)
    slot = lax.rem(i, 2)

    def fetch(step, slot_):
        base = step * tm

        def issue(r, carry):
            for k, buf in enumerate((buf_a, buf_b)):
                row = d_ref[(base + r) * TOP_K + k]
                pltpu.make_async_copy(yb_ref.at[pl.ds(pl.multiple_of(row * c, c), c)],
                                      buf.at[slot_, pl.ds(pl.multiple_of(r * c, c), c)],
                                      sem.at[slot_]).start(priority=k)
            return carry

        lax.fori_loop(0, tm, issue, 0, unroll=DMA_UNROLL)

    @pl.when(i == 0)
    def _():
        fetch(0, 0)

    @pl.when(i + 1 < nsteps)
    def _():
        fetch(i + 1, 1 - slot)

    for buf in (buf_a, buf_b):
        pltpu.make_async_copy(yb_ref.at[pl.ds(0, tm * c)], buf.at[slot], sem.at[slot]).wait()
    w = w_ref[...]
    half = h_ref.shape[1] // 2
    for part, ya, yb in zip((slice(0, half), slice(half, None)),
                            _unpack_bf16_pairs(_tiles_load(buf_a, (slot,), tm, c)),
                            _unpack_bf16_pairs(_tiles_load(buf_b, (slot,), tm, c))):
        o_ref[:, part] = h_ref[:, part] + w[:, 0:1] * ya + w[:, 1:2] * yb


def _collect(h, yb, dest, wts, *, tm=256):
    T, D = h.shape
    c = D // 2 // LANES
    tm = _tile(T, tm)
    nsteps = T // tm
    return pl.pallas_call(
        functools.partial(_collect_body, tm=tm, nsteps=nsteps, c=c),
        out_shape=jax.ShapeDtypeStruct((T, D), F32),
        grid_spec=pltpu.PrefetchScalarGridSpec(
            num_scalar_prefetch=1, grid=(nsteps,),
            in_specs=[pl.BlockSpec((tm, D), lambda i, d: (i, 0)), pl.BlockSpec((tm, LANES), lambda i, d: (i, 0)),
                      pl.BlockSpec(memory_space=pl.ANY)],
            out_specs=pl.BlockSpec((tm, D), lambda i, d: (i, 0)),
            scratch_shapes=[pltpu.VMEM((2, tm * c, LANES), yb.dtype), pltpu.VMEM((2, tm * c, LANES), yb.dtype),
                            pltpu.SemaphoreType.DMA((2,))]),
        compiler_params=pltpu.CompilerParams(dimension_semantics=("arbitrary",), disable_bounds_checks=True,
                                             vmem_limit_bytes=VMEM_LIMIT_BYTES),
        name="moe_collect",
    )(dest, h, wts, yb)


def _moe(h, norm_g, w_group, b_group, w_expert, b_expert, layer, w_gate, w_up, w_down):
    T, D = h.shape
    pad = LANES - N_GROUPS - N_EXPERTS
    w_r = jnp.concatenate([w_group, w_expert, jnp.zeros((D, pad), F32)], axis=1).astype(BF16)
    b_r = jnp.concatenate([b_group, b_expert, jnp.zeros((pad,), F32)]).astype(F32).reshape(1, LANES)
    logits, xn = _mm([(h, 0, D)], [w_r], gain=norm_g, emit_xn=True, tm=512, tn=LANES, name="moe_router")
    eid, wts, pos, cnt = _route(logits, b_r)
    counts = cnt[0, :N_EXPERTS]
    padded = (counts + MOE_ROWS - 1) // MOE_ROWS * MOE_ROWS
    ends = jnp.cumsum(padded)
    offs = ends - padded
    dest = (offs[eid[:, :TOP_K]] + pos[:, :TOP_K]).reshape(T * TOP_K).astype(jnp.int32)
    P = T * TOP_K + N_EXPERTS * MOE_ROWS
    n_used = (ends[-1:] // MOE_ROWS).astype(jnp.int32)
    xb = _dispatch(xn, dest, ends.astype(jnp.int32), padded.astype(jnp.int32), n_used, P, D // 2 // LANES)
    yb = _experts(xb, (padded // MOE_ROWS).astype(jnp.int32), (offs // MOE_ROWS).astype(jnp.int32), n_used, layer,
                  w_gate, w_up, w_down)
    return _collect(h, yb, dest, wts)


def _pad_cols(w, n):
    return jnp.pad(w, ((0, 0), (0, n - w.shape[1])))


def _layer_even(h, B, S, tabs, norm_g, w_in, sinks, q_lat_norm, kv_lat_norm, w_uq, w_ukv, w_o):
    T, D = h.shape
    n_in, tn = 2560, 512
    sa = HEAD_DIM ** -0.5 * LOG2E
    prog = ([("rot128", sa)] * 8 + [("rot128", 1.0)] * 2 + [("copy", 1.0)] * 8 + [("rot64", 1.0), ("copy", 1.0)])
    y, yf = _mm([(h, 0, D)], [_pad_cols(w_in.astype(BF16), n_in)], gain=norm_g, tn=tn, out_dtype=BF16,
                col_prog=prog, tables=tabs, f32_from=12 * LANES // tn, name="ab_in")
    wq = w_uq.reshape(B_Q_LORA, B_HEADS, B_NOPE + B_ROPE)
    wq_n = wq[:, :, :B_NOPE].reshape(B_Q_LORA, B_HEADS * B_NOPE)
    wq_r = jnp.pad(wq[:, :, B_NOPE:], ((0, 0), (0, 0), (0, LANES - B_ROPE))).reshape(B_Q_LORA, B_HEADS * LANES)
    sb = (B_NOPE + B_ROPE) ** -0.5 * LOG2E
    qq = _mm([(yf, 0, B_Q_LORA)], [jnp.concatenate([wq_n, wq_r], 1).astype(BF16)], gain=q_lat_norm, tn=512,
             out_dtype=BF16, col_prog=[("copy", sb)] * 8 + [("rot64", sb)] * 8, tables=tabs, name="mla_uq")
    kv = _mm([(yf, B_Q_LORA // B_KV_LORA, B_KV_LORA)], [w_ukv.astype(BF16)], gain=kv_lat_norm, out_dtype=BF16,
             tn=512, name="mla_ukv")
    r3 = lambda a: a.reshape(B, S, a.shape[-1])
    y3 = r3(y)
    o_a = _gqa_flash((y3, 0), (y3, 8), (y3, 10), name="swa_attn", B=B, S=S, Hk=A_KV_HEADS,
                     G=A_HEADS // A_KV_HEADS, window=A_WINDOW, sinks=sinks, tq=256, tk=256)
    hp = 2
    head = lambda h: h
    o_b = _flash([(r3(qq), hp * LANES, head), (r3(qq), hp * LANES, lambda h: B_HEADS // hp + h)],
                 [(r3(kv), hp * 2 * LANES, head), (y3, LANES, lambda h: 18)],
                 [[(0, g * LANES), (1, g * LANES)] for g in range(hp)],
                 [[(0, 2 * g * LANES), (1, 0)] for g in range(hp)], [(0, (2 * g + 1) * LANES) for g in range(hp)],
                 name="mla_attn", B=B, S=S, n_steps=B_HEADS // hp, tq=1024, tk=1024, rc=256)
    wo = w_o.astype(BF16)
    na = A_HEADS * HEAD_DIM
    return _mm([(o_a.reshape(T, -1), 0, na), (o_b.reshape(T, -1), 0, B_HEADS * B_V)], [wo[:na], wo[na:]],
               resid=h, tm=512, tn=D, name="ab_out")


def _layer_odd(h, B, S, tabs, cmp_tabs, norm_g, w_in, forget_bias, pe_k, w1_k, w2_k, pe_v, w1_v, w2_v, w_o):
    T, D = h.shape
    hc, kvw = C_HEADS * HEAD_DIM, D_KV_HEADS * HEAD_DIM
    o_cf = 3 * hc
    o_dq = o_cf + C_HEADS
    o_kc = o_dq + D_HEADS * HEAD_DIM
    o_ks = o_kc + 2 * kvw
    o_dg = o_ks + 4 * kvw
    n_in, tn = 5760, 640
    w16 = w_in.astype(BF16)
    w_r = jnp.concatenate([w16[:, :o_cf], w16[:, o_dq:o_kc], w16[:, o_ks:o_dg], w16[:, o_kc:o_ks], w16[:, o_cf:o_dq],
                           w16[:, o_dg:], jnp.zeros((D, n_in - w_in.shape[1]), BF16)], axis=1)
    sc = HEAD_DIM ** -0.5 * LOG2E
    prog = ([("copy", sc)] * 8 + [("copy", 1.0)] * 16 + [("rot128", sc)] * 8
            + [("rot128", 1.0)] * 2 + [("copy", 1.0)] * 2 + [("rot128", 1.0)] * 2 + [("copy", 1.0)] * 7)
    y, yf = _mm([(h, 0, D)], [w_r], gain=norm_g, tn=tn, out_dtype=BF16,
                col_prog=prog, tables=tabs, f32_from=n_in // tn - 1, name="cd_in")
    misc = tn // LANES - 1
    r3 = lambda a: a.reshape(B, S, a.shape[-1])
    y3, yf3 = r3(y), r3(yf)
    fb = jnp.pad(forget_bias.astype(F32), (0, LANES - C_HEADS)).reshape(1, LANES)
    cum = _forget_cum(yf3, misc, fb)
    cumt = jnp.swapaxes(cum[:, :, :C_HEADS], 1, 2).reshape(B, C_HEADS, 1, S)
    hp = 2
    at = lambda first: (lambda h: first + h)
    o_c = _flash([(y3, hp * LANES, at(0))], [(y3, hp * LANES, at(C_HEADS // hp)), (y3, hp * LANES, at(C_HEADS))],
                 [[(0, g * LANES)] for g in range(hp)], [[(0, g * LANES)] for g in range(hp)],
                 [(1, g * LANES) for g in range(hp)], name="fox_attn", B=B, S=S, n_steps=C_HEADS // hp,
                 tq=1024, tk=1024, cum=cum, cumt=cumt)
    G = D_HEADS // D_KV_HEADS
    k_cmp = _compress(yf3, 0, pe_k, w1_k, w2_k, rope_tabs=cmp_tabs)
    v_cmp = _compress(yf3, kvw, pe_v, w1_v, w2_v)
    q_d = (y3, 24 // G)
    o_cmp, sel = _cmp_attn(y3, q_d[1], k_cmp, v_cmp)
    o_slc = _gqa_flash(q_d, (y3, 32), (y3, 34), name="nsa_slc_attn", B=B, S=S, Hk=D_KV_HEADS, G=G, sel=sel,
                       tq=1024, tk=1024, rc=256)
    o_win = _gqa_flash(q_d, (y3, 36), (y3, 38), name="nsa_win_attn", B=B, S=S, Hk=D_KV_HEADS, G=G,
                       window=D_WINDOW, tq=512, tk=512)
    o_d = _nsa_gate(yf, misc, C_HEADS, o_cmp.reshape(T, -1), o_slc.reshape(T, -1), o_win.reshape(T, -1))
    wo = w_o.astype(BF16)
    return _mm([(o_c.reshape(T, -1), 0, hc), (o_d, 0, D_HEADS * HEAD_DIM)], [wo[:hc], wo[hc:]],
               resid=h, tm=512, tn=D, name="cd_out")


def kernel(x, p, positions, ab_w_in, ab_sinks, ab_q_lat_norm, ab_kv_lat_norm, ab_w_uq, ab_w_ukv, ab_w_o,
           cd_w_in, cd_forget_bias, cd_cmp_pe_k, cd_cmp_w1_k, cd_cmp_w2_k, cd_cmp_pe_v, cd_cmp_w1_v,
           cd_cmp_w2_v, cd_w_o, mixer_norm, moe_norm, router_group_w, router_group_b, router_expert_w,
           router_expert_b, expert_w_gate, expert_w_up, expert_w_down, ple_proj, ple_gate_norm, ple_gate_w,
           final_norm):
    B, S, D = x.shape
    T = B * S
    depth = p.shape[0]
    tabs = _rope_tables(positions)
    nc = S // D_CMP_STRIDE
    end = np.minimum(np.arange(nc) * D_CMP_STRIDE + D_CMP_LEN - 1, S - 1)
    cmp_tabs = [t.reshape(B, nc, LANES) for t in _rope_tables(positions[:, end])[:2]]
    h = x.reshape(T, D)
    for i in range(depth):
        j = i // 2
        if i % 2 == 0:
            h = _layer_even(h, B, S, tabs, mixer_norm[i], ab_w_in[j], ab_sinks[j], ab_q_lat_norm[j],
                            ab_kv_lat_norm[j], ab_w_uq[j], ab_w_ukv[j], ab_w_o[j])
        else:
            h = _layer_odd(h, B, S, tabs, cmp_tabs, mixer_norm[i], cd_w_in[j], cd_forget_bias[j],
                           cd_cmp_pe_k[j], cd_cmp_w1_k[j], cd_cmp_w2_k[j], cd_cmp_pe_v[j], cd_cmp_w1_v[j],
                           cd_cmp_w2_v[j], cd_w_o[j])
        h = _moe(h, moe_norm[i], router_group_w[i], router_group_b[i], router_expert_w[i], router_expert_b[i],
                 i, expert_w_gate, expert_w_up, expert_w_down)
        h = _mm([(h, 0, D)], [ple_gate_w[i].astype(BF16)], gain=ple_gate_norm[i], tm=512, tn=D,
                ple=(h, p[i].reshape(T, -1), ple_proj[i].astype(BF16)),
                out_gain=final_norm if i == depth - 1 else None, name="ple")
    return h.reshape(B, S, D)
```

```python
import functools
import math

import numpy as np
import jax
import jax.numpy as jnp
from jax import lax
from jax.experimental import pallas as pl
from jax.experimental.pallas import tpu as pltpu

F32 = jnp.float32
BF16 = jnp.bfloat16

HEAD_DIM = 128
ROPE_THETA = 10000.0
NORM_EPS = 1e-6
NEG_INF = -1e30
TAKEN = -3e38
A_HEADS, A_KV_HEADS, A_WINDOW = 8, 2, 128
B_HEADS, B_Q_LORA, B_KV_LORA, B_NOPE, B_ROPE, B_V = 8, 512, 256, 128, 64, 128
C_HEADS = 8
D_HEADS, D_KV_HEADS = 8, 2
D_CMP_LEN, D_CMP_STRIDE, D_SLC_LEN, D_SLC_TOPN, D_WINDOW = 32, 16, 64, 8, 512
FORCE_BONUS = 1e4
N_GROUPS, EXPERTS_PER_GROUP, TOP_K = 4, 8, 2
N_EXPERTS = N_GROUPS * EXPERTS_PER_GROUP

LANES = 128
VMEM_LIMIT_BYTES = 56 * 1024 * 1024
MOE_ROWS = 256
DMA_UNROLL = 16
FLASH_ROW_CHUNK = 128
LOG2E = math.log2(math.e)


def _params(*sem):
    return pltpu.CompilerParams(dimension_semantics=sem, vmem_limit_bytes=VMEM_LIMIT_BYTES)


def _tile(n, pref):
    t = min(n, pref)
    while n % t:
        t -= 1
    return t


def _pack_bf16_pairs(y):
    n = y.shape[1] // 2
    bits = lambda a: lax.bitcast_convert_type(a.astype(BF16).astype(F32), jnp.uint32)
    return (bits(y[:, :n]) >> 16) | (bits(y[:, n:]) & jnp.uint32(0xFFFF0000))


def _unpack_bf16_pairs(w):
    return (lax.bitcast_convert_type(w << 16, F32), lax.bitcast_convert_type(w & jnp.uint32(0xFFFF0000), F32))


def _rot128(x, c, s):
    return x * c + pltpu.roll(x, 64, 1) * s


def _rot_perms():
    i = np.arange(LANES)
    p128 = (i[:, None] == (i[None, :] + HEAD_DIM // 2) % LANES)
    half = B_ROPE // 2
    p64 = ((i[None, :] < half) & (i[:, None] == i[None, :] + half)) | (
        (i[None, :] >= half) & (i[None, :] < B_ROPE) & (i[:, None] == i[None, :] - half))
    return [jnp.asarray(p.astype(np.float32), BF16) for p in (p128, p64)]


def _tiles_store(ref, lead, words):
    rows, c = words.shape[0], words.shape[1] // LANES
    for j in range(c):
        ref[lead + (pl.ds(j, rows, stride=c), slice(None))] = words[:, j * LANES:(j + 1) * LANES]


def _tiles_load(ref, lead, rows, c):
    return jnp.concatenate([ref[lead + (pl.ds(j, rows, stride=c), slice(None))] for j in range(c)], axis=1)


def _mm_body(*refs, nx, has_gain, mode, resid_is_x, has_out_gain, col_prog, f32_from):
    it = iter(refs)
    x_refs = [next(it) for _ in range(nx)]
    g_ref = next(it) if has_gain else None
    w_refs = [next(it) for _ in range(nx)]
    r_ref = next(it) if mode in ("resid", "ple") and not resid_is_x else None
    p_ref = next(it) if mode == "ple" else None
    wp_ref = next(it) if mode == "ple" else None
    og_ref = next(it) if has_out_gain else None
    tab_refs = [next(it) for _ in range(7)] if col_prog is not None else None
    o_ref = next(it)
    of_ref = next(it) if f32_from is not None else None
    xn_ref = next(it) if has_gain else None

    if has_gain:
        @pl.when(pl.program_id(1) == 0)
        def _():
            x = x_refs[0][...].astype(F32)
            y = x * lax.rsqrt(jnp.mean(x * x, axis=-1, keepdims=True) + NORM_EPS) * g_ref[...]
            xn_ref[...] = y.astype(BF16)
        lhs = [xn_ref[...]]
    else:
        lhs = [x_ref[...].astype(BF16) for x_ref in x_refs]
    acc = None
    for a, w_ref in zip(lhs, w_refs):
        d = jnp.dot(a, w_ref[...], preferred_element_type=F32)
        acc = d if acc is None else acc + d
    if resid_is_x:
        r_ref = x_refs[0]
    if mode == "resid":
        acc = r_ref[...] + acc
    elif mode == "ple":
        pp = jnp.dot(p_ref[...].astype(BF16), wp_ref[...], preferred_element_type=F32)
        acc = r_ref[...] + pp * jax.nn.sigmoid(acc)
    if has_out_gain:
        acc = acc * lax.rsqrt(jnp.mean(acc * acc, axis=-1, keepdims=True) + NORM_EPS) * og_ref[...]
    if col_prog is None:
        o_ref[...] = acc.astype(o_ref.dtype)
        return
    j = pl.program_id(1)
    nb = o_ref.shape[1] // LANES
    tiles = {}
    for jj in range(len(col_prog) // nb):
        tiles.setdefault(tuple(col_prog[jj * nb:(jj + 1) * nb]), []).append(jj)
    for prog, jjs in tiles.items():
        @pl.when(functools.reduce(jnp.logical_or, [j == jj for jj in jjs]))
        def _():
            c128, s128, c64, s64a, s64b, p128, p64 = tab_refs
            for b, (kind, scale) in enumerate(prog):
                x = acc[:, b * LANES:(b + 1) * LANES]
                if kind == "rot128":
                    x = x * c128[...] + jnp.dot(x.astype(BF16), p128[...], preferred_element_type=F32) * s128[...]
                elif kind == "rot64":
                    x = x * c64[...] + jnp.dot(x.astype(BF16), p64[...], preferred_element_type=F32) * (
                        s64a[...] + s64b[...])
                if scale != 1.0:
                    x = x * scale
                o_ref[:, b * LANES:(b + 1) * LANES] = x.astype(o_ref.dtype)
    if f32_from is not None:
        @pl.when(j >= f32_from)
        def _():
            of_ref[...] = acc


def _mm(xs, ws, *, name, gain=None, out_dtype=F32, tm=1024, tn=512, resid=None, ple=None,
        out_gain=None, col_prog=None, tables=None, f32_from=None):
    M = xs[0][0].shape[0]
    N = ws[0].shape[1]
    tm, tn = _tile(M, tm), _tile(N, tn)
    nx = len(xs)
    has_gain = gain is not None
    mode = "ple" if ple is not None else ("resid" if resid is not None else "none")
    r_arr = ple[0] if mode == "ple" else resid
    resid_is_x = r_arr is xs[0][0] and tn == N == xs[0][2] and xs[0][1] == 0
    assert out_gain is None or tn == N
    args, in_specs = [], []
    for arr, cb, K in xs:
        args.append(arr)
        in_specs.append(pl.BlockSpec((tm, K), lambda i, j, cb=cb: (i, cb)))
    if has_gain:
        K0 = xs[0][2]
        args.append(gain.reshape(1, K0).astype(F32))
        in_specs.append(pl.BlockSpec((1, K0), lambda i, j: (0, 0)))
    for (arr, cb, K), w in zip(xs, ws):
        args.append(w)
        in_specs.append(pl.BlockSpec((K, tn), lambda i, j: (0, j)))
    if mode in ("resid", "ple") and not resid_is_x:
        args.append(r_arr)
        in_specs.append(pl.BlockSpec((tm, tn), lambda i, j: (i, j)))
    if mode == "ple":
        _, p, wp = ple
        args += [p, wp]
        in_specs += [pl.BlockSpec((tm, p.shape[1]), lambda i, j: (i, 0)),
                     pl.BlockSpec((p.shape[1], tn), lambda i, j: (0, j))]
    if out_gain is not None:
        args.append(out_gain.reshape(1, N).astype(F32))
        in_specs.append(pl.BlockSpec((1, N), lambda i, j: (0, 0)))
    if col_prog is not None:
        assert len(col_prog) * LANES == N
        args += list(tables) + _rot_perms()
        in_specs += [pl.BlockSpec((tm, LANES), lambda i, j: (i, 0))] * 5
        in_specs += [pl.BlockSpec((LANES, LANES), lambda i, j: (0, 0))] * 2
    out_shape = [jax.ShapeDtypeStruct((M, N), out_dtype)]
    out_specs = [pl.BlockSpec((tm, tn), lambda i, j: (i, j))]
    if f32_from is not None:
        out_shape.append(jax.ShapeDtypeStruct((M, N - f32_from * tn), F32))
        out_specs.append(pl.BlockSpec((tm, tn), lambda i, j: (i, jnp.maximum(j - f32_from, 0))))
    scratch = [pltpu.VMEM((tm, xs[0][2]), BF16)] if has_gain else []
    res = pl.pallas_call(
        functools.partial(_mm_body, nx=nx, has_gain=has_gain, mode=mode, resid_is_x=resid_is_x,
                          has_out_gain=out_gain is not None,
                          col_prog=None if col_prog is None else tuple(col_prog), f32_from=f32_from),
        out_shape=out_shape, grid=(M // tm, N // tn), in_specs=in_specs, out_specs=out_specs,
        scratch_shapes=scratch, compiler_params=_params("arbitrary", "arbitrary"), name=name,
    )(*args)
    return res if len(res) > 1 else res[0]


def _rope_tables(positions):
    def tables(dim):
        inv = 1.0 / (ROPE_THETA ** (jnp.arange(0, dim, 2, dtype=F32) / dim))
        ang = positions.astype(F32)[..., None] * inv
        return jnp.cos(ang), jnp.sin(ang)

    lead = positions.shape
    c, s = tables(HEAD_DIM)
    c128 = jnp.concatenate([c, c], -1)
    s128 = jnp.concatenate([-s, s], -1)
    c, s = tables(B_ROPE)
    z = jnp.zeros_like(c)
    c64 = jnp.concatenate([c, c, z, z], -1)
    s64a = jnp.concatenate([-s, z, z, z], -1)
    s64b = jnp.concatenate([z, s, z, z], -1)
    n = int(np.prod(lead))
    return [t.reshape(n, LANES) for t in (c128, s128, c64, s64a, s64b)]


def _pair_tables(S, tq, tk, window):
    qi_l, ki_l, fl_l, mk_l, masks, ids = [], [], [], [], [], {}
    for qi in range(S // tq):
        q0, q1 = qi * tq, (qi + 1) * tq - 1
        ks = []
        for ki in range(S // tk):
            k0, k1 = ki * tk, (ki + 1) * tk - 1
            if k0 > q1 or (window is not None and q0 - k1 >= window):
                continue
            full = k1 <= q0 and (window is None or q1 - k0 < window)
            mid = 0
            if not full:
                delta = q0 - k0
                if delta not in ids:
                    rel = np.arange(tq)[:, None] + delta - np.arange(tk)[None, :]
                    ok = (rel >= 0) if window is None else ((rel >= 0) & (rel < window))
                    masks.append(np.where(ok, 0.0, NEG_INF).astype(np.float32))
                    ids[delta] = len(masks)
                mid = ids[delta]
            ks.append((ki, mid))
        for n, (ki, mid) in enumerate(ks):
            qi_l.append(qi)
            ki_l.append(ki)
            fl_l.append((1 if n == 0 else 0) | (2 if n == len(ks) - 1 else 0))
            mk_l.append(mid)
    if not masks:
        masks.append(np.zeros((tq, tk), np.float32))
    tabs = [jnp.asarray(np.array(a, np.int32)) for a in (qi_l, ki_l, fl_l, mk_l)]
    return tabs, jnp.asarray(np.stack(masks))


def _flash_body(qi_t, ki_t, fl_t, mk_t, *refs, nq, nkv, q_src, k_src, v_src, tq, rc, causal_diag,
                has_sink, has_bias, has_sel):
    nc = len(q_src)
    it = iter(refs)
    q_refs = [next(it) for _ in range(nq)]
    kv_refs = [next(it) for _ in range(nkv)]
    mask_ref = next(it)
    sink_ref = next(it) if has_sink else None
    cum_ref = next(it) if has_bias else None
    cumt_ref = next(it) if has_bias else None
    sel_ref = next(it) if has_sel else None
    exp_ref = next(it) if has_sel else None
    o_ref = next(it)
    m_sc, acc_sc = next(it), next(it)
    cq_sc = next(it) if has_bias else None

    step_id = pl.program_id(2)
    fl, mk = fl_t[step_id], mk_t[step_id]

    def cat(refs_, src, rows=slice(None)):
        xs = [refs_[pi][0, rows, off:off + LANES] for pi, off in src]
        return xs[0] if len(xs) == 1 else jnp.concatenate(xs, axis=-1)

    @pl.when((fl & 1) != 0)
    def _init():
        m_sc[...] = jnp.full(m_sc.shape, NEG_INF, F32)
        acc_sc[...] = jnp.zeros(acc_sc.shape, F32)
        if has_bias:
            lane = lax.broadcasted_iota(jnp.int32, (tq, LANES), 1)
            for g in range(nc):
                head = pl.program_id(1) * nc + g
                cq_sc[g] = jnp.sum(jnp.where(lane == head, cum_ref[0], 0.0), axis=-1, keepdims=True)

    tk = kv_refs[0].shape[1]
    ones = jnp.ones((tk, LANES), BF16)

    def step(masked):
        ks = [cat(kv_refs, k_src[g]) for g in range(nc)]
        vs = [jnp.concatenate([kv_refs[pi][0, :, off:off + LANES], ones], axis=1) for pi, off in v_src]
        for r in range(tq // rc):
            rows = slice(r * rc, (r + 1) * rc)
            nk = min((r + 1) * rc, tk) if (masked and causal_diag) else tk
            reps = nk // LANES
            add = mask_ref[mk - 1, rows, :nk] if masked else None
            if has_sel:
                hidden = (jnp.dot(sel_ref[0, 0, rows, :], exp_ref[0, :, :nk], preferred_element_type=F32)
                          - 1.0) * (-NEG_INF)
                add = hidden if add is None else add + hidden
            for g in range(nc):
                q = cat(q_refs, q_src[g], rows)
                s = lax.dot_general(q, ks[g][:nk], (((1,), (1,)), ((), ())), preferred_element_type=F32)
                if has_bias:
                    s = s + (cq_sc[g, rows, :] - cumt_ref[0, g, :, :nk])
                if add is not None:
                    s = s + add
                m_prev = m_sc[g, rows, :]
                m_new = jnp.maximum(m_prev, jnp.max(s, axis=-1, keepdims=True))
                p = jnp.exp2(s - (jnp.concatenate([m_new] * reps, axis=1) if reps > 1 else m_new))
                alpha = jnp.exp2(m_prev - m_new)
                acc_sc[g, rows, :] = (jnp.concatenate([alpha, alpha], axis=1) * acc_sc[g, rows, :]
                                      + jnp.dot(p.astype(BF16), vs[g][:nk], preferred_element_type=F32))
                m_sc[g, rows, :] = m_new

    @pl.when(mk != 0)
    def _():
        step(True)

    @pl.when(mk == 0)
    def _():
        step(False)

    @pl.when((fl & 2) != 0)
    def _finish():
        for g in range(nc):
            m, acc, l = m_sc[g], acc_sc[g, :, :LANES], acc_sc[g, :, LANES:]
            if has_sink:
                sk = sink_ref[0, g:g + 1, 0:1] * LOG2E
                m_f = jnp.maximum(m, sk)
                w = jnp.exp2(m - m_f)
                l = l * w + jnp.exp2(sk - m_f)
                acc = acc * w
            o_ref[0, :, g * LANES:(g + 1) * LANES] = (acc / l).astype(o_ref.dtype)


def _flash(q_parts, kv_parts, q_src, k_src, v_src, *, name, B, S, n_steps, tq, tk, window=None, sinks=None,
           cum=None, cumt=None, sel=None, out_dtype=BF16, rc=FLASH_ROW_CHUNK):
    nc = len(q_src)
    tq, tk = _tile(S, tq), _tile(S, tk)
    tabs, masks = _pair_tables(S, tq, tk, window)
    npairs = int(tabs[0].shape[0])
    args, in_specs = [], []
    for arr, width, cf in q_parts:
        args.append(arr)
        in_specs.append(pl.BlockSpec((1, tq, width), lambda b, h, s, qt, kt, ft, mt, cf=cf: (b, qt[s], cf(h))))
    for arr, width, cf in kv_parts:
        args.append(arr)
        in_specs.append(pl.BlockSpec((1, tk, width), lambda b, h, s, qt, kt, ft, mt, cf=cf: (b, kt[s], cf(h))))
    args.append(masks)
    in_specs.append(pl.BlockSpec(masks.shape, lambda b, h, s, qt, kt, ft, mt: (0, 0, 0)))
    if sinks is not None:
        args.append(jnp.broadcast_to(sinks.astype(F32).reshape(n_steps, nc, 1), (n_steps, nc, LANES)))
        in_specs.append(pl.BlockSpec((1, nc, LANES), lambda b, h, s, qt, kt, ft, mt: (h, 0, 0)))
    if cum is not None:
        args += [cum, cumt]
        in_specs += [pl.BlockSpec((1, tq, LANES), lambda b, h, s, qt, kt, ft, mt: (b, qt[s], 0)),
                     pl.BlockSpec((1, nc, 1, tk), lambda b, h, s, qt, kt, ft, mt: (b, h, 0, kt[s]))]
    if sel is not None:
        per = tk // D_SLC_LEN
        e = np.zeros((S // tk, LANES, tk), np.float32)
        for ki in range(S // tk):
            e[ki, ki * per + np.arange(tk) // D_SLC_LEN, np.arange(tk)] = 1.0
        args += [sel, jnp.asarray(e, BF16)]
        in_specs += [pl.BlockSpec((1, 1, tq, LANES), lambda b, h, s, qt, kt, ft, mt: (b, h, qt[s], 0)),
                     pl.BlockSpec((1, LANES, tk), lambda b, h, s, qt, kt, ft, mt: (kt[s], 0, 0))]
    scratch = [pltpu.VMEM((nc, tq, LANES), F32), pltpu.VMEM((nc, tq, 2 * LANES), F32)]
    if cum is not None:
        scratch.append(pltpu.VMEM((nc, tq, 1), F32))
    body = functools.partial(_flash_body, nq=len(q_parts), nkv=len(kv_parts), q_src=q_src, k_src=k_src,
                             v_src=v_src, tq=tq, rc=_tile(tq, rc), causal_diag=window is None and tq == tk,
                             has_sink=sinks is not None,
                             has_bias=cum is not None, has_sel=sel is not None)
    return pl.pallas_call(
        body, out_shape=jax.ShapeDtypeStruct((B, S, n_steps * nc * LANES), out_dtype),
        grid_spec=pltpu.PrefetchScalarGridSpec(
            num_scalar_prefetch=4, grid=(B, n_steps, npairs), in_specs=in_specs,
            out_specs=pl.BlockSpec((1, tq, nc * LANES), lambda b, h, s, qt, kt, ft, mt: (b, qt[s], h)),
            scratch_shapes=scratch),
        compiler_params=_params("arbitrary", "arbitrary", "arbitrary"), name=name,
    )(*tabs, *args)


def _gqa_flash(q, k, v, *, name, B, S, Hk, G, **kw):
    at = lambda first: (lambda h: first + h)
    return _flash([(q[0], G * LANES, at(q[1]))], [(k[0], LANES, at(k[1])), (v[0], LANES, at(v[1]))],
                  [[(0, g * LANES)] for g in range(G)], [[(0, 0)]] * G, [(1, 0)] * G,
                  name=name, B=B, S=S, n_steps=Hk, **kw)


def _cum_body(y_ref, b_ref, tri_ref, o_ref, carry):
    @pl.when(pl.program_id(1) == 0)
    def _():
        carry[...] = jnp.zeros(carry.shape, F32)
    x = y_ref[0] + b_ref[...]
    logf = jnp.minimum(x, 0.0) - jnp.log1p(jnp.exp(-jnp.abs(x)))
    cum = jnp.dot(tri_ref[...], logf, preferred_element_type=F32, precision=lax.Precision.HIGHEST) + carry[...]
    o_ref[0] = cum * LOG2E
    carry[...] = cum[-1:, :]


def _forget_cum(y3, col_block, bias_row, *, ts=512):
    B, S, _ = y3.shape
    ts = _tile(S, ts)
    tri = jnp.asarray(np.tril(np.ones((ts, ts), np.float32)))
    return pl.pallas_call(
        _cum_body, out_shape=jax.ShapeDtypeStruct((B, S, LANES), F32), grid=(B, S // ts),
        in_specs=[pl.BlockSpec((1, ts, LANES), lambda b, s: (b, s, col_block)),
                  pl.BlockSpec((1, LANES), lambda b, s: (0, 0)),
                  pl.BlockSpec((ts, ts), lambda b, s: (0, 0))],
        out_specs=pl.BlockSpec((1, ts, LANES), lambda b, s: (b, s, 0)),
        scratch_shapes=[pltpu.VMEM((1, LANES), F32)],
        compiler_params=_params("arbitrary", "arbitrary"), name="forget_cum",
    )(y3, bias_row, tri)


def _compress_body(*refs, rope, nc, width, col0):
    if rope:
        z_ref, pe_ref, w1_ref, w2_ref, c_ref, s_ref, o_ref = refs
    else:
        z_ref, pe_ref, w1_ref, w2_ref, o_ref = refs
    half = D_CMP_LEN // 2
    for hk in range(D_KV_HEADS):
        u = jnp.zeros((nc, w1_ref.shape[1]), F32)
        v = jnp.zeros((nc, w1_ref.shape[1]), F32)
        for l in range(half):
            first = l * width + col0 + hk * HEAD_DIM
            z = z_ref[0, :, first:first + HEAD_DIM]
            zu = (z + pe_ref[l:l + 1, :]).astype(BF16)
            zv = (z + pe_ref[half + l:half + l + 1, :]).astype(BF16)
            u = u + jnp.dot(zu, w1_ref[l * HEAD_DIM:(l + 1) * HEAD_DIM, :], preferred_element_type=F32)
            v = v + jnp.dot(zv, w1_ref[(half + l) * HEAD_DIM:(half + l + 1) * HEAD_DIM, :],
                            preferred_element_type=F32)
        pre = u + pltpu.roll(v, nc - 1, 0)
        hid = jax.nn.gelu(pre, approximate=True)
        out = jnp.dot(hid.astype(BF16), w2_ref[...], preferred_element_type=F32)
        if rope:
            out = _rot128(out, c_ref[0], s_ref[0])
        o_ref[0, hk] = out.astype(o_ref.dtype)


def _compress(z, col0, pe, w1, w2, rope_tabs=None):
    B, S, W = z.shape
    nc = S // D_CMP_STRIDE
    zc = z.reshape(B, nc, D_CMP_STRIDE * W)
    args = [zc, pe.astype(F32), w1.astype(BF16), w2.astype(BF16)]
    in_specs = [pl.BlockSpec((1, nc, D_CMP_STRIDE * W), lambda b: (b, 0, 0)),
                pl.BlockSpec(pe.shape, lambda b: (0, 0)),
                pl.BlockSpec(w1.shape, lambda b: (0, 0)),
                pl.BlockSpec(w2.shape, lambda b: (0, 0))]
    if rope_tabs is not None:
        args += list(rope_tabs)
        in_specs += [pl.BlockSpec((1, nc, LANES), lambda b: (b, 0, 0))] * 2
    return pl.pallas_call(
        functools.partial(_compress_body, rope=rope_tabs is not None, nc=nc, width=W, col0=col0),
        out_shape=jax.ShapeDtypeStruct((B, D_KV_HEADS, nc, HEAD_DIM), BF16), grid=(B,),
        in_specs=in_specs, out_specs=pl.BlockSpec((1, D_KV_HEADS, nc, HEAD_DIM), lambda b: (b, 0, 0, 0)),
        compiler_params=_params("arbitrary"), name="nsa_compress",
    )(*args)


def _cmp_attn_body(q_ref, k_ref, v_ref, ov_ref, o_ref, sel_ref, *, G, tq, nc, n_cmp, n_slc, topn):
    qi = pl.program_id(2)
    t = qi * tq + lax.broadcasted_iota(jnp.int32, (tq, nc), 0)
    c = lax.broadcasted_iota(jnp.int32, (tq, nc), 1)
    valid = (c * D_CMP_STRIDE + (D_CMP_LEN - 1) <= t) & (c < n_cmp)
    k = k_ref[0, 0]
    v = v_ref[0, 0]
    psum = jnp.zeros((tq, nc), F32)
    for g in range(G):
        q = q_ref[0, :, g * LANES:(g + 1) * LANES]
        s = lax.dot_general(q, k, (((1,), (1,)), ((), ())), preferred_element_type=F32)
        s = jnp.where(valid, s, NEG_INF)
        e = jnp.where(valid, jnp.exp2(s - jnp.max(s, axis=-1, keepdims=True)), 0.0)
        p = e / jnp.maximum(jnp.sum(e, axis=-1, keepdims=True), jnp.finfo(F32).tiny)
        o_ref[0, :, g * LANES:(g + 1) * LANES] = jnp.dot(
            p.astype(BF16), v, preferred_element_type=F32).astype(o_ref.dtype)
        psum = psum + p
    imp = lax.dot_general(ov_ref[...], psum, (((1,), (1,)), ((), ())), preferred_element_type=F32,
                          precision=lax.Precision.HIGHEST)
    blk = lax.broadcasted_iota(jnp.int32, (LANES, tq), 0)
    tcol = qi * tq + lax.broadcasted_iota(jnp.int32, (LANES, tq), 1)
    cur = jnp.right_shift(tcol, int(math.log2(D_SLC_LEN)))
    forced = (blk == 0) | (blk == cur) | (blk == cur - 1)
    imp = jnp.where(blk * D_SLC_LEN > tcol, NEG_INF, imp + jnp.where(forced, FORCE_BONUS, 0.0))
    imp = jnp.where(blk >= n_slc, TAKEN, imp)
    chosen = jnp.zeros((LANES, tq), F32)
    blk_f = blk.astype(F32)
    for _ in range(topn):
        mx = jnp.max(imp, axis=0, keepdims=True)
        idx = jnp.min(jnp.where(imp == mx, blk_f, float(LANES)), axis=0, keepdims=True)
        hit = blk_f == idx
        chosen = jnp.where(hit, 1.0, chosen)
        imp = jnp.where(hit, TAKEN, imp)
    sel_ref[0, 0] = chosen.T.astype(sel_ref.dtype)


def _cmp_attn(q, q_first, k_cmp, v_cmp, *, tq=1024):
    B, S, _ = q.shape
    Hk, G = D_KV_HEADS, D_HEADS // D_KV_HEADS
    nc = S // D_CMP_STRIDE
    n_cmp = (S - D_CMP_LEN) // D_CMP_STRIDE + 1
    n_slc = S // D_SLC_LEN
    tq = _tile(S, tq)
    c0 = np.arange(nc) * D_CMP_STRIDE
    s0 = np.arange(LANES) * D_SLC_LEN
    ov = ((c0[:, None] < (s0 + D_SLC_LEN)[None, :]) & ((c0 + D_CMP_LEN)[:, None] > s0[None, :])
          & (np.arange(nc) < n_cmp)[:, None] & (np.arange(LANES) < n_slc)[None, :]).astype(np.float32)
    body = functools.partial(_cmp_attn_body, G=G, tq=tq, nc=nc, n_cmp=n_cmp, n_slc=n_slc,
                             topn=min(D_SLC_TOPN, n_slc))
    return pl.pallas_call(
        body,
        out_shape=[jax.ShapeDtypeStruct((B, S, Hk * G * LANES), BF16),
                   jax.ShapeDtypeStruct((B, Hk, S, LANES), BF16)],
        grid=(B, Hk, S // tq),
        in_specs=[pl.BlockSpec((1, tq, G * LANES), lambda b, h, i: (b, i, q_first + h)),
                  pl.BlockSpec((1, 1, nc, LANES), lambda b, h, i: (b, h, 0, 0)),
                  pl.BlockSpec((1, 1, nc, LANES), lambda b, h, i: (b, h, 0, 0)),
                  pl.BlockSpec((LANES, nc), lambda b, h, i: (0, 0))],
        out_specs=[pl.BlockSpec((1, tq, G * LANES), lambda b, h, i: (b, i, h)),
                   pl.BlockSpec((1, 1, tq, LANES), lambda b, h, i: (b, h, i, 0))],
        compiler_params=_params("arbitrary", "arbitrary", "arbitrary"), name="nsa_cmp_attn",
    )(q, k_cmp, v_cmp, jnp.asarray(ov.T))


def _gate_body(y_ref, a_ref, b_ref, c_ref, o_ref, *, lane0):
    g = jax.nn.sigmoid(y_ref[...])
    for h in range(D_HEADS):
        cols = slice(h * LANES, (h + 1) * LANES)
        ga = g[:, lane0 + h:lane0 + h + 1]
        gb = g[:, lane0 + D_HEADS + h:lane0 + D_HEADS + h + 1]
        gc = g[:, lane0 + 2 * D_HEADS + h:lane0 + 2 * D_HEADS + h + 1]
        o_ref[:, cols] = (ga * a_ref[:, cols] + gb * b_ref[:, cols] + gc * c_ref[:, cols]).astype(o_ref.dtype)


def _nsa_gate(y, col_block, lane0, o_cmp, o_slc, o_win, *, tm=512):
    M = y.shape[0]
    W = o_cmp.shape[1]
    tm = _tile(M, tm)
    row = lambda i: (i, 0)
    return pl.pallas_call(
        functools.partial(_gate_body, lane0=lane0), out_shape=jax.ShapeDtypeStruct((M, W), BF16),
        grid=(M // tm,),
        in_specs=[pl.BlockSpec((tm, LANES), lambda i: (i, col_block))] + [pl.BlockSpec((tm, W), row)] * 3,
        out_specs=pl.BlockSpec((tm, W), row), compiler_params=_params("arbitrary"), name="nsa_gate",
    )(y, o_cmp, o_slc, o_win)


def _route_body(h_ref, g_ref, wr_ref, b_ref, tri_ref, e_ref, w_ref, pos_ref, cnt_ref, xo_ref, carry, *, tm):
    @pl.when(pl.program_id(0) == 0)
    def _():
        carry[...] = jnp.zeros(carry.shape, F32)
    x = h_ref[...]
    y = x * lax.rsqrt(jnp.mean(x * x, axis=-1, keepdims=True) + NORM_EPS) * g_ref[...]
    _tiles_store(xo_ref, (), _pack_bf16_pairs(y))
    lane = lax.broadcasted_iota(jnp.int32, (tm, LANES), 1)
    logits = jnp.dot(y.astype(BF16), wr_ref[...], preferred_element_type=F32) + b_ref[...]
    gl = jnp.where(lane < N_GROUPS, logits, -jnp.inf)
    gmax = jnp.max(gl, axis=-1, keepdims=True)
    g_val = 1.0 / jnp.sum(jnp.exp(gl - gmax), axis=-1, keepdims=True)
    g_idx = jnp.min(jnp.where(gl == gmax, lane, LANES), axis=-1, keepdims=True)
    lo = N_GROUPS + EXPERTS_PER_GROUP * g_idx
    el = jnp.where((lane >= lo) & (lane < lo + EXPERTS_PER_GROUP), logits, -jnp.inf)
    e1 = jnp.max(el, axis=-1, keepdims=True)
    i1 = jnp.min(jnp.where(el == e1, lane, LANES), axis=-1, keepdims=True)
    el2 = jnp.where(lane == i1, -jnp.inf, el)
    e2 = jnp.max(el2, axis=-1, keepdims=True)
    i2 = jnp.min(jnp.where(el2 == e2, lane, LANES), axis=-1, keepdims=True)
    r = jnp.exp(e2 - e1)
    w1 = g_val / (1.0 + r)
    w2 = w1 * r
    x1, x2 = i1 - N_GROUPS, i2 - N_GROUPS
    e_ref[...] = jnp.where(lane == 0, x1, jnp.where(lane == 1, x2, 0))
    w_ref[...] = jnp.where(lane == 0, w1, jnp.where(lane == 1, w2, 0.0))
    hot1 = lane == x1
    hot2 = lane == x2
    both = jnp.where(hot1 | hot2, 1.0, 0.0)
    before = jnp.dot(tri_ref[...], both.astype(BF16), preferred_element_type=F32) + carry[...]
    p1 = jnp.sum(jnp.where(hot1, before, 0.0), axis=-1, keepdims=True)
    p2 = jnp.sum(jnp.where(hot2, before, 0.0), axis=-1, keepdims=True)
    pos_ref[...] = jnp.where(lane == 0, p1, jnp.where(lane == 1, p2, 0.0)).astype(jnp.int32)
    carry[...] = carry[...] + jnp.sum(both, axis=0, keepdims=True)
    cnt_ref[...] = carry[...].astype(jnp.int32)


def _route(h, gain, w_router, bias_row, *, tm=512):
    T, D = h.shape
    c = D // 2 // LANES
    tm = _tile(T, tm)
    tri = jnp.asarray(np.tril(np.ones((tm, tm), np.float32), -1), BF16)
    row = lambda i: (i, 0)
    fixed = lambda i: (0, 0)
    return pl.pallas_call(
        functools.partial(_route_body, tm=tm),
        out_shape=[jax.ShapeDtypeStruct((T, LANES), jnp.int32), jax.ShapeDtypeStruct((T, LANES), F32),
                   jax.ShapeDtypeStruct((T, LANES), jnp.int32), jax.ShapeDtypeStruct((1, LANES), jnp.int32),
                   jax.ShapeDtypeStruct((T * c, LANES), jnp.uint32)],
        grid=(T // tm,),
        in_specs=[pl.BlockSpec((tm, D), row), pl.BlockSpec((1, D), fixed), pl.BlockSpec((D, LANES), fixed),
                  pl.BlockSpec((1, LANES), fixed), pl.BlockSpec((tm, tm), fixed)],
        out_specs=[pl.BlockSpec((tm, LANES), row)] * 3 + [pl.BlockSpec((1, LANES), fixed),
                                                          pl.BlockSpec((tm * c, LANES), row)],
        scratch_shapes=[pltpu.VMEM((1, LANES), F32)],
        compiler_params=_params("arbitrary"), name="moe_route",
    )(h, gain.reshape(1, D).astype(F32), w_router, bias_row, tri)


def _dispatch_body(d_ref, ends_ref, pad_ref, nu_ref, x_ref, out_ref, zero_sc, sem, zsem, *, tm, nblk, c):
    base = pl.program_id(0) * tm

    @pl.when(pl.program_id(0) == 0)
    def _():
        zero_sc[...] = jnp.zeros(zero_sc.shape, zero_sc.dtype)

        def zero_block(first_row):
            return pltpu.make_async_copy(
                zero_sc, out_ref.at[pl.ds(pl.multiple_of(first_row * c, MOE_ROWS * c), MOE_ROWS * c)], zsem)

        for wait in (False, True):
            for e in range(N_EXPERTS):
                for live, first_row in ((pad_ref[e] > 0, ends_ref[e] - MOE_ROWS),
                                        (nu_ref[0] + e < nblk, (nu_ref[0] + e) * MOE_ROWS)):
                    @pl.when(live)
                    def _():
                        zero_block(first_row).wait() if wait else zero_block(first_row).start()

    def issue(r, carry):
        for k in range(TOP_K):
            row = d_ref[(base + r) * TOP_K + k]
            pltpu.make_async_copy(x_ref.at[pl.ds(pl.multiple_of(r * c, c), c)],
                                  out_ref.at[pl.ds(pl.multiple_of(row * c, c), c)], sem).start(priority=k)
        return carry

    lax.fori_loop(0, tm, issue, 0, unroll=DMA_UNROLL)
    for _ in range(TOP_K):
        pltpu.make_async_copy(x_ref, out_ref.at[pl.ds(0, tm * c)], sem).wait()


def _dispatch(xn, dest, ends, padded, n_used, n_rows, c, *, tm=512):
    T = xn.shape[0] // c
    tm = _tile(T, tm)
    return pl.pallas_call(
        functools.partial(_dispatch_body, tm=tm, nblk=n_rows // MOE_ROWS, c=c),
        out_shape=jax.ShapeDtypeStruct((n_rows * c, LANES), xn.dtype),
        grid_spec=pltpu.PrefetchScalarGridSpec(
            num_scalar_prefetch=4, grid=(T // tm,),
            in_specs=[pl.BlockSpec((tm * c, LANES), lambda i, *_: (i, 0))],
            out_specs=pl.BlockSpec(memory_space=pl.ANY),
            scratch_shapes=[pltpu.VMEM((MOE_ROWS * c, LANES), xn.dtype), pltpu.SemaphoreType.DMA(()),
                            pltpu.SemaphoreType.DMA(())]),
        compiler_params=pltpu.CompilerParams(dimension_semantics=("arbitrary",), has_side_effects=True,
                                             vmem_limit_bytes=VMEM_LIMIT_BYTES, disable_bounds_checks=True),
        name="moe_dispatch",
    )(dest, ends, padded, n_used, xn)


def _expert_body(nb_ref, first_ref, nu_ref, xb_ref, wg_ref, wu_ref, wd_ref, yb_ref, wg_sc, wu_sc, wd_sc, xbuf,
                 obuf, sem_in, sem_out, state, *, nblk, c):
    e = pl.program_id(0)
    last = pl.num_programs(0) - 1
    nb, first = nb_ref[e], first_ref[e]

    @pl.when(e == 0)
    def _():
        for i in range(3):
            state[i] = 0

    def rows(first_block, b):
        return pl.ds(pl.multiple_of((first_block + b) * (MOE_ROWS * c), MOE_ROWS * c), MOE_ROWS * c)

    def fetch(first_block, b, slot):
        return pltpu.make_async_copy(xb_ref.at[rows(first_block, b)], xbuf.at[slot], sem_in.at[slot])

    def flush(b, slot):
        return pltpu.make_async_copy(obuf.at[slot], yb_ref.at[rows(first, b)], sem_out.at[slot])

    def drain(slot):
        @pl.when(state[slot] != 0)
        def _():
            pltpu.make_async_copy(obuf.at[slot], yb_ref.at[pl.ds(0, MOE_ROWS * c)], sem_out.at[slot]).wait()
            state[slot] = 0

    @pl.when(nb > 0)
    def _():
        @pl.when(state[2] == 0)
        def _():
            fetch(first, 0, 0).start()

        state[2] = 0
        wg_sc[...] = wg_ref[0, 0].astype(BF16)
        wu_sc[...] = wu_ref[0, 0].astype(BF16)
        wd_sc[...] = wd_ref[0, 0].astype(BF16)

        def block(b, carry):
            slot = lax.rem(b, 2)

            @pl.when(b + 1 < nb)
            def _():
                fetch(first, b + 1, 1 - slot).start()

            fetch(first, b, slot).wait()
            drain(slot)
            x = jnp.concatenate(_unpack_bf16_pairs(_tiles_load(xbuf, (slot,), MOE_ROWS, c)), axis=1).astype(BF16)
            gate = jnp.dot(x, wg_sc[...], preferred_element_type=F32)
            up = jnp.dot(x, wu_sc[...], preferred_element_type=F32)
            hid = (gate * jax.nn.sigmoid(gate) * up).astype(BF16)
            _tiles_store(obuf, (slot,), _pack_bf16_pairs(jnp.dot(hid, wd_sc[...], preferred_element_type=F32)))
            flush(b, slot).start()
            state[slot] = 1
            return carry

        lax.fori_loop(0, nb, block, 0)
        nxt = jnp.minimum(e + 1, last)

        @pl.when((e < last) & (nb_ref[nxt] > 0))
        def _():
            fetch(first_ref[nxt], 0, 0).start()
            state[2] = 1

    @pl.when(e == last)
    def _():
        drain(0)
        drain(1)
        obuf[0] = jnp.zeros(obuf.shape[1:], obuf.dtype)

        def tail(t):
            return pltpu.make_async_copy(obuf.at[0], yb_ref.at[rows(nu_ref[0], t)], sem_out.at[0])

        for wait in (False, True):
            for t in range(N_EXPERTS):
                @pl.when(nu_ref[0] + t < nblk)
                def _():
                    tail(t).wait() if wait else tail(t).start()


def _experts(xb, n_blocks, first_block, n_used, layer, w_gate, w_up, w_down):
    D, Hd = w_gate.shape[2], w_gate.shape[3]
    c = D // 2 // LANES
    nblk = xb.shape[0] // (MOE_ROWS * c)
    wspec = lambda shape: pl.BlockSpec((1, 1) + shape, lambda e, *_: (layer, e, 0, 0))
    return pl.pallas_call(
        functools.partial(_expert_body, nblk=nblk, c=c), out_shape=jax.ShapeDtypeStruct(xb.shape, xb.dtype),
        grid_spec=pltpu.PrefetchScalarGridSpec(
            num_scalar_prefetch=3, grid=(w_gate.shape[1],),
            in_specs=[pl.BlockSpec(memory_space=pl.ANY), wspec((D, Hd)), wspec((D, Hd)), wspec((Hd, D))],
            out_specs=pl.BlockSpec(memory_space=pl.ANY),
            scratch_shapes=[pltpu.VMEM((D, Hd), BF16), pltpu.VMEM((D, Hd), BF16), pltpu.VMEM((Hd, D), BF16),
                            pltpu.VMEM((2, MOE_ROWS * c, LANES), xb.dtype),
                            pltpu.VMEM((2, MOE_ROWS * c, LANES), xb.dtype),
                            pltpu.SemaphoreType.DMA((2,)), pltpu.SemaphoreType.DMA((2,)),
                            pltpu.SMEM((3,), jnp.int32)]),
        compiler_params=pltpu.CompilerParams(dimension_semantics=("arbitrary",), has_side_effects=True,
                                             vmem_limit_bytes=VMEM_LIMIT_BYTES),
        name="moe_experts",
    )(n_blocks, first_block, n_used, xb, w_gate, w_up, w_down)


def _collect_body(d_ref, h_ref, w_ref, yb_ref, o_ref, buf_a, buf_b, sem, *, tm, nsteps, c):
    i = pl.program_id(0)
    slot = lax.rem(i, 2)

    def fetch(step, slot_):
        base = step * tm

        def issue(r, carry):
            for k, buf in enumerate((buf_a, buf_b)):
                row = d_ref[(base + r) * TOP_K + k]
                pltpu.make_async_copy(yb_ref.at[pl.ds(pl.multiple_of(row * c, c), c)],
                                      buf.at[slot_, pl.ds(pl.multiple_of(r * c, c), c)],
                                      sem.at[slot_]).start(priority=k)
            return carry

        lax.fori_loop(0, tm, issue, 0, unroll=DMA_UNROLL)

    @pl.when(i == 0)
    def _():
        fetch(0, 0)

    @pl.when(i + 1 < nsteps)
    def _():
        fetch(i + 1, 1 - slot)

    for buf in (buf_a, buf_b):
        pltpu.make_async_copy(yb_ref.at[pl.ds(0, tm * c)], buf.at[slot], sem.at[slot]).wait()
    w = w_ref[...]
    half = h_ref.shape[1] // 2
    for part, ya, yb in zip((slice(0, half), slice(half, None)),
                            _unpack_bf16_pairs(_tiles_load(buf_a, (slot,), tm, c)),
                            _unpack_bf16_pairs(_tiles_load(buf_b, (slot,), tm, c))):
        o_ref[:, part] = h_ref[:, part] + w[:, 0:1] * ya + w[:, 1:2] * yb


def _collect(h, yb, dest, wts, *, tm=256):
    T, D = h.shape
    c = D // 2 // LANES
    tm = _tile(T, tm)
    nsteps = T // tm
    return pl.pallas_call(
        functools.partial(_collect_body, tm=tm, nsteps=nsteps, c=c),
        out_shape=jax.ShapeDtypeStruct((T, D), F32),
        grid_spec=pltpu.PrefetchScalarGridSpec(
            num_scalar_prefetch=1, grid=(nsteps,),
            in_specs=[pl.BlockSpec((tm, D), lambda i, d: (i, 0)), pl.BlockSpec((tm, LANES), lambda i, d: (i, 0)),
                      pl.BlockSpec(memory_space=pl.ANY)],
            out_specs=pl.BlockSpec((tm, D), lambda i, d: (i, 0)),
            scratch_shapes=[pltpu.VMEM((2, tm * c, LANES), yb.dtype), pltpu.VMEM((2, tm * c, LANES), yb.dtype),
                            pltpu.SemaphoreType.DMA((2,))]),
        compiler_params=pltpu.CompilerParams(dimension_semantics=("arbitrary",), disable_bounds_checks=True,
                                             vmem_limit_bytes=VMEM_LIMIT_BYTES),
        name="moe_collect",
    )(dest, h, wts, yb)


def _moe(h, norm_g, w_group, b_group, w_expert, b_expert, layer, w_gate, w_up, w_down):
    T, D = h.shape
    pad = LANES - N_GROUPS - N_EXPERTS
    w_r = jnp.concatenate([w_group, w_expert, jnp.zeros((D, pad), F32)], axis=1).astype(BF16)
    b_r = jnp.concatenate([b_group, b_expert, jnp.zeros((pad,), F32)]).astype(F32).reshape(1, LANES)
    eid, wts, pos, cnt, xn = _route(h, norm_g, w_r, b_r)
    counts = cnt[0, :N_EXPERTS]
    padded = (counts + MOE_ROWS - 1) // MOE_ROWS * MOE_ROWS
    ends = jnp.cumsum(padded)
    offs = ends - padded
    dest = (offs[eid[:, :TOP_K]] + pos[:, :TOP_K]).reshape(T * TOP_K).astype(jnp.int32)
    P = T * TOP_K + N_EXPERTS * MOE_ROWS
    n_used = (ends[-1:] // MOE_ROWS).astype(jnp.int32)
    xb = _dispatch(xn, dest, ends.astype(jnp.int32), padded.astype(jnp.int32), n_used, P, D // 2 // LANES)
    yb = _experts(xb, (padded // MOE_ROWS).astype(jnp.int32), (offs // MOE_ROWS).astype(jnp.int32), n_used, layer,
                  w_gate, w_up, w_down)
    return _collect(h, yb, dest, wts)


def _pad_cols(w, n):
    return jnp.pad(w, ((0, 0), (0, n - w.shape[1])))


def _layer_even(h, B, S, tabs, norm_g, w_in, sinks, q_lat_norm, kv_lat_norm, w_uq, w_ukv, w_o):
    T, D = h.shape
    n_in, tn = 2560, 512
    sa = HEAD_DIM ** -0.5 * LOG2E
    prog = ([("rot128", sa)] * 8 + [("rot128", 1.0)] * 2 + [("copy", 1.0)] * 8 + [("rot64", 1.0), ("copy", 1.0)])
    y, yf = _mm([(h, 0, D)], [_pad_cols(w_in.astype(BF16), n_in)], gain=norm_g, tn=tn, out_dtype=BF16,
                col_prog=prog, tables=tabs, f32_from=12 * LANES // tn, name="ab_in")
    wq = w_uq.reshape(B_Q_LORA, B_HEADS, B_NOPE + B_ROPE)
    wq_n = wq[:, :, :B_NOPE].reshape(B_Q_LORA, B_HEADS * B_NOPE)
    wq_r = jnp.pad(wq[:, :, B_NOPE:], ((0, 0), (0, 0), (0, LANES - B_ROPE))).reshape(B_Q_LORA, B_HEADS * LANES)
    sb = (B_NOPE + B_ROPE) ** -0.5 * LOG2E
    qq = _mm([(yf, 0, B_Q_LORA)], [jnp.concatenate([wq_n, wq_r], 1).astype(BF16)], gain=q_lat_norm, tn=512,
             out_dtype=BF16, col_prog=[("copy", sb)] * 8 + [("rot64", sb)] * 8, tables=tabs, name="mla_uq")
    kv = _mm([(yf, B_Q_LORA // B_KV_LORA, B_KV_LORA)], [w_ukv.astype(BF16)], gain=kv_lat_norm, out_dtype=BF16,
             tn=512, name="mla_ukv")
    r3 = lambda a: a.reshape(B, S, a.shape[-1])
    y3 = r3(y)
    o_a = _gqa_flash((y3, 0), (y3, 8), (y3, 10), name="swa_attn", B=B, S=S, Hk=A_KV_HEADS,
                     G=A_HEADS // A_KV_HEADS, window=A_WINDOW, sinks=sinks, tq=256, tk=256)
    hp = 2
    head = lambda h: h
    o_b = _flash([(r3(qq), hp * LANES, head), (r3(qq), hp * LANES, lambda h: B_HEADS // hp + h)],
                 [(r3(kv), hp * 2 * LANES, head), (y3, LANES, lambda h: 18)],
                 [[(0, g * LANES), (1, g * LANES)] for g in range(hp)],
                 [[(0, 2 * g * LANES), (1, 0)] for g in range(hp)], [(0, (2 * g + 1) * LANES) for g in range(hp)],
                 name="mla_attn", B=B, S=S, n_steps=B_HEADS // hp, tq=1024, tk=1024, rc=256)
    wo = w_o.astype(BF16)
    na = A_HEADS * HEAD_DIM
    return _mm([(o_a.reshape(T, -1), 0, na), (o_b.reshape(T, -1), 0, B_HEADS * B_V)], [wo[:na], wo[na:]],
               resid=h, tm=512, tn=D, name="ab_out")


def _layer_odd(h, B, S, tabs, cmp_tabs, norm_g, w_in, forget_bias, pe_k, w1_k, w2_k, pe_v, w1_v, w2_v, w_o):
    T, D = h.shape
    hc, kvw = C_HEADS * HEAD_DIM, D_KV_HEADS * HEAD_DIM
    o_cf = 3 * hc
    o_dq = o_cf + C_HEADS
    o_kc = o_dq + D_HEADS * HEAD_DIM
    o_ks = o_kc + 2 * kvw
    o_dg = o_ks + 4 * kvw
    n_in, tn = 5760, 640
    w16 = w_in.astype(BF16)
    w_r = jnp.concatenate([w16[:, :o_cf], w16[:, o_dq:o_kc], w16[:, o_ks:o_dg], w16[:, o_kc:o_ks], w16[:, o_cf:o_dq],
                           w16[:, o_dg:], jnp.zeros((D, n_in - w_in.shape[1]), BF16)], axis=1)
    sc = HEAD_DIM ** -0.5 * LOG2E
    prog = ([("copy", sc)] * 8 + [("copy", 1.0)] * 16 + [("rot128", sc)] * 8
            + [("rot128", 1.0)] * 2 + [("copy", 1.0)] * 2 + [("rot128", 1.0)] * 2 + [("copy", 1.0)] * 7)
    y, yf = _mm([(h, 0, D)], [w_r], gain=norm_g, tn=tn, out_dtype=BF16,
                col_prog=prog, tables=tabs, f32_from=n_in // tn - 1, name="cd_in")
    misc = tn // LANES - 1
    r3 = lambda a: a.reshape(B, S, a.shape[-1])
    y3, yf3 = r3(y), r3(yf)
    fb = jnp.pad(forget_bias.astype(F32), (0, LANES - C_HEADS)).reshape(1, LANES)
    cum = _forget_cum(yf3, misc, fb)
    cumt = jnp.swapaxes(cum[:, :, :C_HEADS], 1, 2).reshape(B, C_HEADS, 1, S)
    hp = 2
    at = lambda first: (lambda h: first + h)
    o_c = _flash([(y3, hp * LANES, at(0))], [(y3, hp * LANES, at(C_HEADS // hp)), (y3, hp * LANES, at(C_HEADS))],
                 [[(0, g * LANES)] for g in range(hp)], [[(0, g * LANES)] for g in range(hp)],
                 [(1, g * LANES) for g in range(hp)], name="fox_attn", B=B, S=S, n_steps=C_HEADS // hp,
                 tq=1024, tk=1024, cum=cum, cumt=cumt)
    G = D_HEADS // D_KV_HEADS
    k_cmp = _compress(yf3, 0, pe_k, w1_k, w2_k, rope_tabs=cmp_tabs)
    v_cmp = _compress(yf3, kvw, pe_v, w1_v, w2_v)
    q_d = (y3, 24 // G)
    o_cmp, sel = _cmp_attn(y3, q_d[1], k_cmp, v_cmp)
    o_slc = _gqa_flash(q_d, (y3, 32), (y3, 34), name="nsa_slc_attn", B=B, S=S, Hk=D_KV_HEADS, G=G, sel=sel,
                       tq=1024, tk=1024, rc=256)
    o_win = _gqa_flash(q_d, (y3, 36), (y3, 38), name="nsa_win_attn", B=B, S=S, Hk=D_KV_HEADS, G=G,
                       window=D_WINDOW, tq=512, tk=512)
    o_d = _nsa_gate(yf, misc, C_HEADS, o_cmp.reshape(T, -1), o_slc.reshape(T, -1), o_win.reshape(T, -1))
    wo = w_o.astype(BF16)
    return _mm([(o_c.reshape(T, -1), 0, hc), (o_d, 0, D_HEADS * HEAD_DIM)], [wo[:hc], wo[hc:]],
               resid=h, tm=512, tn=D, name="cd_out")


def kernel(x, p, positions, ab_w_in, ab_sinks, ab_q_lat_norm, ab_kv_lat_norm, ab_w_uq, ab_w_ukv, ab_w_o,
           cd_w_in, cd_forget_bias, cd_cmp_pe_k, cd_cmp_w1_k, cd_cmp_w2_k, cd_cmp_pe_v, cd_cmp_w1_v,
           cd_cmp_w2_v, cd_w_o, mixer_norm, moe_norm, router_group_w, router_group_b, router_expert_w,
           router_expert_b, expert_w_gate, expert_w_up, expert_w_down, ple_proj, ple_gate_norm, ple_gate_w,
           final_norm):
    B, S, D = x.shape
    T = B * S
    depth = p.shape[0]
    tabs = _rope_tables(positions)
    nc = S // D_CMP_STRIDE
    end = np.minimum(np.arange(nc) * D_CMP_STRIDE + D_CMP_LEN - 1, S - 1)
    cmp_tabs = [t.reshape(B, nc, LANES) for t in _rope_tables(positions[:, end])[:2]]
    h = x.reshape(T, D)
    for i in range(depth):
        j = i // 2
        if i % 2 == 0:
            h = _layer_even(h, B, S, tabs, mixer_norm[i], ab_w_in[j], ab_sinks[j], ab_q_lat_norm[j],
                            ab_kv_lat_norm[j], ab_w_uq[j], ab_w_ukv[j], ab_w_o[j])
        else:
            h = _layer_odd(h, B, S, tabs, cmp_tabs, mixer_norm[i], cd_w_in[j], cd_forget_bias[j],
                           cd_cmp_pe_k[j], cd_cmp_w1_k[j], cd_cmp_w2_k[j], cd_cmp_pe_v[j], cd_cmp_w1_v[j],
                           cd_cmp_w2_v[j], cd_w_o[j])
        h = _moe(h, moe_norm[i], router_group_w[i], router_group_b[i], router_expert_w[i], router_expert_b[i],
                 i, expert_w_gate, expert_w_up, expert_w_down)
        h = _mm([(h, 0, D)], [ple_gate_w[i].astype(BF16)], gain=ple_gate_norm[i], tm=512, tn=D,
                ple=(h, p[i].reshape(T, -1), ple_proj[i].astype(BF16)),
                out_gain=final_norm if i == depth - 1 else None, name="ple")
    return h.reshape(B, S, D)
```

```python
import functools
import math

import numpy as np
import jax
import jax.numpy as jnp
from jax import lax
from jax.experimental import pallas as pl
from jax.experimental.pallas import tpu as pltpu

F32 = jnp.float32
BF16 = jnp.bfloat16

HEAD_DIM = 128
ROPE_THETA = 10000.0
NORM_EPS = 1e-6
NEG_INF = -1e30
TAKEN = -3e38
A_HEADS, A_KV_HEADS, A_WINDOW = 8, 2, 128
B_HEADS, B_Q_LORA, B_KV_LORA, B_NOPE, B_ROPE, B_V = 8, 512, 256, 128, 64, 128
C_HEADS = 8
D_HEADS, D_KV_HEADS = 8, 2
D_CMP_LEN, D_CMP_STRIDE, D_SLC_LEN, D_SLC_TOPN, D_WINDOW = 32, 16, 64, 8, 512
FORCE_BONUS = 1e4
N_GROUPS, EXPERTS_PER_GROUP, TOP_K = 4, 8, 2
N_EXPERTS = N_GROUPS * EXPERTS_PER_GROUP

LANES = 128
VMEM_LIMIT_BYTES = 56 * 1024 * 1024
MOE_ROWS = 256
DMA_UNROLL = 16
FLASH_ROW_CHUNK = 128
LOG2E = math.log2(math.e)


def _params(*sem):
    return pltpu.CompilerParams(dimension_semantics=sem, vmem_limit_bytes=VMEM_LIMIT_BYTES)


def _tile(n, pref):
    t = min(n, pref)
    while n % t:
        t -= 1
    return t


def _pack_bf16_pairs(y):
    n = y.shape[1] // 2
    bits = lambda a: lax.bitcast_convert_type(a.astype(BF16).astype(F32), jnp.uint32)
    return (bits(y[:, :n]) >> 16) | (bits(y[:, n:]) & jnp.uint32(0xFFFF0000))


def _unpack_bf16_pairs(w):
    return (lax.bitcast_convert_type(w << 16, F32), lax.bitcast_convert_type(w & jnp.uint32(0xFFFF0000), F32))


def _rot128(x, c, s):
    return x * c + pltpu.roll(x, 64, 1) * s


def _rot_perms():
    i = np.arange(LANES)
    p128 = (i[:, None] == (i[None, :] + HEAD_DIM // 2) % LANES)
    half = B_ROPE // 2
    p64 = ((i[None, :] < half) & (i[:, None] == i[None, :] + half)) | (
        (i[None, :] >= half) & (i[None, :] < B_ROPE) & (i[:, None] == i[None, :] - half))
    return [jnp.asarray(p.astype(np.float32), BF16) for p in (p128, p64)]


def _tiles_store(ref, lead, words):
    rows, c = words.shape[0], words.shape[1] // LANES
    for j in range(c):
        ref[lead + (pl.ds(j, rows, stride=c), slice(None))] = words[:, j * LANES:(j + 1) * LANES]


def _tiles_load(ref, lead, rows, c):
    return jnp.concatenate([ref[lead + (pl.ds(j, rows, stride=c), slice(None))] for j in range(c)], axis=1)


def _mm_body(*refs, nx, has_gain, mode, resid_is_x, has_out_gain, col_prog, f32_from):
    it = iter(refs)
    x_refs = [next(it) for _ in range(nx)]
    g_ref = next(it) if has_gain else None
    w_refs = [next(it) for _ in range(nx)]
    r_ref = next(it) if mode in ("resid", "ple") and not resid_is_x else None
    p_ref = next(it) if mode == "ple" else None
    wp_ref = next(it) if mode == "ple" else None
    og_ref = next(it) if has_out_gain else None
    tab_refs = [next(it) for _ in range(7)] if col_prog is not None else None
    o_ref = next(it)
    of_ref = next(it) if f32_from is not None else None
    xn_ref = next(it) if has_gain else None

    if has_gain:
        @pl.when(pl.program_id(1) == 0)
        def _():
            x = x_refs[0][...].astype(F32)
            y = x * lax.rsqrt(jnp.mean(x * x, axis=-1, keepdims=True) + NORM_EPS) * g_ref[...]
            xn_ref[...] = y.astype(BF16)
        lhs = [xn_ref[...]]
    else:
        lhs = [x_ref[...].astype(BF16) for x_ref in x_refs]
    acc = None
    for a, w_ref in zip(lhs, w_refs):
        d = jnp.dot(a, w_ref[...], preferred_element_type=F32)
        acc = d if acc is None else acc + d
    if resid_is_x:
        r_ref = x_refs[0]
    if mode == "resid":
        acc = r_ref[...] + acc
    elif mode == "ple":
        pp = jnp.dot(p_ref[...].astype(BF16), wp_ref[...], preferred_element_type=F32)
        acc = r_ref[...] + pp * jax.nn.sigmoid(acc)
    if has_out_gain:
        acc = acc * lax.rsqrt(jnp.mean(acc * acc, axis=-1, keepdims=True) + NORM_EPS) * og_ref[...]
    if col_prog is None:
        o_ref[...] = acc.astype(o_ref.dtype)
        return
    j = pl.program_id(1)
    nb = o_ref.shape[1] // LANES
    tiles = {}
    for jj in range(len(col_prog) // nb):
        tiles.setdefault(tuple(col_prog[jj * nb:(jj + 1) * nb]), []).append(jj)
    for prog, jjs in tiles.items():
        @pl.when(functools.reduce(jnp.logical_or, [j == jj for jj in jjs]))
        def _():
            c128, s128, c64, s64a, s64b, p128, p64 = tab_refs
            for b, (kind, scale) in enumerate(prog):
                x = acc[:, b * LANES:(b + 1) * LANES]
                if kind == "rot128":
                    x = x * c128[...] + jnp.dot(x.astype(BF16), p128[...], preferred_element_type=F32) * s128[...]
                elif kind == "rot64":
                    x = x * c64[...] + jnp.dot(x.astype(BF16), p64[...], preferred_element_type=F32) * (
                        s64a[...] + s64b[...])
                if scale != 1.0:
                    x = x * scale
                o_ref[:, b * LANES:(b + 1) * LANES] = x.astype(o_ref.dtype)
    if f32_from is not None:
        @pl.when(j >= f32_from)
        def _():
            of_ref[...] = acc


def _mm(xs, ws, *, name, gain=None, out_dtype=F32, tm=1024, tn=512, resid=None, ple=None,
        out_gain=None, col_prog=None, tables=None, f32_from=None):
    M = xs[0][0].shape[0]
    N = ws[0].shape[1]
    tm, tn = _tile(M, tm), _tile(N, tn)
    nx = len(xs)
    has_gain = gain is not None
    mode = "ple" if ple is not None else ("resid" if resid is not None else "none")
    r_arr = ple[0] if mode == "ple" else resid
    resid_is_x = r_arr is xs[0][0] and tn == N == xs[0][2] and xs[0][1] == 0
    assert out_gain is None or tn == N
    args, in_specs = [], []
    for arr, cb, K in xs:
        args.append(arr)
        in_specs.append(pl.BlockSpec((tm, K), lambda i, j, cb=cb: (i, cb)))
    if has_gain:
        K0 = xs[0][2]
        args.append(gain.reshape(1, K0).astype(F32))
        in_specs.append(pl.BlockSpec((1, K0), lambda i, j: (0, 0)))
    for (arr, cb, K), w in zip(xs, ws):
        args.append(w)
        in_specs.append(pl.BlockSpec((K, tn), lambda i, j: (0, j)))
    if mode in ("resid", "ple") and not resid_is_x:
        args.append(r_arr)
        in_specs.append(pl.BlockSpec((tm, tn), lambda i, j: (i, j)))
    if mode == "ple":
        _, p, wp = ple
        args += [p, wp]
        in_specs += [pl.BlockSpec((tm, p.shape[1]), lambda i, j: (i, 0)),
                     pl.BlockSpec((p.shape[1], tn), lambda i, j: (0, j))]
    if out_gain is not None:
        args.append(out_gain.reshape(1, N).astype(F32))
        in_specs.append(pl.BlockSpec((1, N), lambda i, j: (0, 0)))
    if col_prog is not None:
        assert len(col_prog) * LANES == N
        args += list(tables) + _rot_perms()
        in_specs += [pl.BlockSpec((tm, LANES), lambda i, j: (i, 0))] * 5
        in_specs += [pl.BlockSpec((LANES, LANES), lambda i, j: (0, 0))] * 2
    out_shape = [jax.ShapeDtypeStruct((M, N), out_dtype)]
    out_specs = [pl.BlockSpec((tm, tn), lambda i, j: (i, j))]
    if f32_from is not None:
        out_shape.append(jax.ShapeDtypeStruct((M, N - f32_from * tn), F32))
        out_specs.append(pl.BlockSpec((tm, tn), lambda i, j: (i, jnp.maximum(j - f32_from, 0))))
    scratch = [pltpu.VMEM((tm, xs[0][2]), BF16)] if has_gain else []
    res = pl.pallas_call(
        functools.partial(_mm_body, nx=nx, has_gain=has_gain, mode=mode, resid_is_x=resid_is_x,
                          has_out_gain=out_gain is not None,
                          col_prog=None if col_prog is None else tuple(col_prog), f32_from=f32_from),
        out_shape=out_shape, grid=(M // tm, N // tn), in_specs=in_specs, out_specs=out_specs,
        scratch_shapes=scratch, compiler_params=_params("arbitrary", "arbitrary"), name=name,
    )(*args)
    return res if len(res) > 1 else res[0]


def _rope_tables(positions):
    def tables(dim):
        inv = 1.0 / (ROPE_THETA ** (jnp.arange(0, dim, 2, dtype=F32) / dim))
        ang = positions.astype(F32)[..., None] * inv
        return jnp.cos(ang), jnp.sin(ang)

    lead = positions.shape
    c, s = tables(HEAD_DIM)
    c128 = jnp.concatenate([c, c], -1)
    s128 = jnp.concatenate([-s, s], -1)
    c, s = tables(B_ROPE)
    z = jnp.zeros_like(c)
    c64 = jnp.concatenate([c, c, z, z], -1)
    s64a = jnp.concatenate([-s, z, z, z], -1)
    s64b = jnp.concatenate([z, s, z, z], -1)
    n = int(np.prod(lead))
    return [t.reshape(n, LANES) for t in (c128, s128, c64, s64a, s64b)]


def _pair_tables(S, tq, tk, window):
    qi_l, ki_l, fl_l, mk_l, masks, ids = [], [], [], [], [], {}
    for qi in range(S // tq):
        q0, q1 = qi * tq, (qi + 1) * tq - 1
        ks = []
        for ki in range(S // tk):
            k0, k1 = ki * tk, (ki + 1) * tk - 1
            if k0 > q1 or (window is not None and q0 - k1 >= window):
                continue
            full = k1 <= q0 and (window is None or q1 - k0 < window)
            mid = 0
            if not full:
                delta = q0 - k0
                if delta not in ids:
                    rel = np.arange(tq)[:, None] + delta - np.arange(tk)[None, :]
                    ok = (rel >= 0) if window is None else ((rel >= 0) & (rel < window))
                    masks.append(np.where(ok, 0.0, NEG_INF).astype(np.float32))
                    ids[delta] = len(masks)
                mid = ids[delta]
            ks.append((ki, mid))
        for n, (ki, mid) in enumerate(ks):
            qi_l.append(qi)
            ki_l.append(ki)
            fl_l.append((1 if n == 0 else 0) | (2 if n == len(ks) - 1 else 0))
            mk_l.append(mid)
    if not masks:
        masks.append(np.zeros((tq, tk), np.float32))
    tabs = [jnp.asarray(np.array(a, np.int32)) for a in (qi_l, ki_l, fl_l, mk_l)]
    return tabs, jnp.asarray(np.stack(masks))


def _flash_body(qi_t, ki_t, fl_t, mk_t, *refs, nq, nkv, q_src, k_src, v_src, tq, rc, causal_diag,
                has_sink, has_bias, has_sel):
    nc = len(q_src)
    it = iter(refs)
    q_refs = [next(it) for _ in range(nq)]
    kv_refs = [next(it) for _ in range(nkv)]
    mask_ref = next(it)
    sink_ref = next(it) if has_sink else None
    cum_ref = next(it) if has_bias else None
    cumt_ref = next(it) if has_bias else None
    sel_ref = next(it) if has_sel else None
    exp_ref = next(it) if has_sel else None
    o_ref = next(it)
    m_sc, acc_sc = next(it), next(it)
    cq_sc = next(it) if has_bias else None

    step_id = pl.program_id(2)
    fl, mk = fl_t[step_id], mk_t[step_id]

    def cat(refs_, src, rows=slice(None)):
        xs = [refs_[pi][0, rows, off:off + LANES] for pi, off in src]
        return xs[0] if len(xs) == 1 else jnp.concatenate(xs, axis=-1)

    @pl.when((fl & 1) != 0)
    def _init():
        m_sc[...] = jnp.full(m_sc.shape, NEG_INF, F32)
        acc_sc[...] = jnp.zeros(acc_sc.shape, F32)
        if has_bias:
            lane = lax.broadcasted_iota(jnp.int32, (tq, LANES), 1)
            for g in range(nc):
                head = pl.program_id(1) * nc + g
                cq_sc[g] = jnp.sum(jnp.where(lane == head, cum_ref[0], 0.0), axis=-1, keepdims=True)

    tk = kv_refs[0].shape[1]
    ones = jnp.ones((tk, LANES), BF16)

    def step(masked):
        ks = [cat(kv_refs, k_src[g]) for g in range(nc)]
        vs = [jnp.concatenate([kv_refs[pi][0, :, off:off + LANES], ones], axis=1) for pi, off in v_src]
        for r in range(tq // rc):
            rows = slice(r * rc, (r + 1) * rc)
            nk = min((r + 1) * rc, tk) if (masked and causal_diag) else tk
            reps = nk // LANES
            add = mask_ref[mk - 1, rows, :nk] if masked else None
            if has_sel:
                hidden = (jnp.dot(sel_ref[0, 0, rows, :], exp_ref[0, :, :nk], preferred_element_type=F32)
                          - 1.0) * (-NEG_INF)
                add = hidden if add is None else add + hidden
            for g in range(nc):
                q = cat(q_refs, q_src[g], rows)
                s = lax.dot_general(q, ks[g][:nk], (((1,), (1,)), ((), ())), preferred_element_type=F32)
                if has_bias:
                    s = s + (cq_sc[g, rows, :] - cumt_ref[0, g, :, :nk])
                if add is not None:
                    s = s + add
                m_prev = m_sc[g, rows, :]
                m_new = jnp.maximum(m_prev, jnp.max(s, axis=-1, keepdims=True))
                p = jnp.exp2(s - (jnp.concatenate([m_new] * reps, axis=1) if reps > 1 else m_new))
                alpha = jnp.exp2(m_prev - m_new)
                acc_sc[g, rows, :] = (jnp.concatenate([alpha, alpha], axis=1) * acc_sc[g, rows, :]
                                      + jnp.dot(p.astype(BF16), vs[g][:nk], preferred_element_type=F32))
                m_sc[g, rows, :] = m_new

    @pl.when(mk != 0)
    def _():
        step(True)

    @pl.when(mk == 0)
    def _():
        step(False)

    @pl.when((fl & 2) != 0)
    def _finish():
        for g in range(nc):
            m, acc, l = m_sc[g], acc_sc[g, :, :LANES], acc_sc[g, :, LANES:]
            if has_sink:
                sk = sink_ref[0, g:g + 1, 0:1] * LOG2E
                m_f = jnp.maximum(m, sk)
                w = jnp.exp2(m - m_f)
                l = l * w + jnp.exp2(sk - m_f)
                acc = acc * w
            o_ref[0, :, g * LANES:(g + 1) * LANES] = (acc / l).astype(o_ref.dtype)


def _flash(q_parts, kv_parts, q_src, k_src, v_src, *, name, B, S, n_steps, tq, tk, window=None, sinks=None,
           cum=None, cumt=None, sel=None, out_dtype=BF16, rc=FLASH_ROW_CHUNK):
    nc = len(q_src)
    tq, tk = _tile(S, tq), _tile(S, tk)
    tabs, masks = _pair_tables(S, tq, tk, window)
    npairs = int(tabs[0].shape[0])
    args, in_specs = [], []
    for arr, width, cf in q_parts:
        args.append(arr)
        in_specs.append(pl.BlockSpec((1, tq, width), lambda b, h, s, qt, kt, ft, mt, cf=cf: (b, qt[s], cf(h))))
    for arr, width, cf in kv_parts:
        args.append(arr)
        in_specs.append(pl.BlockSpec((1, tk, width), lambda b, h, s, qt, kt, ft, mt, cf=cf: (b, kt[s], cf(h))))
    args.append(masks)
    in_specs.append(pl.BlockSpec(masks.shape, lambda b, h, s, qt, kt, ft, mt: (0, 0, 0)))
    if sinks is not None:
        args.append(jnp.broadcast_to(sinks.astype(F32).reshape(n_steps, nc, 1), (n_steps, nc, LANES)))
        in_specs.append(pl.BlockSpec((1, nc, LANES), lambda b, h, s, qt, kt, ft, mt: (h, 0, 0)))
    if cum is not None:
        args += [cum, cumt]
        in_specs += [pl.BlockSpec((1, tq, LANES), lambda b, h, s, qt, kt, ft, mt: (b, qt[s], 0)),
                     pl.BlockSpec((1, nc, 1, tk), lambda b, h, s, qt, kt, ft, mt: (b, h, 0, kt[s]))]
    if sel is not None:
        per = tk // D_SLC_LEN
        e = np.zeros((S // tk, LANES, tk), np.float32)
        for ki in range(S // tk):
            e[ki, ki * per + np.arange(tk) // D_SLC_LEN, np.arange(tk)] = 1.0
        args += [sel, jnp.asarray(e, BF16)]
        in_specs += [pl.BlockSpec((1, 1, tq, LANES), lambda b, h, s, qt, kt, ft, mt: (b, h, qt[s], 0)),
                     pl.BlockSpec((1, LANES, tk), lambda b, h, s, qt, kt, ft, mt: (kt[s], 0, 0))]
    scratch = [pltpu.VMEM((nc, tq, LANES), F32), pltpu.VMEM((nc, tq, 2 * LANES), F32)]
    if cum is not None:
        scratch.append(pltpu.VMEM((nc, tq, 1), F32))
    body = functools.partial(_flash_body, nq=len(q_parts), nkv=len(kv_parts), q_src=q_src, k_src=k_src,
                             v_src=v_src, tq=tq, rc=_tile(tq, rc), causal_diag=window is None and tq == tk,
                             has_sink=sinks is not None,
                             has_bias=cum is not None, has_sel=sel is not None)
    return pl.pallas_call(
        body, out_shape=jax.ShapeDtypeStruct((B, S, n_steps * nc * LANES), out_dtype),
        grid_spec=pltpu.PrefetchScalarGridSpec(
            num_scalar_prefetch=4, grid=(B, n_steps, npairs), in_specs=in_specs,
            out_specs=pl.BlockSpec((1, tq, nc * LANES), lambda b, h, s, qt, kt, ft, mt: (b, qt[s], h)),
            scratch_shapes=scratch),
        compiler_params=_params("arbitrary", "arbitrary", "arbitrary"), name=name,
    )(*tabs, *args)


def _gqa_flash(q, k, v, *, name, B, S, Hk, G, **kw):
    at = lambda first: (lambda h: first + h)
    return _flash([(q[0], G * LANES, at(q[1]))], [(k[0], LANES, at(k[1])), (v[0], LANES, at(v[1]))],
                  [[(0, g * LANES)] for g in range(G)], [[(0, 0)]] * G, [(1, 0)] * G,
                  name=name, B=B, S=S, n_steps=Hk, **kw)


def _cum_body(y_ref, b_ref, tri_ref, o_ref, carry):
    @pl.when(pl.program_id(1) == 0)
    def _():
        carry[...] = jnp.zeros(carry.shape, F32)
    x = y_ref[0] + b_ref[...]
    logf = jnp.minimum(x, 0.0) - jnp.log1p(jnp.exp(-jnp.abs(x)))
    cum = jnp.dot(tri_ref[...], logf, preferred_element_type=F32, precision=lax.Precision.HIGHEST) + carry[...]
    o_ref[0] = cum * LOG2E
    carry[...] = cum[-1:, :]


def _forget_cum(y3, col_block, bias_row, *, ts=512):
    B, S, _ = y3.shape
    ts = _tile(S, ts)
    tri = jnp.asarray(np.tril(np.ones((ts, ts), np.float32)))
    return pl.pallas_call(
        _cum_body, out_shape=jax.ShapeDtypeStruct((B, S, LANES), F32), grid=(B, S // ts),
        in_specs=[pl.BlockSpec((1, ts, LANES), lambda b, s: (b, s, col_block)),
                  pl.BlockSpec((1, LANES), lambda b, s: (0, 0)),
                  pl.BlockSpec((ts, ts), lambda b, s: (0, 0))],
        out_specs=pl.BlockSpec((1, ts, LANES), lambda b, s: (b, s, 0)),
        scratch_shapes=[pltpu.VMEM((1, LANES), F32)],
        compiler_params=_params("arbitrary", "arbitrary"), name="forget_cum",
    )(y3, bias_row, tri)


def _compress_body(*refs, rope, nc, width, col0):
    if rope:
        z_ref, pe_ref, w1_ref, w2_ref, c_ref, s_ref, o_ref = refs
    else:
        z_ref, pe_ref, w1_ref, w2_ref, o_ref = refs
    half = D_CMP_LEN // 2
    for hk in range(D_KV_HEADS):
        u = jnp.zeros((nc, w1_ref.shape[1]), F32)
        v = jnp.zeros((nc, w1_ref.shape[1]), F32)
        for l in range(half):
            first = l * width + col0 + hk * HEAD_DIM
            z = z_ref[0, :, first:first + HEAD_DIM]
            zu = (z + pe_ref[l:l + 1, :]).astype(BF16)
            zv = (z + pe_ref[half + l:half + l + 1, :]).astype(BF16)
            u = u + jnp.dot(zu, w1_ref[l * HEAD_DIM:(l + 1) * HEAD_DIM, :], preferred_element_type=F32)
            v = v + jnp.dot(zv, w1_ref[(half + l) * HEAD_DIM:(half + l + 1) * HEAD_DIM, :],
                            preferred_element_type=F32)
        pre = u + pltpu.roll(v, nc - 1, 0)
        hid = jax.nn.gelu(pre, approximate=True)
        out = jnp.dot(hid.astype(BF16), w2_ref[...], preferred_element_type=F32)
        if rope:
            out = _rot128(out, c_ref[0], s_ref[0])
        o_ref[0, hk] = out.astype(o_ref.dtype)


def _compress(z, col0, pe, w1, w2, rope_tabs=None):
    B, S, W = z.shape
    nc = S // D_CMP_STRIDE
    zc = z.reshape(B, nc, D_CMP_STRIDE * W)
    args = [zc, pe.astype(F32), w1.astype(BF16), w2.astype(BF16)]
    in_specs = [pl.BlockSpec((1, nc, D_CMP_STRIDE * W), lambda b: (b, 0, 0)),
                pl.BlockSpec(pe.shape, lambda b: (0, 0)),
                pl.BlockSpec(w1.shape, lambda b: (0, 0)),
                pl.BlockSpec(w2.shape, lambda b: (0, 0))]
    if rope_tabs is not None:
        args += list(rope_tabs)
        in_specs += [pl.BlockSpec((1, nc, LANES), lambda b: (b, 0, 0))] * 2
    return pl.pallas_call(
        functools.partial(_compress_body, rope=rope_tabs is not None, nc=nc, width=W, col0=col0),
        out_shape=jax.ShapeDtypeStruct((B, D_KV_HEADS, nc, HEAD_DIM), BF16), grid=(B,),
        in_specs=in_specs, out_specs=pl.BlockSpec((1, D_KV_HEADS, nc, HEAD_DIM), lambda b: (b, 0, 0, 0)),
        compiler_params=_params("arbitrary"), name="nsa_compress",
    )(*args)


def _cmp_attn_body(q_ref, k_ref, v_ref, ov_ref, o_ref, sel_ref, *, G, tq, nc, n_cmp, n_slc, topn):
    qi = pl.program_id(2)
    t = qi * tq + lax.broadcasted_iota(jnp.int32, (tq, nc), 0)
    c = lax.broadcasted_iota(jnp.int32, (tq, nc), 1)
    valid = (c * D_CMP_STRIDE + (D_CMP_LEN - 1) <= t) & (c < n_cmp)
    k = k_ref[0, 0]
    v = v_ref[0, 0]
    psum = jnp.zeros((tq, nc), F32)
    for g in range(G):
        q = q_ref[0, :, g * LANES:(g + 1) * LANES]
        s = lax.dot_general(q, k, (((1,), (1,)), ((), ())), preferred_element_type=F32)
        s = jnp.where(valid, s, NEG_INF)
        e = jnp.where(valid, jnp.exp2(s - jnp.max(s, axis=-1, keepdims=True)), 0.0)
        p = e / jnp.maximum(jnp.sum(e, axis=-1, keepdims=True), jnp.finfo(F32).tiny)
        o_ref[0, :, g * LANES:(g + 1) * LANES] = jnp.dot(
            p.astype(BF16), v, preferred_element_type=F32).astype(o_ref.dtype)
        psum = psum + p
    imp = lax.dot_general(ov_ref[...], psum, (((1,), (1,)), ((), ())), preferred_element_type=F32,
                          precision=lax.Precision.HIGHEST)
    blk = lax.broadcasted_iota(jnp.int32, (LANES, tq), 0)
    tcol = qi * tq + lax.broadcasted_iota(jnp.int32, (LANES, tq), 1)
    cur = jnp.right_shift(tcol, int(math.log2(D_SLC_LEN)))
    forced = (blk == 0) | (blk == cur) | (blk == cur - 1)
    imp = jnp.where(blk * D_SLC_LEN > tcol, NEG_INF, imp + jnp.where(forced, FORCE_BONUS, 0.0))
    imp = jnp.where(blk >= n_slc, TAKEN, imp)
    chosen = jnp.zeros((LANES, tq), F32)
    blk_f = blk.astype(F32)
    for _ in range(topn):
        mx = jnp.max(imp, axis=0, keepdims=True)
        idx = jnp.min(jnp.where(imp == mx, blk_f, float(LANES)), axis=0, keepdims=True)
        hit = blk_f == idx
        chosen = jnp.where(hit, 1.0, chosen)
        imp = jnp.where(hit, TAKEN, imp)
    sel_ref[0, 0] = chosen.T.astype(sel_ref.dtype)


def _cmp_attn(q, q_first, k_cmp, v_cmp, *, tq=1024):
    B, S, _ = q.shape
    Hk, G = D_KV_HEADS, D_HEADS // D_KV_HEADS
    nc = S // D_CMP_STRIDE
    n_cmp = (S - D_CMP_LEN) // D_CMP_STRIDE + 1
    n_slc = S // D_SLC_LEN
    tq = _tile(S, tq)
    c0 = np.arange(nc) * D_CMP_STRIDE
    s0 = np.arange(LANES) * D_SLC_LEN
    ov = ((c0[:, None] < (s0 + D_SLC_LEN)[None, :]) & ((c0 + D_CMP_LEN)[:, None] > s0[None, :])
          & (np.arange(nc) < n_cmp)[:, None] & (np.arange(LANES) < n_slc)[None, :]).astype(np.float32)
    body = functools.partial(_cmp_attn_body, G=G, tq=tq, nc=nc, n_cmp=n_cmp, n_slc=n_slc,
                             topn=min(D_SLC_TOPN, n_slc))
    return pl.pallas_call(
        body,
        out_shape=[jax.ShapeDtypeStruct((B, S, Hk * G * LANES), BF16),
                   jax.ShapeDtypeStruct((B, Hk, S, LANES), BF16)],
        grid=(B, Hk, S // tq),
        in_specs=[pl.BlockSpec((1, tq, G * LANES), lambda b, h, i: (b, i, q_first + h)),
                  pl.BlockSpec((1, 1, nc, LANES), lambda b, h, i: (b, h, 0, 0)),
                  pl.BlockSpec((1, 1, nc, LANES), lambda b, h, i: (b, h, 0, 0)),
                  pl.BlockSpec((LANES, nc), lambda b, h, i: (0, 0))],
        out_specs=[pl.BlockSpec((1, tq, G * LANES), lambda b, h, i: (b, i, h)),
                   pl.BlockSpec((1, 1, tq, LANES), lambda b, h, i: (b, h, i, 0))],
        compiler_params=_params("arbitrary", "arbitrary", "arbitrary"), name="nsa_cmp_attn",
    )(q, k_cmp, v_cmp, jnp.asarray(ov.T))


def _gate_body(y_ref, a_ref, b_ref, c_ref, o_ref, *, lane0):
    g = jax.nn.sigmoid(y_ref[...])
    for h in range(D_HEADS):
        cols = slice(h * LANES, (h + 1) * LANES)
        ga = g[:, lane0 + h:lane0 + h + 1]
        gb = g[:, lane0 + D_HEADS + h:lane0 + D_HEADS + h + 1]
        gc = g[:, lane0 + 2 * D_HEADS + h:lane0 + 2 * D_HEADS + h + 1]
        o_ref[:, cols] = (ga * a_ref[:, cols] + gb * b_ref[:, cols] + gc * c_ref[:, cols]).astype(o_ref.dtype)


def _nsa_gate(y, col_block, lane0, o_cmp, o_slc, o_win, *, tm=512):
    M = y.shape[0]
    W = o_cmp.shape[1]
    tm = _tile(M, tm)
    row = lambda i: (i, 0)
    return pl.pallas_call(
        functools.partial(_gate_body, lane0=lane0), out_shape=jax.ShapeDtypeStruct((M, W), BF16),
        grid=(M // tm,),
        in_specs=[pl.BlockSpec((tm, LANES), lambda i: (i, col_block))] + [pl.BlockSpec((tm, W), row)] * 3,
        out_specs=pl.BlockSpec((tm, W), row), compiler_params=_params("arbitrary"), name="nsa_gate",
    )(y, o_cmp, o_slc, o_win)


def _route_body(h_ref, g_ref, wr_ref, b_ref, tri_ref, e_ref, w_ref, pos_ref, cnt_ref, xo_ref, carry, *, tm):
    @pl.when(pl.program_id(0) == 0)
    def _():
        carry[...] = jnp.zeros(carry.shape, F32)
    x = h_ref[...]
    y = x * lax.rsqrt(jnp.mean(x * x, axis=-1, keepdims=True) + NORM_EPS) * g_ref[...]
    _tiles_store(xo_ref, (), _pack_bf16_pairs(y))
    lane = lax.broadcasted_iota(jnp.int32, (tm, LANES), 1)
    logits = jnp.dot(y.astype(BF16), wr_ref[...], preferred_element_type=F32) + b_ref[...]
    gl = jnp.where(lane < N_GROUPS, logits, -jnp.inf)
    gmax = jnp.max(gl, axis=-1, keepdims=True)
    g_val = 1.0 / jnp.sum(jnp.exp(gl - gmax), axis=-1, keepdims=True)
    g_idx = jnp.min(jnp.where(gl == gmax, lane, LANES), axis=-1, keepdims=True)
    lo = N_GROUPS + EXPERTS_PER_GROUP * g_idx
    el = jnp.where((lane >= lo) & (lane < lo + EXPERTS_PER_GROUP), logits, -jnp.inf)
    e1 = jnp.max(el, axis=-1, keepdims=True)
    i1 = jnp.min(jnp.where(el == e1, lane, LANES), axis=-1, keepdims=True)
    el2 = jnp.where(lane == i1, -jnp.inf, el)
    e2 = jnp.max(el2, axis=-1, keepdims=True)
    i2 = jnp.min(jnp.where(el2 == e2, lane, LANES), axis=-1, keepdims=True)
    r = jnp.exp(e2 - e1)
    w1 = g_val / (1.0 + r)
    w2 = w1 * r
    x1, x2 = i1 - N_GROUPS, i2 - N_GROUPS
    e_ref[...] = jnp.where(lane == 0, x1, jnp.where(lane == 1, x2, 0))
    w_ref[...] = jnp.where(lane == 0, w1, jnp.where(lane == 1, w2, 0.0))
    hot1 = lane == x1
    hot2 = lane == x2
    both = jnp.where(hot1 | hot2, 1.0, 0.0)
    before = jnp.dot(tri_ref[...], both.astype(BF16), preferred_element_type=F32) + carry[...]
    p1 = jnp.sum(jnp.where(hot1, before, 0.0), axis=-1, keepdims=True)
    p2 = jnp.sum(jnp.where(hot2, before, 0.0), axis=-1, keepdims=True)
    pos_ref[...] = jnp.where(lane == 0, p1, jnp.where(lane == 1, p2, 0.0)).astype(jnp.int32)
    carry[...] = carry[...] + jnp.sum(both, axis=0, keepdims=True)
    cnt_ref[...] = carry[...].astype(jnp.int32)


def _route(h, gain, w_router, bias_row, *, tm=512):
    T, D = h.shape
    c = D // 2 // LANES
    tm = _tile(T, tm)
    tri = jnp.asarray(np.tril(np.ones((tm, tm), np.float32), -1), BF16)
    row = lambda i: (i, 0)
    fixed = lambda i: (0, 0)
    return pl.pallas_call(
        functools.partial(_route_body, tm=tm),
        out_shape=[jax.ShapeDtypeStruct((T, LANES), jnp.int32), jax.ShapeDtypeStruct((T, LANES), F32),
                   jax.ShapeDtypeStruct((T, LANES), jnp.int32), jax.ShapeDtypeStruct((1, LANES), jnp.int32),
                   jax.ShapeDtypeStruct((T * c, LANES), jnp.uint32)],
        grid=(T // tm,),
        in_specs=[pl.BlockSpec((tm, D), row), pl.BlockSpec((1, D), fixed), pl.BlockSpec((D, LANES), fixed),
                  pl.BlockSpec((1, LANES), fixed), pl.BlockSpec((tm, tm), fixed)],
        out_specs=[pl.BlockSpec((tm, LANES), row)] * 3 + [pl.BlockSpec((1, LANES), fixed),
                                                          pl.BlockSpec((tm * c, LANES), row)],
        scratch_shapes=[pltpu.VMEM((1, LANES), F32)],
        compiler_params=_params("arbitrary"), name="moe_route",
    )(h, gain.reshape(1, D).astype(F32), w_router, bias_row, tri)


def _dispatch_body(d_ref, ends_ref, pad_ref, nu_ref, x_ref, out_ref, zero_sc, sem, zsem, *, tm, nblk, c):
    base = pl.program_id(0) * tm

    @pl.when(pl.program_id(0) == 0)
    def _():
        zero_sc[...] = jnp.zeros(zero_sc.shape, zero_sc.dtype)

        def zero_block(first_row):
            return pltpu.make_async_copy(
                zero_sc, out_ref.at[pl.ds(pl.multiple_of(first_row * c, MOE_ROWS * c), MOE_ROWS * c)], zsem)

        for wait in (False, True):
            for e in range(N_EXPERTS):
                for live, first_row in ((pad_ref[e] > 0, ends_ref[e] - MOE_ROWS),
                                        (nu_ref[0] + e < nblk, (nu_ref[0] + e) * MOE_ROWS)):
                    @pl.when(live)
                    def _():
                        zero_block(first_row).wait() if wait else zero_block(first_row).start()

    def issue(r, carry):
        for k in range(TOP_K):
            row = d_ref[(base + r) * TOP_K + k]
            pltpu.make_async_copy(x_ref.at[pl.ds(pl.multiple_of(r * c, c), c)],
                                  out_ref.at[pl.ds(pl.multiple_of(row * c, c), c)], sem).start(priority=k)
        return carry

    lax.fori_loop(0, tm, issue, 0, unroll=DMA_UNROLL)
    for _ in range(TOP_K):
        pltpu.make_async_copy(x_ref, out_ref.at[pl.ds(0, tm * c)], sem).wait()


def _dispatch(xn, dest, ends, padded, n_used, n_rows, c, *, tm=512):
    T = xn.shape[0] // c
    tm = _tile(T, tm)
    return pl.pallas_call(
        functools.partial(_dispatch_body, tm=tm, nblk=n_rows // MOE_ROWS, c=c),
        out_shape=jax.ShapeDtypeStruct((n_rows * c, LANES), xn.dtype),
        grid_spec=pltpu.PrefetchScalarGridSpec(
            num_scalar_prefetch=4, grid=(T // tm,),
            in_specs=[pl.BlockSpec((tm * c, LANES), lambda i, *_: (i, 0))],
            out_specs=pl.BlockSpec(memory_space=pl.ANY),
            scratch_shapes=[pltpu.VMEM((MOE_ROWS * c, LANES), xn.dtype), pltpu.SemaphoreType.DMA(()),
                            pltpu.SemaphoreType.DMA(())]),
        compiler_params=pltpu.CompilerParams(dimension_semantics=("arbitrary",), has_side_effects=True,
                                             vmem_limit_bytes=VMEM_LIMIT_BYTES, disable_bounds_checks=True),
        name="moe_dispatch",
    )(dest, ends, padded, n_used, xn)


def _expert_body(nb_ref, first_ref, nu_ref, xb_ref, wg_ref, wu_ref, wd_ref, yb_ref, wg_sc, wu_sc, wd_sc, xbuf,
                 obuf, sem_in, sem_out, state, *, nblk, c):
    e = pl.program_id(0)
    last = pl.num_programs(0) - 1
    nb, first = nb_ref[e], first_ref[e]

    @pl.when(e == 0)
    def _():
        for i in range(3):
            state[i] = 0

    def rows(first_block, b):
        return pl.ds(pl.multiple_of((first_block + b) * (MOE_ROWS * c), MOE_ROWS * c), MOE_ROWS * c)

    def fetch(first_block, b, slot):
        return pltpu.make_async_copy(xb_ref.at[rows(first_block, b)], xbuf.at[slot], sem_in.at[slot])

    def flush(b, slot):
        return pltpu.make_async_copy(obuf.at[slot], yb_ref.at[rows(first, b)], sem_out.at[slot])

    def drain(slot):
        @pl.when(state[slot] != 0)
        def _():
            pltpu.make_async_copy(obuf.at[slot], yb_ref.at[pl.ds(0, MOE_ROWS * c)], sem_out.at[slot]).wait()
            state[slot] = 0

    @pl.when(nb > 0)
    def _():
        @pl.when(state[2] == 0)
        def _():
            fetch(first, 0, 0).start()

        state[2] = 0
        wg_sc[...] = wg_ref[0, 0].astype(BF16)
        wu_sc[...] = wu_ref[0, 0].astype(BF16)
        wd_sc[...] = wd_ref[0, 0].astype(BF16)

        def block(b, carry):
            slot = lax.rem(b, 2)

            @pl.when(b + 1 < nb)
            def _():
                fetch(first, b + 1, 1 - slot).start()

            fetch(first, b, slot).wait()
            drain(slot)
            x = jnp.concatenate(_unpack_bf16_pairs(_tiles_load(xbuf, (slot,), MOE_ROWS, c)), axis=1).astype(BF16)
            gate = jnp.dot(x, wg_sc[...], preferred_element_type=F32)
            up = jnp.dot(x, wu_sc[...], preferred_element_type=F32)
            hid = (gate * jax.nn.sigmoid(gate) * up).astype(BF16)
            _tiles_store(obuf, (slot,), _pack_bf16_pairs(jnp.dot(hid, wd_sc[...], preferred_element_type=F32)))
            flush(b, slot).start()
            state[slot] = 1
            return carry

        lax.fori_loop(0, nb, block, 0)
        nxt = jnp.minimum(e + 1, last)

        @pl.when((e < last) & (nb_ref[nxt] > 0))
        def _():
            fetch(first_ref[nxt], 0, 0).start()
            state[2] = 1

    @pl.when(e == last)
    def _():
        drain(0)
        drain(1)
        obuf[0] = jnp.zeros(obuf.shape[1:], obuf.dtype)

        def tail(t):
            return pltpu.make_async_copy(obuf.at[0], yb_ref.at[rows(nu_ref[0], t)], sem_out.at[0])

        for wait in (False, True):
            for t in range(N_EXPERTS):
                @pl.when(nu_ref[0] + t < nblk)
                def _():
                    tail(t).wait() if wait else tail(t).start()


def _experts(xb, n_blocks, first_block, n_used, layer, w_gate, w_up, w_down):
    D, Hd = w_gate.shape[2], w_gate.shape[3]
    c = D // 2 // LANES
    nblk = xb.shape[0] // (MOE_ROWS * c)
    wspec = lambda shape: pl.BlockSpec((1, 1) + shape, lambda e, *_: (layer, e, 0, 0))
    return pl.pallas_call(
        functools.partial(_expert_body, nblk=nblk, c=c), out_shape=jax.ShapeDtypeStruct(xb.shape, xb.dtype),
        grid_spec=pltpu.PrefetchScalarGridSpec(
            num_scalar_prefetch=3, grid=(w_gate.shape[1],),
            in_specs=[pl.BlockSpec(memory_space=pl.ANY), wspec((D, Hd)), wspec((D, Hd)), wspec((Hd, D))],
            out_specs=pl.BlockSpec(memory_space=pl.ANY),
            scratch_shapes=[pltpu.VMEM((D, Hd), BF16), pltpu.VMEM((D, Hd), BF16), pltpu.VMEM((Hd, D), BF16),
                            pltpu.VMEM((2, MOE_ROWS * c, LANES), xb.dtype),
                            pltpu.VMEM((2, MOE_ROWS * c, LANES), xb.dtype),
                            pltpu.SemaphoreType.DMA((2,)), pltpu.SemaphoreType.DMA((2,)),
                            pltpu.SMEM((3,), jnp.int32)]),
        compiler_params=pltpu.CompilerParams(dimension_semantics=("arbitrary",), has_side_effects=True,
                                             vmem_limit_bytes=VMEM_LIMIT_BYTES),
        name="moe_experts",
    )(n_blocks, first_block, n_used, xb, w_gate, w_up, w_down)


def _collect_body(d_ref, h_ref, w_ref, yb_ref, o_ref, buf_a, buf_b, sem, *, tm, nsteps, c):
    i = pl.program_id(0)
    slot = lax.rem(i, 2)

    def fetch(step, slot_):
        base = step * tm

        def issue(r, carry):
            for k, buf in enumerate((buf_a, buf_b)):
                row = d_ref[(base + r) * TOP_K + k]
                pltpu.make_async_copy(yb_ref.at[pl.ds(pl.multiple_of(row * c, c), c)],
                                      buf.at[slot_, pl.ds(pl.multiple_of(r * c, c), c)],
                                      sem.at[slot_]).start(priority=k)
            return carry

        lax.fori_loop(0, tm, issue, 0, unroll=DMA_UNROLL)

    @pl.when(i == 0)
    def _():
        fetch(0, 0)

    @pl.when(i + 1 < nsteps)
    def _():
        fetch(i + 1, 1 - slot)

    for buf in (buf_a, buf_b):
        pltpu.make_async_copy(yb_ref.at[pl.ds(0, tm * c)], buf.at[slot], sem.at[slot]).wait()
    w = w_ref[...]
    half = h_ref.shape[1] // 2
    for part, ya, yb in zip((slice(0, half), slice(half, None)),
                            _unpack_bf16_pairs(_tiles_load(buf_a, (slot,), tm, c)),
                            _unpack_bf16_pairs(_tiles_load(buf_b, (slot,), tm, c))):
        o_ref[:, part] = h_ref[:, part] + w[:, 0:1] * ya + w[:, 1:2] * yb


def _collect(h, yb, dest, wts, *, tm=256):
    T, D = h.shape
    c = D // 2 // LANES
    tm = _tile(T, tm)
    nsteps = T // tm
    return pl.pallas_call(
        functools.partial(_collect_body, tm=tm, nsteps=nsteps, c=c),
        out_shape=jax.ShapeDtypeStruct((T, D), F32),
        grid_spec=pltpu.PrefetchScalarGridSpec(
            num_scalar_prefetch=1, grid=(nsteps,),
            in_specs=[pl.BlockSpec((tm, D), lambda i, d: (i, 0)), pl.BlockSpec((tm, LANES), lambda i, d: (i, 0)),
                      pl.BlockSpec(memory_space=pl.ANY)],
            out_specs=pl.BlockSpec((tm, D), lambda i, d: (i, 0)),
            scratch_shapes=[pltpu.VMEM((2, tm * c, LANES), yb.dtype), pltpu.VMEM((2, tm * c, LANES), yb.dtype),
                            pltpu.SemaphoreType.DMA((2,))]),
        compiler_params=pltpu.CompilerParams(dimension_semantics=("arbitrary",), disable_bounds_checks=True,
                                             vmem_limit_bytes=VMEM_LIMIT_BYTES),
        name="moe_collect",
    )(dest, h, wts, yb)


def _moe(h, norm_g, w_group, b_group, w_expert, b_expert, layer, w_gate, w_up, w_down):
    T, D = h.shape
    pad = LANES - N_GROUPS - N_EXPERTS
    w_r = jnp.concatenate([w_group, w_expert, jnp.zeros((D, pad), F32)], axis=1).astype(BF16)
    b_r = jnp.concatenate([b_group, b_expert, jnp.zeros((pad,), F32)]).astype(F32).reshape(1, LANES)
    eid, wts, pos, cnt, xn = _route(h, norm_g, w_r, b_r)
    counts = cnt[0, :N_EXPERTS]
    padded = (counts + MOE_ROWS - 1) // MOE_ROWS * MOE_ROWS
    ends = jnp.cumsum(padded)
    offs = ends - padded
    dest = (offs[eid[:, :TOP_K]] + pos[:, :TOP_K]).reshape(T * TOP_K).astype(jnp.int32)
    P = T * TOP_K + N_EXPERTS * MOE_ROWS
    n_used = (ends[-1:] // MOE_ROWS).astype(jnp.int32)
    xb = _dispatch(xn, dest, ends.astype(jnp.int32), padded.astype(jnp.int32), n_used, P, D // 2 // LANES)
    yb = _experts(xb, (padded // MOE_ROWS).astype(jnp.int32), (offs // MOE_ROWS).astype(jnp.int32), n_used, layer,
                  w_gate, w_up, w_down)
    return _collect(h, yb, dest, wts)


def _pad_cols(w, n):
    return jnp.pad(w, ((0, 0), (0, n - w.shape[1])))


def _layer_even(h, B, S, tabs, norm_g, w_in, sinks, q_lat_norm, kv_lat_norm, w_uq, w_ukv, w_o):
    T, D = h.shape
    n_in, tn = 2560, 512
    sa = HEAD_DIM ** -0.5 * LOG2E
    prog = ([("rot128", sa)] * 8 + [("rot128", 1.0)] * 2 + [("copy", 1.0)] * 8 + [("rot64", 1.0), ("copy", 1.0)])
    y, yf = _mm([(h, 0, D)], [_pad_cols(w_in.astype(BF16), n_in)], gain=norm_g, tn=tn, out_dtype=BF16,
                col_prog=prog, tables=tabs, f32_from=12 * LANES // tn, name="ab_in")
    wq = w_uq.reshape(B_Q_LORA, B_HEADS, B_NOPE + B_ROPE)
    wq_n = wq[:, :, :B_NOPE].reshape(B_Q_LORA, B_HEADS * B_NOPE)
    wq_r = jnp.pad(wq[:, :, B_NOPE:], ((0, 0), (0, 0), (0, LANES - B_ROPE))).reshape(B_Q_LORA, B_HEADS * LANES)
    sb = (B_NOPE + B_ROPE) ** -0.5 * LOG2E
    qq = _mm([(yf, 0, B_Q_LORA)], [jnp.concatenate([wq_n, wq_r], 1).astype(BF16)], gain=q_lat_norm, tn=512,
             out_dtype=BF16, col_prog=[("copy", sb)] * 8 + [("rot64", sb)] * 8, tables=tabs, name="mla_uq")
    kv = _mm([(yf, B_Q_LORA // B_KV_LORA, B_KV_LORA)], [w_ukv.astype(BF16)], gain=kv_lat_norm, out_dtype=BF16,
             tn=512, name="mla_ukv")
    r3 = lambda a: a.reshape(B, S, a.shape[-1])
    y3 = r3(y)
    o_a = _gqa_flash((y3, 0), (y3, 8), (y3, 10), name="swa_attn", B=B, S=S, Hk=A_KV_HEADS,
                     G=A_HEADS // A_KV_HEADS, window=A_WINDOW, sinks=sinks, tq=256, tk=256)
    hp = 4
    head = lambda h: h
    o_b = _flash([(r3(qq), hp * LANES, head), (r3(qq), hp * LANES, lambda h: B_HEADS // hp + h)],
                 [(r3(kv), hp * 2 * LANES, head), (y3, LANES, lambda h: 18)],
                 [[(0, g * LANES), (1, g * LANES)] for g in range(hp)],
                 [[(0, 2 * g * LANES), (1, 0)] for g in range(hp)], [(0, (2 * g + 1) * LANES) for g in range(hp)],
                 name="mla_attn", B=B, S=S, n_steps=B_HEADS // hp, tq=1024, tk=1024, rc=256)
    wo = w_o.astype(BF16)
    na = A_HEADS * HEAD_DIM
    return _mm([(o_a.reshape(T, -1), 0, na), (o_b.reshape(T, -1), 0, B_HEADS * B_V)], [wo[:na], wo[na:]],
               resid=h, tm=512, tn=D, name="ab_out")


def _layer_odd(h, B, S, tabs, cmp_tabs, norm_g, w_in, forget_bias, pe_k, w1_k, w2_k, pe_v, w1_v, w2_v, w_o):
    T, D = h.shape
    hc, kvw = C_HEADS * HEAD_DIM, D_KV_HEADS * HEAD_DIM
    o_cf = 3 * hc
    o_dq = o_cf + C_HEADS
    o_kc = o_dq + D_HEADS * HEAD_DIM
    o_ks = o_kc + 2 * kvw
    o_dg = o_ks + 4 * kvw
    n_in, tn = 5760, 640
    w16 = w_in.astype(BF16)
    w_r = jnp.concatenate([w16[:, :o_cf], w16[:, o_dq:o_kc], w16[:, o_ks:o_dg], w16[:, o_kc:o_ks], w16[:, o_cf:o_dq],
                           w16[:, o_dg:], jnp.zeros((D, n_in - w_in.shape[1]), BF16)], axis=1)
    sc = HEAD_DIM ** -0.5 * LOG2E
    prog = ([("copy", sc)] * 8 + [("copy", 1.0)] * 16 + [("rot128", sc)] * 8
            + [("rot128", 1.0)] * 2 + [("copy", 1.0)] * 2 + [("rot128", 1.0)] * 2 + [("copy", 1.0)] * 7)
    y, yf = _mm([(h, 0, D)], [w_r], gain=norm_g, tn=tn, out_dtype=BF16,
                col_prog=prog, tables=tabs, f32_from=n_in // tn - 1, name="cd_in")
    misc = tn // LANES - 1
    r3 = lambda a: a.reshape(B, S, a.shape[-1])
    y3, yf3 = r3(y), r3(yf)
    fb = jnp.pad(forget_bias.astype(F32), (0, LANES - C_HEADS)).reshape(1, LANES)
    cum = _forget_cum(yf3, misc, fb)
    cumt = jnp.swapaxes(cum[:, :, :C_HEADS], 1, 2).reshape(B, C_HEADS, 1, S)
    hp = 4
    at = lambda first: (lambda h: first + h)
    o_c = _flash([(y3, hp * LANES, at(0))], [(y3, hp * LANES, at(C_HEADS // hp)), (y3, hp * LANES, at(2 * C_HEADS // hp))],
                 [[(0, g * LANES)] for g in range(hp)], [[(0, g * LANES)] for g in range(hp)],
                 [(1, g * LANES) for g in range(hp)], name="fox_attn", B=B, S=S, n_steps=C_HEADS // hp,
                 tq=1024, tk=1024, cum=cum, cumt=cumt)
    G = D_HEADS // D_KV_HEADS
    k_cmp = _compress(yf3, 0, pe_k, w1_k, w2_k, rope_tabs=cmp_tabs)
    v_cmp = _compress(yf3, kvw, pe_v, w1_v, w2_v)
    q_d = (y3, 24 // G)
    o_cmp, sel = _cmp_attn(y3, q_d[1], k_cmp, v_cmp)
    o_slc = _gqa_flash(q_d, (y3, 32), (y3, 34), name="nsa_slc_attn", B=B, S=S, Hk=D_KV_HEADS, G=G, sel=sel,
                       tq=1024, tk=1024, rc=256)
    o_win = _gqa_flash(q_d, (y3, 36), (y3, 38), name="nsa_win_attn", B=B, S=S, Hk=D_KV_HEADS, G=G,
                       window=D_WINDOW, tq=512, tk=512)
    o_d = _nsa_gate(yf, misc, C_HEADS, o_cmp.reshape(T, -1), o_slc.reshape(T, -1), o_win.reshape(T, -1))
    wo = w_o.astype(BF16)
    return _mm([(o_c.reshape(T, -1), 0, hc), (o_d, 0, D_HEADS * HEAD_DIM)], [wo[:hc], wo[hc:]],
               resid=h, tm=512, tn=D, name="cd_out")


def kernel(x, p, positions, ab_w_in, ab_sinks, ab_q_lat_norm, ab_kv_lat_norm, ab_w_uq, ab_w_ukv, ab_w_o,
           cd_w_in, cd_forget_bias, cd_cmp_pe_k, cd_cmp_w1_k, cd_cmp_w2_k, cd_cmp_pe_v, cd_cmp_w1_v,
           cd_cmp_w2_v, cd_w_o, mixer_norm, moe_norm, router_group_w, router_group_b, router_expert_w,
           router_expert_b, expert_w_gate, expert_w_up, expert_w_down, ple_proj, ple_gate_norm, ple_gate_w,
           final_norm):
    B, S, D = x.shape
    T = B * S
    depth = p.shape[0]
    tabs = _rope_tables(positions)
    nc = S // D_CMP_STRIDE
    end = np.minimum(np.arange(nc) * D_CMP_STRIDE + D_CMP_LEN - 1, S - 1)
    cmp_tabs = [t.reshape(B, nc, LANES) for t in _rope_tables(positions[:, end])[:2]]
    h = x.reshape(T, D)
    for i in range(depth):
        j = i // 2
        if i % 2 == 0:
            h = _layer_even(h, B, S, tabs, mixer_norm[i], ab_w_in[j], ab_sinks[j], ab_q_lat_norm[j],
                            ab_kv_lat_norm[j], ab_w_uq[j], ab_w_ukv[j], ab_w_o[j])
        else:
            h = _layer_odd(h, B, S, tabs, cmp_tabs, mixer_norm[i], cd_w_in[j], cd_forget_bias[j],
                           cd_cmp_pe_k[j], cd_cmp_w1_k[j], cd_cmp_w2_k[j], cd_cmp_pe_v[j], cd_cmp_w1_v[j],
                           cd_cmp_w2_v[j], cd_w_o[j])
        h = _moe(h, moe_norm[i], router_group_w[i], router_group_b[i], router_expert_w[i], router_expert_b[i],
                 i, expert_w_gate, expert_w_up, expert_w_down)
        h = _mm([(h, 0, D)], [ple_gate_w[i].astype(BF16)], gain=ple_gate_norm[i], tm=512, tn=D,
                ple=(h, p[i].reshape(T, -1), ple_proj[i].astype(BF16)),
                out_gain=final_norm if i == depth - 1 else None, name="ple")
    return h.reshape(B, S, D)
```

```python
import functools
import math

import numpy as np
import jax
import jax.numpy as jnp
from jax import lax
from jax.experimental import pallas as pl
from jax.experimental.pallas import tpu as pltpu

F32 = jnp.float32
BF16 = jnp.bfloat16

HEAD_DIM = 128
ROPE_THETA = 10000.0
NORM_EPS = 1e-6
NEG_INF = -1e30
TAKEN = -3e38
A_HEADS, A_KV_HEADS, A_WINDOW = 8, 2, 128
B_HEADS, B_Q_LORA, B_KV_LORA, B_NOPE, B_ROPE, B_V = 8, 512, 256, 128, 64, 128
C_HEADS = 8
D_HEADS, D_KV_HEADS = 8, 2
D_CMP_LEN, D_CMP_STRIDE, D_SLC_LEN, D_SLC_TOPN, D_WINDOW = 32, 16, 64, 8, 512
FORCE_BONUS = 1e4
N_GROUPS, EXPERTS_PER_GROUP, TOP_K = 4, 8, 2
N_EXPERTS = N_GROUPS * EXPERTS_PER_GROUP

LANES = 128
VMEM_LIMIT_BYTES = 56 * 1024 * 1024
MOE_ROWS = 256
DMA_UNROLL = 16
FLASH_ROW_CHUNK = 128
LOG2E = math.log2(math.e)


def _params(*sem):
    return pltpu.CompilerParams(dimension_semantics=sem, vmem_limit_bytes=VMEM_LIMIT_BYTES)


def _tile(n, pref):
    t = min(n, pref)
    while n % t:
        t -= 1
    return t


def _pack_bf16_pairs(y):
    n = y.shape[1] // 2
    bits = lambda a: lax.bitcast_convert_type(a.astype(BF16).astype(F32), jnp.uint32)
    return (bits(y[:, :n]) >> 16) | (bits(y[:, n:]) & jnp.uint32(0xFFFF0000))


def _unpack_bf16_pairs(w):
    return (lax.bitcast_convert_type(w << 16, F32), lax.bitcast_convert_type(w & jnp.uint32(0xFFFF0000), F32))


def _rot128(x, c, s):
    return x * c + pltpu.roll(x, 64, 1) * s


def _rot_perms():
    i = np.arange(LANES)
    p128 = (i[:, None] == (i[None, :] + HEAD_DIM // 2) % LANES)
    half = B_ROPE // 2
    p64 = ((i[None, :] < half) & (i[:, None] == i[None, :] + half)) | (
        (i[None, :] >= half) & (i[None, :] < B_ROPE) & (i[:, None] == i[None, :] - half))
    return [jnp.asarray(p.astype(np.float32), BF16) for p in (p128, p64)]


def _tiles_store(ref, lead, words):
    rows, c = words.shape[0], words.shape[1] // LANES
    for j in range(c):
        ref[lead + (pl.ds(j, rows, stride=c), slice(None))] = words[:, j * LANES:(j + 1) * LANES]


def _tiles_load(ref, lead, rows, c):
    return jnp.concatenate([ref[lead + (pl.ds(j, rows, stride=c), slice(None))] for j in range(c)], axis=1)


def _mm_body(*refs, nx, has_gain, mode, resid_is_x, has_out_gain, col_prog, f32_from):
    it = iter(refs)
    x_refs = [next(it) for _ in range(nx)]
    g_ref = next(it) if has_gain else None
    w_refs = [next(it) for _ in range(nx)]
    r_ref = next(it) if mode in ("resid", "ple") and not resid_is_x else None
    p_ref = next(it) if mode == "ple" else None
    wp_ref = next(it) if mode == "ple" else None
    og_ref = next(it) if has_out_gain else None
    tab_refs = [next(it) for _ in range(7)] if col_prog is not None else None
    o_ref = next(it)
    of_ref = next(it) if f32_from is not None else None
    xn_ref = next(it) if has_gain else None

    if has_gain:
        @pl.when(pl.program_id(1) == 0)
        def _():
            x = x_refs[0][...].astype(F32)
            y = x * lax.rsqrt(jnp.mean(x * x, axis=-1, keepdims=True) + NORM_EPS) * g_ref[...]
            xn_ref[...] = y.astype(BF16)
        lhs = [xn_ref[...]]
    else:
        lhs = [x_ref[...].astype(BF16) for x_ref in x_refs]
    acc = None
    for a, w_ref in zip(lhs, w_refs):
        d = jnp.dot(a, w_ref[...], preferred_element_type=F32)
        acc = d if acc is None else acc + d
    if resid_is_x:
        r_ref = x_refs[0]
    if mode == "resid":
        acc = r_ref[...] + acc
    elif mode == "ple":
        pp = jnp.dot(p_ref[...].astype(BF16), wp_ref[...], preferred_element_type=F32)
        acc = r_ref[...] + pp * jax.nn.sigmoid(acc)
    if has_out_gain:
        acc = acc * lax.rsqrt(jnp.mean(acc * acc, axis=-1, keepdims=True) + NORM_EPS) * og_ref[...]
    if col_prog is None:
        o_ref[...] = acc.astype(o_ref.dtype)
        return
    j = pl.program_id(1)
    nb = o_ref.shape[1] // LANES
    tiles = {}
    for jj in range(len(col_prog) // nb):
        tiles.setdefault(tuple(col_prog[jj * nb:(jj + 1) * nb]), []).append(jj)
    for prog, jjs in tiles.items():
        @pl.when(functools.reduce(jnp.logical_or, [j == jj for jj in jjs]))
        def _():
            c128, s128, c64, s64a, s64b, p128, p64 = tab_refs
            for b, (kind, scale) in enumerate(prog):
                x = acc[:, b * LANES:(b + 1) * LANES]
                if kind == "rot128":
                    x = x * c128[...] + jnp.dot(x.astype(BF16), p128[...], preferred_element_type=F32) * s128[...]
                elif kind == "rot64":
                    x = x * c64[...] + jnp.dot(x.astype(BF16), p64[...], preferred_element_type=F32) * (
                        s64a[...] + s64b[...])
                if scale != 1.0:
                    x = x * scale
                o_ref[:, b * LANES:(b + 1) * LANES] = x.astype(o_ref.dtype)
    if f32_from is not None:
        @pl.when(j >= f32_from)
        def _():
            of_ref[...] = acc


def _mm(xs, ws, *, name, gain=None, out_dtype=F32, tm=1024, tn=512, resid=None, ple=None,
        out_gain=None, col_prog=None, tables=None, f32_from=None):
    M = xs[0][0].shape[0]
    N = ws[0].shape[1]
    tm, tn = _tile(M, tm), _tile(N, tn)
    nx = len(xs)
    has_gain = gain is not None
    mode = "ple" if ple is not None else ("resid" if resid is not None else "none")
    r_arr = ple[0] if mode == "ple" else resid
    resid_is_x = r_arr is xs[0][0] and tn == N == xs[0][2] and xs[0][1] == 0
    assert out_gain is None or tn == N
    args, in_specs = [], []
    for arr, cb, K in xs:
        args.append(arr)
        in_specs.append(pl.BlockSpec((tm, K), lambda i, j, cb=cb: (i, cb)))
    if has_gain:
        K0 = xs[0][2]
        args.append(gain.reshape(1, K0).astype(F32))
        in_specs.append(pl.BlockSpec((1, K0), lambda i, j: (0, 0)))
    for (arr, cb, K), w in zip(xs, ws):
        args.append(w)
        in_specs.append(pl.BlockSpec((K, tn), lambda i, j: (0, j)))
    if mode in ("resid", "ple") and not resid_is_x:
        args.append(r_arr)
        in_specs.append(pl.BlockSpec((tm, tn), lambda i, j: (i, j)))
    if mode == "ple":
        _, p, wp = ple
        args += [p, wp]
        in_specs += [pl.BlockSpec((tm, p.shape[1]), lambda i, j: (i, 0)),
                     pl.BlockSpec((p.shape[1], tn), lambda i, j: (0, j))]
    if out_gain is not None:
        args.append(out_gain.reshape(1, N).astype(F32))
        in_specs.append(pl.BlockSpec((1, N), lambda i, j: (0, 0)))
    if col_prog is not None:
        assert len(col_prog) * LANES == N
        args += list(tables) + _rot_perms()
        in_specs += [pl.BlockSpec((tm, LANES), lambda i, j: (i, 0))] * 5
        in_specs += [pl.BlockSpec((LANES, LANES), lambda i, j: (0, 0))] * 2
    out_shape = [jax.ShapeDtypeStruct((M, N), out_dtype)]
    out_specs = [pl.BlockSpec((tm, tn), lambda i, j: (i, j))]
    if f32_from is not None:
        out_shape.append(jax.ShapeDtypeStruct((M, N - f32_from * tn), F32))
        out_specs.append(pl.BlockSpec((tm, tn), lambda i, j: (i, jnp.maximum(j - f32_from, 0))))
    scratch = [pltpu.VMEM((tm, xs[0][2]), BF16)] if has_gain else []
    res = pl.pallas_call(
        functools.partial(_mm_body, nx=nx, has_gain=has_gain, mode=mode, resid_is_x=resid_is_x,
                          has_out_gain=out_gain is not None,
                          col_prog=None if col_prog is None else tuple(col_prog), f32_from=f32_from),
        out_shape=out_shape, grid=(M // tm, N // tn), in_specs=in_specs, out_specs=out_specs,
        scratch_shapes=scratch, compiler_params=_params("arbitrary", "arbitrary"), name=name,
    )(*args)
    return res if len(res) > 1 else res[0]


def _rope_tables(positions):
    def tables(dim):
        inv = 1.0 / (ROPE_THETA ** (jnp.arange(0, dim, 2, dtype=F32) / dim))
        ang = positions.astype(F32)[..., None] * inv
        return jnp.cos(ang), jnp.sin(ang)

    lead = positions.shape
    c, s = tables(HEAD_DIM)
    c128 = jnp.concatenate([c, c], -1)
    s128 = jnp.concatenate([-s, s], -1)
    c, s = tables(B_ROPE)
    z = jnp.zeros_like(c)
    c64 = jnp.concatenate([c, c, z, z], -1)
    s64a = jnp.concatenate([-s, z, z, z], -1)
    s64b = jnp.concatenate([z, s, z, z], -1)
    n = int(np.prod(lead))
    return [t.reshape(n, LANES) for t in (c128, s128, c64, s64a, s64b)]


def _pair_tables(S, tq, tk, window):
    qi_l, ki_l, fl_l, mk_l, masks, ids = [], [], [], [], [], {}
    for qi in range(S // tq):
        q0, q1 = qi * tq, (qi + 1) * tq - 1
        ks = []
        for ki in range(S // tk):
            k0, k1 = ki * tk, (ki + 1) * tk - 1
            if k0 > q1 or (window is not None and q0 - k1 >= window):
                continue
            full = k1 <= q0 and (window is None or q1 - k0 < window)
            mid = 0
            if not full:
                delta = q0 - k0
                if delta not in ids:
                    rel = np.arange(tq)[:, None] + delta - np.arange(tk)[None, :]
                    ok = (rel >= 0) if window is None else ((rel >= 0) & (rel < window))
                    masks.append(np.where(ok, 0.0, NEG_INF).astype(np.float32))
                    ids[delta] = len(masks)
                mid = ids[delta]
            ks.append((ki, mid))
        for n, (ki, mid) in enumerate(ks):
            qi_l.append(qi)
            ki_l.append(ki)
            fl_l.append((1 if n == 0 else 0) | (2 if n == len(ks) - 1 else 0))
            mk_l.append(mid)
    if not masks:
        masks.append(np.zeros((tq, tk), np.float32))
    tabs = [jnp.asarray(np.array(a, np.int32)) for a in (qi_l, ki_l, fl_l, mk_l)]
    return tabs, jnp.asarray(np.stack(masks))


def _flash_body(qi_t, ki_t, fl_t, mk_t, *refs, nq, nkv, q_src, k_src, v_src, tq, rc, causal_diag,
                has_sink, has_bias, has_sel):
    nc = len(q_src)
    it = iter(refs)
    q_refs = [next(it) for _ in range(nq)]
    kv_refs = [next(it) for _ in range(nkv)]
    mask_ref = next(it)
    sink_ref = next(it) if has_sink else None
    cum_ref = next(it) if has_bias else None
    cumt_ref = next(it) if has_bias else None
    sel_ref = next(it) if has_sel else None
    exp_ref = next(it) if has_sel else None
    o_ref = next(it)
    m_sc, acc_sc = next(it), next(it)
    cq_sc = next(it) if has_bias else None

    step_id = pl.program_id(2)
    fl, mk = fl_t[step_id], mk_t[step_id]

    def cat(refs_, src, rows=slice(None)):
        xs = [refs_[pi][0, rows, off:off + LANES] for pi, off in src]
        return xs[0] if len(xs) == 1 else jnp.concatenate(xs, axis=-1)

    @pl.when((fl & 1) != 0)
    def _init():
        m_sc[...] = jnp.full(m_sc.shape, NEG_INF, F32)
        acc_sc[...] = jnp.zeros(acc_sc.shape, F32)
        if has_bias:
            lane = lax.broadcasted_iota(jnp.int32, (tq, LANES), 1)
            for g in range(nc):
                head = pl.program_id(1) * nc + g
                cq_sc[g] = jnp.sum(jnp.where(lane == head, cum_ref[0], 0.0), axis=-1, keepdims=True)

    tk = kv_refs[0].shape[1]
    ones = jnp.ones((tk, LANES), BF16)

    def step(masked):
        ks = [cat(kv_refs, k_src[g]) for g in range(nc)]
        vs = [jnp.concatenate([kv_refs[pi][0, :, off:off + LANES], ones], axis=1) for pi, off in v_src]
        for r in range(tq // rc):
            rows = slice(r * rc, (r + 1) * rc)
            nk = min((r + 1) * rc, tk) if (masked and causal_diag) else tk
            reps = nk // LANES
            add = mask_ref[mk - 1, rows, :nk] if masked else None
            if has_sel:
                hidden = (jnp.dot(sel_ref[0, 0, rows, :], exp_ref[0, :, :nk], preferred_element_type=F32)
                          - 1.0) * (-NEG_INF)
                add = hidden if add is None else add + hidden
            for g in range(nc):
                q = cat(q_refs, q_src[g], rows)
                s = lax.dot_general(q, ks[g][:nk], (((1,), (1,)), ((), ())), preferred_element_type=F32)
                if has_bias:
                    s = s + (cq_sc[g, rows, :] - cumt_ref[0, g, :, :nk])
                if add is not None:
                    s = s + add
                m_prev = m_sc[g, rows, :]
                m_new = jnp.maximum(m_prev, jnp.max(s, axis=-1, keepdims=True))
                p = jnp.exp2(s - (jnp.concatenate([m_new] * reps, axis=1) if reps > 1 else m_new))
                alpha = jnp.exp2(m_prev - m_new)
                acc_sc[g, rows, :] = (jnp.concatenate([alpha, alpha], axis=1) * acc_sc[g, rows, :]
                                      + jnp.dot(p.astype(BF16), vs[g][:nk], preferred_element_type=F32))
                m_sc[g, rows, :] = m_new

    @pl.when(mk != 0)
    def _():
        step(True)

    @pl.when(mk == 0)
    def _():
        step(False)

    @pl.when((fl & 2) != 0)
    def _finish():
        for g in range(nc):
            m, acc, l = m_sc[g], acc_sc[g, :, :LANES], acc_sc[g, :, LANES:]
            if has_sink:
                sk = sink_ref[0, g:g + 1, 0:1] * LOG2E
                m_f = jnp.maximum(m, sk)
                w = jnp.exp2(m - m_f)
                l = l * w + jnp.exp2(sk - m_f)
                acc = acc * w
            o_ref[0, :, g * LANES:(g + 1) * LANES] = (acc / l).astype(o_ref.dtype)


def _flash(q_parts, kv_parts, q_src, k_src, v_src, *, name, B, S, n_steps, tq, tk, window=None, sinks=None,
           cum=None, cumt=None, sel=None, out_dtype=BF16, rc=FLASH_ROW_CHUNK):
    nc = len(q_src)
    tq, tk = _tile(S, tq), _tile(S, tk)
    tabs, masks = _pair_tables(S, tq, tk, window)
    npairs = int(tabs[0].shape[0])
    args, in_specs = [], []
    for arr, width, cf in q_parts:
        args.append(arr)
        in_specs.append(pl.BlockSpec((1, tq, width), lambda b, h, s, qt, kt, ft, mt, cf=cf: (b, qt[s], cf(h))))
    for arr, width, cf in kv_parts:
        args.append(arr)
        in_specs.append(pl.BlockSpec((1, tk, width), lambda b, h, s, qt, kt, ft, mt, cf=cf: (b, kt[s], cf(h))))
    args.append(masks)
    in_specs.append(pl.BlockSpec(masks.shape, lambda b, h, s, qt, kt, ft, mt: (0, 0, 0)))
    if sinks is not None:
        args.append(jnp.broadcast_to(sinks.astype(F32).reshape(n_steps, nc, 1), (n_steps, nc, LANES)))
        in_specs.append(pl.BlockSpec((1, nc, LANES), lambda b, h, s, qt, kt, ft, mt: (h, 0, 0)))
    if cum is not None:
        args += [cum, cumt]
        in_specs += [pl.BlockSpec((1, tq, LANES), lambda b, h, s, qt, kt, ft, mt: (b, qt[s], 0)),
                     pl.BlockSpec((1, nc, 1, tk), lambda b, h, s, qt, kt, ft, mt: (b, h, 0, kt[s]))]
    if sel is not None:
        per = tk // D_SLC_LEN
        e = np.zeros((S // tk, LANES, tk), np.float32)
        for ki in range(S // tk):
            e[ki, ki * per + np.arange(tk) // D_SLC_LEN, np.arange(tk)] = 1.0
        args += [sel, jnp.asarray(e, BF16)]
        in_specs += [pl.BlockSpec((1, 1, tq, LANES), lambda b, h, s, qt, kt, ft, mt: (b, h, qt[s], 0)),
                     pl.BlockSpec((1, LANES, tk), lambda b, h, s, qt, kt, ft, mt: (kt[s], 0, 0))]
    scratch = [pltpu.VMEM((nc, tq, LANES), F32), pltpu.VMEM((nc, tq, 2 * LANES), F32)]
    if cum is not None:
        scratch.append(pltpu.VMEM((nc, tq, 1), F32))
    body = functools.partial(_flash_body, nq=len(q_parts), nkv=len(kv_parts), q_src=q_src, k_src=k_src,
                             v_src=v_src, tq=tq, rc=_tile(tq, rc), causal_diag=window is None and tq == tk,
                             has_sink=sinks is not None,
                             has_bias=cum is not None, has_sel=sel is not None)
    return pl.pallas_call(
        body, out_shape=jax.ShapeDtypeStruct((B, S, n_steps * nc * LANES), out_dtype),
        grid_spec=pltpu.PrefetchScalarGridSpec(
            num_scalar_prefetch=4, grid=(B, n_steps, npairs), in_specs=in_specs,
            out_specs=pl.BlockSpec((1, tq, nc * LANES), lambda b, h, s, qt, kt, ft, mt: (b, qt[s], h)),
            scratch_shapes=scratch),
        compiler_params=_params("arbitrary", "arbitrary", "arbitrary"), name=name,
    )(*tabs, *args)


def _gqa_flash(q, k, v, *, name, B, S, Hk, G, **kw):
    at = lambda first: (lambda h: first + h)
    return _flash([(q[0], G * LANES, at(q[1]))], [(k[0], LANES, at(k[1])), (v[0], LANES, at(v[1]))],
                  [[(0, g * LANES)] for g in range(G)], [[(0, 0)]] * G, [(1, 0)] * G,
                  name=name, B=B, S=S, n_steps=Hk, **kw)


def _cum_body(y_ref, b_ref, tri_ref, o_ref, carry):
    @pl.when(pl.program_id(1) == 0)
    def _():
        carry[...] = jnp.zeros(carry.shape, F32)
    x = y_ref[0] + b_ref[...]
    logf = jnp.minimum(x, 0.0) - jnp.log1p(jnp.exp(-jnp.abs(x)))
    cum = jnp.dot(tri_ref[...], logf, preferred_element_type=F32, precision=lax.Precision.HIGHEST) + carry[...]
    o_ref[0] = cum * LOG2E
    carry[...] = cum[-1:, :]


def _forget_cum(y3, col_block, bias_row, *, ts=512):
    B, S, _ = y3.shape
    ts = _tile(S, ts)
    tri = jnp.asarray(np.tril(np.ones((ts, ts), np.float32)))
    return pl.pallas_call(
        _cum_body, out_shape=jax.ShapeDtypeStruct((B, S, LANES), F32), grid=(B, S // ts),
        in_specs=[pl.BlockSpec((1, ts, LANES), lambda b, s: (b, s, col_block)),
                  pl.BlockSpec((1, LANES), lambda b, s: (0, 0)),
                  pl.BlockSpec((ts, ts), lambda b, s: (0, 0))],
        out_specs=pl.BlockSpec((1, ts, LANES), lambda b, s: (b, s, 0)),
        scratch_shapes=[pltpu.VMEM((1, LANES), F32)],
        compiler_params=_params("arbitrary", "arbitrary"), name="forget_cum",
    )(y3, bias_row, tri)


def _compress_body(*refs, rope, nc, width, col0):
    if rope:
        z_ref, pe_ref, w1_ref, w2_ref, c_ref, s_ref, o_ref = refs
    else:
        z_ref, pe_ref, w1_ref, w2_ref, o_ref = refs
    half = D_CMP_LEN // 2
    for hk in range(D_KV_HEADS):
        u = jnp.zeros((nc, w1_ref.shape[1]), F32)
        v = jnp.zeros((nc, w1_ref.shape[1]), F32)
        for l in range(half):
            first = l * width + col0 + hk * HEAD_DIM
            z = z_ref[0, :, first:first + HEAD_DIM]
            zu = (z + pe_ref[l:l + 1, :]).astype(BF16)
            zv = (z + pe_ref[half + l:half + l + 1, :]).astype(BF16)
            u = u + jnp.dot(zu, w1_ref[l * HEAD_DIM:(l + 1) * HEAD_DIM, :], preferred_element_type=F32)
            v = v + jnp.dot(zv, w1_ref[(half + l) * HEAD_DIM:(half + l + 1) * HEAD_DIM, :],
                            preferred_element_type=F32)
        pre = u + pltpu.roll(v, nc - 1, 0)
        hid = jax.nn.gelu(pre, approximate=True)
        out = jnp.dot(hid.astype(BF16), w2_ref[...], preferred_element_type=F32)
        if rope:
            out = _rot128(out, c_ref[0], s_ref[0])
        o_ref[0, hk] = out.astype(o_ref.dtype)


def _compress(z, col0, pe, w1, w2, rope_tabs=None):
    B, S, W = z.shape
    nc = S // D_CMP_STRIDE
    zc = z.reshape(B, nc, D_CMP_STRIDE * W)
    args = [zc, pe.astype(F32), w1.astype(BF16), w2.astype(BF16)]
    in_specs = [pl.BlockSpec((1, nc, D_CMP_STRIDE * W), lambda b: (b, 0, 0)),
                pl.BlockSpec(pe.shape, lambda b: (0, 0)),
                pl.BlockSpec(w1.shape, lambda b: (0, 0)),
                pl.BlockSpec(w2.shape, lambda b: (0, 0))]
    if rope_tabs is not None:
        args += list(rope_tabs)
        in_specs += [pl.BlockSpec((1, nc, LANES), lambda b: (b, 0, 0))] * 2
    return pl.pallas_call(
        functools.partial(_compress_body, rope=rope_tabs is not None, nc=nc, width=W, col0=col0),
        out_shape=jax.ShapeDtypeStruct((B, D_KV_HEADS, nc, HEAD_DIM), BF16), grid=(B,),
        in_specs=in_specs, out_specs=pl.BlockSpec((1, D_KV_HEADS, nc, HEAD_DIM), lambda b: (b, 0, 0, 0)),
        compiler_params=_params("arbitrary"), name="nsa_compress",
    )(*args)


def _cmp_attn_body(q_ref, k_ref, v_ref, ov_ref, o_ref, sel_ref, *, G, tq, nc, n_cmp, n_slc, topn):
    qi = pl.program_id(2)
    t = qi * tq + lax.broadcasted_iota(jnp.int32, (tq, nc), 0)
    c = lax.broadcasted_iota(jnp.int32, (tq, nc), 1)
    valid = (c * D_CMP_STRIDE + (D_CMP_LEN - 1) <= t) & (c < n_cmp)
    k = k_ref[0, 0]
    v = v_ref[0, 0]
    psum = jnp.zeros((tq, nc), F32)
    for g in range(G):
        q = q_ref[0, :, g * LANES:(g + 1) * LANES]
        s = lax.dot_general(q, k, (((1,), (1,)), ((), ())), preferred_element_type=F32)
        s = jnp.where(valid, s, NEG_INF)
        e = jnp.where(valid, jnp.exp2(s - jnp.max(s, axis=-1, keepdims=True)), 0.0)
        p = e / jnp.maximum(jnp.sum(e, axis=-1, keepdims=True), jnp.finfo(F32).tiny)
        o_ref[0, :, g * LANES:(g + 1) * LANES] = jnp.dot(
            p.astype(BF16), v, preferred_element_type=F32).astype(o_ref.dtype)
        psum = psum + p
    imp = lax.dot_general(ov_ref[...], psum, (((1,), (1,)), ((), ())), preferred_element_type=F32,
                          precision=lax.Precision.HIGHEST)
    blk = lax.broadcasted_iota(jnp.int32, (LANES, tq), 0)
    tcol = qi * tq + lax.broadcasted_iota(jnp.int32, (LANES, tq), 1)
    cur = jnp.right_shift(tcol, int(math.log2(D_SLC_LEN)))
    forced = (blk == 0) | (blk == cur) | (blk == cur - 1)
    imp = jnp.where(blk * D_SLC_LEN > tcol, NEG_INF, imp + jnp.where(forced, FORCE_BONUS, 0.0))
    imp = jnp.where(blk >= n_slc, TAKEN, imp)
    chosen = jnp.zeros((LANES, tq), F32)
    blk_f = blk.astype(F32)
    for _ in range(topn):
        mx = jnp.max(imp, axis=0, keepdims=True)
        idx = jnp.min(jnp.where(imp == mx, blk_f, float(LANES)), axis=0, keepdims=True)
        hit = blk_f == idx
        chosen = jnp.where(hit, 1.0, chosen)
        imp = jnp.where(hit, TAKEN, imp)
    sel_ref[0, 0] = chosen.T.astype(sel_ref.dtype)


def _cmp_attn(q, q_first, k_cmp, v_cmp, *, tq=1024):
    B, S, _ = q.shape
    Hk, G = D_KV_HEADS, D_HEADS // D_KV_HEADS
    nc = S // D_CMP_STRIDE
    n_cmp = (S - D_CMP_LEN) // D_CMP_STRIDE + 1
    n_slc = S // D_SLC_LEN
    tq = _tile(S, tq)
    c0 = np.arange(nc) * D_CMP_STRIDE
    s0 = np.arange(LANES) * D_SLC_LEN
    ov = ((c0[:, None] < (s0 + D_SLC_LEN)[None, :]) & ((c0 + D_CMP_LEN)[:, None] > s0[None, :])
          & (np.arange(nc) < n_cmp)[:, None] & (np.arange(LANES) < n_slc)[None, :]).astype(np.float32)
    body = functools.partial(_cmp_attn_body, G=G, tq=tq, nc=nc, n_cmp=n_cmp, n_slc=n_slc,
                             topn=min(D_SLC_TOPN, n_slc))
    return pl.pallas_call(
        body,
        out_shape=[jax.ShapeDtypeStruct((B, S, Hk * G * LANES), BF16),
                   jax.ShapeDtypeStruct((B, Hk, S, LANES), BF16)],
        grid=(B, Hk, S // tq),
        in_specs=[pl.BlockSpec((1, tq, G * LANES), lambda b, h, i: (b, i, q_first + h)),
                  pl.BlockSpec((1, 1, nc, LANES), lambda b, h, i: (b, h, 0, 0)),
                  pl.BlockSpec((1, 1, nc, LANES), lambda b, h, i: (b, h, 0, 0)),
                  pl.BlockSpec((LANES, nc), lambda b, h, i: (0, 0))],
        out_specs=[pl.BlockSpec((1, tq, G * LANES), lambda b, h, i: (b, i, h)),
                   pl.BlockSpec((1, 1, tq, LANES), lambda b, h, i: (b, h, i, 0))],
        compiler_params=_params("arbitrary", "arbitrary", "arbitrary"), name="nsa_cmp_attn",
    )(q, k_cmp, v_cmp, jnp.asarray(ov.T))


def _gate_body(y_ref, a_ref, b_ref, c_ref, o_ref, *, lane0):
    g = jax.nn.sigmoid(y_ref[...])
    for h in range(D_HEADS):
        cols = slice(h * LANES, (h + 1) * LANES)
        ga = g[:, lane0 + h:lane0 + h + 1]
        gb = g[:, lane0 + D_HEADS + h:lane0 + D_HEADS + h + 1]
        gc = g[:, lane0 + 2 * D_HEADS + h:lane0 + 2 * D_HEADS + h + 1]
        o_ref[:, cols] = (ga * a_ref[:, cols] + gb * b_ref[:, cols] + gc * c_ref[:, cols]).astype(o_ref.dtype)


def _nsa_gate(y, col_block, lane0, o_cmp, o_slc, o_win, *, tm=512):
    M = y.shape[0]
    W = o_cmp.shape[1]
    tm = _tile(M, tm)
    row = lambda i: (i, 0)
    return pl.pallas_call(
        functools.partial(_gate_body, lane0=lane0), out_shape=jax.ShapeDtypeStruct((M, W), BF16),
        grid=(M // tm,),
        in_specs=[pl.BlockSpec((tm, LANES), lambda i: (i, col_block))] + [pl.BlockSpec((tm, W), row)] * 3,
        out_specs=pl.BlockSpec((tm, W), row), compiler_params=_params("arbitrary"), name="nsa_gate",
    )(y, o_cmp, o_slc, o_win)


def _route_body(h_ref, g_ref, wr_ref, b_ref, tri_ref, e_ref, w_ref, pos_ref, cnt_ref, xo_ref, carry, *, tm):
    @pl.when(pl.program_id(0) == 0)
    def _():
        carry[...] = jnp.zeros(carry.shape, F32)
    x = h_ref[...]
    y = x * lax.rsqrt(jnp.mean(x * x, axis=-1, keepdims=True) + NORM_EPS) * g_ref[...]
    _tiles_store(xo_ref, (), _pack_bf16_pairs(y))
    lane = lax.broadcasted_iota(jnp.int32, (tm, LANES), 1)
    logits = jnp.dot(y.astype(BF16), wr_ref[...], preferred_element_type=F32) + b_ref[...]
    gl = jnp.where(lane < N_GROUPS, logits, -jnp.inf)
    gmax = jnp.max(gl, axis=-1, keepdims=True)
    g_val = 1.0 / jnp.sum(jnp.exp(gl - gmax), axis=-1, keepdims=True)
    g_idx = jnp.min(jnp.where(gl == gmax, lane, LANES), axis=-1, keepdims=True)
    lo = N_GROUPS + EXPERTS_PER_GROUP * g_idx
    el = jnp.where((lane >= lo) & (lane < lo + EXPERTS_PER_GROUP), logits, -jnp.inf)
    e1 = jnp.max(el, axis=-1, keepdims=True)
    i1 = jnp.min(jnp.where(el == e1, lane, LANES), axis=-1, keepdims=True)
    el2 = jnp.where(lane == i1, -jnp.inf, el)
    e2 = jnp.max(el2, axis=-1, keepdims=True)
    i2 = jnp.min(jnp.where(el2 == e2, lane, LANES), axis=-1, keepdims=True)
    r = jnp.exp(e2 - e1)
    w1 = g_val / (1.0 + r)
    w2 = w1 * r
    x1, x2 = i1 - N_GROUPS, i2 - N_GROUPS
    e_ref[...] = jnp.where(lane == 0, x1, jnp.where(lane == 1, x2, 0))
    w_ref[...] = jnp.where(lane == 0, w1, jnp.where(lane == 1, w2, 0.0))
    hot1 = lane == x1
    hot2 = lane == x2
    both = jnp.where(hot1 | hot2, 1.0, 0.0)
    before = jnp.dot(tri_ref[...], both.astype(BF16), preferred_element_type=F32) + carry[...]
    p1 = jnp.sum(jnp.where(hot1, before, 0.0), axis=-1, keepdims=True)
    p2 = jnp.sum(jnp.where(hot2, before, 0.0), axis=-1, keepdims=True)
    pos_ref[...] = jnp.where(lane == 0, p1, jnp.where(lane == 1, p2, 0.0)).astype(jnp.int32)
    carry[...] = carry[...] + jnp.sum(both, axis=0, keepdims=True)
    cnt_ref[...] = carry[...].astype(jnp.int32)


def _route(h, gain, w_router, bias_row, *, tm=512):
    T, D = h.shape
    c = D // 2 // LANES
    tm = _tile(T, tm)
    tri = jnp.asarray(np.tril(np.ones((tm, tm), np.float32), -1), BF16)
    row = lambda i: (i, 0)
    fixed = lambda i: (0, 0)
    return pl.pallas_call(
        functools.partial(_route_body, tm=tm),
        out_shape=[jax.ShapeDtypeStruct((T, LANES), jnp.int32), jax.ShapeDtypeStruct((T, LANES), F32),
                   jax.ShapeDtypeStruct((T, LANES), jnp.int32), jax.ShapeDtypeStruct((1, LANES), jnp.int32),
                   jax.ShapeDtypeStruct((T * c, LANES), jnp.uint32)],
        grid=(T // tm,),
        in_specs=[pl.BlockSpec((tm, D), row), pl.BlockSpec((1, D), fixed), pl.BlockSpec((D, LANES), fixed),
                  pl.BlockSpec((1, LANES), fixed), pl.BlockSpec((tm, tm), fixed)],
        out_specs=[pl.BlockSpec((tm, LANES), row)] * 3 + [pl.BlockSpec((1, LANES), fixed),
                                                          pl.BlockSpec((tm * c, LANES), row)],
        scratch_shapes=[pltpu.VMEM((1, LANES), F32)],
        compiler_params=_params("arbitrary"), name="moe_route",
    )(h, gain.reshape(1, D).astype(F32), w_router, bias_row, tri)


def _dispatch_body(d_ref, ends_ref, pad_ref, nu_ref, x_ref, out_ref, zero_sc, sem, zsem, *, tm, nblk, c):
    base = pl.program_id(0) * tm

    @pl.when(pl.program_id(0) == 0)
    def _():
        zero_sc[...] = jnp.zeros(zero_sc.shape, zero_sc.dtype)

        def zero_block(first_row):
            return pltpu.make_async_copy(
                zero_sc, out_ref.at[pl.ds(pl.multiple_of(first_row * c, MOE_ROWS * c), MOE_ROWS * c)], zsem)

        for wait in (False, True):
            for e in range(N_EXPERTS):
                for live, first_row in ((pad_ref[e] > 0, ends_ref[e] - MOE_ROWS),
                                        (nu_ref[0] + e < nblk, (nu_ref[0] + e) * MOE_ROWS)):
                    @pl.when(live)
                    def _():
                        zero_block(first_row).wait() if wait else zero_block(first_row).start()

    def issue(r, carry):
        for k in range(TOP_K):
            row = d_ref[(base + r) * TOP_K + k]
            pltpu.make_async_copy(x_ref.at[pl.ds(pl.multiple_of(r * c, c), c)],
                                  out_ref.at[pl.ds(pl.multiple_of(row * c, c), c)], sem).start(priority=k)
        return carry

    lax.fori_loop(0, tm, issue, 0, unroll=DMA_UNROLL)
    for _ in range(TOP_K):
        pltpu.make_async_copy(x_ref, out_ref.at[pl.ds(0, tm * c)], sem).wait()


def _dispatch(xn, dest, ends, padded, n_used, n_rows, c, *, tm=512):
    T = xn.shape[0] // c
    tm = _tile(T, tm)
    return pl.pallas_call(
        functools.partial(_dispatch_body, tm=tm, nblk=n_rows // MOE_ROWS, c=c),
        out_shape=jax.ShapeDtypeStruct((n_rows * c, LANES), xn.dtype),
        grid_spec=pltpu.PrefetchScalarGridSpec(
            num_scalar_prefetch=4, grid=(T // tm,),
            in_specs=[pl.BlockSpec((tm * c, LANES), lambda i, *_: (i, 0))],
            out_specs=pl.BlockSpec(memory_space=pl.ANY),
            scratch_shapes=[pltpu.VMEM((MOE_ROWS * c, LANES), xn.dtype), pltpu.SemaphoreType.DMA(()),
                            pltpu.SemaphoreType.DMA(())]),
        compiler_params=pltpu.CompilerParams(dimension_semantics=("arbitrary",), has_side_effects=True,
                                             vmem_limit_bytes=VMEM_LIMIT_BYTES, disable_bounds_checks=True),
        name="moe_dispatch",
    )(dest, ends, padded, n_used, xn)


def _expert_body(nb_ref, first_ref, nu_ref, xb_ref, wg_ref, wu_ref, wd_ref, yb_ref, wg_sc, wu_sc, wd_sc, xbuf,
                 obuf, sem_in, sem_out, state, *, nblk, c):
    e = pl.program_id(0)
    last = pl.num_programs(0) - 1
    nb, first = nb_ref[e], first_ref[e]

    @pl.when(e == 0)
    def _():
        for i in range(3):
            state[i] = 0

    def rows(first_block, b):
        return pl.ds(pl.multiple_of((first_block + b) * (MOE_ROWS * c), MOE_ROWS * c), MOE_ROWS * c)

    def fetch(first_block, b, slot):
        return pltpu.make_async_copy(xb_ref.at[rows(first_block, b)], xbuf.at[slot], sem_in.at[slot])

    def flush(b, slot):
        return pltpu.make_async_copy(obuf.at[slot], yb_ref.at[rows(first, b)], sem_out.at[slot])

    def drain(slot):
        @pl.when(state[slot] != 0)
        def _():
            pltpu.make_async_copy(obuf.at[slot], yb_ref.at[pl.ds(0, MOE_ROWS * c)], sem_out.at[slot]).wait()
            state[slot] = 0

    @pl.when(nb > 0)
    def _():
        @pl.when(state[2] == 0)
        def _():
            fetch(first, 0, 0).start()

        state[2] = 0
        wg_sc[...] = wg_ref[0, 0].astype(BF16)
        wu_sc[...] = wu_ref[0, 0].astype(BF16)
        wd_sc[...] = wd_ref[0, 0].astype(BF16)

        def block(b, carry):
            slot = lax.rem(b, 2)

            @pl.when(b + 1 < nb)
            def _():
                fetch(first, b + 1, 1 - slot).start()

            fetch(first, b, slot).wait()
            drain(slot)
            x = jnp.concatenate(_unpack_bf16_pairs(_tiles_load(xbuf, (slot,), MOE_ROWS, c)), axis=1).astype(BF16)
            gate = jnp.dot(x, wg_sc[...], preferred_element_type=F32)
            up = jnp.dot(x, wu_sc[...], preferred_element_type=F32)
            hid = (gate * jax.nn.sigmoid(gate) * up).astype(BF16)
            _tiles_store(obuf, (slot,), _pack_bf16_pairs(jnp.dot(hid, wd_sc[...], preferred_element_type=F32)))
            flush(b, slot).start()
            state[slot] = 1
            return carry

        lax.fori_loop(0, nb, block, 0)
        nxt = jnp.minimum(e + 1, last)

        @pl.when((e < last) & (nb_ref[nxt] > 0))
        def _():
            fetch(first_ref[nxt], 0, 0).start()
            state[2] = 1

    @pl.when(e == last)
    def _():
        drain(0)
        drain(1)
        obuf[0] = jnp.zeros(obuf.shape[1:], obuf.dtype)

        def tail(t):
            return pltpu.make_async_copy(obuf.at[0], yb_ref.at[rows(nu_ref[0], t)], sem_out.at[0])

        for wait in (False, True):
            for t in range(N_EXPERTS):
                @pl.when(nu_ref[0] + t < nblk)
                def _():
                    tail(t).wait() if wait else tail(t).start()


def _experts(xb, n_blocks, first_block, n_used, layer, w_gate, w_up, w_down):
    D, Hd = w_gate.shape[2], w_gate.shape[3]
    c = D // 2 // LANES
    nblk = xb.shape[0] // (MOE_ROWS * c)
    wspec = lambda shape: pl.BlockSpec((1, 1) + shape, lambda e, *_: (layer, e, 0, 0))
    return pl.pallas_call(
        functools.partial(_expert_body, nblk=nblk, c=c), out_shape=jax.ShapeDtypeStruct(xb.shape, xb.dtype),
        grid_spec=pltpu.PrefetchScalarGridSpec(
            num_scalar_prefetch=3, grid=(w_gate.shape[1],),
            in_specs=[pl.BlockSpec(memory_space=pl.ANY), wspec((D, Hd)), wspec((D, Hd)), wspec((Hd, D))],
            out_specs=pl.BlockSpec(memory_space=pl.ANY),
            scratch_shapes=[pltpu.VMEM((D, Hd), BF16), pltpu.VMEM((D, Hd), BF16), pltpu.VMEM((Hd, D), BF16),
                            pltpu.VMEM((2, MOE_ROWS * c, LANES), xb.dtype),
                            pltpu.VMEM((2, MOE_ROWS * c, LANES), xb.dtype),
                            pltpu.SemaphoreType.DMA((2,)), pltpu.SemaphoreType.DMA((2,)),
                            pltpu.SMEM((3,), jnp.int32)]),
        compiler_params=pltpu.CompilerParams(dimension_semantics=("arbitrary",), has_side_effects=True,
                                             vmem_limit_bytes=VMEM_LIMIT_BYTES),
        name="moe_experts",
    )(n_blocks, first_block, n_used, xb, w_gate, w_up, w_down)


def _collect_body(d_ref, h_ref, w_ref, yb_ref, o_ref, buf_a, buf_b, sem, *, tm, nsteps, c):
    i = pl.program_id(0)
    slot = lax.rem(i, 2)

    def fetch(step, slot_):
        base = step * tm

        def issue(r, carry):
            for k, buf in enumerate((buf_a, buf_b)):
                row = d_ref[(base + r) * TOP_K + k]
                pltpu.make_async_copy(yb_ref.at[pl.ds(pl.multiple_of(row * c, c), c)],
                                      buf.at[slot_, pl.ds(pl.multiple_of(r * c, c), c)],
                                      sem.at[slot_]).start(priority=k)
            return carry

        lax.fori_loop(0, tm, issue, 0, unroll=DMA_UNROLL)

    @pl.when(i == 0)
    def _():
        fetch(0, 0)

    @pl.when(i + 1 < nsteps)
    def _():
        fetch(i + 1, 1 - slot)

    for buf in (buf_a, buf_b):
        pltpu.make_async_copy(yb_ref.at[pl.ds(0, tm * c)], buf.at[slot], sem.at[slot]).wait()
    w = w_ref[...]
    half = h_ref.shape[1] // 2
    for part, ya, yb in zip((slice(0, half), slice(half, None)),
                            _unpack_bf16_pairs(_tiles_load(buf_a, (slot,), tm, c)),
                            _unpack_bf16_pairs(_tiles_load(buf_b, (slot,), tm, c))):
        o_ref[:, part] = h_ref[:, part] + w[:, 0:1] * ya + w[:, 1:2] * yb


def _collect(h, yb, dest, wts, *, tm=256):
    T, D = h.shape
    c = D // 2 // LANES
    tm = _tile(T, tm)
    nsteps = T // tm
    return pl.pallas_call(
        functools.partial(_collect_body, tm=tm, nsteps=nsteps, c=c),
        out_shape=jax.ShapeDtypeStruct((T, D), F32),
        grid_spec=pltpu.PrefetchScalarGridSpec(
            num_scalar_prefetch=1, grid=(nsteps,),
            in_specs=[pl.BlockSpec((tm, D), lambda i, d: (i, 0)), pl.BlockSpec((tm, LANES), lambda i, d: (i, 0)),
                      pl.BlockSpec(memory_space=pl.ANY)],
            out_specs=pl.BlockSpec((tm, D), lambda i, d: (i, 0)),
            scratch_shapes=[pltpu.VMEM((2, tm * c, LANES), yb.dtype), pltpu.VMEM((2, tm * c, LANES), yb.dtype),
                            pltpu.SemaphoreType.DMA((2,))]),
        compiler_params=pltpu.CompilerParams(dimension_semantics=("arbitrary",), disable_bounds_checks=True,
                                             vmem_limit_bytes=VMEM_LIMIT_BYTES),
        name="moe_collect",
    )(dest, h, wts, yb)


def _moe(h, norm_g, w_group, b_group, w_expert, b_expert, layer, w_gate, w_up, w_down):
    T, D = h.shape
    pad = LANES - N_GROUPS - N_EXPERTS
    w_r = jnp.concatenate([w_group, w_expert, jnp.zeros((D, pad), F32)], axis=1).astype(BF16)
    b_r = jnp.concatenate([b_group, b_expert, jnp.zeros((pad,), F32)]).astype(F32).reshape(1, LANES)
    eid, wts, pos, cnt, xn = _route(h, norm_g, w_r, b_r)
    counts = cnt[0, :N_EXPERTS]
    padded = (counts + MOE_ROWS - 1) // MOE_ROWS * MOE_ROWS
    ends = jnp.cumsum(padded)
    offs = ends - padded
    dest = (offs[eid[:, :TOP_K]] + pos[:, :TOP_K]).reshape(T * TOP_K).astype(jnp.int32)
    P = T * TOP_K + N_EXPERTS * MOE_ROWS
    n_used = (ends[-1:] // MOE_ROWS).astype(jnp.int32)
    xb = _dispatch(xn, dest, ends.astype(jnp.int32), padded.astype(jnp.int32), n_used, P, D // 2 // LANES)
    yb = _experts(xb, (padded // MOE_ROWS).astype(jnp.int32), (offs // MOE_ROWS).astype(jnp.int32), n_used, layer,
                  w_gate, w_up, w_down)
    return _collect(h, yb, dest, wts)


def _pad_cols(w, n):
    return jnp.pad(w, ((0, 0), (0, n - w.shape[1])))


def _layer_even(h, B, S, tabs, norm_g, w_in, sinks, q_lat_norm, kv_lat_norm, w_uq, w_ukv, w_o):
    T, D = h.shape
    n_in, tn = 2560, 512
    sa = HEAD_DIM ** -0.5 * LOG2E
    prog = ([("rot128", sa)] * 8 + [("rot128", 1.0)] * 2 + [("copy", 1.0)] * 8 + [("rot64", 1.0), ("copy", 1.0)])
    y, yf = _mm([(h, 0, D)], [_pad_cols(w_in.astype(BF16), n_in)], gain=norm_g, tn=tn, out_dtype=BF16,
                col_prog=prog, tables=tabs, f32_from=12 * LANES // tn, name="ab_in")
    wq = w_uq.reshape(B_Q_LORA, B_HEADS, B_NOPE + B_ROPE)
    wq_n = wq[:, :, :B_NOPE].reshape(B_Q_LORA, B_HEADS * B_NOPE)
    wq_r = jnp.pad(wq[:, :, B_NOPE:], ((0, 0), (0, 0), (0, LANES - B_ROPE))).reshape(B_Q_LORA, B_HEADS * LANES)
    sb = (B_NOPE + B_ROPE) ** -0.5 * LOG2E
    qq = _mm([(yf, 0, B_Q_LORA)], [jnp.concatenate([wq_n, wq_r], 1).astype(BF16)], gain=q_lat_norm, tn=512,
             out_dtype=BF16, col_prog=[("copy", sb)] * 8 + [("rot64", sb)] * 8, tables=tabs, name="mla_uq")
    kv = _mm([(yf, B_Q_LORA // B_KV_LORA, B_KV_LORA)], [w_ukv.astype(BF16)], gain=kv_lat_norm, out_dtype=BF16,
             tn=512, name="mla_ukv")
    r3 = lambda a: a.reshape(B, S, a.shape[-1])
    y3 = r3(y)
    o_a = _gqa_flash((y3, 0), (y3, 8), (y3, 10), name="swa_attn", B=B, S=S, Hk=A_KV_HEADS,
                     G=A_HEADS // A_KV_HEADS, window=A_WINDOW, sinks=sinks, tq=256, tk=256)
    hp = 8
    head = lambda h: h
    o_b = _flash([(r3(qq), hp * LANES, head), (r3(qq), hp * LANES, lambda h: B_HEADS // hp + h)],
                 [(r3(kv), hp * 2 * LANES, head), (y3, LANES, lambda h: 18)],
                 [[(0, g * LANES), (1, g * LANES)] for g in range(hp)],
                 [[(0, 2 * g * LANES), (1, 0)] for g in range(hp)], [(0, (2 * g + 1) * LANES) for g in range(hp)],
                 name="mla_attn", B=B, S=S, n_steps=B_HEADS // hp, tq=1024, tk=1024, rc=256)
    wo = w_o.astype(BF16)
    na = A_HEADS * HEAD_DIM
    return _mm([(o_a.reshape(T, -1), 0, na), (o_b.reshape(T, -1), 0, B_HEADS * B_V)], [wo[:na], wo[na:]],
               resid=h, tm=512, tn=D, name="ab_out")


def _layer_odd(h, B, S, tabs, cmp_tabs, norm_g, w_in, forget_bias, pe_k, w1_k, w2_k, pe_v, w1_v, w2_v, w_o):
    T, D = h.shape
    hc, kvw = C_HEADS * HEAD_DIM, D_KV_HEADS * HEAD_DIM
    o_cf = 3 * hc
    o_dq = o_cf + C_HEADS
    o_kc = o_dq + D_HEADS * HEAD_DIM
    o_ks = o_kc + 2 * kvw
    o_dg = o_ks + 4 * kvw
    n_in, tn = 5760, 640
    w16 = w_in.astype(BF16)
    w_r = jnp.concatenate([w16[:, :o_cf], w16[:, o_dq:o_kc], w16[:, o_ks:o_dg], w16[:, o_kc:o_ks], w16[:, o_cf:o_dq],
                           w16[:, o_dg:], jnp.zeros((D, n_in - w_in.shape[1]), BF16)], axis=1)
    sc = HEAD_DIM ** -0.5 * LOG2E
    prog = ([("copy", sc)] * 8 + [("copy", 1.0)] * 16 + [("rot128", sc)] * 8
            + [("rot128", 1.0)] * 2 + [("copy", 1.0)] * 2 + [("rot128", 1.0)] * 2 + [("copy", 1.0)] * 7)
    y, yf = _mm([(h, 0, D)], [w_r], gain=norm_g, tn=tn, out_dtype=BF16,
                col_prog=prog, tables=tabs, f32_from=n_in // tn - 1, name="cd_in")
    misc = tn // LANES - 1
    r3 = lambda a: a.reshape(B, S, a.shape[-1])
    y3, yf3 = r3(y), r3(yf)
    fb = jnp.pad(forget_bias.astype(F32), (0, LANES - C_HEADS)).reshape(1, LANES)
    cum = _forget_cum(yf3, misc, fb)
    cumt = jnp.swapaxes(cum[:, :, :C_HEADS], 1, 2).reshape(B, C_HEADS, 1, S)
    hp = 8
    at = lambda first: (lambda h: first + h)
    o_c = _flash([(y3, hp * LANES, at(0))], [(y3, hp * LANES, at(C_HEADS // hp)), (y3, hp * LANES, at(2 * C_HEADS // hp))],
                 [[(0, g * LANES)] for g in range(hp)], [[(0, g * LANES)] for g in range(hp)],
                 [(1, g * LANES) for g in range(hp)], name="fox_attn", B=B, S=S, n_steps=C_HEADS // hp,
                 tq=1024, tk=1024, cum=cum, cumt=cumt)
    G = D_HEADS // D_KV_HEADS
    k_cmp = _compress(yf3, 0, pe_k, w1_k, w2_k, rope_tabs=cmp_tabs)
    v_cmp = _compress(yf3, kvw, pe_v, w1_v, w2_v)
    q_d = (y3, 24 // G)
    o_cmp, sel = _cmp_attn(y3, q_d[1], k_cmp, v_cmp)
    o_slc = _gqa_flash(q_d, (y3, 32), (y3, 34), name="nsa_slc_attn", B=B, S=S, Hk=D_KV_HEADS, G=G, sel=sel,
                       tq=1024, tk=1024, rc=256)
    o_win = _gqa_flash(q_d, (y3, 36), (y3, 38), name="nsa_win_attn", B=B, S=S, Hk=D_KV_HEADS, G=G,
                       window=D_WINDOW, tq=512, tk=512)
    o_d = _nsa_gate(yf, misc, C_HEADS, o_cmp.reshape(T, -1), o_slc.reshape(T, -1), o_win.reshape(T, -1))
    wo = w_o.astype(BF16)
    return _mm([(o_c.reshape(T, -1), 0, hc), (o_d, 0, D_HEADS * HEAD_DIM)], [wo[:hc], wo[hc:]],
               resid=h, tm=512, tn=D, name="cd_out")


def kernel(x, p, positions, ab_w_in, ab_sinks, ab_q_lat_norm, ab_kv_lat_norm, ab_w_uq, ab_w_ukv, ab_w_o,
           cd_w_in, cd_forget_bias, cd_cmp_pe_k, cd_cmp_w1_k, cd_cmp_w2_k, cd_cmp_pe_v, cd_cmp_w1_v,
           cd_cmp_w2_v, cd_w_o, mixer_norm, moe_norm, router_group_w, router_group_b, router_expert_w,
           router_expert_b, expert_w_gate, expert_w_up, expert_w_down, ple_proj, ple_gate_norm, ple_gate_w,
           final_norm):
    B, S, D = x.shape
    T = B * S
    depth = p.shape[0]
    tabs = _rope_tables(positions)
    nc = S // D_CMP_STRIDE
    end = np.minimum(np.arange(nc) * D_CMP_STRIDE + D_CMP_LEN - 1, S - 1)
    cmp_tabs = [t.reshape(B, nc, LANES) for t in _rope_tables(positions[:, end])[:2]]
    h = x.reshape(T, D)
    for i in range(depth):
        j = i // 2
        if i % 2 == 0:
            h = _layer_even(h, B, S, tabs, mixer_norm[i], ab_w_in[j], ab_sinks[j], ab_q_lat_norm[j],
                            ab_kv_lat_norm[j], ab_w_uq[j], ab_w_ukv[j], ab_w_o[j])
        else:
            h = _layer_odd(h, B, S, tabs, cmp_tabs, mixer_norm[i], cd_w_in[j], cd_forget_bias[j],
                           cd_cmp_pe_k[j], cd_cmp_w1_k[j], cd_cmp_w2_k[j], cd_cmp_pe_v[j], cd_cmp_w1_v[j],
                           cd_cmp_w2_v[j], cd_w_o[j])
        h = _moe(h, moe_norm[i], router_group_w[i], router_group_b[i], router_expert_w[i], router_expert_b[i],
                 i, expert_w_gate, expert_w_up, expert_w_down)
        h = _mm([(h, 0, D)], [ple_gate_w[i].astype(BF16)], gain=ple_gate_norm[i], tm=512, tn=D,
                ple=(h, p[i].reshape(T, -1), ple_proj[i].astype(BF16)),
                out_gain=final_norm if i == depth - 1 else None, name="ple")
    return h.reshape(B, S, D)
```

```python
import functools
import math

import numpy as np
import jax
import jax.numpy as jnp
from jax import lax
from jax.experimental import pallas as pl
from jax.experimental.pallas import tpu as pltpu

F32 = jnp.float32
BF16 = jnp.bfloat16

HEAD_DIM = 128
ROPE_THETA = 10000.0
NORM_EPS = 1e-6
NEG_INF = -1e30
TAKEN = -3e38
A_HEADS, A_KV_HEADS, A_WINDOW = 8, 2, 128
B_HEADS, B_Q_LORA, B_KV_LORA, B_NOPE, B_ROPE, B_V = 8, 512, 256, 128, 64, 128
C_HEADS = 8
D_HEADS, D_KV_HEADS = 8, 2
D_CMP_LEN, D_CMP_STRIDE, D_SLC_LEN, D_SLC_TOPN, D_WINDOW = 32, 16, 64, 8, 512
FORCE_BONUS = 1e4
N_GROUPS, EXPERTS_PER_GROUP, TOP_K = 4, 8, 2
N_EXPERTS = N_GROUPS * EXPERTS_PER_GROUP

LANES = 128
VMEM_LIMIT_BYTES = 56 * 1024 * 1024
MOE_ROWS = 256
DMA_UNROLL = 16
FLASH_ROW_CHUNK = 128
LOG2E = math.log2(math.e)


def _params(*sem):
    return pltpu.CompilerParams(dimension_semantics=sem, vmem_limit_bytes=VMEM_LIMIT_BYTES)


def _tile(n, pref):
    t = min(n, pref)
    while n % t:
        t -= 1
    return t


def _pack_bf16_pairs(y):
    n = y.shape[1] // 2
    bits = lambda a: lax.bitcast_convert_type(a.astype(BF16).astype(F32), jnp.uint32)
    return (bits(y[:, :n]) >> 16) | (bits(y[:, n:]) & jnp.uint32(0xFFFF0000))


def _unpack_bf16_pairs(w):
    return (lax.bitcast_convert_type(w << 16, F32), lax.bitcast_convert_type(w & jnp.uint32(0xFFFF0000), F32))


def _rot128(x, c, s):
    return x * c + pltpu.roll(x, 64, 1) * s


def _rot_perms():
    i = np.arange(LANES)
    p128 = (i[:, None] == (i[None, :] + HEAD_DIM // 2) % LANES)
    half = B_ROPE // 2
    p64 = ((i[None, :] < half) & (i[:, None] == i[None, :] + half)) | (
        (i[None, :] >= half) & (i[None, :] < B_ROPE) & (i[:, None] == i[None, :] - half))
    return [jnp.asarray(p.astype(np.float32), BF16) for p in (p128, p64)]


def _tiles_store(ref, lead, words):
    rows, c = words.shape[0], words.shape[1] // LANES
    for j in range(c):
        ref[lead + (pl.ds(j, rows, stride=c), slice(None))] = words[:, j * LANES:(j + 1) * LANES]


def _tiles_load(ref, lead, rows, c):
    return jnp.concatenate([ref[lead + (pl.ds(j, rows, stride=c), slice(None))] for j in range(c)], axis=1)


def _mm_body(*refs, nx, has_gain, mode, resid_is_x, has_out_gain, col_prog, f32_from):
    it = iter(refs)
    x_refs = [next(it) for _ in range(nx)]
    g_ref = next(it) if has_gain else None
    w_refs = [next(it) for _ in range(nx)]
    r_ref = next(it) if mode in ("resid", "ple") and not resid_is_x else None
    p_ref = next(it) if mode == "ple" else None
    wp_ref = next(it) if mode == "ple" else None
    og_ref = next(it) if has_out_gain else None
    tab_refs = [next(it) for _ in range(7)] if col_prog is not None else None
    o_ref = next(it)
    of_ref = next(it) if f32_from is not None else None
    xn_ref = next(it) if has_gain else None

    if has_gain:
        @pl.when(pl.program_id(1) == 0)
        def _():
            x = x_refs[0][...].astype(F32)
            y = x * lax.rsqrt(jnp.mean(x * x, axis=-1, keepdims=True) + NORM_EPS) * g_ref[...]
            xn_ref[...] = y.astype(BF16)
        lhs = [xn_ref[...]]
    else:
        lhs = [x_ref[...].astype(BF16) for x_ref in x_refs]
    acc = None
    for a, w_ref in zip(lhs, w_refs):
        d = jnp.dot(a, w_ref[...], preferred_element_type=F32)
        acc = d if acc is None else acc + d
    if resid_is_x:
        r_ref = x_refs[0]
    if mode == "resid":
        acc = r_ref[...] + acc
    elif mode == "ple":
        pp = jnp.dot(p_ref[...].astype(BF16), wp_ref[...], preferred_element_type=F32)
        acc = r_ref[...] + pp * jax.nn.sigmoid(acc)
    if has_out_gain:
        acc = acc * lax.rsqrt(jnp.mean(acc * acc, axis=-1, keepdims=True) + NORM_EPS) * og_ref[...]
    if col_prog is None:
        o_ref[...] = acc.astype(o_ref.dtype)
        return
    j = pl.program_id(1)
    nb = o_ref.shape[1] // LANES
    tiles = {}
    for jj in range(len(col_prog) // nb):
        tiles.setdefault(tuple(col_prog[jj * nb:(jj + 1) * nb]), []).append(jj)
    for prog, jjs in tiles.items():
        @pl.when(functools.reduce(jnp.logical_or, [j == jj for jj in jjs]))
        def _():
            c128, s128, c64, s64a, s64b, p128, p64 = tab_refs
            for b, (kind, scale) in enumerate(prog):
                x = acc[:, b * LANES:(b + 1) * LANES]
                if kind == "rot128":
                    x = x * c128[...] + jnp.dot(x.astype(BF16), p128[...], preferred_element_type=F32) * s128[...]
                elif kind == "rot64":
                    x = x * c64[...] + jnp.dot(x.astype(BF16), p64[...], preferred_element_type=F32) * (
                        s64a[...] + s64b[...])
                if scale != 1.0:
                    x = x * scale
                o_ref[:, b * LANES:(b + 1) * LANES] = x.astype(o_ref.dtype)
    if f32_from is not None:
        @pl.when(j >= f32_from)
        def _():
            of_ref[...] = acc


def _mm(xs, ws, *, name, gain=None, out_dtype=F32, tm=1024, tn=512, resid=None, ple=None,
        out_gain=None, col_prog=None, tables=None, f32_from=None):
    M = xs[0][0].shape[0]
    N = ws[0].shape[1]
    tm, tn = _tile(M, tm), _tile(N, tn)
    nx = len(xs)
    has_gain = gain is not None
    mode = "ple" if ple is not None else ("resid" if resid is not None else "none")
    r_arr = ple[0] if mode == "ple" else resid
    resid_is_x = r_arr is xs[0][0] and tn == N == xs[0][2] and xs[0][1] == 0
    assert out_gain is None or tn == N
    args, in_specs = [], []
    for arr, cb, K in xs:
        args.append(arr)
        in_specs.append(pl.BlockSpec((tm, K), lambda i, j, cb=cb: (i, cb)))
    if has_gain:
        K0 = xs[0][2]
        args.append(gain.reshape(1, K0).astype(F32))
        in_specs.append(pl.BlockSpec((1, K0), lambda i, j: (0, 0)))
    for (arr, cb, K), w in zip(xs, ws):
        args.append(w)
        in_specs.append(pl.BlockSpec((K, tn), lambda i, j: (0, j)))
    if mode in ("resid", "ple") and not resid_is_x:
        args.append(r_arr)
        in_specs.append(pl.BlockSpec((tm, tn), lambda i, j: (i, j)))
    if mode == "ple":
        _, p, wp = ple
        args += [p, wp]
        in_specs += [pl.BlockSpec((tm, p.shape[1]), lambda i, j: (i, 0)),
                     pl.BlockSpec((p.shape[1], tn), lambda i, j: (0, j))]
    if out_gain is not None:
        args.append(out_gain.reshape(1, N).astype(F32))
        in_specs.append(pl.BlockSpec((1, N), lambda i, j: (0, 0)))
    if col_prog is not None:
        assert len(col_prog) * LANES == N
        args += list(tables) + _rot_perms()
        in_specs += [pl.BlockSpec((tm, LANES), lambda i, j: (i, 0))] * 5
        in_specs += [pl.BlockSpec((LANES, LANES), lambda i, j: (0, 0))] * 2
    out_shape = [jax.ShapeDtypeStruct((M, N), out_dtype)]
    out_specs = [pl.BlockSpec((tm, tn), lambda i, j: (i, j))]
    if f32_from is not None:
        out_shape.append(jax.ShapeDtypeStruct((M, N - f32_from * tn), F32))
        out_specs.append(pl.BlockSpec((tm, tn), lambda i, j: (i, jnp.maximum(j - f32_from, 0))))
    scratch = [pltpu.VMEM((tm, xs[0][2]), BF16)] if has_gain else []
    res = pl.pallas_call(
        functools.partial(_mm_body, nx=nx, has_gain=has_gain, mode=mode, resid_is_x=resid_is_x,
                          has_out_gain=out_gain is not None,
                          col_prog=None if col_prog is None else tuple(col_prog), f32_from=f32_from),
        out_shape=out_shape, grid=(M // tm, N // tn), in_specs=in_specs, out_specs=out_specs,
        scratch_shapes=scratch, compiler_params=_params("arbitrary", "arbitrary"), name=name,
    )(*args)
    return res if len(res) > 1 else res[0]


def _rope_tables(positions):
    def tables(dim):
        inv = 1.0 / (ROPE_THETA ** (jnp.arange(0, dim, 2, dtype=F32) / dim))
        ang = positions.astype(F32)[..., None] * inv
        return jnp.cos(ang), jnp.sin(ang)

    lead = positions.shape
    c, s = tables(HEAD_DIM)
    c128 = jnp.concatenate([c, c], -1)
    s128 = jnp.concatenate([-s, s], -1)
    c, s = tables(B_ROPE)
    z = jnp.zeros_like(c)
    c64 = jnp.concatenate([c, c, z, z], -1)
    s64a = jnp.concatenate([-s, z, z, z], -1)
    s64b = jnp.concatenate([z, s, z, z], -1)
    n = int(np.prod(lead))
    return [t.reshape(n, LANES) for t in (c128, s128, c64, s64a, s64b)]


def _pair_tables(S, tq, tk, window):
    qi_l, ki_l, fl_l, mk_l, masks, ids = [], [], [], [], [], {}
    for qi in range(S // tq):
        q0, q1 = qi * tq, (qi + 1) * tq - 1
        ks = []
        for ki in range(S // tk):
            k0, k1 = ki * tk, (ki + 1) * tk - 1
            if k0 > q1 or (window is not None and q0 - k1 >= window):
                continue
            full = k1 <= q0 and (window is None or q1 - k0 < window)
            mid = 0
            if not full:
                delta = q0 - k0
                if delta not in ids:
                    rel = np.arange(tq)[:, None] + delta - np.arange(tk)[None, :]
                    ok = (rel >= 0) if window is None else ((rel >= 0) & (rel < window))
                    masks.append(np.where(ok, 0.0, NEG_INF).astype(np.float32))
                    ids[delta] = len(masks)
                mid = ids[delta]
            ks.append((ki, mid))
        for n, (ki, mid) in enumerate(ks):
            qi_l.append(qi)
            ki_l.append(ki)
            fl_l.append((1 if n == 0 else 0) | (2 if n == len(ks) - 1 else 0))
            mk_l.append(mid)
    if not masks:
        masks.append(np.zeros((tq, tk), np.float32))
    tabs = [jnp.asarray(np.array(a, np.int32)) for a in (qi_l, ki_l, fl_l, mk_l)]
    return tabs, jnp.asarray(np.stack(masks))


def _flash_body(qi_t, ki_t, fl_t, mk_t, *refs, nq, nkv, q_src, k_src, v_src, tq, rc, causal_diag,
                has_sink, has_bias, has_sel):
    nc = len(q_src)
    it = iter(refs)
    q_refs = [next(it) for _ in range(nq)]
    kv_refs = [next(it) for _ in range(nkv)]
    mask_ref = next(it)
    sink_ref = next(it) if has_sink else None
    cum_ref = next(it) if has_bias else None
    cumt_ref = next(it) if has_bias else None
    sel_ref = next(it) if has_sel else None
    exp_ref = next(it) if has_sel else None
    o_ref = next(it)
    m_sc, acc_sc = next(it), next(it)
    cq_sc = next(it) if has_bias else None

    step_id = pl.program_id(2)
    fl, mk = fl_t[step_id], mk_t[step_id]

    def cat(refs_, src, rows=slice(None)):
        xs = [refs_[pi][0, rows, off:off + LANES] for pi, off in src]
        return xs[0] if len(xs) == 1 else jnp.concatenate(xs, axis=-1)

    @pl.when((fl & 1) != 0)
    def _init():
        m_sc[...] = jnp.full(m_sc.shape, NEG_INF, F32)
        acc_sc[...] = jnp.zeros(acc_sc.shape, F32)
        if has_bias:
            lane = lax.broadcasted_iota(jnp.int32, (tq, LANES), 1)
            for g in range(nc):
                head = pl.program_id(1) * nc + g
                cq_sc[g] = jnp.sum(jnp.where(lane == head, cum_ref[0], 0.0), axis=-1, keepdims=True)

    tk = kv_refs[0].shape[1]
    ones = jnp.ones((tk, LANES), BF16)

    def step(masked):
        ks = [cat(kv_refs, k_src[g]) for g in range(nc)]
        vs = [jnp.concatenate([kv_refs[pi][0, :, off:off + LANES], ones], axis=1) for pi, off in v_src]
        for r in range(tq // rc):
            rows = slice(r * rc, (r + 1) * rc)
            nk = min((r + 1) * rc, tk) if (masked and causal_diag) else tk
            reps = nk // LANES
            add = mask_ref[mk - 1, rows, :nk] if masked else None
            if has_sel:
                hidden = (jnp.dot(sel_ref[0, 0, rows, :], exp_ref[0, :, :nk], preferred_element_type=F32)
                          - 1.0) * (-NEG_INF)
                add = hidden if add is None else add + hidden
            for g in range(nc):
                q = cat(q_refs, q_src[g], rows)
                s = lax.dot_general(q, ks[g][:nk], (((1,), (1,)), ((), ())), preferred_element_type=F32)
                if has_bias:
                    s = s + (cq_sc[g, rows, :] - cumt_ref[0, g, :, :nk])
                if add is not None:
                    s = s + add
                m_prev = m_sc[g, rows, :]
                m_new = jnp.maximum(m_prev, jnp.max(s, axis=-1, keepdims=True))
                p = jnp.exp2(s - (jnp.concatenate([m_new] * reps, axis=1) if reps > 1 else m_new))
                alpha = jnp.exp2(m_prev - m_new)
                acc_sc[g, rows, :] = (jnp.concatenate([alpha, alpha], axis=1) * acc_sc[g, rows, :]
                                      + jnp.dot(p.astype(BF16), vs[g][:nk], preferred_element_type=F32))
                m_sc[g, rows, :] = m_new

    @pl.when(mk != 0)
    def _():
        step(True)

    @pl.when(mk == 0)
    def _():
        step(False)

    @pl.when((fl & 2) != 0)
    def _finish():
        for g in range(nc):
            m, acc, l = m_sc[g], acc_sc[g, :, :LANES], acc_sc[g, :, LANES:]
            if has_sink:
                sk = sink_ref[0, g:g + 1, 0:1] * LOG2E
                m_f = jnp.maximum(m, sk)
                w = jnp.exp2(m - m_f)
                l = l * w + jnp.exp2(sk - m_f)
                acc = acc * w
            o_ref[0, :, g * LANES:(g + 1) * LANES] = (acc / l).astype(o_ref.dtype)


def _flash(q_parts, kv_parts, q_src, k_src, v_src, *, name, B, S, n_steps, tq, tk, window=None, sinks=None,
           cum=None, cumt=None, sel=None, out_dtype=BF16, rc=FLASH_ROW_CHUNK):
    nc = len(q_src)
    tq, tk = _tile(S, tq), _tile(S, tk)
    tabs, masks = _pair_tables(S, tq, tk, window)
    npairs = int(tabs[0].shape[0])
    args, in_specs = [], []
    for arr, width, cf in q_parts:
        args.append(arr)
        in_specs.append(pl.BlockSpec((1, tq, width), lambda b, h, s, qt, kt, ft, mt, cf=cf: (b, qt[s], cf(h))))
    for arr, width, cf in kv_parts:
        args.append(arr)
        in_specs.append(pl.BlockSpec((1, tk, width), lambda b, h, s, qt, kt, ft, mt, cf=cf: (b, kt[s], cf(h))))
    args.append(masks)
    in_specs.append(pl.BlockSpec(masks.shape, lambda b, h, s, qt, kt, ft, mt: (0, 0, 0)))
    if sinks is not None:
        args.append(jnp.broadcast_to(sinks.astype(F32).reshape(n_steps, nc, 1), (n_steps, nc, LANES)))
        in_specs.append(pl.BlockSpec((1, nc, LANES), lambda b, h, s, qt, kt, ft, mt: (h, 0, 0)))
    if cum is not None:
        args += [cum, cumt]
        in_specs += [pl.BlockSpec((1, tq, LANES), lambda b, h, s, qt, kt, ft, mt: (b, qt[s], 0)),
                     pl.BlockSpec((1, nc, 1, tk), lambda b, h, s, qt, kt, ft, mt: (b, h, 0, kt[s]))]
    if sel is not None:
        per = tk // D_SLC_LEN
        e = np.zeros((S // tk, LANES, tk), np.float32)
        for ki in range(S // tk):
            e[ki, ki * per + np.arange(tk) // D_SLC_LEN, np.arange(tk)] = 1.0
        args += [sel, jnp.asarray(e, BF16)]
        in_specs += [pl.BlockSpec((1, 1, tq, LANES), lambda b, h, s, qt, kt, ft, mt: (b, h, qt[s], 0)),
                     pl.BlockSpec((1, LANES, tk), lambda b, h, s, qt, kt, ft, mt: (kt[s], 0, 0))]
    scratch = [pltpu.VMEM((nc, tq, LANES), F32), pltpu.VMEM((nc, tq, 2 * LANES), F32)]
    if cum is not None:
        scratch.append(pltpu.VMEM((nc, tq, 1), F32))
    body = functools.partial(_flash_body, nq=len(q_parts), nkv=len(kv_parts), q_src=q_src, k_src=k_src,
                             v_src=v_src, tq=tq, rc=_tile(tq, rc), causal_diag=window is None and tq == tk,
                             has_sink=sinks is not None,
                             has_bias=cum is not None, has_sel=sel is not None)
    return pl.pallas_call(
        body, out_shape=jax.ShapeDtypeStruct((B, S, n_steps * nc * LANES), out_dtype),
        grid_spec=pltpu.PrefetchScalarGridSpec(
            num_scalar_prefetch=4, grid=(B, n_steps, npairs), in_specs=in_specs,
            out_specs=pl.BlockSpec((1, tq, nc * LANES), lambda b, h, s, qt, kt, ft, mt: (b, qt[s], h)),
            scratch_shapes=scratch),
        compiler_params=_params("arbitrary", "arbitrary", "arbitrary"), name=name,
    )(*tabs, *args)


def _gqa_flash(q, k, v, *, name, B, S, Hk, G, **kw):
    at = lambda first: (lambda h: first + h)
    return _flash([(q[0], G * LANES, at(q[1]))], [(k[0], LANES, at(k[1])), (v[0], LANES, at(v[1]))],
                  [[(0, g * LANES)] for g in range(G)], [[(0, 0)]] * G, [(1, 0)] * G,
                  name=name, B=B, S=S, n_steps=Hk, **kw)


def _cum_body(y_ref, b_ref, tri_ref, o_ref, carry):
    @pl.when(pl.program_id(1) == 0)
    def _():
        carry[...] = jnp.zeros(carry.shape, F32)
    x = y_ref[0] + b_ref[...]
    logf = jnp.minimum(x, 0.0) - jnp.log1p(jnp.exp(-jnp.abs(x)))
    cum = jnp.dot(tri_ref[...], logf, preferred_element_type=F32, precision=lax.Precision.HIGHEST) + carry[...]
    o_ref[0] = cum * LOG2E
    carry[...] = cum[-1:, :]


def _forget_cum(y3, col_block, bias_row, *, ts=512):
    B, S, _ = y3.shape
    ts = _tile(S, ts)
    tri = jnp.asarray(np.tril(np.ones((ts, ts), np.float32)))
    return pl.pallas_call(
        _cum_body, out_shape=jax.ShapeDtypeStruct((B, S, LANES), F32), grid=(B, S // ts),
        in_specs=[pl.BlockSpec((1, ts, LANES), lambda b, s: (b, s, col_block)),
                  pl.BlockSpec((1, LANES), lambda b, s: (0, 0)),
                  pl.BlockSpec((ts, ts), lambda b, s: (0, 0))],
        out_specs=pl.BlockSpec((1, ts, LANES), lambda b, s: (b, s, 0)),
        scratch_shapes=[pltpu.VMEM((1, LANES), F32)],
        compiler_params=_params("arbitrary", "arbitrary"), name="forget_cum",
    )(y3, bias_row, tri)


def _compress_body(*refs, rope, nc, width, col0):
    if rope:
        z_ref, pe_ref, w1_ref, w2_ref, c_ref, s_ref, o_ref = refs
    else:
        z_ref, pe_ref, w1_ref, w2_ref, o_ref = refs
    half = D_CMP_LEN // 2
    for hk in range(D_KV_HEADS):
        u = jnp.zeros((nc, w1_ref.shape[1]), F32)
        v = jnp.zeros((nc, w1_ref.shape[1]), F32)
        for l in range(half):
            first = l * width + col0 + hk * HEAD_DIM
            z = z_ref[0, :, first:first + HEAD_DIM]
            zu = (z + pe_ref[l:l + 1, :]).astype(BF16)
            zv = (z + pe_ref[half + l:half + l + 1, :]).astype(BF16)
            u = u + jnp.dot(zu, w1_ref[l * HEAD_DIM:(l + 1) * HEAD_DIM, :], preferred_element_type=F32)
            v = v + jnp.dot(zv, w1_ref[(half + l) * HEAD_DIM:(half + l + 1) * HEAD_DIM, :],
                            preferred_element_type=F32)
        pre = u + pltpu.roll(v, nc - 1, 0)
        hid = jax.nn.gelu(pre, approximate=True)
        out = jnp.dot(hid.astype(BF16), w2_ref[...], preferred_element_type=F32)
        if rope:
            out = _rot128(out, c_ref[0], s_ref[0])
        o_ref[0, hk] = out.astype(o_ref.dtype)


def _compress(z, col0, pe, w1, w2, rope_tabs=None):
    B, S, W = z.shape
    nc = S // D_CMP_STRIDE
    zc = z.reshape(B, nc, D_CMP_STRIDE * W)
    args = [zc, pe.astype(F32), w1.astype(BF16), w2.astype(BF16)]
    in_specs = [pl.BlockSpec((1, nc, D_CMP_STRIDE * W), lambda b: (b, 0, 0)),
                pl.BlockSpec(pe.shape, lambda b: (0, 0)),
                pl.BlockSpec(w1.shape, lambda b: (0, 0)),
                pl.BlockSpec(w2.shape, lambda b: (0, 0))]
    if rope_tabs is not None:
        args += list(rope_tabs)
        in_specs += [pl.BlockSpec((1, nc, LANES), lambda b: (b, 0, 0))] * 2
    return pl.pallas_call(
        functools.partial(_compress_body, rope=rope_tabs is not None, nc=nc, width=W, col0=col0),
        out_shape=jax.ShapeDtypeStruct((B, D_KV_HEADS, nc, HEAD_DIM), BF16), grid=(B,),
        in_specs=in_specs, out_specs=pl.BlockSpec((1, D_KV_HEADS, nc, HEAD_DIM), lambda b: (b, 0, 0, 0)),
        compiler_params=_params("arbitrary"), name="nsa_compress",
    )(*args)


def _cmp_attn_body(q_ref, k_ref, v_ref, ov_ref, o_ref, sel_ref, *, G, tq, nc, n_cmp, n_slc, topn):
    qi = pl.program_id(2)
    t = qi * tq + lax.broadcasted_iota(jnp.int32, (tq, nc), 0)
    c = lax.broadcasted_iota(jnp.int32, (tq, nc), 1)
    valid = (c * D_CMP_STRIDE + (D_CMP_LEN - 1) <= t) & (c < n_cmp)
    k = k_ref[0, 0]
    v = v_ref[0, 0]
    psum = jnp.zeros((tq, nc), F32)
    for g in range(G):
        q = q_ref[0, :, g * LANES:(g + 1) * LANES]
        s = lax.dot_general(q, k, (((1,), (1,)), ((), ())), preferred_element_type=F32)
        s = jnp.where(valid, s, NEG_INF)
        e = jnp.where(valid, jnp.exp2(s - jnp.max(s, axis=-1, keepdims=True)), 0.0)
        p = e / jnp.maximum(jnp.sum(e, axis=-1, keepdims=True), jnp.finfo(F32).tiny)
        o_ref[0, :, g * LANES:(g + 1) * LANES] = jnp.dot(
            p.astype(BF16), v, preferred_element_type=F32).astype(o_ref.dtype)
        psum = psum + p
    imp = lax.dot_general(ov_ref[...], psum, (((1,), (1,)), ((), ())), preferred_element_type=F32,
                          precision=lax.Precision.HIGHEST)
    blk = lax.broadcasted_iota(jnp.int32, (LANES, tq), 0)
    tcol = qi * tq + lax.broadcasted_iota(jnp.int32, (LANES, tq), 1)
    cur = jnp.right_shift(tcol, int(math.log2(D_SLC_LEN)))
    forced = (blk == 0) | (blk == cur) | (blk == cur - 1)
    imp = jnp.where(blk * D_SLC_LEN > tcol, NEG_INF, imp + jnp.where(forced, FORCE_BONUS, 0.0))
    imp = jnp.where(blk >= n_slc, TAKEN, imp)
    chosen = jnp.zeros((LANES, tq), F32)
    blk_f = blk.astype(F32)
    for _ in range(topn):
        mx = jnp.max(imp, axis=0, keepdims=True)
        idx = jnp.min(jnp.where(imp == mx, blk_f, float(LANES)), axis=0, keepdims=True)
        hit = blk_f == idx
        chosen = jnp.where(hit, 1.0, chosen)
        imp = jnp.where(hit, TAKEN, imp)
    sel_ref[0, 0] = chosen.T.astype(sel_ref.dtype)


def _cmp_attn(q, q_first, k_cmp, v_cmp, *, tq=1024):
    B, S, _ = q.shape
    Hk, G = D_KV_HEADS, D_HEADS // D_KV_HEADS
    nc = S // D_CMP_STRIDE
    n_cmp = (S - D_CMP_LEN) // D_CMP_STRIDE + 1
    n_slc = S // D_SLC_LEN
    tq = _tile(S, tq)
    c0 = np.arange(nc) * D_CMP_STRIDE
    s0 = np.arange(LANES) * D_SLC_LEN
    ov = ((c0[:, None] < (s0 + D_SLC_LEN)[None, :]) & ((c0 + D_CMP_LEN)[:, None] > s0[None, :])
          & (np.arange(nc) < n_cmp)[:, None] & (np.arange(LANES) < n_slc)[None, :]).astype(np.float32)
    body = functools.partial(_cmp_attn_body, G=G, tq=tq, nc=nc, n_cmp=n_cmp, n_slc=n_slc,
                             topn=min(D_SLC_TOPN, n_slc))
    return pl.pallas_call(
        body,
        out_shape=[jax.ShapeDtypeStruct((B, S, Hk * G * LANES), BF16),
                   jax.ShapeDtypeStruct((B, Hk, S, LANES), BF16)],
        grid=(B, Hk, S // tq),
        in_specs=[pl.BlockSpec((1, tq, G * LANES), lambda b, h, i: (b, i, q_first + h)),
                  pl.BlockSpec((1, 1, nc, LANES), lambda b, h, i: (b, h, 0, 0)),
                  pl.BlockSpec((1, 1, nc, LANES), lambda b, h, i: (b, h, 0, 0)),
                  pl.BlockSpec((LANES, nc), lambda b, h, i: (0, 0))],
        out_specs=[pl.BlockSpec((1, tq, G * LANES), lambda b, h, i: (b, i, h)),
                   pl.BlockSpec((1, 1, tq, LANES), lambda b, h, i: (b, h, i, 0))],
        compiler_params=_params("arbitrary", "arbitrary", "arbitrary"), name="nsa_cmp_attn",
    )(q, k_cmp, v_cmp, jnp.asarray(ov.T))


def _gate_body(y_ref, a_ref, b_ref, c_ref, o_ref, *, lane0):
    g = jax.nn.sigmoid(y_ref[...])
    for h in range(D_HEADS):
        cols = slice(h * LANES, (h + 1) * LANES)
        ga = g[:, lane0 + h:lane0 + h + 1]
        gb = g[:, lane0 + D_HEADS + h:lane0 + D_HEADS + h + 1]
        gc = g[:, lane0 + 2 * D_HEADS + h:lane0 + 2 * D_HEADS + h + 1]
        o_ref[:, cols] = (ga * a_ref[:, cols] + gb * b_ref[:, cols] + gc * c_ref[:, cols]).astype(o_ref.dtype)


def _nsa_gate(y, col_block, lane0, o_cmp, o_slc, o_win, *, tm=1024):
    M = y.shape[0]
    W = o_cmp.shape[1]
    tm = _tile(M, tm)
    row = lambda i: (i, 0)
    return pl.pallas_call(
        functools.partial(_gate_body, lane0=lane0), out_shape=jax.ShapeDtypeStruct((M, W), BF16),
        grid=(M // tm,),
        in_specs=[pl.BlockSpec((tm, LANES), lambda i: (i, col_block))] + [pl.BlockSpec((tm, W), row)] * 3,
        out_specs=pl.BlockSpec((tm, W), row), compiler_params=_params("arbitrary"), name="nsa_gate",
    )(y, o_cmp, o_slc, o_win)


def _route_body(h_ref, g_ref, wr_ref, b_ref, tri_ref, e_ref, w_ref, pos_ref, cnt_ref, xo_ref, carry, *, tm):
    @pl.when(pl.program_id(0) == 0)
    def _():
        carry[...] = jnp.zeros(carry.shape, F32)
    x = h_ref[...]
    y = x * lax.rsqrt(jnp.mean(x * x, axis=-1, keepdims=True) + NORM_EPS) * g_ref[...]
    _tiles_store(xo_ref, (), _pack_bf16_pairs(y))
    lane = lax.broadcasted_iota(jnp.int32, (tm, LANES), 1)
    logits = jnp.dot(y.astype(BF16), wr_ref[...], preferred_element_type=F32) + b_ref[...]
    gl = jnp.where(lane < N_GROUPS, logits, -jnp.inf)
    gmax = jnp.max(gl, axis=-1, keepdims=True)
    g_val = 1.0 / jnp.sum(jnp.exp(gl - gmax), axis=-1, keepdims=True)
    g_idx = jnp.min(jnp.where(gl == gmax, lane, LANES), axis=-1, keepdims=True)
    lo = N_GROUPS + EXPERTS_PER_GROUP * g_idx
    el = jnp.where((lane >= lo) & (lane < lo + EXPERTS_PER_GROUP), logits, -jnp.inf)
    e1 = jnp.max(el, axis=-1, keepdims=True)
    i1 = jnp.min(jnp.where(el == e1, lane, LANES), axis=-1, keepdims=True)
    el2 = jnp.where(lane == i1, -jnp.inf, el)
    e2 = jnp.max(el2, axis=-1, keepdims=True)
    i2 = jnp.min(jnp.where(el2 == e2, lane, LANES), axis=-1, keepdims=True)
    r = jnp.exp(e2 - e1)
    w1 = g_val / (1.0 + r)
    w2 = w1 * r
    x1, x2 = i1 - N_GROUPS, i2 - N_GROUPS
    e_ref[...] = jnp.where(lane == 0, x1, jnp.where(lane == 1, x2, 0))
    w_ref[...] = jnp.where(lane == 0, w1, jnp.where(lane == 1, w2, 0.0))
    hot1 = lane == x1
    hot2 = lane == x2
    both = jnp.where(hot1 | hot2, 1.0, 0.0)
    before = jnp.dot(tri_ref[...], both.astype(BF16), preferred_element_type=F32) + carry[...]
    p1 = jnp.sum(jnp.where(hot1, before, 0.0), axis=-1, keepdims=True)
    p2 = jnp.sum(jnp.where(hot2, before, 0.0), axis=-1, keepdims=True)
    pos_ref[...] = jnp.where(lane == 0, p1, jnp.where(lane == 1, p2, 0.0)).astype(jnp.int32)
    carry[...] = carry[...] + jnp.sum(both, axis=0, keepdims=True)
    cnt_ref[...] = carry[...].astype(jnp.int32)


def _route(h, gain, w_router, bias_row, *, tm=1024):
    T, D = h.shape
    c = D // 2 // LANES
    tm = _tile(T, tm)
    tri = jnp.asarray(np.tril(np.ones((tm, tm), np.float32), -1), BF16)
    row = lambda i: (i, 0)
    fixed = lambda i: (0, 0)
    return pl.pallas_call(
        functools.partial(_route_body, tm=tm),
        out_shape=[jax.ShapeDtypeStruct((T, LANES), jnp.int32), jax.ShapeDtypeStruct((T, LANES), F32),
                   jax.ShapeDtypeStruct((T, LANES), jnp.int32), jax.ShapeDtypeStruct((1, LANES), jnp.int32),
                   jax.ShapeDtypeStruct((T * c, LANES), jnp.uint32)],
        grid=(T // tm,),
        in_specs=[pl.BlockSpec((tm, D), row), pl.BlockSpec((1, D), fixed), pl.BlockSpec((D, LANES), fixed),
                  pl.BlockSpec((1, LANES), fixed), pl.BlockSpec((tm, tm), fixed)],
        out_specs=[pl.BlockSpec((tm, LANES), row)] * 3 + [pl.BlockSpec((1, LANES), fixed),
                                                          pl.BlockSpec((tm * c, LANES), row)],
        scratch_shapes=[pltpu.VMEM((1, LANES), F32)],
        compiler_params=_params("arbitrary"), name="moe_route",
    )(h, gain.reshape(1, D).astype(F32), w_router, bias_row, tri)


def _dispatch_body(d_ref, ends_ref, pad_ref, nu_ref, x_ref, out_ref, zero_sc, sem, zsem, *, tm, nblk, c):
    base = pl.program_id(0) * tm

    @pl.when(pl.program_id(0) == 0)
    def _():
        zero_sc[...] = jnp.zeros(zero_sc.shape, zero_sc.dtype)

        def zero_block(first_row):
            return pltpu.make_async_copy(
                zero_sc, out_ref.at[pl.ds(pl.multiple_of(first_row * c, MOE_ROWS * c), MOE_ROWS * c)], zsem)

        for wait in (False, True):
            for e in range(N_EXPERTS):
                for live, first_row in ((pad_ref[e] > 0, ends_ref[e] - MOE_ROWS),
                                        (nu_ref[0] + e < nblk, (nu_ref[0] + e) * MOE_ROWS)):
                    @pl.when(live)
                    def _():
                        zero_block(first_row).wait() if wait else zero_block(first_row).start()

    def issue(r, carry):
        for k in range(TOP_K):
            row = d_ref[(base + r) * TOP_K + k]
            pltpu.make_async_copy(x_ref.at[pl.ds(pl.multiple_of(r * c, c), c)],
                                  out_ref.at[pl.ds(pl.multiple_of(row * c, c), c)], sem).start(priority=k)
        return carry

    lax.fori_loop(0, tm, issue, 0, unroll=DMA_UNROLL)
    for _ in range(TOP_K):
        pltpu.make_async_copy(x_ref, out_ref.at[pl.ds(0, tm * c)], sem).wait()


def _dispatch(xn, dest, ends, padded, n_used, n_rows, c, *, tm=512):
    T = xn.shape[0] // c
    tm = _tile(T, tm)
    return pl.pallas_call(
        functools.partial(_dispatch_body, tm=tm, nblk=n_rows // MOE_ROWS, c=c),
        out_shape=jax.ShapeDtypeStruct((n_rows * c, LANES), xn.dtype),
        grid_spec=pltpu.PrefetchScalarGridSpec(
            num_scalar_prefetch=4, grid=(T // tm,),
            in_specs=[pl.BlockSpec((tm * c, LANES), lambda i, *_: (i, 0))],
            out_specs=pl.BlockSpec(memory_space=pl.ANY),
            scratch_shapes=[pltpu.VMEM((MOE_ROWS * c, LANES), xn.dtype), pltpu.SemaphoreType.DMA(()),
                            pltpu.SemaphoreType.DMA(())]),
        compiler_params=pltpu.CompilerParams(dimension_semantics=("arbitrary",), has_side_effects=True,
                                             vmem_limit_bytes=VMEM_LIMIT_BYTES, disable_bounds_checks=True),
        name="moe_dispatch",
    )(dest, ends, padded, n_used, xn)


def _expert_body(nb_ref, first_ref, nu_ref, xb_ref, wg_ref, wu_ref, wd_ref, yb_ref, wg_sc, wu_sc, wd_sc, xbuf,
                 obuf, sem_in, sem_out, state, *, nblk, c):
    e = pl.program_id(0)
    last = pl.num_programs(0) - 1
    nb, first = nb_ref[e], first_ref[e]

    @pl.when(e == 0)
    def _():
        for i in range(3):
            state[i] = 0

    def rows(first_block, b):
        return pl.ds(pl.multiple_of((first_block + b) * (MOE_ROWS * c), MOE_ROWS * c), MOE_ROWS * c)

    def fetch(first_block, b, slot):
        return pltpu.make_async_copy(xb_ref.at[rows(first_block, b)], xbuf.at[slot], sem_in.at[slot])

    def flush(b, slot):
        return pltpu.make_async_copy(obuf.at[slot], yb_ref.at[rows(first, b)], sem_out.at[slot])

    def drain(slot):
        @pl.when(state[slot] != 0)
        def _():
            pltpu.make_async_copy(obuf.at[slot], yb_ref.at[pl.ds(0, MOE_ROWS * c)], sem_out.at[slot]).wait()
            state[slot] = 0

    @pl.when(nb > 0)
    def _():
        @pl.when(state[2] == 0)
        def _():
            fetch(first, 0, 0).start()

        state[2] = 0
        wg_sc[...] = wg_ref[0, 0].astype(BF16)
        wu_sc[...] = wu_ref[0, 0].astype(BF16)
        wd_sc[...] = wd_ref[0, 0].astype(BF16)

        def block(b, carry):
            slot = lax.rem(b, 2)

            @pl.when(b + 1 < nb)
            def _():
                fetch(first, b + 1, 1 - slot).start()

            fetch(first, b, slot).wait()
            drain(slot)
            x = jnp.concatenate(_unpack_bf16_pairs(_tiles_load(xbuf, (slot,), MOE_ROWS, c)), axis=1).astype(BF16)
            gate = jnp.dot(x, wg_sc[...], preferred_element_type=F32)
            up = jnp.dot(x, wu_sc[...], preferred_element_type=F32)
            hid = (gate * jax.nn.sigmoid(gate) * up).astype(BF16)
            _tiles_store(obuf, (slot,), _pack_bf16_pairs(jnp.dot(hid, wd_sc[...], preferred_element_type=F32)))
            flush(b, slot).start()
            state[slot] = 1
            return carry

        lax.fori_loop(0, nb, block, 0)
        nxt = jnp.minimum(e + 1, last)

        @pl.when((e < last) & (nb_ref[nxt] > 0))
        def _():
            fetch(first_ref[nxt], 0, 0).start()
            state[2] = 1

    @pl.when(e == last)
    def _():
        drain(0)
        drain(1)
        obuf[0] = jnp.zeros(obuf.shape[1:], obuf.dtype)

        def tail(t):
            return pltpu.make_async_copy(obuf.at[0], yb_ref.at[rows(nu_ref[0], t)], sem_out.at[0])

        for wait in (False, True):
            for t in range(N_EXPERTS):
                @pl.when(nu_ref[0] + t < nblk)
                def _():
                    tail(t).wait() if wait else tail(t).start()


def _experts(xb, n_blocks, first_block, n_used, layer, w_gate, w_up, w_down):
    D, Hd = w_gate.shape[2], w_gate.shape[3]
    c = D // 2 // LANES
    nblk = xb.shape[0] // (MOE_ROWS * c)
    wspec = lambda shape: pl.BlockSpec((1, 1) + shape, lambda e, *_: (layer, e, 0, 0))
    return pl.pallas_call(
        functools.partial(_expert_body, nblk=nblk, c=c), out_shape=jax.ShapeDtypeStruct(xb.shape, xb.dtype),
        grid_spec=pltpu.PrefetchScalarGridSpec(
            num_scalar_prefetch=3, grid=(w_gate.shape[1],),
            in_specs=[pl.BlockSpec(memory_space=pl.ANY), wspec((D, Hd)), wspec((D, Hd)), wspec((Hd, D))],
            out_specs=pl.BlockSpec(memory_space=pl.ANY),
            scratch_shapes=[pltpu.VMEM((D, Hd), BF16), pltpu.VMEM((D, Hd), BF16), pltpu.VMEM((Hd, D), BF16),
                            pltpu.VMEM((2, MOE_ROWS * c, LANES), xb.dtype),
                            pltpu.VMEM((2, MOE_ROWS * c, LANES), xb.dtype),
                            pltpu.SemaphoreType.DMA((2,)), pltpu.SemaphoreType.DMA((2,)),
                            pltpu.SMEM((3,), jnp.int32)]),
        compiler_params=pltpu.CompilerParams(dimension_semantics=("arbitrary",), has_side_effects=True,
                                             vmem_limit_bytes=VMEM_LIMIT_BYTES),
        name="moe_experts",
    )(n_blocks, first_block, n_used, xb, w_gate, w_up, w_down)


def _collect_body(d_ref, h_ref, w_ref, yb_ref, o_ref, buf_a, buf_b, sem, *, tm, nsteps, c):
    i = pl.program_id(0)
    slot = lax.rem(i, 2)

    def fetch(step, slot_):
        base = step * tm

        def issue(r, carry):
            for k, buf in enumerate((buf_a, buf_b)):
                row = d_ref[(base + r) * TOP_K + k]
                pltpu.make_async_copy(yb_ref.at[pl.ds(pl.multiple_of(row * c, c), c)],
                                      buf.at[slot_, pl.ds(pl.multiple_of(r * c, c), c)],
                                      sem.at[slot_]).start(priority=k)
            return carry

        lax.fori_loop(0, tm, issue, 0, unroll=DMA_UNROLL)

    @pl.when(i == 0)
    def _():
        fetch(0, 0)

    @pl.when(i + 1 < nsteps)
    def _():
        fetch(i + 1, 1 - slot)

    for buf in (buf_a, buf_b):
        pltpu.make_async_copy(yb_ref.at[pl.ds(0, tm * c)], buf.at[slot], sem.at[slot]).wait()
    w = w_ref[...]
    half = h_ref.shape[1] // 2
    for part, ya, yb in zip((slice(0, half), slice(half, None)),
                            _unpack_bf16_pairs(_tiles_load(buf_a, (slot,), tm, c)),
                            _unpack_bf16_pairs(_tiles_load(buf_b, (slot,), tm, c))):
        o_ref[:, part] = h_ref[:, part] + w[:, 0:1] * ya + w[:, 1:2] * yb


def _collect(h, yb, dest, wts, *, tm=512):
    T, D = h.shape
    c = D // 2 // LANES
    tm = _tile(T, tm)
    nsteps = T // tm
    return pl.pallas_call(
        functools.partial(_collect_body, tm=tm, nsteps=nsteps, c=c),
        out_shape=jax.ShapeDtypeStruct((T, D), F32),
        grid_spec=pltpu.PrefetchScalarGridSpec(
            num_scalar_prefetch=1, grid=(nsteps,),
            in_specs=[pl.BlockSpec((tm, D), lambda i, d: (i, 0)), pl.BlockSpec((tm, LANES), lambda i, d: (i, 0)),
                      pl.BlockSpec(memory_space=pl.ANY)],
            out_specs=pl.BlockSpec((tm, D), lambda i, d: (i, 0)),
            scratch_shapes=[pltpu.VMEM((2, tm * c, LANES), yb.dtype), pltpu.VMEM((2, tm * c, LANES), yb.dtype),
                            pltpu.SemaphoreType.DMA((2,))]),
        compiler_params=pltpu.CompilerParams(dimension_semantics=("arbitrary",), disable_bounds_checks=True,
                                             vmem_limit_bytes=VMEM_LIMIT_BYTES),
        name="moe_collect",
    )(dest, h, wts, yb)


def _moe(h, norm_g, w_group, b_group, w_expert, b_expert, layer, w_gate, w_up, w_down):
    T, D = h.shape
    pad = LANES - N_GROUPS - N_EXPERTS
    w_r = jnp.concatenate([w_group, w_expert, jnp.zeros((D, pad), F32)], axis=1).astype(BF16)
    b_r = jnp.concatenate([b_group, b_expert, jnp.zeros((pad,), F32)]).astype(F32).reshape(1, LANES)
    eid, wts, pos, cnt, xn = _route(h, norm_g, w_r, b_r)
    counts = cnt[0, :N_EXPERTS]
    padded = (counts + MOE_ROWS - 1) // MOE_ROWS * MOE_ROWS
    ends = jnp.cumsum(padded)
    offs = ends - padded
    dest = (offs[eid[:, :TOP_K]] + pos[:, :TOP_K]).reshape(T * TOP_K).astype(jnp.int32)
    P = T * TOP_K + N_EXPERTS * MOE_ROWS
    n_used = (ends[-1:] // MOE_ROWS).astype(jnp.int32)
    xb = _dispatch(xn, dest, ends.astype(jnp.int32), padded.astype(jnp.int32), n_used, P, D // 2 // LANES)
    yb = _experts(xb, (padded // MOE_ROWS).astype(jnp.int32), (offs // MOE_ROWS).astype(jnp.int32), n_used, layer,
                  w_gate, w_up, w_down)
    return _collect(h, yb, dest, wts)


def _pad_cols(w, n):
    return jnp.pad(w, ((0, 0), (0, n - w.shape[1])))


def _layer_even(h, B, S, tabs, norm_g, w_in, sinks, q_lat_norm, kv_lat_norm, w_uq, w_ukv, w_o):
    T, D = h.shape
    n_in, tn = 2560, 512
    sa = HEAD_DIM ** -0.5 * LOG2E
    prog = ([("rot128", sa)] * 8 + [("rot128", 1.0)] * 2 + [("copy", 1.0)] * 8 + [("rot64", 1.0), ("copy", 1.0)])
    y, yf = _mm([(h, 0, D)], [_pad_cols(w_in.astype(BF16), n_in)], gain=norm_g, tn=tn, out_dtype=BF16,
                col_prog=prog, tables=tabs, f32_from=12 * LANES // tn, name="ab_in")
    wq = w_uq.reshape(B_Q_LORA, B_HEADS, B_NOPE + B_ROPE)
    wq_n = wq[:, :, :B_NOPE].reshape(B_Q_LORA, B_HEADS * B_NOPE)
    wq_r = jnp.pad(wq[:, :, B_NOPE:], ((0, 0), (0, 0), (0, LANES - B_ROPE))).reshape(B_Q_LORA, B_HEADS * LANES)
    sb = (B_NOPE + B_ROPE) ** -0.5 * LOG2E
    qq = _mm([(yf, 0, B_Q_LORA)], [jnp.concatenate([wq_n, wq_r], 1).astype(BF16)], gain=q_lat_norm, tn=512,
             out_dtype=BF16, col_prog=[("copy", sb)] * 8 + [("rot64", sb)] * 8, tables=tabs, name="mla_uq")
    kv = _mm([(yf, B_Q_LORA // B_KV_LORA, B_KV_LORA)], [w_ukv.astype(BF16)], gain=kv_lat_norm, out_dtype=BF16,
             tn=512, name="mla_ukv")
    r3 = lambda a: a.reshape(B, S, a.shape[-1])
    y3 = r3(y)
    o_a = _gqa_flash((y3, 0), (y3, 8), (y3, 10), name="swa_attn", B=B, S=S, Hk=A_KV_HEADS,
                     G=A_HEADS // A_KV_HEADS, window=A_WINDOW, sinks=sinks, tq=256, tk=256)
    hp = 8
    head = lambda h: h
    o_b = _flash([(r3(qq), hp * LANES, head), (r3(qq), hp * LANES, lambda h: B_HEADS // hp + h)],
                 [(r3(kv), hp * 2 * LANES, head), (y3, LANES, lambda h: 18)],
                 [[(0, g * LANES), (1, g * LANES)] for g in range(hp)],
                 [[(0, 2 * g * LANES), (1, 0)] for g in range(hp)], [(0, (2 * g + 1) * LANES) for g in range(hp)],
                 name="mla_attn", B=B, S=S, n_steps=B_HEADS // hp, tq=1024, tk=1024, rc=256)
    wo = w_o.astype(BF16)
    na = A_HEADS * HEAD_DIM
    return _mm([(o_a.reshape(T, -1), 0, na), (o_b.reshape(T, -1), 0, B_HEADS * B_V)], [wo[:na], wo[na:]],
               resid=h, tm=512, tn=D, name="ab_out")


def _layer_odd(h, B, S, tabs, cmp_tabs, norm_g, w_in, forget_bias, pe_k, w1_k, w2_k, pe_v, w1_v, w2_v, w_o):
    T, D = h.shape
    hc, kvw = C_HEADS * HEAD_DIM, D_KV_HEADS * HEAD_DIM
    o_cf = 3 * hc
    o_dq = o_cf + C_HEADS
    o_kc = o_dq + D_HEADS * HEAD_DIM
    o_ks = o_kc + 2 * kvw
    o_dg = o_ks + 4 * kvw
    n_in, tn = 5760, 640
    w16 = w_in.astype(BF16)
    w_r = jnp.concatenate([w16[:, :o_cf], w16[:, o_dq:o_kc], w16[:, o_ks:o_dg], w16[:, o_kc:o_ks], w16[:, o_cf:o_dq],
                           w16[:, o_dg:], jnp.zeros((D, n_in - w_in.shape[1]), BF16)], axis=1)
    sc = HEAD_DIM ** -0.5 * LOG2E
    prog = ([("copy", sc)] * 8 + [("copy", 1.0)] * 16 + [("rot128", sc)] * 8
            + [("rot128", 1.0)] * 2 + [("copy", 1.0)] * 2 + [("rot128", 1.0)] * 2 + [("copy", 1.0)] * 7)
    y, yf = _mm([(h, 0, D)], [w_r], gain=norm_g, tn=tn, out_dtype=BF16,
                col_prog=prog, tables=tabs, f32_from=n_in // tn - 1, name="cd_in")
    misc = tn // LANES - 1
    r3 = lambda a: a.reshape(B, S, a.shape[-1])
    y3, yf3 = r3(y), r3(yf)
    fb = jnp.pad(forget_bias.astype(F32), (0, LANES - C_HEADS)).reshape(1, LANES)
    cum = _forget_cum(yf3, misc, fb)
    cumt = jnp.swapaxes(cum[:, :, :C_HEADS], 1, 2).reshape(B, C_HEADS, 1, S)
    hp = 8
    at = lambda first: (lambda h: first + h)
    o_c = _flash([(y3, hp * LANES, at(0))], [(y3, hp * LANES, at(C_HEADS // hp)), (y3, hp * LANES, at(2 * C_HEADS // hp))],
                 [[(0, g * LANES)] for g in range(hp)], [[(0, g * LANES)] for g in range(hp)],
                 [(1, g * LANES) for g in range(hp)], name="fox_attn", B=B, S=S, n_steps=C_HEADS // hp,
                 tq=1024, tk=1024, cum=cum, cumt=cumt)
    G = D_HEADS // D_KV_HEADS
    k_cmp = _compress(yf3, 0, pe_k, w1_k, w2_k, rope_tabs=cmp_tabs)
    v_cmp = _compress(yf3, kvw, pe_v, w1_v, w2_v)
    q_d = (y3, 24 // G)
    o_cmp, sel = _cmp_attn(y3, q_d[1], k_cmp, v_cmp)
    o_slc = _gqa_flash(q_d, (y3, 32), (y3, 34), name="nsa_slc_attn", B=B, S=S, Hk=D_KV_HEADS, G=G, sel=sel,
                       tq=1024, tk=1024, rc=256)
    o_win = _gqa_flash(q_d, (y3, 36), (y3, 38), name="nsa_win_attn", B=B, S=S, Hk=D_KV_HEADS, G=G,
                       window=D_WINDOW, tq=512, tk=512)
    o_d = _nsa_gate(yf, misc, C_HEADS, o_cmp.reshape(T, -1), o_slc.reshape(T, -1), o_win.reshape(T, -1))
    wo = w_o.astype(BF16)
    return _mm([(o_c.reshape(T, -1), 0, hc), (o_d, 0, D_HEADS * HEAD_DIM)], [wo[:hc], wo[hc:]],
               resid=h, tm=512, tn=D, name="cd_out")


def kernel(x, p, positions, ab_w_in, ab_sinks, ab_q_lat_norm, ab_kv_lat_norm, ab_w_uq, ab_w_ukv, ab_w_o,
           cd_w_in, cd_forget_bias, cd_cmp_pe_k, cd_cmp_w1_k, cd_cmp_w2_k, cd_cmp_pe_v, cd_cmp_w1_v,
           cd_cmp_w2_v, cd_w_o, mixer_norm, moe_norm, router_group_w, router_group_b, router_expert_w,
           router_expert_b, expert_w_gate, expert_w_up, expert_w_down, ple_proj, ple_gate_norm, ple_gate_w,
           final_norm):
    B, S, D = x.shape
    T = B * S
    depth = p.shape[0]
    tabs = _rope_tables(positions)
    nc = S // D_CMP_STRIDE
    end = np.minimum(np.arange(nc) * D_CMP_STRIDE + D_CMP_LEN - 1, S - 1)
    cmp_tabs = [t.reshape(B, nc, LANES) for t in _rope_tables(positions[:, end])[:2]]
    h = x.reshape(T, D)
    for i in range(depth):
        j = i // 2
        if i % 2 == 0:
            h = _layer_even(h, B, S, tabs, mixer_norm[i], ab_w_in[j], ab_sinks[j], ab_q_lat_norm[j],
                            ab_kv_lat_norm[j], ab_w_uq[j], ab_w_ukv[j], ab_w_o[j])
        else:
            h = _layer_odd(h, B, S, tabs, cmp_tabs, mixer_norm[i], cd_w_in[j], cd_forget_bias[j],
                           cd_cmp_pe_k[j], cd_cmp_w1_k[j], cd_cmp_w2_k[j], cd_cmp_pe_v[j], cd_cmp_w1_v[j],
                           cd_cmp_w2_v[j], cd_w_o[j])
        h = _moe(h, moe_norm[i], router_group_w[i], router_group_b[i], router_expert_w[i], router_expert_b[i],
                 i, expert_w_gate, expert_w_up, expert_w_down)
        h = _mm([(h, 0, D)], [ple_gate_w[i].astype(BF16)], gain=ple_gate_norm[i], tm=512, tn=D,
                ple=(h, p[i].reshape(T, -1), ple_proj[i].astype(BF16)),
                out_gain=final_norm if i == depth - 1 else None, name="ple")
    return h.reshape(B, S, D)
```

```python
import functools
import math

import numpy as np
import jax
import jax.numpy as jnp
from jax import lax
from jax.experimental import pallas as pl
from jax.experimental.pallas import tpu as pltpu

F32 = jnp.float32
BF16 = jnp.bfloat16

HEAD_DIM = 128
ROPE_THETA = 10000.0
NORM_EPS = 1e-6
NEG_INF = -1e30
TAKEN = -3e38
A_HEADS, A_KV_HEADS, A_WINDOW = 8, 2, 128
B_HEADS, B_Q_LORA, B_KV_LORA, B_NOPE, B_ROPE, B_V = 8, 512, 256, 128, 64, 128
C_HEADS = 8
D_HEADS, D_KV_HEADS = 8, 2
D_CMP_LEN, D_CMP_STRIDE, D_SLC_LEN, D_SLC_TOPN, D_WINDOW = 32, 16, 64, 8, 512
FORCE_BONUS = 1e4
N_GROUPS, EXPERTS_PER_GROUP, TOP_K = 4, 8, 2
N_EXPERTS = N_GROUPS * EXPERTS_PER_GROUP

LANES = 128
VMEM_LIMIT_BYTES = 56 * 1024 * 1024
MOE_ROWS = 256
DMA_UNROLL = 16
FLASH_ROW_CHUNK = 128
LOG2E = math.log2(math.e)


def _params(*sem):
    return pltpu.CompilerParams(dimension_semantics=sem, vmem_limit_bytes=VMEM_LIMIT_BYTES)


def _tile(n, pref):
    t = min(n, pref)
    while n % t:
        t -= 1
    return t


def _pack_bf16_pairs(y):
    n = y.shape[1] // 2
    bits = lambda a: lax.bitcast_convert_type(a.astype(BF16).astype(F32), jnp.uint32)
    return (bits(y[:, :n]) >> 16) | (bits(y[:, n:]) & jnp.uint32(0xFFFF0000))


def _unpack_bf16_pairs(w):
    return (lax.bitcast_convert_type(w << 16, F32), lax.bitcast_convert_type(w & jnp.uint32(0xFFFF0000), F32))


def _rot128(x, c, s):
    return x * c + pltpu.roll(x, 64, 1) * s


def _rot_perms():
    i = np.arange(LANES)
    p128 = (i[:, None] == (i[None, :] + HEAD_DIM // 2) % LANES)
    half = B_ROPE // 2
    p64 = ((i[None, :] < half) & (i[:, None] == i[None, :] + half)) | (
        (i[None, :] >= half) & (i[None, :] < B_ROPE) & (i[:, None] == i[None, :] - half))
    return [jnp.asarray(p.astype(np.float32), BF16) for p in (p128, p64)]


def _tiles_store(ref, lead, words):
    rows, c = words.shape[0], words.shape[1] // LANES
    for j in range(c):
        ref[lead + (pl.ds(j, rows, stride=c), slice(None))] = words[:, j * LANES:(j + 1) * LANES]


def _tiles_load(ref, lead, rows, c):
    return jnp.concatenate([ref[lead + (pl.ds(j, rows, stride=c), slice(None))] for j in range(c)], axis=1)


def _mm_body(*refs, nx, has_gain, mode, resid_is_x, has_out_gain, col_prog, f32_from):
    it = iter(refs)
    x_refs = [next(it) for _ in range(nx)]
    g_ref = next(it) if has_gain else None
    w_refs = [next(it) for _ in range(nx)]
    r_ref = next(it) if mode in ("resid", "ple") and not resid_is_x else None
    p_ref = next(it) if mode == "ple" else None
    wp_ref = next(it) if mode == "ple" else None
    og_ref = next(it) if has_out_gain else None
    tab_refs = [next(it) for _ in range(7)] if col_prog is not None else None
    o_ref = next(it)
    of_ref = next(it) if f32_from is not None else None
    xn_ref = next(it) if has_gain else None

    if has_gain:
        @pl.when(pl.program_id(1) == 0)
        def _():
            x = x_refs[0][...].astype(F32)
            y = x * lax.rsqrt(jnp.mean(x * x, axis=-1, keepdims=True) + NORM_EPS) * g_ref[...]
            xn_ref[...] = y.astype(BF16)
        lhs = [xn_ref[...]]
    else:
        lhs = [x_ref[...].astype(BF16) for x_ref in x_refs]
    acc = None
    for a, w_ref in zip(lhs, w_refs):
        d = jnp.dot(a, w_ref[...], preferred_element_type=F32)
        acc = d if acc is None else acc + d
    if resid_is_x:
        r_ref = x_refs[0]
    if mode == "resid":
        acc = r_ref[...] + acc
    elif mode == "ple":
        pp = jnp.dot(p_ref[...].astype(BF16), wp_ref[...], preferred_element_type=F32)
        acc = r_ref[...] + pp * jax.nn.sigmoid(acc)
    if has_out_gain:
        acc = acc * lax.rsqrt(jnp.mean(acc * acc, axis=-1, keepdims=True) + NORM_EPS) * og_ref[...]
    if col_prog is None:
        o_ref[...] = acc.astype(o_ref.dtype)
        return
    j = pl.program_id(1)
    nb = o_ref.shape[1] // LANES
    tiles = {}
    for jj in range(len(col_prog) // nb):
        tiles.setdefault(tuple(col_prog[jj * nb:(jj + 1) * nb]), []).append(jj)
    for prog, jjs in tiles.items():
        @pl.when(functools.reduce(jnp.logical_or, [j == jj for jj in jjs]))
        def _():
            c128, s128, c64, s64a, s64b, p128, p64 = tab_refs
            for b, (kind, scale) in enumerate(prog):
                x = acc[:, b * LANES:(b + 1) * LANES]
                if kind == "rot128":
                    x = x * c128[...] + jnp.dot(x.astype(BF16), p128[...], preferred_element_type=F32) * s128[...]
                elif kind == "rot64":
                    x = x * c64[...] + jnp.dot(x.astype(BF16), p64[...], preferred_element_type=F32) * (
                        s64a[...] + s64b[...])
                if scale != 1.0:
                    x = x * scale
                o_ref[:, b * LANES:(b + 1) * LANES] = x.astype(o_ref.dtype)
    if f32_from is not None:
        @pl.when(j >= f32_from)
        def _():
            of_ref[...] = acc


def _mm(xs, ws, *, name, gain=None, out_dtype=F32, tm=1024, tn=512, resid=None, ple=None,
        out_gain=None, col_prog=None, tables=None, f32_from=None):
    M = xs[0][0].shape[0]
    N = ws[0].shape[1]
    tm, tn = _tile(M, tm), _tile(N, tn)
    nx = len(xs)
    has_gain = gain is not None
    mode = "ple" if ple is not None else ("resid" if resid is not None else "none")
    r_arr = ple[0] if mode == "ple" else resid
    resid_is_x = r_arr is xs[0][0] and tn == N == xs[0][2] and xs[0][1] == 0
    assert out_gain is None or tn == N
    args, in_specs = [], []
    for arr, cb, K in xs:
        args.append(arr)
        in_specs.append(pl.BlockSpec((tm, K), lambda i, j, cb=cb: (i, cb)))
    if has_gain:
        K0 = xs[0][2]
        args.append(gain.reshape(1, K0).astype(F32))
        in_specs.append(pl.BlockSpec((1, K0), lambda i, j: (0, 0)))
    for (arr, cb, K), w in zip(xs, ws):
        args.append(w)
        in_specs.append(pl.BlockSpec((K, tn), lambda i, j: (0, j)))
    if mode in ("resid", "ple") and not resid_is_x:
        args.append(r_arr)
        in_specs.append(pl.BlockSpec((tm, tn), lambda i, j: (i, j)))
    if mode == "ple":
        _, p, wp = ple
        args += [p, wp]
        in_specs += [pl.BlockSpec((tm, p.shape[1]), lambda i, j: (i, 0)),
                     pl.BlockSpec((p.shape[1], tn), lambda i, j: (0, j))]
    if out_gain is not None:
        args.append(out_gain.reshape(1, N).astype(F32))
        in_specs.append(pl.BlockSpec((1, N), lambda i, j: (0, 0)))
    if col_prog is not None:
        assert len(col_prog) * LANES == N
        args += list(tables) + _rot_perms()
        in_specs += [pl.BlockSpec((tm, LANES), lambda i, j: (i, 0))] * 5
        in_specs += [pl.BlockSpec((LANES, LANES), lambda i, j: (0, 0))] * 2
    out_shape = [jax.ShapeDtypeStruct((M, N), out_dtype)]
    out_specs = [pl.BlockSpec((tm, tn), lambda i, j: (i, j))]
    if f32_from is not None:
        out_shape.append(jax.ShapeDtypeStruct((M, N - f32_from * tn), F32))
        out_specs.append(pl.BlockSpec((tm, tn), lambda i, j: (i, jnp.maximum(j - f32_from, 0))))
    scratch = [pltpu.VMEM((tm, xs[0][2]), BF16)] if has_gain else []
    res = pl.pallas_call(
        functools.partial(_mm_body, nx=nx, has_gain=has_gain, mode=mode, resid_is_x=resid_is_x,
                          has_out_gain=out_gain is not None,
                          col_prog=None if col_prog is None else tuple(col_prog), f32_from=f32_from),
        out_shape=out_shape, grid=(M // tm, N // tn), in_specs=in_specs, out_specs=out_specs,
        scratch_shapes=scratch, compiler_params=_params("arbitrary", "arbitrary"), name=name,
    )(*args)
    return res if len(res) > 1 else res[0]


def _rope_tables(positions):
    def tables(dim):
        inv = 1.0 / (ROPE_THETA ** (jnp.arange(0, dim, 2, dtype=F32) / dim))
        ang = positions.astype(F32)[..., None] * inv
        return jnp.cos(ang), jnp.sin(ang)

    lead = positions.shape
    c, s = tables(HEAD_DIM)
    c128 = jnp.concatenate([c, c], -1)
    s128 = jnp.concatenate([-s, s], -1)
    c, s = tables(B_ROPE)
    z = jnp.zeros_like(c)
    c64 = jnp.concatenate([c, c, z, z], -1)
    s64a = jnp.concatenate([-s, z, z, z], -1)
    s64b = jnp.concatenate([z, s, z, z], -1)
    n = int(np.prod(lead))
    return [t.reshape(n, LANES) for t in (c128, s128, c64, s64a, s64b)]


def _pair_tables(S, tq, tk, window):
    qi_l, ki_l, fl_l, mk_l, masks, ids = [], [], [], [], [], {}
    for qi in range(S // tq):
        q0, q1 = qi * tq, (qi + 1) * tq - 1
        ks = []
        for ki in range(S // tk):
            k0, k1 = ki * tk, (ki + 1) * tk - 1
            if k0 > q1 or (window is not None and q0 - k1 >= window):
                continue
            full = k1 <= q0 and (window is None or q1 - k0 < window)
            mid = 0
            if not full:
                delta = q0 - k0
                if delta not in ids:
                    rel = np.arange(tq)[:, None] + delta - np.arange(tk)[None, :]
                    ok = (rel >= 0) if window is None else ((rel >= 0) & (rel < window))
                    masks.append(np.where(ok, 0.0, NEG_INF).astype(np.float32))
                    ids[delta] = len(masks)
                mid = ids[delta]
            ks.append((ki, mid))
        for n, (ki, mid) in enumerate(ks):
            qi_l.append(qi)
            ki_l.append(ki)
            fl_l.append((1 if n == 0 else 0) | (2 if n == len(ks) - 1 else 0))
            mk_l.append(mid)
    if not masks:
        masks.append(np.zeros((tq, tk), np.float32))
    tabs = [jnp.asarray(np.array(a, np.int32)) for a in (qi_l, ki_l, fl_l, mk_l)]
    return tabs, jnp.asarray(np.stack(masks))


def _flash_body(qi_t, ki_t, fl_t, mk_t, *refs, nq, nkv, q_src, k_src, v_src, tq, rc, causal_diag,
                has_sink, has_bias, has_sel):
    nc = len(q_src)
    it = iter(refs)
    q_refs = [next(it) for _ in range(nq)]
    kv_refs = [next(it) for _ in range(nkv)]
    mask_ref = next(it)
    sink_ref = next(it) if has_sink else None
    cum_ref = next(it) if has_bias else None
    cumt_ref = next(it) if has_bias else None
    sel_ref = next(it) if has_sel else None
    exp_ref = next(it) if has_sel else None
    o_ref = next(it)
    m_sc, acc_sc = next(it), next(it)
    cq_sc = next(it) if has_bias else None

    step_id = pl.program_id(2)
    fl, mk = fl_t[step_id], mk_t[step_id]

    def cat(refs_, src, rows=slice(None)):
        xs = [refs_[pi][0, rows, off:off + LANES] for pi, off in src]
        return xs[0] if len(xs) == 1 else jnp.concatenate(xs, axis=-1)

    @pl.when((fl & 1) != 0)
    def _init():
        m_sc[...] = jnp.full(m_sc.shape, NEG_INF, F32)
        acc_sc[...] = jnp.zeros(acc_sc.shape, F32)
        if has_bias:
            lane = lax.broadcasted_iota(jnp.int32, (tq, LANES), 1)
            for g in range(nc):
                head = pl.program_id(1) * nc + g
                cq_sc[g] = jnp.sum(jnp.where(lane == head, cum_ref[0], 0.0), axis=-1, keepdims=True)

    tk = kv_refs[0].shape[1]
    ones = jnp.ones((tk, LANES), BF16)

    def step(masked):
        ks = [cat(kv_refs, k_src[g]) for g in range(nc)]
        vs = [jnp.concatenate([kv_refs[pi][0, :, off:off + LANES], ones], axis=1) for pi, off in v_src]
        for r in range(tq // rc):
            rows = slice(r * rc, (r + 1) * rc)
            nk = min((r + 1) * rc, tk) if (masked and causal_diag) else tk
            reps = nk // LANES
            add = mask_ref[mk - 1, rows, :nk] if masked else None
            if has_sel:
                hidden = (jnp.dot(sel_ref[0, 0, rows, :], exp_ref[0, :, :nk], preferred_element_type=F32)
                          - 1.0) * (-NEG_INF)
                add = hidden if add is None else add + hidden
            for g in range(nc):
                q = cat(q_refs, q_src[g], rows)
                s = lax.dot_general(q, ks[g][:nk], (((1,), (1,)), ((), ())), preferred_element_type=F32)
                if has_bias:
                    s = s + (cq_sc[g, rows, :] - cumt_ref[0, g, :, :nk])
                if add is not None:
                    s = s + add
                m_prev = m_sc[g, rows, :]
                m_new = jnp.maximum(m_prev, jnp.max(s, axis=-1, keepdims=True))
                p = jnp.exp2(s - (jnp.concatenate([m_new] * reps, axis=1) if reps > 1 else m_new))
                alpha = jnp.exp2(m_prev - m_new)
                acc_sc[g, rows, :] = (jnp.concatenate([alpha, alpha], axis=1) * acc_sc[g, rows, :]
                                      + jnp.dot(p.astype(BF16), vs[g][:nk], preferred_element_type=F32))
                m_sc[g, rows, :] = m_new

    @pl.when(mk != 0)
    def _():
        step(True)

    @pl.when(mk == 0)
    def _():
        step(False)

    @pl.when((fl & 2) != 0)
    def _finish():
        for g in range(nc):
            m, acc, l = m_sc[g], acc_sc[g, :, :LANES], acc_sc[g, :, LANES:]
            if has_sink:
                sk = sink_ref[0, g:g + 1, 0:1] * LOG2E
                m_f = jnp.maximum(m, sk)
                w = jnp.exp2(m - m_f)
                l = l * w + jnp.exp2(sk - m_f)
                acc = acc * w
            o_ref[0, :, g * LANES:(g + 1) * LANES] = (acc / l).astype(o_ref.dtype)


def _flash(q_parts, kv_parts, q_src, k_src, v_src, *, name, B, S, n_steps, tq, tk, window=None, sinks=None,
           cum=None, cumt=None, sel=None, out_dtype=BF16, rc=FLASH_ROW_CHUNK):
    nc = len(q_src)
    tq, tk = _tile(S, tq), _tile(S, tk)
    tabs, masks = _pair_tables(S, tq, tk, window)
    npairs = int(tabs[0].shape[0])
    args, in_specs = [], []
    for arr, width, cf in q_parts:
        args.append(arr)
        in_specs.append(pl.BlockSpec((1, tq, width), lambda b, h, s, qt, kt, ft, mt, cf=cf: (b, qt[s], cf(h))))
    for arr, width, cf in kv_parts:
        args.append(arr)
        in_specs.append(pl.BlockSpec((1, tk, width), lambda b, h, s, qt, kt, ft, mt, cf=cf: (b, kt[s], cf(h))))
    args.append(masks)
    in_specs.append(pl.BlockSpec(masks.shape, lambda b, h, s, qt, kt, ft, mt: (0, 0, 0)))
    if sinks is not None:
        args.append(jnp.broadcast_to(sinks.astype(F32).reshape(n_steps, nc, 1), (n_steps, nc, LANES)))
        in_specs.append(pl.BlockSpec((1, nc, LANES), lambda b, h, s, qt, kt, ft, mt: (h, 0, 0)))
    if cum is not None:
        args += [cum, cumt]
        in_specs += [pl.BlockSpec((1, tq, LANES), lambda b, h, s, qt, kt, ft, mt: (b, qt[s], 0)),
                     pl.BlockSpec((1, nc, 1, tk), lambda b, h, s, qt, kt, ft, mt: (b, h, 0, kt[s]))]
    if sel is not None:
        per = tk // D_SLC_LEN
        e = np.zeros((S // tk, LANES, tk), np.float32)
        for ki in range(S // tk):
            e[ki, ki * per + np.arange(tk) // D_SLC_LEN, np.arange(tk)] = 1.0
        args += [sel, jnp.asarray(e, BF16)]
        in_specs += [pl.BlockSpec((1, 1, tq, LANES), lambda b, h, s, qt, kt, ft, mt: (b, h, qt[s], 0)),
                     pl.BlockSpec((1, LANES, tk), lambda b, h, s, qt, kt, ft, mt: (kt[s], 0, 0))]
    scratch = [pltpu.VMEM((nc, tq, LANES), F32), pltpu.VMEM((nc, tq, 2 * LANES), F32)]
    if cum is not None:
        scratch.append(pltpu.VMEM((nc, tq, 1), F32))
    body = functools.partial(_flash_body, nq=len(q_parts), nkv=len(kv_parts), q_src=q_src, k_src=k_src,
                             v_src=v_src, tq=tq, rc=_tile(tq, rc), causal_diag=window is None and tq == tk,
                             has_sink=sinks is not None,
                             has_bias=cum is not None, has_sel=sel is not None)
    return pl.pallas_call(
        body, out_shape=jax.ShapeDtypeStruct((B, S, n_steps * nc * LANES), out_dtype),
        grid_spec=pltpu.PrefetchScalarGridSpec(
            num_scalar_prefetch=4, grid=(B, n_steps, npairs), in_specs=in_specs,
            out_specs=pl.BlockSpec((1, tq, nc * LANES), lambda b, h, s, qt, kt, ft, mt: (b, qt[s], h)),
            scratch_shapes=scratch),
        compiler_params=_params("arbitrary", "arbitrary", "arbitrary"), name=name,
    )(*tabs, *args)


def _gqa_flash_all(q, k, v, *, name, B, S, Hk, G, **kw):
    kvw = Hk * LANES
    return _flash([(q[0], G * kvw, lambda h: q[1] * LANES // (G * kvw))],
                  [(k[0], kvw, lambda h: k[1] * LANES // kvw), (v[0], kvw, lambda h: v[1] * LANES // kvw)],
                  [[(0, g * LANES)] for g in range(Hk * G)], [[(0, g // G * LANES)] for g in range(Hk * G)],
                  [(1, g // G * LANES) for g in range(Hk * G)], name=name, B=B, S=S, n_steps=1, **kw)


def _gqa_flash(q, k, v, *, name, B, S, Hk, G, **kw):
    at = lambda first: (lambda h: first + h)
    return _flash([(q[0], G * LANES, at(q[1]))], [(k[0], LANES, at(k[1])), (v[0], LANES, at(v[1]))],
                  [[(0, g * LANES)] for g in range(G)], [[(0, 0)]] * G, [(1, 0)] * G,
                  name=name, B=B, S=S, n_steps=Hk, **kw)


def _cum_body(y_ref, b_ref, tri_ref, o_ref, carry):
    @pl.when(pl.program_id(1) == 0)
    def _():
        carry[...] = jnp.zeros(carry.shape, F32)
    x = y_ref[0] + b_ref[...]
    logf = jnp.minimum(x, 0.0) - jnp.log1p(jnp.exp(-jnp.abs(x)))
    cum = jnp.dot(tri_ref[...], logf, preferred_element_type=F32, precision=lax.Precision.HIGHEST) + carry[...]
    o_ref[0] = cum * LOG2E
    carry[...] = cum[-1:, :]


def _forget_cum(y3, col_block, bias_row, *, ts=512):
    B, S, _ = y3.shape
    ts = _tile(S, ts)
    tri = jnp.asarray(np.tril(np.ones((ts, ts), np.float32)))
    return pl.pallas_call(
        _cum_body, out_shape=jax.ShapeDtypeStruct((B, S, LANES), F32), grid=(B, S // ts),
        in_specs=[pl.BlockSpec((1, ts, LANES), lambda b, s: (b, s, col_block)),
                  pl.BlockSpec((1, LANES), lambda b, s: (0, 0)),
                  pl.BlockSpec((ts, ts), lambda b, s: (0, 0))],
        out_specs=pl.BlockSpec((1, ts, LANES), lambda b, s: (b, s, 0)),
        scratch_shapes=[pltpu.VMEM((1, LANES), F32)],
        compiler_params=_params("arbitrary", "arbitrary"), name="forget_cum",
    )(y3, bias_row, tri)


def _compress_body(*refs, rope, nc, width, col0):
    if rope:
        z_ref, pe_ref, w1_ref, w2_ref, c_ref, s_ref, o_ref = refs
    else:
        z_ref, pe_ref, w1_ref, w2_ref, o_ref = refs
    half = D_CMP_LEN // 2
    for hk in range(D_KV_HEADS):
        u = jnp.zeros((nc, w1_ref.shape[1]), F32)
        v = jnp.zeros((nc, w1_ref.shape[1]), F32)
        for l in range(half):
            first = l * width + col0 + hk * HEAD_DIM
            z = z_ref[0, :, first:first + HEAD_DIM]
            zu = (z + pe_ref[l:l + 1, :]).astype(BF16)
            zv = (z + pe_ref[half + l:half + l + 1, :]).astype(BF16)
            u = u + jnp.dot(zu, w1_ref[l * HEAD_DIM:(l + 1) * HEAD_DIM, :], preferred_element_type=F32)
            v = v + jnp.dot(zv, w1_ref[(half + l) * HEAD_DIM:(half + l + 1) * HEAD_DIM, :],
                            preferred_element_type=F32)
        pre = u + pltpu.roll(v, nc - 1, 0)
        hid = jax.nn.gelu(pre, approximate=True)
        out = jnp.dot(hid.astype(BF16), w2_ref[...], preferred_element_type=F32)
        if rope:
            out = _rot128(out, c_ref[0], s_ref[0])
        o_ref[0, hk] = out.astype(o_ref.dtype)


def _compress(z, col0, pe, w1, w2, rope_tabs=None):
    B, S, W = z.shape
    nc = S // D_CMP_STRIDE
    zc = z.reshape(B, nc, D_CMP_STRIDE * W)
    args = [zc, pe.astype(F32), w1.astype(BF16), w2.astype(BF16)]
    in_specs = [pl.BlockSpec((1, nc, D_CMP_STRIDE * W), lambda b: (b, 0, 0)),
                pl.BlockSpec(pe.shape, lambda b: (0, 0)),
                pl.BlockSpec(w1.shape, lambda b: (0, 0)),
                pl.BlockSpec(w2.shape, lambda b: (0, 0))]
    if rope_tabs is not None:
        args += list(rope_tabs)
        in_specs += [pl.BlockSpec((1, nc, LANES), lambda b: (b, 0, 0))] * 2
    return pl.pallas_call(
        functools.partial(_compress_body, rope=rope_tabs is not None, nc=nc, width=W, col0=col0),
        out_shape=jax.ShapeDtypeStruct((B, D_KV_HEADS, nc, HEAD_DIM), BF16), grid=(B,),
        in_specs=in_specs, out_specs=pl.BlockSpec((1, D_KV_HEADS, nc, HEAD_DIM), lambda b: (b, 0, 0, 0)),
        compiler_params=_params("arbitrary"), name="nsa_compress",
    )(*args)


def _cmp_attn_body(q_ref, k_ref, v_ref, ov_ref, o_ref, sel_ref, *, G, tq, nc, n_cmp, n_slc, topn):
    qi = pl.program_id(2)
    t = qi * tq + lax.broadcasted_iota(jnp.int32, (tq, nc), 0)
    c = lax.broadcasted_iota(jnp.int32, (tq, nc), 1)
    valid = (c * D_CMP_STRIDE + (D_CMP_LEN - 1) <= t) & (c < n_cmp)
    k = k_ref[0, 0]
    v = v_ref[0, 0]
    psum = jnp.zeros((tq, nc), F32)
    for g in range(G):
        q = q_ref[0, :, g * LANES:(g + 1) * LANES]
        s = lax.dot_general(q, k, (((1,), (1,)), ((), ())), preferred_element_type=F32)
        s = jnp.where(valid, s, NEG_INF)
        e = jnp.where(valid, jnp.exp2(s - jnp.max(s, axis=-1, keepdims=True)), 0.0)
        p = e / jnp.maximum(jnp.sum(e, axis=-1, keepdims=True), jnp.finfo(F32).tiny)
        o_ref[0, :, g * LANES:(g + 1) * LANES] = jnp.dot(
            p.astype(BF16), v, preferred_element_type=F32).astype(o_ref.dtype)
        psum = psum + p
    imp = lax.dot_general(ov_ref[...], psum, (((1,), (1,)), ((), ())), preferred_element_type=F32,
                          precision=lax.Precision.HIGHEST)
    blk = lax.broadcasted_iota(jnp.int32, (LANES, tq), 0)
    tcol = qi * tq + lax.broadcasted_iota(jnp.int32, (LANES, tq), 1)
    cur = jnp.right_shift(tcol, int(math.log2(D_SLC_LEN)))
    forced = (blk == 0) | (blk == cur) | (blk == cur - 1)
    imp = jnp.where(blk * D_SLC_LEN > tcol, NEG_INF, imp + jnp.where(forced, FORCE_BONUS, 0.0))
    imp = jnp.where(blk >= n_slc, TAKEN, imp)
    chosen = jnp.zeros((LANES, tq), F32)
    blk_f = blk.astype(F32)
    for _ in range(topn):
        mx = jnp.max(imp, axis=0, keepdims=True)
        idx = jnp.min(jnp.where(imp == mx, blk_f, float(LANES)), axis=0, keepdims=True)
        hit = blk_f == idx
        chosen = jnp.where(hit, 1.0, chosen)
        imp = jnp.where(hit, TAKEN, imp)
    sel_ref[0, 0] = chosen.T.astype(sel_ref.dtype)


def _cmp_attn(q, q_first, k_cmp, v_cmp, *, tq=1024):
    B, S, _ = q.shape
    Hk, G = D_KV_HEADS, D_HEADS // D_KV_HEADS
    nc = S // D_CMP_STRIDE
    n_cmp = (S - D_CMP_LEN) // D_CMP_STRIDE + 1
    n_slc = S // D_SLC_LEN
    tq = _tile(S, tq)
    c0 = np.arange(nc) * D_CMP_STRIDE
    s0 = np.arange(LANES) * D_SLC_LEN
    ov = ((c0[:, None] < (s0 + D_SLC_LEN)[None, :]) & ((c0 + D_CMP_LEN)[:, None] > s0[None, :])
          & (np.arange(nc) < n_cmp)[:, None] & (np.arange(LANES) < n_slc)[None, :]).astype(np.float32)
    body = functools.partial(_cmp_attn_body, G=G, tq=tq, nc=nc, n_cmp=n_cmp, n_slc=n_slc,
                             topn=min(D_SLC_TOPN, n_slc))
    return pl.pallas_call(
        body,
        out_shape=[jax.ShapeDtypeStruct((B, S, Hk * G * LANES), BF16),
                   jax.ShapeDtypeStruct((B, Hk, S, LANES), BF16)],
        grid=(B, Hk, S // tq),
        in_specs=[pl.BlockSpec((1, tq, G * LANES), lambda b, h, i: (b, i, q_first + h)),
                  pl.BlockSpec((1, 1, nc, LANES), lambda b, h, i: (b, h, 0, 0)),
                  pl.BlockSpec((1, 1, nc, LANES), lambda b, h, i: (b, h, 0, 0)),
                  pl.BlockSpec((LANES, nc), lambda b, h, i: (0, 0))],
        out_specs=[pl.BlockSpec((1, tq, G * LANES), lambda b, h, i: (b, i, h)),
                   pl.BlockSpec((1, 1, tq, LANES), lambda b, h, i: (b, h, i, 0))],
        compiler_params=_params("arbitrary", "arbitrary", "arbitrary"), name="nsa_cmp_attn",
    )(q, k_cmp, v_cmp, jnp.asarray(ov.T))


def _gate_body(y_ref, a_ref, b_ref, c_ref, o_ref, *, lane0):
    g = jax.nn.sigmoid(y_ref[...])
    for h in range(D_HEADS):
        cols = slice(h * LANES, (h + 1) * LANES)
        ga = g[:, lane0 + h:lane0 + h + 1]
        gb = g[:, lane0 + D_HEADS + h:lane0 + D_HEADS + h + 1]
        gc = g[:, lane0 + 2 * D_HEADS + h:lane0 + 2 * D_HEADS + h + 1]
        o_ref[:, cols] = (ga * a_ref[:, cols] + gb * b_ref[:, cols] + gc * c_ref[:, cols]).astype(o_ref.dtype)


def _nsa_gate(y, col_block, lane0, o_cmp, o_slc, o_win, *, tm=512):
    M = y.shape[0]
    W = o_cmp.shape[1]
    tm = _tile(M, tm)
    row = lambda i: (i, 0)
    return pl.pallas_call(
        functools.partial(_gate_body, lane0=lane0), out_shape=jax.ShapeDtypeStruct((M, W), BF16),
        grid=(M // tm,),
        in_specs=[pl.BlockSpec((tm, LANES), lambda i: (i, col_block))] + [pl.BlockSpec((tm, W), row)] * 3,
        out_specs=pl.BlockSpec((tm, W), row), compiler_params=_params("arbitrary"), name="nsa_gate",
    )(y, o_cmp, o_slc, o_win)


def _route_body(h_ref, g_ref, wr_ref, b_ref, tri_ref, e_ref, w_ref, pos_ref, cnt_ref, xo_ref, carry, *, tm):
    @pl.when(pl.program_id(0) == 0)
    def _():
        carry[...] = jnp.zeros(carry.shape, F32)
    x = h_ref[...]
    y = x * lax.rsqrt(jnp.mean(x * x, axis=-1, keepdims=True) + NORM_EPS) * g_ref[...]
    _tiles_store(xo_ref, (), _pack_bf16_pairs(y))
    lane = lax.broadcasted_iota(jnp.int32, (tm, LANES), 1)
    logits = jnp.dot(y.astype(BF16), wr_ref[...], preferred_element_type=F32) + b_ref[...]
    gl = jnp.where(lane < N_GROUPS, logits, -jnp.inf)
    gmax = jnp.max(gl, axis=-1, keepdims=True)
    g_val = 1.0 / jnp.sum(jnp.exp(gl - gmax), axis=-1, keepdims=True)
    g_idx = jnp.min(jnp.where(gl == gmax, lane, LANES), axis=-1, keepdims=True)
    lo = N_GROUPS + EXPERTS_PER_GROUP * g_idx
    el = jnp.where((lane >= lo) & (lane < lo + EXPERTS_PER_GROUP), logits, -jnp.inf)
    e1 = jnp.max(el, axis=-1, keepdims=True)
    i1 = jnp.min(jnp.where(el == e1, lane, LANES), axis=-1, keepdims=True)
    el2 = jnp.where(lane == i1, -jnp.inf, el)
    e2 = jnp.max(el2, axis=-1, keepdims=True)
    i2 = jnp.min(jnp.where(el2 == e2, lane, LANES), axis=-1, keepdims=True)
    r = jnp.exp(e2 - e1)
    w1 = g_val / (1.0 + r)
    w2 = w1 * r
    x1, x2 = i1 - N_GROUPS, i2 - N_GROUPS
    e_ref[...] = jnp.where(lane == 0, x1, jnp.where(lane == 1, x2, 0))
    w_ref[...] = jnp.where(lane == 0, w1, jnp.where(lane == 1, w2, 0.0))
    hot1 = lane == x1
    hot2 = lane == x2
    both = jnp.where(hot1 | hot2, 1.0, 0.0)
    before = jnp.dot(tri_ref[...], both.astype(BF16), preferred_element_type=F32) + carry[...]
    p1 = jnp.sum(jnp.where(hot1, before, 0.0), axis=-1, keepdims=True)
    p2 = jnp.sum(jnp.where(hot2, before, 0.0), axis=-1, keepdims=True)
    pos_ref[...] = jnp.where(lane == 0, p1, jnp.where(lane == 1, p2, 0.0)).astype(jnp.int32)
    carry[...] = carry[...] + jnp.sum(both, axis=0, keepdims=True)
    cnt_ref[...] = carry[...].astype(jnp.int32)


def _route(h, gain, w_router, bias_row, *, tm=512):
    T, D = h.shape
    c = D // 2 // LANES
    tm = _tile(T, tm)
    tri = jnp.asarray(np.tril(np.ones((tm, tm), np.float32), -1), BF16)
    row = lambda i: (i, 0)
    fixed = lambda i: (0, 0)
    return pl.pallas_call(
        functools.partial(_route_body, tm=tm),
        out_shape=[jax.ShapeDtypeStruct((T, LANES), jnp.int32), jax.ShapeDtypeStruct((T, LANES), F32),
                   jax.ShapeDtypeStruct((T, LANES), jnp.int32), jax.ShapeDtypeStruct((1, LANES), jnp.int32),
                   jax.ShapeDtypeStruct((T * c, LANES), jnp.uint32)],
        grid=(T // tm,),
        in_specs=[pl.BlockSpec((tm, D), row), pl.BlockSpec((1, D), fixed), pl.BlockSpec((D, LANES), fixed),
                  pl.BlockSpec((1, LANES), fixed), pl.BlockSpec((tm, tm), fixed)],
        out_specs=[pl.BlockSpec((tm, LANES), row)] * 3 + [pl.BlockSpec((1, LANES), fixed),
                                                          pl.BlockSpec((tm * c, LANES), row)],
        scratch_shapes=[pltpu.VMEM((1, LANES), F32)],
        compiler_params=_params("arbitrary"), name="moe_route",
    )(h, gain.reshape(1, D).astype(F32), w_router, bias_row, tri)


def _dispatch_body(d_ref, ends_ref, pad_ref, nu_ref, x_ref, out_ref, zero_sc, sem, zsem, *, tm, nblk, c):
    base = pl.program_id(0) * tm

    @pl.when(pl.program_id(0) == 0)
    def _():
        zero_sc[...] = jnp.zeros(zero_sc.shape, zero_sc.dtype)

        def zero_block(first_row):
            return pltpu.make_async_copy(
                zero_sc, out_ref.at[pl.ds(pl.multiple_of(first_row * c, MOE_ROWS * c), MOE_ROWS * c)], zsem)

        for wait in (False, True):
            for e in range(N_EXPERTS):
                for live, first_row in ((pad_ref[e] > 0, ends_ref[e] - MOE_ROWS),
                                        (nu_ref[0] + e < nblk, (nu_ref[0] + e) * MOE_ROWS)):
                    @pl.when(live)
                    def _():
                        zero_block(first_row).wait() if wait else zero_block(first_row).start()

    def issue(r, carry):
        for k in range(TOP_K):
            row = d_ref[(base + r) * TOP_K + k]
            pltpu.make_async_copy(x_ref.at[pl.ds(pl.multiple_of(r * c, c), c)],
                                  out_ref.at[pl.ds(pl.multiple_of(row * c, c), c)], sem).start(priority=k)
        return carry

    lax.fori_loop(0, tm, issue, 0, unroll=DMA_UNROLL)
    for _ in range(TOP_K):
        pltpu.make_async_copy(x_ref, out_ref.at[pl.ds(0, tm * c)], sem).wait()


def _dispatch(xn, dest, ends, padded, n_used, n_rows, c, *, tm=512):
    T = xn.shape[0] // c
    tm = _tile(T, tm)
    return pl.pallas_call(
        functools.partial(_dispatch_body, tm=tm, nblk=n_rows // MOE_ROWS, c=c),
        out_shape=jax.ShapeDtypeStruct((n_rows * c, LANES), xn.dtype),
        grid_spec=pltpu.PrefetchScalarGridSpec(
            num_scalar_prefetch=4, grid=(T // tm,),
            in_specs=[pl.BlockSpec((tm * c, LANES), lambda i, *_: (i, 0))],
            out_specs=pl.BlockSpec(memory_space=pl.ANY),
            scratch_shapes=[pltpu.VMEM((MOE_ROWS * c, LANES), xn.dtype), pltpu.SemaphoreType.DMA(()),
                            pltpu.SemaphoreType.DMA(())]),
        compiler_params=pltpu.CompilerParams(dimension_semantics=("arbitrary",), has_side_effects=True,
                                             vmem_limit_bytes=VMEM_LIMIT_BYTES, disable_bounds_checks=True),
        name="moe_dispatch",
    )(dest, ends, padded, n_used, xn)


def _expert_body(nb_ref, first_ref, nu_ref, xb_ref, wg_ref, wu_ref, wd_ref, yb_ref, wg_sc, wu_sc, wd_sc, xbuf,
                 obuf, sem_in, sem_out, state, *, nblk, c):
    e = pl.program_id(0)
    last = pl.num_programs(0) - 1
    nb, first = nb_ref[e], first_ref[e]

    @pl.when(e == 0)
    def _():
        for i in range(3):
            state[i] = 0

    def rows(first_block, b):
        return pl.ds(pl.multiple_of((first_block + b) * (MOE_ROWS * c), MOE_ROWS * c), MOE_ROWS * c)

    def fetch(first_block, b, slot):
        return pltpu.make_async_copy(xb_ref.at[rows(first_block, b)], xbuf.at[slot], sem_in.at[slot])

    def flush(b, slot):
        return pltpu.make_async_copy(obuf.at[slot], yb_ref.at[rows(first, b)], sem_out.at[slot])

    def drain(slot):
        @pl.when(state[slot] != 0)
        def _():
            pltpu.make_async_copy(obuf.at[slot], yb_ref.at[pl.ds(0, MOE_ROWS * c)], sem_out.at[slot]).wait()
            state[slot] = 0

    @pl.when(nb > 0)
    def _():
        @pl.when(state[2] == 0)
        def _():
            fetch(first, 0, 0).start()

        state[2] = 0
        wg_sc[...] = wg_ref[0, 0].astype(BF16)
        wu_sc[...] = wu_ref[0, 0].astype(BF16)
        wd_sc[...] = wd_ref[0, 0].astype(BF16)

        def block(b, carry):
            slot = lax.rem(b, 2)

            @pl.when(b + 1 < nb)
            def _():
                fetch(first, b + 1, 1 - slot).start()

            fetch(first, b, slot).wait()
            drain(slot)
            x = jnp.concatenate(_unpack_bf16_pairs(_tiles_load(xbuf, (slot,), MOE_ROWS, c)), axis=1).astype(BF16)
            gate = jnp.dot(x, wg_sc[...], preferred_element_type=F32)
            up = jnp.dot(x, wu_sc[...], preferred_element_type=F32)
            hid = (gate * jax.nn.sigmoid(gate) * up).astype(BF16)
            _tiles_store(obuf, (slot,), _pack_bf16_pairs(jnp.dot(hid, wd_sc[...], preferred_element_type=F32)))
            flush(b, slot).start()
            state[slot] = 1
            return carry

        lax.fori_loop(0, nb, block, 0)
        nxt = jnp.minimum(e + 1, last)

        @pl.when((e < last) & (nb_ref[nxt] > 0))
        def _():
            fetch(first_ref[nxt], 0, 0).start()
            state[2] = 1

    @pl.when(e == last)
    def _():
        drain(0)
        drain(1)
        obuf[0] = jnp.zeros(obuf.shape[1:], obuf.dtype)

        def tail(t):
            return pltpu.make_async_copy(obuf.at[0], yb_ref.at[rows(nu_ref[0], t)], sem_out.at[0])

        for wait in (False, True):
            for t in range(N_EXPERTS):
                @pl.when(nu_ref[0] + t < nblk)
                def _():
                    tail(t).wait() if wait else tail(t).start()


def _experts(xb, n_blocks, first_block, n_used, layer, w_gate, w_up, w_down):
    D, Hd = w_gate.shape[2], w_gate.shape[3]
    c = D // 2 // LANES
    nblk = xb.shape[0] // (MOE_ROWS * c)
    wspec = lambda shape: pl.BlockSpec((1, 1) + shape, lambda e, *_: (layer, e, 0, 0))
    return pl.pallas_call(
        functools.partial(_expert_body, nblk=nblk, c=c), out_shape=jax.ShapeDtypeStruct(xb.shape, xb.dtype),
        grid_spec=pltpu.PrefetchScalarGridSpec(
            num_scalar_prefetch=3, grid=(w_gate.shape[1],),
            in_specs=[pl.BlockSpec(memory_space=pl.ANY), wspec((D, Hd)), wspec((D, Hd)), wspec((Hd, D))],
            out_specs=pl.BlockSpec(memory_space=pl.ANY),
            scratch_shapes=[pltpu.VMEM((D, Hd), BF16), pltpu.VMEM((D, Hd), BF16), pltpu.VMEM((Hd, D), BF16),
                            pltpu.VMEM((2, MOE_ROWS * c, LANES), xb.dtype),
                            pltpu.VMEM((2, MOE_ROWS * c, LANES), xb.dtype),
                            pltpu.SemaphoreType.DMA((2,)), pltpu.SemaphoreType.DMA((2,)),
                            pltpu.SMEM((3,), jnp.int32)]),
        compiler_params=pltpu.CompilerParams(dimension_semantics=("arbitrary",), has_side_effects=True,
                                             vmem_limit_bytes=VMEM_LIMIT_BYTES),
        name="moe_experts",
    )(n_blocks, first_block, n_used, xb, w_gate, w_up, w_down)


def _collect_body(d_ref, h_ref, w_ref, yb_ref, o_ref, buf_a, buf_b, sem, *, tm, nsteps, c):
    i = pl.program_id(0)
    slot = lax.rem(i, 2)

    def fetch(step, slot_):
        base = step * tm

        def issue(r, carry):
            for k, buf in enumerate((buf_a, buf_b)):
                row = d_ref[(base + r) * TOP_K + k]
                pltpu.make_async_copy(yb_ref.at[pl.ds(pl.multiple_of(row * c, c), c)],
                                      buf.at[slot_, pl.ds(pl.multiple_of(r * c, c), c)],
                                      sem.at[slot_]).start(priority=k)
            return carry

        lax.fori_loop(0, tm, issue, 0, unroll=DMA_UNROLL)

    @pl.when(i == 0)
    def _():
        fetch(0, 0)

    @pl.when(i + 1 < nsteps)
    def _():
        fetch(i + 1, 1 - slot)

    for buf in (buf_a, buf_b):
        pltpu.make_async_copy(yb_ref.at[pl.ds(0, tm * c)], buf.at[slot], sem.at[slot]).wait()
    w = w_ref[...]
    half = h_ref.shape[1] // 2
    for part, ya, yb in zip((slice(0, half), slice(half, None)),
                            _unpack_bf16_pairs(_tiles_load(buf_a, (slot,), tm, c)),
                            _unpack_bf16_pairs(_tiles_load(buf_b, (slot,), tm, c))):
        o_ref[:, part] = h_ref[:, part] + w[:, 0:1] * ya + w[:, 1:2] * yb


def _collect(h, yb, dest, wts, *, tm=256):
    T, D = h.shape
    c = D // 2 // LANES
    tm = _tile(T, tm)
    nsteps = T // tm
    return pl.pallas_call(
        functools.partial(_collect_body, tm=tm, nsteps=nsteps, c=c),
        out_shape=jax.ShapeDtypeStruct((T, D), F32),
        grid_spec=pltpu.PrefetchScalarGridSpec(
            num_scalar_prefetch=1, grid=(nsteps,),
            in_specs=[pl.BlockSpec((tm, D), lambda i, d: (i, 0)), pl.BlockSpec((tm, LANES), lambda i, d: (i, 0)),
                      pl.BlockSpec(memory_space=pl.ANY)],
            out_specs=pl.BlockSpec((tm, D), lambda i, d: (i, 0)),
            scratch_shapes=[pltpu.VMEM((2, tm * c, LANES), yb.dtype), pltpu.VMEM((2, tm * c, LANES), yb.dtype),
                            pltpu.SemaphoreType.DMA((2,))]),
        compiler_params=pltpu.CompilerParams(dimension_semantics=("arbitrary",), disable_bounds_checks=True,
                                             vmem_limit_bytes=VMEM_LIMIT_BYTES),
        name="moe_collect",
    )(dest, h, wts, yb)


def _moe(h, norm_g, w_group, b_group, w_expert, b_expert, layer, w_gate, w_up, w_down):
    T, D = h.shape
    pad = LANES - N_GROUPS - N_EXPERTS
    w_r = jnp.concatenate([w_group, w_expert, jnp.zeros((D, pad), F32)], axis=1).astype(BF16)
    b_r = jnp.concatenate([b_group, b_expert, jnp.zeros((pad,), F32)]).astype(F32).reshape(1, LANES)
    eid, wts, pos, cnt, xn = _route(h, norm_g, w_r, b_r)
    counts = cnt[0, :N_EXPERTS]
    padded = (counts + MOE_ROWS - 1) // MOE_ROWS * MOE_ROWS
    ends = jnp.cumsum(padded)
    offs = ends - padded
    dest = (offs[eid[:, :TOP_K]] + pos[:, :TOP_K]).reshape(T * TOP_K).astype(jnp.int32)
    P = T * TOP_K + N_EXPERTS * MOE_ROWS
    n_used = (ends[-1:] // MOE_ROWS).astype(jnp.int32)
    xb = _dispatch(xn, dest, ends.astype(jnp.int32), padded.astype(jnp.int32), n_used, P, D // 2 // LANES)
    yb = _experts(xb, (padded // MOE_ROWS).astype(jnp.int32), (offs // MOE_ROWS).astype(jnp.int32), n_used, layer,
                  w_gate, w_up, w_down)
    return _collect(h, yb, dest, wts)


def _pad_cols(w, n):
    return jnp.pad(w, ((0, 0), (0, n - w.shape[1])))


def _layer_even(h, B, S, tabs, norm_g, w_in, sinks, q_lat_norm, kv_lat_norm, w_uq, w_ukv, w_o):
    T, D = h.shape
    n_in, tn = 2560, 512
    sa = HEAD_DIM ** -0.5 * LOG2E
    prog = ([("rot128", sa)] * 8 + [("rot128", 1.0)] * 2 + [("copy", 1.0)] * 8 + [("rot64", 1.0), ("copy", 1.0)])
    y, yf = _mm([(h, 0, D)], [_pad_cols(w_in.astype(BF16), n_in)], gain=norm_g, tn=tn, out_dtype=BF16,
                col_prog=prog, tables=tabs, f32_from=12 * LANES // tn, name="ab_in")
    wq = w_uq.reshape(B_Q_LORA, B_HEADS, B_NOPE + B_ROPE)
    wq_n = wq[:, :, :B_NOPE].reshape(B_Q_LORA, B_HEADS * B_NOPE)
    wq_r = jnp.pad(wq[:, :, B_NOPE:], ((0, 0), (0, 0), (0, LANES - B_ROPE))).reshape(B_Q_LORA, B_HEADS * LANES)
    sb = (B_NOPE + B_ROPE) ** -0.5 * LOG2E
    qq = _mm([(yf, 0, B_Q_LORA)], [jnp.concatenate([wq_n, wq_r], 1).astype(BF16)], gain=q_lat_norm, tn=512,
             out_dtype=BF16, col_prog=[("copy", sb)] * 8 + [("rot64", sb)] * 8, tables=tabs, name="mla_uq")
    kv = _mm([(yf, B_Q_LORA // B_KV_LORA, B_KV_LORA)], [w_ukv.astype(BF16)], gain=kv_lat_norm, out_dtype=BF16,
             tn=512, name="mla_ukv")
    r3 = lambda a: a.reshape(B, S, a.shape[-1])
    y3 = r3(y)
    o_a = _gqa_flash_all((y3, 0), (y3, 8), (y3, 10), name="swa_attn", B=B, S=S, Hk=A_KV_HEADS,
                         G=A_HEADS // A_KV_HEADS, window=A_WINDOW, sinks=sinks, tq=256, tk=256)
    hp = 8
    head = lambda h: h
    o_b = _flash([(r3(qq), hp * LANES, head), (r3(qq), hp * LANES, lambda h: B_HEADS // hp + h)],
                 [(r3(kv), hp * 2 * LANES, head), (y3, LANES, lambda h: 18)],
                 [[(0, g * LANES), (1, g * LANES)] for g in range(hp)],
                 [[(0, 2 * g * LANES), (1, 0)] for g in range(hp)], [(0, (2 * g + 1) * LANES) for g in range(hp)],
                 name="mla_attn", B=B, S=S, n_steps=B_HEADS // hp, tq=1024, tk=1024, rc=256)
    wo = w_o.astype(BF16)
    na = A_HEADS * HEAD_DIM
    return _mm([(o_a.reshape(T, -1), 0, na), (o_b.reshape(T, -1), 0, B_HEADS * B_V)], [wo[:na], wo[na:]],
               resid=h, tm=512, tn=D, name="ab_out")


def _layer_odd(h, B, S, tabs, cmp_tabs, norm_g, w_in, forget_bias, pe_k, w1_k, w2_k, pe_v, w1_v, w2_v, w_o):
    T, D = h.shape
    hc, kvw = C_HEADS * HEAD_DIM, D_KV_HEADS * HEAD_DIM
    o_cf = 3 * hc
    o_dq = o_cf + C_HEADS
    o_kc = o_dq + D_HEADS * HEAD_DIM
    o_ks = o_kc + 2 * kvw
    o_dg = o_ks + 4 * kvw
    n_in, tn = 5760, 640
    w16 = w_in.astype(BF16)
    w_r = jnp.concatenate([w16[:, :o_cf], w16[:, o_dq:o_kc], w16[:, o_ks:o_dg], w16[:, o_kc:o_ks], w16[:, o_cf:o_dq],
                           w16[:, o_dg:], jnp.zeros((D, n_in - w_in.shape[1]), BF16)], axis=1)
    sc = HEAD_DIM ** -0.5 * LOG2E
    prog = ([("copy", sc)] * 8 + [("copy", 1.0)] * 16 + [("rot128", sc)] * 8
            + [("rot128", 1.0)] * 2 + [("copy", 1.0)] * 2 + [("rot128", 1.0)] * 2 + [("copy", 1.0)] * 7)
    y, yf = _mm([(h, 0, D)], [w_r], gain=norm_g, tn=tn, out_dtype=BF16,
                col_prog=prog, tables=tabs, f32_from=n_in // tn - 1, name="cd_in")
    misc = tn // LANES - 1
    r3 = lambda a: a.reshape(B, S, a.shape[-1])
    y3, yf3 = r3(y), r3(yf)
    fb = jnp.pad(forget_bias.astype(F32), (0, LANES - C_HEADS)).reshape(1, LANES)
    cum = _forget_cum(yf3, misc, fb)
    cumt = jnp.swapaxes(cum[:, :, :C_HEADS], 1, 2).reshape(B, C_HEADS, 1, S)
    hp = 8
    at = lambda first: (lambda h: first + h)
    o_c = _flash([(y3, hp * LANES, at(0))], [(y3, hp * LANES, at(C_HEADS // hp)), (y3, hp * LANES, at(2 * C_HEADS // hp))],
                 [[(0, g * LANES)] for g in range(hp)], [[(0, g * LANES)] for g in range(hp)],
                 [(1, g * LANES) for g in range(hp)], name="fox_attn", B=B, S=S, n_steps=C_HEADS // hp,
                 tq=1024, tk=1024, cum=cum, cumt=cumt)
    G = D_HEADS // D_KV_HEADS
    k_cmp = _compress(yf3, 0, pe_k, w1_k, w2_k, rope_tabs=cmp_tabs)
    v_cmp = _compress(yf3, kvw, pe_v, w1_v, w2_v)
    q_d = (y3, 24 // G)
    o_cmp, sel = _cmp_attn(y3, q_d[1], k_cmp, v_cmp)
    o_slc = _gqa_flash(q_d, (y3, 32), (y3, 34), name="nsa_slc_attn", B=B, S=S, Hk=D_KV_HEADS, G=G, sel=sel,
                       tq=1024, tk=1024, rc=256)
    o_win = _gqa_flash_all((y3, 24), (y3, 36), (y3, 38), name="nsa_win_attn", B=B, S=S, Hk=D_KV_HEADS, G=G,
                           window=D_WINDOW, tq=512, tk=512)
    o_d = _nsa_gate(yf, misc, C_HEADS, o_cmp.reshape(T, -1), o_slc.reshape(T, -1), o_win.reshape(T, -1))
    wo = w_o.astype(BF16)
    return _mm([(o_c.reshape(T, -1), 0, hc), (o_d, 0, D_HEADS * HEAD_DIM)], [wo[:hc], wo[hc:]],
               resid=h, tm=512, tn=D, name="cd_out")


def kernel(x, p, positions, ab_w_in, ab_sinks, ab_q_lat_norm, ab_kv_lat_norm, ab_w_uq, ab_w_ukv, ab_w_o,
           cd_w_in, cd_forget_bias, cd_cmp_pe_k, cd_cmp_w1_k, cd_cmp_w2_k, cd_cmp_pe_v, cd_cmp_w1_v,
           cd_cmp_w2_v, cd_w_o, mixer_norm, moe_norm, router_group_w, router_group_b, router_expert_w,
           router_expert_b, expert_w_gate, expert_w_up, expert_w_down, ple_proj, ple_gate_norm, ple_gate_w,
           final_norm):
    B, S, D = x.shape
    T = B * S
    depth = p.shape[0]
    tabs = _rope_tables(positions)
    nc = S // D_CMP_STRIDE
    end = np.minimum(np.arange(nc) * D_CMP_STRIDE + D_CMP_LEN - 1, S - 1)
    cmp_tabs = [t.reshape(B, nc, LANES) for t in _rope_tables(positions[:, end])[:2]]
    h = x.reshape(T, D)
    for i in range(depth):
        j = i // 2
        if i % 2 == 0:
            h = _layer_even(h, B, S, tabs, mixer_norm[i], ab_w_in[j], ab_sinks[j], ab_q_lat_norm[j],
                            ab_kv_lat_norm[j], ab_w_uq[j], ab_w_ukv[j], ab_w_o[j])
        else:
            h = _layer_odd(h, B, S, tabs, cmp_tabs, mixer_norm[i], cd_w_in[j], cd_forget_bias[j],
                           cd_cmp_pe_k[j], cd_cmp_w1_k[j], cd_cmp_w2_k[j], cd_cmp_pe_v[j], cd_cmp_w1_v[j],
                           cd_cmp_w2_v[j], cd_w_o[j])
        h = _moe(h, moe_norm[i], router_group_w[i], router_group_b[i], router_expert_w[i], router_expert_b[i],
                 i, expert_w_gate, expert_w_up, expert_w_down)
        h = _mm([(h, 0, D)], [ple_gate_w[i].astype(BF16)], gain=ple_gate_norm[i], tm=512, tn=D,
                ple=(h, p[i].reshape(T, -1), ple_proj[i].astype(BF16)),
                out_gain=final_norm if i == depth - 1 else None, name="ple")
    return h.reshape(B, S, D)
```
